```python
import jax, jax.numpy as jnp
from jax import lax
import numpy as np

D_MODEL = 1024
BATCH = 8
SEQ = 4096
DEPTH = 2

CHUNK = 64
GLA_HEADS = 4
KEY_WIDTH = D_MODEL // 2
VAL_WIDTH = D_MODEL
HEAD_K = KEY_WIDTH // GLA_HEADS
HEAD_V = VAL_WIDTH // GLA_HEADS
GATE_RANK = 16
GATE_TAU = 16.0
CONV_CH = D_MODEL
CONV_WIDTH = 3
FFN_HIDDEN = -(-8 * D_MODEL // (3 * 256)) * 256
IN_WIDTH = 2 * KEY_WIDTH + 2 * VAL_WIDTH + GATE_RANK + 3 * CONV_CH + 2 * D_MODEL
NORM_EPS = 1e-6

kernel_name = "gla_shortconv_gated_hybrid"


def _split_points():
    sizes = (KEY_WIDTH, KEY_WIDTH, VAL_WIDTH, VAL_WIDTH, GATE_RANK,
             CONV_CH, CONV_CH, CONV_CH, D_MODEL, D_MODEL)
    return tuple(int(v) for v in np.cumsum(sizes)[:-1])


def rmsnorm(x, g):
    xf = x.astype(jnp.float32)
    y = xf * lax.rsqrt(jnp.mean(xf * xf, axis=-1, keepdims=True) + NORM_EPS)
    return (y * g.astype(jnp.float32)).astype(x.dtype)


def gla_chunk_causal(q, k, v, log_a):
    b_sz, s_len = q.shape[0], q.shape[1]
    n_chunks = s_len // CHUNK

    def to_chunks(t):
        return t.astype(jnp.float32).reshape(b_sz, n_chunks, CHUNK, GLA_HEADS, t.shape[-1]).transpose(1, 0, 3, 2, 4)

    qc, kc, vc, lc = to_chunks(q), to_chunks(k), to_chunks(v), to_chunks(log_a)
    cum = jnp.cumsum(lc, axis=3)
    cum_end = cum[:, :, :, -1:, :]
    kd = kc * jnp.exp(cum_end - cum)
    gamma = jnp.exp(cum_end[:, :, :, 0, :])

    def step(state, inp):
        q_i, kd_i, v_i, g_i = inp
        state = g_i[..., None] * state + jnp.einsum('bhlk,bhlv->bhkv', kd_i, v_i)
        o_i = jnp.einsum('bhlk,bhkv->bhlv', q_i, state)
        return state, o_i

    s0 = jnp.zeros((b_sz, GLA_HEADS, HEAD_K, HEAD_V), jnp.float32)
    _, o = lax.scan(step, s0, (qc, kd, vc, gamma))
    return o.transpose(1, 0, 3, 2, 4).reshape(b_sz, s_len, GLA_HEADS, HEAD_V)


def causal_depthwise_conv(u, w, bias):
    rhs = w.astype(u.dtype)[:, None, :]
    y = lax.conv_general_dilated(u, rhs, window_strides=(1,), padding=[(CONV_WIDTH - 1, 0)],
                                 dimension_numbers=('NWC', 'WIO', 'NWC'),
                                 feature_group_count=u.shape[-1])
    return y + bias.astype(u.dtype)


def _fwd_setup_inputs(seed: int = 0) -> dict:
    key = jax.random.key(seed)
    ks = jax.random.split(key, 20)

    def nrm(k, shape, scale):
        return jax.random.normal(k, shape, jnp.float32) * scale

    def gain(k, shape):
        return 1.0 + 0.02 * jax.random.normal(k, shape, jnp.float32)

    return {
        "x": nrm(ks[0], (BATCH, SEQ, D_MODEL), 1.0),
        "norm1_g": gain(ks[1], (DEPTH, D_MODEL)),
        "w_in": nrm(ks[2], (DEPTH, D_MODEL, IN_WIDTH), D_MODEL ** -0.5),
        "w_fg2": nrm(ks[3], (DEPTH, GATE_RANK, KEY_WIDTH), GATE_RANK ** -0.5),
        "b_fg": nrm(ks[4], (DEPTH, KEY_WIDTH), 0.01),
        "gla_norm_g": gain(ks[5], (DEPTH, HEAD_V)),
        "w_oa": nrm(ks[6], (DEPTH, VAL_WIDTH, D_MODEL), VAL_WIDTH ** -0.5),
        "conv_w": nrm(ks[7], (DEPTH, CONV_WIDTH, CONV_CH), CONV_WIDTH ** -0.5),
        "conv_b": nrm(ks[8], (DEPTH, CONV_CH), 0.01),
        "w_ob": nrm(ks[9], (DEPTH, CONV_CH, D_MODEL), CONV_CH ** -0.5),
        "w_o": nrm(ks[10], (DEPTH, D_MODEL, D_MODEL), D_MODEL ** -0.5),
        "norm2_g": gain(ks[11], (DEPTH, D_MODEL)),
        "w_ffn_gate": nrm(ks[12], (DEPTH, D_MODEL, FFN_HIDDEN), D_MODEL ** -0.5),
        "w_ffn_up": nrm(ks[13], (DEPTH, D_MODEL, FFN_HIDDEN), D_MODEL ** -0.5),
        "w_ffn_down": nrm(ks[14], (DEPTH, FFN_HIDDEN, D_MODEL), FFN_HIDDEN ** -0.5),
        "final_g": gain(ks[15], (D_MODEL,)),
    }


def _fwd_reference(x, norm1_g, w_in, w_fg2, b_fg, gla_norm_g, w_oa, conv_w, conv_b, w_ob, w_o,
              norm2_g, w_ffn_gate, w_ffn_up, w_ffn_down, final_g):
    b_sz, s_len, _ = x.shape
    split_pts = _split_points()
    for l in range(DEPTH):
        h = rmsnorm(x, norm1_g[l])
        proj = jnp.einsum('bsd,de->bse', h, w_in[l])
        q, k, v, r, fz, gb_in, gc_in, cx, ga, gb = jnp.split(proj, split_pts, axis=-1)

        fg = jnp.einsum('bsr,rk->bsk', fz, w_fg2[l]) + b_fg[l]
        log_a = jax.nn.log_sigmoid(fg.astype(jnp.float32)) / GATE_TAU
        qh = q.reshape(b_sz, s_len, GLA_HEADS, HEAD_K) * (HEAD_K ** -0.5)
        kh = k.reshape(b_sz, s_len, GLA_HEADS, HEAD_K)
        vh = v.reshape(b_sz, s_len, GLA_HEADS, HEAD_V)
        ah = log_a.reshape(b_sz, s_len, GLA_HEADS, HEAD_K)
        o = gla_chunk_causal(qh, kh, vh, ah)
        o = o * lax.rsqrt(jnp.mean(o * o, axis=-1, keepdims=True) + NORM_EPS) * gla_norm_g[l].astype(jnp.float32)
        o = o.reshape(b_sz, s_len, VAL_WIDTH).astype(x.dtype) * jax.nn.silu(r)
        y_a = jnp.einsum('bsv,vd->bsd', o, w_oa[l])

        conv = causal_depthwise_conv(gc_in * cx, conv_w[l], conv_b[l])
        y_b = jnp.einsum('bsc,cd->bsd', gb_in * conv, w_ob[l])

        mix = jax.nn.sigmoid(ga) * y_a + jax.nn.sigmoid(gb) * y_b
        x = x + jnp.einsum('bsd,de->bse', mix, w_o[l])

        h2 = rmsnorm(x, norm2_g[l])
        hid = jax.nn.silu(jnp.einsum('bsd,df->bsf', h2, w_ffn_gate[l])) * jnp.einsum('bsd,df->bsf', h2, w_ffn_up[l])
        x = x + jnp.einsum('bsf,fd->bsd', hid, w_ffn_down[l])
    return rmsnorm(x, final_g)


import jax as _jax
import jax.numpy as _jnp

TWIN_FORMAT = 'train_step'
FWD_PARAMS = ['x', 'norm1_g', 'w_in', 'w_fg2', 'b_fg', 'gla_norm_g', 'w_oa', 'conv_w', 'conv_b', 'w_ob', 'w_o', 'norm2_g', 'w_ffn_gate', 'w_ffn_up', 'w_ffn_down', 'final_g']
TWIN_WEIGHTS = ['norm1_g', 'w_in', 'w_fg2', 'b_fg', 'gla_norm_g', 'w_oa', 'conv_w', 'conv_b', 'w_ob', 'w_o', 'norm2_g', 'w_ffn_gate', 'w_ffn_up', 'w_ffn_down', 'final_g']
TWIN_DIFF_INPUT = 'x'
TWIN_INPUTS = ['x', 'norm1_g', 'w_in', 'w_fg2', 'b_fg', 'gla_norm_g', 'w_oa', 'conv_w', 'conv_b', 'w_ob', 'w_o', 'norm2_g', 'w_ffn_gate', 'w_ffn_up', 'w_ffn_down', 'final_g', 'loss_target', 'm_norm1_g', 'm_w_in', 'm_w_fg2', 'm_b_fg', 'm_gla_norm_g', 'm_w_oa', 'm_conv_w', 'm_conv_b', 'm_w_ob', 'm_w_o', 'm_norm2_g', 'm_w_ffn_gate', 'm_w_ffn_up', 'm_w_ffn_down', 'm_final_g', 'v_norm1_g', 'v_w_in', 'v_w_fg2', 'v_b_fg', 'v_gla_norm_g', 'v_w_oa', 'v_conv_w', 'v_conv_b', 'v_w_ob', 'v_w_o', 'v_norm2_g', 'v_w_ffn_gate', 'v_w_ffn_up', 'v_w_ffn_down', 'v_final_g']
TWIN_OUTPUTS = ['loss', 'grad_x', 'grad_norm1_g', 'grad_w_in', 'grad_w_fg2', 'grad_b_fg', 'grad_gla_norm_g', 'grad_w_oa', 'grad_conv_w', 'grad_conv_b', 'grad_w_ob', 'grad_w_o', 'grad_norm2_g', 'grad_w_ffn_gate', 'grad_w_ffn_up', 'grad_w_ffn_down', 'grad_final_g', 'delta_norm1_g', 'delta_w_in', 'delta_w_fg2', 'delta_b_fg', 'delta_gla_norm_g', 'delta_w_oa', 'delta_conv_w', 'delta_conv_b', 'delta_w_ob', 'delta_w_o', 'delta_norm2_g', 'delta_w_ffn_gate', 'delta_w_ffn_up', 'delta_w_ffn_down', 'delta_final_g', 'new_m_norm1_g', 'new_m_w_in', 'new_m_w_fg2', 'new_m_b_fg', 'new_m_gla_norm_g', 'new_m_w_oa', 'new_m_conv_w', 'new_m_conv_b', 'new_m_w_ob', 'new_m_w_o', 'new_m_norm2_g', 'new_m_w_ffn_gate', 'new_m_w_ffn_up', 'new_m_w_ffn_down', 'new_m_final_g', 'new_v_norm1_g', 'new_v_w_in', 'new_v_w_fg2', 'new_v_b_fg', 'new_v_gla_norm_g', 'new_v_w_oa', 'new_v_conv_w', 'new_v_conv_b', 'new_v_w_ob', 'new_v_w_o', 'new_v_norm2_g', 'new_v_w_ffn_gate', 'new_v_w_ffn_up', 'new_v_w_ffn_down', 'new_v_final_g']
TWIN_LEAF_KINDS = {'loss': 'loss', 'grad_x': 'grad_x', 'grad_norm1_g': 'grad_w', 'grad_w_in': 'grad_w', 'grad_w_fg2': 'grad_w', 'grad_b_fg': 'grad_w', 'grad_gla_norm_g': 'grad_w', 'grad_w_oa': 'grad_w', 'grad_conv_w': 'grad_w', 'grad_conv_b': 'grad_w', 'grad_w_ob': 'grad_w', 'grad_w_o': 'grad_w', 'grad_norm2_g': 'grad_w', 'grad_w_ffn_gate': 'grad_w', 'grad_w_ffn_up': 'grad_w', 'grad_w_ffn_down': 'grad_w', 'grad_final_g': 'grad_w', 'delta_norm1_g': 'delta_w', 'delta_w_in': 'delta_w', 'delta_w_fg2': 'delta_w', 'delta_b_fg': 'delta_w', 'delta_gla_norm_g': 'delta_w', 'delta_w_oa': 'delta_w', 'delta_conv_w': 'delta_w', 'delta_conv_b': 'delta_w', 'delta_w_ob': 'delta_w', 'delta_w_o': 'delta_w', 'delta_norm2_g': 'delta_w', 'delta_w_ffn_gate': 'delta_w', 'delta_w_ffn_up': 'delta_w', 'delta_w_ffn_down': 'delta_w', 'delta_final_g': 'delta_w', 'new_m_norm1_g': 'new_m', 'new_m_w_in': 'new_m', 'new_m_w_fg2': 'new_m', 'new_m_b_fg': 'new_m', 'new_m_gla_norm_g': 'new_m', 'new_m_w_oa': 'new_m', 'new_m_conv_w': 'new_m', 'new_m_conv_b': 'new_m', 'new_m_w_ob': 'new_m', 'new_m_w_o': 'new_m', 'new_m_norm2_g': 'new_m', 'new_m_w_ffn_gate': 'new_m', 'new_m_w_ffn_up': 'new_m', 'new_m_w_ffn_down': 'new_m', 'new_m_final_g': 'new_m', 'new_v_norm1_g': 'new_v', 'new_v_w_in': 'new_v', 'new_v_w_fg2': 'new_v', 'new_v_b_fg': 'new_v', 'new_v_gla_norm_g': 'new_v', 'new_v_w_oa': 'new_v', 'new_v_conv_w': 'new_v', 'new_v_conv_b': 'new_v', 'new_v_w_ob': 'new_v', 'new_v_w_o': 'new_v', 'new_v_norm2_g': 'new_v', 'new_v_w_ffn_gate': 'new_v', 'new_v_w_ffn_up': 'new_v', 'new_v_w_ffn_down': 'new_v', 'new_v_final_g': 'new_v'}


def _forward(args):
    return _fwd_reference(*[args[k] for k in FWD_PARAMS])


def _output_shape():
    out = _jax.eval_shape(lambda: _forward(_fwd_setup_inputs(0)))
    return out.shape, out.dtype

N_MICROBATCH = 1
ADAM_LR = 0.001
ADAM_B1 = 0.9
ADAM_B2 = 0.999
ADAM_EPS = 1e-08
ADAM_WD = 0.01
ADAM_STEP = 10
PER_EXAMPLE_BATCH_AXIS = {'x': 0, 'loss_target': 0}
SHARED_INPUTS = []
_WEIGHT_DTYPES = {'norm1_g': _jnp.float32, 'w_in': _jnp.float32, 'w_fg2': _jnp.float32, 'b_fg': _jnp.float32, 'gla_norm_g': _jnp.float32, 'w_oa': _jnp.float32, 'conv_w': _jnp.float32, 'conv_b': _jnp.float32, 'w_ob': _jnp.float32, 'w_o': _jnp.float32, 'norm2_g': _jnp.float32, 'w_ffn_gate': _jnp.float32, 'w_ffn_up': _jnp.float32, 'w_ffn_down': _jnp.float32, 'final_g': _jnp.float32}
MOMENT_SCALE = {'norm1_g': 2.082461e-01, 'w_in': 7.297991e-02, 'w_fg2': 9.592166e-03, 'b_fg': 3.699460e-02, 'gla_norm_g': 1.150699e-01, 'w_oa': 5.714254e-02, 'conv_w': 9.688971e-02, 'conv_b': 9.774005e-02, 'w_ob': 9.508336e-02, 'w_o': 1.110001e-01, 'norm2_g': 1.257308e-01, 'w_ffn_gate': 5.414401e-02, 'w_ffn_up': 5.234788e-02, 'w_ffn_down': 8.686332e-02, 'final_g': 3.199177e+01}


def _to_microbatches(a, axis):
    t = _jnp.moveaxis(a, axis, 0)
    t = t.reshape((N_MICROBATCH, t.shape[0] // N_MICROBATCH) + t.shape[1:])
    return _jnp.moveaxis(t, 1, axis + 1)


def setup_inputs(seed: int = 0) -> dict:
    inp = _fwd_setup_inputs(seed)
    key = _jax.random.fold_in(_jax.random.key(seed), 7919)
    shape, _ = _output_shape()
    out = dict(inp)
    out["loss_target"] = _jax.random.normal(_jax.random.fold_in(key, 0), shape, _jnp.float32)
    for i, name in enumerate(TWIN_WEIGHTS):
        w = inp[name].astype(_jnp.float32)
        if MOMENT_SCALE is None:
            s = _jnp.sqrt(_jnp.mean(_jnp.square(w)) + 1e-30)
        else:
            s = MOMENT_SCALE[name]
        km, kv = _jax.random.split(_jax.random.fold_in(key, i + 1))
        out[name] = w
        out["m_" + name] = s * _jax.random.normal(km, w.shape, _jnp.float32)
        out["v_" + name] = (s * s) * _jax.random.uniform(kv, w.shape, _jnp.float32, 0.5, 1.5)
    if N_MICROBATCH > 1:
        for name, axis in PER_EXAMPLE_BATCH_AXIS.items():
            out[name] = _to_microbatches(out[name], axis)
    return {'x': out['x'], 'norm1_g': out['norm1_g'], 'w_in': out['w_in'], 'w_fg2': out['w_fg2'], 'b_fg': out['b_fg'], 'gla_norm_g': out['gla_norm_g'], 'w_oa': out['w_oa'], 'conv_w': out['conv_w'], 'conv_b': out['conv_b'], 'w_ob': out['w_ob'], 'w_o': out['w_o'], 'norm2_g': out['norm2_g'], 'w_ffn_gate': out['w_ffn_gate'], 'w_ffn_up': out['w_ffn_up'], 'w_ffn_down': out['w_ffn_down'], 'final_g': out['final_g'], 'loss_target': out['loss_target'], 'm_norm1_g': out['m_norm1_g'], 'm_w_in': out['m_w_in'], 'm_w_fg2': out['m_w_fg2'], 'm_b_fg': out['m_b_fg'], 'm_gla_norm_g': out['m_gla_norm_g'], 'm_w_oa': out['m_w_oa'], 'm_conv_w': out['m_conv_w'], 'm_conv_b': out['m_conv_b'], 'm_w_ob': out['m_w_ob'], 'm_w_o': out['m_w_o'], 'm_norm2_g': out['m_norm2_g'], 'm_w_ffn_gate': out['m_w_ffn_gate'], 'm_w_ffn_up': out['m_w_ffn_up'], 'm_w_ffn_down': out['m_w_ffn_down'], 'm_final_g': out['m_final_g'], 'v_norm1_g': out['v_norm1_g'], 'v_w_in': out['v_w_in'], 'v_w_fg2': out['v_w_fg2'], 'v_b_fg': out['v_b_fg'], 'v_gla_norm_g': out['v_gla_norm_g'], 'v_w_oa': out['v_w_oa'], 'v_conv_w': out['v_conv_w'], 'v_conv_b': out['v_conv_b'], 'v_w_ob': out['v_w_ob'], 'v_w_o': out['v_w_o'], 'v_norm2_g': out['v_norm2_g'], 'v_w_ffn_gate': out['v_w_ffn_gate'], 'v_w_ffn_up': out['v_w_ffn_up'], 'v_w_ffn_down': out['v_w_ffn_down'], 'v_final_g': out['v_final_g']}


def _loss(weights, diff, rest, loss_target):
    with _jax.named_scope("forward"):
        args = {**rest, TWIN_DIFF_INPUT: diff, **{k: w.astype(_WEIGHT_DTYPES[k]) for k, w in weights.items()}}
        y = _forward(args)
    with _jax.named_scope("loss_head"):
        err = _jnp.square(y.astype(_jnp.float32) - loss_target)
        return 0.5 * _jnp.sum(_jnp.mean(err, axis=-1)) if err.ndim else 0.5 * err


def _adamw(w, g, m, v):
    m = ADAM_B1 * m + (1.0 - ADAM_B1) * g
    v = ADAM_B2 * v + (1.0 - ADAM_B2) * _jnp.square(g)
    m_hat = m / (1.0 - ADAM_B1 ** ADAM_STEP)
    v_hat = v / (1.0 - ADAM_B2 ** ADAM_STEP)
    delta = -ADAM_LR * (m_hat / (_jnp.sqrt(v_hat) + ADAM_EPS) + ADAM_WD * w)
    return delta, m, v


def reference(x, norm1_g, w_in, w_fg2, b_fg, gla_norm_g, w_oa, conv_w, conv_b, w_ob, w_o, norm2_g, w_ffn_gate, w_ffn_up, w_ffn_down, final_g, loss_target, m_norm1_g, m_w_in, m_w_fg2, m_b_fg, m_gla_norm_g, m_w_oa, m_conv_w, m_conv_b, m_w_ob, m_w_o, m_norm2_g, m_w_ffn_gate, m_w_ffn_up, m_w_ffn_down, m_final_g, v_norm1_g, v_w_in, v_w_fg2, v_b_fg, v_gla_norm_g, v_w_oa, v_conv_w, v_conv_b, v_w_ob, v_w_o, v_norm2_g, v_w_ffn_gate, v_w_ffn_up, v_w_ffn_down, v_final_g):
    given = dict(x=x, norm1_g=norm1_g, w_in=w_in, w_fg2=w_fg2, b_fg=b_fg, gla_norm_g=gla_norm_g, w_oa=w_oa, conv_w=conv_w, conv_b=conv_b, w_ob=w_ob, w_o=w_o, norm2_g=norm2_g, w_ffn_gate=w_ffn_gate, w_ffn_up=w_ffn_up, w_ffn_down=w_ffn_down, final_g=final_g, loss_target=loss_target, m_norm1_g=m_norm1_g, m_w_in=m_w_in, m_w_fg2=m_w_fg2, m_b_fg=m_b_fg, m_gla_norm_g=m_gla_norm_g, m_w_oa=m_w_oa, m_conv_w=m_conv_w, m_conv_b=m_conv_b, m_w_ob=m_w_ob, m_w_o=m_w_o, m_norm2_g=m_norm2_g, m_w_ffn_gate=m_w_ffn_gate, m_w_ffn_up=m_w_ffn_up, m_w_ffn_down=m_w_ffn_down, m_final_g=m_final_g, v_norm1_g=v_norm1_g, v_w_in=v_w_in, v_w_fg2=v_w_fg2, v_b_fg=v_b_fg, v_gla_norm_g=v_gla_norm_g, v_w_oa=v_w_oa, v_conv_w=v_conv_w, v_conv_b=v_conv_b, v_w_ob=v_w_ob, v_w_o=v_w_o, v_norm2_g=v_norm2_g, v_w_ffn_gate=v_w_ffn_gate, v_w_ffn_up=v_w_ffn_up, v_w_ffn_down=v_w_ffn_down, v_final_g=v_final_g)
    weights = {n: given[n] for n in TWIN_WEIGHTS}
    shared = {n: given[n] for n in SHARED_INPUTS}
    per_example = {n: given[n] for n in ['x']}
    grad_fn = _jax.value_and_grad(_loss, argnums=(0, 1))

    def one_microbatch(ex, loss_target):
        ex = dict(ex)
        diff = ex.pop(TWIN_DIFF_INPUT)
        return grad_fn(weights, diff, {**shared, **ex}, loss_target)

    if N_MICROBATCH == 1:
        loss, (grad_w, grad_x) = one_microbatch(per_example, given["loss_target"])
    else:
        def body(carry, xs):
            loss_sum, grad_sum = carry
            l_k, (gw_k, gx_k) = one_microbatch(xs[0], xs[1])
            with _jax.named_scope("update"):
                return (loss_sum + l_k, _jax.tree.map(_jnp.add, grad_sum, gw_k)), gx_k

        init = (_jnp.zeros((), _jnp.float32), _jax.tree.map(_jnp.zeros_like, weights))
        (loss, grad_w), grad_x = _jax.lax.scan(body, init, (per_example, given["loss_target"]))
    with _jax.named_scope("update"):
        delta_w, new_m, new_v = {}, {}, {}
        for n in TWIN_WEIGHTS:
            delta_w[n], new_m[n], new_v[n] = _adamw(weights[n], grad_w[n], given["m_" + n], given["v_" + n])
    return (loss, grad_x, *[grad_w[n] for n in TWIN_WEIGHTS], *[delta_w[n] for n in TWIN_WEIGHTS],
            *[new_m[n] for n in TWIN_WEIGHTS], *[new_v[n] for n in TWIN_WEIGHTS])
```

```python
import functools

import jax
import jax.numpy as jnp
import numpy as np
from jax import lax
from jax.experimental import pallas as pl
from jax.experimental.pallas import tpu as pltpu

F32 = jnp.float32
BF16 = jnp.bfloat16
MESH = pl.DeviceIdType.MESH

D_MODEL = 1024
DEPTH = 2
CHUNK = 64
HEADS = 4
HEAD_K = 128
HEAD_V = 256
KEY_W = HEADS * HEAD_K
VAL_W = HEADS * HEAD_V
RANK = 16
TAU = 16.0
FFN = 2816
IN_WIDTH = 2 * KEY_W + 2 * VAL_W + RANK + 5 * D_MODEL
EPS = 1e-6
Q_SCALE = HEAD_K ** -0.5
N_DEV = 8
ADAM_LR, ADAM_B1, ADAM_B2, ADAM_EPS, ADAM_WD, ADAM_STEP = 0.001, 0.9, 0.999, 1e-08, 0.01, 10

LANES = 128
SUBLANES_BF16 = 16
VMEM_LIMIT = 48 * 1024 * 1024

FZ_COL = 2 * KEY_W + 2 * VAL_W + 5 * D_MODEL
PROJ_W = FZ_COL + LANES
SEG_R, SEG_GBI, SEG_GCI, SEG_CX, SEG_GA, SEG_GB = 2, 3, 4, 5, 6, 7

W_IN_SHARD = IN_WIDTH // N_DEV
FFN_SHARD = FFN // N_DEV
ROW_SHARD = D_MODEL // N_DEV
OFF_FG2 = W_IN_SHARD
OFF_CONV = W_IN_SHARD + 1
OFF_OA = 1040
OFF_OB = OFF_OA + ROW_SHARD
OFF_O = OFF_OB + ROW_SHARD
OFF_GATE = OFF_O + ROW_SHARD
OFF_UP = OFF_GATE + FFN_SHARD
OFF_DOWN = OFF_UP + FFN_SHARD
ROWS_LAYER = OFF_DOWN + FFN_SHARD
ROWS_PACKED = DEPTH * ROWS_LAYER
SMALL_ROWS = 16


def _pick(n, candidates):
    for c in candidates:
        if n % c == 0:
            return c
    return n


def _cparams(sem):
    return pltpu.CompilerParams(dimension_semantics=sem, vmem_limit_bytes=VMEM_LIMIT)


def _sigmoid(x):
    return 1.0 / (1.0 + jnp.exp(-x))


def _matmul(a, b, dims, out_dtype, name, residual=None):
    if dims == "nn":
        (m, k), (k2, n) = a.shape, b.shape
    elif dims == "nt":
        (m, k), (n, k2) = a.shape, b.shape
    else:
        (k, m), (k2, n) = a.shape, b.shape
    assert k == k2, (a.shape, b.shape, dims)
    tm = _pick(m, (512, 256, 128))
    tn = _pick(n, (1664, 1408, 1024, 512, 256, 128))
    tk = _pick(k, (1664, 1408, 1024, 512, 256, 128))
    nk = k // tk
    if dims == "nn":
        a_spec = pl.BlockSpec((tm, tk), lambda i, j, kk: (i, kk))
        b_spec = pl.BlockSpec((tk, tn), lambda i, j, kk: (kk, j))
        contract = (((1,), (0,)), ((), ()))
    elif dims == "nt":
        a_spec = pl.BlockSpec((tm, tk), lambda i, j, kk: (i, kk))
        b_spec = pl.BlockSpec((tn, tk), lambda i, j, kk: (j, kk))
        contract = (((1,), (1,)), ((), ()))
    else:
        a_spec = pl.BlockSpec((tk, tm), lambda i, j, kk: (kk, i))
        b_spec = pl.BlockSpec((tk, tn), lambda i, j, kk: (kk, j))
        contract = (((0,), (0,)), ((), ()))
    o_spec = pl.BlockSpec((tm, tn), lambda i, j, kk: (i, j))
    has_res = residual is not None

    def body(*refs):
        if has_res:
            a_ref, b_ref, r_ref, o_ref, acc_ref = refs
        else:
            a_ref, b_ref, o_ref, acc_ref = refs
            r_ref = None
        kk = pl.program_id(2)
        part = lax.dot_general(a_ref[...], b_ref[...], contract, preferred_element_type=F32)

        @pl.when(kk == 0)
        def _():
            acc_ref[...] = part

        @pl.when(kk > 0)
        def _():
            acc_ref[...] += part

        @pl.when(kk == nk - 1)
        def _():
            res = acc_ref[...]
            if has_res:
                res = res + r_ref[...]
            o_ref[...] = res.astype(o_ref.dtype)

    in_specs = [a_spec, b_spec] + ([o_spec] if has_res else [])
    args = (a, b) + ((residual,) if has_res else ())
    return pl.pallas_call(
        body, grid=(m // tm, n // tn, nk), in_specs=in_specs, out_specs=o_spec,
        out_shape=jax.ShapeDtypeStruct((m, n), out_dtype),
        scratch_shapes=[pltpu.VMEM((tm, tn), F32)], name=name,
        compiler_params=_cparams(("parallel", "parallel", "arbitrary")),
    )(*args)


def _rmsnorm_fwd(x, g, name):
    s, d = x.shape
    tm = _pick(s, (512, 256))

    def body(x_ref, g_ref, o_ref):
        xv = x_ref[...]
        r = lax.rsqrt(jnp.mean(xv * xv, axis=-1, keepdims=True) + EPS)
        o_ref[...] = (xv * r * g_ref[...]).astype(o_ref.dtype)

    row = pl.BlockSpec((tm, d), lambda i: (i, 0))
    return pl.pallas_call(
        body, grid=(s // tm,), in_specs=[row, pl.BlockSpec((1, d), lambda i: (0, 0))], out_specs=row,
        out_shape=jax.ShapeDtypeStruct((s, d), BF16), name=name, compiler_params=_cparams(("parallel",)),
    )(x, g)


def _rmsnorm_bwd(x, g, dh, dres, name):
    s, d = x.shape
    tm = _pick(s, (512, 256))

    def body(x_ref, g_ref, dh_ref, dres_ref, dx_ref, dg_ref):
        xv = x_ref[...]
        r = lax.rsqrt(jnp.mean(xv * xv, axis=-1, keepdims=True) + EPS)
        xn = xv * r
        dhv = dh_ref[...].astype(F32)
        dxn = dhv * g_ref[...]
        dx_ref[...] = dres_ref[...] + r * (dxn - xn * jnp.mean(dxn * xn, axis=-1, keepdims=True))

        @pl.when(pl.program_id(0) == 0)
        def _():
            dg_ref[...] = jnp.zeros_like(dg_ref)

        dg_ref[...] += jnp.broadcast_to(jnp.sum(dhv * xn, axis=0, keepdims=True), dg_ref.shape)

    row = pl.BlockSpec((tm, d), lambda i: (i, 0))
    acc = pl.BlockSpec((8, d), lambda i: (0, 0))
    return pl.pallas_call(
        body, grid=(s // tm,), in_specs=[row, pl.BlockSpec((1, d), lambda i: (0, 0)), row, row],
        out_specs=[row, acc],
        out_shape=[jax.ShapeDtypeStruct((s, d), F32), jax.ShapeDtypeStruct((8, d), F32)], name=name,
        compiler_params=_cparams(("arbitrary",)),
    )(x, g, dh, dres)


def _loss_head(x, g, target, name):
    s, d = x.shape
    tm = _pick(s, (512, 256))

    def body(x_ref, g_ref, t_ref, loss_ref, dx_ref, dg_ref):
        xv = x_ref[...]
        gv = g_ref[...]
        r = lax.rsqrt(jnp.mean(xv * xv, axis=-1, keepdims=True) + EPS)
        xn = xv * r
        err = xn * gv - t_ref[...]
        dy = err * (1.0 / d)
        dxn = dy * gv
        dx_ref[...] = r * (dxn - xn * jnp.mean(dxn * xn, axis=-1, keepdims=True))

        @pl.when(pl.program_id(0) == 0)
        def _():
            dg_ref[...] = jnp.zeros_like(dg_ref)
            loss_ref[...] = jnp.zeros_like(loss_ref)

        dg_ref[...] += jnp.broadcast_to(jnp.sum(dy * xn, axis=0, keepdims=True), dg_ref.shape)
        row_loss = jnp.sum(err * err, axis=-1, keepdims=True)
        loss_ref[...] += jnp.broadcast_to((0.5 / d) * jnp.sum(row_loss, axis=0, keepdims=True), loss_ref.shape)

    row = pl.BlockSpec((tm, d), lambda i: (i, 0))
    return pl.pallas_call(
        body, grid=(s // tm,), in_specs=[row, pl.BlockSpec((1, d), lambda i: (0, 0)), row],
        out_specs=[pl.BlockSpec((8, LANES), lambda i: (0, 0)), row, pl.BlockSpec((8, d), lambda i: (0, 0))],
        out_shape=[jax.ShapeDtypeStruct((8, LANES), F32), jax.ShapeDtypeStruct((s, d), F32),
                   jax.ShapeDtypeStruct((8, d), F32)],
        name=name, compiler_params=_cparams(("arbitrary",)),
    )(x, g, target)


def _tri_dot(tri, x):
    hi = x.astype(BF16)
    lo = (x - hi.astype(F32)).astype(BF16)
    return jnp.dot(tri, hi, preferred_element_type=F32) + jnp.dot(tri, lo, preferred_element_type=F32)


def _tri(strict):
    r = lax.broadcasted_iota(jnp.int32, (CHUNK, CHUNK), 0)
    c = lax.broadcasted_iota(jnp.int32, (CHUNK, CHUNK), 1)
    return jnp.where((r > c) if strict else (r >= c), 1.0, 0.0).astype(BF16)


def _chunk_decay(fz, w, b, tri):
    fg = jnp.dot(fz, w, preferred_element_type=F32) + b
    la = (jnp.minimum(fg, 0.0) - jnp.log(1.0 + jnp.exp(-jnp.abs(fg)))) * (1.0 / TAU)
    cum = _tri_dot(tri, la)
    cum_end = cum[CHUNK - 1:CHUNK, :]
    return fg, jnp.exp(cum_end - cum), jnp.exp(cum_end)


_TN = (((0,), (0,)), ((), ()))
_NT = (((1,), (1,)), ((), ()))


def _gla_specs(rows):
    q_spec = pl.BlockSpec((rows, HEAD_K), lambda h, c: (c, h))
    k_spec = pl.BlockSpec((rows, HEAD_K), lambda h, c: (c, HEADS + h))
    v_spec = pl.BlockSpec((rows, HEAD_V), lambda h, c: (c, HEADS + h))
    fz_spec = pl.BlockSpec((rows, LANES), lambda h, c: (c, FZ_COL // LANES))
    w_spec = pl.BlockSpec((LANES, HEAD_K), lambda h, c: (0, h))
    b_spec = pl.BlockSpec((1, HEAD_K), lambda h, c: (0, h))
    return q_spec, k_spec, v_spec, fz_spec, w_spec, b_spec


def _gla_fwd(proj, wfg, bfg, name):
    s = proj.shape[0]
    nc = s // CHUNK
    per = _pick(nc, (8, 4, 2, 1))
    rows = per * CHUNK

    def body(q_ref, k_ref, v_ref, fz_ref, w_ref, b_ref, o_ref, st_ref, state):
        @pl.when(pl.program_id(1) == 0)
        def _():
            state[...] = jnp.zeros_like(state)

        tri = _tri(False)
        w = w_ref[...]
        b = b_ref[...]
        for i in range(per):
            sl = pl.ds(i * CHUNK, CHUNK)
            _, dec, gamma = _chunk_decay(fz_ref[sl, :], w, b, tri)
            kd = (k_ref[sl, :].astype(F32) * dec).astype(BF16)
            st = state[...] * gamma + lax.dot_general(v_ref[sl, :], kd, _TN, preferred_element_type=F32)
            state[...] = st
            st16 = st.astype(BF16)
            st_ref[0, i] = st16
            qs = (q_ref[sl, :].astype(F32) * Q_SCALE).astype(BF16)
            o_ref[sl, :] = lax.dot_general(qs, st16, _NT, preferred_element_type=F32).astype(o_ref.dtype)

    q_spec, k_spec, v_spec, fz_spec, w_spec, b_spec = _gla_specs(rows)
    return pl.pallas_call(
        body, grid=(HEADS, nc // per),
        in_specs=[q_spec, k_spec, v_spec, fz_spec, w_spec, b_spec],
        out_specs=[pl.BlockSpec((rows, HEAD_V), lambda h, c: (c, h)),
                   pl.BlockSpec((1, per, HEAD_V, HEAD_K), lambda h, c: (h, c, 0, 0))],
        out_shape=[jax.ShapeDtypeStruct((s, VAL_W), BF16),
                   jax.ShapeDtypeStruct((HEADS, nc, HEAD_V, HEAD_K), BF16)],
        scratch_shapes=[pltpu.VMEM((HEAD_V, HEAD_K), F32)], name=name,
        compiler_params=_cparams(("parallel", "arbitrary")),
    )(proj, proj, proj, proj, wfg, bfg)


def _gla_bwd(proj, wfg, bfg, states, do, name):
    s = proj.shape[0]
    nc = s // CHUNK
    per = _pick(nc, (8, 4, 2, 1))
    rows = per * CHUNK
    nblk = nc // per

    def rev(spec_fn):
        return lambda h, j: spec_fn(h, nblk - 1 - j)

    def body(q_ref, k_ref, v_ref, fz_ref, w_ref, b_ref, do_ref, st_ref, prev_ref,
             dq_ref, dk_ref, dv_ref, dfg_ref, db_ref, carry):
        j = pl.program_id(1)

        @pl.when(j == 0)
        def _():
            carry[...] = jnp.zeros_like(carry)
            db_ref[...] = jnp.zeros_like(db_ref)

        tri = _tri(False)
        tri_strict = _tri(True)
        w = w_ref[...]
        b = b_ref[...]
        has_prev = (j < nblk - 1).astype(F32)
        db = jnp.zeros((1, HEAD_K), F32)
        for i in reversed(range(per)):
            sl = pl.ds(i * CHUNK, CHUNK)
            fg, dec, gamma = _chunk_decay(fz_ref[sl, :], w, b, tri)
            kd = k_ref[sl, :].astype(F32) * dec
            qs = (q_ref[sl, :].astype(F32) * Q_SCALE).astype(BF16)
            dov = do_ref[sl, :]
            v = v_ref[sl, :]
            gt = carry[...] + lax.dot_general(dov, qs, _TN, preferred_element_type=F32)
            gt16 = gt.astype(BF16)
            dq_ref[sl, :] = (jnp.dot(dov, st_ref[0, i], preferred_element_type=F32) * Q_SCALE).astype(dq_ref.dtype)
            dkd = jnp.dot(v, gt16, preferred_element_type=F32)
            dv_ref[sl, :] = lax.dot_general(kd.astype(BF16), gt16, _NT,
                                            preferred_element_type=F32).astype(dv_ref.dtype)
            if i > 0:
                st_prev = st_ref[0, i - 1].astype(F32)
            else:
                st_prev = prev_ref[0, 0].astype(F32) * has_prev
            dgamma = jnp.sum(gt * st_prev, axis=0, keepdims=True)
            carry[...] = gt * gamma
            dk_ref[sl, :] = (dkd * dec).astype(dk_ref.dtype)
            dla = dgamma * gamma + _tri_dot(tri_strict, dkd * kd)
            dfg = dla * (1.0 / TAU) * _sigmoid(-fg)
            dfg_ref[sl, :] = dfg.astype(dfg_ref.dtype)
            db = db + jnp.sum(dfg, axis=0, keepdims=True)
        db_ref[...] += jnp.broadcast_to(db, db_ref.shape)

    q_spec, k_spec, v_spec, fz_spec, w_spec, b_spec = _gla_specs(rows)
    q_spec, k_spec, v_spec, fz_spec = [
        pl.BlockSpec(sp.block_shape, rev(sp.index_map)) for sp in (q_spec, k_spec, v_spec, fz_spec)]
    do_spec = pl.BlockSpec((rows, HEAD_V), lambda h, j: (nblk - 1 - j, h))
    st_spec = pl.BlockSpec((1, per, HEAD_V, HEAD_K), lambda h, j: (h, nblk - 1 - j, 0, 0))
    prev_spec = pl.BlockSpec((1, 1, HEAD_V, HEAD_K),
                             lambda h, j: (h, jnp.maximum((nblk - 1 - j) * per - 1, 0), 0, 0))
    key_out = pl.BlockSpec((rows, HEAD_K), lambda h, j: (nblk - 1 - j, h))
    return pl.pallas_call(
        body, grid=(HEADS, nblk),
        in_specs=[q_spec, k_spec, v_spec, fz_spec, w_spec, b_spec, do_spec, st_spec, prev_spec],
        out_specs=[key_out, key_out, do_spec, key_out, pl.BlockSpec((8, HEAD_K), lambda h, j: (0, h))],
        out_shape=[jax.ShapeDtypeStruct((s, KEY_W), BF16), jax.ShapeDtypeStruct((s, KEY_W), BF16),
                   jax.ShapeDtypeStruct((s, VAL_W), BF16), jax.ShapeDtypeStruct((s, KEY_W), BF16),
                   jax.ShapeDtypeStruct((8, KEY_W), F32)],
        scratch_shapes=[pltpu.VMEM((HEAD_V, HEAD_K), F32)], name=name,
        compiler_params=_cparams(("parallel", "arbitrary")),
    )(proj, proj, proj, proj, wfg, bfg, do, states, states)


def _seg(tm, seg):
    return pl.BlockSpec((tm, D_MODEL), lambda i: (i, seg))


def _gla_post_fwd(o, proj, g, name):
    s = o.shape[0]
    tm = _pick(s, (512, 256))

    def body(o_ref, r_ref, g_ref, oa_ref):
        gv = g_ref[...]
        for h in range(HEADS):
            sl = slice(h * HEAD_V, (h + 1) * HEAD_V)
            ov = o_ref[:, sl].astype(F32)
            rstd = lax.rsqrt(jnp.mean(ov * ov, axis=-1, keepdims=True) + EPS)
            rv = r_ref[:, sl].astype(F32)
            oa_ref[:, sl] = (ov * rstd * gv * (rv * _sigmoid(rv))).astype(oa_ref.dtype)

    row = pl.BlockSpec((tm, VAL_W), lambda i: (i, 0))
    return pl.pallas_call(
        body, grid=(s // tm,), in_specs=[row, _seg(tm, SEG_R), pl.BlockSpec((1, HEAD_V), lambda i: (0, 0))],
        out_specs=row, out_shape=jax.ShapeDtypeStruct((s, VAL_W), BF16), name=name,
        compiler_params=_cparams(("parallel",)),
    )(o, proj, g)


def _gla_post_bwd(doa, o, proj, g, name):
    s = o.shape[0]
    tm = _pick(s, (512, 256))

    def body(doa_ref, o_ref, r_ref, g_ref, dr_ref, do_ref, dg_ref):
        @pl.when(pl.program_id(0) == 0)
        def _():
            dg_ref[...] = jnp.zeros_like(dg_ref)

        gv = g_ref[...]
        dg = jnp.zeros((1, HEAD_V), F32)
        for h in range(HEADS):
            sl = slice(h * HEAD_V, (h + 1) * HEAD_V)
            ov = o_ref[:, sl].astype(F32)
            rstd = lax.rsqrt(jnp.mean(ov * ov, axis=-1, keepdims=True) + EPS)
            ohat = ov * rstd
            rv = r_ref[:, sl].astype(F32)
            sg = _sigmoid(rv)
            dv = doa_ref[:, sl].astype(F32)
            dr_ref[:, sl] = (dv * ohat * gv * (sg * (1.0 + rv * (1.0 - sg)))).astype(dr_ref.dtype)
            don = dv * (rv * sg)
            dg = dg + jnp.sum(don * ohat, axis=0, keepdims=True)
            dohat = don * gv
            do_ref[:, sl] = (rstd * (dohat - ohat * jnp.mean(dohat * ohat, axis=-1, keepdims=True))
                             ).astype(do_ref.dtype)
        dg_ref[...] += jnp.broadcast_to(dg, dg_ref.shape)

    row = pl.BlockSpec((tm, VAL_W), lambda i: (i, 0))
    return pl.pallas_call(
        body, grid=(s // tm,),
        in_specs=[row, row, _seg(tm, SEG_R), pl.BlockSpec((1, HEAD_V), lambda i: (0, 0))],
        out_specs=[row, row, pl.BlockSpec((8, HEAD_V), lambda i: (0, 0))],
        out_shape=[jax.ShapeDtypeStruct((s, VAL_W), BF16), jax.ShapeDtypeStruct((s, VAL_W), BF16),
                   jax.ShapeDtypeStruct((8, HEAD_V), F32)],
        name=name, compiler_params=_cparams(("arbitrary",)),
    )(doa, o, proj, g)


HALO = SUBLANES_BF16


def _shift_down(u, p1, p2, n, rows):
    rolled = pltpu.roll(u, n, 0)
    if n == 1:
        return jnp.where(rows == 0, p1, rolled)
    return jnp.where(rows == 0, p2, jnp.where(rows == 1, p1, rolled))


def _shift_up(u, n1, n2, n, rows, tm):
    rolled = pltpu.roll(u, tm - n, 0)
    if n == 1:
        return jnp.where(rows == tm - 1, n1, rolled)
    return jnp.where(rows == tm - 2, n1, jnp.where(rows == tm - 1, n2, rolled))


def _conv_terms(gc_ref, cx_ref, gcp_ref, cxp_ref, tm):
    i = pl.program_id(0)
    u = gc_ref[...].astype(F32) * cx_ref[...].astype(F32)
    up = gcp_ref[...].astype(F32) * cxp_ref[...].astype(F32) * (i > 0).astype(F32)
    rows = lax.broadcasted_iota(jnp.int32, (tm, 1), 0)
    u1 = _shift_down(u, up[HALO - 1:HALO, :], up[HALO - 2:HALO - 1, :], 1, rows)
    u2 = _shift_down(u, up[HALO - 1:HALO, :], up[HALO - 2:HALO - 1, :], 2, rows)
    return u, u1, u2, rows


def _prev_halo(tm, seg):
    return pl.BlockSpec((HALO, D_MODEL), lambda i: (jnp.maximum(i * (tm // HALO) - 1, 0), seg))


def _conv_fwd(proj, w, b, name):
    s = proj.shape[0]
    tm = _pick(s, (512, 256))

    def body(gbi_ref, gc_ref, cx_ref, gcp_ref, cxp_ref, w_ref, b_ref, cb_ref):
        u, u1, u2, _ = _conv_terms(gc_ref, cx_ref, gcp_ref, cxp_ref, tm)
        conv = w_ref[0:1, :] * u2 + w_ref[1:2, :] * u1 + w_ref[2:3, :] * u + b_ref[...]
        cb_ref[...] = (gbi_ref[...].astype(F32) * conv).astype(cb_ref.dtype)

    return pl.pallas_call(
        body, grid=(s // tm,),
        in_specs=[_seg(tm, SEG_GBI), _seg(tm, SEG_GCI), _seg(tm, SEG_CX),
                  _prev_halo(tm, SEG_GCI), _prev_halo(tm, SEG_CX),
                  pl.BlockSpec((3, D_MODEL), lambda i: (0, 0)), pl.BlockSpec((1, D_MODEL), lambda i: (0, 0))],
        out_specs=pl.BlockSpec((tm, D_MODEL), lambda i: (i, 0)),
        out_shape=jax.ShapeDtypeStruct((s, D_MODEL), BF16), name=name, compiler_params=_cparams(("parallel",)),
    )(proj, proj, proj, proj, proj, w, b)


def _conv_bwd(dcb, proj, w, b, name):
    s = proj.shape[0]
    tm = _pick(s, (512, 256))
    nt = s // tm

    def body(dcb_ref, gbi_ref, gc_ref, cx_ref, gcp_ref, cxp_ref, dcbn_ref, gbin_ref, w_ref, b_ref,
             d3_ref, dwb_ref):
        i = pl.program_id(0)

        @pl.when(i == 0)
        def _():
            dwb_ref[...] = jnp.zeros_like(dwb_ref)

        u, u1, u2, rows = _conv_terms(gc_ref, cx_ref, gcp_ref, cxp_ref, tm)
        w0, w1, w2 = w_ref[0:1, :], w_ref[1:2, :], w_ref[2:3, :]
        conv = w0 * u2 + w1 * u1 + w2 * u + b_ref[...]
        dcbv = dcb_ref[...].astype(F32)
        gbi = gbi_ref[...].astype(F32)
        dconv = dcbv * gbi
        dnext = dcbn_ref[...].astype(F32) * gbin_ref[...].astype(F32) * (i < nt - 1).astype(F32)
        dc1 = _shift_up(dconv, dnext[0:1, :], dnext[1:2, :], 1, rows, tm)
        dc2 = _shift_up(dconv, dnext[0:1, :], dnext[1:2, :], 2, rows, tm)
        du = w2 * dconv + w1 * dc1 + w0 * dc2
        d3_ref[:, 0:D_MODEL] = (dcbv * conv).astype(d3_ref.dtype)
        d3_ref[:, D_MODEL:2 * D_MODEL] = (du * cx_ref[...].astype(F32)).astype(d3_ref.dtype)
        d3_ref[:, 2 * D_MODEL:3 * D_MODEL] = (du * gc_ref[...].astype(F32)).astype(d3_ref.dtype)
        dwb_ref[0:1, :] += jnp.sum(dconv * u2, axis=0, keepdims=True)
        dwb_ref[1:2, :] += jnp.sum(dconv * u1, axis=0, keepdims=True)
        dwb_ref[2:3, :] += jnp.sum(dconv * u, axis=0, keepdims=True)
        dwb_ref[3:4, :] += jnp.sum(dconv, axis=0, keepdims=True)

    def next_halo(seg_fn):
        return pl.BlockSpec((HALO, D_MODEL), lambda i: (jnp.minimum((i + 1) * (tm // HALO), s // HALO - 1), seg_fn))

    return pl.pallas_call(
        body, grid=(nt,),
        in_specs=[pl.BlockSpec((tm, D_MODEL), lambda i: (i, 0)),
                  _seg(tm, SEG_GBI), _seg(tm, SEG_GCI), _seg(tm, SEG_CX),
                  _prev_halo(tm, SEG_GCI), _prev_halo(tm, SEG_CX),
                  next_halo(0), next_halo(SEG_GBI),
                  pl.BlockSpec((3, D_MODEL), lambda i: (0, 0)), pl.BlockSpec((1, D_MODEL), lambda i: (0, 0))],
        out_specs=[pl.BlockSpec((tm, 3 * D_MODEL), lambda i: (i, 0)), pl.BlockSpec((8, D_MODEL), lambda i: (0, 0))],
        out_shape=[jax.ShapeDtypeStruct((s, 3 * D_MODEL), BF16), jax.ShapeDtypeStruct((8, D_MODEL), F32)],
        name=name, compiler_params=_cparams(("arbitrary",)),
    )(dcb, proj, proj, proj, proj, proj, dcb, proj, w, b)


def _mix_fwd(proj, ya, yb, name):
    s = proj.shape[0]
    tm = _pick(s, (512, 256))

    def body(ga_ref, gb_ref, ya_ref, yb_ref, o_ref):
        o_ref[...] = (_sigmoid(ga_ref[...].astype(F32)) * ya_ref[...].astype(F32)
                      + _sigmoid(gb_ref[...].astype(F32)) * yb_ref[...].astype(F32)).astype(o_ref.dtype)

    row = pl.BlockSpec((tm, D_MODEL), lambda i: (i, 0))
    return pl.pallas_call(
        body, grid=(s // tm,), in_specs=[_seg(tm, SEG_GA), _seg(tm, SEG_GB), row, row], out_specs=row,
        out_shape=jax.ShapeDtypeStruct((s, D_MODEL), BF16), name=name, compiler_params=_cparams(("parallel",)),
    )(proj, proj, ya, yb)


def _mix_bwd(dmix, proj, ya, yb, name):
    s = proj.shape[0]
    tm = _pick(s, (512, 256))

    def body(dm_ref, ga_ref, gb_ref, ya_ref, yb_ref, dg_ref, dya_ref, dyb_ref):
        dm = dm_ref[...].astype(F32)
        sa = _sigmoid(ga_ref[...].astype(F32))
        sb = _sigmoid(gb_ref[...].astype(F32))
        dg_ref[:, 0:D_MODEL] = (dm * ya_ref[...].astype(F32) * sa * (1.0 - sa)).astype(dg_ref.dtype)
        dg_ref[:, D_MODEL:2 * D_MODEL] = (dm * yb_ref[...].astype(F32) * sb * (1.0 - sb)).astype(dg_ref.dtype)
        dya_ref[...] = (dm * sa).astype(dya_ref.dtype)
        dyb_ref[...] = (dm * sb).astype(dyb_ref.dtype)

    row = pl.BlockSpec((tm, D_MODEL), lambda i: (i, 0))
    return pl.pallas_call(
        body, grid=(s // tm,), in_specs=[row, _seg(tm, SEG_GA), _seg(tm, SEG_GB), row, row],
        out_specs=[pl.BlockSpec((tm, 2 * D_MODEL), lambda i: (i, 0)), row, row],
        out_shape=[jax.ShapeDtypeStruct((s, 2 * D_MODEL), BF16), jax.ShapeDtypeStruct((s, D_MODEL), BF16),
                   jax.ShapeDtypeStruct((s, D_MODEL), BF16)],
        name=name, compiler_params=_cparams(("parallel",)),
    )(dmix, proj, proj, ya, yb)


def _swiglu_fwd(gu, name):
    s = gu.shape[0]
    tm = _pick(s, (256,))

    def body(gu_ref, o_ref):
        gate = gu_ref[:, 0:FFN].astype(F32)
        o_ref[...] = (gate * _sigmoid(gate) * gu_ref[:, FFN:2 * FFN].astype(F32)).astype(o_ref.dtype)

    return pl.pallas_call(
        body, grid=(s // tm,), in_specs=[pl.BlockSpec((tm, 2 * FFN), lambda i: (i, 0))],
        out_specs=pl.BlockSpec((tm, FFN), lambda i: (i, 0)),
        out_shape=jax.ShapeDtypeStruct((s, FFN), BF16), name=name, compiler_params=_cparams(("parallel",)),
    )(gu)


def _swiglu_bwd(dhid, gu, name):
    s = gu.shape[0]
    tm = _pick(s, (256,))

    def body(dh_ref, gu_ref, o_ref):
        gate = gu_ref[:, 0:FFN].astype(F32)
        up = gu_ref[:, FFN:2 * FFN].astype(F32)
        dh = dh_ref[...].astype(F32)
        sg = _sigmoid(gate)
        o_ref[:, 0:FFN] = (dh * up * (sg * (1.0 + gate * (1.0 - sg)))).astype(o_ref.dtype)
        o_ref[:, FFN:2 * FFN] = (dh * gate * sg).astype(o_ref.dtype)

    wide = pl.BlockSpec((tm, 2 * FFN), lambda i: (i, 0))
    return pl.pallas_call(
        body, grid=(s // tm,), in_specs=[pl.BlockSpec((tm, FFN), lambda i: (i, 0)), wide], out_specs=wide,
        out_shape=jax.ShapeDtypeStruct((s, 2 * FFN), BF16), name=name, compiler_params=_cparams(("parallel",)),
    )(dhid, gu)


def _adamw_math(w, g, m, v):
    m2 = ADAM_B1 * m + (1.0 - ADAM_B1) * g
    v2 = ADAM_B2 * v + (1.0 - ADAM_B2) * (g * g)
    m_hat = m2 / (1.0 - ADAM_B1 ** ADAM_STEP)
    v_hat = v2 / (1.0 - ADAM_B2 ** ADAM_STEP)
    delta = -ADAM_LR * (m_hat / (jnp.sqrt(v_hat) + ADAM_EPS) + ADAM_WD * w)
    return delta, m2, v2


def _adamw(w, g, m, v, name):
    shape = w.shape
    cols = shape[-1]
    rows = int(np.prod(shape[:-1])) if len(shape) > 1 else 1
    w2, g2, m2, v2 = [t.reshape(rows, cols) for t in (w, g, m, v)]
    tr = _pick(rows, (512, 352, 256)) if rows % 8 == 0 else rows

    def body(w_ref, g_ref, m_ref, v_ref, d_ref, nm_ref, nv_ref):
        d, nm, nv = _adamw_math(w_ref[...], g_ref[...], m_ref[...], v_ref[...])
        d_ref[...] = d
        nm_ref[...] = nm
        nv_ref[...] = nv

    blk = pl.BlockSpec((tr, cols), lambda i: (i, 0))
    out = pl.pallas_call(
        body, grid=(rows // tr,), in_specs=[blk] * 4, out_specs=[blk] * 3,
        out_shape=[jax.ShapeDtypeStruct((rows, cols), F32)] * 3, name=name,
        compiler_params=_cparams(("parallel",)),
    )(w2, g2, m2, v2)
    return [t.reshape(shape) for t in out]


def _pair_sum(g2, recv, core, name):
    _, nchip, r, c = g2.shape
    tr = _pick(r, (496, 248))

    def body(core_ref, a_ref, b_ref, o_ref):
        o_ref[...] = (a_ref[...].astype(F32) + b_ref[...].astype(F32)).astype(o_ref.dtype)

    grid_spec = pltpu.PrefetchScalarGridSpec(
        num_scalar_prefetch=1, grid=(nchip, r // tr),
        in_specs=[pl.BlockSpec((None, None, tr, c), lambda k, i, cr: (cr[0], k, i, 0)),
                  pl.BlockSpec((None, tr, c), lambda k, i, cr: (k, i, 0))],
        out_specs=pl.BlockSpec((None, tr, c), lambda k, i, cr: (k, i, 0)))
    return pl.pallas_call(
        body, grid_spec=grid_spec, out_shape=jax.ShapeDtypeStruct((nchip, r, c), BF16), name=name,
        compiler_params=_cparams(("parallel", "parallel")),
    )(core, g2, recv)


def _chip_sum(a, recv, chip, name):
    _, r, c = a.shape
    tr = _pick(r, (496, 248))

    def body(chip_ref, a_ref, b_ref, o_ref):
        o_ref[...] = ((a_ref[...].astype(F32) + b_ref[0].astype(F32)) + b_ref[1].astype(F32)) + b_ref[2].astype(F32)

    grid_spec = pltpu.PrefetchScalarGridSpec(
        num_scalar_prefetch=1, grid=(r // tr,),
        in_specs=[pl.BlockSpec((None, tr, c), lambda i, cr: (cr[0], i, 0)),
                  pl.BlockSpec((3, tr, c), lambda i, cr: (0, i, 0))],
        out_specs=pl.BlockSpec((tr, c), lambda i, cr: (i, 0)))
    return pl.pallas_call(
        body, grid_spec=grid_spec, out_shape=jax.ShapeDtypeStruct((r, c), F32), name=name,
        compiler_params=_cparams(("parallel",)),
    )(chip, a, recv)


def _sum_devices(parts, name):
    n, r, c = parts.shape

    def body(p_ref, o_ref):
        acc = p_ref[0]
        for d in range(1, n):
            acc = acc + p_ref[d]
        o_ref[...] = acc

    return pl.pallas_call(
        body, out_shape=jax.ShapeDtypeStruct((r, c), F32), name=name,
        in_specs=[pl.BlockSpec(memory_space=pltpu.VMEM)], out_specs=pl.BlockSpec(memory_space=pltpu.VMEM),
    )(parts)


def _me():
    return lax.axis_index("x"), lax.axis_index("y"), lax.axis_index("c")


_CHIP_FLIPS = ((1, 0), (0, 1), (1, 1))


def _all_gather(shard, name):
    r, c = shard.shape

    def body(x_ref, out_ref, send_sems, recv_sems, local_sem):
        x, y, cc = _me()
        sibling = (x, y, 1 - cc)
        chips = [(x ^ fx, y ^ fy) for fx, fy in _CHIP_FLIPS]

        def slot(px, py, pc):
            return out_ref.at[4 * px + 2 * py + pc]

        def copy(k, block, to, src=None):
            return pltpu.make_async_remote_copy(
                src_ref=slot(*block) if src is None else src, dst_ref=slot(*block),
                send_sem=send_sems.at[k], recv_sem=recv_sems.at[k], device_id=to, device_id_type=MESH)

        mine = pltpu.make_async_copy(x_ref, slot(x, y, cc), local_sem)
        mine.start()
        first = [copy(0, (x, y, cc), sibling, src=x_ref)]
        first += [copy(1 + j, (x, y, cc), (*chip, cc), src=x_ref) for j, chip in enumerate(chips)]
        for cp in first:
            cp.start()
        passed = [copy(4 + j, (*chip, cc), sibling) for j, chip in enumerate(chips)]
        for j, chip in enumerate(chips):
            copy(1 + j, (*chip, cc), (x, y, cc)).wait_recv()
            passed[j].start()
        copy(0, sibling, (x, y, cc)).wait_recv()
        for j, chip in enumerate(chips):
            copy(4 + j, (*chip, 1 - cc), (x, y, cc)).wait_recv()
        for cp in first + passed:
            cp.wait_send()
        mine.wait()

    return pl.pallas_call(
        body, out_shape=jax.ShapeDtypeStruct((N_DEV, r, c), shard.dtype), name=name,
        in_specs=[pl.BlockSpec(memory_space=pl.ANY)], out_specs=pl.BlockSpec(memory_space=pl.ANY),
        scratch_shapes=[pltpu.SemaphoreType.DMA((7,)), pltpu.SemaphoreType.DMA((7,)), pltpu.SemaphoreType.DMA],
    )(shard)


def _send_to_sibling(g2, name):
    _, nchip, r, c = g2.shape

    def body(g_ref, out_ref, send_sem, recv_sem):
        x, y, cc = _me()
        cp = pltpu.make_async_remote_copy(
            src_ref=g_ref.at[1 - cc], dst_ref=out_ref, send_sem=send_sem, recv_sem=recv_sem,
            device_id=(x, y, 1 - cc), device_id_type=MESH)
        cp.start()
        cp.wait()

    return pl.pallas_call(
        body, out_shape=jax.ShapeDtypeStruct((nchip, r, c), g2.dtype), name=name,
        in_specs=[pl.BlockSpec(memory_space=pl.ANY)], out_specs=pl.BlockSpec(memory_space=pl.ANY),
        scratch_shapes=[pltpu.SemaphoreType.DMA, pltpu.SemaphoreType.DMA],
    )(g2)


def _send_to_chips(a, name):
    _, r, c = a.shape

    def body(a_ref, out_ref, send_sems, recv_sems):
        x, y, cc = _me()
        copies = []
        for k, (fx, fy) in enumerate(_CHIP_FLIPS):
            px, py = x ^ fx, y ^ fy
            cp = pltpu.make_async_remote_copy(
                src_ref=a_ref.at[2 * px + py], dst_ref=out_ref.at[k], send_sem=send_sems.at[k],
                recv_sem=recv_sems.at[k], device_id=(px, py, cc), device_id_type=MESH)
            cp.start()
            copies.append(cp)
        for cp in copies:
            cp.wait()

    return pl.pallas_call(
        body, out_shape=jax.ShapeDtypeStruct((3, r, c), a.dtype), name=name,
        in_specs=[pl.BlockSpec(memory_space=pl.ANY)], out_specs=pl.BlockSpec(memory_space=pl.ANY),
        scratch_shapes=[pltpu.SemaphoreType.DMA((3,)), pltpu.SemaphoreType.DMA((3,))],
    )(a)


def _pad_rows(a, rows):
    return jnp.pad(a, ((0, rows - a.shape[0]), (0, 0)))


def _pack_weight_shards(w_in, w_fg2, conv_w, w_oa, w_ob, w_o, w_gate, w_up, w_down):
    parts = []
    for l in range(DEPTH):
        conv_bits = lax.bitcast_convert_type(conv_w[l].reshape(-1), BF16).reshape(1, -1)
        head = jnp.concatenate([
            w_in[l].astype(BF16).reshape(W_IN_SHARD, D_MODEL),
            w_fg2[l].astype(BF16).reshape(1, D_MODEL),
            jnp.pad(conv_bits, ((0, 0), (0, D_MODEL - conv_bits.shape[1])))], axis=0)
        parts += [_pad_rows(head, OFF_OA), w_oa[l].astype(BF16), w_ob[l].astype(BF16), w_o[l].astype(BF16),
                  w_gate[l].astype(BF16).reshape(FFN_SHARD, D_MODEL),
                  w_up[l].astype(BF16).reshape(FFN_SHARD, D_MODEL), w_down[l].astype(BF16)]
    return jnp.concatenate(parts, axis=0)


def _cols_from_shards(blocks, rows, shard_cols):
    return blocks.reshape(N_DEV, rows, shard_cols).transpose(1, 0, 2).reshape(rows, N_DEV * shard_cols)


def _unpack_weights(gathered, l):
    base = l * ROWS_LAYER

    def rows(off, n):
        return gathered[:, base + off:base + off + n, :]

    w_in = _cols_from_shards(rows(0, W_IN_SHARD), D_MODEL, W_IN_SHARD)
    fz0 = 2 * KEY_W + 2 * VAL_W
    w_in = jnp.concatenate([w_in[:, :fz0], w_in[:, fz0 + RANK:], w_in[:, fz0:fz0 + RANK],
                            jnp.zeros((D_MODEL, PROJ_W - IN_WIDTH), BF16)], axis=1)
    w_fg2 = _cols_from_shards(rows(OFF_FG2, 1), RANK, KEY_W // N_DEV)
    w_fg2 = jnp.pad(w_fg2, ((0, LANES - RANK), (0, 0)))
    conv_bits = rows(OFF_CONV, 1)[:, 0, :2 * 3 * ROW_SHARD].reshape(N_DEV, 3 * ROW_SHARD, 2)
    conv_w = lax.bitcast_convert_type(conv_bits, F32).reshape(N_DEV, 3, ROW_SHARD)
    conv_w = conv_w.transpose(1, 0, 2).reshape(3, D_MODEL)
    w_gu = jnp.concatenate([_cols_from_shards(rows(OFF_GATE, FFN_SHARD), D_MODEL, FFN_SHARD),
                            _cols_from_shards(rows(OFF_UP, FFN_SHARD), D_MODEL, FFN_SHARD)], axis=1)
    return dict(
        w_in=w_in, w_fg2=w_fg2, conv_w=conv_w,
        w_oa=rows(OFF_OA, ROW_SHARD).reshape(D_MODEL, D_MODEL),
        w_ob=rows(OFF_OB, ROW_SHARD).reshape(D_MODEL, D_MODEL),
        w_o=rows(OFF_O, ROW_SHARD).reshape(D_MODEL, D_MODEL),
        w_gu=w_gu, w_down=rows(OFF_DOWN, FFN_SHARD).reshape(FFN, D_MODEL))


def _shards_from_cols(full, rows, shard_cols):
    t = full.reshape(rows, 2, 2, 2, shard_cols).transpose(3, 1, 2, 0, 4)
    return t.reshape(2, 4, rows * shard_cols // D_MODEL, D_MODEL)


def _shards_from_rows(full, shard_rows):
    return full.reshape(2, 2, 2, shard_rows, D_MODEL).transpose(2, 0, 1, 3, 4).reshape(2, 4, shard_rows, D_MODEL)


def _pack_grads(grads):
    parts = []
    for g in grads:
        fz0 = 2 * KEY_W + 2 * VAL_W
        d_in = g["w_in"]
        d_in = jnp.concatenate([d_in[:, :fz0], d_in[:, FZ_COL:FZ_COL + RANK], d_in[:, fz0:FZ_COL]], axis=1)
        conv = g["conv_w"].astype(BF16).reshape(3, 2, 2, 2, ROW_SHARD).transpose(3, 1, 2, 0, 4)
        conv = jnp.pad(conv.reshape(2, 4, 1, 3 * ROW_SHARD), ((0, 0), (0, 0), (0, 0), (0, D_MODEL - 3 * ROW_SHARD)))
        head = jnp.concatenate([
            _shards_from_cols(d_in, D_MODEL, W_IN_SHARD),
            _shards_from_cols(g["w_fg2"][:RANK], RANK, KEY_W // N_DEV),
            conv], axis=2)
        head = jnp.pad(head, ((0, 0), (0, 0), (0, OFF_OA - head.shape[2]), (0, 0)))
        parts += [head, _shards_from_rows(g["w_oa"], ROW_SHARD), _shards_from_rows(g["w_ob"], ROW_SHARD),
                  _shards_from_rows(g["w_o"], ROW_SHARD),
                  _shards_from_cols(g["w_gu"][:, :FFN], D_MODEL, FFN_SHARD),
                  _shards_from_cols(g["w_gu"][:, FFN:], D_MODEL, FFN_SHARD),
                  _shards_from_rows(g["w_down"], FFN_SHARD)]
    return jnp.concatenate(parts, axis=2)


def _unpack_grad_shard(gs):
    def per_layer(fn):
        return jnp.stack([fn(gs[l * ROWS_LAYER:(l + 1) * ROWS_LAYER]) for l in range(DEPTH)])

    return dict(
        w_in=per_layer(lambda t: t[:W_IN_SHARD].reshape(D_MODEL, W_IN_SHARD)),
        w_fg2=per_layer(lambda t: t[OFF_FG2].reshape(RANK, KEY_W // N_DEV)),
        conv_w=per_layer(lambda t: t[OFF_CONV, :3 * ROW_SHARD].reshape(3, ROW_SHARD)),
        w_oa=per_layer(lambda t: t[OFF_OA:OFF_OA + ROW_SHARD]),
        w_ob=per_layer(lambda t: t[OFF_OB:OFF_OB + ROW_SHARD]),
        w_o=per_layer(lambda t: t[OFF_O:OFF_O + ROW_SHARD]),
        w_ffn_gate=per_layer(lambda t: t[OFF_GATE:OFF_GATE + FFN_SHARD].reshape(D_MODEL, FFN_SHARD)),
        w_ffn_up=per_layer(lambda t: t[OFF_UP:OFF_UP + FFN_SHARD].reshape(D_MODEL, FFN_SHARD)),
        w_ffn_down=per_layer(lambda t: t[OFF_DOWN:OFF_DOWN + FFN_SHARD]))


def _layer_fwd(x, p, l):
    tag = f"l{l}_"
    h = _rmsnorm_fwd(x, p["norm1_g"], tag + "norm1")
    proj = _matmul(h, p["w_in"], "nn", BF16, tag + "proj")
    o, states = _gla_fwd(proj, p["w_fg2"], p["b_fg"], tag + "gla_fwd")
    oa = _gla_post_fwd(o, proj, p["gla_norm_g"], tag + "gla_post")
    ya = _matmul(oa, p["w_oa"], "nn", BF16, tag + "ya")
    cb = _conv_fwd(proj, p["conv_w"], p["conv_b"], tag + "conv")
    yb = _matmul(cb, p["w_ob"], "nn", BF16, tag + "yb")
    mix = _mix_fwd(proj, ya, yb, tag + "mix")
    x1 = _matmul(mix, p["w_o"], "nn", F32, tag + "x1", residual=x)
    h2 = _rmsnorm_fwd(x1, p["norm2_g"], tag + "norm2")
    gu = _matmul(h2, p["w_gu"], "nn", BF16, tag + "gu")
    hid = _swiglu_fwd(gu, tag + "swiglu")
    x2 = _matmul(hid, p["w_down"], "nn", F32, tag + "x2", residual=x1)
    saved = dict(x=x, h=h, proj=proj, o=o, states=states, oa=oa, ya=ya, cb=cb, yb=yb, mix=mix, x1=x1, h2=h2,
                 gu=gu, hid=hid)
    return x2, saved


def _layer_bwd(dx2, p, sv, l):
    tag = f"l{l}_b_"
    dx2h = dx2.astype(BF16)
    dhid = _matmul(dx2h, p["w_down"], "nt", BF16, tag + "dhid")
    d_down = _matmul(sv["hid"], dx2h, "tn", BF16, tag + "dw_down")
    dgu = _swiglu_bwd(dhid, sv["gu"], tag + "swiglu")
    dh2 = _matmul(dgu, p["w_gu"], "nt", F32, tag + "dh2")
    d_gu = _matmul(sv["h2"], dgu, "tn", BF16, tag + "dw_gu")
    dx1, dg2 = _rmsnorm_bwd(sv["x1"], p["norm2_g"], dh2, dx2, tag + "norm2")
    dx1h = dx1.astype(BF16)
    dmix = _matmul(dx1h, p["w_o"], "nt", BF16, tag + "dmix")
    d_o = _matmul(sv["mix"], dx1h, "tn", BF16, tag + "dw_o")
    dgab, dya, dyb = _mix_bwd(dmix, sv["proj"], sv["ya"], sv["yb"], tag + "mix")
    dcb = _matmul(dyb, p["w_ob"], "nt", BF16, tag + "dcb")
    d_ob = _matmul(sv["cb"], dyb, "tn", BF16, tag + "dw_ob")
    d3, dwb = _conv_bwd(dcb, sv["proj"], p["conv_w"], p["conv_b"], tag + "conv")
    doa = _matmul(dya, p["w_oa"], "nt", BF16, tag + "doa")
    d_oa = _matmul(sv["oa"], dya, "tn", BF16, tag + "dw_oa")
    dr, do, dgg = _gla_post_bwd(doa, sv["o"], sv["proj"], p["gla_norm_g"], tag + "gla_post")
    dq, dk, dv, dfg, dbfg = _gla_bwd(sv["proj"], p["w_fg2"], p["b_fg"], sv["states"], do, tag + "gla")
    fz = sv["proj"][:, FZ_COL:]
    dfz = _matmul(dfg, p["w_fg2"], "nt", BF16, tag + "dfz")
    d_fg2 = _matmul(fz, dfg, "tn", BF16, tag + "dw_fg2")
    dproj = jnp.concatenate([dq, dk, dv, dr, d3, dgab, dfz], axis=1)
    dh = _matmul(dproj, p["w_in"], "nt", F32, tag + "dh")
    d_in = _matmul(sv["h"], dproj, "tn", BF16, tag + "dw_in")
    dx, dg1 = _rmsnorm_bwd(sv["x"], p["norm1_g"], dh, dx1, tag + "norm1")
    big = dict(w_in=d_in, w_fg2=d_fg2, conv_w=dwb[0:3], w_oa=d_oa, w_ob=d_ob, w_o=d_o, w_gu=d_gu, w_down=d_down)
    pad = lambda t: jnp.pad(t, ((0, 0), (0, D_MODEL - t.shape[1])))
    small = [dg1[0:1], pad(dbfg[0:1]), pad(dgg[0:1]), dwb[3:4], dg2[0:1]]
    return dx, big, small


def _local_step(x, target, layers, final_g):
    saved = []
    for l, p in enumerate(layers):
        x, sv = _layer_fwd(x, p, l)
        saved.append(sv)
    loss, dx, dgf = _loss_head(x, final_g, target, "loss_head")
    bigs, smalls = [None] * DEPTH, [None] * DEPTH
    for l in reversed(range(DEPTH)):
        dx, bigs[l], smalls[l] = _layer_bwd(dx, layers[l], saved[l], l)
    small = jnp.concatenate(smalls[0] + smalls[1] + [dgf[0:1]], axis=0)
    small = jnp.pad(small, ((0, SMALL_ROWS - small.shape[0]), (0, 0)))
    return loss[0, 0], dx, bigs, small


def kernel(x, norm1_g, w_in, w_fg2, b_fg, gla_norm_g, w_oa, conv_w, conv_b, w_ob, w_o, norm2_g, w_ffn_gate, w_ffn_up, w_ffn_down, final_g, loss_target, m_norm1_g, m_w_in, m_w_fg2, m_b_fg, m_gla_norm_g, m_w_oa, m_conv_w, m_conv_b, m_w_ob, m_w_o, m_norm2_g, m_w_ffn_gate, m_w_ffn_up, m_w_ffn_down, m_final_g, v_norm1_g, v_w_in, v_w_fg2, v_b_fg, v_gla_norm_g, v_w_oa, v_conv_w, v_conv_b, v_w_ob, v_w_o, v_norm2_g, v_w_ffn_gate, v_w_ffn_up, v_w_ffn_down, v_final_g):
    names = ["norm1_g", "w_in", "w_fg2", "b_fg", "gla_norm_g", "w_oa", "conv_w", "conv_b", "w_ob", "w_o",
             "norm2_g", "w_ffn_gate", "w_ffn_up", "w_ffn_down", "final_g"]
    w = dict(zip(names, [norm1_g, w_in, w_fg2, b_fg, gla_norm_g, w_oa, conv_w, conv_b, w_ob, w_o, norm2_g,
                         w_ffn_gate, w_ffn_up, w_ffn_down, final_g]))
    m = dict(zip(names, [m_norm1_g, m_w_in, m_w_fg2, m_b_fg, m_gla_norm_g, m_w_oa, m_conv_w, m_conv_b, m_w_ob,
                         m_w_o, m_norm2_g, m_w_ffn_gate, m_w_ffn_up, m_w_ffn_down, m_final_g]))
    v = dict(zip(names, [v_norm1_g, v_w_in, v_w_fg2, v_b_fg, v_gla_norm_g, v_w_oa, v_conv_w, v_conv_b, v_w_ob,
                         v_w_o, v_norm2_g, v_w_ffn_gate, v_w_ffn_up, v_w_ffn_down, v_final_g]))
    cx, cy, cc = _me()

    packed = _pack_weight_shards(w_in, w_fg2, conv_w, w_oa, w_ob, w_o, w_ffn_gate, w_ffn_up, w_ffn_down)
    gathered = _all_gather(packed, "gather_weights")
    layers = []
    for l in range(DEPTH):
        p = _unpack_weights(gathered, l)
        p.update(norm1_g=norm1_g[l][None], b_fg=b_fg[l][None], gla_norm_g=gla_norm_g[l][None],
                 conv_b=conv_b[l][None], norm2_g=norm2_g[l][None])
        layers.append(p)

    loss, dx, bigs, small = _local_step(x[0], loss_target[0], layers, final_g[None])

    g2 = _pack_grads(bigs)
    from_sibling = _send_to_sibling(g2, "grads_to_sibling")
    chip_sums = _pair_sum(g2, from_sibling, jnp.reshape(cc, (1,)).astype(jnp.int32), "pair_sum")
    from_chips = _send_to_chips(chip_sums, "grads_to_chips")
    grad_shard = _chip_sum(chip_sums, from_chips, jnp.reshape(2 * cx + cy, (1,)).astype(jnp.int32), "chip_sum")
    grads = _unpack_grad_shard(grad_shard)

    small_sum = _sum_devices(_all_gather(small, "gather_small"), "sum_small")
    r512, r256 = slice(0, KEY_W), slice(0, HEAD_V)
    grads.update(
        norm1_g=jnp.stack([small_sum[0], small_sum[5]]), b_fg=jnp.stack([small_sum[1, r512], small_sum[6, r512]]),
        gla_norm_g=jnp.stack([small_sum[2, r256], small_sum[7, r256]]),
        conv_b=jnp.stack([small_sum[3], small_sum[8]]), norm2_g=jnp.stack([small_sum[4], small_sum[9]]),
        final_g=small_sum[10])

    deltas, new_m, new_v = {}, {}, {}
    for n in names:
        deltas[n], new_m[n], new_v[n] = _adamw(w[n], grads[n], m[n], v[n], "adamw_" + n)

    total_loss = lax.psum(loss, ("x", "y", "c"))
    return (total_loss, dx[None], *[grads[n] for n in names], *[deltas[n] for n in names],
            *[new_m[n] for n in names], *[new_v[n] for n in names])
```

```python
import functools

import jax
import jax.numpy as jnp
import numpy as np
from jax import lax
from jax.experimental import pallas as pl
from jax.experimental.pallas import tpu as pltpu

F32 = jnp.float32
BF16 = jnp.bfloat16
MESH = pl.DeviceIdType.MESH

D_MODEL = 1024
DEPTH = 2
CHUNK = 64
HEADS = 4
HEAD_K = 128
HEAD_V = 256
KEY_W = HEADS * HEAD_K
VAL_W = HEADS * HEAD_V
RANK = 16
TAU = 16.0
FFN = 2816
IN_WIDTH = 2 * KEY_W + 2 * VAL_W + RANK + 5 * D_MODEL
EPS = 1e-6
Q_SCALE = HEAD_K ** -0.5
N_DEV = 8
ADAM_LR, ADAM_B1, ADAM_B2, ADAM_EPS, ADAM_WD, ADAM_STEP = 0.001, 0.9, 0.999, 1e-08, 0.01, 10

LANES = 128
SUBLANES_BF16 = 16
VMEM_LIMIT = 48 * 1024 * 1024

FZ_COL = 2 * KEY_W + 2 * VAL_W + 5 * D_MODEL
PROJ_W = FZ_COL + LANES
SEG_R, SEG_GBI, SEG_GCI, SEG_CX, SEG_GA, SEG_GB = 2, 3, 4, 5, 6, 7

W_IN_SHARD = IN_WIDTH // N_DEV
FFN_SHARD = FFN // N_DEV
ROW_SHARD = D_MODEL // N_DEV

WIN_IN = 9 * LANES
WIN_GU = 4 * LANES
A_FZ = WIN_IN
A_GATE = A_FZ + LANES
A_UP = A_GATE + WIN_GU
A_COLS = A_UP + WIN_GU
ORIG_FZ = 2 * KEY_W + 2 * VAL_W


def _new_col(o):
    if o < ORIG_FZ:
        return o
    if o < ORIG_FZ + RANK:
        return FZ_COL + (o - ORIG_FZ)
    return o - RANK


def _shift_table():
    t_in, rows = [], []
    for j in range(N_DEV):
        new = [_new_col(W_IN_SHARD * j + i) for i in range(W_IN_SHARD)]
        main = [i for i in range(W_IN_SHARD) if new[i] < FZ_COL]
        code = [i for i in range(W_IN_SHARD) if new[i] >= FZ_COL]
        shift = new[main[0]] - main[0]
        t_in.append(shift // LANES)
        assert all(new[i] - i == shift for i in main) and shift % LANES + W_IN_SHARD <= WIN_IN
        if code:
            cshift = new[code[0]] - FZ_COL - code[0]
            crow = [cshift % LANES, code[0], code[-1] + 1, int(cshift < 0)]
        else:
            crow = [0, 0, 0, 0]
        rows.append([shift % LANES, main[0], main[-1] + 1] + crow + [FFN_SHARD * j % LANES])
    return tuple(t_in), np.asarray(rows, np.int32)


T_IN, SHIFT_TABLE = _shift_table()
T_GU = tuple(FFN_SHARD * j // LANES for j in range(N_DEV))

B_OA, B_OB, B_O, B_DOWN = 0, ROW_SHARD, 2 * ROW_SHARD, 3 * ROW_SHARD
B_FG2 = B_DOWN + FFN_SHARD
B_CONV = B_FG2 + 1
B_ROWS = B_FG2 + SUBLANES_BF16
SMALL_ROWS = 16


def _pick(n, candidates):
    for c in candidates:
        if n % c == 0:
            return c
    return n


def _cparams(sem):
    return pltpu.CompilerParams(dimension_semantics=sem, vmem_limit_bytes=VMEM_LIMIT)


def _sigmoid(x):
    return 1.0 / (1.0 + jnp.exp(-x))


def _matmul(a, b, dims, out_dtype, name, residual=None):
    if dims == "nn":
        (m, k), (k2, n) = a.shape, b.shape
    elif dims == "nt":
        (m, k), (n, k2) = a.shape, b.shape
    else:
        (k, m), (k2, n) = a.shape, b.shape
    assert k == k2, (a.shape, b.shape, dims)
    tm = _pick(m, (512, 256, 128))
    tn = _pick(n, (1664, 1408, 1024, 512, 256, 128))
    tk = _pick(k, (1664, 1408, 1024, 512, 256, 128))
    nk = k // tk
    if dims == "nn":
        a_spec = pl.BlockSpec((tm, tk), lambda i, j, kk: (i, kk))
        b_spec = pl.BlockSpec((tk, tn), lambda i, j, kk: (kk, j))
        contract = (((1,), (0,)), ((), ()))
    elif dims == "nt":
        a_spec = pl.BlockSpec((tm, tk), lambda i, j, kk: (i, kk))
        b_spec = pl.BlockSpec((tn, tk), lambda i, j, kk: (j, kk))
        contract = (((1,), (1,)), ((), ()))
    else:
        a_spec = pl.BlockSpec((tk, tm), lambda i, j, kk: (kk, i))
        b_spec = pl.BlockSpec((tk, tn), lambda i, j, kk: (kk, j))
        contract = (((0,), (0,)), ((), ()))
    o_spec = pl.BlockSpec((tm, tn), lambda i, j, kk: (i, j))
    has_res = residual is not None

    def body(*refs):
        if has_res:
            a_ref, b_ref, r_ref, o_ref, acc_ref = refs
        else:
            a_ref, b_ref, o_ref, acc_ref = refs
            r_ref = None
        kk = pl.program_id(2)
        part = lax.dot_general(a_ref[...], b_ref[...], contract, preferred_element_type=F32)

        @pl.when(kk == 0)
        def _():
            acc_ref[...] = part

        @pl.when(kk > 0)
        def _():
            acc_ref[...] += part

        @pl.when(kk == nk - 1)
        def _():
            res = acc_ref[...]
            if has_res:
                res = res + r_ref[...]
            o_ref[...] = res.astype(o_ref.dtype)

    in_specs = [a_spec, b_spec] + ([o_spec] if has_res else [])
    args = (a, b) + ((residual,) if has_res else ())
    return pl.pallas_call(
        body, grid=(m // tm, n // tn, nk), in_specs=in_specs, out_specs=o_spec,
        out_shape=jax.ShapeDtypeStruct((m, n), out_dtype),
        scratch_shapes=[pltpu.VMEM((tm, tn), F32)], name=name,
        compiler_params=_cparams(("parallel", "parallel", "arbitrary")),
    )(*args)


def _rmsnorm_fwd(x, g, name):
    s, d = x.shape
    tm = _pick(s, (512, 256))

    def body(x_ref, g_ref, o_ref):
        xv = x_ref[...]
        r = lax.rsqrt(jnp.mean(xv * xv, axis=-1, keepdims=True) + EPS)
        o_ref[...] = (xv * r * g_ref[...]).astype(o_ref.dtype)

    row = pl.BlockSpec((tm, d), lambda i: (i, 0))
    return pl.pallas_call(
        body, grid=(s // tm,), in_specs=[row, pl.BlockSpec((1, d), lambda i: (0, 0))], out_specs=row,
        out_shape=jax.ShapeDtypeStruct((s, d), BF16), name=name, compiler_params=_cparams(("parallel",)),
    )(x, g)


def _rmsnorm_bwd(x, g, dh, dres, name):
    s, d = x.shape
    tm = _pick(s, (512, 256))

    def body(x_ref, g_ref, dh_ref, dres_ref, dx_ref, dg_ref):
        xv = x_ref[...]
        r = lax.rsqrt(jnp.mean(xv * xv, axis=-1, keepdims=True) + EPS)
        xn = xv * r
        dhv = dh_ref[...].astype(F32)
        dxn = dhv * g_ref[...]
        dx_ref[...] = dres_ref[...] + r * (dxn - xn * jnp.mean(dxn * xn, axis=-1, keepdims=True))

        @pl.when(pl.program_id(0) == 0)
        def _():
            dg_ref[...] = jnp.zeros_like(dg_ref)

        dg_ref[...] += jnp.broadcast_to(jnp.sum(dhv * xn, axis=0, keepdims=True), dg_ref.shape)

    row = pl.BlockSpec((tm, d), lambda i: (i, 0))
    acc = pl.BlockSpec((8, d), lambda i: (0, 0))
    return pl.pallas_call(
        body, grid=(s // tm,), in_specs=[row, pl.BlockSpec((1, d), lambda i: (0, 0)), row, row],
        out_specs=[row, acc],
        out_shape=[jax.ShapeDtypeStruct((s, d), F32), jax.ShapeDtypeStruct((8, d), F32)], name=name,
        compiler_params=_cparams(("arbitrary",)),
    )(x, g, dh, dres)


def _loss_head(x, g, target, name):
    s, d = x.shape
    tm = _pick(s, (512, 256))

    def body(x_ref, g_ref, t_ref, loss_ref, dx_ref, dg_ref):
        xv = x_ref[...]
        gv = g_ref[...]
        r = lax.rsqrt(jnp.mean(xv * xv, axis=-1, keepdims=True) + EPS)
        xn = xv * r
        err = xn * gv - t_ref[...]
        dy = err * (1.0 / d)
        dxn = dy * gv
        dx_ref[...] = r * (dxn - xn * jnp.mean(dxn * xn, axis=-1, keepdims=True))

        @pl.when(pl.program_id(0) == 0)
        def _():
            dg_ref[...] = jnp.zeros_like(dg_ref)
            loss_ref[...] = jnp.zeros_like(loss_ref)

        dg_ref[...] += jnp.broadcast_to(jnp.sum(dy * xn, axis=0, keepdims=True), dg_ref.shape)
        row_loss = jnp.sum(err * err, axis=-1, keepdims=True)
        loss_ref[...] += jnp.broadcast_to((0.5 / d) * jnp.sum(row_loss, axis=0, keepdims=True), loss_ref.shape)

    row = pl.BlockSpec((tm, d), lambda i: (i, 0))
    return pl.pallas_call(
        body, grid=(s // tm,), in_specs=[row, pl.BlockSpec((1, d), lambda i: (0, 0)), row],
        out_specs=[pl.BlockSpec((8, LANES), lambda i: (0, 0)), row, pl.BlockSpec((8, d), lambda i: (0, 0))],
        out_shape=[jax.ShapeDtypeStruct((8, LANES), F32), jax.ShapeDtypeStruct((s, d), F32),
                   jax.ShapeDtypeStruct((8, d), F32)],
        name=name, compiler_params=_cparams(("arbitrary",)),
    )(x, g, target)


def _tri_dot(tri, x):
    hi = x.astype(BF16)
    lo = (x - hi.astype(F32)).astype(BF16)
    return jnp.dot(tri, hi, preferred_element_type=F32) + jnp.dot(tri, lo, preferred_element_type=F32)


def _tri(strict):
    r = lax.broadcasted_iota(jnp.int32, (CHUNK, CHUNK), 0)
    c = lax.broadcasted_iota(jnp.int32, (CHUNK, CHUNK), 1)
    return jnp.where((r > c) if strict else (r >= c), 1.0, 0.0).astype(BF16)


def _chunk_decay(fz, w, b, tri):
    fg = jnp.dot(fz, w, preferred_element_type=F32) + b
    la = (jnp.minimum(fg, 0.0) - jnp.log(1.0 + jnp.exp(-jnp.abs(fg)))) * (1.0 / TAU)
    cum = _tri_dot(tri, la)
    cum_end = cum[CHUNK - 1:CHUNK, :]
    return fg, jnp.exp(cum_end - cum), jnp.exp(cum_end)


_TN = (((0,), (0,)), ((), ()))
_NT = (((1,), (1,)), ((), ()))


def _gla_specs(rows):
    q_spec = pl.BlockSpec((rows, HEAD_K), lambda h, c: (c, h))
    k_spec = pl.BlockSpec((rows, HEAD_K), lambda h, c: (c, HEADS + h))
    v_spec = pl.BlockSpec((rows, HEAD_V), lambda h, c: (c, HEADS + h))
    fz_spec = pl.BlockSpec((rows, LANES), lambda h, c: (c, FZ_COL // LANES))
    w_spec = pl.BlockSpec((LANES, HEAD_K), lambda h, c: (0, h))
    b_spec = pl.BlockSpec((1, HEAD_K), lambda h, c: (0, h))
    return q_spec, k_spec, v_spec, fz_spec, w_spec, b_spec


def _gla_fwd(proj, wfg, bfg, name):
    s = proj.shape[0]
    nc = s // CHUNK
    per = _pick(nc, (8, 4, 2, 1))
    rows = per * CHUNK

    def body(q_ref, k_ref, v_ref, fz_ref, w_ref, b_ref, o_ref, st_ref, state):
        @pl.when(pl.program_id(1) == 0)
        def _():
            state[...] = jnp.zeros_like(state)

        tri = _tri(False)
        w = w_ref[...]
        b = b_ref[...]
        for i in range(per):
            sl = pl.ds(i * CHUNK, CHUNK)
            _, dec, gamma = _chunk_decay(fz_ref[sl, :], w, b, tri)
            kd = (k_ref[sl, :].astype(F32) * dec).astype(BF16)
            st = state[...] * gamma + lax.dot_general(v_ref[sl, :], kd, _TN, preferred_element_type=F32)
            state[...] = st
            st16 = st.astype(BF16)
            st_ref[0, i] = st16
            qs = (q_ref[sl, :].astype(F32) * Q_SCALE).astype(BF16)
            o_ref[sl, :] = lax.dot_general(qs, st16, _NT, preferred_element_type=F32).astype(o_ref.dtype)

    q_spec, k_spec, v_spec, fz_spec, w_spec, b_spec = _gla_specs(rows)
    return pl.pallas_call(
        body, grid=(HEADS, nc // per),
        in_specs=[q_spec, k_spec, v_spec, fz_spec, w_spec, b_spec],
        out_specs=[pl.BlockSpec((rows, HEAD_V), lambda h, c: (c, h)),
                   pl.BlockSpec((1, per, HEAD_V, HEAD_K), lambda h, c: (h, c, 0, 0))],
        out_shape=[jax.ShapeDtypeStruct((s, VAL_W), BF16),
                   jax.ShapeDtypeStruct((HEADS, nc, HEAD_V, HEAD_K), BF16)],
        scratch_shapes=[pltpu.VMEM((HEAD_V, HEAD_K), F32)], name=name,
        compiler_params=_cparams(("parallel", "arbitrary")),
    )(proj, proj, proj, proj, wfg, bfg)


def _gla_bwd(proj, wfg, bfg, states, do, name):
    s = proj.shape[0]
    nc = s // CHUNK
    per = _pick(nc, (8, 4, 2, 1))
    rows = per * CHUNK
    nblk = nc // per

    def rev(spec_fn):
        return lambda h, j: spec_fn(h, nblk - 1 - j)

    def body(q_ref, k_ref, v_ref, fz_ref, w_ref, b_ref, do_ref, st_ref, prev_ref,
             dq_ref, dk_ref, dv_ref, dfg_ref, db_ref, carry):
        j = pl.program_id(1)

        @pl.when(j == 0)
        def _():
            carry[...] = jnp.zeros_like(carry)
            db_ref[...] = jnp.zeros_like(db_ref)

        tri = _tri(False)
        tri_strict = _tri(True)
        w = w_ref[...]
        b = b_ref[...]
        has_prev = (j < nblk - 1).astype(F32)
        db = jnp.zeros((1, HEAD_K), F32)
        for i in reversed(range(per)):
            sl = pl.ds(i * CHUNK, CHUNK)
            fg, dec, gamma = _chunk_decay(fz_ref[sl, :], w, b, tri)
            kd = k_ref[sl, :].astype(F32) * dec
            qs = (q_ref[sl, :].astype(F32) * Q_SCALE).astype(BF16)
            dov = do_ref[sl, :]
            v = v_ref[sl, :]
            gt = carry[...] + lax.dot_general(dov, qs, _TN, preferred_element_type=F32)
            gt16 = gt.astype(BF16)
            dq_ref[sl, :] = (jnp.dot(dov, st_ref[0, i], preferred_element_type=F32) * Q_SCALE).astype(dq_ref.dtype)
            dkd = jnp.dot(v, gt16, preferred_element_type=F32)
            dv_ref[sl, :] = lax.dot_general(kd.astype(BF16), gt16, _NT,
                                            preferred_element_type=F32).astype(dv_ref.dtype)
            if i > 0:
                st_prev = st_ref[0, i - 1].astype(F32)
            else:
                st_prev = prev_ref[0, 0].astype(F32) * has_prev
            dgamma = jnp.sum(gt * st_prev, axis=0, keepdims=True)
            carry[...] = gt * gamma
            dk_ref[sl, :] = (dkd * dec).astype(dk_ref.dtype)
            dla = dgamma * gamma + _tri_dot(tri_strict, dkd * kd)
            dfg = dla * (1.0 / TAU) * _sigmoid(-fg)
            dfg_ref[sl, :] = dfg.astype(dfg_ref.dtype)
            db = db + jnp.sum(dfg, axis=0, keepdims=True)
        db_ref[...] += jnp.broadcast_to(db, db_ref.shape)

    q_spec, k_spec, v_spec, fz_spec, w_spec, b_spec = _gla_specs(rows)
    q_spec, k_spec, v_spec, fz_spec = [
        pl.BlockSpec(sp.block_shape, rev(sp.index_map)) for sp in (q_spec, k_spec, v_spec, fz_spec)]
    do_spec = pl.BlockSpec((rows, HEAD_V), lambda h, j: (nblk - 1 - j, h))
    st_spec = pl.BlockSpec((1, per, HEAD_V, HEAD_K), lambda h, j: (h, nblk - 1 - j, 0, 0))
    prev_spec = pl.BlockSpec((1, 1, HEAD_V, HEAD_K),
                             lambda h, j: (h, jnp.maximum((nblk - 1 - j) * per - 1, 0), 0, 0))
    key_out = pl.BlockSpec((rows, HEAD_K), lambda h, j: (nblk - 1 - j, h))
    return pl.pallas_call(
        body, grid=(HEADS, nblk),
        in_specs=[q_spec, k_spec, v_spec, fz_spec, w_spec, b_spec, do_spec, st_spec, prev_spec],
        out_specs=[key_out, key_out, do_spec, key_out, pl.BlockSpec((8, HEAD_K), lambda h, j: (0, h))],
        out_shape=[jax.ShapeDtypeStruct((s, KEY_W), BF16), jax.ShapeDtypeStruct((s, KEY_W), BF16),
                   jax.ShapeDtypeStruct((s, VAL_W), BF16), jax.ShapeDtypeStruct((s, KEY_W), BF16),
                   jax.ShapeDtypeStruct((8, KEY_W), F32)],
        scratch_shapes=[pltpu.VMEM((HEAD_V, HEAD_K), F32)], name=name,
        compiler_params=_cparams(("parallel", "arbitrary")),
    )(proj, proj, proj, proj, wfg, bfg, do, states, states)


def _seg(tm, seg):
    return pl.BlockSpec((tm, D_MODEL), lambda i: (i, seg))


def _gla_post_fwd(o, proj, g, name):
    s = o.shape[0]
    tm = _pick(s, (512, 256))

    def body(o_ref, r_ref, g_ref, oa_ref):
        gv = g_ref[...]
        for h in range(HEADS):
            sl = slice(h * HEAD_V, (h + 1) * HEAD_V)
            ov = o_ref[:, sl].astype(F32)
            rstd = lax.rsqrt(jnp.mean(ov * ov, axis=-1, keepdims=True) + EPS)
            rv = r_ref[:, sl].astype(F32)
            oa_ref[:, sl] = (ov * rstd * gv * (rv * _sigmoid(rv))).astype(oa_ref.dtype)

    row = pl.BlockSpec((tm, VAL_W), lambda i: (i, 0))
    return pl.pallas_call(
        body, grid=(s // tm,), in_specs=[row, _seg(tm, SEG_R), pl.BlockSpec((1, HEAD_V), lambda i: (0, 0))],
        out_specs=row, out_shape=jax.ShapeDtypeStruct((s, VAL_W), BF16), name=name,
        compiler_params=_cparams(("parallel",)),
    )(o, proj, g)


def _gla_post_bwd(doa, o, proj, g, name):
    s = o.shape[0]
    tm = _pick(s, (512, 256))

    def body(doa_ref, o_ref, r_ref, g_ref, dr_ref, do_ref, dg_ref):
        @pl.when(pl.program_id(0) == 0)
        def _():
            dg_ref[...] = jnp.zeros_like(dg_ref)

        gv = g_ref[...]
        dg = jnp.zeros((1, HEAD_V), F32)
        for h in range(HEADS):
            sl = slice(h * HEAD_V, (h + 1) * HEAD_V)
            ov = o_ref[:, sl].astype(F32)
            rstd = lax.rsqrt(jnp.mean(ov * ov, axis=-1, keepdims=True) + EPS)
            ohat = ov * rstd
            rv = r_ref[:, sl].astype(F32)
            sg = _sigmoid(rv)
            dv = doa_ref[:, sl].astype(F32)
            dr_ref[:, sl] = (dv * ohat * gv * (sg * (1.0 + rv * (1.0 - sg)))).astype(dr_ref.dtype)
            don = dv * (rv * sg)
            dg = dg + jnp.sum(don * ohat, axis=0, keepdims=True)
            dohat = don * gv
            do_ref[:, sl] = (rstd * (dohat - ohat * jnp.mean(dohat * ohat, axis=-1, keepdims=True))
                             ).astype(do_ref.dtype)
        dg_ref[...] += jnp.broadcast_to(dg, dg_ref.shape)

    row = pl.BlockSpec((tm, VAL_W), lambda i: (i, 0))
    return pl.pallas_call(
        body, grid=(s // tm,),
        in_specs=[row, row, _seg(tm, SEG_R), pl.BlockSpec((1, HEAD_V), lambda i: (0, 0))],
        out_specs=[row, row, pl.BlockSpec((8, HEAD_V), lambda i: (0, 0))],
        out_shape=[jax.ShapeDtypeStruct((s, VAL_W), BF16), jax.ShapeDtypeStruct((s, VAL_W), BF16),
                   jax.ShapeDtypeStruct((8, HEAD_V), F32)],
        name=name, compiler_params=_cparams(("arbitrary",)),
    )(doa, o, proj, g)


HALO = SUBLANES_BF16


def _shift_down(u, p1, p2, n, rows):
    rolled = pltpu.roll(u, n, 0)
    if n == 1:
        return jnp.where(rows == 0, p1, rolled)
    return jnp.where(rows == 0, p2, jnp.where(rows == 1, p1, rolled))


def _shift_up(u, n1, n2, n, rows, tm):
    rolled = pltpu.roll(u, tm - n, 0)
    if n == 1:
        return jnp.where(rows == tm - 1, n1, rolled)
    return jnp.where(rows == tm - 2, n1, jnp.where(rows == tm - 1, n2, rolled))


def _conv_terms(gc_ref, cx_ref, gcp_ref, cxp_ref, tm):
    i = pl.program_id(0)
    u = gc_ref[...].astype(F32) * cx_ref[...].astype(F32)
    up = gcp_ref[...].astype(F32) * cxp_ref[...].astype(F32) * (i > 0).astype(F32)
    rows = lax.broadcasted_iota(jnp.int32, (tm, 1), 0)
    u1 = _shift_down(u, up[HALO - 1:HALO, :], up[HALO - 2:HALO - 1, :], 1, rows)
    u2 = _shift_down(u, up[HALO - 1:HALO, :], up[HALO - 2:HALO - 1, :], 2, rows)
    return u, u1, u2, rows


def _prev_halo(tm, seg):
    return pl.BlockSpec((HALO, D_MODEL), lambda i: (jnp.maximum(i * (tm // HALO) - 1, 0), seg))


def _conv_fwd(proj, w, b, name):
    s = proj.shape[0]
    tm = _pick(s, (512, 256))

    def body(gbi_ref, gc_ref, cx_ref, gcp_ref, cxp_ref, w_ref, b_ref, cb_ref):
        u, u1, u2, _ = _conv_terms(gc_ref, cx_ref, gcp_ref, cxp_ref, tm)
        conv = w_ref[0:1, :] * u2 + w_ref[1:2, :] * u1 + w_ref[2:3, :] * u + b_ref[...]
        cb_ref[...] = (gbi_ref[...].astype(F32) * conv).astype(cb_ref.dtype)

    return pl.pallas_call(
        body, grid=(s // tm,),
        in_specs=[_seg(tm, SEG_GBI), _seg(tm, SEG_GCI), _seg(tm, SEG_CX),
                  _prev_halo(tm, SEG_GCI), _prev_halo(tm, SEG_CX),
                  pl.BlockSpec((3, D_MODEL), lambda i: (0, 0)), pl.BlockSpec((1, D_MODEL), lambda i: (0, 0))],
        out_specs=pl.BlockSpec((tm, D_MODEL), lambda i: (i, 0)),
        out_shape=jax.ShapeDtypeStruct((s, D_MODEL), BF16), name=name, compiler_params=_cparams(("parallel",)),
    )(proj, proj, proj, proj, proj, w, b)


def _conv_bwd(dcb, proj, w, b, name):
    s = proj.shape[0]
    tm = _pick(s, (512, 256))
    nt = s // tm

    def body(dcb_ref, gbi_ref, gc_ref, cx_ref, gcp_ref, cxp_ref, dcbn_ref, gbin_ref, w_ref, b_ref,
             d3_ref, dwb_ref):
        i = pl.program_id(0)

        @pl.when(i == 0)
        def _():
            dwb_ref[...] = jnp.zeros_like(dwb_ref)

        u, u1, u2, rows = _conv_terms(gc_ref, cx_ref, gcp_ref, cxp_ref, tm)
        w0, w1, w2 = w_ref[0:1, :], w_ref[1:2, :], w_ref[2:3, :]
        conv = w0 * u2 + w1 * u1 + w2 * u + b_ref[...]
        dcbv = dcb_ref[...].astype(F32)
        gbi = gbi_ref[...].astype(F32)
        dconv = dcbv * gbi
        dnext = dcbn_ref[...].astype(F32) * gbin_ref[...].astype(F32) * (i < nt - 1).astype(F32)
        dc1 = _shift_up(dconv, dnext[0:1, :], dnext[1:2, :], 1, rows, tm)
        dc2 = _shift_up(dconv, dnext[0:1, :], dnext[1:2, :], 2, rows, tm)
        du = w2 * dconv + w1 * dc1 + w0 * dc2
        d3_ref[:, 0:D_MODEL] = (dcbv * conv).astype(d3_ref.dtype)
        d3_ref[:, D_MODEL:2 * D_MODEL] = (du * cx_ref[...].astype(F32)).astype(d3_ref.dtype)
        d3_ref[:, 2 * D_MODEL:3 * D_MODEL] = (du * gc_ref[...].astype(F32)).astype(d3_ref.dtype)
        dwb_ref[0:1, :] += jnp.sum(dconv * u2, axis=0, keepdims=True)
        dwb_ref[1:2, :] += jnp.sum(dconv * u1, axis=0, keepdims=True)
        dwb_ref[2:3, :] += jnp.sum(dconv * u, axis=0, keepdims=True)
        dwb_ref[3:4, :] += jnp.sum(dconv, axis=0, keepdims=True)

    def next_halo(seg_fn):
        return pl.BlockSpec((HALO, D_MODEL), lambda i: (jnp.minimum((i + 1) * (tm // HALO), s // HALO - 1), seg_fn))

    return pl.pallas_call(
        body, grid=(nt,),
        in_specs=[pl.BlockSpec((tm, D_MODEL), lambda i: (i, 0)),
                  _seg(tm, SEG_GBI), _seg(tm, SEG_GCI), _seg(tm, SEG_CX),
                  _prev_halo(tm, SEG_GCI), _prev_halo(tm, SEG_CX),
                  next_halo(0), next_halo(SEG_GBI),
                  pl.BlockSpec((3, D_MODEL), lambda i: (0, 0)), pl.BlockSpec((1, D_MODEL), lambda i: (0, 0))],
        out_specs=[pl.BlockSpec((tm, 3 * D_MODEL), lambda i: (i, 0)), pl.BlockSpec((8, D_MODEL), lambda i: (0, 0))],
        out_shape=[jax.ShapeDtypeStruct((s, 3 * D_MODEL), BF16), jax.ShapeDtypeStruct((8, D_MODEL), F32)],
        name=name, compiler_params=_cparams(("arbitrary",)),
    )(dcb, proj, proj, proj, proj, proj, dcb, proj, w, b)


def _mix_fwd(proj, ya, yb, name):
    s = proj.shape[0]
    tm = _pick(s, (512, 256))

    def body(ga_ref, gb_ref, ya_ref, yb_ref, o_ref):
        o_ref[...] = (_sigmoid(ga_ref[...].astype(F32)) * ya_ref[...].astype(F32)
                      + _sigmoid(gb_ref[...].astype(F32)) * yb_ref[...].astype(F32)).astype(o_ref.dtype)

    row = pl.BlockSpec((tm, D_MODEL), lambda i: (i, 0))
    return pl.pallas_call(
        body, grid=(s // tm,), in_specs=[_seg(tm, SEG_GA), _seg(tm, SEG_GB), row, row], out_specs=row,
        out_shape=jax.ShapeDtypeStruct((s, D_MODEL), BF16), name=name, compiler_params=_cparams(("parallel",)),
    )(proj, proj, ya, yb)


def _mix_bwd(dmix, proj, ya, yb, name):
    s = proj.shape[0]
    tm = _pick(s, (512, 256))

    def body(dm_ref, ga_ref, gb_ref, ya_ref, yb_ref, dg_ref, dya_ref, dyb_ref):
        dm = dm_ref[...].astype(F32)
        sa = _sigmoid(ga_ref[...].astype(F32))
        sb = _sigmoid(gb_ref[...].astype(F32))
        dg_ref[:, 0:D_MODEL] = (dm * ya_ref[...].astype(F32) * sa * (1.0 - sa)).astype(dg_ref.dtype)
        dg_ref[:, D_MODEL:2 * D_MODEL] = (dm * yb_ref[...].astype(F32) * sb * (1.0 - sb)).astype(dg_ref.dtype)
        dya_ref[...] = (dm * sa).astype(dya_ref.dtype)
        dyb_ref[...] = (dm * sb).astype(dyb_ref.dtype)

    row = pl.BlockSpec((tm, D_MODEL), lambda i: (i, 0))
    return pl.pallas_call(
        body, grid=(s // tm,), in_specs=[row, _seg(tm, SEG_GA), _seg(tm, SEG_GB), row, row],
        out_specs=[pl.BlockSpec((tm, 2 * D_MODEL), lambda i: (i, 0)), row, row],
        out_shape=[jax.ShapeDtypeStruct((s, 2 * D_MODEL), BF16), jax.ShapeDtypeStruct((s, D_MODEL), BF16),
                   jax.ShapeDtypeStruct((s, D_MODEL), BF16)],
        name=name, compiler_params=_cparams(("parallel",)),
    )(dmix, proj, proj, ya, yb)


def _swiglu_fwd(gu, name):
    s = gu.shape[0]
    tm = _pick(s, (256,))

    def body(gu_ref, o_ref):
        gate = gu_ref[:, 0:FFN].astype(F32)
        o_ref[...] = (gate * _sigmoid(gate) * gu_ref[:, FFN:2 * FFN].astype(F32)).astype(o_ref.dtype)

    return pl.pallas_call(
        body, grid=(s // tm,), in_specs=[pl.BlockSpec((tm, 2 * FFN), lambda i: (i, 0))],
        out_specs=pl.BlockSpec((tm, FFN), lambda i: (i, 0)),
        out_shape=jax.ShapeDtypeStruct((s, FFN), BF16), name=name, compiler_params=_cparams(("parallel",)),
    )(gu)


def _swiglu_bwd(dhid, gu, name):
    s = gu.shape[0]
    tm = _pick(s, (256,))

    def body(dh_ref, gu_ref, o_ref):
        gate = gu_ref[:, 0:FFN].astype(F32)
        up = gu_ref[:, FFN:2 * FFN].astype(F32)
        dh = dh_ref[...].astype(F32)
        sg = _sigmoid(gate)
        o_ref[:, 0:FFN] = (dh * up * (sg * (1.0 + gate * (1.0 - sg)))).astype(o_ref.dtype)
        o_ref[:, FFN:2 * FFN] = (dh * gate * sg).astype(o_ref.dtype)

    wide = pl.BlockSpec((tm, 2 * FFN), lambda i: (i, 0))
    return pl.pallas_call(
        body, grid=(s // tm,), in_specs=[pl.BlockSpec((tm, FFN), lambda i: (i, 0)), wide], out_specs=wide,
        out_shape=jax.ShapeDtypeStruct((s, 2 * FFN), BF16), name=name, compiler_params=_cparams(("parallel",)),
    )(dhid, gu)


def _adamw_math(w, g, m, v):
    m2 = ADAM_B1 * m + (1.0 - ADAM_B1) * g
    v2 = ADAM_B2 * v + (1.0 - ADAM_B2) * (g * g)
    m_hat = m2 / (1.0 - ADAM_B1 ** ADAM_STEP)
    v_hat = v2 / (1.0 - ADAM_B2 ** ADAM_STEP)
    delta = -ADAM_LR * (m_hat / (jnp.sqrt(v_hat) + ADAM_EPS) + ADAM_WD * w)
    return delta, m2, v2


def _adamw(w, g, m, v, name):
    shape = w.shape
    cols = shape[-1]
    rows = int(np.prod(shape[:-1])) if len(shape) > 1 else 1
    w2, g2, m2, v2 = [t.reshape(rows, cols) for t in (w, g, m, v)]
    tr = _pick(rows, (512, 352, 256)) if rows % 8 == 0 else rows

    def body(w_ref, g_ref, m_ref, v_ref, d_ref, nm_ref, nv_ref):
        d, nm, nv = _adamw_math(w_ref[...], g_ref[...], m_ref[...], v_ref[...])
        d_ref[...] = d
        nm_ref[...] = nm
        nv_ref[...] = nv

    blk = pl.BlockSpec((tr, cols), lambda i: (i, 0))
    out = pl.pallas_call(
        body, grid=(rows // tr,), in_specs=[blk] * 4, out_specs=[blk] * 3,
        out_shape=[jax.ShapeDtypeStruct((rows, cols), F32)] * 3, name=name,
        compiler_params=_cparams(("parallel",)),
    )(w2, g2, m2, v2)
    return [t.reshape(shape) for t in out]


def _pair_sum(g2, recv, core, name):
    _, nchip, r, c = g2.shape
    tr = _pick(r, (512,))

    def body(core_ref, a_ref, b_ref, o_ref):
        o_ref[...] = (a_ref[...].astype(F32) + b_ref[...].astype(F32)).astype(o_ref.dtype)

    grid_spec = pltpu.PrefetchScalarGridSpec(
        num_scalar_prefetch=1, grid=(nchip, r // tr),
        in_specs=[pl.BlockSpec((None, None, tr, c), lambda k, i, cr: (cr[0], k, i, 0)),
                  pl.BlockSpec((None, tr, c), lambda k, i, cr: (k, i, 0))],
        out_specs=pl.BlockSpec((None, tr, c), lambda k, i, cr: (k, i, 0)))
    return pl.pallas_call(
        body, grid_spec=grid_spec, out_shape=jax.ShapeDtypeStruct((nchip, r, c), BF16), name=name,
        compiler_params=_cparams(("parallel", "parallel")),
    )(core, g2, recv)


def _chip_sum(a, recv, chip, name):
    _, r, c = a.shape
    tr = _pick(r, (512,))

    def body(chip_ref, a_ref, b_ref, o_ref):
        o_ref[...] = ((a_ref[...].astype(F32) + b_ref[0].astype(F32)) + b_ref[1].astype(F32)) + b_ref[2].astype(F32)

    grid_spec = pltpu.PrefetchScalarGridSpec(
        num_scalar_prefetch=1, grid=(r // tr,),
        in_specs=[pl.BlockSpec((None, tr, c), lambda i, cr: (cr[0], i, 0)),
                  pl.BlockSpec((3, tr, c), lambda i, cr: (0, i, 0))],
        out_specs=pl.BlockSpec((tr, c), lambda i, cr: (i, 0)))
    return pl.pallas_call(
        body, grid_spec=grid_spec, out_shape=jax.ShapeDtypeStruct((r, c), F32), name=name,
        compiler_params=_cparams(("parallel",)),
    )(chip, a, recv)


def _sum_devices(parts, name):
    n, r, c = parts.shape

    def body(p_ref, o_ref):
        acc = p_ref[0]
        for d in range(1, n):
            acc = acc + p_ref[d]
        o_ref[...] = acc

    return pl.pallas_call(
        body, out_shape=jax.ShapeDtypeStruct((r, c), F32), name=name,
        in_specs=[pl.BlockSpec(memory_space=pltpu.VMEM)], out_specs=pl.BlockSpec(memory_space=pltpu.VMEM),
    )(parts)


def _lane_iota():
    return lax.broadcasted_iota(jnp.int32, (1, LANES), 1)


def _tiles_up(tiles, s, lane):
    rolled = [pltpu.roll(t, s, 1) for t in tiles]
    zero = jnp.zeros_like(tiles[0])
    return [jnp.where(lane < s, p, c) for p, c in zip([zero] + rolled, rolled + [zero])]


def _tiles_down(tiles, s, lane):
    back = (LANES - s) % LANES
    rolled = [pltpu.roll(t, back, 1) for t in tiles]
    zero = jnp.zeros_like(tiles[0])
    return [jnp.where(lane < LANES - s, c, n) for c, n in zip(rolled, rolled[1:] + [zero])]


def _window_cols(prm, w_in, w_gate, w_up, l, name):
    tm = 256
    n_in, n_gu = WIN_IN // LANES, FFN_SHARD // LANES + 1

    def body(prm_ref, win_ref, g_ref, u_ref, out_ref, scr_in, scr_gu):
        lane = _lane_iota()
        s_main, lo, hi, s_code, clo, chi, code_hi, s_gu = [prm_ref[i] for i in range(8)]
        scr_in[:, D_MODEL:WIN_IN] = jnp.zeros((tm, WIN_IN - D_MODEL), F32)
        scr_in[:, 0:W_IN_SHARD] = win_ref[...]

        def keep(t, a, b):
            col = lane + t * LANES
            return jnp.where((col >= a) & (col < b), scr_in[:, t * LANES:(t + 1) * LANES], 0.0)

        main = _tiles_up([keep(t, lo, hi) for t in range(n_in)], s_main, lane)
        for t in range(n_in):
            out_ref[:, t * LANES:(t + 1) * LANES] = main[t].astype(BF16)
        low = _tiles_up([keep(0, clo, chi)], s_code, lane)[0]
        high = _tiles_up([keep(n_in - 2, clo, chi), keep(n_in - 1, clo, chi)], s_code, lane)[1]
        out_ref[:, A_FZ:A_FZ + LANES] = jnp.where(code_hi == 1, high, low).astype(BF16)
        for ref, base in ((g_ref, A_GATE), (u_ref, A_UP)):
            scr_gu[:, (n_gu - 1) * LANES:n_gu * LANES] = jnp.zeros((tm, LANES), F32)
            scr_gu[:, 0:FFN_SHARD] = ref[...]
            moved = _tiles_up([scr_gu[:, t * LANES:(t + 1) * LANES] for t in range(n_gu)], s_gu, lane)
            for t in range(n_gu + 1):
                out_ref[:, base + t * LANES:base + (t + 1) * LANES] = moved[t].astype(BF16)

    grid_spec = pltpu.PrefetchScalarGridSpec(
        num_scalar_prefetch=1, grid=(D_MODEL // tm,),
        in_specs=[pl.BlockSpec((None, tm, W_IN_SHARD), lambda i, p: (l, i, 0)),
                  pl.BlockSpec((None, tm, FFN_SHARD), lambda i, p: (l, i, 0)),
                  pl.BlockSpec((None, tm, FFN_SHARD), lambda i, p: (l, i, 0))],
        out_specs=pl.BlockSpec((tm, A_COLS), lambda i, p: (i, 0)),
        scratch_shapes=[pltpu.VMEM((tm, WIN_IN), F32), pltpu.VMEM((tm, n_gu * LANES), F32)])
    return pl.pallas_call(
        body, grid_spec=grid_spec, out_shape=jax.ShapeDtypeStruct((D_MODEL, A_COLS), BF16), name=name,
        compiler_params=_cparams(("parallel",)),
    )(prm, w_in, w_gate, w_up)


def _gu_width(j):
    return min(WIN_GU, FFN - T_GU[j] * LANES)


def _assemble_cols(a_all, name):
    tm = 128

    def body(a_ref, win_ref, wgu_ref):
        win_ref[...] = jnp.zeros_like(win_ref)
        wgu_ref[...] = jnp.zeros_like(wgu_ref)
        code = a_ref[0, :, A_FZ:A_FZ + LANES]
        for j in range(N_DEV):
            c0 = T_IN[j] * LANES
            win_ref[:, c0:c0 + WIN_IN] += a_ref[j, :, 0:WIN_IN]
            g0, width = T_GU[j] * LANES, _gu_width(j)
            wgu_ref[:, g0:g0 + width] += a_ref[j, :, A_GATE:A_GATE + width]
            wgu_ref[:, FFN + g0:FFN + g0 + width] += a_ref[j, :, A_UP:A_UP + width]
            if j > 0:
                code = code + a_ref[j, :, A_FZ:A_FZ + LANES]
        win_ref[:, FZ_COL:PROJ_W] = code

    return pl.pallas_call(
        body, grid=(D_MODEL // tm,), in_specs=[pl.BlockSpec((N_DEV, tm, A_COLS), lambda i: (0, i, 0))],
        out_specs=[pl.BlockSpec((tm, PROJ_W), lambda i: (i, 0)), pl.BlockSpec((tm, 2 * FFN), lambda i: (i, 0))],
        out_shape=[jax.ShapeDtypeStruct((D_MODEL, PROJ_W), BF16), jax.ShapeDtypeStruct((D_MODEL, 2 * FFN), BF16)],
        name=name, compiler_params=_cparams(("parallel",)),
    )(a_all)


def _grad_windows(d_in, d_gu, name):
    tm = 128

    def body(din_ref, dgu_ref, out_ref):
        for j in range(N_DEV):
            o = out_ref.at[j & 1, j >> 1]
            c0 = T_IN[j] * LANES
            o[:, 0:WIN_IN] = din_ref[:, c0:c0 + WIN_IN]
            o[:, A_FZ:A_FZ + LANES] = din_ref[:, FZ_COL:PROJ_W]
            g0, width = T_GU[j] * LANES, _gu_width(j)
            o[:, A_GATE:A_GATE + width] = dgu_ref[:, g0:g0 + width]
            o[:, A_UP:A_UP + width] = dgu_ref[:, FFN + g0:FFN + g0 + width]
            if width < WIN_GU:
                o[:, A_GATE + width:A_UP] = jnp.zeros((tm, WIN_GU - width), BF16)
                o[:, A_UP + width:A_COLS] = jnp.zeros((tm, WIN_GU - width), BF16)

    return pl.pallas_call(
        body, grid=(D_MODEL // tm,),
        in_specs=[pl.BlockSpec((tm, PROJ_W), lambda i: (i, 0)), pl.BlockSpec((tm, 2 * FFN), lambda i: (i, 0))],
        out_specs=pl.BlockSpec((2, 4, tm, A_COLS), lambda i: (0, 0, i, 0)),
        out_shape=jax.ShapeDtypeStruct((2, 4, D_MODEL, A_COLS), BF16), name=name,
        compiler_params=_cparams(("parallel",)),
    )(d_in, d_gu)


def _final_cols(prm, chip, a, recv, params, l, name):
    tm = 128
    n_in, n_gu = WIN_IN // LANES, FFN_SHARD // LANES + 1
    widths = (W_IN_SHARD, FFN_SHARD, FFN_SHARD)

    def body(prm_ref, chip_ref, a_ref, r_ref, *refs):
        ins, outs = refs[:9], refs[9:]
        lane = _lane_iota()
        s_main, s_code, clo, chi, s_gu = [prm_ref[i] for i in (0, 3, 4, 5, 7)]

        def total(c0):
            sl = slice(c0, c0 + LANES)
            return ((a_ref[:, sl].astype(F32) + r_ref[0, :, sl].astype(F32)) + r_ref[1, :, sl].astype(F32)) \
                + r_ref[2, :, sl].astype(F32)

        grads = _tiles_down([total(t * LANES) for t in range(n_in)], s_main, lane)
        code = pltpu.roll(total(A_FZ), (LANES - s_code) % LANES, 1)
        for t in (0, n_in - 2, n_in - 1):
            col = lane + t * LANES
            grads[t] = jnp.where((col >= clo) & (col < chi), code, grads[t])
        per_weight = [grads]
        for base in (A_GATE, A_UP):
            per_weight.append(_tiles_down([total(base + t * LANES) for t in range(n_gu + 1)], s_gu, lane)[:n_gu])
        for k, (tiles, width) in enumerate(zip(per_weight, widths)):
            w_ref, m_ref, v_ref = ins[3 * k:3 * k + 3]
            g_ref, d_ref, nm_ref, nv_ref = outs[4 * k:4 * k + 4]
            for t, g in enumerate(tiles):
                n = min(LANES, width - t * LANES)
                sl = slice(t * LANES, t * LANES + n)
                g = g[:, 0:n]
                d, nm, nv = _adamw_math(w_ref[:, sl], g, m_ref[:, sl], v_ref[:, sl])
                g_ref[:, sl] = g
                d_ref[:, sl] = d
                nm_ref[:, sl] = nm
                nv_ref[:, sl] = nv

    def native(width):
        return pl.BlockSpec((None, tm, width), lambda i, p, c: (l, i, 0))

    def native_out(width):
        return pl.BlockSpec((tm, width), lambda i, p, c: (i, 0))

    in_specs = [pl.BlockSpec((None, tm, A_COLS), lambda i, p, c: (c[0], i, 0)),
                pl.BlockSpec((3, tm, A_COLS), lambda i, p, c: (0, i, 0))]
    in_specs += [native(wd) for wd in widths for _ in range(3)]
    out_specs = [native_out(wd) for wd in widths for _ in range(4)]
    out_shape = [jax.ShapeDtypeStruct((D_MODEL, wd), F32) for wd in widths for _ in range(4)]
    grid_spec = pltpu.PrefetchScalarGridSpec(num_scalar_prefetch=2, grid=(D_MODEL // tm,), in_specs=in_specs,
                                             out_specs=out_specs)
    out = pl.pallas_call(body, grid_spec=grid_spec, out_shape=out_shape, name=name,
                         compiler_params=_cparams(("parallel",)))(prm, chip, a, recv, *params)
    return [out[4 * k:4 * k + 4] for k in range(3)]


def _me():
    return lax.axis_index("x"), lax.axis_index("y"), lax.axis_index("c")


_CHIP_FLIPS = ((1, 0), (0, 1), (1, 1))
_ANY = pl.BlockSpec(memory_space=pl.ANY)


def _all_gather(shards, name):
    n = len(shards)

    def body(*refs):
        x_refs, out_refs = refs[:n], refs[n:2 * n]
        send_sems, recv_sems, local_sems = refs[2 * n:]
        x, y, cc = _me()
        sibling = (x, y, 1 - cc)
        chips = [(x ^ fx, y ^ fy) for fx, fy in _CHIP_FLIPS]

        def copy(a, k, block, to, from_shard=False):
            px, py, pc = block
            slot = out_refs[a].at[4 * px + 2 * py + pc]
            return pltpu.make_async_remote_copy(
                src_ref=x_refs[a] if from_shard else slot, dst_ref=slot,
                send_sem=send_sems.at[a, k], recv_sem=recv_sems.at[a, k], device_id=to, device_id_type=MESH)

        mine = [pltpu.make_async_copy(x_refs[a], out_refs[a].at[4 * x + 2 * y + cc], local_sems.at[a])
                for a in range(n)]
        for cp in mine:
            cp.start()
        first = [copy(a, 0, (x, y, cc), sibling, True) for a in range(n)]
        first += [copy(a, 1 + j, (x, y, cc), (*chip, cc), True) for j, chip in enumerate(chips) for a in range(n)]
        for cp in first:
            cp.start()
        passed = []
        for j, chip in enumerate(chips):
            for a in range(n):
                copy(a, 1 + j, (*chip, cc), (x, y, cc)).wait_recv()
                passed.append(copy(a, 4 + j, (*chip, cc), sibling))
                passed[-1].start()
        for a in range(n):
            copy(a, 0, sibling, (x, y, cc)).wait_recv()
            for j, chip in enumerate(chips):
                copy(a, 4 + j, (*chip, 1 - cc), (x, y, cc)).wait_recv()
        for cp in first + passed:
            cp.wait_send()
        for cp in mine:
            cp.wait()

    return pl.pallas_call(
        body, out_shape=[jax.ShapeDtypeStruct((N_DEV,) + s.shape, s.dtype) for s in shards], name=name,
        in_specs=[_ANY] * n, out_specs=[_ANY] * n,
        scratch_shapes=[pltpu.SemaphoreType.DMA((n, 7)), pltpu.SemaphoreType.DMA((n, 7)),
                        pltpu.SemaphoreType.DMA((n,))],
    )(*shards)


def _send_to_sibling(parts, name):
    n = len(parts)

    def body(*refs):
        g_refs, out_refs = refs[:n], refs[n:2 * n]
        send_sems, recv_sems = refs[2 * n:]
        x, y, cc = _me()
        copies = [pltpu.make_async_remote_copy(
            src_ref=g_refs[a].at[1 - cc], dst_ref=out_refs[a], send_sem=send_sems.at[a], recv_sem=recv_sems.at[a],
            device_id=(x, y, 1 - cc), device_id_type=MESH) for a in range(n)]
        for cp in copies:
            cp.start()
        for cp in copies:
            cp.wait()

    return pl.pallas_call(
        body, out_shape=[jax.ShapeDtypeStruct(p.shape[1:], p.dtype) for p in parts], name=name,
        in_specs=[_ANY] * n, out_specs=[_ANY] * n,
        scratch_shapes=[pltpu.SemaphoreType.DMA((n,)), pltpu.SemaphoreType.DMA((n,))],
    )(*parts)


def _send_to_chips(parts, name):
    n = len(parts)

    def body(*refs):
        a_refs, out_refs = refs[:n], refs[n:2 * n]
        send_sems, recv_sems = refs[2 * n:]
        x, y, cc = _me()
        copies = []
        for k, (fx, fy) in enumerate(_CHIP_FLIPS):
            px, py = x ^ fx, y ^ fy
            for a in range(n):
                copies.append(pltpu.make_async_remote_copy(
                    src_ref=a_refs[a].at[2 * px + py], dst_ref=out_refs[a].at[k], send_sem=send_sems.at[a, k],
                    recv_sem=recv_sems.at[a, k], device_id=(px, py, cc), device_id_type=MESH))
                copies[-1].start()
        for cp in copies:
            cp.wait()

    return pl.pallas_call(
        body, out_shape=[jax.ShapeDtypeStruct((3,) + p.shape[1:], p.dtype) for p in parts], name=name,
        in_specs=[_ANY] * n, out_specs=[_ANY] * n,
        scratch_shapes=[pltpu.SemaphoreType.DMA((n, 3)), pltpu.SemaphoreType.DMA((n, 3))],
    )(*parts)


def _pack_rows(w_oa, w_ob, w_o, w_down, w_fg2, conv_w, l):
    conv_bits = lax.bitcast_convert_type(conv_w[l].reshape(-1), BF16).reshape(1, -1)
    tail = jnp.concatenate([w_fg2[l].astype(BF16).reshape(1, D_MODEL),
                            jnp.pad(conv_bits, ((0, 0), (0, D_MODEL - conv_bits.shape[1])))], axis=0)
    tail = jnp.pad(tail, ((0, B_ROWS - B_FG2 - tail.shape[0]), (0, 0)))
    return jnp.concatenate([w_oa[l].astype(BF16), w_ob[l].astype(BF16), w_o[l].astype(BF16),
                            w_down[l].astype(BF16), tail], axis=0)


def _unpack_rows(gathered):
    w_fg2 = gathered[:, B_FG2, :].reshape(N_DEV, RANK, KEY_W // N_DEV).transpose(1, 0, 2).reshape(RANK, KEY_W)
    conv_bits = gathered[:, B_CONV, :2 * 3 * ROW_SHARD].reshape(N_DEV, 3 * ROW_SHARD, 2)
    conv_w = lax.bitcast_convert_type(conv_bits, F32).reshape(N_DEV, 3, ROW_SHARD)
    return dict(
        w_oa=gathered[:, B_OA:B_OA + ROW_SHARD].reshape(D_MODEL, D_MODEL),
        w_ob=gathered[:, B_OB:B_OB + ROW_SHARD].reshape(D_MODEL, D_MODEL),
        w_o=gathered[:, B_O:B_O + ROW_SHARD].reshape(D_MODEL, D_MODEL),
        w_down=gathered[:, B_DOWN:B_DOWN + FFN_SHARD].reshape(FFN, D_MODEL),
        w_fg2=jnp.pad(w_fg2, ((0, LANES - RANK), (0, 0))),
        conv_w=conv_w.transpose(1, 0, 2).reshape(3, D_MODEL))


def _by_core_chip(t):
    return t.reshape((2, 2, 2) + t.shape[1:]).transpose((2, 0, 1) + tuple(range(3, t.ndim + 2))).reshape(
        (2, 4) + t.shape[1:])


def _grad_rows(g):
    fg2 = g["w_fg2"][:RANK].reshape(RANK, N_DEV, KEY_W // N_DEV).transpose(1, 0, 2).reshape(N_DEV, 1, D_MODEL)
    conv = g["conv_w"].astype(BF16).reshape(3, N_DEV, ROW_SHARD).transpose(1, 0, 2).reshape(N_DEV, 1, 3 * ROW_SHARD)
    tail = jnp.concatenate([fg2, jnp.pad(conv, ((0, 0), (0, 0), (0, D_MODEL - 3 * ROW_SHARD)))], axis=1)
    tail = jnp.pad(tail, ((0, 0), (0, B_ROWS - B_FG2 - 2), (0, 0)))
    parts = [g["w_oa"].reshape(N_DEV, ROW_SHARD, D_MODEL), g["w_ob"].reshape(N_DEV, ROW_SHARD, D_MODEL),
             g["w_o"].reshape(N_DEV, ROW_SHARD, D_MODEL), g["w_down"].reshape(N_DEV, FFN_SHARD, D_MODEL), tail]
    return _by_core_chip(jnp.concatenate(parts, axis=1))


def _ungrad_rows(gs):
    return dict(w_oa=gs[B_OA:B_OA + ROW_SHARD], w_ob=gs[B_OB:B_OB + ROW_SHARD], w_o=gs[B_O:B_O + ROW_SHARD],
                w_ffn_down=gs[B_DOWN:B_DOWN + FFN_SHARD], w_fg2=gs[B_FG2].reshape(RANK, KEY_W // N_DEV),
                conv_w=gs[B_CONV, :3 * ROW_SHARD].reshape(3, ROW_SHARD))


def _layer_fwd(x, p, l):
    tag = f"l{l}_"
    h = _rmsnorm_fwd(x, p["norm1_g"], tag + "norm1")
    proj = _matmul(h, p["w_in"], "nn", BF16, tag + "proj")
    o, states = _gla_fwd(proj, p["w_fg2"], p["b_fg"], tag + "gla_fwd")
    oa = _gla_post_fwd(o, proj, p["gla_norm_g"], tag + "gla_post")
    ya = _matmul(oa, p["w_oa"], "nn", BF16, tag + "ya")
    cb = _conv_fwd(proj, p["conv_w"], p["conv_b"], tag + "conv")
    yb = _matmul(cb, p["w_ob"], "nn", BF16, tag + "yb")
    mix = _mix_fwd(proj, ya, yb, tag + "mix")
    x1 = _matmul(mix, p["w_o"], "nn", F32, tag + "x1", residual=x)
    h2 = _rmsnorm_fwd(x1, p["norm2_g"], tag + "norm2")
    gu = _matmul(h2, p["w_gu"], "nn", BF16, tag + "gu")
    hid = _swiglu_fwd(gu, tag + "swiglu")
    x2 = _matmul(hid, p["w_down"], "nn", F32, tag + "x2", residual=x1)
    saved = dict(x=x, h=h, proj=proj, o=o, states=states, oa=oa, ya=ya, cb=cb, yb=yb, mix=mix, x1=x1, h2=h2,
                 gu=gu, hid=hid)
    return x2, saved


def _layer_bwd(dx2, p, sv, l):
    tag = f"l{l}_b_"
    dx2h = dx2.astype(BF16)
    dhid = _matmul(dx2h, p["w_down"], "nt", BF16, tag + "dhid")
    d_down = _matmul(sv["hid"], dx2h, "tn", BF16, tag + "dw_down")
    dgu = _swiglu_bwd(dhid, sv["gu"], tag + "swiglu")
    dh2 = _matmul(dgu, p["w_gu"], "nt", F32, tag + "dh2")
    d_gu = _matmul(sv["h2"], dgu, "tn", BF16, tag + "dw_gu")
    dx1, dg2 = _rmsnorm_bwd(sv["x1"], p["norm2_g"], dh2, dx2, tag + "norm2")
    dx1h = dx1.astype(BF16)
    dmix = _matmul(dx1h, p["w_o"], "nt", BF16, tag + "dmix")
    d_o = _matmul(sv["mix"], dx1h, "tn", BF16, tag + "dw_o")
    dgab, dya, dyb = _mix_bwd(dmix, sv["proj"], sv["ya"], sv["yb"], tag + "mix")
    dcb = _matmul(dyb, p["w_ob"], "nt", BF16, tag + "dcb")
    d_ob = _matmul(sv["cb"], dyb, "tn", BF16, tag + "dw_ob")
    d3, dwb = _conv_bwd(dcb, sv["proj"], p["conv_w"], p["conv_b"], tag + "conv")
    doa = _matmul(dya, p["w_oa"], "nt", BF16, tag + "doa")
    d_oa = _matmul(sv["oa"], dya, "tn", BF16, tag + "dw_oa")
    dr, do, dgg = _gla_post_bwd(doa, sv["o"], sv["proj"], p["gla_norm_g"], tag + "gla_post")
    dq, dk, dv, dfg, dbfg = _gla_bwd(sv["proj"], p["w_fg2"], p["b_fg"], sv["states"], do, tag + "gla")
    fz = sv["proj"][:, FZ_COL:]
    dfz = _matmul(dfg, p["w_fg2"], "nt", BF16, tag + "dfz")
    d_fg2 = _matmul(fz, dfg, "tn", BF16, tag + "dw_fg2")
    dproj = jnp.concatenate([dq, dk, dv, dr, d3, dgab, dfz], axis=1)
    dh = _matmul(dproj, p["w_in"], "nt", F32, tag + "dh")
    d_in = _matmul(sv["h"], dproj, "tn", BF16, tag + "dw_in")
    dx, dg1 = _rmsnorm_bwd(sv["x"], p["norm1_g"], dh, dx1, tag + "norm1")
    big = dict(w_in=d_in, w_fg2=d_fg2, conv_w=dwb[0:3], w_oa=d_oa, w_ob=d_ob, w_o=d_o, w_gu=d_gu, w_down=d_down)
    pad = lambda t: jnp.pad(t, ((0, 0), (0, D_MODEL - t.shape[1])))
    small = [dg1[0:1], pad(dbfg[0:1]), pad(dgg[0:1]), dwb[3:4], dg2[0:1]]
    return dx, big, small


def _local_step(x, target, layers, final_g):
    saved = []
    for l, p in enumerate(layers):
        x, sv = _layer_fwd(x, p, l)
        saved.append(sv)
    loss, dx, dgf = _loss_head(x, final_g, target, "loss_head")
    bigs, smalls = [None] * DEPTH, [None] * DEPTH
    for l in reversed(range(DEPTH)):
        dx, bigs[l], smalls[l] = _layer_bwd(dx, layers[l], saved[l], l)
    small = jnp.concatenate(smalls[0] + smalls[1] + [dgf[0:1]], axis=0)
    small = jnp.pad(small, ((0, SMALL_ROWS - small.shape[0]), (0, 0)))
    return loss[0, 0], dx, bigs, small


def kernel(x, norm1_g, w_in, w_fg2, b_fg, gla_norm_g, w_oa, conv_w, conv_b, w_ob, w_o, norm2_g, w_ffn_gate, w_ffn_up, w_ffn_down, final_g, loss_target, m_norm1_g, m_w_in, m_w_fg2, m_b_fg, m_gla_norm_g, m_w_oa, m_conv_w, m_conv_b, m_w_ob, m_w_o, m_norm2_g, m_w_ffn_gate, m_w_ffn_up, m_w_ffn_down, m_final_g, v_norm1_g, v_w_in, v_w_fg2, v_b_fg, v_gla_norm_g, v_w_oa, v_conv_w, v_conv_b, v_w_ob, v_w_o, v_norm2_g, v_w_ffn_gate, v_w_ffn_up, v_w_ffn_down, v_final_g):
    names = ["norm1_g", "w_in", "w_fg2", "b_fg", "gla_norm_g", "w_oa", "conv_w", "conv_b", "w_ob", "w_o",
             "norm2_g", "w_ffn_gate", "w_ffn_up", "w_ffn_down", "final_g"]
    w = dict(zip(names, [norm1_g, w_in, w_fg2, b_fg, gla_norm_g, w_oa, conv_w, conv_b, w_ob, w_o, norm2_g,
                         w_ffn_gate, w_ffn_up, w_ffn_down, final_g]))
    m = dict(zip(names, [m_norm1_g, m_w_in, m_w_fg2, m_b_fg, m_gla_norm_g, m_w_oa, m_conv_w, m_conv_b, m_w_ob,
                         m_w_o, m_norm2_g, m_w_ffn_gate, m_w_ffn_up, m_w_ffn_down, m_final_g]))
    v = dict(zip(names, [v_norm1_g, v_w_in, v_w_fg2, v_b_fg, v_gla_norm_g, v_w_oa, v_conv_w, v_conv_b, v_w_ob,
                         v_w_o, v_norm2_g, v_w_ffn_gate, v_w_ffn_up, v_w_ffn_down, v_final_g]))
    col_names = ["w_in", "w_ffn_gate", "w_ffn_up"]
    cx, cy, cc = _me()
    prm = jnp.asarray(SHIFT_TABLE)[4 * cx + 2 * cy + cc]
    core = jnp.reshape(cc, (1,)).astype(jnp.int32)
    chip = jnp.reshape(2 * cx + cy, (1,)).astype(jnp.int32)

    layers = []
    for l in range(DEPTH):
        windows = _window_cols(prm, w_in, w_ffn_gate, w_ffn_up, l, f"l{l}_windows")
        rows = _pack_rows(w_oa, w_ob, w_o, w_ffn_down, w_fg2, conv_w, l)
        all_windows, all_rows = _all_gather([windows, rows], f"l{l}_gather")
        p = _unpack_rows(all_rows)
        p["w_in"], p["w_gu"] = _assemble_cols(all_windows, f"l{l}_assemble")
        p.update(norm1_g=norm1_g[l][None], b_fg=b_fg[l][None], gla_norm_g=gla_norm_g[l][None],
                 conv_b=conv_b[l][None], norm2_g=norm2_g[l][None])
        layers.append(p)

    loss, dx, bigs, small = _local_step(x[0], loss_target[0], layers, final_g[None])

    col_params = [t for n in col_names for t in (w[n], m[n], v[n])]
    row_grads, col_out = [], []
    for l in range(DEPTH):
        parts = [_grad_windows(bigs[l]["w_in"], bigs[l]["w_gu"], f"l{l}_grad_windows"), _grad_rows(bigs[l])]
        from_sibling = _send_to_sibling(parts, f"l{l}_grads_to_sibling")
        sums = [_pair_sum(part, got, core, f"l{l}_pair_sum{k}")
                for k, (part, got) in enumerate(zip(parts, from_sibling))]
        from_chips = _send_to_chips(sums, f"l{l}_grads_to_chips")
        row_grads.append(_ungrad_rows(_chip_sum(sums[1], from_chips[1], chip, f"l{l}_chip_sum")))
        col_out.append(_final_cols(prm, chip, sums[0], from_chips[0], col_params, l, f"l{l}_final_cols"))

    grads, deltas, new_m, new_v = {}, {}, {}, {}
    for k, n in enumerate(col_names):
        grads[n], deltas[n], new_m[n], new_v[n] = [jnp.stack([col_out[l][k][q] for l in range(DEPTH)])
                                                   for q in range(4)]
    for n in row_grads[0]:
        grads[n] = jnp.stack([row_grads[l][n] for l in range(DEPTH)])

    small_sum = _sum_devices(_all_gather([small], "gather_small")[0], "sum_small")
    r512, r256 = slice(0, KEY_W), slice(0, HEAD_V)
    grads.update(
        norm1_g=jnp.stack([small_sum[0], small_sum[5]]), b_fg=jnp.stack([small_sum[1, r512], small_sum[6, r512]]),
        gla_norm_g=jnp.stack([small_sum[2, r256], small_sum[7, r256]]),
        conv_b=jnp.stack([small_sum[3], small_sum[8]]), norm2_g=jnp.stack([small_sum[4], small_sum[9]]),
        final_g=small_sum[10])

    for n in names:
        if n not in col_names:
            deltas[n], new_m[n], new_v[n] = _adamw(w[n], grads[n], m[n], v[n], "adamw_" + n)

    total_loss = lax.psum(loss, ("x", "y", "c"))
    return (total_loss, dx[None], *[grads[n] for n in names], *[deltas[n] for n in names],
            *[new_m[n] for n in names], *[new_v[n] for n in names])
```

```python
import functools

import jax
import jax.numpy as jnp
import numpy as np
from jax import lax
from jax.experimental import pallas as pl
from jax.experimental.pallas import tpu as pltpu
from jax.experimental.pallas import tpu_sc as plsc

F32 = jnp.float32
BF16 = jnp.bfloat16
MESH = pl.DeviceIdType.MESH

D_MODEL = 1024
DEPTH = 2
CHUNK = 64
HEADS = 4
HEAD_K = 128
HEAD_V = 256
KEY_W = HEADS * HEAD_K
VAL_W = HEADS * HEAD_V
RANK = 16
TAU = 16.0
FFN = 2816
IN_WIDTH = 2 * KEY_W + 2 * VAL_W + RANK + 5 * D_MODEL
EPS = 1e-6
Q_SCALE = HEAD_K ** -0.5
N_DEV = 8
ADAM_LR, ADAM_B1, ADAM_B2, ADAM_EPS, ADAM_WD, ADAM_STEP = 0.001, 0.9, 0.999, 1e-08, 0.01, 10

LANES = 128
SUBLANES_BF16 = 16
VMEM_LIMIT = 48 * 1024 * 1024

FZ_COL = 2 * KEY_W + 2 * VAL_W + 5 * D_MODEL
PROJ_W = FZ_COL + LANES
SEG_R, SEG_GBI, SEG_GCI, SEG_CX, SEG_GA, SEG_GB = 2, 3, 4, 5, 6, 7

W_IN_SHARD = IN_WIDTH // N_DEV
FFN_SHARD = FFN // N_DEV
ROW_SHARD = D_MODEL // N_DEV

WIN_IN = 9 * LANES
WIN_GU = 4 * LANES
A_FZ = WIN_IN
A_GATE = A_FZ + LANES
A_UP = A_GATE + WIN_GU
A_COLS = A_UP + WIN_GU
ORIG_FZ = 2 * KEY_W + 2 * VAL_W


def _new_col(o):
    if o < ORIG_FZ:
        return o
    if o < ORIG_FZ + RANK:
        return FZ_COL + (o - ORIG_FZ)
    return o - RANK


def _shift_table():
    t_in, rows = [], []
    for j in range(N_DEV):
        new = [_new_col(W_IN_SHARD * j + i) for i in range(W_IN_SHARD)]
        main = [i for i in range(W_IN_SHARD) if new[i] < FZ_COL]
        code = [i for i in range(W_IN_SHARD) if new[i] >= FZ_COL]
        shift = new[main[0]] - main[0]
        t_in.append(shift // LANES)
        assert all(new[i] - i == shift for i in main) and shift % LANES + W_IN_SHARD <= WIN_IN
        if code:
            cshift = new[code[0]] - FZ_COL - code[0]
            crow = [cshift % LANES, code[0], code[-1] + 1, int(cshift < 0)]
        else:
            crow = [0, 0, 0, 0]
        rows.append([shift % LANES, main[0], main[-1] + 1] + crow + [FFN_SHARD * j % LANES])
    return tuple(t_in), np.asarray(rows, np.int32)


T_IN, SHIFT_TABLE = _shift_table()
T_GU = tuple(FFN_SHARD * j // LANES for j in range(N_DEV))

B_OA, B_OB, B_O, B_DOWN = 0, ROW_SHARD, 2 * ROW_SHARD, 3 * ROW_SHARD
B_FG2 = B_DOWN + FFN_SHARD
B_CONV = B_FG2 + 1
B_ROWS = B_FG2 + SUBLANES_BF16
SMALL_ROWS = 16


def _pick(n, candidates):
    for c in candidates:
        if n % c == 0:
            return c
    return n


def _cparams(sem):
    return pltpu.CompilerParams(dimension_semantics=sem, vmem_limit_bytes=VMEM_LIMIT)


def _sigmoid(x):
    return 1.0 / (1.0 + jnp.exp(-x))


def _matmul(a, b, dims, out_dtype, name, residual=None, after=()):
    if dims == "nn":
        (m, k), (k2, n) = a.shape, b.shape
    elif dims == "nt":
        (m, k), (n, k2) = a.shape, b.shape
    else:
        (k, m), (k2, n) = a.shape, b.shape
    assert k == k2, (a.shape, b.shape, dims)
    tm = _pick(m, (512, 256, 128))
    tn = _pick(n, (1664, 1408, 1024, 512, 256, 128))
    tk = _pick(k, (1664, 1408, 1024, 512, 256, 128))
    nk = k // tk
    if dims == "nn":
        a_spec = pl.BlockSpec((tm, tk), lambda i, j, kk: (i, kk))
        b_spec = pl.BlockSpec((tk, tn), lambda i, j, kk: (kk, j))
        contract = (((1,), (0,)), ((), ()))
    elif dims == "nt":
        a_spec = pl.BlockSpec((tm, tk), lambda i, j, kk: (i, kk))
        b_spec = pl.BlockSpec((tn, tk), lambda i, j, kk: (j, kk))
        contract = (((1,), (1,)), ((), ()))
    else:
        a_spec = pl.BlockSpec((tk, tm), lambda i, j, kk: (kk, i))
        b_spec = pl.BlockSpec((tk, tn), lambda i, j, kk: (kk, j))
        contract = (((0,), (0,)), ((), ()))
    o_spec = pl.BlockSpec((tm, tn), lambda i, j, kk: (i, j))
    has_res = residual is not None

    def body(*refs):
        a_ref, b_ref = refs[:2]
        r_ref = refs[2] if has_res else None
        o_ref, acc_ref = refs[-2:]
        kk = pl.program_id(2)
        part = lax.dot_general(a_ref[...], b_ref[...], contract, preferred_element_type=F32)

        @pl.when(kk == 0)
        def _():
            acc_ref[...] = part

        @pl.when(kk > 0)
        def _():
            acc_ref[...] += part

        @pl.when(kk == nk - 1)
        def _():
            res = acc_ref[...]
            if has_res:
                res = res + r_ref[...]
            o_ref[...] = res.astype(o_ref.dtype)

    in_specs = [a_spec, b_spec] + ([o_spec] if has_res else []) + [_after_spec(t) for t in after]
    args = (a, b) + ((residual,) if has_res else ()) + tuple(after)
    return pl.pallas_call(
        body, grid=(m // tm, n // tn, nk), in_specs=in_specs, out_specs=o_spec,
        out_shape=jax.ShapeDtypeStruct((m, n), out_dtype),
        scratch_shapes=[pltpu.VMEM((tm, tn), F32)], name=name,
        compiler_params=_cparams(("parallel", "parallel", "arbitrary")),
    )(*args)


def _rmsnorm_fwd(x, g, name):
    s, d = x.shape
    tm = _pick(s, (512, 256))

    def body(x_ref, g_ref, o_ref):
        xv = x_ref[...]
        r = lax.rsqrt(jnp.mean(xv * xv, axis=-1, keepdims=True) + EPS)
        o_ref[...] = (xv * r * g_ref[...]).astype(o_ref.dtype)

    row = pl.BlockSpec((tm, d), lambda i: (i, 0))
    return pl.pallas_call(
        body, grid=(s // tm,), in_specs=[row, pl.BlockSpec((1, d), lambda i: (0, 0))], out_specs=row,
        out_shape=jax.ShapeDtypeStruct((s, d), BF16), name=name, compiler_params=_cparams(("parallel",)),
    )(x, g)


def _rmsnorm_bwd(x, g, dh, dres, name):
    s, d = x.shape
    tm = _pick(s, (512, 256))

    def body(x_ref, g_ref, dh_ref, dres_ref, dx_ref, dg_ref):
        xv = x_ref[...]
        r = lax.rsqrt(jnp.mean(xv * xv, axis=-1, keepdims=True) + EPS)
        xn = xv * r
        dhv = dh_ref[...].astype(F32)
        dxn = dhv * g_ref[...]
        dx_ref[...] = dres_ref[...] + r * (dxn - xn * jnp.mean(dxn * xn, axis=-1, keepdims=True))

        @pl.when(pl.program_id(0) == 0)
        def _():
            dg_ref[...] = jnp.zeros_like(dg_ref)

        dg_ref[...] += jnp.broadcast_to(jnp.sum(dhv * xn, axis=0, keepdims=True), dg_ref.shape)

    row = pl.BlockSpec((tm, d), lambda i: (i, 0))
    acc = pl.BlockSpec((8, d), lambda i: (0, 0))
    return pl.pallas_call(
        body, grid=(s // tm,), in_specs=[row, pl.BlockSpec((1, d), lambda i: (0, 0)), row, row],
        out_specs=[row, acc],
        out_shape=[jax.ShapeDtypeStruct((s, d), F32), jax.ShapeDtypeStruct((8, d), F32)], name=name,
        compiler_params=_cparams(("arbitrary",)),
    )(x, g, dh, dres)


def _loss_head(x, g, target, name):
    s, d = x.shape
    tm = _pick(s, (512, 256))

    def body(x_ref, g_ref, t_ref, loss_ref, dx_ref, dg_ref):
        xv = x_ref[...]
        gv = g_ref[...]
        r = lax.rsqrt(jnp.mean(xv * xv, axis=-1, keepdims=True) + EPS)
        xn = xv * r
        err = xn * gv - t_ref[...]
        dy = err * (1.0 / d)
        dxn = dy * gv
        dx_ref[...] = r * (dxn - xn * jnp.mean(dxn * xn, axis=-1, keepdims=True))

        @pl.when(pl.program_id(0) == 0)
        def _():
            dg_ref[...] = jnp.zeros_like(dg_ref)
            loss_ref[...] = jnp.zeros_like(loss_ref)

        dg_ref[...] += jnp.broadcast_to(jnp.sum(dy * xn, axis=0, keepdims=True), dg_ref.shape)
        row_loss = jnp.sum(err * err, axis=-1, keepdims=True)
        loss_ref[...] += jnp.broadcast_to((0.5 / d) * jnp.sum(row_loss, axis=0, keepdims=True), loss_ref.shape)

    row = pl.BlockSpec((tm, d), lambda i: (i, 0))
    return pl.pallas_call(
        body, grid=(s // tm,), in_specs=[row, pl.BlockSpec((1, d), lambda i: (0, 0)), row],
        out_specs=[pl.BlockSpec((8, LANES), lambda i: (0, 0)), row, pl.BlockSpec((8, d), lambda i: (0, 0))],
        out_shape=[jax.ShapeDtypeStruct((8, LANES), F32), jax.ShapeDtypeStruct((s, d), F32),
                   jax.ShapeDtypeStruct((8, d), F32)],
        name=name, compiler_params=_cparams(("arbitrary",)),
    )(x, g, target)


def _tri_dot(tri, x):
    hi = x.astype(BF16)
    lo = (x - hi.astype(F32)).astype(BF16)
    return jnp.dot(tri, hi, preferred_element_type=F32) + jnp.dot(tri, lo, preferred_element_type=F32)


def _tri(strict):
    r = lax.broadcasted_iota(jnp.int32, (CHUNK, CHUNK), 0)
    c = lax.broadcasted_iota(jnp.int32, (CHUNK, CHUNK), 1)
    return jnp.where((r > c) if strict else (r >= c), 1.0, 0.0).astype(BF16)


def _chunk_decay(fz, w, b, tri):
    fg = jnp.dot(fz, w, preferred_element_type=F32) + b
    la = (jnp.minimum(fg, 0.0) - jnp.log(1.0 + jnp.exp(-jnp.abs(fg)))) * (1.0 / TAU)
    cum = _tri_dot(tri, la)
    cum_end = cum[CHUNK - 1:CHUNK, :]
    return fg, jnp.exp(cum_end - cum), jnp.exp(cum_end)


_TN = (((0,), (0,)), ((), ()))
_NT = (((1,), (1,)), ((), ()))


def _gla_specs(rows):
    q_spec = pl.BlockSpec((rows, HEAD_K), lambda h, c: (c, h))
    k_spec = pl.BlockSpec((rows, HEAD_K), lambda h, c: (c, HEADS + h))
    v_spec = pl.BlockSpec((rows, HEAD_V), lambda h, c: (c, HEADS + h))
    fz_spec = pl.BlockSpec((rows, LANES), lambda h, c: (c, FZ_COL // LANES))
    w_spec = pl.BlockSpec((LANES, HEAD_K), lambda h, c: (0, h))
    b_spec = pl.BlockSpec((1, HEAD_K), lambda h, c: (0, h))
    return q_spec, k_spec, v_spec, fz_spec, w_spec, b_spec


def _gla_fwd(proj, wfg, bfg, name):
    s = proj.shape[0]
    nc = s // CHUNK
    per = _pick(nc, (8, 4, 2, 1))
    rows = per * CHUNK

    def body(q_ref, k_ref, v_ref, fz_ref, w_ref, b_ref, o_ref, st_ref, state):
        @pl.when(pl.program_id(1) == 0)
        def _():
            state[...] = jnp.zeros_like(state)

        tri = _tri(False)
        w = w_ref[...]
        b = b_ref[...]
        for i in range(per):
            sl = pl.ds(i * CHUNK, CHUNK)
            _, dec, gamma = _chunk_decay(fz_ref[sl, :], w, b, tri)
            kd = (k_ref[sl, :].astype(F32) * dec).astype(BF16)
            st = state[...] * gamma + lax.dot_general(v_ref[sl, :], kd, _TN, preferred_element_type=F32)
            state[...] = st
            st16 = st.astype(BF16)
            st_ref[0, i] = st16
            qs = (q_ref[sl, :].astype(F32) * Q_SCALE).astype(BF16)
            o_ref[sl, :] = lax.dot_general(qs, st16, _NT, preferred_element_type=F32).astype(o_ref.dtype)

    q_spec, k_spec, v_spec, fz_spec, w_spec, b_spec = _gla_specs(rows)
    return pl.pallas_call(
        body, grid=(HEADS, nc // per),
        in_specs=[q_spec, k_spec, v_spec, fz_spec, w_spec, b_spec],
        out_specs=[pl.BlockSpec((rows, HEAD_V), lambda h, c: (c, h)),
                   pl.BlockSpec((1, per, HEAD_V, HEAD_K), lambda h, c: (h, c, 0, 0))],
        out_shape=[jax.ShapeDtypeStruct((s, VAL_W), BF16),
                   jax.ShapeDtypeStruct((HEADS, nc, HEAD_V, HEAD_K), BF16)],
        scratch_shapes=[pltpu.VMEM((HEAD_V, HEAD_K), F32)], name=name,
        compiler_params=_cparams(("parallel", "arbitrary")),
    )(proj, proj, proj, proj, wfg, bfg)


def _gla_bwd(proj, wfg, bfg, states, do, name):
    s = proj.shape[0]
    nc = s // CHUNK
    per = _pick(nc, (8, 4, 2, 1))
    rows = per * CHUNK
    nblk = nc // per

    def rev(spec_fn):
        return lambda h, j: spec_fn(h, nblk - 1 - j)

    def body(q_ref, k_ref, v_ref, fz_ref, w_ref, b_ref, do_ref, st_ref, prev_ref,
             dq_ref, dk_ref, dv_ref, dfg_ref, db_ref, carry):
        j = pl.program_id(1)

        @pl.when(j == 0)
        def _():
            carry[...] = jnp.zeros_like(carry)
            db_ref[...] = jnp.zeros_like(db_ref)

        tri = _tri(False)
        tri_strict = _tri(True)
        w = w_ref[...]
        b = b_ref[...]
        has_prev = (j < nblk - 1).astype(F32)
        db = jnp.zeros((1, HEAD_K), F32)
        for i in reversed(range(per)):
            sl = pl.ds(i * CHUNK, CHUNK)
            fg, dec, gamma = _chunk_decay(fz_ref[sl, :], w, b, tri)
            kd = k_ref[sl, :].astype(F32) * dec
            qs = (q_ref[sl, :].astype(F32) * Q_SCALE).astype(BF16)
            dov = do_ref[sl, :]
            v = v_ref[sl, :]
            gt = carry[...] + lax.dot_general(dov, qs, _TN, preferred_element_type=F32)
            gt16 = gt.astype(BF16)
            dq_ref[sl, :] = (jnp.dot(dov, st_ref[0, i], preferred_element_type=F32) * Q_SCALE).astype(dq_ref.dtype)
            dkd = jnp.dot(v, gt16, preferred_element_type=F32)
            dv_ref[sl, :] = lax.dot_general(kd.astype(BF16), gt16, _NT,
                                            preferred_element_type=F32).astype(dv_ref.dtype)
            if i > 0:
                st_prev = st_ref[0, i - 1].astype(F32)
            else:
                st_prev = prev_ref[0, 0].astype(F32) * has_prev
            dgamma = jnp.sum(gt * st_prev, axis=0, keepdims=True)
            carry[...] = gt * gamma
            dk_ref[sl, :] = (dkd * dec).astype(dk_ref.dtype)
            dla = dgamma * gamma + _tri_dot(tri_strict, dkd * kd)
            dfg = dla * (1.0 / TAU) * _sigmoid(-fg)
            dfg_ref[sl, :] = dfg.astype(dfg_ref.dtype)
            db = db + jnp.sum(dfg, axis=0, keepdims=True)
        db_ref[...] += jnp.broadcast_to(db, db_ref.shape)

    q_spec, k_spec, v_spec, fz_spec, w_spec, b_spec = _gla_specs(rows)
    q_spec, k_spec, v_spec, fz_spec = [
        pl.BlockSpec(sp.block_shape, rev(sp.index_map)) for sp in (q_spec, k_spec, v_spec, fz_spec)]
    do_spec = pl.BlockSpec((rows, HEAD_V), lambda h, j: (nblk - 1 - j, h))
    st_spec = pl.BlockSpec((1, per, HEAD_V, HEAD_K), lambda h, j: (h, nblk - 1 - j, 0, 0))
    prev_spec = pl.BlockSpec((1, 1, HEAD_V, HEAD_K),
                             lambda h, j: (h, jnp.maximum((nblk - 1 - j) * per - 1, 0), 0, 0))
    key_out = pl.BlockSpec((rows, HEAD_K), lambda h, j: (nblk - 1 - j, h))
    return pl.pallas_call(
        body, grid=(HEADS, nblk),
        in_specs=[q_spec, k_spec, v_spec, fz_spec, w_spec, b_spec, do_spec, st_spec, prev_spec],
        out_specs=[key_out, key_out, do_spec, key_out, pl.BlockSpec((8, HEAD_K), lambda h, j: (0, h))],
        out_shape=[jax.ShapeDtypeStruct((s, KEY_W), BF16), jax.ShapeDtypeStruct((s, KEY_W), BF16),
                   jax.ShapeDtypeStruct((s, VAL_W), BF16), jax.ShapeDtypeStruct((s, KEY_W), BF16),
                   jax.ShapeDtypeStruct((8, KEY_W), F32)],
        scratch_shapes=[pltpu.VMEM((HEAD_V, HEAD_K), F32)], name=name,
        compiler_params=_cparams(("parallel", "arbitrary")),
    )(proj, proj, proj, proj, wfg, bfg, do, states, states)


def _seg(tm, seg):
    return pl.BlockSpec((tm, D_MODEL), lambda i: (i, seg))


def _gla_post_fwd(o, proj, g, name):
    s = o.shape[0]
    tm = _pick(s, (512, 256))

    def body(o_ref, r_ref, g_ref, oa_ref):
        gv = g_ref[...]
        for h in range(HEADS):
            sl = slice(h * HEAD_V, (h + 1) * HEAD_V)
            ov = o_ref[:, sl].astype(F32)
            rstd = lax.rsqrt(jnp.mean(ov * ov, axis=-1, keepdims=True) + EPS)
            rv = r_ref[:, sl].astype(F32)
            oa_ref[:, sl] = (ov * rstd * gv * (rv * _sigmoid(rv))).astype(oa_ref.dtype)

    row = pl.BlockSpec((tm, VAL_W), lambda i: (i, 0))
    return pl.pallas_call(
        body, grid=(s // tm,), in_specs=[row, _seg(tm, SEG_R), pl.BlockSpec((1, HEAD_V), lambda i: (0, 0))],
        out_specs=row, out_shape=jax.ShapeDtypeStruct((s, VAL_W), BF16), name=name,
        compiler_params=_cparams(("parallel",)),
    )(o, proj, g)


def _gla_post_bwd(doa, o, proj, g, name):
    s = o.shape[0]
    tm = _pick(s, (512, 256))

    def body(doa_ref, o_ref, r_ref, g_ref, dr_ref, do_ref, dg_ref):
        @pl.when(pl.program_id(0) == 0)
        def _():
            dg_ref[...] = jnp.zeros_like(dg_ref)

        gv = g_ref[...]
        dg = jnp.zeros((1, HEAD_V), F32)
        for h in range(HEADS):
            sl = slice(h * HEAD_V, (h + 1) * HEAD_V)
            ov = o_ref[:, sl].astype(F32)
            rstd = lax.rsqrt(jnp.mean(ov * ov, axis=-1, keepdims=True) + EPS)
            ohat = ov * rstd
            rv = r_ref[:, sl].astype(F32)
            sg = _sigmoid(rv)
            dv = doa_ref[:, sl].astype(F32)
            dr_ref[:, sl] = (dv * ohat * gv * (sg * (1.0 + rv * (1.0 - sg)))).astype(dr_ref.dtype)
            don = dv * (rv * sg)
            dg = dg + jnp.sum(don * ohat, axis=0, keepdims=True)
            dohat = don * gv
            do_ref[:, sl] = (rstd * (dohat - ohat * jnp.mean(dohat * ohat, axis=-1, keepdims=True))
                             ).astype(do_ref.dtype)
        dg_ref[...] += jnp.broadcast_to(dg, dg_ref.shape)

    row = pl.BlockSpec((tm, VAL_W), lambda i: (i, 0))
    return pl.pallas_call(
        body, grid=(s // tm,),
        in_specs=[row, row, _seg(tm, SEG_R), pl.BlockSpec((1, HEAD_V), lambda i: (0, 0))],
        out_specs=[row, row, pl.BlockSpec((8, HEAD_V), lambda i: (0, 0))],
        out_shape=[jax.ShapeDtypeStruct((s, VAL_W), BF16), jax.ShapeDtypeStruct((s, VAL_W), BF16),
                   jax.ShapeDtypeStruct((8, HEAD_V), F32)],
        name=name, compiler_params=_cparams(("arbitrary",)),
    )(doa, o, proj, g)


HALO = SUBLANES_BF16


def _shift_down(u, p1, p2, n, rows):
    rolled = pltpu.roll(u, n, 0)
    if n == 1:
        return jnp.where(rows == 0, p1, rolled)
    return jnp.where(rows == 0, p2, jnp.where(rows == 1, p1, rolled))


def _shift_up(u, n1, n2, n, rows, tm):
    rolled = pltpu.roll(u, tm - n, 0)
    if n == 1:
        return jnp.where(rows == tm - 1, n1, rolled)
    return jnp.where(rows == tm - 2, n1, jnp.where(rows == tm - 1, n2, rolled))


def _conv_terms(gc_ref, cx_ref, gcp_ref, cxp_ref, tm):
    i = pl.program_id(0)
    u = gc_ref[...].astype(F32) * cx_ref[...].astype(F32)
    up = gcp_ref[...].astype(F32) * cxp_ref[...].astype(F32) * (i > 0).astype(F32)
    rows = lax.broadcasted_iota(jnp.int32, (tm, 1), 0)
    u1 = _shift_down(u, up[HALO - 1:HALO, :], up[HALO - 2:HALO - 1, :], 1, rows)
    u2 = _shift_down(u, up[HALO - 1:HALO, :], up[HALO - 2:HALO - 1, :], 2, rows)
    return u, u1, u2, rows


def _prev_halo(tm, seg):
    return pl.BlockSpec((HALO, D_MODEL), lambda i: (jnp.maximum(i * (tm // HALO) - 1, 0), seg))


def _conv_fwd(proj, w, b, name):
    s = proj.shape[0]
    tm = _pick(s, (512, 256))

    def body(gbi_ref, gc_ref, cx_ref, gcp_ref, cxp_ref, w_ref, b_ref, cb_ref):
        u, u1, u2, _ = _conv_terms(gc_ref, cx_ref, gcp_ref, cxp_ref, tm)
        conv = w_ref[0:1, :] * u2 + w_ref[1:2, :] * u1 + w_ref[2:3, :] * u + b_ref[...]
        cb_ref[...] = (gbi_ref[...].astype(F32) * conv).astype(cb_ref.dtype)

    return pl.pallas_call(
        body, grid=(s // tm,),
        in_specs=[_seg(tm, SEG_GBI), _seg(tm, SEG_GCI), _seg(tm, SEG_CX),
                  _prev_halo(tm, SEG_GCI), _prev_halo(tm, SEG_CX),
                  pl.BlockSpec((3, D_MODEL), lambda i: (0, 0)), pl.BlockSpec((1, D_MODEL), lambda i: (0, 0))],
        out_specs=pl.BlockSpec((tm, D_MODEL), lambda i: (i, 0)),
        out_shape=jax.ShapeDtypeStruct((s, D_MODEL), BF16), name=name, compiler_params=_cparams(("parallel",)),
    )(proj, proj, proj, proj, proj, w, b)


def _conv_bwd(dcb, proj, w, b, name):
    s = proj.shape[0]
    tm = _pick(s, (512, 256))
    nt = s // tm

    def body(dcb_ref, gbi_ref, gc_ref, cx_ref, gcp_ref, cxp_ref, dcbn_ref, gbin_ref, w_ref, b_ref,
             d3_ref, dwb_ref):
        i = pl.program_id(0)

        @pl.when(i == 0)
        def _():
            dwb_ref[...] = jnp.zeros_like(dwb_ref)

        u, u1, u2, rows = _conv_terms(gc_ref, cx_ref, gcp_ref, cxp_ref, tm)
        w0, w1, w2 = w_ref[0:1, :], w_ref[1:2, :], w_ref[2:3, :]
        conv = w0 * u2 + w1 * u1 + w2 * u + b_ref[...]
        dcbv = dcb_ref[...].astype(F32)
        gbi = gbi_ref[...].astype(F32)
        dconv = dcbv * gbi
        dnext = dcbn_ref[...].astype(F32) * gbin_ref[...].astype(F32) * (i < nt - 1).astype(F32)
        dc1 = _shift_up(dconv, dnext[0:1, :], dnext[1:2, :], 1, rows, tm)
        dc2 = _shift_up(dconv, dnext[0:1, :], dnext[1:2, :], 2, rows, tm)
        du = w2 * dconv + w1 * dc1 + w0 * dc2
        d3_ref[:, 0:D_MODEL] = (dcbv * conv).astype(d3_ref.dtype)
        d3_ref[:, D_MODEL:2 * D_MODEL] = (du * cx_ref[...].astype(F32)).astype(d3_ref.dtype)
        d3_ref[:, 2 * D_MODEL:3 * D_MODEL] = (du * gc_ref[...].astype(F32)).astype(d3_ref.dtype)
        dwb_ref[0:1, :] += jnp.sum(dconv * u2, axis=0, keepdims=True)
        dwb_ref[1:2, :] += jnp.sum(dconv * u1, axis=0, keepdims=True)
        dwb_ref[2:3, :] += jnp.sum(dconv * u, axis=0, keepdims=True)
        dwb_ref[3:4, :] += jnp.sum(dconv, axis=0, keepdims=True)

    def next_halo(seg_fn):
        return pl.BlockSpec((HALO, D_MODEL), lambda i: (jnp.minimum((i + 1) * (tm // HALO), s // HALO - 1), seg_fn))

    return pl.pallas_call(
        body, grid=(nt,),
        in_specs=[pl.BlockSpec((tm, D_MODEL), lambda i: (i, 0)),
                  _seg(tm, SEG_GBI), _seg(tm, SEG_GCI), _seg(tm, SEG_CX),
                  _prev_halo(tm, SEG_GCI), _prev_halo(tm, SEG_CX),
                  next_halo(0), next_halo(SEG_GBI),
                  pl.BlockSpec((3, D_MODEL), lambda i: (0, 0)), pl.BlockSpec((1, D_MODEL), lambda i: (0, 0))],
        out_specs=[pl.BlockSpec((tm, 3 * D_MODEL), lambda i: (i, 0)), pl.BlockSpec((8, D_MODEL), lambda i: (0, 0))],
        out_shape=[jax.ShapeDtypeStruct((s, 3 * D_MODEL), BF16), jax.ShapeDtypeStruct((8, D_MODEL), F32)],
        name=name, compiler_params=_cparams(("arbitrary",)),
    )(dcb, proj, proj, proj, proj, proj, dcb, proj, w, b)


def _mix_fwd(proj, ya, yb, name):
    s = proj.shape[0]
    tm = _pick(s, (512, 256))

    def body(ga_ref, gb_ref, ya_ref, yb_ref, o_ref):
        o_ref[...] = (_sigmoid(ga_ref[...].astype(F32)) * ya_ref[...].astype(F32)
                      + _sigmoid(gb_ref[...].astype(F32)) * yb_ref[...].astype(F32)).astype(o_ref.dtype)

    row = pl.BlockSpec((tm, D_MODEL), lambda i: (i, 0))
    return pl.pallas_call(
        body, grid=(s // tm,), in_specs=[_seg(tm, SEG_GA), _seg(tm, SEG_GB), row, row], out_specs=row,
        out_shape=jax.ShapeDtypeStruct((s, D_MODEL), BF16), name=name, compiler_params=_cparams(("parallel",)),
    )(proj, proj, ya, yb)


def _mix_bwd(dmix, proj, ya, yb, name):
    s = proj.shape[0]
    tm = _pick(s, (512, 256))

    def body(dm_ref, ga_ref, gb_ref, ya_ref, yb_ref, dg_ref, dya_ref, dyb_ref):
        dm = dm_ref[...].astype(F32)
        sa = _sigmoid(ga_ref[...].astype(F32))
        sb = _sigmoid(gb_ref[...].astype(F32))
        dg_ref[:, 0:D_MODEL] = (dm * ya_ref[...].astype(F32) * sa * (1.0 - sa)).astype(dg_ref.dtype)
        dg_ref[:, D_MODEL:2 * D_MODEL] = (dm * yb_ref[...].astype(F32) * sb * (1.0 - sb)).astype(dg_ref.dtype)
        dya_ref[...] = (dm * sa).astype(dya_ref.dtype)
        dyb_ref[...] = (dm * sb).astype(dyb_ref.dtype)

    row = pl.BlockSpec((tm, D_MODEL), lambda i: (i, 0))
    return pl.pallas_call(
        body, grid=(s // tm,), in_specs=[row, _seg(tm, SEG_GA), _seg(tm, SEG_GB), row, row],
        out_specs=[pl.BlockSpec((tm, 2 * D_MODEL), lambda i: (i, 0)), row, row],
        out_shape=[jax.ShapeDtypeStruct((s, 2 * D_MODEL), BF16), jax.ShapeDtypeStruct((s, D_MODEL), BF16),
                   jax.ShapeDtypeStruct((s, D_MODEL), BF16)],
        name=name, compiler_params=_cparams(("parallel",)),
    )(dmix, proj, proj, ya, yb)


def _swiglu_fwd(gu, name):
    s = gu.shape[0]
    tm = _pick(s, (256,))

    def body(gu_ref, o_ref):
        gate = gu_ref[:, 0:FFN].astype(F32)
        o_ref[...] = (gate * _sigmoid(gate) * gu_ref[:, FFN:2 * FFN].astype(F32)).astype(o_ref.dtype)

    return pl.pallas_call(
        body, grid=(s // tm,), in_specs=[pl.BlockSpec((tm, 2 * FFN), lambda i: (i, 0))],
        out_specs=pl.BlockSpec((tm, FFN), lambda i: (i, 0)),
        out_shape=jax.ShapeDtypeStruct((s, FFN), BF16), name=name, compiler_params=_cparams(("parallel",)),
    )(gu)


def _swiglu_bwd(dhid, gu, name):
    s = gu.shape[0]
    tm = _pick(s, (256,))

    def body(dh_ref, gu_ref, o_ref):
        gate = gu_ref[:, 0:FFN].astype(F32)
        up = gu_ref[:, FFN:2 * FFN].astype(F32)
        dh = dh_ref[...].astype(F32)
        sg = _sigmoid(gate)
        o_ref[:, 0:FFN] = (dh * up * (sg * (1.0 + gate * (1.0 - sg)))).astype(o_ref.dtype)
        o_ref[:, FFN:2 * FFN] = (dh * gate * sg).astype(o_ref.dtype)

    wide = pl.BlockSpec((tm, 2 * FFN), lambda i: (i, 0))
    return pl.pallas_call(
        body, grid=(s // tm,), in_specs=[pl.BlockSpec((tm, FFN), lambda i: (i, 0)), wide], out_specs=wide,
        out_shape=jax.ShapeDtypeStruct((s, 2 * FFN), BF16), name=name, compiler_params=_cparams(("parallel",)),
    )(dhid, gu)


def _adamw_math(w, g, m, v):
    m2 = ADAM_B1 * m + (1.0 - ADAM_B1) * g
    v2 = ADAM_B2 * v + (1.0 - ADAM_B2) * (g * g)
    m_hat = m2 / (1.0 - ADAM_B1 ** ADAM_STEP)
    v_hat = v2 / (1.0 - ADAM_B2 ** ADAM_STEP)
    delta = -ADAM_LR * (m_hat / (jnp.sqrt(v_hat) + ADAM_EPS) + ADAM_WD * w)
    return delta, m2, v2


def _adamw(w, g, m, v, name):
    shape = w.shape
    cols = shape[-1]
    rows = int(np.prod(shape[:-1])) if len(shape) > 1 else 1
    w2, g2, m2, v2 = [t.reshape(rows, cols) for t in (w, g, m, v)]
    tr = _pick(rows, (512, 352, 256)) if rows % 8 == 0 else rows

    def body(w_ref, g_ref, m_ref, v_ref, d_ref, nm_ref, nv_ref):
        d, nm, nv = _adamw_math(w_ref[...], g_ref[...], m_ref[...], v_ref[...])
        d_ref[...] = d
        nm_ref[...] = nm
        nv_ref[...] = nv

    blk = pl.BlockSpec((tr, cols), lambda i: (i, 0))
    out = pl.pallas_call(
        body, grid=(rows // tr,), in_specs=[blk] * 4, out_specs=[blk] * 3,
        out_shape=[jax.ShapeDtypeStruct((rows, cols), F32)] * 3, name=name,
        compiler_params=_cparams(("parallel",)),
    )(w2, g2, m2, v2)
    return [t.reshape(shape) for t in out]


def _pair_sum(g2, recv, core, name):
    _, nchip, r, c = g2.shape
    tr = _pick(r, (512,))

    def body(core_ref, a_ref, b_ref, o_ref):
        o_ref[...] = (a_ref[...].astype(F32) + b_ref[...].astype(F32)).astype(o_ref.dtype)

    grid_spec = pltpu.PrefetchScalarGridSpec(
        num_scalar_prefetch=1, grid=(nchip, r // tr),
        in_specs=[pl.BlockSpec((None, None, tr, c), lambda k, i, cr: (cr[0], k, i, 0)),
                  pl.BlockSpec((None, tr, c), lambda k, i, cr: (k, i, 0))],
        out_specs=pl.BlockSpec((None, tr, c), lambda k, i, cr: (k, i, 0)))
    return pl.pallas_call(
        body, grid_spec=grid_spec, out_shape=jax.ShapeDtypeStruct((nchip, r, c), BF16), name=name,
        compiler_params=_cparams(("parallel", "parallel")),
    )(core, g2, recv)


def _chip_sum(a, recv, chip, name, after=()):
    _, r, c = a.shape
    tr = _pick(r, (512,))

    def body(chip_ref, a_ref, b_ref, *rest):
        o_ref = rest[-1]
        o_ref[...] = ((a_ref[...].astype(F32) + b_ref[0].astype(F32)) + b_ref[1].astype(F32)) + b_ref[2].astype(F32)

    grid_spec = pltpu.PrefetchScalarGridSpec(
        num_scalar_prefetch=1, grid=(r // tr,),
        in_specs=[pl.BlockSpec((None, tr, c), lambda i, cr: (cr[0], i, 0)),
                  pl.BlockSpec((3, tr, c), lambda i, cr: (0, i, 0))] + [_after_spec(t) for t in after],
        out_specs=pl.BlockSpec((tr, c), lambda i, cr: (i, 0)))
    return pl.pallas_call(
        body, grid_spec=grid_spec, out_shape=jax.ShapeDtypeStruct((r, c), F32), name=name,
        compiler_params=_cparams(("parallel",)),
    )(chip, a, recv, *after)


def _sum_devices(parts, name):
    n, r, c = parts.shape

    def body(p_ref, o_ref):
        acc = p_ref[0]
        for d in range(1, n):
            acc = acc + p_ref[d]
        o_ref[...] = acc

    return pl.pallas_call(
        body, out_shape=jax.ShapeDtypeStruct((r, c), F32), name=name,
        in_specs=[pl.BlockSpec(memory_space=pltpu.VMEM)], out_specs=pl.BlockSpec(memory_space=pltpu.VMEM),
    )(parts)


def _lane_iota():
    return lax.broadcasted_iota(jnp.int32, (1, LANES), 1)


def _tiles_up(tiles, s, lane):
    rolled = [pltpu.roll(t, s, 1) for t in tiles]
    zero = jnp.zeros_like(tiles[0])
    return [jnp.where(lane < s, p, c) for p, c in zip([zero] + rolled, rolled + [zero])]


def _tiles_down(tiles, s, lane):
    back = (LANES - s) % LANES
    rolled = [pltpu.roll(t, back, 1) for t in tiles]
    zero = jnp.zeros_like(tiles[0])
    return [jnp.where(lane < LANES - s, c, n) for c, n in zip(rolled, rolled[1:] + [zero])]


def _window_cols(prm, w_in, w_gate, w_up, l, name, after=None):
    tm = 256
    n_in, n_gu = WIN_IN // LANES, FFN_SHARD // LANES + 1

    def body(prm_ref, win_ref, g_ref, u_ref, *rest):
        out_ref, scr_in, scr_gu = rest[-3:]
        lane = _lane_iota()
        s_main, lo, hi, s_code, clo, chi, code_hi, s_gu = [prm_ref[i] for i in range(8)]
        scr_in[:, D_MODEL:WIN_IN] = jnp.zeros((tm, WIN_IN - D_MODEL), F32)
        scr_in[:, 0:W_IN_SHARD] = win_ref[...]

        def keep(t, a, b):
            col = lane + t * LANES
            return jnp.where((col >= a) & (col < b), scr_in[:, t * LANES:(t + 1) * LANES], 0.0)

        main = _tiles_up([keep(t, lo, hi) for t in range(n_in)], s_main, lane)
        for t in range(n_in):
            out_ref[:, t * LANES:(t + 1) * LANES] = main[t].astype(BF16)
        low = _tiles_up([keep(0, clo, chi)], s_code, lane)[0]
        high = _tiles_up([keep(n_in - 2, clo, chi), keep(n_in - 1, clo, chi)], s_code, lane)[1]
        out_ref[:, A_FZ:A_FZ + LANES] = jnp.where(code_hi == 1, high, low).astype(BF16)
        for ref, base in ((g_ref, A_GATE), (u_ref, A_UP)):
            scr_gu[:, (n_gu - 1) * LANES:n_gu * LANES] = jnp.zeros((tm, LANES), F32)
            scr_gu[:, 0:FFN_SHARD] = ref[...]
            moved = _tiles_up([scr_gu[:, t * LANES:(t + 1) * LANES] for t in range(n_gu)], s_gu, lane)
            for t in range(n_gu + 1):
                out_ref[:, base + t * LANES:base + (t + 1) * LANES] = moved[t].astype(BF16)

    after_args = [] if after is None else [after]
    grid_spec = pltpu.PrefetchScalarGridSpec(
        num_scalar_prefetch=1, grid=(D_MODEL // tm,),
        in_specs=[pl.BlockSpec((None, tm, W_IN_SHARD), lambda i, p: (l, i, 0)),
                  pl.BlockSpec((None, tm, FFN_SHARD), lambda i, p: (l, i, 0)),
                  pl.BlockSpec((None, tm, FFN_SHARD), lambda i, p: (l, i, 0))] + [_after_spec(t) for t in after_args],
        out_specs=pl.BlockSpec((tm, A_COLS), lambda i, p: (i, 0)),
        scratch_shapes=[pltpu.VMEM((tm, WIN_IN), F32), pltpu.VMEM((tm, n_gu * LANES), F32)])
    return pl.pallas_call(
        body, grid_spec=grid_spec, out_shape=jax.ShapeDtypeStruct((D_MODEL, A_COLS), BF16), name=name,
        compiler_params=_cparams(("parallel",)),
    )(prm, w_in, w_gate, w_up, *after_args)


def _gu_width(j):
    return min(WIN_GU, FFN - T_GU[j] * LANES)


def _after_spec(t):
    tile = (SUBLANES_BF16 if t.dtype == BF16 else 8, LANES)
    return pl.BlockSpec((None,) * (t.ndim - 2) + tile, lambda *_: (0,) * t.ndim)


def _assemble(a_all, rows_all, name, after=None):
    tm = D_MODEL // N_DEV
    n_in = 2 if after is None else 3

    def body(*refs):
        a_ref, r_ref = refs[:2]
        win_ref, wgu_ref, oa_ref, ob_ref, o_ref, down_ref, tail_ref = refs[n_in:]
        win_ref[...] = jnp.zeros_like(win_ref)
        wgu_ref[...] = jnp.zeros_like(wgu_ref)
        code = a_ref[0, :, A_FZ:A_FZ + LANES]
        for j in range(N_DEV):
            c0 = T_IN[j] * LANES
            win_ref[:, c0:c0 + WIN_IN] += a_ref[j, :, 0:WIN_IN]
            g0, width = T_GU[j] * LANES, _gu_width(j)
            wgu_ref[:, g0:g0 + width] += a_ref[j, :, A_GATE:A_GATE + width]
            wgu_ref[:, FFN + g0:FFN + g0 + width] += a_ref[j, :, A_UP:A_UP + width]
            if j > 0:
                code = code + a_ref[j, :, A_FZ:A_FZ + LANES]
        win_ref[:, FZ_COL:PROJ_W] = code
        oa_ref[...] = r_ref[B_OA:B_OA + ROW_SHARD, :]
        ob_ref[...] = r_ref[B_OB:B_OB + ROW_SHARD, :]
        o_ref[...] = r_ref[B_O:B_O + ROW_SHARD, :]
        down_ref[...] = r_ref[B_DOWN:B_DOWN + FFN_SHARD, :]
        tail_ref[...] = r_ref[B_FG2:B_ROWS, :]

    def rows(n):
        return pl.BlockSpec((n, D_MODEL), lambda i: (i, 0))

    in_specs = [pl.BlockSpec((N_DEV, tm, A_COLS), lambda i: (0, i, 0)),
                pl.BlockSpec((None, B_ROWS, D_MODEL), lambda i: (i, 0, 0))]
    args = [a_all, rows_all]
    if after is not None:
        in_specs.append(_after_spec(after))
        args.append(after)
    square = jax.ShapeDtypeStruct((D_MODEL, D_MODEL), BF16)
    return pl.pallas_call(
        body, grid=(N_DEV,), in_specs=in_specs,
        out_specs=[pl.BlockSpec((tm, PROJ_W), lambda i: (i, 0)), pl.BlockSpec((tm, 2 * FFN), lambda i: (i, 0)),
                   rows(ROW_SHARD), rows(ROW_SHARD), rows(ROW_SHARD), rows(FFN_SHARD),
                   pl.BlockSpec((None, B_ROWS - B_FG2, D_MODEL), lambda i: (i, 0, 0))],
        out_shape=[jax.ShapeDtypeStruct((D_MODEL, PROJ_W), BF16), jax.ShapeDtypeStruct((D_MODEL, 2 * FFN), BF16),
                   square, square, square, jax.ShapeDtypeStruct((FFN, D_MODEL), BF16),
                   jax.ShapeDtypeStruct((N_DEV, B_ROWS - B_FG2, D_MODEL), BF16)],
        name=name, compiler_params=_cparams(("parallel",)),
    )(*args)


def _grad_windows(d_in, d_gu, name):
    tm = 128

    def body(din_ref, dgu_ref, out_ref):
        for j in range(N_DEV):
            o = out_ref.at[j & 1, j >> 1]
            c0 = T_IN[j] * LANES
            o[:, 0:WIN_IN] = din_ref[:, c0:c0 + WIN_IN]
            o[:, A_FZ:A_FZ + LANES] = din_ref[:, FZ_COL:PROJ_W]
            g0, width = T_GU[j] * LANES, _gu_width(j)
            o[:, A_GATE:A_GATE + width] = dgu_ref[:, g0:g0 + width]
            o[:, A_UP:A_UP + width] = dgu_ref[:, FFN + g0:FFN + g0 + width]
            if width < WIN_GU:
                o[:, A_GATE + width:A_UP] = jnp.zeros((tm, WIN_GU - width), BF16)
                o[:, A_UP + width:A_COLS] = jnp.zeros((tm, WIN_GU - width), BF16)

    return pl.pallas_call(
        body, grid=(D_MODEL // tm,),
        in_specs=[pl.BlockSpec((tm, PROJ_W), lambda i: (i, 0)), pl.BlockSpec((tm, 2 * FFN), lambda i: (i, 0))],
        out_specs=pl.BlockSpec((2, 4, tm, A_COLS), lambda i: (0, 0, i, 0)),
        out_shape=jax.ShapeDtypeStruct((2, 4, D_MODEL, A_COLS), BF16), name=name,
        compiler_params=_cparams(("parallel",)),
    )(d_in, d_gu)


def _final_cols(prm, chip, a, recv, params, l, name, after=()):
    tm = 128
    n_in, n_gu = WIN_IN // LANES, FFN_SHARD // LANES + 1
    widths = (W_IN_SHARD, FFN_SHARD, FFN_SHARD)

    def body(prm_ref, chip_ref, a_ref, r_ref, *refs):
        ins, outs = refs[:9], refs[9 + len(after):]
        lane = _lane_iota()
        s_main, s_code, clo, chi, s_gu = [prm_ref[i] for i in (0, 3, 4, 5, 7)]

        def total(c0):
            sl = slice(c0, c0 + LANES)
            return ((a_ref[:, sl].astype(F32) + r_ref[0, :, sl].astype(F32)) + r_ref[1, :, sl].astype(F32)) \
                + r_ref[2, :, sl].astype(F32)

        grads = _tiles_down([total(t * LANES) for t in range(n_in)], s_main, lane)
        code = pltpu.roll(total(A_FZ), (LANES - s_code) % LANES, 1)
        for t in (0, n_in - 2, n_in - 1):
            col = lane + t * LANES
            grads[t] = jnp.where((col >= clo) & (col < chi), code, grads[t])
        per_weight = [grads]
        for base in (A_GATE, A_UP):
            per_weight.append(_tiles_down([total(base + t * LANES) for t in range(n_gu + 1)], s_gu, lane)[:n_gu])
        for k, (tiles, width) in enumerate(zip(per_weight, widths)):
            w_ref, m_ref, v_ref = ins[3 * k:3 * k + 3]
            g_ref, d_ref, nm_ref, nv_ref = outs[4 * k:4 * k + 4]
            for t, g in enumerate(tiles):
                n = min(LANES, width - t * LANES)
                sl = slice(t * LANES, t * LANES + n)
                g = g[:, 0:n]
                d, nm, nv = _adamw_math(w_ref[:, sl], g, m_ref[:, sl], v_ref[:, sl])
                g_ref[:, sl] = g
                d_ref[:, sl] = d
                nm_ref[:, sl] = nm
                nv_ref[:, sl] = nv

    def native(width):
        return pl.BlockSpec((None, tm, width), lambda i, p, c: (l, i, 0))

    def native_out(width):
        return pl.BlockSpec((tm, width), lambda i, p, c: (i, 0))

    in_specs = [pl.BlockSpec((None, tm, A_COLS), lambda i, p, c: (c[0], i, 0)),
                pl.BlockSpec((3, tm, A_COLS), lambda i, p, c: (0, i, 0))]
    in_specs += [native(wd) for wd in widths for _ in range(3)] + [_after_spec(t) for t in after]
    out_specs = [native_out(wd) for wd in widths for _ in range(4)]
    out_shape = [jax.ShapeDtypeStruct((D_MODEL, wd), F32) for wd in widths for _ in range(4)]
    grid_spec = pltpu.PrefetchScalarGridSpec(num_scalar_prefetch=2, grid=(D_MODEL // tm,), in_specs=in_specs,
                                             out_specs=out_specs)
    out = pl.pallas_call(body, grid_spec=grid_spec, out_shape=out_shape, name=name,
                         compiler_params=_cparams(("parallel",)))(prm, chip, a, recv, *params, *after)
    return [out[4 * k:4 * k + 4] for k in range(3)]


def _me():
    return lax.axis_index("x"), lax.axis_index("y"), lax.axis_index("c")


_CHIP_FLIPS = ((1, 0), (0, 1), (1, 1))
_ANY = pl.BlockSpec(memory_space=pl.ANY)


def _comm_call(body, peers, out_shape, sems, name, args, collective_id):
    if collective_id is None:
        n_in = len(args)
        return pl.pallas_call(body, out_shape=out_shape, name=name, in_specs=[_ANY] * n_in,
                              out_specs=[_ANY] * len(out_shape), scratch_shapes=sems)(*args)

    def sequencer_body(*refs):
        barrier = pltpu.get_barrier_semaphore()
        targets = peers()
        for peer in targets:
            pl.semaphore_signal(barrier, inc=1, device_id=peer, device_id_type=MESH)
        pl.semaphore_wait(barrier, len(targets))
        body(*refs)

    sequencer = plsc.ScalarSubcoreMesh(axis_name="seq", num_cores=1)
    return pl.kernel(sequencer_body, out_type=out_shape, mesh=sequencer, scratch_types=sems, name=name,
                     compiler_params=pltpu.CompilerParams(collective_id=collective_id))(*args)


def _sibling_peer():
    x, y, cc = _me()
    return [(x, y, 1 - cc)]


def _chip_peers():
    x, y, cc = _me()
    return [(x ^ fx, y ^ fy, cc) for fx, fy in _CHIP_FLIPS]


def _all_gather(shards, name, collective_id=None):
    n = len(shards)

    def body(*refs):
        x_refs, out_refs = refs[:n], refs[n:2 * n]
        send_sems, recv_sems, local_sems = refs[2 * n:]
        x, y, cc = _me()
        sibling = (x, y, 1 - cc)
        chips = [(x ^ fx, y ^ fy) for fx, fy in _CHIP_FLIPS]

        def copy(a, k, block, to, from_shard=False):
            px, py, pc = block
            slot = out_refs[a].at[4 * px + 2 * py + pc]
            return pltpu.make_async_remote_copy(
                src_ref=x_refs[a] if from_shard else slot, dst_ref=slot,
                send_sem=send_sems.at[a, k], recv_sem=recv_sems.at[a, k], device_id=to, device_id_type=MESH)

        mine = [pltpu.make_async_copy(x_refs[a], out_refs[a].at[4 * x + 2 * y + cc], local_sems.at[a])
                for a in range(n)]
        for cp in mine:
            cp.start()
        first = [copy(a, 0, (x, y, cc), sibling, True) for a in range(n)]
        first += [copy(a, 1 + j, (x, y, cc), (*chip, cc), True) for j, chip in enumerate(chips) for a in range(n)]
        for cp in first:
            cp.start()
        passed = []
        for j, chip in enumerate(chips):
            for a in range(n):
                copy(a, 1 + j, (*chip, cc), (x, y, cc)).wait_recv()
                passed.append(copy(a, 4 + j, (*chip, cc), sibling))
                passed[-1].start()
        for a in range(n):
            copy(a, 0, sibling, (x, y, cc)).wait_recv()
            for j, chip in enumerate(chips):
                copy(a, 4 + j, (*chip, 1 - cc), (x, y, cc)).wait_recv()
        for cp in first + passed:
            cp.wait_send()
        for cp in mine:
            cp.wait()

    return _comm_call(
        body, lambda: _sibling_peer() + _chip_peers(),
        [jax.ShapeDtypeStruct((N_DEV,) + s.shape, s.dtype) for s in shards],
        [pltpu.SemaphoreType.DMA((n, 7)), pltpu.SemaphoreType.DMA((n, 7)), pltpu.SemaphoreType.DMA((n,))],
        name, shards, collective_id)


def _send_to_sibling(parts, name, collective_id=None):
    n = len(parts)

    def body(*refs):
        g_refs, out_refs = refs[:n], refs[n:2 * n]
        send_sems, recv_sems = refs[2 * n:]
        x, y, cc = _me()
        copies = [pltpu.make_async_remote_copy(
            src_ref=g_refs[a].at[1 - cc], dst_ref=out_refs[a], send_sem=send_sems.at[a], recv_sem=recv_sems.at[a],
            device_id=(x, y, 1 - cc), device_id_type=MESH) for a in range(n)]
        for cp in copies:
            cp.start()
        for cp in copies:
            cp.wait()

    return _comm_call(
        body, _sibling_peer, [jax.ShapeDtypeStruct(p.shape[1:], p.dtype) for p in parts],
        [pltpu.SemaphoreType.DMA((n,)), pltpu.SemaphoreType.DMA((n,))], name, parts, collective_id)


def _send_to_chips(parts, name, collective_id=None):
    n = len(parts)

    def body(*refs):
        a_refs, out_refs = refs[:n], refs[n:2 * n]
        send_sems, recv_sems = refs[2 * n:]
        x, y, cc = _me()
        copies = []
        for k, (fx, fy) in enumerate(_CHIP_FLIPS):
            px, py = x ^ fx, y ^ fy
            for a in range(n):
                copies.append(pltpu.make_async_remote_copy(
                    src_ref=a_refs[a].at[2 * px + py], dst_ref=out_refs[a].at[k], send_sem=send_sems.at[a, k],
                    recv_sem=recv_sems.at[a, k], device_id=(px, py, cc), device_id_type=MESH))
                copies[-1].start()
        for cp in copies:
            cp.wait()

    return _comm_call(
        body, _chip_peers, [jax.ShapeDtypeStruct((3,) + p.shape[1:], p.dtype) for p in parts],
        [pltpu.SemaphoreType.DMA((n, 3)), pltpu.SemaphoreType.DMA((n, 3))], name, parts, collective_id)


def _pack_rows(w_oa, w_ob, w_o, w_down, w_fg2, conv_w, l):
    conv_bits = lax.bitcast_convert_type(conv_w[l].reshape(-1), BF16).reshape(1, -1)
    tail = jnp.concatenate([w_fg2[l].astype(BF16).reshape(1, D_MODEL),
                            jnp.pad(conv_bits, ((0, 0), (0, D_MODEL - conv_bits.shape[1])))], axis=0)
    tail = jnp.pad(tail, ((0, B_ROWS - B_FG2 - tail.shape[0]), (0, 0)))
    return jnp.concatenate([w_oa[l].astype(BF16), w_ob[l].astype(BF16), w_o[l].astype(BF16),
                            w_down[l].astype(BF16), tail], axis=0)


def _unpack_tail(tail):
    w_fg2 = tail[:, 0, :].reshape(N_DEV, RANK, KEY_W // N_DEV).transpose(1, 0, 2).reshape(RANK, KEY_W)
    conv_bits = tail[:, B_CONV - B_FG2, :2 * 3 * ROW_SHARD].reshape(N_DEV, 3 * ROW_SHARD, 2)
    conv_w = lax.bitcast_convert_type(conv_bits, F32).reshape(N_DEV, 3, ROW_SHARD)
    return jnp.pad(w_fg2, ((0, LANES - RANK), (0, 0))), conv_w.transpose(1, 0, 2).reshape(3, D_MODEL)


def _by_core_chip(t):
    return t.reshape((2, 2, 2) + t.shape[1:]).transpose((2, 0, 1) + tuple(range(3, t.ndim + 2))).reshape(
        (2, 4) + t.shape[1:])


def _grad_rows(g):
    fg2 = g["w_fg2"][:RANK].reshape(RANK, N_DEV, KEY_W // N_DEV).transpose(1, 0, 2).reshape(N_DEV, 1, D_MODEL)
    conv = g["conv_w"].astype(BF16).reshape(3, N_DEV, ROW_SHARD).transpose(1, 0, 2).reshape(N_DEV, 1, 3 * ROW_SHARD)
    tail = jnp.concatenate([fg2, jnp.pad(conv, ((0, 0), (0, 0), (0, D_MODEL - 3 * ROW_SHARD)))], axis=1)
    tail = jnp.pad(tail, ((0, 0), (0, B_ROWS - B_FG2 - 2), (0, 0)))
    parts = [g["w_oa"].reshape(N_DEV, ROW_SHARD, D_MODEL), g["w_ob"].reshape(N_DEV, ROW_SHARD, D_MODEL),
             g["w_o"].reshape(N_DEV, ROW_SHARD, D_MODEL), g["w_down"].reshape(N_DEV, FFN_SHARD, D_MODEL), tail]
    return _by_core_chip(jnp.concatenate(parts, axis=1))


def _ungrad_rows(gs):
    return dict(w_oa=gs[B_OA:B_OA + ROW_SHARD], w_ob=gs[B_OB:B_OB + ROW_SHARD], w_o=gs[B_O:B_O + ROW_SHARD],
                w_ffn_down=gs[B_DOWN:B_DOWN + FFN_SHARD], w_fg2=gs[B_FG2].reshape(RANK, KEY_W // N_DEV),
                conv_w=gs[B_CONV, :3 * ROW_SHARD].reshape(3, ROW_SHARD))


def _layer_fwd(x, p, l):
    tag = f"l{l}_"
    h = _rmsnorm_fwd(x, p["norm1_g"], tag + "norm1")
    proj = _matmul(h, p["w_in"], "nn", BF16, tag + "proj")
    o, states = _gla_fwd(proj, p["w_fg2"], p["b_fg"], tag + "gla_fwd")
    oa = _gla_post_fwd(o, proj, p["gla_norm_g"], tag + "gla_post")
    ya = _matmul(oa, p["w_oa"], "nn", BF16, tag + "ya")
    cb = _conv_fwd(proj, p["conv_w"], p["conv_b"], tag + "conv")
    yb = _matmul(cb, p["w_ob"], "nn", BF16, tag + "yb")
    mix = _mix_fwd(proj, ya, yb, tag + "mix")
    x1 = _matmul(mix, p["w_o"], "nn", F32, tag + "x1", residual=x)
    h2 = _rmsnorm_fwd(x1, p["norm2_g"], tag + "norm2")
    gu = _matmul(h2, p["w_gu"], "nn", BF16, tag + "gu")
    hid = _swiglu_fwd(gu, tag + "swiglu")
    x2 = _matmul(hid, p["w_down"], "nn", F32, tag + "x2", residual=x1)
    saved = dict(x=x, h=h, proj=proj, o=o, states=states, oa=oa, ya=ya, cb=cb, yb=yb, mix=mix, x1=x1, h2=h2,
                 gu=gu, hid=hid)
    return x2, saved


def _layer_bwd(dx2, p, sv, l, after_first=(), after_mid=()):
    tag = f"l{l}_b_"
    dx2h = dx2.astype(BF16)
    dhid = _matmul(dx2h, p["w_down"], "nt", BF16, tag + "dhid", after=after_first)
    d_down = _matmul(sv["hid"], dx2h, "tn", BF16, tag + "dw_down")
    dgu = _swiglu_bwd(dhid, sv["gu"], tag + "swiglu")
    dh2 = _matmul(dgu, p["w_gu"], "nt", F32, tag + "dh2", after=after_mid)
    d_gu = _matmul(sv["h2"], dgu, "tn", BF16, tag + "dw_gu")
    dx1, dg2 = _rmsnorm_bwd(sv["x1"], p["norm2_g"], dh2, dx2, tag + "norm2")
    dx1h = dx1.astype(BF16)
    dmix = _matmul(dx1h, p["w_o"], "nt", BF16, tag + "dmix")
    d_o = _matmul(sv["mix"], dx1h, "tn", BF16, tag + "dw_o")
    dgab, dya, dyb = _mix_bwd(dmix, sv["proj"], sv["ya"], sv["yb"], tag + "mix")
    dcb = _matmul(dyb, p["w_ob"], "nt", BF16, tag + "dcb")
    d_ob = _matmul(sv["cb"], dyb, "tn", BF16, tag + "dw_ob")
    d3, dwb = _conv_bwd(dcb, sv["proj"], p["conv_w"], p["conv_b"], tag + "conv")
    doa = _matmul(dya, p["w_oa"], "nt", BF16, tag + "doa")
    d_oa = _matmul(sv["oa"], dya, "tn", BF16, tag + "dw_oa")
    dr, do, dgg = _gla_post_bwd(doa, sv["o"], sv["proj"], p["gla_norm_g"], tag + "gla_post")
    dq, dk, dv, dfg, dbfg = _gla_bwd(sv["proj"], p["w_fg2"], p["b_fg"], sv["states"], do, tag + "gla")
    fz = sv["proj"][:, FZ_COL:]
    dfz = _matmul(dfg, p["w_fg2"], "nt", BF16, tag + "dfz")
    d_fg2 = _matmul(fz, dfg, "tn", BF16, tag + "dw_fg2")
    dproj = jnp.concatenate([dq, dk, dv, dr, d3, dgab, dfz], axis=1)
    dh = _matmul(dproj, p["w_in"], "nt", F32, tag + "dh")
    d_in = _matmul(sv["h"], dproj, "tn", BF16, tag + "dw_in")
    dx, dg1 = _rmsnorm_bwd(sv["x"], p["norm1_g"], dh, dx1, tag + "norm1")
    big = dict(w_in=d_in, w_fg2=d_fg2, conv_w=dwb[0:3], w_oa=d_oa, w_ob=d_ob, w_o=d_o, w_gu=d_gu, w_down=d_down)
    pad = lambda t: jnp.pad(t, ((0, 0), (0, D_MODEL - t.shape[1])))
    small = [dg1[0:1], pad(dbfg[0:1]), pad(dgg[0:1]), dwb[3:4], dg2[0:1]]
    return dx, big, small


def _local_step(x, target, weights_of, final_g, on_layer_grads=None):
    saved, layers = [], []
    for l in range(DEPTH):
        layers.append(weights_of(l, x))
        x, sv = _layer_fwd(x, layers[l], l)
        saved.append(sv)
    loss, dx, dgf = _loss_head(x, final_g, target, "loss_head")
    bigs, smalls = [None] * DEPTH, [None] * DEPTH
    first, mid = (), ()
    for l in reversed(range(DEPTH)):
        dx, bigs[l], smalls[l] = _layer_bwd(dx, layers[l], saved[l], l, first, mid)
        if on_layer_grads is not None:
            first, mid = on_layer_grads(l, bigs[l])
    small = jnp.concatenate(smalls[0] + smalls[1] + [dgf[0:1]], axis=0)
    small = jnp.pad(small, ((0, SMALL_ROWS - small.shape[0]), (0, 0)))
    return loss[0, 0], dx, bigs, small


def kernel(x, norm1_g, w_in, w_fg2, b_fg, gla_norm_g, w_oa, conv_w, conv_b, w_ob, w_o, norm2_g, w_ffn_gate, w_ffn_up, w_ffn_down, final_g, loss_target, m_norm1_g, m_w_in, m_w_fg2, m_b_fg, m_gla_norm_g, m_w_oa, m_conv_w, m_conv_b, m_w_ob, m_w_o, m_norm2_g, m_w_ffn_gate, m_w_ffn_up, m_w_ffn_down, m_final_g, v_norm1_g, v_w_in, v_w_fg2, v_b_fg, v_gla_norm_g, v_w_oa, v_conv_w, v_conv_b, v_w_ob, v_w_o, v_norm2_g, v_w_ffn_gate, v_w_ffn_up, v_w_ffn_down, v_final_g):
    names = ["norm1_g", "w_in", "w_fg2", "b_fg", "gla_norm_g", "w_oa", "conv_w", "conv_b", "w_ob", "w_o",
             "norm2_g", "w_ffn_gate", "w_ffn_up", "w_ffn_down", "final_g"]
    w = dict(zip(names, [norm1_g, w_in, w_fg2, b_fg, gla_norm_g, w_oa, conv_w, conv_b, w_ob, w_o, norm2_g,
                         w_ffn_gate, w_ffn_up, w_ffn_down, final_g]))
    m = dict(zip(names, [m_norm1_g, m_w_in, m_w_fg2, m_b_fg, m_gla_norm_g, m_w_oa, m_conv_w, m_conv_b, m_w_ob,
                         m_w_o, m_norm2_g, m_w_ffn_gate, m_w_ffn_up, m_w_ffn_down, m_final_g]))
    v = dict(zip(names, [v_norm1_g, v_w_in, v_w_fg2, v_b_fg, v_gla_norm_g, v_w_oa, v_conv_w, v_conv_b, v_w_ob,
                         v_w_o, v_norm2_g, v_w_ffn_gate, v_w_ffn_up, v_w_ffn_down, v_final_g]))
    col_names = ["w_in", "w_ffn_gate", "w_ffn_up"]
    cx, cy, cc = _me()
    prm = jnp.asarray(SHIFT_TABLE)[4 * cx + 2 * cy + cc]
    core = jnp.reshape(cc, (1,)).astype(jnp.int32)
    chip = jnp.reshape(2 * cx + cy, (1,)).astype(jnp.int32)

    gathered = []
    for l in range(DEPTH):
        windows = _window_cols(prm, w_in, w_ffn_gate, w_ffn_up, l, f"l{l}_windows",
                               after=gathered[-1][0] if gathered else None)
        rows = _pack_rows(w_oa, w_ob, w_o, w_ffn_down, w_fg2, conv_w, l)
        gathered.append(_all_gather([windows, rows], f"l{l}_gather", collective_id=l))

    def weights_of(l, x_in):
        all_windows, all_rows = gathered[l]
        w_in_full, w_gu, w_oa_full, w_ob_full, w_o_full, w_down, tail = _assemble(
            all_windows, all_rows, f"l{l}_assemble", after=x_in if l > 0 else None)
        w_fg2_full, conv_w_full = _unpack_tail(tail)
        return dict(w_in=w_in_full, w_gu=w_gu, w_oa=w_oa_full, w_ob=w_ob_full, w_o=w_o_full, w_down=w_down,
                    w_fg2=w_fg2_full, conv_w=conv_w_full, norm1_g=norm1_g[l][None], b_fg=b_fg[l][None],
                    gla_norm_g=gla_norm_g[l][None], conv_b=conv_b[l][None], norm2_g=norm2_g[l][None])

    col_params = [t for n in col_names for t in (w[n], m[n], v[n])]
    pending = [None] * DEPTH

    def reduce_layer(l, big):
        parts = [_grad_windows(big["w_in"], big["w_gu"], f"l{l}_grad_windows"), _grad_rows(big)]
        from_sibling = _send_to_sibling(parts, f"l{l}_grads_to_sibling", collective_id=DEPTH + l)
        sums = [_pair_sum(part, got, core, f"l{l}_pair_sum{k}")
                for k, (part, got) in enumerate(zip(parts, from_sibling))]
        pending[l] = (sums, _send_to_chips(sums, f"l{l}_grads_to_chips", collective_id=2 * DEPTH + l))
        return (parts[0],), tuple(sums)

    loss, dx, bigs, small = _local_step(x[0], loss_target[0], weights_of, final_g[None], reduce_layer)

    row_grads, col_out = [None] * DEPTH, [None] * DEPTH
    for l in reversed(range(DEPTH)):
        sums, from_chips = pending[l]
        after = (bigs[0]["w_in"],) if l > 0 else ()
        row_grads[l] = _ungrad_rows(_chip_sum(sums[1], from_chips[1], chip, f"l{l}_chip_sum", after))
        col_out[l] = _final_cols(prm, chip, sums[0], from_chips[0], col_params, l, f"l{l}_final_cols", after)

    grads, deltas, new_m, new_v = {}, {}, {}, {}
    for k, n in enumerate(col_names):
        grads[n], deltas[n], new_m[n], new_v[n] = [jnp.stack([col_out[l][k][q] for l in range(DEPTH)])
                                                   for q in range(4)]
    for n in row_grads[0]:
        grads[n] = jnp.stack([row_grads[l][n] for l in range(DEPTH)])

    small_sum = _sum_devices(_all_gather([small], "gather_small")[0], "sum_small")
    r512, r256 = slice(0, KEY_W), slice(0, HEAD_V)
    grads.update(
        norm1_g=jnp.stack([small_sum[0], small_sum[5]]), b_fg=jnp.stack([small_sum[1, r512], small_sum[6, r512]]),
        gla_norm_g=jnp.stack([small_sum[2, r256], small_sum[7, r256]]),
        conv_b=jnp.stack([small_sum[3], small_sum[8]]), norm2_g=jnp.stack([small_sum[4], small_sum[9]]),
        final_g=small_sum[10])

    for n in names:
        if n not in col_names:
            deltas[n], new_m[n], new_v[n] = _adamw(w[n], grads[n], m[n], v[n], "adamw_" + n)

    total_loss = lax.psum(loss, ("x", "y", "c"))
    return (total_loss, dx[None], *[grads[n] for n in names], *[deltas[n] for n in names],
            *[new_m[n] for n in names], *[new_v[n] for n in names])
```

```python
import functools

import jax
import jax.numpy as jnp
import numpy as np
from jax import lax
from jax.experimental import pallas as pl
from jax.experimental.pallas import tpu as pltpu
from jax.experimental.pallas import tpu_sc as plsc

F32 = jnp.float32
BF16 = jnp.bfloat16
MESH = pl.DeviceIdType.MESH

D_MODEL = 1024
DEPTH = 2
CHUNK = 64
HEADS = 4
HEAD_K = 128
HEAD_V = 256
KEY_W = HEADS * HEAD_K
VAL_W = HEADS * HEAD_V
RANK = 16
TAU = 16.0
FFN = 2816
IN_WIDTH = 2 * KEY_W + 2 * VAL_W + RANK + 5 * D_MODEL
EPS = 1e-6
Q_SCALE = HEAD_K ** -0.5
N_DEV = 8
ADAM_LR, ADAM_B1, ADAM_B2, ADAM_EPS, ADAM_WD, ADAM_STEP = 0.001, 0.9, 0.999, 1e-08, 0.01, 10

LANES = 128
SUBLANES_BF16 = 16
VMEM_LIMIT = 48 * 1024 * 1024

FZ_COL = 2 * KEY_W + 2 * VAL_W + 5 * D_MODEL
PROJ_W = FZ_COL + LANES
SEG_R, SEG_GBI, SEG_GCI, SEG_CX, SEG_GA, SEG_GB = 2, 3, 4, 5, 6, 7

W_IN_SHARD = IN_WIDTH // N_DEV
FFN_SHARD = FFN // N_DEV
ROW_SHARD = D_MODEL // N_DEV

WIN_IN = 9 * LANES
WIN_GU = 4 * LANES
A_FZ = WIN_IN
IN_COLS = A_FZ + LANES
GU_COLS = 2 * WIN_GU
ORIG_FZ = 2 * KEY_W + 2 * VAL_W


def _new_col(o):
    if o < ORIG_FZ:
        return o
    if o < ORIG_FZ + RANK:
        return FZ_COL + (o - ORIG_FZ)
    return o - RANK


def _shift_table():
    t_in, rows = [], []
    for j in range(N_DEV):
        new = [_new_col(W_IN_SHARD * j + i) for i in range(W_IN_SHARD)]
        main = [i for i in range(W_IN_SHARD) if new[i] < FZ_COL]
        code = [i for i in range(W_IN_SHARD) if new[i] >= FZ_COL]
        shift = new[main[0]] - main[0]
        t_in.append(shift // LANES)
        assert all(new[i] - i == shift for i in main) and shift % LANES + W_IN_SHARD <= WIN_IN
        if code:
            cshift = new[code[0]] - FZ_COL - code[0]
            crow = [cshift % LANES, code[0], code[-1] + 1, int(cshift < 0)]
        else:
            crow = [0, 0, 0, 0]
        rows.append([shift % LANES, main[0], main[-1] + 1] + crow + [FFN_SHARD * j % LANES])
    return tuple(t_in), np.asarray(rows, np.int32)


T_IN, SHIFT_TABLE = _shift_table()
T_GU = tuple(FFN_SHARD * j // LANES for j in range(N_DEV))

B_OA, B_OB, B_O, B_DOWN = 0, ROW_SHARD, 2 * ROW_SHARD, 3 * ROW_SHARD
B_FG2 = B_DOWN + FFN_SHARD
B_CONV = B_FG2 + 1
B_ROWS = B_FG2 + SUBLANES_BF16
SMALL_ROWS = 16


def _pick(n, candidates):
    for c in candidates:
        if n % c == 0:
            return c
    return n


def _cparams(sem):
    return pltpu.CompilerParams(dimension_semantics=sem, vmem_limit_bytes=VMEM_LIMIT)


def _sigmoid(x):
    return 1.0 / (1.0 + jnp.exp(-x))


def _matmul(a, b, dims, out_dtype, name, residual=None, after=()):
    if dims == "nn":
        (m, k), (k2, n) = a.shape, b.shape
    elif dims == "nt":
        (m, k), (n, k2) = a.shape, b.shape
    else:
        (k, m), (k2, n) = a.shape, b.shape
    assert k == k2, (a.shape, b.shape, dims)
    tm = _pick(m, (512, 256, 128))
    tn = _pick(n, (1664, 1408, 1024, 512, 256, 128))
    tk = _pick(k, (1664, 1408, 1024, 512, 256, 128))
    nk = k // tk
    if dims == "nn":
        a_spec = pl.BlockSpec((tm, tk), lambda i, j, kk: (i, kk))
        b_spec = pl.BlockSpec((tk, tn), lambda i, j, kk: (kk, j))
        contract = (((1,), (0,)), ((), ()))
    elif dims == "nt":
        a_spec = pl.BlockSpec((tm, tk), lambda i, j, kk: (i, kk))
        b_spec = pl.BlockSpec((tn, tk), lambda i, j, kk: (j, kk))
        contract = (((1,), (1,)), ((), ()))
    else:
        a_spec = pl.BlockSpec((tk, tm), lambda i, j, kk: (kk, i))
        b_spec = pl.BlockSpec((tk, tn), lambda i, j, kk: (kk, j))
        contract = (((0,), (0,)), ((), ()))
    o_spec = pl.BlockSpec((tm, tn), lambda i, j, kk: (i, j))
    has_res = residual is not None

    def body(*refs):
        a_ref, b_ref = refs[:2]
        r_ref = refs[2] if has_res else None
        o_ref, acc_ref = refs[-2:]
        kk = pl.program_id(2)
        part = lax.dot_general(a_ref[...], b_ref[...], contract, preferred_element_type=F32)

        @pl.when(kk == 0)
        def _():
            acc_ref[...] = part

        @pl.when(kk > 0)
        def _():
            acc_ref[...] += part

        @pl.when(kk == nk - 1)
        def _():
            res = acc_ref[...]
            if has_res:
                res = res + r_ref[...]
            o_ref[...] = res.astype(o_ref.dtype)

    in_specs = [a_spec, b_spec] + ([o_spec] if has_res else []) + [_after_spec(t) for t in after]
    args = (a, b) + ((residual,) if has_res else ()) + tuple(after)
    return pl.pallas_call(
        body, grid=(m // tm, n // tn, nk), in_specs=in_specs, out_specs=o_spec,
        out_shape=jax.ShapeDtypeStruct((m, n), out_dtype),
        scratch_shapes=[pltpu.VMEM((tm, tn), F32)], name=name,
        compiler_params=_cparams(("parallel", "parallel", "arbitrary")),
    )(*args)


def _rmsnorm_fwd(x, g, name):
    s, d = x.shape
    tm = _pick(s, (512, 256))

    def body(x_ref, g_ref, o_ref):
        xv = x_ref[...]
        r = lax.rsqrt(jnp.mean(xv * xv, axis=-1, keepdims=True) + EPS)
        o_ref[...] = (xv * r * g_ref[...]).astype(o_ref.dtype)

    row = pl.BlockSpec((tm, d), lambda i: (i, 0))
    return pl.pallas_call(
        body, grid=(s // tm,), in_specs=[row, pl.BlockSpec((1, d), lambda i: (0, 0))], out_specs=row,
        out_shape=jax.ShapeDtypeStruct((s, d), BF16), name=name, compiler_params=_cparams(("parallel",)),
    )(x, g)


def _rmsnorm_bwd(x, g, dh, dres, name, after=()):
    s, d = x.shape
    tm = _pick(s, (512, 256))

    def body(x_ref, g_ref, dh_ref, dres_ref, *rest):
        dx_ref, dg_ref = rest[-2:]
        xv = x_ref[...]
        r = lax.rsqrt(jnp.mean(xv * xv, axis=-1, keepdims=True) + EPS)
        xn = xv * r
        dhv = dh_ref[...].astype(F32)
        dxn = dhv * g_ref[...]
        dx_ref[...] = dres_ref[...] + r * (dxn - xn * jnp.mean(dxn * xn, axis=-1, keepdims=True))

        @pl.when(pl.program_id(0) == 0)
        def _():
            dg_ref[...] = jnp.zeros_like(dg_ref)

        dg_ref[...] += jnp.broadcast_to(jnp.sum(dhv * xn, axis=0, keepdims=True), dg_ref.shape)

    row = pl.BlockSpec((tm, d), lambda i: (i, 0))
    acc = pl.BlockSpec((8, d), lambda i: (0, 0))
    return pl.pallas_call(
        body, grid=(s // tm,),
        in_specs=[row, pl.BlockSpec((1, d), lambda i: (0, 0)), row, row] + [_after_spec(t) for t in after],
        out_specs=[row, acc],
        out_shape=[jax.ShapeDtypeStruct((s, d), F32), jax.ShapeDtypeStruct((8, d), F32)], name=name,
        compiler_params=_cparams(("arbitrary",)),
    )(x, g, dh, dres, *after)


def _loss_head(x, g, target, name):
    s, d = x.shape
    tm = _pick(s, (512, 256))

    def body(x_ref, g_ref, t_ref, loss_ref, dx_ref, dg_ref):
        xv = x_ref[...]
        gv = g_ref[...]
        r = lax.rsqrt(jnp.mean(xv * xv, axis=-1, keepdims=True) + EPS)
        xn = xv * r
        err = xn * gv - t_ref[...]
        dy = err * (1.0 / d)
        dxn = dy * gv
        dx_ref[...] = r * (dxn - xn * jnp.mean(dxn * xn, axis=-1, keepdims=True))

        @pl.when(pl.program_id(0) == 0)
        def _():
            dg_ref[...] = jnp.zeros_like(dg_ref)
            loss_ref[...] = jnp.zeros_like(loss_ref)

        dg_ref[...] += jnp.broadcast_to(jnp.sum(dy * xn, axis=0, keepdims=True), dg_ref.shape)
        row_loss = jnp.sum(err * err, axis=-1, keepdims=True)
        loss_ref[...] += jnp.broadcast_to((0.5 / d) * jnp.sum(row_loss, axis=0, keepdims=True), loss_ref.shape)

    row = pl.BlockSpec((tm, d), lambda i: (i, 0))
    return pl.pallas_call(
        body, grid=(s // tm,), in_specs=[row, pl.BlockSpec((1, d), lambda i: (0, 0)), row],
        out_specs=[pl.BlockSpec((8, LANES), lambda i: (0, 0)), row, pl.BlockSpec((8, d), lambda i: (0, 0))],
        out_shape=[jax.ShapeDtypeStruct((8, LANES), F32), jax.ShapeDtypeStruct((s, d), F32),
                   jax.ShapeDtypeStruct((8, d), F32)],
        name=name, compiler_params=_cparams(("arbitrary",)),
    )(x, g, target)


def _tri_dot(tri, x):
    hi = x.astype(BF16)
    lo = (x - hi.astype(F32)).astype(BF16)
    return jnp.dot(tri, hi, preferred_element_type=F32) + jnp.dot(tri, lo, preferred_element_type=F32)


def _tri(strict):
    r = lax.broadcasted_iota(jnp.int32, (CHUNK, CHUNK), 0)
    c = lax.broadcasted_iota(jnp.int32, (CHUNK, CHUNK), 1)
    return jnp.where((r > c) if strict else (r >= c), 1.0, 0.0).astype(BF16)


def _chunk_decay(fz, w, b, tri):
    fg = jnp.dot(fz, w, preferred_element_type=F32) + b
    la = (jnp.minimum(fg, 0.0) - jnp.log(1.0 + jnp.exp(-jnp.abs(fg)))) * (1.0 / TAU)
    cum = _tri_dot(tri, la)
    cum_end = cum[CHUNK - 1:CHUNK, :]
    return fg, jnp.exp(cum_end - cum), jnp.exp(cum_end)


_TN = (((0,), (0,)), ((), ()))
_NT = (((1,), (1,)), ((), ()))


def _gla_specs(rows):
    q_spec = pl.BlockSpec((rows, HEAD_K), lambda h, c: (c, h))
    k_spec = pl.BlockSpec((rows, HEAD_K), lambda h, c: (c, HEADS + h))
    v_spec = pl.BlockSpec((rows, HEAD_V), lambda h, c: (c, HEADS + h))
    fz_spec = pl.BlockSpec((rows, LANES), lambda h, c: (c, FZ_COL // LANES))
    w_spec = pl.BlockSpec((LANES, HEAD_K), lambda h, c: (0, h))
    b_spec = pl.BlockSpec((1, HEAD_K), lambda h, c: (0, h))
    return q_spec, k_spec, v_spec, fz_spec, w_spec, b_spec


def _gla_fwd(proj, wfg, bfg, name):
    s = proj.shape[0]
    nc = s // CHUNK
    per = _pick(nc, (8, 4, 2, 1))
    rows = per * CHUNK

    def body(q_ref, k_ref, v_ref, fz_ref, w_ref, b_ref, o_ref, st_ref, state):
        @pl.when(pl.program_id(1) == 0)
        def _():
            state[...] = jnp.zeros_like(state)

        tri = _tri(False)
        w = w_ref[...]
        b = b_ref[...]
        for i in range(per):
            sl = pl.ds(i * CHUNK, CHUNK)
            _, dec, gamma = _chunk_decay(fz_ref[sl, :], w, b, tri)
            kd = (k_ref[sl, :].astype(F32) * dec).astype(BF16)
            st = state[...] * gamma + lax.dot_general(v_ref[sl, :], kd, _TN, preferred_element_type=F32)
            state[...] = st
            st16 = st.astype(BF16)
            st_ref[0, i] = st16
            qs = (q_ref[sl, :].astype(F32) * Q_SCALE).astype(BF16)
            o_ref[sl, :] = lax.dot_general(qs, st16, _NT, preferred_element_type=F32).astype(o_ref.dtype)

    q_spec, k_spec, v_spec, fz_spec, w_spec, b_spec = _gla_specs(rows)
    return pl.pallas_call(
        body, grid=(HEADS, nc // per),
        in_specs=[q_spec, k_spec, v_spec, fz_spec, w_spec, b_spec],
        out_specs=[pl.BlockSpec((rows, HEAD_V), lambda h, c: (c, h)),
                   pl.BlockSpec((1, per, HEAD_V, HEAD_K), lambda h, c: (h, c, 0, 0))],
        out_shape=[jax.ShapeDtypeStruct((s, VAL_W), BF16),
                   jax.ShapeDtypeStruct((HEADS, nc, HEAD_V, HEAD_K), BF16)],
        scratch_shapes=[pltpu.VMEM((HEAD_V, HEAD_K), F32)], name=name,
        compiler_params=_cparams(("parallel", "arbitrary")),
    )(proj, proj, proj, proj, wfg, bfg)


def _gla_bwd(proj, wfg, bfg, states, do, name):
    s = proj.shape[0]
    nc = s // CHUNK
    per = _pick(nc, (8, 4, 2, 1))
    rows = per * CHUNK
    nblk = nc // per

    def rev(spec_fn):
        return lambda h, j: spec_fn(h, nblk - 1 - j)

    def body(q_ref, k_ref, v_ref, fz_ref, w_ref, b_ref, do_ref, st_ref, prev_ref,
             dq_ref, dk_ref, dv_ref, dfg_ref, db_ref, carry):
        j = pl.program_id(1)

        @pl.when(j == 0)
        def _():
            carry[...] = jnp.zeros_like(carry)
            db_ref[...] = jnp.zeros_like(db_ref)

        tri = _tri(False)
        tri_strict = _tri(True)
        w = w_ref[...]
        b = b_ref[...]
        has_prev = (j < nblk - 1).astype(F32)
        db = jnp.zeros((1, HEAD_K), F32)
        for i in reversed(range(per)):
            sl = pl.ds(i * CHUNK, CHUNK)
            fg, dec, gamma = _chunk_decay(fz_ref[sl, :], w, b, tri)
            kd = k_ref[sl, :].astype(F32) * dec
            qs = (q_ref[sl, :].astype(F32) * Q_SCALE).astype(BF16)
            dov = do_ref[sl, :]
            v = v_ref[sl, :]
            gt = carry[...] + lax.dot_general(dov, qs, _TN, preferred_element_type=F32)
            gt16 = gt.astype(BF16)
            dq_ref[sl, :] = (jnp.dot(dov, st_ref[0, i], preferred_element_type=F32) * Q_SCALE).astype(dq_ref.dtype)
            dkd = jnp.dot(v, gt16, preferred_element_type=F32)
            dv_ref[sl, :] = lax.dot_general(kd.astype(BF16), gt16, _NT,
                                            preferred_element_type=F32).astype(dv_ref.dtype)
            if i > 0:
                st_prev = st_ref[0, i - 1].astype(F32)
            else:
                st_prev = prev_ref[0, 0].astype(F32) * has_prev
            dgamma = jnp.sum(gt * st_prev, axis=0, keepdims=True)
            carry[...] = gt * gamma
            dk_ref[sl, :] = (dkd * dec).astype(dk_ref.dtype)
            dla = dgamma * gamma + _tri_dot(tri_strict, dkd * kd)
            dfg = dla * (1.0 / TAU) * _sigmoid(-fg)
            dfg_ref[sl, :] = dfg.astype(dfg_ref.dtype)
            db = db + jnp.sum(dfg, axis=0, keepdims=True)
        db_ref[...] += jnp.broadcast_to(db, db_ref.shape)

    q_spec, k_spec, v_spec, fz_spec, w_spec, b_spec = _gla_specs(rows)
    q_spec, k_spec, v_spec, fz_spec = [
        pl.BlockSpec(sp.block_shape, rev(sp.index_map)) for sp in (q_spec, k_spec, v_spec, fz_spec)]
    do_spec = pl.BlockSpec((rows, HEAD_V), lambda h, j: (nblk - 1 - j, h))
    st_spec = pl.BlockSpec((1, per, HEAD_V, HEAD_K), lambda h, j: (h, nblk - 1 - j, 0, 0))
    prev_spec = pl.BlockSpec((1, 1, HEAD_V, HEAD_K),
                             lambda h, j: (h, jnp.maximum((nblk - 1 - j) * per - 1, 0), 0, 0))
    key_out = pl.BlockSpec((rows, HEAD_K), lambda h, j: (nblk - 1 - j, h))
    return pl.pallas_call(
        body, grid=(HEADS, nblk),
        in_specs=[q_spec, k_spec, v_spec, fz_spec, w_spec, b_spec, do_spec, st_spec, prev_spec],
        out_specs=[key_out, key_out, do_spec, key_out, pl.BlockSpec((8, HEAD_K), lambda h, j: (0, h))],
        out_shape=[jax.ShapeDtypeStruct((s, KEY_W), BF16), jax.ShapeDtypeStruct((s, KEY_W), BF16),
                   jax.ShapeDtypeStruct((s, VAL_W), BF16), jax.ShapeDtypeStruct((s, KEY_W), BF16),
                   jax.ShapeDtypeStruct((8, KEY_W), F32)],
        scratch_shapes=[pltpu.VMEM((HEAD_V, HEAD_K), F32)], name=name,
        compiler_params=_cparams(("parallel", "arbitrary")),
    )(proj, proj, proj, proj, wfg, bfg, do, states, states)


def _seg(tm, seg):
    return pl.BlockSpec((tm, D_MODEL), lambda i: (i, seg))


def _gla_post_fwd(o, proj, g, name):
    s = o.shape[0]
    tm = _pick(s, (512, 256))

    def body(o_ref, r_ref, g_ref, oa_ref):
        gv = g_ref[...]
        for h in range(HEADS):
            sl = slice(h * HEAD_V, (h + 1) * HEAD_V)
            ov = o_ref[:, sl].astype(F32)
            rstd = lax.rsqrt(jnp.mean(ov * ov, axis=-1, keepdims=True) + EPS)
            rv = r_ref[:, sl].astype(F32)
            oa_ref[:, sl] = (ov * rstd * gv * (rv * _sigmoid(rv))).astype(oa_ref.dtype)

    row = pl.BlockSpec((tm, VAL_W), lambda i: (i, 0))
    return pl.pallas_call(
        body, grid=(s // tm,), in_specs=[row, _seg(tm, SEG_R), pl.BlockSpec((1, HEAD_V), lambda i: (0, 0))],
        out_specs=row, out_shape=jax.ShapeDtypeStruct((s, VAL_W), BF16), name=name,
        compiler_params=_cparams(("parallel",)),
    )(o, proj, g)


def _gla_post_bwd(doa, o, proj, g, name):
    s = o.shape[0]
    tm = _pick(s, (512, 256))

    def body(doa_ref, o_ref, r_ref, g_ref, dr_ref, do_ref, dg_ref):
        @pl.when(pl.program_id(0) == 0)
        def _():
            dg_ref[...] = jnp.zeros_like(dg_ref)

        gv = g_ref[...]
        dg = jnp.zeros((1, HEAD_V), F32)
        for h in range(HEADS):
            sl = slice(h * HEAD_V, (h + 1) * HEAD_V)
            ov = o_ref[:, sl].astype(F32)
            rstd = lax.rsqrt(jnp.mean(ov * ov, axis=-1, keepdims=True) + EPS)
            ohat = ov * rstd
            rv = r_ref[:, sl].astype(F32)
            sg = _sigmoid(rv)
            dv = doa_ref[:, sl].astype(F32)
            dr_ref[:, sl] = (dv * ohat * gv * (sg * (1.0 + rv * (1.0 - sg)))).astype(dr_ref.dtype)
            don = dv * (rv * sg)
            dg = dg + jnp.sum(don * ohat, axis=0, keepdims=True)
            dohat = don * gv
            do_ref[:, sl] = (rstd * (dohat - ohat * jnp.mean(dohat * ohat, axis=-1, keepdims=True))
                             ).astype(do_ref.dtype)
        dg_ref[...] += jnp.broadcast_to(dg, dg_ref.shape)

    row = pl.BlockSpec((tm, VAL_W), lambda i: (i, 0))
    return pl.pallas_call(
        body, grid=(s // tm,),
        in_specs=[row, row, _seg(tm, SEG_R), pl.BlockSpec((1, HEAD_V), lambda i: (0, 0))],
        out_specs=[row, row, pl.BlockSpec((8, HEAD_V), lambda i: (0, 0))],
        out_shape=[jax.ShapeDtypeStruct((s, VAL_W), BF16), jax.ShapeDtypeStruct((s, VAL_W), BF16),
                   jax.ShapeDtypeStruct((8, HEAD_V), F32)],
        name=name, compiler_params=_cparams(("arbitrary",)),
    )(doa, o, proj, g)


HALO = SUBLANES_BF16


def _shift_down(u, p1, p2, n, rows):
    rolled = pltpu.roll(u, n, 0)
    if n == 1:
        return jnp.where(rows == 0, p1, rolled)
    return jnp.where(rows == 0, p2, jnp.where(rows == 1, p1, rolled))


def _shift_up(u, n1, n2, n, rows, tm):
    rolled = pltpu.roll(u, tm - n, 0)
    if n == 1:
        return jnp.where(rows == tm - 1, n1, rolled)
    return jnp.where(rows == tm - 2, n1, jnp.where(rows == tm - 1, n2, rolled))


def _conv_terms(gc_ref, cx_ref, gcp_ref, cxp_ref, tm):
    i = pl.program_id(0)
    u = gc_ref[...].astype(F32) * cx_ref[...].astype(F32)
    up = gcp_ref[...].astype(F32) * cxp_ref[...].astype(F32) * (i > 0).astype(F32)
    rows = lax.broadcasted_iota(jnp.int32, (tm, 1), 0)
    u1 = _shift_down(u, up[HALO - 1:HALO, :], up[HALO - 2:HALO - 1, :], 1, rows)
    u2 = _shift_down(u, up[HALO - 1:HALO, :], up[HALO - 2:HALO - 1, :], 2, rows)
    return u, u1, u2, rows


def _prev_halo(tm, seg):
    return pl.BlockSpec((HALO, D_MODEL), lambda i: (jnp.maximum(i * (tm // HALO) - 1, 0), seg))


def _conv_fwd(proj, w, b, name):
    s = proj.shape[0]
    tm = _pick(s, (512, 256))

    def body(gbi_ref, gc_ref, cx_ref, gcp_ref, cxp_ref, w_ref, b_ref, cb_ref):
        u, u1, u2, _ = _conv_terms(gc_ref, cx_ref, gcp_ref, cxp_ref, tm)
        conv = w_ref[0:1, :] * u2 + w_ref[1:2, :] * u1 + w_ref[2:3, :] * u + b_ref[...]
        cb_ref[...] = (gbi_ref[...].astype(F32) * conv).astype(cb_ref.dtype)

    return pl.pallas_call(
        body, grid=(s // tm,),
        in_specs=[_seg(tm, SEG_GBI), _seg(tm, SEG_GCI), _seg(tm, SEG_CX),
                  _prev_halo(tm, SEG_GCI), _prev_halo(tm, SEG_CX),
                  pl.BlockSpec((3, D_MODEL), lambda i: (0, 0)), pl.BlockSpec((1, D_MODEL), lambda i: (0, 0))],
        out_specs=pl.BlockSpec((tm, D_MODEL), lambda i: (i, 0)),
        out_shape=jax.ShapeDtypeStruct((s, D_MODEL), BF16), name=name, compiler_params=_cparams(("parallel",)),
    )(proj, proj, proj, proj, proj, w, b)


def _conv_bwd(dcb, proj, w, b, name):
    s = proj.shape[0]
    tm = _pick(s, (512, 256))
    nt = s // tm

    def body(dcb_ref, gbi_ref, gc_ref, cx_ref, gcp_ref, cxp_ref, dcbn_ref, gbin_ref, w_ref, b_ref,
             d3_ref, dwb_ref):
        i = pl.program_id(0)

        @pl.when(i == 0)
        def _():
            dwb_ref[...] = jnp.zeros_like(dwb_ref)

        u, u1, u2, rows = _conv_terms(gc_ref, cx_ref, gcp_ref, cxp_ref, tm)
        w0, w1, w2 = w_ref[0:1, :], w_ref[1:2, :], w_ref[2:3, :]
        conv = w0 * u2 + w1 * u1 + w2 * u + b_ref[...]
        dcbv = dcb_ref[...].astype(F32)
        gbi = gbi_ref[...].astype(F32)
        dconv = dcbv * gbi
        dnext = dcbn_ref[...].astype(F32) * gbin_ref[...].astype(F32) * (i < nt - 1).astype(F32)
        dc1 = _shift_up(dconv, dnext[0:1, :], dnext[1:2, :], 1, rows, tm)
        dc2 = _shift_up(dconv, dnext[0:1, :], dnext[1:2, :], 2, rows, tm)
        du = w2 * dconv + w1 * dc1 + w0 * dc2
        d3_ref[:, 0:D_MODEL] = (dcbv * conv).astype(d3_ref.dtype)
        d3_ref[:, D_MODEL:2 * D_MODEL] = (du * cx_ref[...].astype(F32)).astype(d3_ref.dtype)
        d3_ref[:, 2 * D_MODEL:3 * D_MODEL] = (du * gc_ref[...].astype(F32)).astype(d3_ref.dtype)
        dwb_ref[0:1, :] += jnp.sum(dconv * u2, axis=0, keepdims=True)
        dwb_ref[1:2, :] += jnp.sum(dconv * u1, axis=0, keepdims=True)
        dwb_ref[2:3, :] += jnp.sum(dconv * u, axis=0, keepdims=True)
        dwb_ref[3:4, :] += jnp.sum(dconv, axis=0, keepdims=True)

    def next_halo(seg_fn):
        return pl.BlockSpec((HALO, D_MODEL), lambda i: (jnp.minimum((i + 1) * (tm // HALO), s // HALO - 1), seg_fn))

    return pl.pallas_call(
        body, grid=(nt,),
        in_specs=[pl.BlockSpec((tm, D_MODEL), lambda i: (i, 0)),
                  _seg(tm, SEG_GBI), _seg(tm, SEG_GCI), _seg(tm, SEG_CX),
                  _prev_halo(tm, SEG_GCI), _prev_halo(tm, SEG_CX),
                  next_halo(0), next_halo(SEG_GBI),
                  pl.BlockSpec((3, D_MODEL), lambda i: (0, 0)), pl.BlockSpec((1, D_MODEL), lambda i: (0, 0))],
        out_specs=[pl.BlockSpec((tm, 3 * D_MODEL), lambda i: (i, 0)), pl.BlockSpec((8, D_MODEL), lambda i: (0, 0))],
        out_shape=[jax.ShapeDtypeStruct((s, 3 * D_MODEL), BF16), jax.ShapeDtypeStruct((8, D_MODEL), F32)],
        name=name, compiler_params=_cparams(("arbitrary",)),
    )(dcb, proj, proj, proj, proj, proj, dcb, proj, w, b)


def _mix_fwd(proj, ya, yb, name):
    s = proj.shape[0]
    tm = _pick(s, (512, 256))

    def body(ga_ref, gb_ref, ya_ref, yb_ref, o_ref):
        o_ref[...] = (_sigmoid(ga_ref[...].astype(F32)) * ya_ref[...].astype(F32)
                      + _sigmoid(gb_ref[...].astype(F32)) * yb_ref[...].astype(F32)).astype(o_ref.dtype)

    row = pl.BlockSpec((tm, D_MODEL), lambda i: (i, 0))
    return pl.pallas_call(
        body, grid=(s // tm,), in_specs=[_seg(tm, SEG_GA), _seg(tm, SEG_GB), row, row], out_specs=row,
        out_shape=jax.ShapeDtypeStruct((s, D_MODEL), BF16), name=name, compiler_params=_cparams(("parallel",)),
    )(proj, proj, ya, yb)


def _mix_bwd(dmix, proj, ya, yb, name):
    s = proj.shape[0]
    tm = _pick(s, (512, 256))

    def body(dm_ref, ga_ref, gb_ref, ya_ref, yb_ref, dg_ref, dya_ref, dyb_ref):
        dm = dm_ref[...].astype(F32)
        sa = _sigmoid(ga_ref[...].astype(F32))
        sb = _sigmoid(gb_ref[...].astype(F32))
        dg_ref[:, 0:D_MODEL] = (dm * ya_ref[...].astype(F32) * sa * (1.0 - sa)).astype(dg_ref.dtype)
        dg_ref[:, D_MODEL:2 * D_MODEL] = (dm * yb_ref[...].astype(F32) * sb * (1.0 - sb)).astype(dg_ref.dtype)
        dya_ref[...] = (dm * sa).astype(dya_ref.dtype)
        dyb_ref[...] = (dm * sb).astype(dyb_ref.dtype)

    row = pl.BlockSpec((tm, D_MODEL), lambda i: (i, 0))
    return pl.pallas_call(
        body, grid=(s // tm,), in_specs=[row, _seg(tm, SEG_GA), _seg(tm, SEG_GB), row, row],
        out_specs=[pl.BlockSpec((tm, 2 * D_MODEL), lambda i: (i, 0)), row, row],
        out_shape=[jax.ShapeDtypeStruct((s, 2 * D_MODEL), BF16), jax.ShapeDtypeStruct((s, D_MODEL), BF16),
                   jax.ShapeDtypeStruct((s, D_MODEL), BF16)],
        name=name, compiler_params=_cparams(("parallel",)),
    )(dmix, proj, proj, ya, yb)


def _swiglu_fwd(gu, name):
    s = gu.shape[0]
    tm = _pick(s, (256,))

    def body(gu_ref, o_ref):
        gate = gu_ref[:, 0:FFN].astype(F32)
        o_ref[...] = (gate * _sigmoid(gate) * gu_ref[:, FFN:2 * FFN].astype(F32)).astype(o_ref.dtype)

    return pl.pallas_call(
        body, grid=(s // tm,), in_specs=[pl.BlockSpec((tm, 2 * FFN), lambda i: (i, 0))],
        out_specs=pl.BlockSpec((tm, FFN), lambda i: (i, 0)),
        out_shape=jax.ShapeDtypeStruct((s, FFN), BF16), name=name, compiler_params=_cparams(("parallel",)),
    )(gu)


def _swiglu_bwd(dhid, gu, name):
    s = gu.shape[0]
    tm = _pick(s, (256,))

    def body(dh_ref, gu_ref, o_ref):
        gate = gu_ref[:, 0:FFN].astype(F32)
        up = gu_ref[:, FFN:2 * FFN].astype(F32)
        dh = dh_ref[...].astype(F32)
        sg = _sigmoid(gate)
        o_ref[:, 0:FFN] = (dh * up * (sg * (1.0 + gate * (1.0 - sg)))).astype(o_ref.dtype)
        o_ref[:, FFN:2 * FFN] = (dh * gate * sg).astype(o_ref.dtype)

    wide = pl.BlockSpec((tm, 2 * FFN), lambda i: (i, 0))
    return pl.pallas_call(
        body, grid=(s // tm,), in_specs=[pl.BlockSpec((tm, FFN), lambda i: (i, 0)), wide], out_specs=wide,
        out_shape=jax.ShapeDtypeStruct((s, 2 * FFN), BF16), name=name, compiler_params=_cparams(("parallel",)),
    )(dhid, gu)


def _adamw_math(w, g, m, v):
    m2 = ADAM_B1 * m + (1.0 - ADAM_B1) * g
    v2 = ADAM_B2 * v + (1.0 - ADAM_B2) * (g * g)
    m_hat = m2 / (1.0 - ADAM_B1 ** ADAM_STEP)
    v_hat = v2 / (1.0 - ADAM_B2 ** ADAM_STEP)
    delta = -ADAM_LR * (m_hat / (jnp.sqrt(v_hat) + ADAM_EPS) + ADAM_WD * w)
    return delta, m2, v2


def _adamw(w, g, m, v, name):
    shape = w.shape
    cols = shape[-1]
    rows = int(np.prod(shape[:-1])) if len(shape) > 1 else 1
    w2, g2, m2, v2 = [t.reshape(rows, cols) for t in (w, g, m, v)]
    tr = _pick(rows, (512, 352, 256)) if rows % 8 == 0 else rows

    def body(w_ref, g_ref, m_ref, v_ref, d_ref, nm_ref, nv_ref):
        d, nm, nv = _adamw_math(w_ref[...], g_ref[...], m_ref[...], v_ref[...])
        d_ref[...] = d
        nm_ref[...] = nm
        nv_ref[...] = nv

    blk = pl.BlockSpec((tr, cols), lambda i: (i, 0))
    out = pl.pallas_call(
        body, grid=(rows // tr,), in_specs=[blk] * 4, out_specs=[blk] * 3,
        out_shape=[jax.ShapeDtypeStruct((rows, cols), F32)] * 3, name=name,
        compiler_params=_cparams(("parallel",)),
    )(w2, g2, m2, v2)
    return [t.reshape(shape) for t in out]


def _pair_sum(g2, recv, core, name, after=()):
    _, nchip, r, c = g2.shape
    tr = _pick(r, (512,))

    def body(core_ref, a_ref, b_ref, *rest):
        o_ref = rest[-1]
        o_ref[...] = (a_ref[...].astype(F32) + b_ref[...].astype(F32)).astype(o_ref.dtype)

    grid_spec = pltpu.PrefetchScalarGridSpec(
        num_scalar_prefetch=1, grid=(nchip, r // tr),
        in_specs=[pl.BlockSpec((None, None, tr, c), lambda k, i, cr: (cr[0], k, i, 0)),
                  pl.BlockSpec((None, tr, c), lambda k, i, cr: (k, i, 0))] + [_after_spec(t) for t in after],
        out_specs=pl.BlockSpec((None, tr, c), lambda k, i, cr: (k, i, 0)))
    return pl.pallas_call(
        body, grid_spec=grid_spec, out_shape=jax.ShapeDtypeStruct((nchip, r, c), BF16), name=name,
        compiler_params=_cparams(("parallel", "parallel")),
    )(core, g2, recv, *after)


def _chip_sum(a, recv, chip, name, after=()):
    _, r, c = a.shape
    tr = _pick(r, (512,))

    def body(chip_ref, a_ref, b_ref, *rest):
        o_ref = rest[-1]
        o_ref[...] = ((a_ref[...].astype(F32) + b_ref[0].astype(F32)) + b_ref[1].astype(F32)) + b_ref[2].astype(F32)

    grid_spec = pltpu.PrefetchScalarGridSpec(
        num_scalar_prefetch=1, grid=(r // tr,),
        in_specs=[pl.BlockSpec((None, tr, c), lambda i, cr: (cr[0], i, 0)),
                  pl.BlockSpec((3, tr, c), lambda i, cr: (0, i, 0))] + [_after_spec(t) for t in after],
        out_specs=pl.BlockSpec((tr, c), lambda i, cr: (i, 0)))
    return pl.pallas_call(
        body, grid_spec=grid_spec, out_shape=jax.ShapeDtypeStruct((r, c), F32), name=name,
        compiler_params=_cparams(("parallel",)),
    )(chip, a, recv, *after)


def _sum_devices(parts, name):
    n, r, c = parts.shape

    def body(p_ref, o_ref):
        acc = p_ref[0]
        for d in range(1, n):
            acc = acc + p_ref[d]
        o_ref[...] = acc

    return pl.pallas_call(
        body, out_shape=jax.ShapeDtypeStruct((r, c), F32), name=name,
        in_specs=[pl.BlockSpec(memory_space=pltpu.VMEM)], out_specs=pl.BlockSpec(memory_space=pltpu.VMEM),
    )(parts)


def _lane_iota():
    return lax.broadcasted_iota(jnp.int32, (1, LANES), 1)


def _tiles_up(tiles, s, lane):
    rolled = [pltpu.roll(t, s, 1) for t in tiles]
    zero = jnp.zeros_like(tiles[0])
    return [jnp.where(lane < s, p, c) for p, c in zip([zero] + rolled, rolled + [zero])]


def _tiles_down(tiles, s, lane):
    back = (LANES - s) % LANES
    rolled = [pltpu.roll(t, back, 1) for t in tiles]
    zero = jnp.zeros_like(tiles[0])
    return [jnp.where(lane < LANES - s, c, n) for c, n in zip(rolled, rolled[1:] + [zero])]


def _window_cols(prm, w_in, w_gate, w_up, l, name, after=None):
    tm = 256
    n_in, n_gu = WIN_IN // LANES, FFN_SHARD // LANES + 1

    def body(prm_ref, win_ref, g_ref, u_ref, *rest):
        out_ref, gu_ref, scr_in, scr_gu = rest[-4:]
        lane = _lane_iota()
        s_main, lo, hi, s_code, clo, chi, code_hi, s_gu = [prm_ref[i] for i in range(8)]
        scr_in[:, D_MODEL:WIN_IN] = jnp.zeros((tm, WIN_IN - D_MODEL), F32)
        scr_in[:, 0:W_IN_SHARD] = win_ref[...]

        def keep(t, a, b):
            col = lane + t * LANES
            return jnp.where((col >= a) & (col < b), scr_in[:, t * LANES:(t + 1) * LANES], 0.0)

        main = _tiles_up([keep(t, lo, hi) for t in range(n_in)], s_main, lane)
        for t in range(n_in):
            out_ref[:, t * LANES:(t + 1) * LANES] = main[t].astype(BF16)
        low = _tiles_up([keep(0, clo, chi)], s_code, lane)[0]
        high = _tiles_up([keep(n_in - 2, clo, chi), keep(n_in - 1, clo, chi)], s_code, lane)[1]
        out_ref[:, A_FZ:A_FZ + LANES] = jnp.where(code_hi == 1, high, low).astype(BF16)
        for ref, base in ((g_ref, 0), (u_ref, WIN_GU)):
            scr_gu[:, (n_gu - 1) * LANES:n_gu * LANES] = jnp.zeros((tm, LANES), F32)
            scr_gu[:, 0:FFN_SHARD] = ref[...]
            moved = _tiles_up([scr_gu[:, t * LANES:(t + 1) * LANES] for t in range(n_gu)], s_gu, lane)
            for t in range(n_gu + 1):
                gu_ref[:, base + t * LANES:base + (t + 1) * LANES] = moved[t].astype(BF16)

    after_args = [] if after is None else [after]
    grid_spec = pltpu.PrefetchScalarGridSpec(
        num_scalar_prefetch=1, grid=(D_MODEL // tm,),
        in_specs=[pl.BlockSpec((None, tm, W_IN_SHARD), lambda i, p: (l, i, 0)),
                  pl.BlockSpec((None, tm, FFN_SHARD), lambda i, p: (l, i, 0)),
                  pl.BlockSpec((None, tm, FFN_SHARD), lambda i, p: (l, i, 0))] + [_after_spec(t) for t in after_args],
        out_specs=[pl.BlockSpec((tm, IN_COLS), lambda i, p: (i, 0)), pl.BlockSpec((tm, GU_COLS), lambda i, p: (i, 0))],
        scratch_shapes=[pltpu.VMEM((tm, WIN_IN), F32), pltpu.VMEM((tm, n_gu * LANES), F32)])
    return pl.pallas_call(
        body, grid_spec=grid_spec, name=name, compiler_params=_cparams(("parallel",)),
        out_shape=[jax.ShapeDtypeStruct((D_MODEL, IN_COLS), BF16), jax.ShapeDtypeStruct((D_MODEL, GU_COLS), BF16)],
    )(prm, w_in, w_gate, w_up, *after_args)


def _gu_width(j):
    return min(WIN_GU, FFN - T_GU[j] * LANES)


def _after_spec(t):
    tile = (SUBLANES_BF16 if t.dtype == BF16 else 8, LANES)
    return pl.BlockSpec((None,) * (t.ndim - 2) + tile, lambda *_: (0,) * t.ndim)


def _assemble_in(a_all, tail_all, name, after=()):
    tm = D_MODEL // N_DEV

    def body(a_ref, t_ref, *rest):
        win_ref, tail_ref = rest[-2:]
        tail_ref[...] = t_ref[...]
        win_ref[...] = jnp.zeros_like(win_ref)
        code = a_ref[0, :, A_FZ:A_FZ + LANES]
        for j in range(N_DEV):
            c0 = T_IN[j] * LANES
            win_ref[:, c0:c0 + WIN_IN] += a_ref[j, :, 0:WIN_IN]
            if j > 0:
                code = code + a_ref[j, :, A_FZ:A_FZ + LANES]
        win_ref[:, FZ_COL:PROJ_W] = code

    return pl.pallas_call(
        body, grid=(N_DEV,),
        in_specs=[pl.BlockSpec((N_DEV, tm, IN_COLS), lambda i: (0, i, 0)),
                  pl.BlockSpec((None,) + tail_all.shape[1:], lambda i: (i, 0, 0))] + [_after_spec(t) for t in after],
        out_specs=[pl.BlockSpec((tm, PROJ_W), lambda i: (i, 0)),
                   pl.BlockSpec((None,) + tail_all.shape[1:], lambda i: (i, 0, 0))],
        out_shape=[jax.ShapeDtypeStruct((D_MODEL, PROJ_W), BF16), jax.ShapeDtypeStruct(tail_all.shape, BF16)],
        name=name, compiler_params=_cparams(("parallel",)),
    )(a_all, tail_all, *after)


def _assemble_rest(a_all, rows_all, name, after=()):
    tm = D_MODEL // N_DEV
    n_in = 2 + len(after)

    def body(*refs):
        a_ref, r_ref = refs[:2]
        wgu_ref, oa_ref, ob_ref, o_ref, down_ref = refs[n_in:]
        wgu_ref[...] = jnp.zeros_like(wgu_ref)
        for j in range(N_DEV):
            g0, width = T_GU[j] * LANES, _gu_width(j)
            wgu_ref[:, g0:g0 + width] += a_ref[j, :, 0:width]
            wgu_ref[:, FFN + g0:FFN + g0 + width] += a_ref[j, :, WIN_GU:WIN_GU + width]
        oa_ref[...] = r_ref[B_OA:B_OA + ROW_SHARD, :]
        ob_ref[...] = r_ref[B_OB:B_OB + ROW_SHARD, :]
        o_ref[...] = r_ref[B_O:B_O + ROW_SHARD, :]
        down_ref[...] = r_ref[B_DOWN:B_DOWN + FFN_SHARD, :]

    def rows(n):
        return pl.BlockSpec((n, D_MODEL), lambda i: (i, 0))

    square = jax.ShapeDtypeStruct((D_MODEL, D_MODEL), BF16)
    return pl.pallas_call(
        body, grid=(N_DEV,),
        in_specs=[pl.BlockSpec((N_DEV, tm, GU_COLS), lambda i: (0, i, 0)),
                  pl.BlockSpec((None, B_FG2, D_MODEL), lambda i: (i, 0, 0))] + [_after_spec(t) for t in after],
        out_specs=[pl.BlockSpec((tm, 2 * FFN), lambda i: (i, 0)),
                   rows(ROW_SHARD), rows(ROW_SHARD), rows(ROW_SHARD), rows(FFN_SHARD)],
        out_shape=[jax.ShapeDtypeStruct((D_MODEL, 2 * FFN), BF16),
                   square, square, square, jax.ShapeDtypeStruct((FFN, D_MODEL), BF16)],
        name=name, compiler_params=_cparams(("parallel",)),
    )(a_all, rows_all, *after)


def _grad_windows_in(d_in, name):
    tm = 256

    def body(din_ref, out_ref):
        for j in range(N_DEV):
            c0 = T_IN[j] * LANES
            out_ref[j & 1, j >> 1, :, 0:WIN_IN] = din_ref[:, c0:c0 + WIN_IN]
            out_ref[j & 1, j >> 1, :, A_FZ:IN_COLS] = din_ref[:, FZ_COL:PROJ_W]

    return pl.pallas_call(
        body, grid=(D_MODEL // tm,), in_specs=[pl.BlockSpec((tm, PROJ_W), lambda i: (i, 0))],
        out_specs=pl.BlockSpec((2, 4, tm, IN_COLS), lambda i: (0, 0, i, 0)),
        out_shape=jax.ShapeDtypeStruct((2, 4, D_MODEL, IN_COLS), BF16), name=name,
        compiler_params=_cparams(("parallel",)),
    )(d_in)


def _grad_windows_gu(d_gu, name):
    tm = 256

    def body(dgu_ref, out_ref):
        for j in range(N_DEV):
            g0, width = T_GU[j] * LANES, _gu_width(j)
            for half, base in ((0, 0), (FFN, WIN_GU)):
                out_ref[j & 1, j >> 1, :, base:base + width] = dgu_ref[:, half + g0:half + g0 + width]
                if width < WIN_GU:
                    out_ref[j & 1, j >> 1, :, base + width:base + WIN_GU] = jnp.zeros((tm, WIN_GU - width), BF16)

    return pl.pallas_call(
        body, grid=(D_MODEL // tm,), in_specs=[pl.BlockSpec((tm, 2 * FFN), lambda i: (i, 0))],
        out_specs=pl.BlockSpec((2, 4, tm, GU_COLS), lambda i: (0, 0, i, 0)),
        out_shape=jax.ShapeDtypeStruct((2, 4, D_MODEL, GU_COLS), BF16), name=name,
        compiler_params=_cparams(("parallel",)),
    )(d_gu)


def _final_windows(prm, chip, a, recv, params, l, kind, name, prev=None):
    tm = 128
    n_in, n_gu = WIN_IN // LANES, FFN_SHARD // LANES + 1
    widths = (W_IN_SHARD,) if kind == "in" else (FFN_SHARD, FFN_SHARD)
    cols = IN_COLS if kind == "in" else GU_COLS
    n_par = 3 * len(widths)
    prev = [] if prev is None else [t for group in prev for t in group]

    def body(prm_ref, chip_ref, a_ref, r_ref, *refs):
        ins, outs = refs[:n_par], refs[n_par + len(prev):]
        lane = _lane_iota()
        s_main, s_code, clo, chi, s_gu = [prm_ref[i] for i in (0, 3, 4, 5, 7)]

        def total(c0):
            sl = slice(c0, c0 + LANES)
            return ((a_ref[:, sl].astype(F32) + r_ref[0, :, sl].astype(F32)) + r_ref[1, :, sl].astype(F32)) \
                + r_ref[2, :, sl].astype(F32)

        if kind == "in":
            grads = _tiles_down([total(t * LANES) for t in range(n_in)], s_main, lane)
            code = pltpu.roll(total(A_FZ), (LANES - s_code) % LANES, 1)
            for t in (0, n_in - 2, n_in - 1):
                col = lane + t * LANES
                grads[t] = jnp.where((col >= clo) & (col < chi), code, grads[t])
            per_weight = [grads]
        else:
            per_weight = [_tiles_down([total(base + t * LANES) for t in range(n_gu + 1)], s_gu, lane)[:n_gu]
                          for base in (0, WIN_GU)]
        for k, (tiles, width) in enumerate(zip(per_weight, widths)):
            w_ref, m_ref, v_ref = ins[3 * k:3 * k + 3]
            g_ref, d_ref, nm_ref, nv_ref = outs[4 * k:4 * k + 4]
            for t, g in enumerate(tiles):
                n = min(LANES, width - t * LANES)
                sl = slice(t * LANES, t * LANES + n)
                g = g[:, 0:n]
                d, nm, nv = _adamw_math(w_ref[:, sl], g, m_ref[:, sl], v_ref[:, sl])
                g_ref[:, sl] = g
                d_ref[:, sl] = d
                nm_ref[:, sl] = nm
                nv_ref[:, sl] = nv

    def native(width):
        return pl.BlockSpec((None, tm, width), lambda i, p, c: (l, i, 0))

    in_specs = [pl.BlockSpec((None, tm, cols), lambda i, p, c: (c[0], i, 0)),
                pl.BlockSpec((3, tm, cols), lambda i, p, c: (0, i, 0))]
    in_specs += [native(wd) for wd in widths for _ in range(3)] + [_ANY] * len(prev)
    grid_spec = pltpu.PrefetchScalarGridSpec(
        num_scalar_prefetch=2, grid=(D_MODEL // tm,), in_specs=in_specs,
        out_specs=[native(wd) for wd in widths for _ in range(4)])
    out = pl.pallas_call(
        body, grid_spec=grid_spec, name=name, compiler_params=_cparams(("parallel",)),
        out_shape=[jax.ShapeDtypeStruct((DEPTH, D_MODEL, wd), F32) for wd in widths for _ in range(4)],
        input_output_aliases={4 + n_par + k: k for k in range(len(prev))},
    )(prm, chip, a, recv, *params, *prev)
    return [out[4 * k:4 * k + 4] for k in range(len(widths))]


def _me():
    return lax.axis_index("x"), lax.axis_index("y"), lax.axis_index("c")


_CHIP_FLIPS = ((1, 0), (0, 1), (1, 1))
_ANY = pl.BlockSpec(memory_space=pl.ANY)


def _comm_call(body, peers, out_shape, sems, name, args, collective_id):
    if collective_id is None:
        n_in = len(args)
        return pl.pallas_call(body, out_shape=out_shape, name=name, in_specs=[_ANY] * n_in,
                              out_specs=[_ANY] * len(out_shape), scratch_shapes=sems)(*args)

    def sequencer_body(*refs):
        barrier = pltpu.get_barrier_semaphore()
        targets = peers()
        for peer in targets:
            pl.semaphore_signal(barrier, inc=1, device_id=peer, device_id_type=MESH)
        pl.semaphore_wait(barrier, len(targets))
        body(*refs)

    sequencer = plsc.ScalarSubcoreMesh(axis_name="seq", num_cores=1)
    return pl.kernel(sequencer_body, out_type=out_shape, mesh=sequencer, scratch_types=sems, name=name,
                     compiler_params=pltpu.CompilerParams(collective_id=collective_id))(*args)


def _sibling_peer():
    x, y, cc = _me()
    return [(x, y, 1 - cc)]


def _chip_peers():
    x, y, cc = _me()
    return [(x ^ fx, y ^ fy, cc) for fx, fy in _CHIP_FLIPS]


def _all_gather(shards, name, collective_id=None):
    n = len(shards)

    def body(*refs):
        x_refs, out_refs = refs[:n], refs[n:2 * n]
        send_sems, recv_sems, local_sems = refs[2 * n:]
        x, y, cc = _me()
        sibling = (x, y, 1 - cc)
        chips = [(x ^ fx, y ^ fy) for fx, fy in _CHIP_FLIPS]

        def copy(a, k, block, to, from_shard=False):
            px, py, pc = block
            slot = out_refs[a].at[4 * px + 2 * py + pc]
            return pltpu.make_async_remote_copy(
                src_ref=x_refs[a] if from_shard else slot, dst_ref=slot,
                send_sem=send_sems.at[a, k], recv_sem=recv_sems.at[a, k], device_id=to, device_id_type=MESH)

        mine = [pltpu.make_async_copy(x_refs[a], out_refs[a].at[4 * x + 2 * y + cc], local_sems.at[a])
                for a in range(n)]
        for cp in mine:
            cp.start()
        first = [copy(a, 0, (x, y, cc), sibling, True) for a in range(n)]
        first += [copy(a, 1 + j, (x, y, cc), (*chip, cc), True) for j, chip in enumerate(chips) for a in range(n)]
        for cp in first:
            cp.start()
        passed = []
        for j, chip in enumerate(chips):
            for a in range(n):
                copy(a, 1 + j, (*chip, cc), (x, y, cc)).wait_recv()
                passed.append(copy(a, 4 + j, (*chip, cc), sibling))
                passed[-1].start()
        for a in range(n):
            copy(a, 0, sibling, (x, y, cc)).wait_recv()
            for j, chip in enumerate(chips):
                copy(a, 4 + j, (*chip, 1 - cc), (x, y, cc)).wait_recv()
        for cp in first + passed:
            cp.wait_send()
        for cp in mine:
            cp.wait()

    return _comm_call(
        body, lambda: _sibling_peer() + _chip_peers(),
        [jax.ShapeDtypeStruct((N_DEV,) + s.shape, s.dtype) for s in shards],
        [pltpu.SemaphoreType.DMA((n, 7)), pltpu.SemaphoreType.DMA((n, 7)), pltpu.SemaphoreType.DMA((n,))],
        name, shards, collective_id)


def _send_to_sibling(parts, name, collective_id=None):
    n = len(parts)

    def body(*refs):
        g_refs, out_refs = refs[:n], refs[n:2 * n]
        send_sems, recv_sems = refs[2 * n:]
        x, y, cc = _me()
        copies = [pltpu.make_async_remote_copy(
            src_ref=g_refs[a].at[1 - cc], dst_ref=out_refs[a], send_sem=send_sems.at[a], recv_sem=recv_sems.at[a],
            device_id=(x, y, 1 - cc), device_id_type=MESH) for a in range(n)]
        for cp in copies:
            cp.start()
        for cp in copies:
            cp.wait()

    return _comm_call(
        body, _sibling_peer, [jax.ShapeDtypeStruct(p.shape[1:], p.dtype) for p in parts],
        [pltpu.SemaphoreType.DMA((n,)), pltpu.SemaphoreType.DMA((n,))], name, parts, collective_id)


def _send_to_chips(parts, name, collective_id=None):
    n = len(parts)

    def body(*refs):
        a_refs, out_refs = refs[:n], refs[n:2 * n]
        send_sems, recv_sems = refs[2 * n:]
        x, y, cc = _me()
        copies = []
        for k, (fx, fy) in enumerate(_CHIP_FLIPS):
            px, py = x ^ fx, y ^ fy
            for a in range(n):
                copies.append(pltpu.make_async_remote_copy(
                    src_ref=a_refs[a].at[2 * px + py], dst_ref=out_refs[a].at[k], send_sem=send_sems.at[a, k],
                    recv_sem=recv_sems.at[a, k], device_id=(px, py, cc), device_id_type=MESH))
                copies[-1].start()
        for cp in copies:
            cp.wait()

    return _comm_call(
        body, _chip_peers, [jax.ShapeDtypeStruct((3,) + p.shape[1:], p.dtype) for p in parts],
        [pltpu.SemaphoreType.DMA((n, 3)), pltpu.SemaphoreType.DMA((n, 3))], name, parts, collective_id)


def _pack_rows(w_oa, w_ob, w_o, w_down, w_fg2, conv_w, l):
    conv_bits = lax.bitcast_convert_type(conv_w[l].reshape(-1), BF16).reshape(1, -1)
    tail = jnp.concatenate([w_fg2[l].astype(BF16).reshape(1, D_MODEL),
                            jnp.pad(conv_bits, ((0, 0), (0, D_MODEL - conv_bits.shape[1])))], axis=0)
    tail = jnp.pad(tail, ((0, B_ROWS - B_FG2 - tail.shape[0]), (0, 0)))
    rows = jnp.concatenate([w_oa[l].astype(BF16), w_ob[l].astype(BF16), w_o[l].astype(BF16),
                            w_down[l].astype(BF16)], axis=0)
    return rows, tail


def _unpack_tail(tail):
    w_fg2 = tail[:, 0, :].reshape(N_DEV, RANK, KEY_W // N_DEV).transpose(1, 0, 2).reshape(RANK, KEY_W)
    conv_bits = tail[:, B_CONV - B_FG2, :2 * 3 * ROW_SHARD].reshape(N_DEV, 3 * ROW_SHARD, 2)
    conv_w = lax.bitcast_convert_type(conv_bits, F32).reshape(N_DEV, 3, ROW_SHARD)
    return jnp.pad(w_fg2, ((0, LANES - RANK), (0, 0))), conv_w.transpose(1, 0, 2).reshape(3, D_MODEL)


def _by_core_chip(t):
    return t.reshape((2, 2, 2) + t.shape[1:]).transpose((2, 0, 1) + tuple(range(3, t.ndim + 2))).reshape(
        (2, 4) + t.shape[1:])


def _grad_rows(g):
    fg2 = g["w_fg2"][:RANK].reshape(RANK, N_DEV, KEY_W // N_DEV).transpose(1, 0, 2).reshape(N_DEV, 1, D_MODEL)
    conv = g["conv_w"].astype(BF16).reshape(3, N_DEV, ROW_SHARD).transpose(1, 0, 2).reshape(N_DEV, 1, 3 * ROW_SHARD)
    tail = jnp.concatenate([fg2, jnp.pad(conv, ((0, 0), (0, 0), (0, D_MODEL - 3 * ROW_SHARD)))], axis=1)
    tail = jnp.pad(tail, ((0, 0), (0, B_ROWS - B_FG2 - 2), (0, 0)))
    parts = [g["w_oa"].reshape(N_DEV, ROW_SHARD, D_MODEL), g["w_ob"].reshape(N_DEV, ROW_SHARD, D_MODEL),
             g["w_o"].reshape(N_DEV, ROW_SHARD, D_MODEL), g["w_down"].reshape(N_DEV, FFN_SHARD, D_MODEL), tail]
    return _by_core_chip(jnp.concatenate(parts, axis=1))


def _ungrad_rows(gs):
    return dict(w_oa=gs[B_OA:B_OA + ROW_SHARD], w_ob=gs[B_OB:B_OB + ROW_SHARD], w_o=gs[B_O:B_O + ROW_SHARD],
                w_ffn_down=gs[B_DOWN:B_DOWN + FFN_SHARD], w_fg2=gs[B_FG2].reshape(RANK, KEY_W // N_DEV),
                conv_w=gs[B_CONV, :3 * ROW_SHARD].reshape(3, ROW_SHARD))


def _layer_fwd(x, p, l):
    tag = f"l{l}_"
    h = _rmsnorm_fwd(x, p["norm1_g"], tag + "norm1")
    proj = _matmul(h, p["w_in"], "nn", BF16, tag + "proj")
    o, states = _gla_fwd(proj, p["w_fg2"], p["b_fg"], tag + "gla_fwd")
    p.update(p.pop("rest")((o,)))
    oa = _gla_post_fwd(o, proj, p["gla_norm_g"], tag + "gla_post")
    ya = _matmul(oa, p["w_oa"], "nn", BF16, tag + "ya")
    cb = _conv_fwd(proj, p["conv_w"], p["conv_b"], tag + "conv")
    yb = _matmul(cb, p["w_ob"], "nn", BF16, tag + "yb")
    mix = _mix_fwd(proj, ya, yb, tag + "mix")
    x1 = _matmul(mix, p["w_o"], "nn", F32, tag + "x1", residual=x)
    h2 = _rmsnorm_fwd(x1, p["norm2_g"], tag + "norm2")
    gu = _matmul(h2, p["w_gu"], "nn", BF16, tag + "gu")
    hid = _swiglu_fwd(gu, tag + "swiglu")
    x2 = _matmul(hid, p["w_down"], "nn", F32, tag + "x2", residual=x1)
    saved = dict(x=x, h=h, proj=proj, o=o, states=states, oa=oa, ya=ya, cb=cb, yb=yb, mix=mix, x1=x1, h2=h2,
                 gu=gu, hid=hid)
    return x2, saved


def _layer_bwd(dx2, p, sv, l, reduce=None):
    if reduce is None:
        reduce = lambda group, grads: ((), ())
    tag = f"l{l}_b_"
    dx2h = dx2.astype(BF16)
    dhid = _matmul(dx2h, p["w_down"], "nt", BF16, tag + "dhid")
    d_down = _matmul(sv["hid"], dx2h, "tn", BF16, tag + "dw_down")
    dgu = _swiglu_bwd(dhid, sv["gu"], tag + "swiglu")
    dh2 = _matmul(dgu, p["w_gu"], "nt", F32, tag + "dh2")
    d_gu = _matmul(sv["h2"], dgu, "tn", BF16, tag + "dw_gu")
    gu_packed, gu_sums = reduce("gu", d_gu)
    dx1, dg2 = _rmsnorm_bwd(sv["x1"], p["norm2_g"], dh2, dx2, tag + "norm2")
    dx1h = dx1.astype(BF16)
    dmix = _matmul(dx1h, p["w_o"], "nt", BF16, tag + "dmix", after=gu_packed)
    d_o = _matmul(sv["mix"], dx1h, "tn", BF16, tag + "dw_o")
    dgab, dya, dyb = _mix_bwd(dmix, sv["proj"], sv["ya"], sv["yb"], tag + "mix")
    dcb = _matmul(dyb, p["w_ob"], "nt", BF16, tag + "dcb", after=gu_sums)
    d_ob = _matmul(sv["cb"], dyb, "tn", BF16, tag + "dw_ob")
    d3, dwb = _conv_bwd(dcb, sv["proj"], p["conv_w"], p["conv_b"], tag + "conv")
    doa = _matmul(dya, p["w_oa"], "nt", BF16, tag + "doa")
    d_oa = _matmul(sv["oa"], dya, "tn", BF16, tag + "dw_oa")
    dr, do, dgg = _gla_post_bwd(doa, sv["o"], sv["proj"], p["gla_norm_g"], tag + "gla_post")
    dq, dk, dv, dfg, dbfg = _gla_bwd(sv["proj"], p["w_fg2"], p["b_fg"], sv["states"], do, tag + "gla")
    fz = sv["proj"][:, FZ_COL:]
    dfz = _matmul(dfg, p["w_fg2"], "nt", BF16, tag + "dfz")
    d_fg2 = _matmul(fz, dfg, "tn", BF16, tag + "dw_fg2")
    rows = dict(w_fg2=d_fg2, conv_w=dwb[0:3], w_oa=d_oa, w_ob=d_ob, w_o=d_o, w_down=d_down)
    rows_packed, rows_sums = reduce("rows", rows)
    dproj = jnp.concatenate([dq, dk, dv, dr, d3, dgab, dfz], axis=1)
    d_in = _matmul(sv["h"], dproj, "tn", BF16, tag + "dw_in", after=rows_packed)
    _, in_sums = reduce("in", d_in)
    dh = _matmul(dproj, p["w_in"], "nt", F32, tag + "dh", after=(d_in,) + tuple(rows_sums))
    dx, dg1 = _rmsnorm_bwd(sv["x"], p["norm1_g"], dh, dx1, tag + "norm1", after=in_sums)
    big = dict(w_in=d_in, w_gu=d_gu, **rows)
    pad = lambda t: jnp.pad(t, ((0, 0), (0, D_MODEL - t.shape[1])))
    small = [dg1[0:1], pad(dbfg[0:1]), pad(dgg[0:1]), dwb[3:4], dg2[0:1]]
    return dx, big, small


def _local_step(x, target, weights_of, final_g, reduce_of=None):
    saved, layers = [], []
    for l in range(DEPTH):
        layers.append(weights_of(l, x))
        x, sv = _layer_fwd(x, layers[l], l)
        saved.append(sv)
    loss, dx, dgf = _loss_head(x, final_g, target, "loss_head")
    bigs, smalls = [None] * DEPTH, [None] * DEPTH
    for l in reversed(range(DEPTH)):
        dx, bigs[l], smalls[l] = _layer_bwd(dx, layers[l], saved[l], l, reduce_of(l) if reduce_of else None)
    small = jnp.concatenate(smalls[0] + smalls[1] + [dgf[0:1]], axis=0)
    small = jnp.pad(small, ((0, SMALL_ROWS - small.shape[0]), (0, 0)))
    return loss[0, 0], dx, bigs, small


def kernel(x, norm1_g, w_in, w_fg2, b_fg, gla_norm_g, w_oa, conv_w, conv_b, w_ob, w_o, norm2_g, w_ffn_gate, w_ffn_up, w_ffn_down, final_g, loss_target, m_norm1_g, m_w_in, m_w_fg2, m_b_fg, m_gla_norm_g, m_w_oa, m_conv_w, m_conv_b, m_w_ob, m_w_o, m_norm2_g, m_w_ffn_gate, m_w_ffn_up, m_w_ffn_down, m_final_g, v_norm1_g, v_w_in, v_w_fg2, v_b_fg, v_gla_norm_g, v_w_oa, v_conv_w, v_conv_b, v_w_ob, v_w_o, v_norm2_g, v_w_ffn_gate, v_w_ffn_up, v_w_ffn_down, v_final_g):
    names = ["norm1_g", "w_in", "w_fg2", "b_fg", "gla_norm_g", "w_oa", "conv_w", "conv_b", "w_ob", "w_o",
             "norm2_g", "w_ffn_gate", "w_ffn_up", "w_ffn_down", "final_g"]
    w = dict(zip(names, [norm1_g, w_in, w_fg2, b_fg, gla_norm_g, w_oa, conv_w, conv_b, w_ob, w_o, norm2_g,
                         w_ffn_gate, w_ffn_up, w_ffn_down, final_g]))
    m = dict(zip(names, [m_norm1_g, m_w_in, m_w_fg2, m_b_fg, m_gla_norm_g, m_w_oa, m_conv_w, m_conv_b, m_w_ob,
                         m_w_o, m_norm2_g, m_w_ffn_gate, m_w_ffn_up, m_w_ffn_down, m_final_g]))
    v = dict(zip(names, [v_norm1_g, v_w_in, v_w_fg2, v_b_fg, v_gla_norm_g, v_w_oa, v_conv_w, v_conv_b, v_w_ob,
                         v_w_o, v_norm2_g, v_w_ffn_gate, v_w_ffn_up, v_w_ffn_down, v_final_g]))
    col_names = ["w_in", "w_ffn_gate", "w_ffn_up"]
    cx, cy, cc = _me()
    prm = jnp.asarray(SHIFT_TABLE)[4 * cx + 2 * cy + cc]
    core = jnp.reshape(cc, (1,)).astype(jnp.int32)
    chip = jnp.reshape(2 * cx + cy, (1,)).astype(jnp.int32)

    ids = iter(range(32))

    gathered, previous = [], None
    for l in range(DEPTH):
        rows, tail = _pack_rows(w_oa, w_ob, w_o, w_ffn_down, w_fg2, conv_w, l)
        win_in, win_gu = _window_cols(prm, w_in, w_ffn_gate, w_ffn_up, l, f"l{l}_windows", after=previous)
        previous = rows
        if l == 0:
            first = _all_gather([win_in, tail], "l0_gather_in", next(ids))
            gathered.append(list(first) + list(_all_gather([win_gu, rows], "l0_gather_rest", next(ids))))
        else:
            all_in, all_tail, all_gu, all_rows = _all_gather([win_in, tail, win_gu, rows], f"l{l}_gather", next(ids))
            gathered.append([all_in, all_tail, all_gu, all_rows])

    def weights_of(l, x_in):
        all_in, all_tail, all_gu, all_rows = gathered[l]
        after = (x_in,) if l > 0 else ()
        w_in_full, tail = _assemble_in(all_in, all_tail, f"l{l}_assemble_in", after)
        w_fg2_full, conv_w_full = _unpack_tail(tail)

        def rest(after_rest):
            names_rest = ("w_gu", "w_oa", "w_ob", "w_o", "w_down")
            return dict(zip(names_rest, _assemble_rest(all_gu, all_rows, f"l{l}_assemble_rest", after + after_rest)))

        return dict(w_in=w_in_full, rest=rest, w_fg2=w_fg2_full,
                    conv_w=conv_w_full, norm1_g=norm1_g[l][None], b_fg=b_fg[l][None],
                    gla_norm_g=gla_norm_g[l][None], conv_b=conv_b[l][None], norm2_g=norm2_g[l][None])

    pending = [dict() for _ in range(DEPTH)]
    landed = [()]

    def reduce_of(l):
        def reduce(group, grads):
            tag = f"l{l}_{group}"
            if group == "gu":
                packed = _grad_windows_gu(grads, tag + "_windows")
            elif group == "in":
                packed = _grad_windows_in(grads, tag + "_windows")
            else:
                packed = _grad_rows(grads)
            (from_sibling,) = _send_to_sibling([packed], tag + "_to_sibling", next(ids))
            sums = _pair_sum(packed, from_sibling, core, tag + "_pair_sum", after=landed[0])
            (from_chips,) = _send_to_chips([sums], tag + "_to_chips", next(ids))
            landed[0] = (from_chips,)
            pending[l][group] = (sums, from_chips)
            return (packed,), (sums,)
        return reduce

    loss, dx, bigs, small = _local_step(x[0], loss_target[0], weights_of, final_g[None], reduce_of)

    grads, deltas, new_m, new_v = {}, {}, {}, {}
    for kind, group_names in (("gu", col_names[1:]), ("in", col_names[:1])):
        params = [t for n in group_names for t in (w[n], m[n], v[n])]
        out = None
        for l in reversed(range(DEPTH)):
            sums, from_chips = pending[l][kind]
            out = _final_windows(prm, chip, sums, from_chips, params, l, kind, f"l{l}_{kind}_final", out)
        for n, (g, d, nm, nv) in zip(group_names, out):
            grads[n], deltas[n], new_m[n], new_v[n] = g, d, nm, nv
    row_grads = [_ungrad_rows(_chip_sum(*pending[l]["rows"], chip, f"l{l}_rows_chip_sum")) for l in range(DEPTH)]
    for n in row_grads[0]:
        grads[n] = jnp.stack([row_grads[l][n] for l in range(DEPTH)])

    small_sum = _sum_devices(_all_gather([small], "gather_small")[0], "sum_small")
    r512, r256 = slice(0, KEY_W), slice(0, HEAD_V)
    grads.update(
        norm1_g=jnp.stack([small_sum[0], small_sum[5]]), b_fg=jnp.stack([small_sum[1, r512], small_sum[6, r512]]),
        gla_norm_g=jnp.stack([small_sum[2, r256], small_sum[7, r256]]),
        conv_b=jnp.stack([small_sum[3], small_sum[8]]), norm2_g=jnp.stack([small_sum[4], small_sum[9]]),
        final_g=small_sum[10])

    for n in names:
        if n not in col_names:
            deltas[n], new_m[n], new_v[n] = _adamw(w[n], grads[n], m[n], v[n], "adamw_" + n)

    total_loss = lax.psum(loss, ("x", "y", "c"))
    return (total_loss, dx[None], *[grads[n] for n in names], *[deltas[n] for n in names],
            *[new_m[n] for n in names], *[new_v[n] for n in names])
```

```python
import functools

import jax
import jax.numpy as jnp
import numpy as np
from jax import lax
from jax.experimental import pallas as pl
from jax.experimental.pallas import tpu as pltpu
from jax.experimental.pallas import tpu_sc as plsc

F32 = jnp.float32
BF16 = jnp.bfloat16
MESH = pl.DeviceIdType.MESH

D_MODEL = 1024
DEPTH = 2
CHUNK = 64
HEADS = 4
HEAD_K = 128
HEAD_V = 256
KEY_W = HEADS * HEAD_K
VAL_W = HEADS * HEAD_V
RANK = 16
TAU = 16.0
FFN = 2816
IN_WIDTH = 2 * KEY_W + 2 * VAL_W + RANK + 5 * D_MODEL
EPS = 1e-6
Q_SCALE = HEAD_K ** -0.5
N_DEV = 8
ADAM_LR, ADAM_B1, ADAM_B2, ADAM_EPS, ADAM_WD, ADAM_STEP = 0.001, 0.9, 0.999, 1e-08, 0.01, 10

LANES = 128
SUBLANES_BF16 = 16
VMEM_LIMIT = 48 * 1024 * 1024

FZ_COL = 2 * KEY_W + 2 * VAL_W + 5 * D_MODEL
PROJ_W = FZ_COL + LANES
SEG_R, SEG_GBI, SEG_GCI, SEG_CX, SEG_GA, SEG_GB = 2, 3, 4, 5, 6, 7

W_IN_SHARD = IN_WIDTH // N_DEV
FFN_SHARD = FFN // N_DEV
ROW_SHARD = D_MODEL // N_DEV

WIN_IN = 9 * LANES
WIN_GU = 4 * LANES
A_FZ = WIN_IN
IN_COLS = A_FZ + LANES
GU_COLS = 2 * WIN_GU
ORIG_FZ = 2 * KEY_W + 2 * VAL_W


def _new_col(o):
    if o < ORIG_FZ:
        return o
    if o < ORIG_FZ + RANK:
        return FZ_COL + (o - ORIG_FZ)
    return o - RANK


def _shift_table():
    t_in, rows = [], []
    for j in range(N_DEV):
        new = [_new_col(W_IN_SHARD * j + i) for i in range(W_IN_SHARD)]
        main = [i for i in range(W_IN_SHARD) if new[i] < FZ_COL]
        code = [i for i in range(W_IN_SHARD) if new[i] >= FZ_COL]
        shift = new[main[0]] - main[0]
        t_in.append(shift // LANES)
        assert all(new[i] - i == shift for i in main) and shift % LANES + W_IN_SHARD <= WIN_IN
        if code:
            cshift = new[code[0]] - FZ_COL - code[0]
            crow = [cshift % LANES, code[0], code[-1] + 1, int(cshift < 0)]
        else:
            crow = [0, 0, 0, 0]
        rows.append([shift % LANES, main[0], main[-1] + 1] + crow + [FFN_SHARD * j % LANES])
    return tuple(t_in), np.asarray(rows, np.int32)


T_IN, SHIFT_TABLE = _shift_table()
T_GU = tuple(FFN_SHARD * j // LANES for j in range(N_DEV))

B_OA, B_OB, B_O, B_DOWN = 0, ROW_SHARD, 2 * ROW_SHARD, 3 * ROW_SHARD
B_FG2 = B_DOWN + FFN_SHARD
B_CONV = B_FG2 + 1
B_ROWS = B_FG2 + SUBLANES_BF16
SMALL_ROWS = 16


def _pick(n, candidates):
    for c in candidates:
        if n % c == 0:
            return c
    return n


def _cparams(sem):
    return pltpu.CompilerParams(dimension_semantics=sem, vmem_limit_bytes=VMEM_LIMIT)


def _sigmoid(x):
    return 1.0 / (1.0 + jnp.exp(-x))


def _matmul(a, b, dims, out_dtype, name, residual=None, after=(), into=None):
    if dims == "nn":
        (m, k), (k2, n) = a.shape, b.shape
    elif dims == "nt":
        (m, k), (n, k2) = a.shape, b.shape
    else:
        (k, m), (k2, n) = a.shape, b.shape
    assert k == k2, (a.shape, b.shape, dims)
    tm = _pick(m, (1024, 1408, 512, 256, 128))
    tn = _pick(n, (1664, 1408, 1024, 512, 256, 128))
    tk = _pick(k, (1664, 1408, 1024, 512, 256, 128))
    nk = k // tk
    if dims == "nn":
        a_spec = pl.BlockSpec((tm, tk), lambda i, j, kk: (i, kk))
        b_spec = pl.BlockSpec((tk, tn), lambda i, j, kk: (kk, j))
        contract = (((1,), (0,)), ((), ()))
    elif dims == "nt":
        a_spec = pl.BlockSpec((tm, tk), lambda i, j, kk: (i, kk))
        b_spec = pl.BlockSpec((tn, tk), lambda i, j, kk: (j, kk))
        contract = (((1,), (1,)), ((), ()))
    else:
        a_spec = pl.BlockSpec((tk, tm), lambda i, j, kk: (kk, i))
        b_spec = pl.BlockSpec((tk, tn), lambda i, j, kk: (kk, j))
        contract = (((0,), (0,)), ((), ()))
    o_spec = pl.BlockSpec((tm, tn), lambda i, j, kk: (i, j))
    has_res = residual is not None
    out_spec, out_struct, placed, aliases = o_spec, jax.ShapeDtypeStruct((m, n), out_dtype), (), {}
    if into is not None:
        buffer, col = into
        assert col % tn == 0 and buffer.dtype == out_dtype and not has_res
        out_spec = pl.BlockSpec((tm, tn), lambda i, j, kk: (i, col // tn + j))
        out_struct, placed, aliases = jax.ShapeDtypeStruct(buffer.shape, out_dtype), (buffer,), {2 + len(after): 0}

    def body(*refs):
        a_ref, b_ref = refs[:2]
        r_ref = refs[2] if has_res else None
        o_ref = refs[2 + has_res + len(after) + len(placed)]
        kk = pl.program_id(2)
        part = lax.dot_general(a_ref[...], b_ref[...], contract, preferred_element_type=F32)

        def finish(total):
            if has_res:
                total = total + r_ref[...]
            o_ref[...] = total.astype(o_ref.dtype)

        if nk == 1:
            finish(part)
            return
        acc_ref = refs[-1]

        @pl.when(kk == 0)
        def _():
            acc_ref[...] = part

        @pl.when((kk > 0) & (kk < nk - 1))
        def _():
            acc_ref[...] += part

        @pl.when(kk == nk - 1)
        def _():
            finish(acc_ref[...] + part)

    in_specs = [a_spec, b_spec] + ([o_spec] if has_res else []) + [_after_spec(t) for t in after]
    in_specs += [_ANY] * len(placed)
    args = (a, b) + ((residual,) if has_res else ()) + tuple(after) + placed
    return pl.pallas_call(
        body, grid=(m // tm, n // tn, nk), in_specs=in_specs, out_specs=out_spec,
        out_shape=out_struct, input_output_aliases=aliases,
        scratch_shapes=[pltpu.VMEM((tm, tn), F32)] if nk > 1 else [], name=name,
        compiler_params=_cparams(("parallel", "parallel", "arbitrary")),
    )(*args)


def _rmsnorm_fwd(x, g, name):
    s, d = x.shape
    tm = _pick(s, (512, 256))

    def body(x_ref, g_ref, o_ref):
        xv = x_ref[...]
        r = lax.rsqrt(jnp.mean(xv * xv, axis=-1, keepdims=True) + EPS)
        o_ref[...] = (xv * r * g_ref[...]).astype(o_ref.dtype)

    row = pl.BlockSpec((tm, d), lambda i: (i, 0))
    return pl.pallas_call(
        body, grid=(s // tm,), in_specs=[row, pl.BlockSpec((1, d), lambda i: (0, 0))], out_specs=row,
        out_shape=jax.ShapeDtypeStruct((s, d), BF16), name=name, compiler_params=_cparams(("parallel",)),
    )(x, g)


def _rmsnorm_bwd(x, g, dh, dres, name, after=()):
    s, d = x.shape
    tm = _pick(s, (512, 256))

    def body(x_ref, g_ref, dh_ref, dres_ref, *rest):
        dx_ref, dx16_ref, dg_ref = rest[-3:]
        xv = x_ref[...]
        r = lax.rsqrt(jnp.mean(xv * xv, axis=-1, keepdims=True) + EPS)
        xn = xv * r
        dhv = dh_ref[...].astype(F32)
        dxn = dhv * g_ref[...]
        dx = dres_ref[...] + r * (dxn - xn * jnp.mean(dxn * xn, axis=-1, keepdims=True))
        dx_ref[...] = dx
        dx16_ref[...] = dx.astype(BF16)

        @pl.when(pl.program_id(0) == 0)
        def _():
            dg_ref[...] = jnp.zeros_like(dg_ref)

        dg_ref[...] += jnp.broadcast_to(jnp.sum(dhv * xn, axis=0, keepdims=True), dg_ref.shape)

    row = pl.BlockSpec((tm, d), lambda i: (i, 0))
    acc = pl.BlockSpec((8, d), lambda i: (0, 0))
    return pl.pallas_call(
        body, grid=(s // tm,),
        in_specs=[row, pl.BlockSpec((1, d), lambda i: (0, 0)), row, row] + [_after_spec(t) for t in after],
        out_specs=[row, row, acc],
        out_shape=[jax.ShapeDtypeStruct((s, d), F32), jax.ShapeDtypeStruct((s, d), BF16),
                   jax.ShapeDtypeStruct((8, d), F32)],
        name=name, compiler_params=_cparams(("arbitrary",)),
    )(x, g, dh, dres, *after)


def _loss_head(x, g, target, name):
    s, d = x.shape
    tm = _pick(s, (512, 256))

    def body(x_ref, g_ref, t_ref, loss_ref, dx_ref, dx16_ref, dg_ref):
        xv = x_ref[...]
        gv = g_ref[...]
        r = lax.rsqrt(jnp.mean(xv * xv, axis=-1, keepdims=True) + EPS)
        xn = xv * r
        err = xn * gv - t_ref[...]
        dy = err * (1.0 / d)
        dxn = dy * gv
        dx = r * (dxn - xn * jnp.mean(dxn * xn, axis=-1, keepdims=True))
        dx_ref[...] = dx
        dx16_ref[...] = dx.astype(BF16)

        @pl.when(pl.program_id(0) == 0)
        def _():
            dg_ref[...] = jnp.zeros_like(dg_ref)
            loss_ref[...] = jnp.zeros_like(loss_ref)

        dg_ref[...] += jnp.broadcast_to(jnp.sum(dy * xn, axis=0, keepdims=True), dg_ref.shape)
        row_loss = jnp.sum(err * err, axis=-1, keepdims=True)
        loss_ref[...] += jnp.broadcast_to((0.5 / d) * jnp.sum(row_loss, axis=0, keepdims=True), loss_ref.shape)

    row = pl.BlockSpec((tm, d), lambda i: (i, 0))
    return pl.pallas_call(
        body, grid=(s // tm,), in_specs=[row, pl.BlockSpec((1, d), lambda i: (0, 0)), row],
        out_specs=[pl.BlockSpec((8, LANES), lambda i: (0, 0)), row, row, pl.BlockSpec((8, d), lambda i: (0, 0))],
        out_shape=[jax.ShapeDtypeStruct((8, LANES), F32), jax.ShapeDtypeStruct((s, d), F32),
                   jax.ShapeDtypeStruct((s, d), BF16), jax.ShapeDtypeStruct((8, d), F32)],
        name=name, compiler_params=_cparams(("arbitrary",)),
    )(x, g, target)


def _tri_dot(tri, x):
    hi = x.astype(BF16)
    lo = (x - hi.astype(F32)).astype(BF16)
    return jnp.dot(tri, hi, preferred_element_type=F32) + jnp.dot(tri, lo, preferred_element_type=F32)


def _tri(strict):
    r = lax.broadcasted_iota(jnp.int32, (CHUNK, CHUNK), 0)
    c = lax.broadcasted_iota(jnp.int32, (CHUNK, CHUNK), 1)
    return jnp.where((r > c) if strict else (r >= c), 1.0, 0.0).astype(BF16)


def _chunk_decay(fz, w, b, tri):
    fg = jnp.dot(fz, w, preferred_element_type=F32) + b
    la = (jnp.minimum(fg, 0.0) - jnp.log(1.0 + jnp.exp(-jnp.abs(fg)))) * (1.0 / TAU)
    cum = _tri_dot(tri, la)
    cum_end = cum[CHUNK - 1:CHUNK, :]
    return fg, jnp.exp(cum_end - cum), jnp.exp(cum_end)


_TN = (((0,), (0,)), ((), ()))
_NT = (((1,), (1,)), ((), ()))


def _gla_specs(rows):
    q_spec = pl.BlockSpec((rows, HEAD_K), lambda h, c: (c, h))
    k_spec = pl.BlockSpec((rows, HEAD_K), lambda h, c: (c, HEADS + h))
    v_spec = pl.BlockSpec((rows, HEAD_V), lambda h, c: (c, HEADS + h))
    fz_spec = pl.BlockSpec((rows, LANES), lambda h, c: (c, FZ_COL // LANES))
    w_spec = pl.BlockSpec((LANES, HEAD_K), lambda h, c: (0, h))
    b_spec = pl.BlockSpec((1, HEAD_K), lambda h, c: (0, h))
    return q_spec, k_spec, v_spec, fz_spec, w_spec, b_spec


def _gla_fwd(proj, wfg, bfg, name):
    s = proj.shape[0]
    nc = s // CHUNK
    per = _pick(nc, (8, 4, 2, 1))
    rows = per * CHUNK

    def body(q_ref, k_ref, v_ref, fz_ref, w_ref, b_ref, o_ref, st_ref, state):
        @pl.when(pl.program_id(1) == 0)
        def _():
            state[...] = jnp.zeros_like(state)

        tri = _tri(False)
        w = w_ref[...]
        b = b_ref[...]
        for i in range(per):
            sl = pl.ds(i * CHUNK, CHUNK)
            _, dec, gamma = _chunk_decay(fz_ref[sl, :], w, b, tri)
            kd = (k_ref[sl, :].astype(F32) * dec).astype(BF16)
            st = state[...] * gamma + lax.dot_general(v_ref[sl, :], kd, _TN, preferred_element_type=F32)
            state[...] = st
            st16 = st.astype(BF16)
            st_ref[0, i] = st16
            qs = (q_ref[sl, :].astype(F32) * Q_SCALE).astype(BF16)
            o_ref[sl, :] = lax.dot_general(qs, st16, _NT, preferred_element_type=F32).astype(o_ref.dtype)

    q_spec, k_spec, v_spec, fz_spec, w_spec, b_spec = _gla_specs(rows)
    return pl.pallas_call(
        body, grid=(HEADS, nc // per),
        in_specs=[q_spec, k_spec, v_spec, fz_spec, w_spec, b_spec],
        out_specs=[pl.BlockSpec((rows, HEAD_V), lambda h, c: (c, h)),
                   pl.BlockSpec((1, per, HEAD_V, HEAD_K), lambda h, c: (h, c, 0, 0))],
        out_shape=[jax.ShapeDtypeStruct((s, VAL_W), BF16),
                   jax.ShapeDtypeStruct((HEADS, nc, HEAD_V, HEAD_K), BF16)],
        scratch_shapes=[pltpu.VMEM((HEAD_V, HEAD_K), F32)], name=name,
        compiler_params=_cparams(("parallel", "arbitrary")),
    )(proj, proj, proj, proj, wfg, bfg)


def _gla_bwd(proj, wfg, bfg, states, do, name):
    s = proj.shape[0]
    nc = s // CHUNK
    per = _pick(nc, (8, 4, 2, 1))
    rows = per * CHUNK
    nblk = nc // per

    def rev(spec_fn):
        return lambda h, j: spec_fn(h, nblk - 1 - j)

    def body(q_ref, k_ref, v_ref, fz_ref, w_ref, b_ref, do_ref, st_ref, prev_ref,
             dq_ref, dk_ref, dv_ref, dfg_ref, db_ref, carry):
        j = pl.program_id(1)

        @pl.when(j == 0)
        def _():
            carry[...] = jnp.zeros_like(carry)
            db_ref[...] = jnp.zeros_like(db_ref)

        tri = _tri(False)
        tri_strict = _tri(True)
        w = w_ref[...]
        b = b_ref[...]
        has_prev = (j < nblk - 1).astype(F32)
        db = jnp.zeros((1, HEAD_K), F32)
        for i in reversed(range(per)):
            sl = pl.ds(i * CHUNK, CHUNK)
            fg, dec, gamma = _chunk_decay(fz_ref[sl, :], w, b, tri)
            kd = k_ref[sl, :].astype(F32) * dec
            qs = (q_ref[sl, :].astype(F32) * Q_SCALE).astype(BF16)
            dov = do_ref[sl, :]
            v = v_ref[sl, :]
            gt = carry[...] + lax.dot_general(dov, qs, _TN, preferred_element_type=F32)
            gt16 = gt.astype(BF16)
            dq_ref[sl, :] = (jnp.dot(dov, st_ref[0, i], preferred_element_type=F32) * Q_SCALE).astype(dq_ref.dtype)
            dkd = jnp.dot(v, gt16, preferred_element_type=F32)
            dv_ref[sl, :] = lax.dot_general(kd.astype(BF16), gt16, _NT,
                                            preferred_element_type=F32).astype(dv_ref.dtype)
            if i > 0:
                st_prev = st_ref[0, i - 1].astype(F32)
            else:
                st_prev = prev_ref[0, 0].astype(F32) * has_prev
            dgamma = jnp.sum(gt * st_prev, axis=0, keepdims=True)
            carry[...] = gt * gamma
            dk_ref[sl, :] = (dkd * dec).astype(dk_ref.dtype)
            dla = dgamma * gamma + _tri_dot(tri_strict, dkd * kd)
            dfg = dla * (1.0 / TAU) * _sigmoid(-fg)
            dfg_ref[sl, :] = dfg.astype(dfg_ref.dtype)
            db = db + jnp.sum(dfg, axis=0, keepdims=True)
        db_ref[...] += jnp.broadcast_to(db, db_ref.shape)

    q_spec, k_spec, v_spec, fz_spec, w_spec, b_spec = _gla_specs(rows)
    q_spec, k_spec, v_spec, fz_spec = [
        pl.BlockSpec(sp.block_shape, rev(sp.index_map)) for sp in (q_spec, k_spec, v_spec, fz_spec)]
    do_spec = pl.BlockSpec((rows, HEAD_V), lambda h, j: (nblk - 1 - j, h))
    st_spec = pl.BlockSpec((1, per, HEAD_V, HEAD_K), lambda h, j: (h, nblk - 1 - j, 0, 0))
    prev_spec = pl.BlockSpec((1, 1, HEAD_V, HEAD_K),
                             lambda h, j: (h, jnp.maximum((nblk - 1 - j) * per - 1, 0), 0, 0))
    key_out = pl.BlockSpec((rows, HEAD_K), lambda h, j: (nblk - 1 - j, h))
    return pl.pallas_call(
        body, grid=(HEADS, nblk),
        in_specs=[q_spec, k_spec, v_spec, fz_spec, w_spec, b_spec, do_spec, st_spec, prev_spec],
        out_specs=[key_out, key_out, do_spec, key_out, pl.BlockSpec((8, HEAD_K), lambda h, j: (0, h))],
        out_shape=[jax.ShapeDtypeStruct((s, KEY_W), BF16), jax.ShapeDtypeStruct((s, KEY_W), BF16),
                   jax.ShapeDtypeStruct((s, VAL_W), BF16), jax.ShapeDtypeStruct((s, KEY_W), BF16),
                   jax.ShapeDtypeStruct((8, KEY_W), F32)],
        scratch_shapes=[pltpu.VMEM((HEAD_V, HEAD_K), F32)], name=name,
        compiler_params=_cparams(("parallel", "arbitrary")),
    )(proj, proj, proj, proj, wfg, bfg, do, states, states)


def _place_qkv(dq, dk, dv, dproj, name):
    s = dq.shape[0]
    tm = _pick(s, (512, 256))

    def body(dq_ref, dk_ref, dv_ref, _, o_ref):
        o_ref[:, 0:KEY_W] = dq_ref[...]
        o_ref[:, KEY_W:2 * KEY_W] = dk_ref[...]
        o_ref[:, 2 * KEY_W:2 * KEY_W + VAL_W] = dv_ref[...]

    def rows(width):
        return pl.BlockSpec((tm, width), lambda i: (i, 0))

    return pl.pallas_call(
        body, grid=(s // tm,), in_specs=[rows(KEY_W), rows(KEY_W), rows(VAL_W), _ANY],
        out_specs=rows(2 * KEY_W + VAL_W), out_shape=jax.ShapeDtypeStruct(dproj.shape, BF16),
        input_output_aliases={3: 0}, name=name, compiler_params=_cparams(("parallel",)),
    )(dq, dk, dv, dproj)


def _seg(tm, seg):
    return pl.BlockSpec((tm, D_MODEL), lambda i: (i, seg))


def _gla_post_fwd(o, proj, g, name):
    s = o.shape[0]
    tm = _pick(s, (512, 256))

    def body(o_ref, r_ref, g_ref, oa_ref):
        gv = g_ref[...]
        for h in range(HEADS):
            sl = slice(h * HEAD_V, (h + 1) * HEAD_V)
            ov = o_ref[:, sl].astype(F32)
            rstd = lax.rsqrt(jnp.mean(ov * ov, axis=-1, keepdims=True) + EPS)
            rv = r_ref[:, sl].astype(F32)
            oa_ref[:, sl] = (ov * rstd * gv * (rv * _sigmoid(rv))).astype(oa_ref.dtype)

    row = pl.BlockSpec((tm, VAL_W), lambda i: (i, 0))
    return pl.pallas_call(
        body, grid=(s // tm,), in_specs=[row, _seg(tm, SEG_R), pl.BlockSpec((1, HEAD_V), lambda i: (0, 0))],
        out_specs=row, out_shape=jax.ShapeDtypeStruct((s, VAL_W), BF16), name=name,
        compiler_params=_cparams(("parallel",)),
    )(o, proj, g)


def _gla_post_bwd(doa, o, proj, g, dproj, name):
    s = o.shape[0]
    tm = _pick(s, (512, 256))

    def body(doa_ref, o_ref, r_ref, g_ref, _, dr_ref, do_ref, dg_ref):
        @pl.when(pl.program_id(0) == 0)
        def _():
            dg_ref[...] = jnp.zeros_like(dg_ref)

        gv = g_ref[...]
        dg = jnp.zeros((1, HEAD_V), F32)
        for h in range(HEADS):
            sl = slice(h * HEAD_V, (h + 1) * HEAD_V)
            ov = o_ref[:, sl].astype(F32)
            rstd = lax.rsqrt(jnp.mean(ov * ov, axis=-1, keepdims=True) + EPS)
            ohat = ov * rstd
            rv = r_ref[:, sl].astype(F32)
            sg = _sigmoid(rv)
            dv = doa_ref[:, sl].astype(F32)
            dr_ref[:, sl] = (dv * ohat * gv * (sg * (1.0 + rv * (1.0 - sg)))).astype(dr_ref.dtype)
            don = dv * (rv * sg)
            dg = dg + jnp.sum(don * ohat, axis=0, keepdims=True)
            dohat = don * gv
            do_ref[:, sl] = (rstd * (dohat - ohat * jnp.mean(dohat * ohat, axis=-1, keepdims=True))
                             ).astype(do_ref.dtype)
        dg_ref[...] += jnp.broadcast_to(dg, dg_ref.shape)

    row = pl.BlockSpec((tm, VAL_W), lambda i: (i, 0))
    return pl.pallas_call(
        body, grid=(s // tm,),
        in_specs=[row, row, _seg(tm, SEG_R), pl.BlockSpec((1, HEAD_V), lambda i: (0, 0)), _ANY],
        out_specs=[_seg(tm, SEG_R), row, pl.BlockSpec((8, HEAD_V), lambda i: (0, 0))],
        out_shape=[jax.ShapeDtypeStruct(dproj.shape, BF16), jax.ShapeDtypeStruct((s, VAL_W), BF16),
                   jax.ShapeDtypeStruct((8, HEAD_V), F32)],
        input_output_aliases={4: 0}, name=name, compiler_params=_cparams(("arbitrary",)),
    )(doa, o, proj, g, dproj)


HALO = SUBLANES_BF16


def _shift_down(u, p1, p2, n, rows):
    rolled = pltpu.roll(u, n, 0)
    if n == 1:
        return jnp.where(rows == 0, p1, rolled)
    return jnp.where(rows == 0, p2, jnp.where(rows == 1, p1, rolled))


def _shift_up(u, n1, n2, n, rows, tm):
    rolled = pltpu.roll(u, tm - n, 0)
    if n == 1:
        return jnp.where(rows == tm - 1, n1, rolled)
    return jnp.where(rows == tm - 2, n1, jnp.where(rows == tm - 1, n2, rolled))


def _conv_terms(gc_ref, cx_ref, gcp_ref, cxp_ref, tm):
    i = pl.program_id(0)
    u = gc_ref[...].astype(F32) * cx_ref[...].astype(F32)
    up = gcp_ref[...].astype(F32) * cxp_ref[...].astype(F32) * (i > 0).astype(F32)
    rows = lax.broadcasted_iota(jnp.int32, (tm, 1), 0)
    u1 = _shift_down(u, up[HALO - 1:HALO, :], up[HALO - 2:HALO - 1, :], 1, rows)
    u2 = _shift_down(u, up[HALO - 1:HALO, :], up[HALO - 2:HALO - 1, :], 2, rows)
    return u, u1, u2, rows


def _prev_halo(tm, seg):
    return pl.BlockSpec((HALO, D_MODEL), lambda i: (jnp.maximum(i * (tm // HALO) - 1, 0), seg))


def _conv_fwd(proj, w, b, name):
    s = proj.shape[0]
    tm = _pick(s, (512, 256))

    def body(gbi_ref, gc_ref, cx_ref, gcp_ref, cxp_ref, w_ref, b_ref, cb_ref):
        u, u1, u2, _ = _conv_terms(gc_ref, cx_ref, gcp_ref, cxp_ref, tm)
        conv = w_ref[0:1, :] * u2 + w_ref[1:2, :] * u1 + w_ref[2:3, :] * u + b_ref[...]
        cb_ref[...] = (gbi_ref[...].astype(F32) * conv).astype(cb_ref.dtype)

    return pl.pallas_call(
        body, grid=(s // tm,),
        in_specs=[_seg(tm, SEG_GBI), _seg(tm, SEG_GCI), _seg(tm, SEG_CX),
                  _prev_halo(tm, SEG_GCI), _prev_halo(tm, SEG_CX),
                  pl.BlockSpec((3, D_MODEL), lambda i: (0, 0)), pl.BlockSpec((1, D_MODEL), lambda i: (0, 0))],
        out_specs=pl.BlockSpec((tm, D_MODEL), lambda i: (i, 0)),
        out_shape=jax.ShapeDtypeStruct((s, D_MODEL), BF16), name=name, compiler_params=_cparams(("parallel",)),
    )(proj, proj, proj, proj, proj, w, b)


def _conv_bwd(dcb, proj, w, b, dproj, name):
    s = proj.shape[0]
    tm = _pick(s, (512, 256))
    nt = s // tm

    def body(dcb_ref, gbi_ref, gc_ref, cx_ref, gcp_ref, cxp_ref, dcbn_ref, gbin_ref, w_ref, b_ref, _,
             d3_ref, dwb_ref):
        i = pl.program_id(0)

        @pl.when(i == 0)
        def _():
            dwb_ref[...] = jnp.zeros_like(dwb_ref)

        u, u1, u2, rows = _conv_terms(gc_ref, cx_ref, gcp_ref, cxp_ref, tm)
        w0, w1, w2 = w_ref[0:1, :], w_ref[1:2, :], w_ref[2:3, :]
        conv = w0 * u2 + w1 * u1 + w2 * u + b_ref[...]
        dcbv = dcb_ref[...].astype(F32)
        gbi = gbi_ref[...].astype(F32)
        dconv = dcbv * gbi
        dnext = dcbn_ref[...].astype(F32) * gbin_ref[...].astype(F32) * (i < nt - 1).astype(F32)
        dc1 = _shift_up(dconv, dnext[0:1, :], dnext[1:2, :], 1, rows, tm)
        dc2 = _shift_up(dconv, dnext[0:1, :], dnext[1:2, :], 2, rows, tm)
        du = w2 * dconv + w1 * dc1 + w0 * dc2
        d3_ref[:, 0:D_MODEL] = (dcbv * conv).astype(d3_ref.dtype)
        d3_ref[:, D_MODEL:2 * D_MODEL] = (du * cx_ref[...].astype(F32)).astype(d3_ref.dtype)
        d3_ref[:, 2 * D_MODEL:3 * D_MODEL] = (du * gc_ref[...].astype(F32)).astype(d3_ref.dtype)
        dwb_ref[0:1, :] += jnp.sum(dconv * u2, axis=0, keepdims=True)
        dwb_ref[1:2, :] += jnp.sum(dconv * u1, axis=0, keepdims=True)
        dwb_ref[2:3, :] += jnp.sum(dconv * u, axis=0, keepdims=True)
        dwb_ref[3:4, :] += jnp.sum(dconv, axis=0, keepdims=True)

    def next_halo(seg_fn):
        return pl.BlockSpec((HALO, D_MODEL), lambda i: (jnp.minimum((i + 1) * (tm // HALO), s // HALO - 1), seg_fn))

    return pl.pallas_call(
        body, grid=(nt,),
        in_specs=[pl.BlockSpec((tm, D_MODEL), lambda i: (i, 0)),
                  _seg(tm, SEG_GBI), _seg(tm, SEG_GCI), _seg(tm, SEG_CX),
                  _prev_halo(tm, SEG_GCI), _prev_halo(tm, SEG_CX),
                  next_halo(0), next_halo(SEG_GBI),
                  pl.BlockSpec((3, D_MODEL), lambda i: (0, 0)), pl.BlockSpec((1, D_MODEL), lambda i: (0, 0)), _ANY],
        out_specs=[pl.BlockSpec((tm, 3 * D_MODEL), lambda i: (i, SEG_GBI // 3)),
                   pl.BlockSpec((8, D_MODEL), lambda i: (0, 0))],
        out_shape=[jax.ShapeDtypeStruct(dproj.shape, BF16), jax.ShapeDtypeStruct((8, D_MODEL), F32)],
        input_output_aliases={10: 0}, name=name, compiler_params=_cparams(("arbitrary",)),
    )(dcb, proj, proj, proj, proj, proj, dcb, proj, w, b, dproj)


def _mix_fwd(proj, ya, yb, name):
    s = proj.shape[0]
    tm = _pick(s, (512, 256))

    def body(ga_ref, gb_ref, ya_ref, yb_ref, o_ref):
        o_ref[...] = (_sigmoid(ga_ref[...].astype(F32)) * ya_ref[...].astype(F32)
                      + _sigmoid(gb_ref[...].astype(F32)) * yb_ref[...].astype(F32)).astype(o_ref.dtype)

    row = pl.BlockSpec((tm, D_MODEL), lambda i: (i, 0))
    return pl.pallas_call(
        body, grid=(s // tm,), in_specs=[_seg(tm, SEG_GA), _seg(tm, SEG_GB), row, row], out_specs=row,
        out_shape=jax.ShapeDtypeStruct((s, D_MODEL), BF16), name=name, compiler_params=_cparams(("parallel",)),
    )(proj, proj, ya, yb)


def _mix_bwd(dmix, proj, ya, yb, name):
    s = proj.shape[0]
    tm = _pick(s, (512, 256))

    def body(dm_ref, ga_ref, gb_ref, ya_ref, yb_ref, dg_ref, dya_ref, dyb_ref):
        dm = dm_ref[...].astype(F32)
        sa = _sigmoid(ga_ref[...].astype(F32))
        sb = _sigmoid(gb_ref[...].astype(F32))
        dg_ref[:, 0:D_MODEL] = (dm * ya_ref[...].astype(F32) * sa * (1.0 - sa)).astype(dg_ref.dtype)
        dg_ref[:, D_MODEL:2 * D_MODEL] = (dm * yb_ref[...].astype(F32) * sb * (1.0 - sb)).astype(dg_ref.dtype)
        dya_ref[...] = (dm * sa).astype(dya_ref.dtype)
        dyb_ref[...] = (dm * sb).astype(dyb_ref.dtype)

    row = pl.BlockSpec((tm, D_MODEL), lambda i: (i, 0))
    return pl.pallas_call(
        body, grid=(s // tm,), in_specs=[row, _seg(tm, SEG_GA), _seg(tm, SEG_GB), row, row],
        out_specs=[pl.BlockSpec((tm, 2 * D_MODEL), lambda i: (i, SEG_GA // 2)), row, row],
        out_shape=[jax.ShapeDtypeStruct((s, PROJ_W), BF16), jax.ShapeDtypeStruct((s, D_MODEL), BF16),
                   jax.ShapeDtypeStruct((s, D_MODEL), BF16)],
        name=name, compiler_params=_cparams(("parallel",)),
    )(dmix, proj, proj, ya, yb)


def _swiglu_fwd(gu, name):
    s = gu.shape[0]
    tm = _pick(s, (256,))

    def body(gu_ref, o_ref):
        gate = gu_ref[:, 0:FFN].astype(F32)
        o_ref[...] = (gate * _sigmoid(gate) * gu_ref[:, FFN:2 * FFN].astype(F32)).astype(o_ref.dtype)

    return pl.pallas_call(
        body, grid=(s // tm,), in_specs=[pl.BlockSpec((tm, 2 * FFN), lambda i: (i, 0))],
        out_specs=pl.BlockSpec((tm, FFN), lambda i: (i, 0)),
        out_shape=jax.ShapeDtypeStruct((s, FFN), BF16), name=name, compiler_params=_cparams(("parallel",)),
    )(gu)


def _swiglu_bwd(dhid, gu, name):
    s = gu.shape[0]
    tm = _pick(s, (256,))

    def body(dh_ref, gu_ref, o_ref):
        gate = gu_ref[:, 0:FFN].astype(F32)
        up = gu_ref[:, FFN:2 * FFN].astype(F32)
        dh = dh_ref[...].astype(F32)
        sg = _sigmoid(gate)
        o_ref[:, 0:FFN] = (dh * up * (sg * (1.0 + gate * (1.0 - sg)))).astype(o_ref.dtype)
        o_ref[:, FFN:2 * FFN] = (dh * gate * sg).astype(o_ref.dtype)

    wide = pl.BlockSpec((tm, 2 * FFN), lambda i: (i, 0))
    return pl.pallas_call(
        body, grid=(s // tm,), in_specs=[pl.BlockSpec((tm, FFN), lambda i: (i, 0)), wide], out_specs=wide,
        out_shape=jax.ShapeDtypeStruct((s, 2 * FFN), BF16), name=name, compiler_params=_cparams(("parallel",)),
    )(dhid, gu)


def _adamw_math(w, g, m, v):
    m2 = ADAM_B1 * m + (1.0 - ADAM_B1) * g
    v2 = ADAM_B2 * v + (1.0 - ADAM_B2) * (g * g)
    m_hat = m2 / (1.0 - ADAM_B1 ** ADAM_STEP)
    v_hat = v2 / (1.0 - ADAM_B2 ** ADAM_STEP)
    delta = -ADAM_LR * (m_hat / (jnp.sqrt(v_hat) + ADAM_EPS) + ADAM_WD * w)
    return delta, m2, v2


def _adamw(w, g, m, v, name):
    shape = w.shape
    cols = shape[-1]
    rows = int(np.prod(shape[:-1])) if len(shape) > 1 else 1
    w2, g2, m2, v2 = [t.reshape(rows, cols) for t in (w, g, m, v)]
    tr = _pick(rows, (512, 352, 256)) if rows % 8 == 0 else rows

    def body(w_ref, g_ref, m_ref, v_ref, d_ref, nm_ref, nv_ref):
        d, nm, nv = _adamw_math(w_ref[...], g_ref[...], m_ref[...], v_ref[...])
        d_ref[...] = d
        nm_ref[...] = nm
        nv_ref[...] = nv

    blk = pl.BlockSpec((tr, cols), lambda i: (i, 0))
    out = pl.pallas_call(
        body, grid=(rows // tr,), in_specs=[blk] * 4, out_specs=[blk] * 3,
        out_shape=[jax.ShapeDtypeStruct((rows, cols), F32)] * 3, name=name,
        compiler_params=_cparams(("parallel",)),
    )(w2, g2, m2, v2)
    return [t.reshape(shape) for t in out]


def _pair_sum(g2, recv, core, name, after=()):
    _, nchip, r, c = g2.shape
    tr = _pick(r, (512,))

    def body(core_ref, a_ref, b_ref, *rest):
        o_ref = rest[-1]
        o_ref[...] = (a_ref[...].astype(F32) + b_ref[...].astype(F32)).astype(o_ref.dtype)

    grid_spec = pltpu.PrefetchScalarGridSpec(
        num_scalar_prefetch=1, grid=(nchip, r // tr),
        in_specs=[pl.BlockSpec((None, None, tr, c), lambda k, i, cr: (cr[0], k, i, 0)),
                  pl.BlockSpec((None, tr, c), lambda k, i, cr: (k, i, 0))] + [_after_spec(t) for t in after],
        out_specs=pl.BlockSpec((None, tr, c), lambda k, i, cr: (k, i, 0)))
    return pl.pallas_call(
        body, grid_spec=grid_spec, out_shape=jax.ShapeDtypeStruct((nchip, r, c), BF16), name=name,
        compiler_params=_cparams(("parallel", "parallel")),
    )(core, g2, recv, *after)


def _chip_sum(a, recv, chip, name, after=()):
    _, r, c = a.shape
    tr = _pick(r, (512,))

    def body(chip_ref, a_ref, b_ref, *rest):
        o_ref = rest[-1]
        o_ref[...] = ((a_ref[...].astype(F32) + b_ref[0].astype(F32)) + b_ref[1].astype(F32)) + b_ref[2].astype(F32)

    grid_spec = pltpu.PrefetchScalarGridSpec(
        num_scalar_prefetch=1, grid=(r // tr,),
        in_specs=[pl.BlockSpec((None, tr, c), lambda i, cr: (cr[0], i, 0)),
                  pl.BlockSpec((3, tr, c), lambda i, cr: (0, i, 0))] + [_after_spec(t) for t in after],
        out_specs=pl.BlockSpec((tr, c), lambda i, cr: (i, 0)))
    return pl.pallas_call(
        body, grid_spec=grid_spec, out_shape=jax.ShapeDtypeStruct((r, c), F32), name=name,
        compiler_params=_cparams(("parallel",)),
    )(chip, a, recv, *after)


def _sum_devices(parts, name):
    n, r, c = parts.shape

    def body(p_ref, o_ref):
        acc = p_ref[0]
        for d in range(1, n):
            acc = acc + p_ref[d]
        o_ref[...] = acc

    return pl.pallas_call(
        body, out_shape=jax.ShapeDtypeStruct((r, c), F32), name=name,
        in_specs=[pl.BlockSpec(memory_space=pltpu.VMEM)], out_specs=pl.BlockSpec(memory_space=pltpu.VMEM),
    )(parts)


def _lane_iota():
    return lax.broadcasted_iota(jnp.int32, (1, LANES), 1)


def _tiles_up(tiles, s, lane):
    rolled = [pltpu.roll(t, s, 1) for t in tiles]
    zero = jnp.zeros_like(tiles[0])
    return [jnp.where(lane < s, p, c) for p, c in zip([zero] + rolled, rolled + [zero])]


def _tiles_down(tiles, s, lane):
    back = (LANES - s) % LANES
    rolled = [pltpu.roll(t, back, 1) for t in tiles]
    zero = jnp.zeros_like(tiles[0])
    return [jnp.where(lane < LANES - s, c, n) for c, n in zip(rolled, rolled[1:] + [zero])]


def _window_cols(prm, w_in, w_gate, w_up, l, name, after=None):
    tm = 256
    n_in, n_gu = WIN_IN // LANES, FFN_SHARD // LANES + 1

    def body(prm_ref, win_ref, g_ref, u_ref, *rest):
        out_ref, gu_ref, scr_in, scr_gu = rest[-4:]
        lane = _lane_iota()
        s_main, lo, hi, s_code, clo, chi, code_hi, s_gu = [prm_ref[i] for i in range(8)]
        scr_in[:, D_MODEL:WIN_IN] = jnp.zeros((tm, WIN_IN - D_MODEL), F32)
        scr_in[:, 0:W_IN_SHARD] = win_ref[...]

        def keep(t, a, b):
            col = lane + t * LANES
            return jnp.where((col >= a) & (col < b), scr_in[:, t * LANES:(t + 1) * LANES], 0.0)

        main = _tiles_up([keep(t, lo, hi) for t in range(n_in)], s_main, lane)
        for t in range(n_in):
            out_ref[:, t * LANES:(t + 1) * LANES] = main[t].astype(BF16)
        low = _tiles_up([keep(0, clo, chi)], s_code, lane)[0]
        high = _tiles_up([keep(n_in - 2, clo, chi), keep(n_in - 1, clo, chi)], s_code, lane)[1]
        out_ref[:, A_FZ:A_FZ + LANES] = jnp.where(code_hi == 1, high, low).astype(BF16)
        for ref, base in ((g_ref, 0), (u_ref, WIN_GU)):
            scr_gu[:, (n_gu - 1) * LANES:n_gu * LANES] = jnp.zeros((tm, LANES), F32)
            scr_gu[:, 0:FFN_SHARD] = ref[...]
            moved = _tiles_up([scr_gu[:, t * LANES:(t + 1) * LANES] for t in range(n_gu)], s_gu, lane)
            for t in range(n_gu + 1):
                gu_ref[:, base + t * LANES:base + (t + 1) * LANES] = moved[t].astype(BF16)

    after_args = [] if after is None else [after]
    grid_spec = pltpu.PrefetchScalarGridSpec(
        num_scalar_prefetch=1, grid=(D_MODEL // tm,),
        in_specs=[pl.BlockSpec((None, tm, W_IN_SHARD), lambda i, p: (l, i, 0)),
                  pl.BlockSpec((None, tm, FFN_SHARD), lambda i, p: (l, i, 0)),
                  pl.BlockSpec((None, tm, FFN_SHARD), lambda i, p: (l, i, 0))] + [_after_spec(t) for t in after_args],
        out_specs=[pl.BlockSpec((tm, IN_COLS), lambda i, p: (i, 0)), pl.BlockSpec((tm, GU_COLS), lambda i, p: (i, 0))],
        scratch_shapes=[pltpu.VMEM((tm, WIN_IN), F32), pltpu.VMEM((tm, n_gu * LANES), F32)])
    return pl.pallas_call(
        body, grid_spec=grid_spec, name=name, compiler_params=_cparams(("parallel",)),
        out_shape=[jax.ShapeDtypeStruct((D_MODEL, IN_COLS), BF16), jax.ShapeDtypeStruct((D_MODEL, GU_COLS), BF16)],
    )(prm, w_in, w_gate, w_up, *after_args)


def _gu_width(j):
    return min(WIN_GU, FFN - T_GU[j] * LANES)


def _after_spec(t):
    tile = (SUBLANES_BF16 if t.dtype == BF16 else 8, LANES)
    return pl.BlockSpec((None,) * (t.ndim - 2) + tile, lambda *_: (0,) * t.ndim)


def _assemble_in(a_all, tail_all, name, after=()):
    tm = D_MODEL // N_DEV

    def body(a_ref, t_ref, *rest):
        win_ref, tail_ref = rest[-2:]
        tail_ref[...] = t_ref[...]
        win_ref[...] = jnp.zeros_like(win_ref)
        code = a_ref[0, :, A_FZ:A_FZ + LANES]
        for j in range(N_DEV):
            c0 = T_IN[j] * LANES
            win_ref[:, c0:c0 + WIN_IN] += a_ref[j, :, 0:WIN_IN]
            if j > 0:
                code = code + a_ref[j, :, A_FZ:A_FZ + LANES]
        win_ref[:, FZ_COL:PROJ_W] = code

    return pl.pallas_call(
        body, grid=(N_DEV,),
        in_specs=[pl.BlockSpec((N_DEV, tm, IN_COLS), lambda i: (0, i, 0)),
                  pl.BlockSpec((None,) + tail_all.shape[1:], lambda i: (i, 0, 0))] + [_after_spec(t) for t in after],
        out_specs=[pl.BlockSpec((tm, PROJ_W), lambda i: (i, 0)),
                   pl.BlockSpec((None,) + tail_all.shape[1:], lambda i: (i, 0, 0))],
        out_shape=[jax.ShapeDtypeStruct((D_MODEL, PROJ_W), BF16), jax.ShapeDtypeStruct(tail_all.shape, BF16)],
        name=name, compiler_params=_cparams(("parallel",)),
    )(a_all, tail_all, *after)


def _assemble_rest(a_all, rows_all, name, after=()):
    tm = D_MODEL // N_DEV
    n_in = 2 + len(after)

    def body(*refs):
        a_ref, r_ref = refs[:2]
        wgu_ref, oa_ref, ob_ref, o_ref, down_ref = refs[n_in:]
        wgu_ref[...] = jnp.zeros_like(wgu_ref)
        for j in range(N_DEV):
            g0, width = T_GU[j] * LANES, _gu_width(j)
            wgu_ref[:, g0:g0 + width] += a_ref[j, :, 0:width]
            wgu_ref[:, FFN + g0:FFN + g0 + width] += a_ref[j, :, WIN_GU:WIN_GU + width]
        oa_ref[...] = r_ref[B_OA:B_OA + ROW_SHARD, :]
        ob_ref[...] = r_ref[B_OB:B_OB + ROW_SHARD, :]
        o_ref[...] = r_ref[B_O:B_O + ROW_SHARD, :]
        down_ref[...] = r_ref[B_DOWN:B_DOWN + FFN_SHARD, :]

    def rows(n):
        return pl.BlockSpec((n, D_MODEL), lambda i: (i, 0))

    square = jax.ShapeDtypeStruct((D_MODEL, D_MODEL), BF16)
    return pl.pallas_call(
        body, grid=(N_DEV,),
        in_specs=[pl.BlockSpec((N_DEV, tm, GU_COLS), lambda i: (0, i, 0)),
                  pl.BlockSpec((None, B_FG2, D_MODEL), lambda i: (i, 0, 0))] + [_after_spec(t) for t in after],
        out_specs=[pl.BlockSpec((tm, 2 * FFN), lambda i: (i, 0)),
                   rows(ROW_SHARD), rows(ROW_SHARD), rows(ROW_SHARD), rows(FFN_SHARD)],
        out_shape=[jax.ShapeDtypeStruct((D_MODEL, 2 * FFN), BF16),
                   square, square, square, jax.ShapeDtypeStruct((FFN, D_MODEL), BF16)],
        name=name, compiler_params=_cparams(("parallel",)),
    )(a_all, rows_all, *after)


def _grad_windows_in(d_in, name):
    tm = 256

    def body(din_ref, out_ref):
        for j in range(N_DEV):
            c0 = T_IN[j] * LANES
            out_ref[j & 1, j >> 1, :, 0:WIN_IN] = din_ref[:, c0:c0 + WIN_IN]
            out_ref[j & 1, j >> 1, :, A_FZ:IN_COLS] = din_ref[:, FZ_COL:PROJ_W]

    return pl.pallas_call(
        body, grid=(D_MODEL // tm,), in_specs=[pl.BlockSpec((tm, PROJ_W), lambda i: (i, 0))],
        out_specs=pl.BlockSpec((2, 4, tm, IN_COLS), lambda i: (0, 0, i, 0)),
        out_shape=jax.ShapeDtypeStruct((2, 4, D_MODEL, IN_COLS), BF16), name=name,
        compiler_params=_cparams(("parallel",)),
    )(d_in)


def _grad_windows_gu(d_gu, name):
    tm = 256

    def body(dgu_ref, out_ref):
        for j in range(N_DEV):
            g0, width = T_GU[j] * LANES, _gu_width(j)
            for half, base in ((0, 0), (FFN, WIN_GU)):
                out_ref[j & 1, j >> 1, :, base:base + width] = dgu_ref[:, half + g0:half + g0 + width]
                if width < WIN_GU:
                    out_ref[j & 1, j >> 1, :, base + width:base + WIN_GU] = jnp.zeros((tm, WIN_GU - width), BF16)

    return pl.pallas_call(
        body, grid=(D_MODEL // tm,), in_specs=[pl.BlockSpec((tm, 2 * FFN), lambda i: (i, 0))],
        out_specs=pl.BlockSpec((2, 4, tm, GU_COLS), lambda i: (0, 0, i, 0)),
        out_shape=jax.ShapeDtypeStruct((2, 4, D_MODEL, GU_COLS), BF16), name=name,
        compiler_params=_cparams(("parallel",)),
    )(d_gu)


def _final_windows(prm, chip, a, recv, params, l, kind, name, prev=None):
    tm = 128
    n_in, n_gu = WIN_IN // LANES, FFN_SHARD // LANES + 1
    widths = (W_IN_SHARD,) if kind == "in" else (FFN_SHARD, FFN_SHARD)
    cols = IN_COLS if kind == "in" else GU_COLS
    n_par = 3 * len(widths)
    prev = [] if prev is None else [t for group in prev for t in group]

    def body(prm_ref, chip_ref, a_ref, r_ref, *refs):
        ins, outs = refs[:n_par], refs[n_par + len(prev):]
        lane = _lane_iota()
        s_main, s_code, clo, chi, s_gu = [prm_ref[i] for i in (0, 3, 4, 5, 7)]

        def total(c0):
            sl = slice(c0, c0 + LANES)
            return ((a_ref[:, sl].astype(F32) + r_ref[0, :, sl].astype(F32)) + r_ref[1, :, sl].astype(F32)) \
                + r_ref[2, :, sl].astype(F32)

        if kind == "in":
            grads = _tiles_down([total(t * LANES) for t in range(n_in)], s_main, lane)
            code = pltpu.roll(total(A_FZ), (LANES - s_code) % LANES, 1)
            for t in (0, n_in - 2, n_in - 1):
                col = lane + t * LANES
                grads[t] = jnp.where((col >= clo) & (col < chi), code, grads[t])
            per_weight = [grads]
        else:
            per_weight = [_tiles_down([total(base + t * LANES) for t in range(n_gu + 1)], s_gu, lane)[:n_gu]
                          for base in (0, WIN_GU)]
        for k, (tiles, width) in enumerate(zip(per_weight, widths)):
            w_ref, m_ref, v_ref = ins[3 * k:3 * k + 3]
            g_ref, d_ref, nm_ref, nv_ref = outs[4 * k:4 * k + 4]
            for t, g in enumerate(tiles):
                n = min(LANES, width - t * LANES)
                sl = slice(t * LANES, t * LANES + n)
                g = g[:, 0:n]
                d, nm, nv = _adamw_math(w_ref[:, sl], g, m_ref[:, sl], v_ref[:, sl])
                g_ref[:, sl] = g
                d_ref[:, sl] = d
                nm_ref[:, sl] = nm
                nv_ref[:, sl] = nv

    def native(width):
        return pl.BlockSpec((None, tm, width), lambda i, p, c: (l, i, 0))

    in_specs = [pl.BlockSpec((None, tm, cols), lambda i, p, c: (c[0], i, 0)),
                pl.BlockSpec((3, tm, cols), lambda i, p, c: (0, i, 0))]
    in_specs += [native(wd) for wd in widths for _ in range(3)] + [_ANY] * len(prev)
    grid_spec = pltpu.PrefetchScalarGridSpec(
        num_scalar_prefetch=2, grid=(D_MODEL // tm,), in_specs=in_specs,
        out_specs=[native(wd) for wd in widths for _ in range(4)])
    out = pl.pallas_call(
        body, grid_spec=grid_spec, name=name, compiler_params=_cparams(("parallel",)),
        out_shape=[jax.ShapeDtypeStruct((DEPTH, D_MODEL, wd), F32) for wd in widths for _ in range(4)],
        input_output_aliases={4 + n_par + k: k for k in range(len(prev))},
    )(prm, chip, a, recv, *params, *prev)
    return [out[4 * k:4 * k + 4] for k in range(len(widths))]


def _me():
    return lax.axis_index("x"), lax.axis_index("y"), lax.axis_index("c")


_CHIP_FLIPS = ((1, 0), (0, 1), (1, 1))
_ANY = pl.BlockSpec(memory_space=pl.ANY)


def _comm_call(body, peers, out_shape, sems, name, args, collective_id):
    if collective_id is None:
        n_in = len(args)
        return pl.pallas_call(body, out_shape=out_shape, name=name, in_specs=[_ANY] * n_in,
                              out_specs=[_ANY] * len(out_shape), scratch_shapes=sems)(*args)

    def sequencer_body(*refs):
        barrier = pltpu.get_barrier_semaphore()
        targets = peers()
        for peer in targets:
            pl.semaphore_signal(barrier, inc=1, device_id=peer, device_id_type=MESH)
        pl.semaphore_wait(barrier, len(targets))
        body(*refs)

    sequencer = plsc.ScalarSubcoreMesh(axis_name="seq", num_cores=1)
    return pl.kernel(sequencer_body, out_type=out_shape, mesh=sequencer, scratch_types=sems, name=name,
                     compiler_params=pltpu.CompilerParams(collective_id=collective_id))(*args)


def _sibling_peer():
    x, y, cc = _me()
    return [(x, y, 1 - cc)]


def _chip_peers():
    x, y, cc = _me()
    return [(x ^ fx, y ^ fy, cc) for fx, fy in _CHIP_FLIPS]


def _all_gather(shards, name, collective_id=None):
    n = len(shards)

    def body(*refs):
        x_refs, out_refs = refs[:n], refs[n:2 * n]
        send_sems, recv_sems, local_sems = refs[2 * n:]
        x, y, cc = _me()
        sibling = (x, y, 1 - cc)
        chips = [(x ^ fx, y ^ fy) for fx, fy in _CHIP_FLIPS]

        def copy(a, k, block, to, from_shard=False):
            px, py, pc = block
            slot = out_refs[a].at[4 * px + 2 * py + pc]
            return pltpu.make_async_remote_copy(
                src_ref=x_refs[a] if from_shard else slot, dst_ref=slot,
                send_sem=send_sems.at[a, k], recv_sem=recv_sems.at[a, k], device_id=to, device_id_type=MESH)

        mine = [pltpu.make_async_copy(x_refs[a], out_refs[a].at[4 * x + 2 * y + cc], local_sems.at[a])
                for a in range(n)]
        for cp in mine:
            cp.start()
        first = [copy(a, 0, (x, y, cc), sibling, True) for a in range(n)]
        first += [copy(a, 1 + j, (x, y, cc), (*chip, cc), True) for j, chip in enumerate(chips) for a in range(n)]
        for cp in first:
            cp.start()
        passed = []
        for j, chip in enumerate(chips):
            for a in range(n):
                copy(a, 1 + j, (*chip, cc), (x, y, cc)).wait_recv()
                passed.append(copy(a, 4 + j, (*chip, cc), sibling))
                passed[-1].start()
        for a in range(n):
            copy(a, 0, sibling, (x, y, cc)).wait_recv()
            for j, chip in enumerate(chips):
                copy(a, 4 + j, (*chip, 1 - cc), (x, y, cc)).wait_recv()
        for cp in first + passed:
            cp.wait_send()
        for cp in mine:
            cp.wait()

    return _comm_call(
        body, lambda: _sibling_peer() + _chip_peers(),
        [jax.ShapeDtypeStruct((N_DEV,) + s.shape, s.dtype) for s in shards],
        [pltpu.SemaphoreType.DMA((n, 7)), pltpu.SemaphoreType.DMA((n, 7)), pltpu.SemaphoreType.DMA((n,))],
        name, shards, collective_id)


def _send_to_sibling(parts, name, collective_id=None):
    n = len(parts)

    def body(*refs):
        g_refs, out_refs = refs[:n], refs[n:2 * n]
        send_sems, recv_sems = refs[2 * n:]
        x, y, cc = _me()
        copies = [pltpu.make_async_remote_copy(
            src_ref=g_refs[a].at[1 - cc], dst_ref=out_refs[a], send_sem=send_sems.at[a], recv_sem=recv_sems.at[a],
            device_id=(x, y, 1 - cc), device_id_type=MESH) for a in range(n)]
        for cp in copies:
            cp.start()
        for cp in copies:
            cp.wait()

    return _comm_call(
        body, _sibling_peer, [jax.ShapeDtypeStruct(p.shape[1:], p.dtype) for p in parts],
        [pltpu.SemaphoreType.DMA((n,)), pltpu.SemaphoreType.DMA((n,))], name, parts, collective_id)


def _send_to_chips(parts, name, collective_id=None):
    n = len(parts)

    def body(*refs):
        a_refs, out_refs = refs[:n], refs[n:2 * n]
        send_sems, recv_sems = refs[2 * n:]
        x, y, cc = _me()
        copies = []
        for k, (fx, fy) in enumerate(_CHIP_FLIPS):
            px, py = x ^ fx, y ^ fy
            for a in range(n):
                copies.append(pltpu.make_async_remote_copy(
                    src_ref=a_refs[a].at[2 * px + py], dst_ref=out_refs[a].at[k], send_sem=send_sems.at[a, k],
                    recv_sem=recv_sems.at[a, k], device_id=(px, py, cc), device_id_type=MESH))
                copies[-1].start()
        for cp in copies:
            cp.wait()

    return _comm_call(
        body, _chip_peers, [jax.ShapeDtypeStruct((3,) + p.shape[1:], p.dtype) for p in parts],
        [pltpu.SemaphoreType.DMA((n, 3)), pltpu.SemaphoreType.DMA((n, 3))], name, parts, collective_id)


def _pack_rows(w_oa, w_ob, w_o, w_down, w_fg2, conv_w, l):
    conv_bits = lax.bitcast_convert_type(conv_w[l].reshape(-1), BF16).reshape(1, -1)
    tail = jnp.concatenate([w_fg2[l].astype(BF16).reshape(1, D_MODEL),
                            jnp.pad(conv_bits, ((0, 0), (0, D_MODEL - conv_bits.shape[1])))], axis=0)
    tail = jnp.pad(tail, ((0, B_ROWS - B_FG2 - tail.shape[0]), (0, 0)))
    rows = jnp.concatenate([w_oa[l].astype(BF16), w_ob[l].astype(BF16), w_o[l].astype(BF16),
                            w_down[l].astype(BF16)], axis=0)
    return rows, tail


def _unpack_tail(tail):
    w_fg2 = tail[:, 0, :].reshape(N_DEV, RANK, KEY_W // N_DEV).transpose(1, 0, 2).reshape(RANK, KEY_W)
    conv_bits = tail[:, B_CONV - B_FG2, :2 * 3 * ROW_SHARD].reshape(N_DEV, 3 * ROW_SHARD, 2)
    conv_w = lax.bitcast_convert_type(conv_bits, F32).reshape(N_DEV, 3, ROW_SHARD)
    return jnp.pad(w_fg2, ((0, LANES - RANK), (0, 0))), conv_w.transpose(1, 0, 2).reshape(3, D_MODEL)


def _by_core_chip(t):
    return t.reshape((2, 2, 2) + t.shape[1:]).transpose((2, 0, 1) + tuple(range(3, t.ndim + 2))).reshape(
        (2, 4) + t.shape[1:])


def _grad_rows(g):
    fg2 = g["w_fg2"][:RANK].reshape(RANK, N_DEV, KEY_W // N_DEV).transpose(1, 0, 2).reshape(N_DEV, 1, D_MODEL)
    conv = g["conv_w"].astype(BF16).reshape(3, N_DEV, ROW_SHARD).transpose(1, 0, 2).reshape(N_DEV, 1, 3 * ROW_SHARD)
    tail = jnp.concatenate([fg2, jnp.pad(conv, ((0, 0), (0, 0), (0, D_MODEL - 3 * ROW_SHARD)))], axis=1)
    tail = jnp.pad(tail, ((0, 0), (0, B_ROWS - B_FG2 - 2), (0, 0)))
    parts = [g["w_oa"].reshape(N_DEV, ROW_SHARD, D_MODEL), g["w_ob"].reshape(N_DEV, ROW_SHARD, D_MODEL),
             g["w_o"].reshape(N_DEV, ROW_SHARD, D_MODEL), g["w_down"].reshape(N_DEV, FFN_SHARD, D_MODEL), tail]
    return _by_core_chip(jnp.concatenate(parts, axis=1))


def _ungrad_rows(gs):
    return dict(w_oa=gs[B_OA:B_OA + ROW_SHARD], w_ob=gs[B_OB:B_OB + ROW_SHARD], w_o=gs[B_O:B_O + ROW_SHARD],
                w_ffn_down=gs[B_DOWN:B_DOWN + FFN_SHARD], w_fg2=gs[B_FG2].reshape(RANK, KEY_W // N_DEV),
                conv_w=gs[B_CONV, :3 * ROW_SHARD].reshape(3, ROW_SHARD))


def _layer_fwd(x, p, l):
    tag = f"l{l}_"
    h = _rmsnorm_fwd(x, p["norm1_g"], tag + "norm1")
    proj = _matmul(h, p["w_in"], "nn", BF16, tag + "proj")
    o, states = _gla_fwd(proj, p["w_fg2"], p["b_fg"], tag + "gla_fwd")
    p.update(p.pop("rest")((o,)))
    oa = _gla_post_fwd(o, proj, p["gla_norm_g"], tag + "gla_post")
    ya = _matmul(oa, p["w_oa"], "nn", BF16, tag + "ya")
    cb = _conv_fwd(proj, p["conv_w"], p["conv_b"], tag + "conv")
    yb = _matmul(cb, p["w_ob"], "nn", BF16, tag + "yb")
    mix = _mix_fwd(proj, ya, yb, tag + "mix")
    x1 = _matmul(mix, p["w_o"], "nn", F32, tag + "x1", residual=x)
    h2 = _rmsnorm_fwd(x1, p["norm2_g"], tag + "norm2")
    gu = _matmul(h2, p["w_gu"], "nn", BF16, tag + "gu")
    hid = _swiglu_fwd(gu, tag + "swiglu")
    x2 = _matmul(hid, p["w_down"], "nn", F32, tag + "x2", residual=x1)
    saved = dict(x=x, h=h, proj=proj, o=o, states=states, oa=oa, ya=ya, cb=cb, yb=yb, mix=mix, x1=x1, h2=h2,
                 gu=gu, hid=hid)
    return x2, saved


def _layer_bwd(dx2, dx2h, p, sv, l, reduce=None):
    if reduce is None:
        reduce = lambda group, grads: ((), ())
    tag = f"l{l}_b_"
    dhid = _matmul(dx2h, p["w_down"], "nt", BF16, tag + "dhid")
    d_down = _matmul(sv["hid"], dx2h, "tn", BF16, tag + "dw_down")
    dgu = _swiglu_bwd(dhid, sv["gu"], tag + "swiglu")
    dh2 = _matmul(dgu, p["w_gu"], "nt", F32, tag + "dh2")
    d_gu = _matmul(sv["h2"], dgu, "tn", BF16, tag + "dw_gu")
    gu_packed, gu_sums = reduce("gu", d_gu)
    dx1, dx1h, dg2 = _rmsnorm_bwd(sv["x1"], p["norm2_g"], dh2, dx2, tag + "norm2")
    dmix = _matmul(dx1h, p["w_o"], "nt", BF16, tag + "dmix", after=gu_packed)
    d_o = _matmul(sv["mix"], dx1h, "tn", BF16, tag + "dw_o")
    dproj, dya, dyb = _mix_bwd(dmix, sv["proj"], sv["ya"], sv["yb"], tag + "mix")
    dcb = _matmul(dyb, p["w_ob"], "nt", BF16, tag + "dcb", after=gu_sums)
    d_ob = _matmul(sv["cb"], dyb, "tn", BF16, tag + "dw_ob")
    dproj, dwb = _conv_bwd(dcb, sv["proj"], p["conv_w"], p["conv_b"], dproj, tag + "conv")
    doa = _matmul(dya, p["w_oa"], "nt", BF16, tag + "doa")
    d_oa = _matmul(sv["oa"], dya, "tn", BF16, tag + "dw_oa")
    dproj, do, dgg = _gla_post_bwd(doa, sv["o"], sv["proj"], p["gla_norm_g"], dproj, tag + "gla_post")
    dq, dk, dv, dfg, dbfg = _gla_bwd(sv["proj"], p["w_fg2"], p["b_fg"], sv["states"], do, tag + "gla")
    fz = sv["proj"][:, FZ_COL:]
    dproj = _place_qkv(dq, dk, dv, dproj, tag + "place_qkv")
    dproj = _matmul(dfg, p["w_fg2"], "nt", BF16, tag + "dfz", into=(dproj, FZ_COL))
    d_fg2 = _matmul(fz, dfg, "tn", BF16, tag + "dw_fg2")
    rows = dict(w_fg2=d_fg2, conv_w=dwb[0:3], w_oa=d_oa, w_ob=d_ob, w_o=d_o, w_down=d_down)
    rows_packed, rows_sums = reduce("rows", rows)
    d_in = _matmul(sv["h"], dproj, "tn", BF16, tag + "dw_in", after=rows_packed)
    _, in_sums = reduce("in", d_in)
    dh = _matmul(dproj, p["w_in"], "nt", F32, tag + "dh", after=(d_in,) + tuple(rows_sums))
    dx, dxh, dg1 = _rmsnorm_bwd(sv["x"], p["norm1_g"], dh, dx1, tag + "norm1", after=in_sums)
    big = dict(w_in=d_in, w_gu=d_gu, **rows)
    pad = lambda t: jnp.pad(t, ((0, 0), (0, D_MODEL - t.shape[1])))
    small = [dg1[0:1], pad(dbfg[0:1]), pad(dgg[0:1]), dwb[3:4], dg2[0:1]]
    return dx, dxh, big, small


def _local_step(x, target, weights_of, final_g, reduce_of=None):
    saved, layers = [], []
    for l in range(DEPTH):
        layers.append(weights_of(l, x))
        x, sv = _layer_fwd(x, layers[l], l)
        saved.append(sv)
    loss, dx, dxh, dgf = _loss_head(x, final_g, target, "loss_head")
    bigs, smalls = [None] * DEPTH, [None] * DEPTH
    for l in reversed(range(DEPTH)):
        dx, dxh, bigs[l], smalls[l] = _layer_bwd(dx, dxh, layers[l], saved[l], l, reduce_of(l) if reduce_of else None)
    small = jnp.concatenate(smalls[0] + smalls[1] + [dgf[0:1]], axis=0)
    small = jnp.pad(small, ((0, SMALL_ROWS - small.shape[0]), (0, 0)))
    return loss[0, 0], dx, bigs, small


def kernel(x, norm1_g, w_in, w_fg2, b_fg, gla_norm_g, w_oa, conv_w, conv_b, w_ob, w_o, norm2_g, w_ffn_gate, w_ffn_up, w_ffn_down, final_g, loss_target, m_norm1_g, m_w_in, m_w_fg2, m_b_fg, m_gla_norm_g, m_w_oa, m_conv_w, m_conv_b, m_w_ob, m_w_o, m_norm2_g, m_w_ffn_gate, m_w_ffn_up, m_w_ffn_down, m_final_g, v_norm1_g, v_w_in, v_w_fg2, v_b_fg, v_gla_norm_g, v_w_oa, v_conv_w, v_conv_b, v_w_ob, v_w_o, v_norm2_g, v_w_ffn_gate, v_w_ffn_up, v_w_ffn_down, v_final_g):
    names = ["norm1_g", "w_in", "w_fg2", "b_fg", "gla_norm_g", "w_oa", "conv_w", "conv_b", "w_ob", "w_o",
             "norm2_g", "w_ffn_gate", "w_ffn_up", "w_ffn_down", "final_g"]
    w = dict(zip(names, [norm1_g, w_in, w_fg2, b_fg, gla_norm_g, w_oa, conv_w, conv_b, w_ob, w_o, norm2_g,
                         w_ffn_gate, w_ffn_up, w_ffn_down, final_g]))
    m = dict(zip(names, [m_norm1_g, m_w_in, m_w_fg2, m_b_fg, m_gla_norm_g, m_w_oa, m_conv_w, m_conv_b, m_w_ob,
                         m_w_o, m_norm2_g, m_w_ffn_gate, m_w_ffn_up, m_w_ffn_down, m_final_g]))
    v = dict(zip(names, [v_norm1_g, v_w_in, v_w_fg2, v_b_fg, v_gla_norm_g, v_w_oa, v_conv_w, v_conv_b, v_w_ob,
                         v_w_o, v_norm2_g, v_w_ffn_gate, v_w_ffn_up, v_w_ffn_down, v_final_g]))
    col_names = ["w_in", "w_ffn_gate", "w_ffn_up"]
    cx, cy, cc = _me()
    prm = jnp.asarray(SHIFT_TABLE)[4 * cx + 2 * cy + cc]
    core = jnp.reshape(cc, (1,)).astype(jnp.int32)
    chip = jnp.reshape(2 * cx + cy, (1,)).astype(jnp.int32)

    ids = iter(range(32))

    gathered, previous = [], None
    for l in range(DEPTH):
        rows, tail = _pack_rows(w_oa, w_ob, w_o, w_ffn_down, w_fg2, conv_w, l)
        win_in, win_gu = _window_cols(prm, w_in, w_ffn_gate, w_ffn_up, l, f"l{l}_windows", after=previous)
        previous = rows
        if l == 0:
            first = _all_gather([win_in, tail], "l0_gather_in", next(ids))
            gathered.append(list(first) + list(_all_gather([win_gu, rows], "l0_gather_rest", next(ids))))
        else:
            all_in, all_tail, all_gu, all_rows = _all_gather([win_in, tail, win_gu, rows], f"l{l}_gather", next(ids))
            gathered.append([all_in, all_tail, all_gu, all_rows])

    def weights_of(l, x_in):
        all_in, all_tail, all_gu, all_rows = gathered[l]
        after = (x_in,) if l > 0 else ()
        w_in_full, tail = _assemble_in(all_in, all_tail, f"l{l}_assemble_in", after)
        w_fg2_full, conv_w_full = _unpack_tail(tail)

        def rest(after_rest):
            names_rest = ("w_gu", "w_oa", "w_ob", "w_o", "w_down")
            return dict(zip(names_rest, _assemble_rest(all_gu, all_rows, f"l{l}_assemble_rest", after + after_rest)))

        return dict(w_in=w_in_full, rest=rest, w_fg2=w_fg2_full,
                    conv_w=conv_w_full, norm1_g=norm1_g[l][None], b_fg=b_fg[l][None],
                    gla_norm_g=gla_norm_g[l][None], conv_b=conv_b[l][None], norm2_g=norm2_g[l][None])

    pending = [dict() for _ in range(DEPTH)]
    landed = [()]

    def reduce_of(l):
        def reduce(group, grads):
            tag = f"l{l}_{group}"
            if group == "gu":
                packed = _grad_windows_gu(grads, tag + "_windows")
            elif group == "in":
                packed = _grad_windows_in(grads, tag + "_windows")
            else:
                packed = _grad_rows(grads)
            (from_sibling,) = _send_to_sibling([packed], tag + "_to_sibling", next(ids))
            last = l == 0 and group == "in"
            sums = _pair_sum(packed, from_sibling, core, tag + "_pair_sum", after=() if last else landed[0])
            (from_chips,) = _send_to_chips([sums], tag + "_to_chips", next(ids))
            landed[0] = (from_chips,)
            pending[l][group] = (sums, from_chips)
            return (packed,), (sums,)
        return reduce

    loss, dx, bigs, small = _local_step(x[0], loss_target[0], weights_of, final_g[None], reduce_of)

    grads, deltas, new_m, new_v = {}, {}, {}, {}
    for kind, group_names in (("gu", col_names[1:]), ("in", col_names[:1])):
        params = [t for n in group_names for t in (w[n], m[n], v[n])]
        out = None
        for l in reversed(range(DEPTH)):
            sums, from_chips = pending[l][kind]
            out = _final_windows(prm, chip, sums, from_chips, params, l, kind, f"l{l}_{kind}_final", out)
        for n, (g, d, nm, nv) in zip(group_names, out):
            grads[n], deltas[n], new_m[n], new_v[n] = g, d, nm, nv
    row_grads = [_ungrad_rows(_chip_sum(*pending[l]["rows"], chip, f"l{l}_rows_chip_sum")) for l in range(DEPTH)]
    for n in row_grads[0]:
        grads[n] = jnp.stack([row_grads[l][n] for l in range(DEPTH)])

    small_sum = _sum_devices(_all_gather([small], "gather_small")[0], "sum_small")
    r512, r256 = slice(0, KEY_W), slice(0, HEAD_V)
    grads.update(
        norm1_g=jnp.stack([small_sum[0], small_sum[5]]), b_fg=jnp.stack([small_sum[1, r512], small_sum[6, r512]]),
        gla_norm_g=jnp.stack([small_sum[2, r256], small_sum[7, r256]]),
        conv_b=jnp.stack([small_sum[3], small_sum[8]]), norm2_g=jnp.stack([small_sum[4], small_sum[9]]),
        final_g=small_sum[10])

    for n in names:
        if n not in col_names:
            deltas[n], new_m[n], new_v[n] = _adamw(w[n], grads[n], m[n], v[n], "adamw_" + n)

    total_loss = lax.psum(loss, ("x", "y", "c"))
    return (total_loss, dx[None], *[grads[n] for n in names], *[deltas[n] for n in names],
            *[new_m[n] for n in names], *[new_v[n] for n in names])
```

```python
import functools

import jax
import jax.numpy as jnp
import numpy as np
from jax import lax
from jax.experimental import pallas as pl
from jax.experimental.pallas import tpu as pltpu
from jax.experimental.pallas import tpu_sc as plsc

F32 = jnp.float32
BF16 = jnp.bfloat16
MESH = pl.DeviceIdType.MESH

D_MODEL = 1024
DEPTH = 2
CHUNK = 64
HEADS = 4
HEAD_K = 128
HEAD_V = 256
KEY_W = HEADS * HEAD_K
VAL_W = HEADS * HEAD_V
RANK = 16
TAU = 16.0
FFN = 2816
IN_WIDTH = 2 * KEY_W + 2 * VAL_W + RANK + 5 * D_MODEL
EPS = 1e-6
Q_SCALE = HEAD_K ** -0.5
N_DEV = 8
ADAM_LR, ADAM_B1, ADAM_B2, ADAM_EPS, ADAM_WD, ADAM_STEP = 0.001, 0.9, 0.999, 1e-08, 0.01, 10

LANES = 128
SUBLANES_BF16 = 16
VMEM_LIMIT = 48 * 1024 * 1024

FZ_COL = 2 * KEY_W + 2 * VAL_W + 5 * D_MODEL
PROJ_W = FZ_COL + LANES
SEG_R, SEG_GBI, SEG_GCI, SEG_CX, SEG_GA, SEG_GB = 2, 3, 4, 5, 6, 7

W_IN_SHARD = IN_WIDTH // N_DEV
FFN_SHARD = FFN // N_DEV
ROW_SHARD = D_MODEL // N_DEV

WIN_IN = 9 * LANES
WIN_GU = 4 * LANES
A_FZ = WIN_IN
IN_COLS = A_FZ + LANES
GU_COLS = 2 * WIN_GU
ORIG_FZ = 2 * KEY_W + 2 * VAL_W


def _new_col(o):
    if o < ORIG_FZ:
        return o
    if o < ORIG_FZ + RANK:
        return FZ_COL + (o - ORIG_FZ)
    return o - RANK


def _shift_table():
    t_in, rows = [], []
    for j in range(N_DEV):
        new = [_new_col(W_IN_SHARD * j + i) for i in range(W_IN_SHARD)]
        main = [i for i in range(W_IN_SHARD) if new[i] < FZ_COL]
        code = [i for i in range(W_IN_SHARD) if new[i] >= FZ_COL]
        shift = new[main[0]] - main[0]
        t_in.append(shift // LANES)
        assert all(new[i] - i == shift for i in main) and shift % LANES + W_IN_SHARD <= WIN_IN
        if code:
            cshift = new[code[0]] - FZ_COL - code[0]
            crow = [cshift % LANES, code[0], code[-1] + 1, int(cshift < 0)]
        else:
            crow = [0, 0, 0, 0]
        rows.append([shift % LANES, main[0], main[-1] + 1] + crow + [FFN_SHARD * j % LANES])
    return tuple(t_in), np.asarray(rows, np.int32)


T_IN, SHIFT_TABLE = _shift_table()
T_GU = tuple(FFN_SHARD * j // LANES for j in range(N_DEV))

B_OA, B_OB, B_O, B_DOWN = 0, ROW_SHARD, 2 * ROW_SHARD, 3 * ROW_SHARD
B_FG2 = B_DOWN + FFN_SHARD
B_CONV = B_FG2 + 1
B_ROWS = B_FG2 + SUBLANES_BF16
SMALL_ROWS = 16


def _pick(n, candidates):
    for c in candidates:
        if n % c == 0:
            return c
    return n


def _cparams(sem):
    return pltpu.CompilerParams(dimension_semantics=sem, vmem_limit_bytes=VMEM_LIMIT)


def _sigmoid(x):
    return 1.0 / (1.0 + jnp.exp(-x))


def _matmul(a, b, dims, out_dtype, name, residual=None, after=(), into=None):
    if dims == "nn":
        (m, k), (k2, n) = a.shape, b.shape
    elif dims == "nt":
        (m, k), (n, k2) = a.shape, b.shape
    else:
        (k, m), (k2, n) = a.shape, b.shape
    assert k == k2, (a.shape, b.shape, dims)
    tm = _pick(m, (1024, 1408, 512, 256, 128))
    tn = _pick(n, (1664, 1408, 1024, 512, 256, 128))
    tk = _pick(k, (1664, 1408, 1024, 512, 256, 128))
    nk = k // tk
    if dims == "nn":
        a_spec = pl.BlockSpec((tm, tk), lambda i, j, kk: (i, kk))
        b_spec = pl.BlockSpec((tk, tn), lambda i, j, kk: (kk, j))
        contract = (((1,), (0,)), ((), ()))
    elif dims == "nt":
        a_spec = pl.BlockSpec((tm, tk), lambda i, j, kk: (i, kk))
        b_spec = pl.BlockSpec((tn, tk), lambda i, j, kk: (j, kk))
        contract = (((1,), (1,)), ((), ()))
    else:
        a_spec = pl.BlockSpec((tk, tm), lambda i, j, kk: (kk, i))
        b_spec = pl.BlockSpec((tk, tn), lambda i, j, kk: (kk, j))
        contract = (((0,), (0,)), ((), ()))
    o_spec = pl.BlockSpec((tm, tn), lambda i, j, kk: (i, j))
    has_res = residual is not None
    out_spec, out_struct, placed, aliases = o_spec, jax.ShapeDtypeStruct((m, n), out_dtype), (), {}
    if into is not None:
        buffer, col = into
        assert col % tn == 0 and buffer.dtype == out_dtype and not has_res
        out_spec = pl.BlockSpec((tm, tn), lambda i, j, kk: (i, col // tn + j))
        out_struct, placed, aliases = jax.ShapeDtypeStruct(buffer.shape, out_dtype), (buffer,), {2 + len(after): 0}

    def body(*refs):
        a_ref, b_ref = refs[:2]
        r_ref = refs[2] if has_res else None
        o_ref = refs[2 + has_res + len(after) + len(placed)]
        kk = pl.program_id(2)
        part = lax.dot_general(a_ref[...], b_ref[...], contract, preferred_element_type=F32)

        def finish(total):
            if has_res:
                total = total + r_ref[...]
            o_ref[...] = total.astype(o_ref.dtype)

        if nk == 1:
            finish(part)
            return
        acc_ref = refs[-1]

        @pl.when(kk == 0)
        def _():
            acc_ref[...] = part

        @pl.when((kk > 0) & (kk < nk - 1))
        def _():
            acc_ref[...] += part

        @pl.when(kk == nk - 1)
        def _():
            finish(acc_ref[...] + part)

    in_specs = [a_spec, b_spec] + ([o_spec] if has_res else []) + [_after_spec(t) for t in after]
    in_specs += [_ANY] * len(placed)
    args = (a, b) + ((residual,) if has_res else ()) + tuple(after) + placed
    return pl.pallas_call(
        body, grid=(m // tm, n // tn, nk), in_specs=in_specs, out_specs=out_spec,
        out_shape=out_struct, input_output_aliases=aliases,
        scratch_shapes=[pltpu.VMEM((tm, tn), F32)] if nk > 1 else [], name=name,
        compiler_params=_cparams(("parallel", "parallel", "arbitrary")),
    )(*args)


def _rmsnorm_fwd(x, g, name):
    s, d = x.shape
    tm = _pick(s, (512, 256))

    def body(x_ref, g_ref, o_ref):
        xv = x_ref[...]
        r = lax.rsqrt(jnp.mean(xv * xv, axis=-1, keepdims=True) + EPS)
        o_ref[...] = (xv * r * g_ref[...]).astype(o_ref.dtype)

    row = pl.BlockSpec((tm, d), lambda i: (i, 0))
    return pl.pallas_call(
        body, grid=(s // tm,), in_specs=[row, pl.BlockSpec((1, d), lambda i: (0, 0))], out_specs=row,
        out_shape=jax.ShapeDtypeStruct((s, d), BF16), name=name, compiler_params=_cparams(("parallel",)),
    )(x, g)


def _rmsnorm_bwd(x, g, dh, dres, name, after=()):
    s, d = x.shape
    tm = _pick(s, (512, 256))

    def body(x_ref, g_ref, dh_ref, dres_ref, *rest):
        dx_ref, dx16_ref, dg_ref = rest[-3:]
        xv = x_ref[...]
        r = lax.rsqrt(jnp.mean(xv * xv, axis=-1, keepdims=True) + EPS)
        xn = xv * r
        dhv = dh_ref[...].astype(F32)
        dxn = dhv * g_ref[...]
        dx = dres_ref[...] + r * (dxn - xn * jnp.mean(dxn * xn, axis=-1, keepdims=True))
        dx_ref[...] = dx
        dx16_ref[...] = dx.astype(BF16)

        @pl.when(pl.program_id(0) == 0)
        def _():
            dg_ref[...] = jnp.zeros_like(dg_ref)

        dg_ref[...] += jnp.broadcast_to(jnp.sum(dhv * xn, axis=0, keepdims=True), dg_ref.shape)

    row = pl.BlockSpec((tm, d), lambda i: (i, 0))
    acc = pl.BlockSpec((8, d), lambda i: (0, 0))
    return pl.pallas_call(
        body, grid=(s // tm,),
        in_specs=[row, pl.BlockSpec((1, d), lambda i: (0, 0)), row, row] + [_after_spec(t) for t in after],
        out_specs=[row, row, acc],
        out_shape=[jax.ShapeDtypeStruct((s, d), F32), jax.ShapeDtypeStruct((s, d), BF16),
                   jax.ShapeDtypeStruct((8, d), F32)],
        name=name, compiler_params=_cparams(("arbitrary",)),
    )(x, g, dh, dres, *after)


def _loss_head(x, g, target, name):
    s, d = x.shape
    tm = _pick(s, (512, 256))

    def body(x_ref, g_ref, t_ref, loss_ref, dx_ref, dx16_ref, dg_ref):
        xv = x_ref[...]
        gv = g_ref[...]
        r = lax.rsqrt(jnp.mean(xv * xv, axis=-1, keepdims=True) + EPS)
        xn = xv * r
        err = xn * gv - t_ref[...]
        dy = err * (1.0 / d)
        dxn = dy * gv
        dx = r * (dxn - xn * jnp.mean(dxn * xn, axis=-1, keepdims=True))
        dx_ref[...] = dx
        dx16_ref[...] = dx.astype(BF16)

        @pl.when(pl.program_id(0) == 0)
        def _():
            dg_ref[...] = jnp.zeros_like(dg_ref)
            loss_ref[...] = jnp.zeros_like(loss_ref)

        dg_ref[...] += jnp.broadcast_to(jnp.sum(dy * xn, axis=0, keepdims=True), dg_ref.shape)
        row_loss = jnp.sum(err * err, axis=-1, keepdims=True)
        loss_ref[...] += jnp.broadcast_to((0.5 / d) * jnp.sum(row_loss, axis=0, keepdims=True), loss_ref.shape)

    row = pl.BlockSpec((tm, d), lambda i: (i, 0))
    return pl.pallas_call(
        body, grid=(s // tm,), in_specs=[row, pl.BlockSpec((1, d), lambda i: (0, 0)), row],
        out_specs=[pl.BlockSpec((8, LANES), lambda i: (0, 0)), row, row, pl.BlockSpec((8, d), lambda i: (0, 0))],
        out_shape=[jax.ShapeDtypeStruct((8, LANES), F32), jax.ShapeDtypeStruct((s, d), F32),
                   jax.ShapeDtypeStruct((s, d), BF16), jax.ShapeDtypeStruct((8, d), F32)],
        name=name, compiler_params=_cparams(("arbitrary",)),
    )(x, g, target)


def _tri_dot(tri, x):
    hi = x.astype(BF16)
    lo = (x - hi.astype(F32)).astype(BF16)
    return jnp.dot(tri, hi, preferred_element_type=F32) + jnp.dot(tri, lo, preferred_element_type=F32)


def _chunk_masks(rows):
    r, c = np.arange(rows)[:, None], np.arange(rows)[None, :]
    same = (r // CHUNK) == (c // CHUNK)
    return [jnp.asarray(m, BF16) for m in (same & (r >= c), same & (r > c), same)]


def _block_decay(fz, w, b, incl, ones=None):
    fg = jnp.dot(fz, w, preferred_element_type=F32) + b
    la = (jnp.minimum(fg, 0.0) - jnp.log(1.0 + jnp.exp(-jnp.abs(fg)))) * (1.0 / TAU)
    cum = _tri_dot(incl, la)
    ends = [cum[i + CHUNK - 1:i + CHUNK, :] for i in range(0, fz.shape[0], CHUNK)]
    if ones is None:
        end = jnp.concatenate([jnp.broadcast_to(e, (CHUNK, e.shape[1])) for e in ends], axis=0)
    else:
        end = _tri_dot(ones, la)
    return fg, jnp.exp(end - cum), [jnp.exp(e) for e in ends]


_TN = (((0,), (0,)), ((), ()))
_NT = (((1,), (1,)), ((), ()))


def _gla_specs(rows):
    q_spec = pl.BlockSpec((rows, HEAD_K), lambda h, c: (c, h))
    k_spec = pl.BlockSpec((rows, HEAD_K), lambda h, c: (c, HEADS + h))
    v_spec = pl.BlockSpec((rows, HEAD_V), lambda h, c: (c, HEADS + h))
    fz_spec = pl.BlockSpec((rows, LANES), lambda h, c: (c, FZ_COL // LANES))
    w_spec = pl.BlockSpec((LANES, HEAD_K), lambda h, c: (0, h))
    b_spec = pl.BlockSpec((1, HEAD_K), lambda h, c: (0, h))
    mask_spec = pl.BlockSpec((rows, rows), lambda h, c: (0, 0))
    return q_spec, k_spec, v_spec, fz_spec, w_spec, b_spec, mask_spec


def _gla_fwd(proj, wfg, bfg, name):
    s = proj.shape[0]
    nc = s // CHUNK
    per = _pick(nc, (8, 4, 2, 1))
    rows = per * CHUNK
    incl, _, ones = _chunk_masks(rows)

    def body(q_ref, k_ref, v_ref, fz_ref, w_ref, b_ref, incl_ref, ones_ref, o_ref, st_ref, state, update):
        @pl.when(pl.program_id(1) == 0)
        def _():
            state[...] = jnp.zeros_like(state)

        _, dec, gammas = _block_decay(fz_ref[...], w_ref[...], b_ref[...], incl_ref[...], ones_ref[...])
        kd = (k_ref[...].astype(F32) * dec).astype(BF16)
        qs = (q_ref[...].astype(F32) * Q_SCALE).astype(BF16)
        for i in range(per):
            sl = slice(i * CHUNK, (i + 1) * CHUNK)
            update[i] = lax.dot_general(v_ref[sl, :], kd[sl], _TN, preferred_element_type=F32)
        st = state[...]
        for i in range(per):
            st = st * gammas[i] + update[i]
            st_ref[0, i] = st.astype(BF16)
        state[...] = st
        for i in range(per):
            sl = slice(i * CHUNK, (i + 1) * CHUNK)
            o_ref[sl, :] = lax.dot_general(qs[sl], st_ref[0, i], _NT, preferred_element_type=F32).astype(o_ref.dtype)

    q_spec, k_spec, v_spec, fz_spec, w_spec, b_spec, mask_spec = _gla_specs(rows)
    return pl.pallas_call(
        body, grid=(HEADS, nc // per),
        in_specs=[q_spec, k_spec, v_spec, fz_spec, w_spec, b_spec, mask_spec, mask_spec],
        out_specs=[pl.BlockSpec((rows, HEAD_V), lambda h, c: (c, h)),
                   pl.BlockSpec((1, per, HEAD_V, HEAD_K), lambda h, c: (h, c, 0, 0))],
        out_shape=[jax.ShapeDtypeStruct((s, VAL_W), BF16),
                   jax.ShapeDtypeStruct((HEADS, nc, HEAD_V, HEAD_K), BF16)],
        scratch_shapes=[pltpu.VMEM((HEAD_V, HEAD_K), F32), pltpu.VMEM((per, HEAD_V, HEAD_K), F32)], name=name,
        compiler_params=_cparams(("parallel", "arbitrary")),
    )(proj, proj, proj, proj, wfg, bfg, incl, ones)


def _gla_bwd(proj, wfg, bfg, states, do, name):
    s = proj.shape[0]
    nc = s // CHUNK
    per = _pick(nc, (8, 4, 2, 1))
    rows = per * CHUNK
    nblk = nc // per
    incl, strict, _ = _chunk_masks(rows)

    def rev(spec_fn):
        return lambda h, j: spec_fn(h, nblk - 1 - j)

    def body(q_ref, k_ref, v_ref, fz_ref, w_ref, b_ref, incl_ref, strict_ref, do_ref, st_ref, prev_ref,
             dq_ref, dk_ref, dv_ref, dfg_ref, db_ref, carry, gt_all, dkd_all, dgg_all):
        j = pl.program_id(1)

        @pl.when(j == 0)
        def _():
            carry[...] = jnp.zeros_like(carry)
            db_ref[...] = jnp.zeros_like(db_ref)

        fg, dec, gammas = _block_decay(fz_ref[...], w_ref[...], b_ref[...], incl_ref[...])
        kd = k_ref[...].astype(F32) * dec
        kd16 = kd.astype(BF16)
        qs = (q_ref[...].astype(F32) * Q_SCALE).astype(BF16)
        for i in range(per):
            sl = slice(i * CHUNK, (i + 1) * CHUNK)
            gt_all[i] = lax.dot_general(do_ref[sl, :], qs[sl], _TN, preferred_element_type=F32)
        back = carry[...]
        for i in reversed(range(per)):
            gt = back + gt_all[i]
            gt_all[i] = gt
            back = gt * gammas[i]
        carry[...] = back
        has_prev = (j < nblk - 1).astype(F32)
        for i in range(per):
            sl = slice(i * CHUNK, (i + 1) * CHUNK)
            gt = gt_all[i]
            gt16 = gt.astype(BF16)
            dov = do_ref[sl, :]
            dq_ref[sl, :] = (jnp.dot(dov, st_ref[0, i], preferred_element_type=F32) * Q_SCALE).astype(dq_ref.dtype)
            dkd_all[sl, :] = jnp.dot(v_ref[sl, :], gt16, preferred_element_type=F32)
            dv_ref[sl, :] = lax.dot_general(kd16[sl], gt16, _NT, preferred_element_type=F32).astype(dv_ref.dtype)
            if i > 0:
                st_prev = st_ref[0, i - 1].astype(F32)
            else:
                st_prev = prev_ref[0, 0].astype(F32) * has_prev
            dgamma = jnp.sum(gt * st_prev, axis=0, keepdims=True)
            dgg_all[sl, :] = jnp.broadcast_to(dgamma * gammas[i], (CHUNK, HEAD_K))
        dkd = dkd_all[...]
        dk_ref[...] = (dkd * dec).astype(dk_ref.dtype)
        dla = dgg_all[...] + _tri_dot(strict_ref[...], dkd * kd)
        dfg = dla * (1.0 / TAU) * _sigmoid(-fg)
        dfg_ref[...] = dfg.astype(dfg_ref.dtype)
        db_ref[...] += jnp.broadcast_to(jnp.sum(dfg, axis=0, keepdims=True), db_ref.shape)

    q_spec, k_spec, v_spec, fz_spec, w_spec, b_spec, mask_spec = _gla_specs(rows)
    q_spec, k_spec, v_spec, fz_spec = [
        pl.BlockSpec(sp.block_shape, rev(sp.index_map)) for sp in (q_spec, k_spec, v_spec, fz_spec)]
    do_spec = pl.BlockSpec((rows, HEAD_V), lambda h, j: (nblk - 1 - j, h))
    st_spec = pl.BlockSpec((1, per, HEAD_V, HEAD_K), lambda h, j: (h, nblk - 1 - j, 0, 0))
    prev_spec = pl.BlockSpec((1, 1, HEAD_V, HEAD_K),
                             lambda h, j: (h, jnp.maximum((nblk - 1 - j) * per - 1, 0), 0, 0))
    key_out = pl.BlockSpec((rows, HEAD_K), lambda h, j: (nblk - 1 - j, h))
    return pl.pallas_call(
        body, grid=(HEADS, nblk),
        in_specs=[q_spec, k_spec, v_spec, fz_spec, w_spec, b_spec, mask_spec, mask_spec,
                  do_spec, st_spec, prev_spec],
        out_specs=[key_out, key_out, do_spec, key_out, pl.BlockSpec((8, HEAD_K), lambda h, j: (0, h))],
        out_shape=[jax.ShapeDtypeStruct((s, KEY_W), BF16), jax.ShapeDtypeStruct((s, KEY_W), BF16),
                   jax.ShapeDtypeStruct((s, VAL_W), BF16), jax.ShapeDtypeStruct((s, KEY_W), BF16),
                   jax.ShapeDtypeStruct((8, KEY_W), F32)],
        scratch_shapes=[pltpu.VMEM((HEAD_V, HEAD_K), F32), pltpu.VMEM((per, HEAD_V, HEAD_K), F32),
                        pltpu.VMEM((rows, HEAD_K), F32), pltpu.VMEM((rows, HEAD_K), F32)], name=name,
        compiler_params=_cparams(("parallel", "arbitrary")),
    )(proj, proj, proj, proj, wfg, bfg, incl, strict, do, states, states)


def _place_qkv(dq, dk, dv, dproj, name):
    s = dq.shape[0]
    tm = _pick(s, (512, 256))

    def body(dq_ref, dk_ref, dv_ref, _, o_ref):
        o_ref[:, 0:KEY_W] = dq_ref[...]
        o_ref[:, KEY_W:2 * KEY_W] = dk_ref[...]
        o_ref[:, 2 * KEY_W:2 * KEY_W + VAL_W] = dv_ref[...]

    def rows(width):
        return pl.BlockSpec((tm, width), lambda i: (i, 0))

    return pl.pallas_call(
        body, grid=(s // tm,), in_specs=[rows(KEY_W), rows(KEY_W), rows(VAL_W), _ANY],
        out_specs=rows(2 * KEY_W + VAL_W), out_shape=jax.ShapeDtypeStruct(dproj.shape, BF16),
        input_output_aliases={3: 0}, name=name, compiler_params=_cparams(("parallel",)),
    )(dq, dk, dv, dproj)


def _seg(tm, seg):
    return pl.BlockSpec((tm, D_MODEL), lambda i: (i, seg))


def _gla_post_fwd(o, proj, g, name):
    s = o.shape[0]
    tm = _pick(s, (512, 256))

    def body(o_ref, r_ref, g_ref, oa_ref):
        gv = g_ref[...]
        for h in range(HEADS):
            sl = slice(h * HEAD_V, (h + 1) * HEAD_V)
            ov = o_ref[:, sl].astype(F32)
            rstd = lax.rsqrt(jnp.mean(ov * ov, axis=-1, keepdims=True) + EPS)
            rv = r_ref[:, sl].astype(F32)
            oa_ref[:, sl] = (ov * rstd * gv * (rv * _sigmoid(rv))).astype(oa_ref.dtype)

    row = pl.BlockSpec((tm, VAL_W), lambda i: (i, 0))
    return pl.pallas_call(
        body, grid=(s // tm,), in_specs=[row, _seg(tm, SEG_R), pl.BlockSpec((1, HEAD_V), lambda i: (0, 0))],
        out_specs=row, out_shape=jax.ShapeDtypeStruct((s, VAL_W), BF16), name=name,
        compiler_params=_cparams(("parallel",)),
    )(o, proj, g)


def _gla_post_bwd(doa, o, proj, g, dproj, name):
    s = o.shape[0]
    tm = _pick(s, (512, 256))

    def body(doa_ref, o_ref, r_ref, g_ref, _, dr_ref, do_ref, dg_ref):
        @pl.when(pl.program_id(0) == 0)
        def _():
            dg_ref[...] = jnp.zeros_like(dg_ref)

        gv = g_ref[...]
        dg = jnp.zeros((1, HEAD_V), F32)
        for h in range(HEADS):
            sl = slice(h * HEAD_V, (h + 1) * HEAD_V)
            ov = o_ref[:, sl].astype(F32)
            rstd = lax.rsqrt(jnp.mean(ov * ov, axis=-1, keepdims=True) + EPS)
            ohat = ov * rstd
            rv = r_ref[:, sl].astype(F32)
            sg = _sigmoid(rv)
            dv = doa_ref[:, sl].astype(F32)
            dr_ref[:, sl] = (dv * ohat * gv * (sg * (1.0 + rv * (1.0 - sg)))).astype(dr_ref.dtype)
            don = dv * (rv * sg)
            dg = dg + jnp.sum(don * ohat, axis=0, keepdims=True)
            dohat = don * gv
            do_ref[:, sl] = (rstd * (dohat - ohat * jnp.mean(dohat * ohat, axis=-1, keepdims=True))
                             ).astype(do_ref.dtype)
        dg_ref[...] += jnp.broadcast_to(dg, dg_ref.shape)

    row = pl.BlockSpec((tm, VAL_W), lambda i: (i, 0))
    return pl.pallas_call(
        body, grid=(s // tm,),
        in_specs=[row, row, _seg(tm, SEG_R), pl.BlockSpec((1, HEAD_V), lambda i: (0, 0)), _ANY],
        out_specs=[_seg(tm, SEG_R), row, pl.BlockSpec((8, HEAD_V), lambda i: (0, 0))],
        out_shape=[jax.ShapeDtypeStruct(dproj.shape, BF16), jax.ShapeDtypeStruct((s, VAL_W), BF16),
                   jax.ShapeDtypeStruct((8, HEAD_V), F32)],
        input_output_aliases={4: 0}, name=name, compiler_params=_cparams(("arbitrary",)),
    )(doa, o, proj, g, dproj)


HALO = SUBLANES_BF16


def _shift_down(u, p1, p2, n, rows):
    rolled = pltpu.roll(u, n, 0)
    if n == 1:
        return jnp.where(rows == 0, p1, rolled)
    return jnp.where(rows == 0, p2, jnp.where(rows == 1, p1, rolled))


def _shift_up(u, n1, n2, n, rows, tm):
    rolled = pltpu.roll(u, tm - n, 0)
    if n == 1:
        return jnp.where(rows == tm - 1, n1, rolled)
    return jnp.where(rows == tm - 2, n1, jnp.where(rows == tm - 1, n2, rolled))


def _conv_terms(gc_ref, cx_ref, gcp_ref, cxp_ref, tm):
    i = pl.program_id(0)
    u = gc_ref[...].astype(F32) * cx_ref[...].astype(F32)
    up = gcp_ref[...].astype(F32) * cxp_ref[...].astype(F32) * (i > 0).astype(F32)
    rows = lax.broadcasted_iota(jnp.int32, (tm, 1), 0)
    u1 = _shift_down(u, up[HALO - 1:HALO, :], up[HALO - 2:HALO - 1, :], 1, rows)
    u2 = _shift_down(u, up[HALO - 1:HALO, :], up[HALO - 2:HALO - 1, :], 2, rows)
    return u, u1, u2, rows


def _prev_halo(tm, seg):
    return pl.BlockSpec((HALO, D_MODEL), lambda i: (jnp.maximum(i * (tm // HALO) - 1, 0), seg))


def _conv_fwd(proj, w, b, name):
    s = proj.shape[0]
    tm = _pick(s, (512, 256))

    def body(gbi_ref, gc_ref, cx_ref, gcp_ref, cxp_ref, w_ref, b_ref, cb_ref):
        u, u1, u2, _ = _conv_terms(gc_ref, cx_ref, gcp_ref, cxp_ref, tm)
        conv = w_ref[0:1, :] * u2 + w_ref[1:2, :] * u1 + w_ref[2:3, :] * u + b_ref[...]
        cb_ref[...] = (gbi_ref[...].astype(F32) * conv).astype(cb_ref.dtype)

    return pl.pallas_call(
        body, grid=(s // tm,),
        in_specs=[_seg(tm, SEG_GBI), _seg(tm, SEG_GCI), _seg(tm, SEG_CX),
                  _prev_halo(tm, SEG_GCI), _prev_halo(tm, SEG_CX),
                  pl.BlockSpec((3, D_MODEL), lambda i: (0, 0)), pl.BlockSpec((1, D_MODEL), lambda i: (0, 0))],
        out_specs=pl.BlockSpec((tm, D_MODEL), lambda i: (i, 0)),
        out_shape=jax.ShapeDtypeStruct((s, D_MODEL), BF16), name=name, compiler_params=_cparams(("parallel",)),
    )(proj, proj, proj, proj, proj, w, b)


def _conv_bwd(dcb, proj, w, b, dproj, name):
    s = proj.shape[0]
    tm = _pick(s, (512, 256))
    nt = s // tm

    def body(dcb_ref, gbi_ref, gc_ref, cx_ref, gcp_ref, cxp_ref, dcbn_ref, gbin_ref, w_ref, b_ref, _,
             d3_ref, dwb_ref):
        i = pl.program_id(0)

        @pl.when(i == 0)
        def _():
            dwb_ref[...] = jnp.zeros_like(dwb_ref)

        u, u1, u2, rows = _conv_terms(gc_ref, cx_ref, gcp_ref, cxp_ref, tm)
        w0, w1, w2 = w_ref[0:1, :], w_ref[1:2, :], w_ref[2:3, :]
        conv = w0 * u2 + w1 * u1 + w2 * u + b_ref[...]
        dcbv = dcb_ref[...].astype(F32)
        gbi = gbi_ref[...].astype(F32)
        dconv = dcbv * gbi
        dnext = dcbn_ref[...].astype(F32) * gbin_ref[...].astype(F32) * (i < nt - 1).astype(F32)
        dc1 = _shift_up(dconv, dnext[0:1, :], dnext[1:2, :], 1, rows, tm)
        dc2 = _shift_up(dconv, dnext[0:1, :], dnext[1:2, :], 2, rows, tm)
        du = w2 * dconv + w1 * dc1 + w0 * dc2
        d3_ref[:, 0:D_MODEL] = (dcbv * conv).astype(d3_ref.dtype)
        d3_ref[:, D_MODEL:2 * D_MODEL] = (du * cx_ref[...].astype(F32)).astype(d3_ref.dtype)
        d3_ref[:, 2 * D_MODEL:3 * D_MODEL] = (du * gc_ref[...].astype(F32)).astype(d3_ref.dtype)
        dwb_ref[0:1, :] += jnp.sum(dconv * u2, axis=0, keepdims=True)
        dwb_ref[1:2, :] += jnp.sum(dconv * u1, axis=0, keepdims=True)
        dwb_ref[2:3, :] += jnp.sum(dconv * u, axis=0, keepdims=True)
        dwb_ref[3:4, :] += jnp.sum(dconv, axis=0, keepdims=True)

    def next_halo(seg_fn):
        return pl.BlockSpec((HALO, D_MODEL), lambda i: (jnp.minimum((i + 1) * (tm // HALO), s // HALO - 1), seg_fn))

    return pl.pallas_call(
        body, grid=(nt,),
        in_specs=[pl.BlockSpec((tm, D_MODEL), lambda i: (i, 0)),
                  _seg(tm, SEG_GBI), _seg(tm, SEG_GCI), _seg(tm, SEG_CX),
                  _prev_halo(tm, SEG_GCI), _prev_halo(tm, SEG_CX),
                  next_halo(0), next_halo(SEG_GBI),
                  pl.BlockSpec((3, D_MODEL), lambda i: (0, 0)), pl.BlockSpec((1, D_MODEL), lambda i: (0, 0)), _ANY],
        out_specs=[pl.BlockSpec((tm, 3 * D_MODEL), lambda i: (i, SEG_GBI // 3)),
                   pl.BlockSpec((8, D_MODEL), lambda i: (0, 0))],
        out_shape=[jax.ShapeDtypeStruct(dproj.shape, BF16), jax.ShapeDtypeStruct((8, D_MODEL), F32)],
        input_output_aliases={10: 0}, name=name, compiler_params=_cparams(("arbitrary",)),
    )(dcb, proj, proj, proj, proj, proj, dcb, proj, w, b, dproj)


def _mix_fwd(proj, ya, yb, name):
    s = proj.shape[0]
    tm = _pick(s, (512, 256))

    def body(ga_ref, gb_ref, ya_ref, yb_ref, o_ref):
        o_ref[...] = (_sigmoid(ga_ref[...].astype(F32)) * ya_ref[...].astype(F32)
                      + _sigmoid(gb_ref[...].astype(F32)) * yb_ref[...].astype(F32)).astype(o_ref.dtype)

    row = pl.BlockSpec((tm, D_MODEL), lambda i: (i, 0))
    return pl.pallas_call(
        body, grid=(s // tm,), in_specs=[_seg(tm, SEG_GA), _seg(tm, SEG_GB), row, row], out_specs=row,
        out_shape=jax.ShapeDtypeStruct((s, D_MODEL), BF16), name=name, compiler_params=_cparams(("parallel",)),
    )(proj, proj, ya, yb)


def _mix_bwd(dmix, proj, ya, yb, name):
    s = proj.shape[0]
    tm = _pick(s, (512, 256))

    def body(dm_ref, ga_ref, gb_ref, ya_ref, yb_ref, dg_ref, dya_ref, dyb_ref):
        dm = dm_ref[...].astype(F32)
        sa = _sigmoid(ga_ref[...].astype(F32))
        sb = _sigmoid(gb_ref[...].astype(F32))
        dg_ref[:, 0:D_MODEL] = (dm * ya_ref[...].astype(F32) * sa * (1.0 - sa)).astype(dg_ref.dtype)
        dg_ref[:, D_MODEL:2 * D_MODEL] = (dm * yb_ref[...].astype(F32) * sb * (1.0 - sb)).astype(dg_ref.dtype)
        dya_ref[...] = (dm * sa).astype(dya_ref.dtype)
        dyb_ref[...] = (dm * sb).astype(dyb_ref.dtype)

    row = pl.BlockSpec((tm, D_MODEL), lambda i: (i, 0))
    return pl.pallas_call(
        body, grid=(s // tm,), in_specs=[row, _seg(tm, SEG_GA), _seg(tm, SEG_GB), row, row],
        out_specs=[pl.BlockSpec((tm, 2 * D_MODEL), lambda i: (i, SEG_GA // 2)), row, row],
        out_shape=[jax.ShapeDtypeStruct((s, PROJ_W), BF16), jax.ShapeDtypeStruct((s, D_MODEL), BF16),
                   jax.ShapeDtypeStruct((s, D_MODEL), BF16)],
        name=name, compiler_params=_cparams(("parallel",)),
    )(dmix, proj, proj, ya, yb)


def _swiglu_fwd(gu, name):
    s = gu.shape[0]
    tm = _pick(s, (256,))

    def body(gu_ref, o_ref):
        gate = gu_ref[:, 0:FFN].astype(F32)
        o_ref[...] = (gate * _sigmoid(gate) * gu_ref[:, FFN:2 * FFN].astype(F32)).astype(o_ref.dtype)

    return pl.pallas_call(
        body, grid=(s // tm,), in_specs=[pl.BlockSpec((tm, 2 * FFN), lambda i: (i, 0))],
        out_specs=pl.BlockSpec((tm, FFN), lambda i: (i, 0)),
        out_shape=jax.ShapeDtypeStruct((s, FFN), BF16), name=name, compiler_params=_cparams(("parallel",)),
    )(gu)


def _swiglu_bwd(dhid, gu, name):
    s = gu.shape[0]
    tm = _pick(s, (256,))

    def body(dh_ref, gu_ref, o_ref):
        gate = gu_ref[:, 0:FFN].astype(F32)
        up = gu_ref[:, FFN:2 * FFN].astype(F32)
        dh = dh_ref[...].astype(F32)
        sg = _sigmoid(gate)
        o_ref[:, 0:FFN] = (dh * up * (sg * (1.0 + gate * (1.0 - sg)))).astype(o_ref.dtype)
        o_ref[:, FFN:2 * FFN] = (dh * gate * sg).astype(o_ref.dtype)

    wide = pl.BlockSpec((tm, 2 * FFN), lambda i: (i, 0))
    return pl.pallas_call(
        body, grid=(s // tm,), in_specs=[pl.BlockSpec((tm, FFN), lambda i: (i, 0)), wide], out_specs=wide,
        out_shape=jax.ShapeDtypeStruct((s, 2 * FFN), BF16), name=name, compiler_params=_cparams(("parallel",)),
    )(dhid, gu)


def _adamw_math(w, g, m, v):
    m2 = ADAM_B1 * m + (1.0 - ADAM_B1) * g
    v2 = ADAM_B2 * v + (1.0 - ADAM_B2) * (g * g)
    m_hat = m2 / (1.0 - ADAM_B1 ** ADAM_STEP)
    v_hat = v2 / (1.0 - ADAM_B2 ** ADAM_STEP)
    delta = -ADAM_LR * (m_hat / (jnp.sqrt(v_hat) + ADAM_EPS) + ADAM_WD * w)
    return delta, m2, v2


def _adamw(w, g, m, v, name):
    shape = w.shape
    cols = shape[-1]
    rows = int(np.prod(shape[:-1])) if len(shape) > 1 else 1
    w2, g2, m2, v2 = [t.reshape(rows, cols) for t in (w, g, m, v)]
    tr = _pick(rows, (512, 352, 256)) if rows % 8 == 0 else rows

    def body(w_ref, g_ref, m_ref, v_ref, d_ref, nm_ref, nv_ref):
        d, nm, nv = _adamw_math(w_ref[...], g_ref[...], m_ref[...], v_ref[...])
        d_ref[...] = d
        nm_ref[...] = nm
        nv_ref[...] = nv

    blk = pl.BlockSpec((tr, cols), lambda i: (i, 0))
    out = pl.pallas_call(
        body, grid=(rows // tr,), in_specs=[blk] * 4, out_specs=[blk] * 3,
        out_shape=[jax.ShapeDtypeStruct((rows, cols), F32)] * 3, name=name,
        compiler_params=_cparams(("parallel",)),
    )(w2, g2, m2, v2)
    return [t.reshape(shape) for t in out]


def _pair_sum(g2, recv, core, name, after=()):
    _, nchip, r, c = g2.shape
    tr = _pick(r, (512,))

    def body(core_ref, a_ref, b_ref, *rest):
        o_ref = rest[-1]
        o_ref[...] = (a_ref[...].astype(F32) + b_ref[...].astype(F32)).astype(o_ref.dtype)

    grid_spec = pltpu.PrefetchScalarGridSpec(
        num_scalar_prefetch=1, grid=(nchip, r // tr),
        in_specs=[pl.BlockSpec((None, None, tr, c), lambda k, i, cr: (cr[0], k, i, 0)),
                  pl.BlockSpec((None, tr, c), lambda k, i, cr: (k, i, 0))] + [_after_spec(t) for t in after],
        out_specs=pl.BlockSpec((None, tr, c), lambda k, i, cr: (k, i, 0)))
    return pl.pallas_call(
        body, grid_spec=grid_spec, out_shape=jax.ShapeDtypeStruct((nchip, r, c), BF16), name=name,
        compiler_params=_cparams(("parallel", "parallel")),
    )(core, g2, recv, *after)


def _chip_sum(a, recv, chip, name, after=()):
    _, r, c = a.shape
    tr = _pick(r, (512,))

    def body(chip_ref, a_ref, b_ref, *rest):
        o_ref = rest[-1]
        o_ref[...] = ((a_ref[...].astype(F32) + b_ref[0].astype(F32)) + b_ref[1].astype(F32)) + b_ref[2].astype(F32)

    grid_spec = pltpu.PrefetchScalarGridSpec(
        num_scalar_prefetch=1, grid=(r // tr,),
        in_specs=[pl.BlockSpec((None, tr, c), lambda i, cr: (cr[0], i, 0)),
                  pl.BlockSpec((3, tr, c), lambda i, cr: (0, i, 0))] + [_after_spec(t) for t in after],
        out_specs=pl.BlockSpec((tr, c), lambda i, cr: (i, 0)))
    return pl.pallas_call(
        body, grid_spec=grid_spec, out_shape=jax.ShapeDtypeStruct((r, c), F32), name=name,
        compiler_params=_cparams(("parallel",)),
    )(chip, a, recv, *after)


def _sum_devices(parts, name):
    n, r, c = parts.shape

    def body(p_ref, o_ref):
        acc = p_ref[0]
        for d in range(1, n):
            acc = acc + p_ref[d]
        o_ref[...] = acc

    return pl.pallas_call(
        body, out_shape=jax.ShapeDtypeStruct((r, c), F32), name=name,
        in_specs=[pl.BlockSpec(memory_space=pltpu.VMEM)], out_specs=pl.BlockSpec(memory_space=pltpu.VMEM),
    )(parts)


def _lane_iota():
    return lax.broadcasted_iota(jnp.int32, (1, LANES), 1)


def _tiles_up(tiles, s, lane):
    rolled = [pltpu.roll(t, s, 1) for t in tiles]
    zero = jnp.zeros_like(tiles[0])
    return [jnp.where(lane < s, p, c) for p, c in zip([zero] + rolled, rolled + [zero])]


def _tiles_down(tiles, s, lane):
    back = (LANES - s) % LANES
    rolled = [pltpu.roll(t, back, 1) for t in tiles]
    zero = jnp.zeros_like(tiles[0])
    return [jnp.where(lane < LANES - s, c, n) for c, n in zip(rolled, rolled[1:] + [zero])]


def _window_cols(prm, w_in, w_gate, w_up, l, name, after=None):
    tm = 256
    n_in, n_gu = WIN_IN // LANES, FFN_SHARD // LANES + 1

    def body(prm_ref, win_ref, g_ref, u_ref, *rest):
        out_ref, gu_ref, scr_in, scr_gu = rest[-4:]
        lane = _lane_iota()
        s_main, lo, hi, s_code, clo, chi, code_hi, s_gu = [prm_ref[i] for i in range(8)]
        scr_in[:, D_MODEL:WIN_IN] = jnp.zeros((tm, WIN_IN - D_MODEL), F32)
        scr_in[:, 0:W_IN_SHARD] = win_ref[...]

        def keep(t, a, b):
            col = lane + t * LANES
            return jnp.where((col >= a) & (col < b), scr_in[:, t * LANES:(t + 1) * LANES], 0.0)

        main = _tiles_up([keep(t, lo, hi) for t in range(n_in)], s_main, lane)
        for t in range(n_in):
            out_ref[:, t * LANES:(t + 1) * LANES] = main[t].astype(BF16)
        low = _tiles_up([keep(0, clo, chi)], s_code, lane)[0]
        high = _tiles_up([keep(n_in - 2, clo, chi), keep(n_in - 1, clo, chi)], s_code, lane)[1]
        out_ref[:, A_FZ:A_FZ + LANES] = jnp.where(code_hi == 1, high, low).astype(BF16)
        for ref, base in ((g_ref, 0), (u_ref, WIN_GU)):
            scr_gu[:, (n_gu - 1) * LANES:n_gu * LANES] = jnp.zeros((tm, LANES), F32)
            scr_gu[:, 0:FFN_SHARD] = ref[...]
            moved = _tiles_up([scr_gu[:, t * LANES:(t + 1) * LANES] for t in range(n_gu)], s_gu, lane)
            for t in range(n_gu + 1):
                gu_ref[:, base + t * LANES:base + (t + 1) * LANES] = moved[t].astype(BF16)

    after_args = [] if after is None else [after]
    grid_spec = pltpu.PrefetchScalarGridSpec(
        num_scalar_prefetch=1, grid=(D_MODEL // tm,),
        in_specs=[pl.BlockSpec((None, tm, W_IN_SHARD), lambda i, p: (l, i, 0)),
                  pl.BlockSpec((None, tm, FFN_SHARD), lambda i, p: (l, i, 0)),
                  pl.BlockSpec((None, tm, FFN_SHARD), lambda i, p: (l, i, 0))] + [_after_spec(t) for t in after_args],
        out_specs=[pl.BlockSpec((tm, IN_COLS), lambda i, p: (i, 0)), pl.BlockSpec((tm, GU_COLS), lambda i, p: (i, 0))],
        scratch_shapes=[pltpu.VMEM((tm, WIN_IN), F32), pltpu.VMEM((tm, n_gu * LANES), F32)])
    return pl.pallas_call(
        body, grid_spec=grid_spec, name=name, compiler_params=_cparams(("parallel",)),
        out_shape=[jax.ShapeDtypeStruct((D_MODEL, IN_COLS), BF16), jax.ShapeDtypeStruct((D_MODEL, GU_COLS), BF16)],
    )(prm, w_in, w_gate, w_up, *after_args)


def _gu_width(j):
    return min(WIN_GU, FFN - T_GU[j] * LANES)


def _after_spec(t):
    tile = (SUBLANES_BF16 if t.dtype == BF16 else 8, LANES)
    return pl.BlockSpec((None,) * (t.ndim - 2) + tile, lambda *_: (0,) * t.ndim)


def _assemble_in(a_all, tail_all, name, after=()):
    tm = D_MODEL // N_DEV

    def body(a_ref, t_ref, *rest):
        win_ref, tail_ref = rest[-2:]
        tail_ref[...] = t_ref[...]
        win_ref[...] = jnp.zeros_like(win_ref)
        code = a_ref[0, :, A_FZ:A_FZ + LANES]
        for j in range(N_DEV):
            c0 = T_IN[j] * LANES
            win_ref[:, c0:c0 + WIN_IN] += a_ref[j, :, 0:WIN_IN]
            if j > 0:
                code = code + a_ref[j, :, A_FZ:A_FZ + LANES]
        win_ref[:, FZ_COL:PROJ_W] = code

    return pl.pallas_call(
        body, grid=(N_DEV,),
        in_specs=[pl.BlockSpec((N_DEV, tm, IN_COLS), lambda i: (0, i, 0)),
                  pl.BlockSpec((None,) + tail_all.shape[1:], lambda i: (i, 0, 0))] + [_after_spec(t) for t in after],
        out_specs=[pl.BlockSpec((tm, PROJ_W), lambda i: (i, 0)),
                   pl.BlockSpec((None,) + tail_all.shape[1:], lambda i: (i, 0, 0))],
        out_shape=[jax.ShapeDtypeStruct((D_MODEL, PROJ_W), BF16), jax.ShapeDtypeStruct(tail_all.shape, BF16)],
        name=name, compiler_params=_cparams(("parallel",)),
    )(a_all, tail_all, *after)


def _assemble_rest(a_all, rows_all, name, after=()):
    tm = D_MODEL // N_DEV
    n_in = 2 + len(after)

    def body(*refs):
        a_ref, r_ref = refs[:2]
        wgu_ref, oa_ref, ob_ref, o_ref, down_ref = refs[n_in:]
        wgu_ref[...] = jnp.zeros_like(wgu_ref)
        for j in range(N_DEV):
            g0, width = T_GU[j] * LANES, _gu_width(j)
            wgu_ref[:, g0:g0 + width] += a_ref[j, :, 0:width]
            wgu_ref[:, FFN + g0:FFN + g0 + width] += a_ref[j, :, WIN_GU:WIN_GU + width]
        oa_ref[...] = r_ref[B_OA:B_OA + ROW_SHARD, :]
        ob_ref[...] = r_ref[B_OB:B_OB + ROW_SHARD, :]
        o_ref[...] = r_ref[B_O:B_O + ROW_SHARD, :]
        down_ref[...] = r_ref[B_DOWN:B_DOWN + FFN_SHARD, :]

    def rows(n):
        return pl.BlockSpec((n, D_MODEL), lambda i: (i, 0))

    square = jax.ShapeDtypeStruct((D_MODEL, D_MODEL), BF16)
    return pl.pallas_call(
        body, grid=(N_DEV,),
        in_specs=[pl.BlockSpec((N_DEV, tm, GU_COLS), lambda i: (0, i, 0)),
                  pl.BlockSpec((None, B_FG2, D_MODEL), lambda i: (i, 0, 0))] + [_after_spec(t) for t in after],
        out_specs=[pl.BlockSpec((tm, 2 * FFN), lambda i: (i, 0)),
                   rows(ROW_SHARD), rows(ROW_SHARD), rows(ROW_SHARD), rows(FFN_SHARD)],
        out_shape=[jax.ShapeDtypeStruct((D_MODEL, 2 * FFN), BF16),
                   square, square, square, jax.ShapeDtypeStruct((FFN, D_MODEL), BF16)],
        name=name, compiler_params=_cparams(("parallel",)),
    )(a_all, rows_all, *after)


def _grad_windows_in(d_in, name):
    tm = 256

    def body(din_ref, out_ref):
        for j in range(N_DEV):
            c0 = T_IN[j] * LANES
            out_ref[j & 1, j >> 1, :, 0:WIN_IN] = din_ref[:, c0:c0 + WIN_IN]
            out_ref[j & 1, j >> 1, :, A_FZ:IN_COLS] = din_ref[:, FZ_COL:PROJ_W]

    return pl.pallas_call(
        body, grid=(D_MODEL // tm,), in_specs=[pl.BlockSpec((tm, PROJ_W), lambda i: (i, 0))],
        out_specs=pl.BlockSpec((2, 4, tm, IN_COLS), lambda i: (0, 0, i, 0)),
        out_shape=jax.ShapeDtypeStruct((2, 4, D_MODEL, IN_COLS), BF16), name=name,
        compiler_params=_cparams(("parallel",)),
    )(d_in)


def _grad_windows_gu(d_gu, name):
    tm = 256

    def body(dgu_ref, out_ref):
        for j in range(N_DEV):
            g0, width = T_GU[j] * LANES, _gu_width(j)
            for half, base in ((0, 0), (FFN, WIN_GU)):
                out_ref[j & 1, j >> 1, :, base:base + width] = dgu_ref[:, half + g0:half + g0 + width]
                if width < WIN_GU:
                    out_ref[j & 1, j >> 1, :, base + width:base + WIN_GU] = jnp.zeros((tm, WIN_GU - width), BF16)

    return pl.pallas_call(
        body, grid=(D_MODEL // tm,), in_specs=[pl.BlockSpec((tm, 2 * FFN), lambda i: (i, 0))],
        out_specs=pl.BlockSpec((2, 4, tm, GU_COLS), lambda i: (0, 0, i, 0)),
        out_shape=jax.ShapeDtypeStruct((2, 4, D_MODEL, GU_COLS), BF16), name=name,
        compiler_params=_cparams(("parallel",)),
    )(d_gu)


def _final_windows(prm, chip, a, recv, params, l, kind, name, prev=None):
    tm = 128
    n_in, n_gu = WIN_IN // LANES, FFN_SHARD // LANES + 1
    widths = (W_IN_SHARD,) if kind == "in" else (FFN_SHARD, FFN_SHARD)
    cols = IN_COLS if kind == "in" else GU_COLS
    n_par = 3 * len(widths)
    prev = [] if prev is None else [t for group in prev for t in group]

    def body(prm_ref, chip_ref, a_ref, r_ref, *refs):
        ins, outs = refs[:n_par], refs[n_par + len(prev):]
        lane = _lane_iota()
        s_main, s_code, clo, chi, s_gu = [prm_ref[i] for i in (0, 3, 4, 5, 7)]

        def total(c0):
            sl = slice(c0, c0 + LANES)
            return ((a_ref[:, sl].astype(F32) + r_ref[0, :, sl].astype(F32)) + r_ref[1, :, sl].astype(F32)) \
                + r_ref[2, :, sl].astype(F32)

        if kind == "in":
            grads = _tiles_down([total(t * LANES) for t in range(n_in)], s_main, lane)
            code = pltpu.roll(total(A_FZ), (LANES - s_code) % LANES, 1)
            for t in (0, n_in - 2, n_in - 1):
                col = lane + t * LANES
                grads[t] = jnp.where((col >= clo) & (col < chi), code, grads[t])
            per_weight = [grads]
        else:
            per_weight = [_tiles_down([total(base + t * LANES) for t in range(n_gu + 1)], s_gu, lane)[:n_gu]
                          for base in (0, WIN_GU)]
        for k, (tiles, width) in enumerate(zip(per_weight, widths)):
            w_ref, m_ref, v_ref = ins[3 * k:3 * k + 3]
            g_ref, d_ref, nm_ref, nv_ref = outs[4 * k:4 * k + 4]
            for t, g in enumerate(tiles):
                n = min(LANES, width - t * LANES)
                sl = slice(t * LANES, t * LANES + n)
                g = g[:, 0:n]
                d, nm, nv = _adamw_math(w_ref[:, sl], g, m_ref[:, sl], v_ref[:, sl])
                g_ref[:, sl] = g
                d_ref[:, sl] = d
                nm_ref[:, sl] = nm
                nv_ref[:, sl] = nv

    def native(width):
        return pl.BlockSpec((None, tm, width), lambda i, p, c: (l, i, 0))

    in_specs = [pl.BlockSpec((None, tm, cols), lambda i, p, c: (c[0], i, 0)),
                pl.BlockSpec((3, tm, cols), lambda i, p, c: (0, i, 0))]
    in_specs += [native(wd) for wd in widths for _ in range(3)] + [_ANY] * len(prev)
    grid_spec = pltpu.PrefetchScalarGridSpec(
        num_scalar_prefetch=2, grid=(D_MODEL // tm,), in_specs=in_specs,
        out_specs=[native(wd) for wd in widths for _ in range(4)])
    out = pl.pallas_call(
        body, grid_spec=grid_spec, name=name, compiler_params=_cparams(("parallel",)),
        out_shape=[jax.ShapeDtypeStruct((DEPTH, D_MODEL, wd), F32) for wd in widths for _ in range(4)],
        input_output_aliases={4 + n_par + k: k for k in range(len(prev))},
    )(prm, chip, a, recv, *params, *prev)
    return [out[4 * k:4 * k + 4] for k in range(len(widths))]


def _me():
    return lax.axis_index("x"), lax.axis_index("y"), lax.axis_index("c")


_CHIP_FLIPS = ((1, 0), (0, 1), (1, 1))
_ANY = pl.BlockSpec(memory_space=pl.ANY)


def _comm_call(body, peers, out_shape, sems, name, args, collective_id):
    if collective_id is None:
        n_in = len(args)
        return pl.pallas_call(body, out_shape=out_shape, name=name, in_specs=[_ANY] * n_in,
                              out_specs=[_ANY] * len(out_shape), scratch_shapes=sems)(*args)

    def sequencer_body(*refs):
        barrier = pltpu.get_barrier_semaphore()
        targets = peers()
        for peer in targets:
            pl.semaphore_signal(barrier, inc=1, device_id=peer, device_id_type=MESH)
        pl.semaphore_wait(barrier, len(targets))
        body(*refs)

    sequencer = plsc.ScalarSubcoreMesh(axis_name="seq", num_cores=1)
    return pl.kernel(sequencer_body, out_type=out_shape, mesh=sequencer, scratch_types=sems, name=name,
                     compiler_params=pltpu.CompilerParams(collective_id=collective_id))(*args)


def _sibling_peer():
    x, y, cc = _me()
    return [(x, y, 1 - cc)]


def _chip_peers():
    x, y, cc = _me()
    return [(x ^ fx, y ^ fy, cc) for fx, fy in _CHIP_FLIPS]


def _all_gather(shards, name, collective_id=None):
    n = len(shards)

    def body(*refs):
        x_refs, out_refs = refs[:n], refs[n:2 * n]
        send_sems, recv_sems, local_sems = refs[2 * n:]
        x, y, cc = _me()
        sibling = (x, y, 1 - cc)
        chips = [(x ^ fx, y ^ fy) for fx, fy in _CHIP_FLIPS]

        def copy(a, k, block, to, from_shard=False):
            px, py, pc = block
            slot = out_refs[a].at[4 * px + 2 * py + pc]
            return pltpu.make_async_remote_copy(
                src_ref=x_refs[a] if from_shard else slot, dst_ref=slot,
                send_sem=send_sems.at[a, k], recv_sem=recv_sems.at[a, k], device_id=to, device_id_type=MESH)

        mine = [pltpu.make_async_copy(x_refs[a], out_refs[a].at[4 * x + 2 * y + cc], local_sems.at[a])
                for a in range(n)]
        for cp in mine:
            cp.start()
        first = [copy(a, 0, (x, y, cc), sibling, True) for a in range(n)]
        first += [copy(a, 1 + j, (x, y, cc), (*chip, cc), True) for j, chip in enumerate(chips) for a in range(n)]
        for cp in first:
            cp.start()
        passed = []
        for j, chip in enumerate(chips):
            for a in range(n):
                copy(a, 1 + j, (*chip, cc), (x, y, cc)).wait_recv()
                passed.append(copy(a, 4 + j, (*chip, cc), sibling))
                passed[-1].start()
        for a in range(n):
            copy(a, 0, sibling, (x, y, cc)).wait_recv()
            for j, chip in enumerate(chips):
                copy(a, 4 + j, (*chip, 1 - cc), (x, y, cc)).wait_recv()
        for cp in first + passed:
            cp.wait_send()
        for cp in mine:
            cp.wait()

    return _comm_call(
        body, lambda: _sibling_peer() + _chip_peers(),
        [jax.ShapeDtypeStruct((N_DEV,) + s.shape, s.dtype) for s in shards],
        [pltpu.SemaphoreType.DMA((n, 7)), pltpu.SemaphoreType.DMA((n, 7)), pltpu.SemaphoreType.DMA((n,))],
        name, shards, collective_id)


def _send_to_sibling(parts, name, collective_id=None):
    n = len(parts)

    def body(*refs):
        g_refs, out_refs = refs[:n], refs[n:2 * n]
        send_sems, recv_sems = refs[2 * n:]
        x, y, cc = _me()
        copies = [pltpu.make_async_remote_copy(
            src_ref=g_refs[a].at[1 - cc], dst_ref=out_refs[a], send_sem=send_sems.at[a], recv_sem=recv_sems.at[a],
            device_id=(x, y, 1 - cc), device_id_type=MESH) for a in range(n)]
        for cp in copies:
            cp.start()
        for cp in copies:
            cp.wait()

    return _comm_call(
        body, _sibling_peer, [jax.ShapeDtypeStruct(p.shape[1:], p.dtype) for p in parts],
        [pltpu.SemaphoreType.DMA((n,)), pltpu.SemaphoreType.DMA((n,))], name, parts, collective_id)


def _send_to_chips(parts, name, collective_id=None):
    n = len(parts)

    def body(*refs):
        a_refs, out_refs = refs[:n], refs[n:2 * n]
        send_sems, recv_sems = refs[2 * n:]
        x, y, cc = _me()
        copies = []
        for k, (fx, fy) in enumerate(_CHIP_FLIPS):
            px, py = x ^ fx, y ^ fy
            for a in range(n):
                copies.append(pltpu.make_async_remote_copy(
                    src_ref=a_refs[a].at[2 * px + py], dst_ref=out_refs[a].at[k], send_sem=send_sems.at[a, k],
                    recv_sem=recv_sems.at[a, k], device_id=(px, py, cc), device_id_type=MESH))
                copies[-1].start()
        for cp in copies:
            cp.wait()

    return _comm_call(
        body, _chip_peers, [jax.ShapeDtypeStruct((3,) + p.shape[1:], p.dtype) for p in parts],
        [pltpu.SemaphoreType.DMA((n, 3)), pltpu.SemaphoreType.DMA((n, 3))], name, parts, collective_id)


def _pack_rows(w_oa, w_ob, w_o, w_down, w_fg2, conv_w, l):
    conv_bits = lax.bitcast_convert_type(conv_w[l].reshape(-1), BF16).reshape(1, -1)
    tail = jnp.concatenate([w_fg2[l].astype(BF16).reshape(1, D_MODEL),
                            jnp.pad(conv_bits, ((0, 0), (0, D_MODEL - conv_bits.shape[1])))], axis=0)
    tail = jnp.pad(tail, ((0, B_ROWS - B_FG2 - tail.shape[0]), (0, 0)))
    rows = jnp.concatenate([w_oa[l].astype(BF16), w_ob[l].astype(BF16), w_o[l].astype(BF16),
                            w_down[l].astype(BF16)], axis=0)
    return rows, tail


def _unpack_tail(tail):
    w_fg2 = tail[:, 0, :].reshape(N_DEV, RANK, KEY_W // N_DEV).transpose(1, 0, 2).reshape(RANK, KEY_W)
    conv_bits = tail[:, B_CONV - B_FG2, :2 * 3 * ROW_SHARD].reshape(N_DEV, 3 * ROW_SHARD, 2)
    conv_w = lax.bitcast_convert_type(conv_bits, F32).reshape(N_DEV, 3, ROW_SHARD)
    return jnp.pad(w_fg2, ((0, LANES - RANK), (0, 0))), conv_w.transpose(1, 0, 2).reshape(3, D_MODEL)


def _by_core_chip(t):
    return t.reshape((2, 2, 2) + t.shape[1:]).transpose((2, 0, 1) + tuple(range(3, t.ndim + 2))).reshape(
        (2, 4) + t.shape[1:])


def _grad_rows(g):
    fg2 = g["w_fg2"][:RANK].reshape(RANK, N_DEV, KEY_W // N_DEV).transpose(1, 0, 2).reshape(N_DEV, 1, D_MODEL)
    conv = g["conv_w"].astype(BF16).reshape(3, N_DEV, ROW_SHARD).transpose(1, 0, 2).reshape(N_DEV, 1, 3 * ROW_SHARD)
    tail = jnp.concatenate([fg2, jnp.pad(conv, ((0, 0), (0, 0), (0, D_MODEL - 3 * ROW_SHARD)))], axis=1)
    tail = jnp.pad(tail, ((0, 0), (0, B_ROWS - B_FG2 - 2), (0, 0)))
    parts = [g["w_oa"].reshape(N_DEV, ROW_SHARD, D_MODEL), g["w_ob"].reshape(N_DEV, ROW_SHARD, D_MODEL),
             g["w_o"].reshape(N_DEV, ROW_SHARD, D_MODEL), g["w_down"].reshape(N_DEV, FFN_SHARD, D_MODEL), tail]
    return _by_core_chip(jnp.concatenate(parts, axis=1))


def _ungrad_rows(gs):
    return dict(w_oa=gs[B_OA:B_OA + ROW_SHARD], w_ob=gs[B_OB:B_OB + ROW_SHARD], w_o=gs[B_O:B_O + ROW_SHARD],
                w_ffn_down=gs[B_DOWN:B_DOWN + FFN_SHARD], w_fg2=gs[B_FG2].reshape(RANK, KEY_W // N_DEV),
                conv_w=gs[B_CONV, :3 * ROW_SHARD].reshape(3, ROW_SHARD))


def _layer_fwd(x, p, l):
    tag = f"l{l}_"
    h = _rmsnorm_fwd(x, p["norm1_g"], tag + "norm1")
    proj = _matmul(h, p["w_in"], "nn", BF16, tag + "proj")
    o, states = _gla_fwd(proj, p["w_fg2"], p["b_fg"], tag + "gla_fwd")
    p.update(p.pop("rest")((o,)))
    oa = _gla_post_fwd(o, proj, p["gla_norm_g"], tag + "gla_post")
    ya = _matmul(oa, p["w_oa"], "nn", BF16, tag + "ya")
    cb = _conv_fwd(proj, p["conv_w"], p["conv_b"], tag + "conv")
    yb = _matmul(cb, p["w_ob"], "nn", BF16, tag + "yb")
    mix = _mix_fwd(proj, ya, yb, tag + "mix")
    x1 = _matmul(mix, p["w_o"], "nn", F32, tag + "x1", residual=x)
    h2 = _rmsnorm_fwd(x1, p["norm2_g"], tag + "norm2")
    gu = _matmul(h2, p["w_gu"], "nn", BF16, tag + "gu")
    hid = _swiglu_fwd(gu, tag + "swiglu")
    x2 = _matmul(hid, p["w_down"], "nn", F32, tag + "x2", residual=x1)
    saved = dict(x=x, h=h, proj=proj, o=o, states=states, oa=oa, ya=ya, cb=cb, yb=yb, mix=mix, x1=x1, h2=h2,
                 gu=gu, hid=hid)
    return x2, saved


def _layer_bwd(dx2, dx2h, p, sv, l, reduce=None):
    if reduce is None:
        reduce = lambda group, grads: ((), ())
    tag = f"l{l}_b_"
    dhid = _matmul(dx2h, p["w_down"], "nt", BF16, tag + "dhid")
    d_down = _matmul(sv["hid"], dx2h, "tn", BF16, tag + "dw_down")
    dgu = _swiglu_bwd(dhid, sv["gu"], tag + "swiglu")
    dh2 = _matmul(dgu, p["w_gu"], "nt", F32, tag + "dh2")
    d_gu = _matmul(sv["h2"], dgu, "tn", BF16, tag + "dw_gu")
    gu_packed, gu_sums = reduce("gu", d_gu)
    dx1, dx1h, dg2 = _rmsnorm_bwd(sv["x1"], p["norm2_g"], dh2, dx2, tag + "norm2")
    dmix = _matmul(dx1h, p["w_o"], "nt", BF16, tag + "dmix", after=gu_packed)
    d_o = _matmul(sv["mix"], dx1h, "tn", BF16, tag + "dw_o")
    dproj, dya, dyb = _mix_bwd(dmix, sv["proj"], sv["ya"], sv["yb"], tag + "mix")
    dcb = _matmul(dyb, p["w_ob"], "nt", BF16, tag + "dcb", after=gu_sums)
    d_ob = _matmul(sv["cb"], dyb, "tn", BF16, tag + "dw_ob")
    dproj, dwb = _conv_bwd(dcb, sv["proj"], p["conv_w"], p["conv_b"], dproj, tag + "conv")
    doa = _matmul(dya, p["w_oa"], "nt", BF16, tag + "doa")
    d_oa = _matmul(sv["oa"], dya, "tn", BF16, tag + "dw_oa")
    dproj, do, dgg = _gla_post_bwd(doa, sv["o"], sv["proj"], p["gla_norm_g"], dproj, tag + "gla_post")
    dq, dk, dv, dfg, dbfg = _gla_bwd(sv["proj"], p["w_fg2"], p["b_fg"], sv["states"], do, tag + "gla")
    fz = sv["proj"][:, FZ_COL:]
    dproj = _place_qkv(dq, dk, dv, dproj, tag + "place_qkv")
    dproj = _matmul(dfg, p["w_fg2"], "nt", BF16, tag + "dfz", into=(dproj, FZ_COL))
    d_fg2 = _matmul(fz, dfg, "tn", BF16, tag + "dw_fg2")
    rows = dict(w_fg2=d_fg2, conv_w=dwb[0:3], w_oa=d_oa, w_ob=d_ob, w_o=d_o, w_down=d_down)
    rows_packed, rows_sums = reduce("rows", rows)
    d_in = _matmul(sv["h"], dproj, "tn", BF16, tag + "dw_in", after=rows_packed)
    _, in_sums = reduce("in", d_in)
    dh = _matmul(dproj, p["w_in"], "nt", F32, tag + "dh", after=(d_in,) + tuple(rows_sums))
    dx, dxh, dg1 = _rmsnorm_bwd(sv["x"], p["norm1_g"], dh, dx1, tag + "norm1", after=in_sums)
    big = dict(w_in=d_in, w_gu=d_gu, **rows)
    pad = lambda t: jnp.pad(t, ((0, 0), (0, D_MODEL - t.shape[1])))
    small = [dg1[0:1], pad(dbfg[0:1]), pad(dgg[0:1]), dwb[3:4], dg2[0:1]]
    return dx, dxh, big, small


def _local_step(x, target, weights_of, final_g, reduce_of=None):
    saved, layers = [], []
    for l in range(DEPTH):
        layers.append(weights_of(l, x))
        x, sv = _layer_fwd(x, layers[l], l)
        saved.append(sv)
    loss, dx, dxh, dgf = _loss_head(x, final_g, target, "loss_head")
    bigs, smalls = [None] * DEPTH, [None] * DEPTH
    for l in reversed(range(DEPTH)):
        dx, dxh, bigs[l], smalls[l] = _layer_bwd(dx, dxh, layers[l], saved[l], l, reduce_of(l) if reduce_of else None)
    small = jnp.concatenate(smalls[0] + smalls[1] + [dgf[0:1]], axis=0)
    small = jnp.pad(small, ((0, SMALL_ROWS - small.shape[0]), (0, 0)))
    return loss[0, 0], dx, bigs, small


def kernel(x, norm1_g, w_in, w_fg2, b_fg, gla_norm_g, w_oa, conv_w, conv_b, w_ob, w_o, norm2_g, w_ffn_gate, w_ffn_up, w_ffn_down, final_g, loss_target, m_norm1_g, m_w_in, m_w_fg2, m_b_fg, m_gla_norm_g, m_w_oa, m_conv_w, m_conv_b, m_w_ob, m_w_o, m_norm2_g, m_w_ffn_gate, m_w_ffn_up, m_w_ffn_down, m_final_g, v_norm1_g, v_w_in, v_w_fg2, v_b_fg, v_gla_norm_g, v_w_oa, v_conv_w, v_conv_b, v_w_ob, v_w_o, v_norm2_g, v_w_ffn_gate, v_w_ffn_up, v_w_ffn_down, v_final_g):
    names = ["norm1_g", "w_in", "w_fg2", "b_fg", "gla_norm_g", "w_oa", "conv_w", "conv_b", "w_ob", "w_o",
             "norm2_g", "w_ffn_gate", "w_ffn_up", "w_ffn_down", "final_g"]
    w = dict(zip(names, [norm1_g, w_in, w_fg2, b_fg, gla_norm_g, w_oa, conv_w, conv_b, w_ob, w_o, norm2_g,
                         w_ffn_gate, w_ffn_up, w_ffn_down, final_g]))
    m = dict(zip(names, [m_norm1_g, m_w_in, m_w_fg2, m_b_fg, m_gla_norm_g, m_w_oa, m_conv_w, m_conv_b, m_w_ob,
                         m_w_o, m_norm2_g, m_w_ffn_gate, m_w_ffn_up, m_w_ffn_down, m_final_g]))
    v = dict(zip(names, [v_norm1_g, v_w_in, v_w_fg2, v_b_fg, v_gla_norm_g, v_w_oa, v_conv_w, v_conv_b, v_w_ob,
                         v_w_o, v_norm2_g, v_w_ffn_gate, v_w_ffn_up, v_w_ffn_down, v_final_g]))
    col_names = ["w_in", "w_ffn_gate", "w_ffn_up"]
    cx, cy, cc = _me()
    prm = jnp.asarray(SHIFT_TABLE)[4 * cx + 2 * cy + cc]
    core = jnp.reshape(cc, (1,)).astype(jnp.int32)
    chip = jnp.reshape(2 * cx + cy, (1,)).astype(jnp.int32)

    ids = iter(range(32))

    gathered, previous = [], None
    for l in range(DEPTH):
        rows, tail = _pack_rows(w_oa, w_ob, w_o, w_ffn_down, w_fg2, conv_w, l)
        win_in, win_gu = _window_cols(prm, w_in, w_ffn_gate, w_ffn_up, l, f"l{l}_windows", after=previous)
        previous = rows
        if l == 0:
            first = _all_gather([win_in, tail], "l0_gather_in", next(ids))
            gathered.append(list(first) + list(_all_gather([win_gu, rows], "l0_gather_rest", next(ids))))
        else:
            all_in, all_tail, all_gu, all_rows = _all_gather([win_in, tail, win_gu, rows], f"l{l}_gather", next(ids))
            gathered.append([all_in, all_tail, all_gu, all_rows])

    def weights_of(l, x_in):
        all_in, all_tail, all_gu, all_rows = gathered[l]
        after = (x_in,) if l > 0 else ()
        w_in_full, tail = _assemble_in(all_in, all_tail, f"l{l}_assemble_in", after)
        w_fg2_full, conv_w_full = _unpack_tail(tail)

        def rest(after_rest):
            names_rest = ("w_gu", "w_oa", "w_ob", "w_o", "w_down")
            return dict(zip(names_rest, _assemble_rest(all_gu, all_rows, f"l{l}_assemble_rest", after + after_rest)))

        return dict(w_in=w_in_full, rest=rest, w_fg2=w_fg2_full,
                    conv_w=conv_w_full, norm1_g=norm1_g[l][None], b_fg=b_fg[l][None],
                    gla_norm_g=gla_norm_g[l][None], conv_b=conv_b[l][None], norm2_g=norm2_g[l][None])

    pending = [dict() for _ in range(DEPTH)]
    landed = []

    def reduce_of(l):
        def reduce(group, grads):
            tag = f"l{l}_{group}"
            if group == "gu":
                packed = _grad_windows_gu(grads, tag + "_windows")
            elif group == "in":
                packed = _grad_windows_in(grads, tag + "_windows")
            else:
                packed = _grad_rows(grads)
            (from_sibling,) = _send_to_sibling([packed], tag + "_to_sibling", next(ids))
            waited = () if group == "in" else tuple(landed)
            if waited:
                landed.clear()
            sums = _pair_sum(packed, from_sibling, core, tag + "_pair_sum", after=waited)
            (from_chips,) = _send_to_chips([sums], tag + "_to_chips", next(ids))
            landed.append(from_chips)
            pending[l][group] = (sums, from_chips)
            return (packed,), (sums,)
        return reduce

    loss, dx, bigs, small = _local_step(x[0], loss_target[0], weights_of, final_g[None], reduce_of)

    grads, deltas, new_m, new_v = {}, {}, {}, {}
    for kind, group_names in (("gu", col_names[1:]), ("in", col_names[:1])):
        params = [t for n in group_names for t in (w[n], m[n], v[n])]
        out = None
        for l in reversed(range(DEPTH)):
            sums, from_chips = pending[l][kind]
            out = _final_windows(prm, chip, sums, from_chips, params, l, kind, f"l{l}_{kind}_final", out)
        for n, (g, d, nm, nv) in zip(group_names, out):
            grads[n], deltas[n], new_m[n], new_v[n] = g, d, nm, nv
    row_grads = [_ungrad_rows(_chip_sum(*pending[l]["rows"], chip, f"l{l}_rows_chip_sum")) for l in range(DEPTH)]
    for n in row_grads[0]:
        grads[n] = jnp.stack([row_grads[l][n] for l in range(DEPTH)])

    small_sum = _sum_devices(_all_gather([small], "gather_small")[0], "sum_small")
    r512, r256 = slice(0, KEY_W), slice(0, HEAD_V)
    grads.update(
        norm1_g=jnp.stack([small_sum[0], small_sum[5]]), b_fg=jnp.stack([small_sum[1, r512], small_sum[6, r512]]),
        gla_norm_g=jnp.stack([small_sum[2, r256], small_sum[7, r256]]),
        conv_b=jnp.stack([small_sum[3], small_sum[8]]), norm2_g=jnp.stack([small_sum[4], small_sum[9]]),
        final_g=small_sum[10])

    for n in names:
        if n not in col_names:
            deltas[n], new_m[n], new_v[n] = _adamw(w[n], grads[n], m[n], v[n], "adamw_" + n)

    total_loss = lax.psum(loss, ("x", "y", "c"))
    return (total_loss, dx[None], *[grads[n] for n in names], *[deltas[n] for n in names],
            *[new_m[n] for n in names], *[new_v[n] for n in names])
```

```python
import functools

import jax
import jax.numpy as jnp
import numpy as np
from jax import lax
from jax.experimental import pallas as pl
from jax.experimental.pallas import tpu as pltpu
from jax.experimental.pallas import tpu_sc as plsc

F32 = jnp.float32
BF16 = jnp.bfloat16
MESH = pl.DeviceIdType.MESH

D_MODEL = 1024
DEPTH = 2
CHUNK = 64
HEADS = 4
HEAD_K = 128
HEAD_V = 256
KEY_W = HEADS * HEAD_K
VAL_W = HEADS * HEAD_V
RANK = 16
TAU = 16.0
FFN = 2816
IN_WIDTH = 2 * KEY_W + 2 * VAL_W + RANK + 5 * D_MODEL
EPS = 1e-6
Q_SCALE = HEAD_K ** -0.5
N_DEV = 8
ADAM_LR, ADAM_B1, ADAM_B2, ADAM_EPS, ADAM_WD, ADAM_STEP = 0.001, 0.9, 0.999, 1e-08, 0.01, 10

LANES = 128
SUBLANES_BF16 = 16
VMEM_LIMIT = 48 * 1024 * 1024

FZ_COL = 2 * KEY_W + 2 * VAL_W + 5 * D_MODEL
PROJ_W = FZ_COL + LANES
SEG_R, SEG_GBI, SEG_GCI, SEG_CX, SEG_GA, SEG_GB = 2, 3, 4, 5, 6, 7

W_IN_SHARD = IN_WIDTH // N_DEV
FFN_SHARD = FFN // N_DEV
ROW_SHARD = D_MODEL // N_DEV

WIN_IN = 9 * LANES
WIN_GU = 4 * LANES
A_FZ = WIN_IN
IN_COLS = A_FZ + LANES
GU_COLS = 2 * WIN_GU
ORIG_FZ = 2 * KEY_W + 2 * VAL_W


def _new_col(o):
    if o < ORIG_FZ:
        return o
    if o < ORIG_FZ + RANK:
        return FZ_COL + (o - ORIG_FZ)
    return o - RANK


def _shift_table():
    t_in, rows = [], []
    for j in range(N_DEV):
        new = [_new_col(W_IN_SHARD * j + i) for i in range(W_IN_SHARD)]
        main = [i for i in range(W_IN_SHARD) if new[i] < FZ_COL]
        code = [i for i in range(W_IN_SHARD) if new[i] >= FZ_COL]
        shift = new[main[0]] - main[0]
        t_in.append(shift // LANES)
        assert all(new[i] - i == shift for i in main) and shift % LANES + W_IN_SHARD <= WIN_IN
        if code:
            cshift = new[code[0]] - FZ_COL - code[0]
            crow = [cshift % LANES, code[0], code[-1] + 1, int(cshift < 0)]
        else:
            crow = [0, 0, 0, 0]
        rows.append([shift % LANES, main[0], main[-1] + 1] + crow + [FFN_SHARD * j % LANES])
    return tuple(t_in), np.asarray(rows, np.int32)


T_IN, SHIFT_TABLE = _shift_table()
T_GU = tuple(FFN_SHARD * j // LANES for j in range(N_DEV))

B_OA, B_OB, B_O, B_DOWN = 0, ROW_SHARD, 2 * ROW_SHARD, 3 * ROW_SHARD
B_FG2 = B_DOWN + FFN_SHARD
B_CONV = B_FG2 + 1
B_ROWS = B_FG2 + SUBLANES_BF16
SMALL_ROWS = 32


def _pick(n, candidates):
    for c in candidates:
        if n % c == 0:
            return c
    return n


def _cparams(sem):
    return pltpu.CompilerParams(dimension_semantics=sem, vmem_limit_bytes=VMEM_LIMIT)


def _sigmoid(x):
    return 1.0 / (1.0 + jnp.exp(-x))


def _matmul(a, b, dims, out_dtype, name, residual=None, after=(), into=None):
    if dims == "nn":
        (m, k), (k2, n) = a.shape, b.shape
    elif dims == "nt":
        (m, k), (n, k2) = a.shape, b.shape
    else:
        (k, m), (k2, n) = a.shape, b.shape
    assert k == k2, (a.shape, b.shape, dims)
    tm = _pick(m, (1024, 1408, 512, 256, 128))
    tn = _pick(n, (1664, 1408, 1024, 512, 256, 128))
    tk = _pick(k, (1664, 1408, 1024, 512, 256, 128))
    nk = k // tk
    if dims == "nn":
        a_spec = pl.BlockSpec((tm, tk), lambda i, j, kk: (i, kk))
        b_spec = pl.BlockSpec((tk, tn), lambda i, j, kk: (kk, j))
        contract = (((1,), (0,)), ((), ()))
    elif dims == "nt":
        a_spec = pl.BlockSpec((tm, tk), lambda i, j, kk: (i, kk))
        b_spec = pl.BlockSpec((tn, tk), lambda i, j, kk: (j, kk))
        contract = (((1,), (1,)), ((), ()))
    else:
        a_spec = pl.BlockSpec((tk, tm), lambda i, j, kk: (kk, i))
        b_spec = pl.BlockSpec((tk, tn), lambda i, j, kk: (kk, j))
        contract = (((0,), (0,)), ((), ()))
    o_spec = pl.BlockSpec((tm, tn), lambda i, j, kk: (i, j))
    has_res = residual is not None
    out_spec, out_struct, placed, aliases = o_spec, jax.ShapeDtypeStruct((m, n), out_dtype), (), {}
    if into is not None:
        buffer, col = into
        assert col % tn == 0 and buffer.dtype == out_dtype and not has_res
        out_spec = pl.BlockSpec((tm, tn), lambda i, j, kk: (i, col // tn + j))
        out_struct, placed, aliases = jax.ShapeDtypeStruct(buffer.shape, out_dtype), (buffer,), {2 + len(after): 0}

    def body(*refs):
        a_ref, b_ref = refs[:2]
        r_ref = refs[2] if has_res else None
        o_ref = refs[2 + has_res + len(after) + len(placed)]
        kk = pl.program_id(2)
        part = lax.dot_general(a_ref[...], b_ref[...], contract, preferred_element_type=F32)

        def finish(total):
            if has_res:
                total = total + r_ref[...]
            o_ref[...] = total.astype(o_ref.dtype)

        if nk == 1:
            finish(part)
            return
        acc_ref = refs[-1]

        @pl.when(kk == 0)
        def _():
            acc_ref[...] = part

        @pl.when((kk > 0) & (kk < nk - 1))
        def _():
            acc_ref[...] += part

        @pl.when(kk == nk - 1)
        def _():
            finish(acc_ref[...] + part)

    in_specs = [a_spec, b_spec] + ([o_spec] if has_res else []) + [_after_spec(t) for t in after]
    in_specs += [_ANY] * len(placed)
    args = (a, b) + ((residual,) if has_res else ()) + tuple(after) + placed
    return pl.pallas_call(
        body, grid=(m // tm, n // tn, nk), in_specs=in_specs, out_specs=out_spec,
        out_shape=out_struct, input_output_aliases=aliases,
        scratch_shapes=[pltpu.VMEM((tm, tn), F32)] if nk > 1 else [], name=name,
        compiler_params=_cparams(("parallel", "parallel", "arbitrary")),
    )(*args)


def _rmsnorm_fwd(x, g, name):
    s, d = x.shape
    tm = _pick(s, (512, 256))

    def body(x_ref, g_ref, o_ref):
        xv = x_ref[...]
        r = lax.rsqrt(jnp.mean(xv * xv, axis=-1, keepdims=True) + EPS)
        o_ref[...] = (xv * r * g_ref[...]).astype(o_ref.dtype)

    row = pl.BlockSpec((tm, d), lambda i: (i, 0))
    return pl.pallas_call(
        body, grid=(s // tm,), in_specs=[row, pl.BlockSpec((1, d), lambda i: (0, 0))], out_specs=row,
        out_shape=jax.ShapeDtypeStruct((s, d), BF16), name=name, compiler_params=_cparams(("parallel",)),
    )(x, g)


def _rmsnorm_bwd(x, g, dh, dres, name, after=()):
    s, d = x.shape
    tm = _pick(s, (512, 256))

    def body(x_ref, g_ref, dh_ref, dres_ref, *rest):
        dx_ref, dx16_ref, dg_ref = rest[-3:]
        xv = x_ref[...]
        r = lax.rsqrt(jnp.mean(xv * xv, axis=-1, keepdims=True) + EPS)
        xn = xv * r
        dhv = dh_ref[...].astype(F32)
        dxn = dhv * g_ref[...]
        dx = dres_ref[...] + r * (dxn - xn * jnp.mean(dxn * xn, axis=-1, keepdims=True))
        dx_ref[...] = dx
        dx16_ref[...] = dx.astype(BF16)

        @pl.when(pl.program_id(0) == 0)
        def _():
            dg_ref[...] = jnp.zeros_like(dg_ref)

        dg_ref[...] += jnp.broadcast_to(jnp.sum(dhv * xn, axis=0, keepdims=True), dg_ref.shape)

    row = pl.BlockSpec((tm, d), lambda i: (i, 0))
    acc = pl.BlockSpec((8, d), lambda i: (0, 0))
    return pl.pallas_call(
        body, grid=(s // tm,),
        in_specs=[row, pl.BlockSpec((1, d), lambda i: (0, 0)), row, row] + [_after_spec(t) for t in after],
        out_specs=[row, row, acc],
        out_shape=[jax.ShapeDtypeStruct((s, d), F32), jax.ShapeDtypeStruct((s, d), BF16),
                   jax.ShapeDtypeStruct((8, d), F32)],
        name=name, compiler_params=_cparams(("arbitrary",)),
    )(x, g, dh, dres, *after)


def _loss_head(x, g, target, name):
    s, d = x.shape
    tm = _pick(s, (512, 256))

    def body(x_ref, g_ref, t_ref, loss_ref, dx_ref, dx16_ref, dg_ref):
        xv = x_ref[...]
        gv = g_ref[...]
        r = lax.rsqrt(jnp.mean(xv * xv, axis=-1, keepdims=True) + EPS)
        xn = xv * r
        err = xn * gv - t_ref[...]
        dy = err * (1.0 / d)
        dxn = dy * gv
        dx = r * (dxn - xn * jnp.mean(dxn * xn, axis=-1, keepdims=True))
        dx_ref[...] = dx
        dx16_ref[...] = dx.astype(BF16)

        @pl.when(pl.program_id(0) == 0)
        def _():
            dg_ref[...] = jnp.zeros_like(dg_ref)
            loss_ref[...] = jnp.zeros_like(loss_ref)

        dg_ref[...] += jnp.broadcast_to(jnp.sum(dy * xn, axis=0, keepdims=True), dg_ref.shape)
        row_loss = jnp.sum(err * err, axis=-1, keepdims=True)
        loss_ref[...] += jnp.broadcast_to((0.5 / d) * jnp.sum(row_loss, axis=0, keepdims=True), loss_ref.shape)

    row = pl.BlockSpec((tm, d), lambda i: (i, 0))
    return pl.pallas_call(
        body, grid=(s // tm,), in_specs=[row, pl.BlockSpec((1, d), lambda i: (0, 0)), row],
        out_specs=[pl.BlockSpec((8, LANES), lambda i: (0, 0)), row, row, pl.BlockSpec((8, d), lambda i: (0, 0))],
        out_shape=[jax.ShapeDtypeStruct((8, LANES), F32), jax.ShapeDtypeStruct((s, d), F32),
                   jax.ShapeDtypeStruct((s, d), BF16), jax.ShapeDtypeStruct((8, d), F32)],
        name=name, compiler_params=_cparams(("arbitrary",)),
    )(x, g, target)


def _tri_dot(tri, x):
    hi = x.astype(BF16)
    lo = (x - hi.astype(F32)).astype(BF16)
    return jnp.dot(tri, hi, preferred_element_type=F32) + jnp.dot(tri, lo, preferred_element_type=F32)


def _chunk_masks(rows):
    r, c = np.arange(rows)[:, None], np.arange(rows)[None, :]
    same = (r // CHUNK) == (c // CHUNK)
    return [jnp.asarray(m, BF16) for m in (same & (r >= c), same & (r > c), same)]


def _block_decay(fz, w, b, incl, ones=None):
    fg = jnp.dot(fz, w, preferred_element_type=F32) + b
    la = (jnp.minimum(fg, 0.0) - jnp.log(1.0 + jnp.exp(-jnp.abs(fg)))) * (1.0 / TAU)
    cum = _tri_dot(incl, la)
    ends = [cum[i + CHUNK - 1:i + CHUNK, :] for i in range(0, fz.shape[0], CHUNK)]
    if ones is None:
        end = jnp.concatenate([jnp.broadcast_to(e, (CHUNK, e.shape[1])) for e in ends], axis=0)
    else:
        end = _tri_dot(ones, la)
    return fg, jnp.exp(end - cum), [jnp.exp(e) for e in ends]


_TN = (((0,), (0,)), ((), ()))
_NT = (((1,), (1,)), ((), ()))


def _gla_specs(rows):
    q_spec = pl.BlockSpec((rows, HEAD_K), lambda h, c: (c, h))
    k_spec = pl.BlockSpec((rows, HEAD_K), lambda h, c: (c, HEADS + h))
    v_spec = pl.BlockSpec((rows, HEAD_V), lambda h, c: (c, HEADS + h))
    fz_spec = pl.BlockSpec((rows, LANES), lambda h, c: (c, FZ_COL // LANES))
    w_spec = pl.BlockSpec((LANES, HEAD_K), lambda h, c: (0, h))
    b_spec = pl.BlockSpec((1, HEAD_K), lambda h, c: (0, h))
    mask_spec = pl.BlockSpec((rows, rows), lambda h, c: (0, 0))
    return q_spec, k_spec, v_spec, fz_spec, w_spec, b_spec, mask_spec


def _gla_fwd(proj, wfg, bfg, name):
    s = proj.shape[0]
    nc = s // CHUNK
    per = _pick(nc, (8, 4, 2, 1))
    rows = per * CHUNK
    incl, _, ones = _chunk_masks(rows)

    def body(q_ref, k_ref, v_ref, fz_ref, w_ref, b_ref, incl_ref, ones_ref, o_ref, st_ref, state, update):
        @pl.when(pl.program_id(1) == 0)
        def _():
            state[...] = jnp.zeros_like(state)

        _, dec, gammas = _block_decay(fz_ref[...], w_ref[...], b_ref[...], incl_ref[...], ones_ref[...])
        kd = (k_ref[...].astype(F32) * dec).astype(BF16)
        qs = (q_ref[...].astype(F32) * Q_SCALE).astype(BF16)
        for i in range(per):
            sl = slice(i * CHUNK, (i + 1) * CHUNK)
            update[i] = lax.dot_general(v_ref[sl, :], kd[sl], _TN, preferred_element_type=F32)
        st = state[...]
        for i in range(per):
            st = st * gammas[i] + update[i]
            st_ref[0, i] = st.astype(BF16)
        state[...] = st
        for i in range(per):
            sl = slice(i * CHUNK, (i + 1) * CHUNK)
            o_ref[sl, :] = lax.dot_general(qs[sl], st_ref[0, i], _NT, preferred_element_type=F32).astype(o_ref.dtype)

    q_spec, k_spec, v_spec, fz_spec, w_spec, b_spec, mask_spec = _gla_specs(rows)
    return pl.pallas_call(
        body, grid=(HEADS, nc // per),
        in_specs=[q_spec, k_spec, v_spec, fz_spec, w_spec, b_spec, mask_spec, mask_spec],
        out_specs=[pl.BlockSpec((rows, HEAD_V), lambda h, c: (c, h)),
                   pl.BlockSpec((1, per, HEAD_V, HEAD_K), lambda h, c: (h, c, 0, 0))],
        out_shape=[jax.ShapeDtypeStruct((s, VAL_W), BF16),
                   jax.ShapeDtypeStruct((HEADS, nc, HEAD_V, HEAD_K), BF16)],
        scratch_shapes=[pltpu.VMEM((HEAD_V, HEAD_K), F32), pltpu.VMEM((per, HEAD_V, HEAD_K), F32)], name=name,
        compiler_params=_cparams(("parallel", "arbitrary")),
    )(proj, proj, proj, proj, wfg, bfg, incl, ones)


def _gla_bwd(proj, wfg, bfg, states, do, name):
    s = proj.shape[0]
    nc = s // CHUNK
    per = _pick(nc, (8, 4, 2, 1))
    rows = per * CHUNK
    nblk = nc // per
    incl, strict, _ = _chunk_masks(rows)

    def rev(spec_fn):
        return lambda h, j: spec_fn(h, nblk - 1 - j)

    def body(q_ref, k_ref, v_ref, fz_ref, w_ref, b_ref, incl_ref, strict_ref, do_ref, st_ref, prev_ref,
             dq_ref, dk_ref, dv_ref, dfg_ref, db_ref, carry, gt_all, dkd_all, dgg_all):
        j = pl.program_id(1)

        @pl.when(j == 0)
        def _():
            carry[...] = jnp.zeros_like(carry)
            db_ref[...] = jnp.zeros_like(db_ref)

        fg, dec, gammas = _block_decay(fz_ref[...], w_ref[...], b_ref[...], incl_ref[...])
        kd = k_ref[...].astype(F32) * dec
        kd16 = kd.astype(BF16)
        qs = (q_ref[...].astype(F32) * Q_SCALE).astype(BF16)
        for i in range(per):
            sl = slice(i * CHUNK, (i + 1) * CHUNK)
            gt_all[i] = lax.dot_general(do_ref[sl, :], qs[sl], _TN, preferred_element_type=F32)
        back = carry[...]
        for i in reversed(range(per)):
            gt = back + gt_all[i]
            gt_all[i] = gt
            back = gt * gammas[i]
        carry[...] = back
        has_prev = (j < nblk - 1).astype(F32)
        for i in range(per):
            sl = slice(i * CHUNK, (i + 1) * CHUNK)
            gt = gt_all[i]
            gt16 = gt.astype(BF16)
            dov = do_ref[sl, :]
            dq_ref[sl, :] = (jnp.dot(dov, st_ref[0, i], preferred_element_type=F32) * Q_SCALE).astype(dq_ref.dtype)
            dkd_all[sl, :] = jnp.dot(v_ref[sl, :], gt16, preferred_element_type=F32)
            dv_ref[sl, :] = lax.dot_general(kd16[sl], gt16, _NT, preferred_element_type=F32).astype(dv_ref.dtype)
            if i > 0:
                st_prev = st_ref[0, i - 1].astype(F32)
            else:
                st_prev = prev_ref[0, 0].astype(F32) * has_prev
            dgamma = jnp.sum(gt * st_prev, axis=0, keepdims=True)
            dgg_all[sl, :] = jnp.broadcast_to(dgamma * gammas[i], (CHUNK, HEAD_K))
        dkd = dkd_all[...]
        dk_ref[...] = (dkd * dec).astype(dk_ref.dtype)
        dla = dgg_all[...] + _tri_dot(strict_ref[...], dkd * kd)
        dfg = dla * (1.0 / TAU) * _sigmoid(-fg)
        dfg_ref[...] = dfg.astype(dfg_ref.dtype)
        db_ref[...] += jnp.broadcast_to(jnp.sum(dfg, axis=0, keepdims=True), db_ref.shape)

    q_spec, k_spec, v_spec, fz_spec, w_spec, b_spec, mask_spec = _gla_specs(rows)
    q_spec, k_spec, v_spec, fz_spec = [
        pl.BlockSpec(sp.block_shape, rev(sp.index_map)) for sp in (q_spec, k_spec, v_spec, fz_spec)]
    do_spec = pl.BlockSpec((rows, HEAD_V), lambda h, j: (nblk - 1 - j, h))
    st_spec = pl.BlockSpec((1, per, HEAD_V, HEAD_K), lambda h, j: (h, nblk - 1 - j, 0, 0))
    prev_spec = pl.BlockSpec((1, 1, HEAD_V, HEAD_K),
                             lambda h, j: (h, jnp.maximum((nblk - 1 - j) * per - 1, 0), 0, 0))
    key_out = pl.BlockSpec((rows, HEAD_K), lambda h, j: (nblk - 1 - j, h))
    return pl.pallas_call(
        body, grid=(HEADS, nblk),
        in_specs=[q_spec, k_spec, v_spec, fz_spec, w_spec, b_spec, mask_spec, mask_spec,
                  do_spec, st_spec, prev_spec],
        out_specs=[key_out, key_out, do_spec, key_out, pl.BlockSpec((8, HEAD_K), lambda h, j: (0, h))],
        out_shape=[jax.ShapeDtypeStruct((s, KEY_W), BF16), jax.ShapeDtypeStruct((s, KEY_W), BF16),
                   jax.ShapeDtypeStruct((s, VAL_W), BF16), jax.ShapeDtypeStruct((s, KEY_W), BF16),
                   jax.ShapeDtypeStruct((8, KEY_W), F32)],
        scratch_shapes=[pltpu.VMEM((HEAD_V, HEAD_K), F32), pltpu.VMEM((per, HEAD_V, HEAD_K), F32),
                        pltpu.VMEM((rows, HEAD_K), F32), pltpu.VMEM((rows, HEAD_K), F32)], name=name,
        compiler_params=_cparams(("parallel", "arbitrary")),
    )(proj, proj, proj, proj, wfg, bfg, incl, strict, do, states, states)


def _place_qkv(dq, dk, dv, dproj, name):
    s = dq.shape[0]
    tm = _pick(s, (512, 256))

    def body(dq_ref, dk_ref, dv_ref, _, o_ref):
        o_ref[:, 0:KEY_W] = dq_ref[...]
        o_ref[:, KEY_W:2 * KEY_W] = dk_ref[...]
        o_ref[:, 2 * KEY_W:2 * KEY_W + VAL_W] = dv_ref[...]

    def rows(width):
        return pl.BlockSpec((tm, width), lambda i: (i, 0))

    return pl.pallas_call(
        body, grid=(s // tm,), in_specs=[rows(KEY_W), rows(KEY_W), rows(VAL_W), _ANY],
        out_specs=rows(2 * KEY_W + VAL_W), out_shape=jax.ShapeDtypeStruct(dproj.shape, BF16),
        input_output_aliases={3: 0}, name=name, compiler_params=_cparams(("parallel",)),
    )(dq, dk, dv, dproj)


def _seg(tm, seg):
    return pl.BlockSpec((tm, D_MODEL), lambda i: (i, seg))


def _gla_post_fwd(o, proj, g, name):
    s = o.shape[0]
    tm = _pick(s, (512, 256))

    def body(o_ref, r_ref, g_ref, oa_ref):
        gv = g_ref[...]
        for h in range(HEADS):
            sl = slice(h * HEAD_V, (h + 1) * HEAD_V)
            ov = o_ref[:, sl].astype(F32)
            rstd = lax.rsqrt(jnp.mean(ov * ov, axis=-1, keepdims=True) + EPS)
            rv = r_ref[:, sl].astype(F32)
            oa_ref[:, sl] = (ov * rstd * gv * (rv * _sigmoid(rv))).astype(oa_ref.dtype)

    row = pl.BlockSpec((tm, VAL_W), lambda i: (i, 0))
    return pl.pallas_call(
        body, grid=(s // tm,), in_specs=[row, _seg(tm, SEG_R), pl.BlockSpec((1, HEAD_V), lambda i: (0, 0))],
        out_specs=row, out_shape=jax.ShapeDtypeStruct((s, VAL_W), BF16), name=name,
        compiler_params=_cparams(("parallel",)),
    )(o, proj, g)


def _gla_post_bwd(doa, o, proj, g, dproj, name):
    s = o.shape[0]
    tm = _pick(s, (512, 256))

    def body(doa_ref, o_ref, r_ref, g_ref, _, dr_ref, do_ref, dg_ref):
        @pl.when(pl.program_id(0) == 0)
        def _():
            dg_ref[...] = jnp.zeros_like(dg_ref)

        gv = g_ref[...]
        dg = jnp.zeros((1, HEAD_V), F32)
        for h in range(HEADS):
            sl = slice(h * HEAD_V, (h + 1) * HEAD_V)
            ov = o_ref[:, sl].astype(F32)
            rstd = lax.rsqrt(jnp.mean(ov * ov, axis=-1, keepdims=True) + EPS)
            ohat = ov * rstd
            rv = r_ref[:, sl].astype(F32)
            sg = _sigmoid(rv)
            dv = doa_ref[:, sl].astype(F32)
            dr_ref[:, sl] = (dv * ohat * gv * (sg * (1.0 + rv * (1.0 - sg)))).astype(dr_ref.dtype)
            don = dv * (rv * sg)
            dg = dg + jnp.sum(don * ohat, axis=0, keepdims=True)
            dohat = don * gv
            do_ref[:, sl] = (rstd * (dohat - ohat * jnp.mean(dohat * ohat, axis=-1, keepdims=True))
                             ).astype(do_ref.dtype)
        dg_ref[...] += jnp.broadcast_to(dg, dg_ref.shape)

    row = pl.BlockSpec((tm, VAL_W), lambda i: (i, 0))
    return pl.pallas_call(
        body, grid=(s // tm,),
        in_specs=[row, row, _seg(tm, SEG_R), pl.BlockSpec((1, HEAD_V), lambda i: (0, 0)), _ANY],
        out_specs=[_seg(tm, SEG_R), row, pl.BlockSpec((8, HEAD_V), lambda i: (0, 0))],
        out_shape=[jax.ShapeDtypeStruct(dproj.shape, BF16), jax.ShapeDtypeStruct((s, VAL_W), BF16),
                   jax.ShapeDtypeStruct((8, HEAD_V), F32)],
        input_output_aliases={4: 0}, name=name, compiler_params=_cparams(("arbitrary",)),
    )(doa, o, proj, g, dproj)


HALO = SUBLANES_BF16


def _shift_down(u, p1, p2, n, rows):
    rolled = pltpu.roll(u, n, 0)
    if n == 1:
        return jnp.where(rows == 0, p1, rolled)
    return jnp.where(rows == 0, p2, jnp.where(rows == 1, p1, rolled))


def _shift_up(u, n1, n2, n, rows, tm):
    rolled = pltpu.roll(u, tm - n, 0)
    if n == 1:
        return jnp.where(rows == tm - 1, n1, rolled)
    return jnp.where(rows == tm - 2, n1, jnp.where(rows == tm - 1, n2, rolled))


def _conv_terms(gc_ref, cx_ref, gcp_ref, cxp_ref, tm):
    i = pl.program_id(0)
    u = gc_ref[...].astype(F32) * cx_ref[...].astype(F32)
    up = gcp_ref[...].astype(F32) * cxp_ref[...].astype(F32) * (i > 0).astype(F32)
    rows = lax.broadcasted_iota(jnp.int32, (tm, 1), 0)
    u1 = _shift_down(u, up[HALO - 1:HALO, :], up[HALO - 2:HALO - 1, :], 1, rows)
    u2 = _shift_down(u, up[HALO - 1:HALO, :], up[HALO - 2:HALO - 1, :], 2, rows)
    return u, u1, u2, rows


def _prev_halo(tm, seg):
    return pl.BlockSpec((HALO, D_MODEL), lambda i: (jnp.maximum(i * (tm // HALO) - 1, 0), seg))


def _conv_fwd(proj, w, b, name):
    s = proj.shape[0]
    tm = _pick(s, (512, 256))

    def body(gbi_ref, gc_ref, cx_ref, gcp_ref, cxp_ref, w_ref, b_ref, cb_ref):
        u, u1, u2, _ = _conv_terms(gc_ref, cx_ref, gcp_ref, cxp_ref, tm)
        conv = w_ref[0:1, :] * u2 + w_ref[1:2, :] * u1 + w_ref[2:3, :] * u + b_ref[...]
        cb_ref[...] = (gbi_ref[...].astype(F32) * conv).astype(cb_ref.dtype)

    return pl.pallas_call(
        body, grid=(s // tm,),
        in_specs=[_seg(tm, SEG_GBI), _seg(tm, SEG_GCI), _seg(tm, SEG_CX),
                  _prev_halo(tm, SEG_GCI), _prev_halo(tm, SEG_CX),
                  pl.BlockSpec((3, D_MODEL), lambda i: (0, 0)), pl.BlockSpec((1, D_MODEL), lambda i: (0, 0))],
        out_specs=pl.BlockSpec((tm, D_MODEL), lambda i: (i, 0)),
        out_shape=jax.ShapeDtypeStruct((s, D_MODEL), BF16), name=name, compiler_params=_cparams(("parallel",)),
    )(proj, proj, proj, proj, proj, w, b)


def _conv_bwd(dcb, proj, w, b, dproj, name):
    s = proj.shape[0]
    tm = _pick(s, (512, 256))
    nt = s // tm

    def body(dcb_ref, gbi_ref, gc_ref, cx_ref, gcp_ref, cxp_ref, dcbn_ref, gbin_ref, w_ref, b_ref, _,
             d3_ref, dwb_ref):
        i = pl.program_id(0)

        @pl.when(i == 0)
        def _():
            dwb_ref[...] = jnp.zeros_like(dwb_ref)

        u, u1, u2, rows = _conv_terms(gc_ref, cx_ref, gcp_ref, cxp_ref, tm)
        w0, w1, w2 = w_ref[0:1, :], w_ref[1:2, :], w_ref[2:3, :]
        conv = w0 * u2 + w1 * u1 + w2 * u + b_ref[...]
        dcbv = dcb_ref[...].astype(F32)
        gbi = gbi_ref[...].astype(F32)
        dconv = dcbv * gbi
        dnext = dcbn_ref[...].astype(F32) * gbin_ref[...].astype(F32) * (i < nt - 1).astype(F32)
        dc1 = _shift_up(dconv, dnext[0:1, :], dnext[1:2, :], 1, rows, tm)
        dc2 = _shift_up(dconv, dnext[0:1, :], dnext[1:2, :], 2, rows, tm)
        du = w2 * dconv + w1 * dc1 + w0 * dc2
        d3_ref[:, 0:D_MODEL] = (dcbv * conv).astype(d3_ref.dtype)
        d3_ref[:, D_MODEL:2 * D_MODEL] = (du * cx_ref[...].astype(F32)).astype(d3_ref.dtype)
        d3_ref[:, 2 * D_MODEL:3 * D_MODEL] = (du * gc_ref[...].astype(F32)).astype(d3_ref.dtype)
        dwb_ref[0:1, :] += jnp.sum(dconv * u2, axis=0, keepdims=True)
        dwb_ref[1:2, :] += jnp.sum(dconv * u1, axis=0, keepdims=True)
        dwb_ref[2:3, :] += jnp.sum(dconv * u, axis=0, keepdims=True)
        dwb_ref[3:4, :] += jnp.sum(dconv, axis=0, keepdims=True)

    def next_halo(seg_fn):
        return pl.BlockSpec((HALO, D_MODEL), lambda i: (jnp.minimum((i + 1) * (tm // HALO), s // HALO - 1), seg_fn))

    return pl.pallas_call(
        body, grid=(nt,),
        in_specs=[pl.BlockSpec((tm, D_MODEL), lambda i: (i, 0)),
                  _seg(tm, SEG_GBI), _seg(tm, SEG_GCI), _seg(tm, SEG_CX),
                  _prev_halo(tm, SEG_GCI), _prev_halo(tm, SEG_CX),
                  next_halo(0), next_halo(SEG_GBI),
                  pl.BlockSpec((3, D_MODEL), lambda i: (0, 0)), pl.BlockSpec((1, D_MODEL), lambda i: (0, 0)), _ANY],
        out_specs=[pl.BlockSpec((tm, 3 * D_MODEL), lambda i: (i, SEG_GBI // 3)),
                   pl.BlockSpec((8, D_MODEL), lambda i: (0, 0))],
        out_shape=[jax.ShapeDtypeStruct(dproj.shape, BF16), jax.ShapeDtypeStruct((8, D_MODEL), F32)],
        input_output_aliases={10: 0}, name=name, compiler_params=_cparams(("arbitrary",)),
    )(dcb, proj, proj, proj, proj, proj, dcb, proj, w, b, dproj)


def _mix_fwd(proj, ya, yb, name):
    s = proj.shape[0]
    tm = _pick(s, (512, 256))

    def body(ga_ref, gb_ref, ya_ref, yb_ref, o_ref):
        o_ref[...] = (_sigmoid(ga_ref[...].astype(F32)) * ya_ref[...].astype(F32)
                      + _sigmoid(gb_ref[...].astype(F32)) * yb_ref[...].astype(F32)).astype(o_ref.dtype)

    row = pl.BlockSpec((tm, D_MODEL), lambda i: (i, 0))
    return pl.pallas_call(
        body, grid=(s // tm,), in_specs=[_seg(tm, SEG_GA), _seg(tm, SEG_GB), row, row], out_specs=row,
        out_shape=jax.ShapeDtypeStruct((s, D_MODEL), BF16), name=name, compiler_params=_cparams(("parallel",)),
    )(proj, proj, ya, yb)


def _mix_bwd(dmix, proj, ya, yb, name):
    s = proj.shape[0]
    tm = _pick(s, (512, 256))

    def body(dm_ref, ga_ref, gb_ref, ya_ref, yb_ref, dg_ref, dya_ref, dyb_ref):
        dm = dm_ref[...].astype(F32)
        sa = _sigmoid(ga_ref[...].astype(F32))
        sb = _sigmoid(gb_ref[...].astype(F32))
        dg_ref[:, 0:D_MODEL] = (dm * ya_ref[...].astype(F32) * sa * (1.0 - sa)).astype(dg_ref.dtype)
        dg_ref[:, D_MODEL:2 * D_MODEL] = (dm * yb_ref[...].astype(F32) * sb * (1.0 - sb)).astype(dg_ref.dtype)
        dya_ref[...] = (dm * sa).astype(dya_ref.dtype)
        dyb_ref[...] = (dm * sb).astype(dyb_ref.dtype)

    row = pl.BlockSpec((tm, D_MODEL), lambda i: (i, 0))
    return pl.pallas_call(
        body, grid=(s // tm,), in_specs=[row, _seg(tm, SEG_GA), _seg(tm, SEG_GB), row, row],
        out_specs=[pl.BlockSpec((tm, 2 * D_MODEL), lambda i: (i, SEG_GA // 2)), row, row],
        out_shape=[jax.ShapeDtypeStruct((s, PROJ_W), BF16), jax.ShapeDtypeStruct((s, D_MODEL), BF16),
                   jax.ShapeDtypeStruct((s, D_MODEL), BF16)],
        name=name, compiler_params=_cparams(("parallel",)),
    )(dmix, proj, proj, ya, yb)


def _swiglu_fwd(gu, name):
    s = gu.shape[0]
    tm = _pick(s, (256,))

    def body(gu_ref, o_ref):
        gate = gu_ref[:, 0:FFN].astype(F32)
        o_ref[...] = (gate * _sigmoid(gate) * gu_ref[:, FFN:2 * FFN].astype(F32)).astype(o_ref.dtype)

    return pl.pallas_call(
        body, grid=(s // tm,), in_specs=[pl.BlockSpec((tm, 2 * FFN), lambda i: (i, 0))],
        out_specs=pl.BlockSpec((tm, FFN), lambda i: (i, 0)),
        out_shape=jax.ShapeDtypeStruct((s, FFN), BF16), name=name, compiler_params=_cparams(("parallel",)),
    )(gu)


def _swiglu_bwd(dhid, gu, name):
    s = gu.shape[0]
    tm = _pick(s, (256,))

    def body(dh_ref, gu_ref, o_ref):
        gate = gu_ref[:, 0:FFN].astype(F32)
        up = gu_ref[:, FFN:2 * FFN].astype(F32)
        dh = dh_ref[...].astype(F32)
        sg = _sigmoid(gate)
        o_ref[:, 0:FFN] = (dh * up * (sg * (1.0 + gate * (1.0 - sg)))).astype(o_ref.dtype)
        o_ref[:, FFN:2 * FFN] = (dh * gate * sg).astype(o_ref.dtype)

    wide = pl.BlockSpec((tm, 2 * FFN), lambda i: (i, 0))
    return pl.pallas_call(
        body, grid=(s // tm,), in_specs=[pl.BlockSpec((tm, FFN), lambda i: (i, 0)), wide], out_specs=wide,
        out_shape=jax.ShapeDtypeStruct((s, 2 * FFN), BF16), name=name, compiler_params=_cparams(("parallel",)),
    )(dhid, gu)


def _adamw_math(w, g, m, v):
    m2 = ADAM_B1 * m + (1.0 - ADAM_B1) * g
    v2 = ADAM_B2 * v + (1.0 - ADAM_B2) * (g * g)
    m_hat = m2 / (1.0 - ADAM_B1 ** ADAM_STEP)
    v_hat = v2 / (1.0 - ADAM_B2 ** ADAM_STEP)
    delta = -ADAM_LR * (m_hat / (jnp.sqrt(v_hat) + ADAM_EPS) + ADAM_WD * w)
    return delta, m2, v2


def _adamw(w, g, m, v, name):
    shape = w.shape
    cols = shape[-1]
    rows = int(np.prod(shape[:-1])) if len(shape) > 1 else 1
    w2, g2, m2, v2 = [t.reshape(rows, cols) for t in (w, g, m, v)]
    tr = _pick(rows, (512, 352, 256)) if rows % 8 == 0 else rows

    def body(w_ref, g_ref, m_ref, v_ref, d_ref, nm_ref, nv_ref):
        d, nm, nv = _adamw_math(w_ref[...], g_ref[...], m_ref[...], v_ref[...])
        d_ref[...] = d
        nm_ref[...] = nm
        nv_ref[...] = nv

    blk = pl.BlockSpec((tr, cols), lambda i: (i, 0))
    out = pl.pallas_call(
        body, grid=(rows // tr,), in_specs=[blk] * 4, out_specs=[blk] * 3,
        out_shape=[jax.ShapeDtypeStruct((rows, cols), F32)] * 3, name=name,
        compiler_params=_cparams(("parallel",)),
    )(w2, g2, m2, v2)
    return [t.reshape(shape) for t in out]


def _pair_sum(g2, recv, core, name, after=()):
    _, nchip, r, c = g2.shape
    tr = _pick(r, (512,))

    def body(core_ref, a_ref, b_ref, *rest):
        o_ref = rest[-1]
        o_ref[...] = (a_ref[...].astype(F32) + b_ref[...].astype(F32)).astype(o_ref.dtype)

    grid_spec = pltpu.PrefetchScalarGridSpec(
        num_scalar_prefetch=1, grid=(nchip, r // tr),
        in_specs=[pl.BlockSpec((None, None, tr, c), lambda k, i, cr: (cr[0], k, i, 0)),
                  pl.BlockSpec((None, tr, c), lambda k, i, cr: (k, i, 0))] + [_after_spec(t) for t in after],
        out_specs=pl.BlockSpec((None, tr, c), lambda k, i, cr: (k, i, 0)))
    return pl.pallas_call(
        body, grid_spec=grid_spec, out_shape=jax.ShapeDtypeStruct((nchip, r, c), BF16), name=name,
        compiler_params=_cparams(("parallel", "parallel")),
    )(core, g2, recv, *after)


def _chip_sum(a, recv, chip, name, after=()):
    _, r, c = a.shape
    tr = _pick(r, (512,))

    def body(chip_ref, a_ref, b_ref, *rest):
        o_ref = rest[-1]
        o_ref[...] = ((a_ref[...].astype(F32) + b_ref[0].astype(F32)) + b_ref[1].astype(F32)) + b_ref[2].astype(F32)

    grid_spec = pltpu.PrefetchScalarGridSpec(
        num_scalar_prefetch=1, grid=(r // tr,),
        in_specs=[pl.BlockSpec((None, tr, c), lambda i, cr: (cr[0], i, 0)),
                  pl.BlockSpec((3, tr, c), lambda i, cr: (0, i, 0))] + [_after_spec(t) for t in after],
        out_specs=pl.BlockSpec((tr, c), lambda i, cr: (i, 0)))
    return pl.pallas_call(
        body, grid_spec=grid_spec, out_shape=jax.ShapeDtypeStruct((r, c), F32), name=name,
        compiler_params=_cparams(("parallel",)),
    )(chip, a, recv, *after)


def _sum_devices(parts, name):
    n, r, c = parts.shape

    def body(p_ref, o_ref):
        acc = p_ref[0]
        for d in range(1, n):
            acc = acc + p_ref[d]
        o_ref[...] = acc

    return pl.pallas_call(
        body, out_shape=jax.ShapeDtypeStruct((r, c), F32), name=name,
        in_specs=[pl.BlockSpec(memory_space=pltpu.VMEM)], out_specs=pl.BlockSpec(memory_space=pltpu.VMEM),
    )(parts)


def _lane_iota():
    return lax.broadcasted_iota(jnp.int32, (1, LANES), 1)


def _tiles_up(tiles, s, lane):
    rolled = [pltpu.roll(t, s, 1) for t in tiles]
    zero = jnp.zeros_like(tiles[0])
    return [jnp.where(lane < s, p, c) for p, c in zip([zero] + rolled, rolled + [zero])]


def _tiles_down(tiles, s, lane):
    back = (LANES - s) % LANES
    rolled = [pltpu.roll(t, back, 1) for t in tiles]
    zero = jnp.zeros_like(tiles[0])
    return [jnp.where(lane < LANES - s, c, n) for c, n in zip(rolled, rolled[1:] + [zero])]


def _window_cols(prm, w_in, w_gate, w_up, l, name, after=None):
    tm = 256
    n_in, n_gu = WIN_IN // LANES, FFN_SHARD // LANES + 1

    def body(prm_ref, win_ref, g_ref, u_ref, *rest):
        out_ref, gu_ref, scr_in, scr_gu = rest[-4:]
        lane = _lane_iota()
        s_main, lo, hi, s_code, clo, chi, code_hi, s_gu = [prm_ref[i] for i in range(8)]
        scr_in[:, D_MODEL:WIN_IN] = jnp.zeros((tm, WIN_IN - D_MODEL), F32)
        scr_in[:, 0:W_IN_SHARD] = win_ref[...]

        def keep(t, a, b):
            col = lane + t * LANES
            return jnp.where((col >= a) & (col < b), scr_in[:, t * LANES:(t + 1) * LANES], 0.0)

        main = _tiles_up([keep(t, lo, hi) for t in range(n_in)], s_main, lane)
        for t in range(n_in):
            out_ref[:, t * LANES:(t + 1) * LANES] = main[t].astype(BF16)
        low = _tiles_up([keep(0, clo, chi)], s_code, lane)[0]
        high = _tiles_up([keep(n_in - 2, clo, chi), keep(n_in - 1, clo, chi)], s_code, lane)[1]
        out_ref[:, A_FZ:A_FZ + LANES] = jnp.where(code_hi == 1, high, low).astype(BF16)
        for ref, base in ((g_ref, 0), (u_ref, WIN_GU)):
            scr_gu[:, (n_gu - 1) * LANES:n_gu * LANES] = jnp.zeros((tm, LANES), F32)
            scr_gu[:, 0:FFN_SHARD] = ref[...]
            moved = _tiles_up([scr_gu[:, t * LANES:(t + 1) * LANES] for t in range(n_gu)], s_gu, lane)
            for t in range(n_gu + 1):
                gu_ref[:, base + t * LANES:base + (t + 1) * LANES] = moved[t].astype(BF16)

    after_args = [] if after is None else [after]
    grid_spec = pltpu.PrefetchScalarGridSpec(
        num_scalar_prefetch=1, grid=(D_MODEL // tm,),
        in_specs=[pl.BlockSpec((None, tm, W_IN_SHARD), lambda i, p: (l, i, 0)),
                  pl.BlockSpec((None, tm, FFN_SHARD), lambda i, p: (l, i, 0)),
                  pl.BlockSpec((None, tm, FFN_SHARD), lambda i, p: (l, i, 0))] + [_after_spec(t) for t in after_args],
        out_specs=[pl.BlockSpec((tm, IN_COLS), lambda i, p: (i, 0)), pl.BlockSpec((tm, GU_COLS), lambda i, p: (i, 0))],
        scratch_shapes=[pltpu.VMEM((tm, WIN_IN), F32), pltpu.VMEM((tm, n_gu * LANES), F32)])
    return pl.pallas_call(
        body, grid_spec=grid_spec, name=name, compiler_params=_cparams(("parallel",)),
        out_shape=[jax.ShapeDtypeStruct((D_MODEL, IN_COLS), BF16), jax.ShapeDtypeStruct((D_MODEL, GU_COLS), BF16)],
    )(prm, w_in, w_gate, w_up, *after_args)


def _gu_width(j):
    return min(WIN_GU, FFN - T_GU[j] * LANES)


def _after_spec(t):
    tile = (SUBLANES_BF16 if t.dtype == BF16 else 8, LANES)
    return pl.BlockSpec((None,) * (t.ndim - 2) + tile, lambda *_: (0,) * t.ndim)


def _assemble_in(a_all, tail_all, name, after=()):
    tm = D_MODEL // N_DEV

    def body(a_ref, t_ref, *rest):
        win_ref, tail_ref = rest[-2:]
        tail_ref[...] = t_ref[...]
        win_ref[...] = jnp.zeros_like(win_ref)
        code = a_ref[0, :, A_FZ:A_FZ + LANES]
        for j in range(N_DEV):
            c0 = T_IN[j] * LANES
            win_ref[:, c0:c0 + WIN_IN] += a_ref[j, :, 0:WIN_IN]
            if j > 0:
                code = code + a_ref[j, :, A_FZ:A_FZ + LANES]
        win_ref[:, FZ_COL:PROJ_W] = code

    return pl.pallas_call(
        body, grid=(N_DEV,),
        in_specs=[pl.BlockSpec((N_DEV, tm, IN_COLS), lambda i: (0, i, 0)),
                  pl.BlockSpec((None,) + tail_all.shape[1:], lambda i: (i, 0, 0))] + [_after_spec(t) for t in after],
        out_specs=[pl.BlockSpec((tm, PROJ_W), lambda i: (i, 0)),
                   pl.BlockSpec((None,) + tail_all.shape[1:], lambda i: (i, 0, 0))],
        out_shape=[jax.ShapeDtypeStruct((D_MODEL, PROJ_W), BF16), jax.ShapeDtypeStruct(tail_all.shape, BF16)],
        name=name, compiler_params=_cparams(("parallel",)),
    )(a_all, tail_all, *after)


def _assemble_rest(a_all, rows_all, name, after=()):
    tm = D_MODEL // N_DEV
    n_in = 2 + len(after)

    def body(*refs):
        a_ref, r_ref = refs[:2]
        wgu_ref, oa_ref, ob_ref, o_ref, down_ref = refs[n_in:]
        wgu_ref[...] = jnp.zeros_like(wgu_ref)
        for j in range(N_DEV):
            g0, width = T_GU[j] * LANES, _gu_width(j)
            wgu_ref[:, g0:g0 + width] += a_ref[j, :, 0:width]
            wgu_ref[:, FFN + g0:FFN + g0 + width] += a_ref[j, :, WIN_GU:WIN_GU + width]
        oa_ref[...] = r_ref[B_OA:B_OA + ROW_SHARD, :]
        ob_ref[...] = r_ref[B_OB:B_OB + ROW_SHARD, :]
        o_ref[...] = r_ref[B_O:B_O + ROW_SHARD, :]
        down_ref[...] = r_ref[B_DOWN:B_DOWN + FFN_SHARD, :]

    def rows(n):
        return pl.BlockSpec((n, D_MODEL), lambda i: (i, 0))

    square = jax.ShapeDtypeStruct((D_MODEL, D_MODEL), BF16)
    return pl.pallas_call(
        body, grid=(N_DEV,),
        in_specs=[pl.BlockSpec((N_DEV, tm, GU_COLS), lambda i: (0, i, 0)),
                  pl.BlockSpec((None, B_FG2, D_MODEL), lambda i: (i, 0, 0))] + [_after_spec(t) for t in after],
        out_specs=[pl.BlockSpec((tm, 2 * FFN), lambda i: (i, 0)),
                   rows(ROW_SHARD), rows(ROW_SHARD), rows(ROW_SHARD), rows(FFN_SHARD)],
        out_shape=[jax.ShapeDtypeStruct((D_MODEL, 2 * FFN), BF16),
                   square, square, square, jax.ShapeDtypeStruct((FFN, D_MODEL), BF16)],
        name=name, compiler_params=_cparams(("parallel",)),
    )(a_all, rows_all, *after)


def _grad_windows_in(d_in, name):
    tm = 256

    def body(din_ref, out_ref):
        for j in range(N_DEV):
            c0 = T_IN[j] * LANES
            out_ref[j & 1, j >> 1, :, 0:WIN_IN] = din_ref[:, c0:c0 + WIN_IN]
            out_ref[j & 1, j >> 1, :, A_FZ:IN_COLS] = din_ref[:, FZ_COL:PROJ_W]

    return pl.pallas_call(
        body, grid=(D_MODEL // tm,), in_specs=[pl.BlockSpec((tm, PROJ_W), lambda i: (i, 0))],
        out_specs=pl.BlockSpec((2, 4, tm, IN_COLS), lambda i: (0, 0, i, 0)),
        out_shape=jax.ShapeDtypeStruct((2, 4, D_MODEL, IN_COLS), BF16), name=name,
        compiler_params=_cparams(("parallel",)),
    )(d_in)


def _grad_windows_gu(d_gu, name):
    tm = 256

    def body(dgu_ref, out_ref):
        for j in range(N_DEV):
            g0, width = T_GU[j] * LANES, _gu_width(j)
            for half, base in ((0, 0), (FFN, WIN_GU)):
                out_ref[j & 1, j >> 1, :, base:base + width] = dgu_ref[:, half + g0:half + g0 + width]
                if width < WIN_GU:
                    out_ref[j & 1, j >> 1, :, base + width:base + WIN_GU] = jnp.zeros((tm, WIN_GU - width), BF16)

    return pl.pallas_call(
        body, grid=(D_MODEL // tm,), in_specs=[pl.BlockSpec((tm, 2 * FFN), lambda i: (i, 0))],
        out_specs=pl.BlockSpec((2, 4, tm, GU_COLS), lambda i: (0, 0, i, 0)),
        out_shape=jax.ShapeDtypeStruct((2, 4, D_MODEL, GU_COLS), BF16), name=name,
        compiler_params=_cparams(("parallel",)),
    )(d_gu)


def _final_windows(prm, chip, a, recv, params, l, kind, name, prev=None):
    tm = 128
    n_in, n_gu = WIN_IN // LANES, FFN_SHARD // LANES + 1
    widths = (W_IN_SHARD,) if kind == "in" else (FFN_SHARD, FFN_SHARD)
    cols = IN_COLS if kind == "in" else GU_COLS
    n_par = 3 * len(widths)
    prev = [] if prev is None else [t for group in prev for t in group]

    def body(prm_ref, chip_ref, a_ref, r_ref, *refs):
        ins, outs = refs[:n_par], refs[n_par + len(prev):]
        lane = _lane_iota()
        s_main, s_code, clo, chi, s_gu = [prm_ref[i] for i in (0, 3, 4, 5, 7)]

        def total(c0):
            sl = slice(c0, c0 + LANES)
            return ((a_ref[:, sl].astype(F32) + r_ref[0, :, sl].astype(F32)) + r_ref[1, :, sl].astype(F32)) \
                + r_ref[2, :, sl].astype(F32)

        if kind == "in":
            grads = _tiles_down([total(t * LANES) for t in range(n_in)], s_main, lane)
            code = pltpu.roll(total(A_FZ), (LANES - s_code) % LANES, 1)
            for t in (0, n_in - 2, n_in - 1):
                col = lane + t * LANES
                grads[t] = jnp.where((col >= clo) & (col < chi), code, grads[t])
            per_weight = [grads]
        else:
            per_weight = [_tiles_down([total(base + t * LANES) for t in range(n_gu + 1)], s_gu, lane)[:n_gu]
                          for base in (0, WIN_GU)]
        for k, (tiles, width) in enumerate(zip(per_weight, widths)):
            w_ref, m_ref, v_ref = ins[3 * k:3 * k + 3]
            g_ref, d_ref, nm_ref, nv_ref = outs[4 * k:4 * k + 4]
            for t, g in enumerate(tiles):
                n = min(LANES, width - t * LANES)
                sl = slice(t * LANES, t * LANES + n)
                g = g[:, 0:n]
                d, nm, nv = _adamw_math(w_ref[:, sl], g, m_ref[:, sl], v_ref[:, sl])
                g_ref[:, sl] = g
                d_ref[:, sl] = d
                nm_ref[:, sl] = nm
                nv_ref[:, sl] = nv

    def native(width):
        return pl.BlockSpec((None, tm, width), lambda i, p, c: (l, i, 0))

    in_specs = [pl.BlockSpec((None, tm, cols), lambda i, p, c: (c[0], i, 0)),
                pl.BlockSpec((3, tm, cols), lambda i, p, c: (0, i, 0))]
    in_specs += [native(wd) for wd in widths for _ in range(3)] + [_ANY] * len(prev)
    grid_spec = pltpu.PrefetchScalarGridSpec(
        num_scalar_prefetch=2, grid=(D_MODEL // tm,), in_specs=in_specs,
        out_specs=[native(wd) for wd in widths for _ in range(4)])
    out = pl.pallas_call(
        body, grid_spec=grid_spec, name=name, compiler_params=_cparams(("parallel",)),
        out_shape=[jax.ShapeDtypeStruct((DEPTH, D_MODEL, wd), F32) for wd in widths for _ in range(4)],
        input_output_aliases={4 + n_par + k: k for k in range(len(prev))},
    )(prm, chip, a, recv, *params, *prev)
    return [out[4 * k:4 * k + 4] for k in range(len(widths))]


def _me():
    return lax.axis_index("x"), lax.axis_index("y"), lax.axis_index("c")


_CHIP_FLIPS = ((1, 0), (0, 1), (1, 1))
_ANY = pl.BlockSpec(memory_space=pl.ANY)


def _comm_call(body, peers, out_shape, sems, name, args, collective_id):
    if collective_id is None:
        n_in = len(args)
        return pl.pallas_call(body, out_shape=out_shape, name=name, in_specs=[_ANY] * n_in,
                              out_specs=[_ANY] * len(out_shape), scratch_shapes=sems)(*args)

    def sequencer_body(*refs):
        barrier = pltpu.get_barrier_semaphore()
        targets = peers()
        for peer in targets:
            pl.semaphore_signal(barrier, inc=1, device_id=peer, device_id_type=MESH)
        pl.semaphore_wait(barrier, len(targets))
        body(*refs)

    sequencer = plsc.ScalarSubcoreMesh(axis_name="seq", num_cores=1)
    return pl.kernel(sequencer_body, out_type=out_shape, mesh=sequencer, scratch_types=sems, name=name,
                     compiler_params=pltpu.CompilerParams(collective_id=collective_id))(*args)


def _sibling_peer():
    x, y, cc = _me()
    return [(x, y, 1 - cc)]


def _chip_peers():
    x, y, cc = _me()
    return [(x ^ fx, y ^ fy, cc) for fx, fy in _CHIP_FLIPS]


def _all_gather(shards, name, collective_id=None):
    n = len(shards)
    split = [s.shape[0] % (2 * SUBLANES_BF16) == 0 for s in shards]

    def body(*refs):
        x_refs, out_refs = refs[:n], refs[n:2 * n]
        send_sems, recv_sems, local_sems = refs[2 * n:]
        x, y, cc = _me()
        me, sibling = (x, y, cc), (x, y, 1 - cc)
        near, far = [(x ^ 1, y), (x, y ^ 1)], (x ^ 1, y ^ 1)

        def copy(a, k, block, to, half=None, from_shard=False):
            px, py, pc = block
            slot = out_refs[a].at[4 * px + 2 * py + pc]
            if half is not None:
                rows = shards[a].shape[0] // 2
                slot = slot.at[pl.ds(half * rows, rows)]
            return pltpu.make_async_remote_copy(
                src_ref=x_refs[a] if from_shard else slot, dst_ref=slot,
                send_sem=send_sems.at[a, k], recv_sem=recv_sems.at[a, k], device_id=to, device_id_type=MESH)

        mine = [pltpu.make_async_copy(x_refs[a], out_refs[a].at[4 * x + 2 * y + cc], local_sems.at[a])
                for a in range(n)]
        for cp in mine:
            cp.start()
        sent = [copy(a, 0, me, sibling, from_shard=True) for a in range(n)]
        sent += [copy(a, 1 + j, me, (*chip, cc), from_shard=True) for j, chip in enumerate(near) for a in range(n)]
        sent += [copy(a, 3, me, (*far, cc), from_shard=True) for a in range(n) if not split[a]]
        for cp in sent:
            cp.start()

        def pass_on(cp):
            cp.start()
            sent.append(cp)

        for j, chip in enumerate(near):
            for a in range(n):
                copy(a, 1 + j, (*chip, cc), me).wait_recv()
                pass_on(copy(a, 4 + j, (*chip, cc), sibling))
                if split[a]:
                    pass_on(copy(a, 7 + j, (*chip, cc), (*near[1 - j], cc), half=j))
        for a in range(n):
            if split[a]:
                copy(a, 7, (*far, cc), me, half=0).wait_recv()
                copy(a, 8, (*far, cc), me, half=1).wait_recv()
            else:
                copy(a, 3, (*far, cc), me).wait_recv()
            pass_on(copy(a, 6, (*far, cc), sibling))
        for a in range(n):
            copy(a, 0, sibling, me).wait_recv()
            for j, chip in enumerate(near + [far]):
                copy(a, 4 + j, (*chip, 1 - cc), me).wait_recv()
        for cp in sent:
            cp.wait_send()
        for cp in mine:
            cp.wait()

    return _comm_call(
        body, lambda: _sibling_peer() + _chip_peers(),
        [jax.ShapeDtypeStruct((N_DEV,) + s.shape, s.dtype) for s in shards],
        [pltpu.SemaphoreType.DMA((n, 9)), pltpu.SemaphoreType.DMA((n, 9)), pltpu.SemaphoreType.DMA((n,))],
        name, shards, collective_id)


def _send_to_sibling(parts, name, collective_id=None):
    n = len(parts)

    def body(*refs):
        g_refs, out_refs = refs[:n], refs[n:2 * n]
        send_sems, recv_sems = refs[2 * n:]
        x, y, cc = _me()
        copies = [pltpu.make_async_remote_copy(
            src_ref=g_refs[a].at[1 - cc], dst_ref=out_refs[a], send_sem=send_sems.at[a], recv_sem=recv_sems.at[a],
            device_id=(x, y, 1 - cc), device_id_type=MESH) for a in range(n)]
        for cp in copies:
            cp.start()
        for cp in copies:
            cp.wait()

    return _comm_call(
        body, _sibling_peer, [jax.ShapeDtypeStruct(p.shape[1:], p.dtype) for p in parts],
        [pltpu.SemaphoreType.DMA((n,)), pltpu.SemaphoreType.DMA((n,))], name, parts, collective_id)


def _send_to_chips(parts, name, collective_id=None):
    n = len(parts)

    def body(*refs):
        a_refs, out_refs = refs[:n], refs[n:2 * n]
        send_sems, recv_sems = refs[2 * n:]
        x, y, cc = _me()
        copies = []
        for k, (fx, fy) in enumerate(_CHIP_FLIPS):
            px, py = x ^ fx, y ^ fy
            for a in range(n):
                copies.append(pltpu.make_async_remote_copy(
                    src_ref=a_refs[a].at[2 * px + py], dst_ref=out_refs[a].at[k], send_sem=send_sems.at[a, k],
                    recv_sem=recv_sems.at[a, k], device_id=(px, py, cc), device_id_type=MESH))
                copies[-1].start()
        for cp in copies:
            cp.wait()

    return _comm_call(
        body, _chip_peers, [jax.ShapeDtypeStruct((3,) + p.shape[1:], p.dtype) for p in parts],
        [pltpu.SemaphoreType.DMA((n, 3)), pltpu.SemaphoreType.DMA((n, 3))], name, parts, collective_id)


def _pack_rows(w_oa, w_ob, w_o, w_down, w_fg2, conv_w, l):
    conv_bits = lax.bitcast_convert_type(conv_w[l].reshape(-1), BF16).reshape(1, -1)
    tail = jnp.concatenate([w_fg2[l].astype(BF16).reshape(1, D_MODEL),
                            jnp.pad(conv_bits, ((0, 0), (0, D_MODEL - conv_bits.shape[1])))], axis=0)
    tail = jnp.pad(tail, ((0, B_ROWS - B_FG2 - tail.shape[0]), (0, 0)))
    rows = jnp.concatenate([w_oa[l].astype(BF16), w_ob[l].astype(BF16), w_o[l].astype(BF16),
                            w_down[l].astype(BF16)], axis=0)
    return rows, tail


def _unpack_tail(tail):
    w_fg2 = tail[:, 0, :].reshape(N_DEV, RANK, KEY_W // N_DEV).transpose(1, 0, 2).reshape(RANK, KEY_W)
    conv_bits = tail[:, B_CONV - B_FG2, :2 * 3 * ROW_SHARD].reshape(N_DEV, 3 * ROW_SHARD, 2)
    conv_w = lax.bitcast_convert_type(conv_bits, F32).reshape(N_DEV, 3, ROW_SHARD)
    return jnp.pad(w_fg2, ((0, LANES - RANK), (0, 0))), conv_w.transpose(1, 0, 2).reshape(3, D_MODEL)


def _by_core_chip(t):
    return t.reshape((2, 2, 2) + t.shape[1:]).transpose((2, 0, 1) + tuple(range(3, t.ndim + 2))).reshape(
        (2, 4) + t.shape[1:])


def _grad_rows(g):
    fg2 = g["w_fg2"][:RANK].reshape(RANK, N_DEV, KEY_W // N_DEV).transpose(1, 0, 2).reshape(N_DEV, 1, D_MODEL)
    conv = g["conv_w"].astype(BF16).reshape(3, N_DEV, ROW_SHARD).transpose(1, 0, 2).reshape(N_DEV, 1, 3 * ROW_SHARD)
    tail = jnp.concatenate([fg2, jnp.pad(conv, ((0, 0), (0, 0), (0, D_MODEL - 3 * ROW_SHARD)))], axis=1)
    tail = jnp.pad(tail, ((0, 0), (0, B_ROWS - B_FG2 - 2), (0, 0)))
    parts = [g["w_oa"].reshape(N_DEV, ROW_SHARD, D_MODEL), g["w_ob"].reshape(N_DEV, ROW_SHARD, D_MODEL),
             g["w_o"].reshape(N_DEV, ROW_SHARD, D_MODEL), g["w_down"].reshape(N_DEV, FFN_SHARD, D_MODEL), tail]
    return _by_core_chip(jnp.concatenate(parts, axis=1))


def _ungrad_rows(gs):
    return dict(w_oa=gs[B_OA:B_OA + ROW_SHARD], w_ob=gs[B_OB:B_OB + ROW_SHARD], w_o=gs[B_O:B_O + ROW_SHARD],
                w_ffn_down=gs[B_DOWN:B_DOWN + FFN_SHARD], w_fg2=gs[B_FG2].reshape(RANK, KEY_W // N_DEV),
                conv_w=gs[B_CONV, :3 * ROW_SHARD].reshape(3, ROW_SHARD))


def _layer_fwd(x, p, l):
    tag = f"l{l}_"
    h = _rmsnorm_fwd(x, p["norm1_g"], tag + "norm1")
    proj = _matmul(h, p["w_in"], "nn", BF16, tag + "proj")
    o, states = _gla_fwd(proj, p["w_fg2"], p["b_fg"], tag + "gla_fwd")
    p.update(p.pop("rest")((o,)))
    oa = _gla_post_fwd(o, proj, p["gla_norm_g"], tag + "gla_post")
    ya = _matmul(oa, p["w_oa"], "nn", BF16, tag + "ya")
    cb = _conv_fwd(proj, p["conv_w"], p["conv_b"], tag + "conv")
    yb = _matmul(cb, p["w_ob"], "nn", BF16, tag + "yb")
    mix = _mix_fwd(proj, ya, yb, tag + "mix")
    x1 = _matmul(mix, p["w_o"], "nn", F32, tag + "x1", residual=x)
    h2 = _rmsnorm_fwd(x1, p["norm2_g"], tag + "norm2")
    gu = _matmul(h2, p["w_gu"], "nn", BF16, tag + "gu")
    hid = _swiglu_fwd(gu, tag + "swiglu")
    x2 = _matmul(hid, p["w_down"], "nn", F32, tag + "x2", residual=x1)
    saved = dict(x=x, h=h, proj=proj, o=o, states=states, oa=oa, ya=ya, cb=cb, yb=yb, mix=mix, x1=x1, h2=h2,
                 gu=gu, hid=hid)
    return x2, saved


def _layer_bwd(dx2, dx2h, p, sv, l, reduce=None):
    if reduce is None:
        reduce = lambda group, grads: ((), ())
    tag = f"l{l}_b_"
    dhid = _matmul(dx2h, p["w_down"], "nt", BF16, tag + "dhid")
    d_down = _matmul(sv["hid"], dx2h, "tn", BF16, tag + "dw_down")
    dgu = _swiglu_bwd(dhid, sv["gu"], tag + "swiglu")
    dh2 = _matmul(dgu, p["w_gu"], "nt", F32, tag + "dh2")
    d_gu = _matmul(sv["h2"], dgu, "tn", BF16, tag + "dw_gu")
    gu_packed, gu_sums = reduce("gu", d_gu)
    dx1, dx1h, dg2 = _rmsnorm_bwd(sv["x1"], p["norm2_g"], dh2, dx2, tag + "norm2")
    dmix = _matmul(dx1h, p["w_o"], "nt", BF16, tag + "dmix", after=gu_packed)
    d_o = _matmul(sv["mix"], dx1h, "tn", BF16, tag + "dw_o")
    dproj, dya, dyb = _mix_bwd(dmix, sv["proj"], sv["ya"], sv["yb"], tag + "mix")
    dcb = _matmul(dyb, p["w_ob"], "nt", BF16, tag + "dcb", after=gu_sums)
    d_ob = _matmul(sv["cb"], dyb, "tn", BF16, tag + "dw_ob")
    dproj, dwb = _conv_bwd(dcb, sv["proj"], p["conv_w"], p["conv_b"], dproj, tag + "conv")
    doa = _matmul(dya, p["w_oa"], "nt", BF16, tag + "doa")
    d_oa = _matmul(sv["oa"], dya, "tn", BF16, tag + "dw_oa")
    dproj, do, dgg = _gla_post_bwd(doa, sv["o"], sv["proj"], p["gla_norm_g"], dproj, tag + "gla_post")
    dq, dk, dv, dfg, dbfg = _gla_bwd(sv["proj"], p["w_fg2"], p["b_fg"], sv["states"], do, tag + "gla")
    fz = sv["proj"][:, FZ_COL:]
    dproj = _place_qkv(dq, dk, dv, dproj, tag + "place_qkv")
    dproj = _matmul(dfg, p["w_fg2"], "nt", BF16, tag + "dfz", into=(dproj, FZ_COL))
    d_fg2 = _matmul(fz, dfg, "tn", BF16, tag + "dw_fg2")
    rows = dict(w_fg2=d_fg2, conv_w=dwb[0:3], w_oa=d_oa, w_ob=d_ob, w_o=d_o, w_down=d_down)
    rows_packed, rows_sums = reduce("rows", rows)
    d_in = _matmul(sv["h"], dproj, "tn", BF16, tag + "dw_in", after=rows_packed)
    _, in_sums = reduce("in", d_in)
    dh = _matmul(dproj, p["w_in"], "nt", F32, tag + "dh", after=(d_in,) + tuple(rows_sums))
    dx, dxh, dg1 = _rmsnorm_bwd(sv["x"], p["norm1_g"], dh, dx1, tag + "norm1", after=in_sums)
    big = dict(w_in=d_in, w_gu=d_gu, **rows)
    pad = lambda t: jnp.pad(t, ((0, 0), (0, D_MODEL - t.shape[1])))
    small = [dg1[0:1], pad(dbfg[0:1]), pad(dgg[0:1]), dwb[3:4], dg2[0:1]]
    return dx, dxh, big, small


def _local_step(x, target, weights_of, final_g, reduce_of=None):
    saved, layers = [], []
    for l in range(DEPTH):
        layers.append(weights_of(l, x))
        x, sv = _layer_fwd(x, layers[l], l)
        saved.append(sv)
    loss, dx, dxh, dgf = _loss_head(x, final_g, target, "loss_head")
    bigs, smalls = [None] * DEPTH, [None] * DEPTH
    for l in reversed(range(DEPTH)):
        dx, dxh, bigs[l], smalls[l] = _layer_bwd(dx, dxh, layers[l], saved[l], l, reduce_of(l) if reduce_of else None)
    loss_row = jnp.pad(loss[0:1], ((0, 0), (0, D_MODEL - loss.shape[1])))
    small = jnp.concatenate(smalls[0] + smalls[1] + [dgf[0:1], loss_row], axis=0)
    small = jnp.pad(small, ((0, SMALL_ROWS - small.shape[0]), (0, 0)))
    return loss[0, 0], dx, bigs, small


def kernel(x, norm1_g, w_in, w_fg2, b_fg, gla_norm_g, w_oa, conv_w, conv_b, w_ob, w_o, norm2_g, w_ffn_gate, w_ffn_up, w_ffn_down, final_g, loss_target, m_norm1_g, m_w_in, m_w_fg2, m_b_fg, m_gla_norm_g, m_w_oa, m_conv_w, m_conv_b, m_w_ob, m_w_o, m_norm2_g, m_w_ffn_gate, m_w_ffn_up, m_w_ffn_down, m_final_g, v_norm1_g, v_w_in, v_w_fg2, v_b_fg, v_gla_norm_g, v_w_oa, v_conv_w, v_conv_b, v_w_ob, v_w_o, v_norm2_g, v_w_ffn_gate, v_w_ffn_up, v_w_ffn_down, v_final_g):
    names = ["norm1_g", "w_in", "w_fg2", "b_fg", "gla_norm_g", "w_oa", "conv_w", "conv_b", "w_ob", "w_o",
             "norm2_g", "w_ffn_gate", "w_ffn_up", "w_ffn_down", "final_g"]
    w = dict(zip(names, [norm1_g, w_in, w_fg2, b_fg, gla_norm_g, w_oa, conv_w, conv_b, w_ob, w_o, norm2_g,
                         w_ffn_gate, w_ffn_up, w_ffn_down, final_g]))
    m = dict(zip(names, [m_norm1_g, m_w_in, m_w_fg2, m_b_fg, m_gla_norm_g, m_w_oa, m_conv_w, m_conv_b, m_w_ob,
                         m_w_o, m_norm2_g, m_w_ffn_gate, m_w_ffn_up, m_w_ffn_down, m_final_g]))
    v = dict(zip(names, [v_norm1_g, v_w_in, v_w_fg2, v_b_fg, v_gla_norm_g, v_w_oa, v_conv_w, v_conv_b, v_w_ob,
                         v_w_o, v_norm2_g, v_w_ffn_gate, v_w_ffn_up, v_w_ffn_down, v_final_g]))
    col_names = ["w_in", "w_ffn_gate", "w_ffn_up"]
    cx, cy, cc = _me()
    prm = jnp.asarray(SHIFT_TABLE)[4 * cx + 2 * cy + cc]
    core = jnp.reshape(cc, (1,)).astype(jnp.int32)
    chip = jnp.reshape(2 * cx + cy, (1,)).astype(jnp.int32)

    ids = iter(range(32))

    gathered, previous = [], None
    for l in range(DEPTH):
        rows, tail = _pack_rows(w_oa, w_ob, w_o, w_ffn_down, w_fg2, conv_w, l)
        win_in, win_gu = _window_cols(prm, w_in, w_ffn_gate, w_ffn_up, l, f"l{l}_windows", after=previous)
        previous = rows
        first = _all_gather([win_in, tail], f"l{l}_gather_in", next(ids))
        gathered.append(list(first) + list(_all_gather([win_gu, rows], f"l{l}_gather_rest", next(ids))))

    def weights_of(l, x_in):
        all_in, all_tail, all_gu, all_rows = gathered[l]
        after = (x_in,) if l > 0 else ()
        w_in_full, tail = _assemble_in(all_in, all_tail, f"l{l}_assemble_in", after)
        w_fg2_full, conv_w_full = _unpack_tail(tail)

        def rest(after_rest):
            names_rest = ("w_gu", "w_oa", "w_ob", "w_o", "w_down")
            return dict(zip(names_rest, _assemble_rest(all_gu, all_rows, f"l{l}_assemble_rest", after + after_rest)))

        return dict(w_in=w_in_full, rest=rest, w_fg2=w_fg2_full,
                    conv_w=conv_w_full, norm1_g=norm1_g[l][None], b_fg=b_fg[l][None],
                    gla_norm_g=gla_norm_g[l][None], conv_b=conv_b[l][None], norm2_g=norm2_g[l][None])

    pending = [dict() for _ in range(DEPTH)]
    landed = []

    def reduce_of(l):
        def reduce(group, grads):
            tag = f"l{l}_{group}"
            if group == "gu":
                packed = _grad_windows_gu(grads, tag + "_windows")
            elif group == "in":
                packed = _grad_windows_in(grads, tag + "_windows")
            else:
                packed = _grad_rows(grads)
            (from_sibling,) = _send_to_sibling([packed], tag + "_to_sibling", next(ids))
            waited = () if group == "in" else tuple(landed)
            if waited:
                landed.clear()
            sums = _pair_sum(packed, from_sibling, core, tag + "_pair_sum", after=waited)
            (from_chips,) = _send_to_chips([sums], tag + "_to_chips", next(ids))
            landed.append(from_chips)
            pending[l][group] = (sums, from_chips)
            return (packed,), (sums,)
        return reduce

    loss, dx, bigs, small = _local_step(x[0], loss_target[0], weights_of, final_g[None], reduce_of)

    grads, deltas, new_m, new_v = {}, {}, {}, {}
    for kind, group_names in (("gu", col_names[1:]), ("in", col_names[:1])):
        params = [t for n in group_names for t in (w[n], m[n], v[n])]
        out = None
        for l in reversed(range(DEPTH)):
            sums, from_chips = pending[l][kind]
            out = _final_windows(prm, chip, sums, from_chips, params, l, kind, f"l{l}_{kind}_final", out)
        for n, (g, d, nm, nv) in zip(group_names, out):
            grads[n], deltas[n], new_m[n], new_v[n] = g, d, nm, nv
    row_grads = [_ungrad_rows(_chip_sum(*pending[l]["rows"], chip, f"l{l}_rows_chip_sum")) for l in range(DEPTH)]
    for n in row_grads[0]:
        grads[n] = jnp.stack([row_grads[l][n] for l in range(DEPTH)])

    small_sum = _sum_devices(_all_gather([small], "gather_small")[0], "sum_small")
    r512, r256 = slice(0, KEY_W), slice(0, HEAD_V)
    grads.update(
        norm1_g=jnp.stack([small_sum[0], small_sum[5]]), b_fg=jnp.stack([small_sum[1, r512], small_sum[6, r512]]),
        gla_norm_g=jnp.stack([small_sum[2, r256], small_sum[7, r256]]),
        conv_b=jnp.stack([small_sum[3], small_sum[8]]), norm2_g=jnp.stack([small_sum[4], small_sum[9]]),
        final_g=small_sum[10])

    for n in names:
        if n not in col_names:
            deltas[n], new_m[n], new_v[n] = _adamw(w[n], grads[n], m[n], v[n], "adamw_" + n)

    total_loss = small_sum[2 * 5 + 1, 0]
    return (total_loss, dx[None], *[grads[n] for n in names], *[deltas[n] for n in names],
            *[new_m[n] for n in names], *[new_v[n] for n in names])
```

```python
import functools

import jax
import jax.numpy as jnp
import numpy as np
from jax import lax
from jax.experimental import pallas as pl
from jax.experimental.pallas import tpu as pltpu
from jax.experimental.pallas import tpu_sc as plsc

F32 = jnp.float32
BF16 = jnp.bfloat16
MESH = pl.DeviceIdType.MESH

D_MODEL = 1024
DEPTH = 2
CHUNK = 64
HEADS = 4
HEAD_K = 128
HEAD_V = 256
KEY_W = HEADS * HEAD_K
VAL_W = HEADS * HEAD_V
RANK = 16
TAU = 16.0
FFN = 2816
IN_WIDTH = 2 * KEY_W + 2 * VAL_W + RANK + 5 * D_MODEL
EPS = 1e-6
Q_SCALE = HEAD_K ** -0.5
N_DEV = 8
ADAM_LR, ADAM_B1, ADAM_B2, ADAM_EPS, ADAM_WD, ADAM_STEP = 0.001, 0.9, 0.999, 1e-08, 0.01, 10

LANES = 128
SUBLANES_BF16 = 16
VMEM_LIMIT = 48 * 1024 * 1024
VMEM_LIMIT_WIDE = 56 * 1024 * 1024

FZ_COL = 2 * KEY_W + 2 * VAL_W + 5 * D_MODEL
PROJ_W = FZ_COL + LANES
SEG_R, SEG_GBI, SEG_GCI, SEG_CX, SEG_GA, SEG_GB = 2, 3, 4, 5, 6, 7

W_IN_SHARD = IN_WIDTH // N_DEV
FFN_SHARD = FFN // N_DEV
ROW_SHARD = D_MODEL // N_DEV

WIN_IN = 9 * LANES
WIN_GU = 4 * LANES
A_FZ = WIN_IN
IN_COLS = A_FZ + LANES
GU_COLS = 2 * WIN_GU
ORIG_FZ = 2 * KEY_W + 2 * VAL_W


def _new_col(o):
    if o < ORIG_FZ:
        return o
    if o < ORIG_FZ + RANK:
        return FZ_COL + (o - ORIG_FZ)
    return o - RANK


def _shift_table():
    t_in, rows = [], []
    for j in range(N_DEV):
        new = [_new_col(W_IN_SHARD * j + i) for i in range(W_IN_SHARD)]
        main = [i for i in range(W_IN_SHARD) if new[i] < FZ_COL]
        code = [i for i in range(W_IN_SHARD) if new[i] >= FZ_COL]
        shift = new[main[0]] - main[0]
        t_in.append(shift // LANES)
        assert all(new[i] - i == shift for i in main) and shift % LANES + W_IN_SHARD <= WIN_IN
        if code:
            cshift = new[code[0]] - FZ_COL - code[0]
            crow = [cshift % LANES, code[0], code[-1] + 1, int(cshift < 0)]
        else:
            crow = [0, 0, 0, 0]
        rows.append([shift % LANES, main[0], main[-1] + 1] + crow + [FFN_SHARD * j % LANES])
    return tuple(t_in), np.asarray(rows, np.int32)


T_IN, SHIFT_TABLE = _shift_table()
T_GU = tuple(FFN_SHARD * j // LANES for j in range(N_DEV))

B_OA, B_OB, B_O, B_DOWN = 0, ROW_SHARD, 2 * ROW_SHARD, 3 * ROW_SHARD
B_FG2 = B_DOWN + FFN_SHARD
B_CONV = B_FG2 + 1
B_ROWS = B_FG2 + SUBLANES_BF16
SMALL_ROWS = 32


def _pick(n, candidates):
    for c in candidates:
        if n % c == 0:
            return c
    return n


def _cparams(sem):
    return pltpu.CompilerParams(dimension_semantics=sem, vmem_limit_bytes=VMEM_LIMIT)


def _sigmoid(x):
    return 1.0 / (1.0 + jnp.exp(-x))


def _matmul(a, b, dims, out_dtype, name, residual=None, after=(), into=None, norm_g=None):
    if dims == "nn":
        (m, k), (k2, n) = a.shape, b.shape
    elif dims == "nt":
        (m, k), (n, k2) = a.shape, b.shape
    else:
        (k, m), (k2, n) = a.shape, b.shape
    assert k == k2, (a.shape, b.shape, dims)
    tm = _pick(m, (1024, 1408, 512, 256, 128))
    tn = _pick(n, (1664, 1408, 1024, 512, 256, 128))
    tk = _pick(k, (1664, 1408, 1024, 512, 256, 128))
    nk = k // tk
    if dims == "nn":
        a_spec = pl.BlockSpec((tm, tk), lambda i, j, kk: (i, kk))
        b_spec = pl.BlockSpec((tk, tn), lambda i, j, kk: (kk, j))
        contract = (((1,), (0,)), ((), ()))
    elif dims == "nt":
        a_spec = pl.BlockSpec((tm, tk), lambda i, j, kk: (i, kk))
        b_spec = pl.BlockSpec((tn, tk), lambda i, j, kk: (j, kk))
        contract = (((1,), (1,)), ((), ()))
    else:
        a_spec = pl.BlockSpec((tk, tm), lambda i, j, kk: (kk, i))
        b_spec = pl.BlockSpec((tk, tn), lambda i, j, kk: (kk, j))
        contract = (((0,), (0,)), ((), ()))
    o_spec = pl.BlockSpec((tm, tn), lambda i, j, kk: (i, j))
    has_res = residual is not None
    out_spec, out_struct, placed, aliases = o_spec, jax.ShapeDtypeStruct((m, n), out_dtype), (), {}
    if into is not None:
        buffer, col = into
        assert col % tn == 0 and buffer.dtype == out_dtype and not has_res
        out_spec = pl.BlockSpec((tm, tn), lambda i, j, kk: (i, col // tn + j))
        out_struct, placed, aliases = jax.ShapeDtypeStruct(buffer.shape, out_dtype), (buffer,), {2 + len(after): 0}

    has_norm = norm_g is not None
    assert not has_norm or (tn == n and into is None)

    def body(*refs):
        a_ref, b_ref = refs[:2]
        r_ref = refs[2] if has_res else None
        n_in = 2 + has_res + has_norm + len(after) + len(placed)
        o_ref = refs[n_in]
        kk = pl.program_id(2)
        part = lax.dot_general(a_ref[...], b_ref[...], contract, preferred_element_type=F32)

        def finish(total):
            if has_res:
                total = total + r_ref[...]
            o_ref[...] = total.astype(o_ref.dtype)
            if has_norm:
                rstd = lax.rsqrt(jnp.mean(total * total, axis=-1, keepdims=True) + EPS)
                refs[n_in + 1][...] = (total * rstd * refs[2 + has_res][...]).astype(BF16)

        if nk == 1:
            finish(part)
            return
        acc_ref = refs[-1]

        @pl.when(kk == 0)
        def _():
            acc_ref[...] = part

        @pl.when((kk > 0) & (kk < nk - 1))
        def _():
            acc_ref[...] += part

        @pl.when(kk == nk - 1)
        def _():
            finish(acc_ref[...] + part)

    in_specs = [a_spec, b_spec] + ([o_spec] if has_res else [])
    in_specs += [pl.BlockSpec((1, n), lambda i, j, kk: (0, 0))] if has_norm else []
    in_specs += [_after_spec(t) for t in after] + [_ANY] * len(placed)
    args = (a, b) + ((residual,) if has_res else ()) + ((norm_g,) if has_norm else ()) + tuple(after) + placed
    if has_norm:
        out_spec, out_struct = [out_spec, o_spec], [out_struct, jax.ShapeDtypeStruct((m, n), BF16)]
    return pl.pallas_call(
        body, grid=(m // tm, n // tn, nk), in_specs=in_specs, out_specs=out_spec,
        out_shape=out_struct, input_output_aliases=aliases,
        scratch_shapes=[pltpu.VMEM((tm, tn), F32)] if nk > 1 else [], name=name,
        compiler_params=_cparams(("parallel", "parallel", "arbitrary")),
    )(*args)


def _rmsnorm_fwd(x, g, name):
    s, d = x.shape
    tm = _pick(s, (512, 256))

    def body(x_ref, g_ref, o_ref):
        xv = x_ref[...]
        r = lax.rsqrt(jnp.mean(xv * xv, axis=-1, keepdims=True) + EPS)
        o_ref[...] = (xv * r * g_ref[...]).astype(o_ref.dtype)

    row = pl.BlockSpec((tm, d), lambda i: (i, 0))
    return pl.pallas_call(
        body, grid=(s // tm,), in_specs=[row, pl.BlockSpec((1, d), lambda i: (0, 0))], out_specs=row,
        out_shape=jax.ShapeDtypeStruct((s, d), BF16), name=name, compiler_params=_cparams(("parallel",)),
    )(x, g)


def _matmul_norm_bwd(a, b, x, g, dres, name, after=()):
    (s, k), (d, k2) = a.shape, b.shape
    assert k == k2 and x.shape == (s, d)
    tm = _pick(s, (1024, 512, 256))
    tk = _pick(k, (1664, 1408, 1024, 512, 256, 128))
    nk = k // tk

    def body(a_ref, b_ref, x_ref, g_ref, dres_ref, *rest):
        dx_ref, dx16_ref, dg_ref, acc_ref = rest[len(after):]
        i, kk = pl.program_id(0), pl.program_id(1)
        part = lax.dot_general(a_ref[...], b_ref[...], _NT, preferred_element_type=F32)

        def finish(dh):
            xv = x_ref[...]
            r = lax.rsqrt(jnp.mean(xv * xv, axis=-1, keepdims=True) + EPS)
            xn = xv * r
            dxn = dh * g_ref[...]
            dx = dres_ref[...] + r * (dxn - xn * jnp.mean(dxn * xn, axis=-1, keepdims=True))
            dx_ref[...] = dx
            dx16_ref[...] = dx.astype(BF16)
            dg = jnp.broadcast_to(jnp.sum(dh * xn, axis=0, keepdims=True), dg_ref.shape)

            @pl.when(i == 0)
            def _():
                dg_ref[...] = dg

            @pl.when(i > 0)
            def _():
                dg_ref[...] += dg

        @pl.when(kk == 0)
        def _():
            acc_ref[...] = part

        @pl.when((kk > 0) & (kk < nk - 1))
        def _():
            acc_ref[...] += part

        @pl.when(kk == nk - 1)
        def _():
            finish(acc_ref[...] + part)

    assert nk > 1
    row = pl.BlockSpec((tm, d), lambda i, kk: (i, 0))
    return pl.pallas_call(
        body, grid=(s // tm, nk),
        in_specs=[pl.BlockSpec((tm, tk), lambda i, kk: (i, kk)), pl.BlockSpec((d, tk), lambda i, kk: (0, kk)), row,
                  pl.BlockSpec((1, d), lambda i, kk: (0, 0)), row] + [_after_spec(t) for t in after],
        out_specs=[row, row, pl.BlockSpec((8, d), lambda i, kk: (0, 0))],
        out_shape=[jax.ShapeDtypeStruct((s, d), F32), jax.ShapeDtypeStruct((s, d), BF16),
                   jax.ShapeDtypeStruct((8, d), F32)],
        scratch_shapes=[pltpu.VMEM((tm, d), F32)], name=name,
        compiler_params=pltpu.CompilerParams(dimension_semantics=("arbitrary", "arbitrary"),
                                             vmem_limit_bytes=VMEM_LIMIT_WIDE),
    )(a, b, x, g, dres, *after)


def _loss_head(x, g, target, name):
    s, d = x.shape
    tm = _pick(s, (512, 256))

    def body(x_ref, g_ref, t_ref, loss_ref, dx_ref, dx16_ref, dg_ref):
        xv = x_ref[...]
        gv = g_ref[...]
        r = lax.rsqrt(jnp.mean(xv * xv, axis=-1, keepdims=True) + EPS)
        xn = xv * r
        err = xn * gv - t_ref[...]
        dy = err * (1.0 / d)
        dxn = dy * gv
        dx = r * (dxn - xn * jnp.mean(dxn * xn, axis=-1, keepdims=True))
        dx_ref[...] = dx
        dx16_ref[...] = dx.astype(BF16)

        @pl.when(pl.program_id(0) == 0)
        def _():
            dg_ref[...] = jnp.zeros_like(dg_ref)
            loss_ref[...] = jnp.zeros_like(loss_ref)

        dg_ref[...] += jnp.broadcast_to(jnp.sum(dy * xn, axis=0, keepdims=True), dg_ref.shape)
        row_loss = jnp.sum(err * err, axis=-1, keepdims=True)
        loss_ref[...] += jnp.broadcast_to((0.5 / d) * jnp.sum(row_loss, axis=0, keepdims=True), loss_ref.shape)

    row = pl.BlockSpec((tm, d), lambda i: (i, 0))
    return pl.pallas_call(
        body, grid=(s // tm,), in_specs=[row, pl.BlockSpec((1, d), lambda i: (0, 0)), row],
        out_specs=[pl.BlockSpec((8, LANES), lambda i: (0, 0)), row, row, pl.BlockSpec((8, d), lambda i: (0, 0))],
        out_shape=[jax.ShapeDtypeStruct((8, LANES), F32), jax.ShapeDtypeStruct((s, d), F32),
                   jax.ShapeDtypeStruct((s, d), BF16), jax.ShapeDtypeStruct((8, d), F32)],
        name=name, compiler_params=_cparams(("arbitrary",)),
    )(x, g, target)


def _tri_dot(tri, x):
    hi = x.astype(BF16)
    lo = (x - hi.astype(F32)).astype(BF16)
    return jnp.dot(tri, hi, preferred_element_type=F32) + jnp.dot(tri, lo, preferred_element_type=F32)


def _chunk_masks(rows):
    r, c = np.arange(rows)[:, None], np.arange(rows)[None, :]
    same = (r // CHUNK) == (c // CHUNK)
    return [jnp.asarray(m, BF16) for m in (same & (r >= c), same & (r > c), same)]


def _block_decay(fz, w, b, incl, ones=None):
    fg = jnp.dot(fz, w, preferred_element_type=F32) + b
    la = (jnp.minimum(fg, 0.0) - jnp.log(1.0 + jnp.exp(-jnp.abs(fg)))) * (1.0 / TAU)
    cum = _tri_dot(incl, la)
    ends = [cum[i + CHUNK - 1:i + CHUNK, :] for i in range(0, fz.shape[0], CHUNK)]
    if ones is None:
        end = jnp.concatenate([jnp.broadcast_to(e, (CHUNK, e.shape[1])) for e in ends], axis=0)
    else:
        end = _tri_dot(ones, la)
    return fg, jnp.exp(end - cum), [jnp.exp(e) for e in ends]


_TN = (((0,), (0,)), ((), ()))
_NT = (((1,), (1,)), ((), ()))


def _gla_specs(rows):
    q_spec = pl.BlockSpec((rows, HEAD_K), lambda h, c: (c, h))
    k_spec = pl.BlockSpec((rows, HEAD_K), lambda h, c: (c, HEADS + h))
    v_spec = pl.BlockSpec((rows, HEAD_V), lambda h, c: (c, HEADS + h))
    fz_spec = pl.BlockSpec((rows, LANES), lambda h, c: (c, FZ_COL // LANES))
    w_spec = pl.BlockSpec((LANES, HEAD_K), lambda h, c: (0, h))
    b_spec = pl.BlockSpec((1, HEAD_K), lambda h, c: (0, h))
    mask_spec = pl.BlockSpec((rows, rows), lambda h, c: (0, 0))
    return q_spec, k_spec, v_spec, fz_spec, w_spec, b_spec, mask_spec


def _gla_fwd(proj, wfg, bfg, name):
    s = proj.shape[0]
    nc = s // CHUNK
    per = _pick(nc, (8, 4, 2, 1))
    rows = per * CHUNK
    incl, _, ones = _chunk_masks(rows)

    def body(q_ref, k_ref, v_ref, fz_ref, w_ref, b_ref, incl_ref, ones_ref, o_ref, st_ref, state, update):
        @pl.when(pl.program_id(1) == 0)
        def _():
            state[...] = jnp.zeros_like(state)

        _, dec, gammas = _block_decay(fz_ref[...], w_ref[...], b_ref[...], incl_ref[...], ones_ref[...])
        kd = (k_ref[...].astype(F32) * dec).astype(BF16)
        qs = (q_ref[...].astype(F32) * Q_SCALE).astype(BF16)
        for i in range(per):
            sl = slice(i * CHUNK, (i + 1) * CHUNK)
            update[i] = lax.dot_general(v_ref[sl, :], kd[sl], _TN, preferred_element_type=F32)
        st = state[...]
        for i in range(per):
            st = st * gammas[i] + update[i]
            st_ref[0, i] = st.astype(BF16)
        state[...] = st
        for i in range(per):
            sl = slice(i * CHUNK, (i + 1) * CHUNK)
            o_ref[sl, :] = lax.dot_general(qs[sl], st_ref[0, i], _NT, preferred_element_type=F32).astype(o_ref.dtype)

    q_spec, k_spec, v_spec, fz_spec, w_spec, b_spec, mask_spec = _gla_specs(rows)
    return pl.pallas_call(
        body, grid=(HEADS, nc // per),
        in_specs=[q_spec, k_spec, v_spec, fz_spec, w_spec, b_spec, mask_spec, mask_spec],
        out_specs=[pl.BlockSpec((rows, HEAD_V), lambda h, c: (c, h)),
                   pl.BlockSpec((1, per, HEAD_V, HEAD_K), lambda h, c: (h, c, 0, 0))],
        out_shape=[jax.ShapeDtypeStruct((s, VAL_W), BF16),
                   jax.ShapeDtypeStruct((HEADS, nc, HEAD_V, HEAD_K), BF16)],
        scratch_shapes=[pltpu.VMEM((HEAD_V, HEAD_K), F32), pltpu.VMEM((per, HEAD_V, HEAD_K), F32)], name=name,
        compiler_params=_cparams(("parallel", "arbitrary")),
    )(proj, proj, proj, proj, wfg, bfg, incl, ones)


def _gla_bwd(proj, wfg, bfg, states, do, name):
    s = proj.shape[0]
    nc = s // CHUNK
    per = _pick(nc, (8, 4, 2, 1))
    rows = per * CHUNK
    nblk = nc // per
    incl, strict, _ = _chunk_masks(rows)

    def rev(spec_fn):
        return lambda h, j: spec_fn(h, nblk - 1 - j)

    def body(q_ref, k_ref, v_ref, fz_ref, w_ref, b_ref, incl_ref, strict_ref, do_ref, st_ref, prev_ref,
             dq_ref, dk_ref, dv_ref, dfg_ref, db_ref, carry, gt_all, dkd_all, dgg_all):
        j = pl.program_id(1)

        @pl.when(j == 0)
        def _():
            carry[...] = jnp.zeros_like(carry)
            db_ref[...] = jnp.zeros_like(db_ref)

        fg, dec, gammas = _block_decay(fz_ref[...], w_ref[...], b_ref[...], incl_ref[...])
        kd = k_ref[...].astype(F32) * dec
        kd16 = kd.astype(BF16)
        qs = (q_ref[...].astype(F32) * Q_SCALE).astype(BF16)
        for i in range(per):
            sl = slice(i * CHUNK, (i + 1) * CHUNK)
            gt_all[i] = lax.dot_general(do_ref[sl, :], qs[sl], _TN, preferred_element_type=F32)
        back = carry[...]
        for i in reversed(range(per)):
            gt = back + gt_all[i]
            gt_all[i] = gt
            back = gt * gammas[i]
        carry[...] = back
        has_prev = (j < nblk - 1).astype(F32)
        for i in range(per):
            sl = slice(i * CHUNK, (i + 1) * CHUNK)
            gt = gt_all[i]
            gt16 = gt.astype(BF16)
            dov = do_ref[sl, :]
            dq_ref[sl, :] = (jnp.dot(dov, st_ref[0, i], preferred_element_type=F32) * Q_SCALE).astype(dq_ref.dtype)
            dkd_all[sl, :] = jnp.dot(v_ref[sl, :], gt16, preferred_element_type=F32)
            dv_ref[sl, :] = lax.dot_general(kd16[sl], gt16, _NT, preferred_element_type=F32).astype(dv_ref.dtype)
            if i > 0:
                st_prev = st_ref[0, i - 1].astype(F32)
            else:
                st_prev = prev_ref[0, 0].astype(F32) * has_prev
            dgamma = jnp.sum(gt * st_prev, axis=0, keepdims=True)
            dgg_all[sl, :] = jnp.broadcast_to(dgamma * gammas[i], (CHUNK, HEAD_K))
        dkd = dkd_all[...]
        dk_ref[...] = (dkd * dec).astype(dk_ref.dtype)
        dla = dgg_all[...] + _tri_dot(strict_ref[...], dkd * kd)
        dfg = dla * (1.0 / TAU) * _sigmoid(-fg)
        dfg_ref[...] = dfg.astype(dfg_ref.dtype)
        db_ref[...] += jnp.broadcast_to(jnp.sum(dfg, axis=0, keepdims=True), db_ref.shape)

    q_spec, k_spec, v_spec, fz_spec, w_spec, b_spec, mask_spec = _gla_specs(rows)
    q_spec, k_spec, v_spec, fz_spec = [
        pl.BlockSpec(sp.block_shape, rev(sp.index_map)) for sp in (q_spec, k_spec, v_spec, fz_spec)]
    do_spec = pl.BlockSpec((rows, HEAD_V), lambda h, j: (nblk - 1 - j, h))
    st_spec = pl.BlockSpec((1, per, HEAD_V, HEAD_K), lambda h, j: (h, nblk - 1 - j, 0, 0))
    prev_spec = pl.BlockSpec((1, 1, HEAD_V, HEAD_K),
                             lambda h, j: (h, jnp.maximum((nblk - 1 - j) * per - 1, 0), 0, 0))
    key_out = pl.BlockSpec((rows, HEAD_K), lambda h, j: (nblk - 1 - j, h))
    return pl.pallas_call(
        body, grid=(HEADS, nblk),
        in_specs=[q_spec, k_spec, v_spec, fz_spec, w_spec, b_spec, mask_spec, mask_spec,
                  do_spec, st_spec, prev_spec],
        out_specs=[key_out, key_out, do_spec, key_out, pl.BlockSpec((8, HEAD_K), lambda h, j: (0, h))],
        out_shape=[jax.ShapeDtypeStruct((s, KEY_W), BF16), jax.ShapeDtypeStruct((s, KEY_W), BF16),
                   jax.ShapeDtypeStruct((s, VAL_W), BF16), jax.ShapeDtypeStruct((s, KEY_W), BF16),
                   jax.ShapeDtypeStruct((8, KEY_W), F32)],
        scratch_shapes=[pltpu.VMEM((HEAD_V, HEAD_K), F32), pltpu.VMEM((per, HEAD_V, HEAD_K), F32),
                        pltpu.VMEM((rows, HEAD_K), F32), pltpu.VMEM((rows, HEAD_K), F32)], name=name,
        compiler_params=_cparams(("parallel", "arbitrary")),
    )(proj, proj, proj, proj, wfg, bfg, incl, strict, do, states, states)


def _place_qkv(dq, dk, dv, dproj, name):
    s = dq.shape[0]
    tm = _pick(s, (512, 256))

    def body(dq_ref, dk_ref, dv_ref, _, o_ref):
        o_ref[:, 0:KEY_W] = dq_ref[...]
        o_ref[:, KEY_W:2 * KEY_W] = dk_ref[...]
        o_ref[:, 2 * KEY_W:2 * KEY_W + VAL_W] = dv_ref[...]

    def rows(width):
        return pl.BlockSpec((tm, width), lambda i: (i, 0))

    return pl.pallas_call(
        body, grid=(s // tm,), in_specs=[rows(KEY_W), rows(KEY_W), rows(VAL_W), _ANY],
        out_specs=rows(2 * KEY_W + VAL_W), out_shape=jax.ShapeDtypeStruct(dproj.shape, BF16),
        input_output_aliases={3: 0}, name=name, compiler_params=_cparams(("parallel",)),
    )(dq, dk, dv, dproj)


def _seg(tm, seg):
    return pl.BlockSpec((tm, D_MODEL), lambda i: (i, seg))


def _gla_post_fwd(o, proj, g, name):
    s = o.shape[0]
    tm = _pick(s, (512, 256))

    def body(o_ref, r_ref, g_ref, oa_ref):
        gv = g_ref[...]
        for h in range(HEADS):
            sl = slice(h * HEAD_V, (h + 1) * HEAD_V)
            ov = o_ref[:, sl].astype(F32)
            rstd = lax.rsqrt(jnp.mean(ov * ov, axis=-1, keepdims=True) + EPS)
            rv = r_ref[:, sl].astype(F32)
            oa_ref[:, sl] = (ov * rstd * gv * (rv * _sigmoid(rv))).astype(oa_ref.dtype)

    row = pl.BlockSpec((tm, VAL_W), lambda i: (i, 0))
    return pl.pallas_call(
        body, grid=(s // tm,), in_specs=[row, _seg(tm, SEG_R), pl.BlockSpec((1, HEAD_V), lambda i: (0, 0))],
        out_specs=row, out_shape=jax.ShapeDtypeStruct((s, VAL_W), BF16), name=name,
        compiler_params=_cparams(("parallel",)),
    )(o, proj, g)


def _gla_post_bwd(doa, o, proj, g, dproj, name):
    s = o.shape[0]
    tm = _pick(s, (512, 256))

    def body(doa_ref, o_ref, r_ref, g_ref, _, dr_ref, do_ref, dg_ref):
        @pl.when(pl.program_id(0) == 0)
        def _():
            dg_ref[...] = jnp.zeros_like(dg_ref)

        gv = g_ref[...]
        dg = jnp.zeros((1, HEAD_V), F32)
        for h in range(HEADS):
            sl = slice(h * HEAD_V, (h + 1) * HEAD_V)
            ov = o_ref[:, sl].astype(F32)
            rstd = lax.rsqrt(jnp.mean(ov * ov, axis=-1, keepdims=True) + EPS)
            ohat = ov * rstd
            rv = r_ref[:, sl].astype(F32)
            sg = _sigmoid(rv)
            dv = doa_ref[:, sl].astype(F32)
            dr_ref[:, sl] = (dv * ohat * gv * (sg * (1.0 + rv * (1.0 - sg)))).astype(dr_ref.dtype)
            don = dv * (rv * sg)
            dg = dg + jnp.sum(don * ohat, axis=0, keepdims=True)
            dohat = don * gv
            do_ref[:, sl] = (rstd * (dohat - ohat * jnp.mean(dohat * ohat, axis=-1, keepdims=True))
                             ).astype(do_ref.dtype)
        dg_ref[...] += jnp.broadcast_to(dg, dg_ref.shape)

    row = pl.BlockSpec((tm, VAL_W), lambda i: (i, 0))
    return pl.pallas_call(
        body, grid=(s // tm,),
        in_specs=[row, row, _seg(tm, SEG_R), pl.BlockSpec((1, HEAD_V), lambda i: (0, 0)), _ANY],
        out_specs=[_seg(tm, SEG_R), row, pl.BlockSpec((8, HEAD_V), lambda i: (0, 0))],
        out_shape=[jax.ShapeDtypeStruct(dproj.shape, BF16), jax.ShapeDtypeStruct((s, VAL_W), BF16),
                   jax.ShapeDtypeStruct((8, HEAD_V), F32)],
        input_output_aliases={4: 0}, name=name, compiler_params=_cparams(("arbitrary",)),
    )(doa, o, proj, g, dproj)


HALO = SUBLANES_BF16


def _shift_down(u, p1, p2, n, rows):
    rolled = pltpu.roll(u, n, 0)
    if n == 1:
        return jnp.where(rows == 0, p1, rolled)
    return jnp.where(rows == 0, p2, jnp.where(rows == 1, p1, rolled))


def _shift_up(u, n1, n2, n, rows, tm):
    rolled = pltpu.roll(u, tm - n, 0)
    if n == 1:
        return jnp.where(rows == tm - 1, n1, rolled)
    return jnp.where(rows == tm - 2, n1, jnp.where(rows == tm - 1, n2, rolled))


def _conv_terms(gc_ref, cx_ref, gcp_ref, cxp_ref, tm):
    i = pl.program_id(0)
    u = gc_ref[...].astype(F32) * cx_ref[...].astype(F32)
    up = gcp_ref[...].astype(F32) * cxp_ref[...].astype(F32) * (i > 0).astype(F32)
    rows = lax.broadcasted_iota(jnp.int32, (tm, 1), 0)
    u1 = _shift_down(u, up[HALO - 1:HALO, :], up[HALO - 2:HALO - 1, :], 1, rows)
    u2 = _shift_down(u, up[HALO - 1:HALO, :], up[HALO - 2:HALO - 1, :], 2, rows)
    return u, u1, u2, rows


def _prev_halo(tm, seg):
    return pl.BlockSpec((HALO, D_MODEL), lambda i: (jnp.maximum(i * (tm // HALO) - 1, 0), seg))


def _conv_fwd(proj, w, b, name):
    s = proj.shape[0]
    tm = _pick(s, (512, 256))

    def body(gbi_ref, gc_ref, cx_ref, gcp_ref, cxp_ref, w_ref, b_ref, cb_ref):
        u, u1, u2, _ = _conv_terms(gc_ref, cx_ref, gcp_ref, cxp_ref, tm)
        conv = w_ref[0:1, :] * u2 + w_ref[1:2, :] * u1 + w_ref[2:3, :] * u + b_ref[...]
        cb_ref[...] = (gbi_ref[...].astype(F32) * conv).astype(cb_ref.dtype)

    return pl.pallas_call(
        body, grid=(s // tm,),
        in_specs=[_seg(tm, SEG_GBI), _seg(tm, SEG_GCI), _seg(tm, SEG_CX),
                  _prev_halo(tm, SEG_GCI), _prev_halo(tm, SEG_CX),
                  pl.BlockSpec((3, D_MODEL), lambda i: (0, 0)), pl.BlockSpec((1, D_MODEL), lambda i: (0, 0))],
        out_specs=pl.BlockSpec((tm, D_MODEL), lambda i: (i, 0)),
        out_shape=jax.ShapeDtypeStruct((s, D_MODEL), BF16), name=name, compiler_params=_cparams(("parallel",)),
    )(proj, proj, proj, proj, proj, w, b)


def _conv_bwd(dcb, proj, w, b, dproj, name):
    s = proj.shape[0]
    tm = _pick(s, (512, 256))
    nt = s // tm

    def body(dcb_ref, gbi_ref, gc_ref, cx_ref, gcp_ref, cxp_ref, dcbn_ref, gbin_ref, w_ref, b_ref, _,
             d3_ref, dwb_ref):
        i = pl.program_id(0)

        @pl.when(i == 0)
        def _():
            dwb_ref[...] = jnp.zeros_like(dwb_ref)

        u, u1, u2, rows = _conv_terms(gc_ref, cx_ref, gcp_ref, cxp_ref, tm)
        w0, w1, w2 = w_ref[0:1, :], w_ref[1:2, :], w_ref[2:3, :]
        conv = w0 * u2 + w1 * u1 + w2 * u + b_ref[...]
        dcbv = dcb_ref[...].astype(F32)
        gbi = gbi_ref[...].astype(F32)
        dconv = dcbv * gbi
        dnext = dcbn_ref[...].astype(F32) * gbin_ref[...].astype(F32) * (i < nt - 1).astype(F32)
        dc1 = _shift_up(dconv, dnext[0:1, :], dnext[1:2, :], 1, rows, tm)
        dc2 = _shift_up(dconv, dnext[0:1, :], dnext[1:2, :], 2, rows, tm)
        du = w2 * dconv + w1 * dc1 + w0 * dc2
        d3_ref[:, 0:D_MODEL] = (dcbv * conv).astype(d3_ref.dtype)
        d3_ref[:, D_MODEL:2 * D_MODEL] = (du * cx_ref[...].astype(F32)).astype(d3_ref.dtype)
        d3_ref[:, 2 * D_MODEL:3 * D_MODEL] = (du * gc_ref[...].astype(F32)).astype(d3_ref.dtype)
        dwb_ref[0:1, :] += jnp.sum(dconv * u2, axis=0, keepdims=True)
        dwb_ref[1:2, :] += jnp.sum(dconv * u1, axis=0, keepdims=True)
        dwb_ref[2:3, :] += jnp.sum(dconv * u, axis=0, keepdims=True)
        dwb_ref[3:4, :] += jnp.sum(dconv, axis=0, keepdims=True)

    def next_halo(seg_fn):
        return pl.BlockSpec((HALO, D_MODEL), lambda i: (jnp.minimum((i + 1) * (tm // HALO), s // HALO - 1), seg_fn))

    return pl.pallas_call(
        body, grid=(nt,),
        in_specs=[pl.BlockSpec((tm, D_MODEL), lambda i: (i, 0)),
                  _seg(tm, SEG_GBI), _seg(tm, SEG_GCI), _seg(tm, SEG_CX),
                  _prev_halo(tm, SEG_GCI), _prev_halo(tm, SEG_CX),
                  next_halo(0), next_halo(SEG_GBI),
                  pl.BlockSpec((3, D_MODEL), lambda i: (0, 0)), pl.BlockSpec((1, D_MODEL), lambda i: (0, 0)), _ANY],
        out_specs=[pl.BlockSpec((tm, 3 * D_MODEL), lambda i: (i, SEG_GBI // 3)),
                   pl.BlockSpec((8, D_MODEL), lambda i: (0, 0))],
        out_shape=[jax.ShapeDtypeStruct(dproj.shape, BF16), jax.ShapeDtypeStruct((8, D_MODEL), F32)],
        input_output_aliases={10: 0}, name=name, compiler_params=_cparams(("arbitrary",)),
    )(dcb, proj, proj, proj, proj, proj, dcb, proj, w, b, dproj)


def _mix_fwd(proj, ya, yb, name):
    s = proj.shape[0]
    tm = _pick(s, (512, 256))

    def body(ga_ref, gb_ref, ya_ref, yb_ref, o_ref):
        o_ref[...] = (_sigmoid(ga_ref[...].astype(F32)) * ya_ref[...].astype(F32)
                      + _sigmoid(gb_ref[...].astype(F32)) * yb_ref[...].astype(F32)).astype(o_ref.dtype)

    row = pl.BlockSpec((tm, D_MODEL), lambda i: (i, 0))
    return pl.pallas_call(
        body, grid=(s // tm,), in_specs=[_seg(tm, SEG_GA), _seg(tm, SEG_GB), row, row], out_specs=row,
        out_shape=jax.ShapeDtypeStruct((s, D_MODEL), BF16), name=name, compiler_params=_cparams(("parallel",)),
    )(proj, proj, ya, yb)


def _mix_bwd(dmix, proj, ya, yb, name):
    s = proj.shape[0]
    tm = _pick(s, (512, 256))

    def body(dm_ref, ga_ref, gb_ref, ya_ref, yb_ref, dg_ref, dya_ref, dyb_ref):
        dm = dm_ref[...].astype(F32)
        sa = _sigmoid(ga_ref[...].astype(F32))
        sb = _sigmoid(gb_ref[...].astype(F32))
        dg_ref[:, 0:D_MODEL] = (dm * ya_ref[...].astype(F32) * sa * (1.0 - sa)).astype(dg_ref.dtype)
        dg_ref[:, D_MODEL:2 * D_MODEL] = (dm * yb_ref[...].astype(F32) * sb * (1.0 - sb)).astype(dg_ref.dtype)
        dya_ref[...] = (dm * sa).astype(dya_ref.dtype)
        dyb_ref[...] = (dm * sb).astype(dyb_ref.dtype)

    row = pl.BlockSpec((tm, D_MODEL), lambda i: (i, 0))
    return pl.pallas_call(
        body, grid=(s // tm,), in_specs=[row, _seg(tm, SEG_GA), _seg(tm, SEG_GB), row, row],
        out_specs=[pl.BlockSpec((tm, 2 * D_MODEL), lambda i: (i, SEG_GA // 2)), row, row],
        out_shape=[jax.ShapeDtypeStruct((s, PROJ_W), BF16), jax.ShapeDtypeStruct((s, D_MODEL), BF16),
                   jax.ShapeDtypeStruct((s, D_MODEL), BF16)],
        name=name, compiler_params=_cparams(("parallel",)),
    )(dmix, proj, proj, ya, yb)


def _swiglu_fwd(gu, name):
    s = gu.shape[0]
    tm = _pick(s, (256,))

    def body(gu_ref, o_ref):
        gate = gu_ref[:, 0:FFN].astype(F32)
        o_ref[...] = (gate * _sigmoid(gate) * gu_ref[:, FFN:2 * FFN].astype(F32)).astype(o_ref.dtype)

    return pl.pallas_call(
        body, grid=(s // tm,), in_specs=[pl.BlockSpec((tm, 2 * FFN), lambda i: (i, 0))],
        out_specs=pl.BlockSpec((tm, FFN), lambda i: (i, 0)),
        out_shape=jax.ShapeDtypeStruct((s, FFN), BF16), name=name, compiler_params=_cparams(("parallel",)),
    )(gu)


def _swiglu_bwd(dhid, gu, name):
    s = gu.shape[0]
    tm = _pick(s, (256,))

    def body(dh_ref, gu_ref, o_ref):
        gate = gu_ref[:, 0:FFN].astype(F32)
        up = gu_ref[:, FFN:2 * FFN].astype(F32)
        dh = dh_ref[...].astype(F32)
        sg = _sigmoid(gate)
        o_ref[:, 0:FFN] = (dh * up * (sg * (1.0 + gate * (1.0 - sg)))).astype(o_ref.dtype)
        o_ref[:, FFN:2 * FFN] = (dh * gate * sg).astype(o_ref.dtype)

    wide = pl.BlockSpec((tm, 2 * FFN), lambda i: (i, 0))
    return pl.pallas_call(
        body, grid=(s // tm,), in_specs=[pl.BlockSpec((tm, FFN), lambda i: (i, 0)), wide], out_specs=wide,
        out_shape=jax.ShapeDtypeStruct((s, 2 * FFN), BF16), name=name, compiler_params=_cparams(("parallel",)),
    )(dhid, gu)


def _adamw_math(w, g, m, v):
    m2 = ADAM_B1 * m + (1.0 - ADAM_B1) * g
    v2 = ADAM_B2 * v + (1.0 - ADAM_B2) * (g * g)
    m_hat = m2 / (1.0 - ADAM_B1 ** ADAM_STEP)
    v_hat = v2 / (1.0 - ADAM_B2 ** ADAM_STEP)
    delta = -ADAM_LR * (m_hat / (jnp.sqrt(v_hat) + ADAM_EPS) + ADAM_WD * w)
    return delta, m2, v2


def _adamw(w, g, m, v, name):
    shape = w.shape
    cols = shape[-1]
    rows = int(np.prod(shape[:-1])) if len(shape) > 1 else 1
    w2, g2, m2, v2 = [t.reshape(rows, cols) for t in (w, g, m, v)]
    tr = _pick(rows, (512, 352, 256)) if rows % 8 == 0 else rows

    def body(w_ref, g_ref, m_ref, v_ref, d_ref, nm_ref, nv_ref):
        d, nm, nv = _adamw_math(w_ref[...], g_ref[...], m_ref[...], v_ref[...])
        d_ref[...] = d
        nm_ref[...] = nm
        nv_ref[...] = nv

    blk = pl.BlockSpec((tr, cols), lambda i: (i, 0))
    out = pl.pallas_call(
        body, grid=(rows // tr,), in_specs=[blk] * 4, out_specs=[blk] * 3,
        out_shape=[jax.ShapeDtypeStruct((rows, cols), F32)] * 3, name=name,
        compiler_params=_cparams(("parallel",)),
    )(w2, g2, m2, v2)
    return [t.reshape(shape) for t in out]


def _pair_sum(g2, recv, core, name, after=()):
    _, nchip, r, c = g2.shape
    tr = _pick(r, (512,))

    def body(core_ref, a_ref, b_ref, *rest):
        o_ref = rest[-1]
        o_ref[...] = (a_ref[...].astype(F32) + b_ref[...].astype(F32)).astype(o_ref.dtype)

    grid_spec = pltpu.PrefetchScalarGridSpec(
        num_scalar_prefetch=1, grid=(nchip, r // tr),
        in_specs=[pl.BlockSpec((None, None, tr, c), lambda k, i, cr: (cr[0], k, i, 0)),
                  pl.BlockSpec((None, tr, c), lambda k, i, cr: (k, i, 0))] + [_after_spec(t) for t in after],
        out_specs=pl.BlockSpec((None, tr, c), lambda k, i, cr: (k, i, 0)))
    return pl.pallas_call(
        body, grid_spec=grid_spec, out_shape=jax.ShapeDtypeStruct((nchip, r, c), BF16), name=name,
        compiler_params=_cparams(("parallel", "parallel")),
    )(core, g2, recv, *after)


def _chip_sum(a, recv, chip, name, after=()):
    _, r, c = a.shape
    tr = _pick(r, (512,))

    def body(chip_ref, a_ref, b_ref, *rest):
        o_ref = rest[-1]
        o_ref[...] = ((a_ref[...].astype(F32) + b_ref[0].astype(F32)) + b_ref[1].astype(F32)) + b_ref[2].astype(F32)

    grid_spec = pltpu.PrefetchScalarGridSpec(
        num_scalar_prefetch=1, grid=(r // tr,),
        in_specs=[pl.BlockSpec((None, tr, c), lambda i, cr: (cr[0], i, 0)),
                  pl.BlockSpec((3, tr, c), lambda i, cr: (0, i, 0))] + [_after_spec(t) for t in after],
        out_specs=pl.BlockSpec((tr, c), lambda i, cr: (i, 0)))
    return pl.pallas_call(
        body, grid_spec=grid_spec, out_shape=jax.ShapeDtypeStruct((r, c), F32), name=name,
        compiler_params=_cparams(("parallel",)),
    )(chip, a, recv, *after)


def _sum_devices(parts, name):
    n, r, c = parts.shape

    def body(p_ref, o_ref):
        acc = p_ref[0]
        for d in range(1, n):
            acc = acc + p_ref[d]
        o_ref[...] = acc

    return pl.pallas_call(
        body, out_shape=jax.ShapeDtypeStruct((r, c), F32), name=name,
        in_specs=[pl.BlockSpec(memory_space=pltpu.VMEM)], out_specs=pl.BlockSpec(memory_space=pltpu.VMEM),
    )(parts)


def _lane_iota():
    return lax.broadcasted_iota(jnp.int32, (1, LANES), 1)


def _tiles_up(tiles, s, lane):
    rolled = [pltpu.roll(t, s, 1) for t in tiles]
    zero = jnp.zeros_like(tiles[0])
    return [jnp.where(lane < s, p, c) for p, c in zip([zero] + rolled, rolled + [zero])]


def _tiles_down(tiles, s, lane):
    back = (LANES - s) % LANES
    rolled = [pltpu.roll(t, back, 1) for t in tiles]
    zero = jnp.zeros_like(tiles[0])
    return [jnp.where(lane < LANES - s, c, n) for c, n in zip(rolled, rolled[1:] + [zero])]


def _window_cols(prm, w_in, w_gate, w_up, l, name, after=None):
    tm = 256
    n_in, n_gu = WIN_IN // LANES, FFN_SHARD // LANES + 1

    def body(prm_ref, win_ref, g_ref, u_ref, *rest):
        out_ref, gu_ref, scr_in, scr_gu = rest[-4:]
        lane = _lane_iota()
        s_main, lo, hi, s_code, clo, chi, code_hi, s_gu = [prm_ref[i] for i in range(8)]
        scr_in[:, D_MODEL:WIN_IN] = jnp.zeros((tm, WIN_IN - D_MODEL), F32)
        scr_in[:, 0:W_IN_SHARD] = win_ref[...]

        def keep(t, a, b):
            col = lane + t * LANES
            return jnp.where((col >= a) & (col < b), scr_in[:, t * LANES:(t + 1) * LANES], 0.0)

        main = _tiles_up([keep(t, lo, hi) for t in range(n_in)], s_main, lane)
        for t in range(n_in):
            out_ref[:, t * LANES:(t + 1) * LANES] = main[t].astype(BF16)
        low = _tiles_up([keep(0, clo, chi)], s_code, lane)[0]
        high = _tiles_up([keep(n_in - 2, clo, chi), keep(n_in - 1, clo, chi)], s_code, lane)[1]
        out_ref[:, A_FZ:A_FZ + LANES] = jnp.where(code_hi == 1, high, low).astype(BF16)
        for ref, base in ((g_ref, 0), (u_ref, WIN_GU)):
            scr_gu[:, (n_gu - 1) * LANES:n_gu * LANES] = jnp.zeros((tm, LANES), F32)
            scr_gu[:, 0:FFN_SHARD] = ref[...]
            moved = _tiles_up([scr_gu[:, t * LANES:(t + 1) * LANES] for t in range(n_gu)], s_gu, lane)
            for t in range(n_gu + 1):
                gu_ref[:, base + t * LANES:base + (t + 1) * LANES] = moved[t].astype(BF16)

    after_args = [] if after is None else [after]
    grid_spec = pltpu.PrefetchScalarGridSpec(
        num_scalar_prefetch=1, grid=(D_MODEL // tm,),
        in_specs=[pl.BlockSpec((None, tm, W_IN_SHARD), lambda i, p: (l, i, 0)),
                  pl.BlockSpec((None, tm, FFN_SHARD), lambda i, p: (l, i, 0)),
                  pl.BlockSpec((None, tm, FFN_SHARD), lambda i, p: (l, i, 0))] + [_after_spec(t) for t in after_args],
        out_specs=[pl.BlockSpec((tm, IN_COLS), lambda i, p: (i, 0)), pl.BlockSpec((tm, GU_COLS), lambda i, p: (i, 0))],
        scratch_shapes=[pltpu.VMEM((tm, WIN_IN), F32), pltpu.VMEM((tm, n_gu * LANES), F32)])
    return pl.pallas_call(
        body, grid_spec=grid_spec, name=name, compiler_params=_cparams(("parallel",)),
        out_shape=[jax.ShapeDtypeStruct((D_MODEL, IN_COLS), BF16), jax.ShapeDtypeStruct((D_MODEL, GU_COLS), BF16)],
    )(prm, w_in, w_gate, w_up, *after_args)


def _gu_width(j):
    return min(WIN_GU, FFN - T_GU[j] * LANES)


def _after_spec(t):
    tile = (SUBLANES_BF16 if t.dtype == BF16 else 8, LANES)
    return pl.BlockSpec((None,) * (t.ndim - 2) + tile, lambda *_: (0,) * t.ndim)


def _assemble_in(a_all, tail_all, name, after=()):
    tm = D_MODEL // N_DEV

    def body(a_ref, t_ref, *rest):
        win_ref, tail_ref = rest[-2:]
        tail_ref[...] = t_ref[...]
        win_ref[...] = jnp.zeros_like(win_ref)
        code = a_ref[0, :, A_FZ:A_FZ + LANES]
        for j in range(N_DEV):
            c0 = T_IN[j] * LANES
            win_ref[:, c0:c0 + WIN_IN] += a_ref[j, :, 0:WIN_IN]
            if j > 0:
                code = code + a_ref[j, :, A_FZ:A_FZ + LANES]
        win_ref[:, FZ_COL:PROJ_W] = code

    return pl.pallas_call(
        body, grid=(N_DEV,),
        in_specs=[pl.BlockSpec((N_DEV, tm, IN_COLS), lambda i: (0, i, 0)),
                  pl.BlockSpec((None,) + tail_all.shape[1:], lambda i: (i, 0, 0))] + [_after_spec(t) for t in after],
        out_specs=[pl.BlockSpec((tm, PROJ_W), lambda i: (i, 0)),
                   pl.BlockSpec((None,) + tail_all.shape[1:], lambda i: (i, 0, 0))],
        out_shape=[jax.ShapeDtypeStruct((D_MODEL, PROJ_W), BF16), jax.ShapeDtypeStruct(tail_all.shape, BF16)],
        name=name, compiler_params=_cparams(("parallel",)),
    )(a_all, tail_all, *after)


def _assemble_rest(a_all, rows_all, name, after=()):
    tm = D_MODEL // N_DEV
    n_in = 2 + len(after)

    def body(*refs):
        a_ref, r_ref = refs[:2]
        wgu_ref, oa_ref, ob_ref, o_ref, down_ref = refs[n_in:]
        wgu_ref[...] = jnp.zeros_like(wgu_ref)
        for j in range(N_DEV):
            g0, width = T_GU[j] * LANES, _gu_width(j)
            wgu_ref[:, g0:g0 + width] += a_ref[j, :, 0:width]
            wgu_ref[:, FFN + g0:FFN + g0 + width] += a_ref[j, :, WIN_GU:WIN_GU + width]
        oa_ref[...] = r_ref[B_OA:B_OA + ROW_SHARD, :]
        ob_ref[...] = r_ref[B_OB:B_OB + ROW_SHARD, :]
        o_ref[...] = r_ref[B_O:B_O + ROW_SHARD, :]
        down_ref[...] = r_ref[B_DOWN:B_DOWN + FFN_SHARD, :]

    def rows(n):
        return pl.BlockSpec((n, D_MODEL), lambda i: (i, 0))

    square = jax.ShapeDtypeStruct((D_MODEL, D_MODEL), BF16)
    return pl.pallas_call(
        body, grid=(N_DEV,),
        in_specs=[pl.BlockSpec((N_DEV, tm, GU_COLS), lambda i: (0, i, 0)),
                  pl.BlockSpec((None, B_FG2, D_MODEL), lambda i: (i, 0, 0))] + [_after_spec(t) for t in after],
        out_specs=[pl.BlockSpec((tm, 2 * FFN), lambda i: (i, 0)),
                   rows(ROW_SHARD), rows(ROW_SHARD), rows(ROW_SHARD), rows(FFN_SHARD)],
        out_shape=[jax.ShapeDtypeStruct((D_MODEL, 2 * FFN), BF16),
                   square, square, square, jax.ShapeDtypeStruct((FFN, D_MODEL), BF16)],
        name=name, compiler_params=_cparams(("parallel",)),
    )(a_all, rows_all, *after)


def _grad_windows_in(d_in, name):
    tm = 256

    def body(din_ref, out_ref):
        for j in range(N_DEV):
            c0 = T_IN[j] * LANES
            out_ref[j & 1, j >> 1, :, 0:WIN_IN] = din_ref[:, c0:c0 + WIN_IN]
            out_ref[j & 1, j >> 1, :, A_FZ:IN_COLS] = din_ref[:, FZ_COL:PROJ_W]

    return pl.pallas_call(
        body, grid=(D_MODEL // tm,), in_specs=[pl.BlockSpec((tm, PROJ_W), lambda i: (i, 0))],
        out_specs=pl.BlockSpec((2, 4, tm, IN_COLS), lambda i: (0, 0, i, 0)),
        out_shape=jax.ShapeDtypeStruct((2, 4, D_MODEL, IN_COLS), BF16), name=name,
        compiler_params=_cparams(("parallel",)),
    )(d_in)


def _grad_windows_gu(d_gu, name):
    tm = 256

    def body(dgu_ref, out_ref):
        for j in range(N_DEV):
            g0, width = T_GU[j] * LANES, _gu_width(j)
            for half, base in ((0, 0), (FFN, WIN_GU)):
                out_ref[j & 1, j >> 1, :, base:base + width] = dgu_ref[:, half + g0:half + g0 + width]
                if width < WIN_GU:
                    out_ref[j & 1, j >> 1, :, base + width:base + WIN_GU] = jnp.zeros((tm, WIN_GU - width), BF16)

    return pl.pallas_call(
        body, grid=(D_MODEL // tm,), in_specs=[pl.BlockSpec((tm, 2 * FFN), lambda i: (i, 0))],
        out_specs=pl.BlockSpec((2, 4, tm, GU_COLS), lambda i: (0, 0, i, 0)),
        out_shape=jax.ShapeDtypeStruct((2, 4, D_MODEL, GU_COLS), BF16), name=name,
        compiler_params=_cparams(("parallel",)),
    )(d_gu)


def _final_windows(prm, chip, a, recv, params, l, kind, name, prev=None):
    tm = 128
    n_in, n_gu = WIN_IN // LANES, FFN_SHARD // LANES + 1
    widths = (W_IN_SHARD,) if kind == "in" else (FFN_SHARD, FFN_SHARD)
    cols = IN_COLS if kind == "in" else GU_COLS
    n_par = 3 * len(widths)
    prev = [] if prev is None else [t for group in prev for t in group]

    def body(prm_ref, chip_ref, a_ref, r_ref, *refs):
        ins, outs = refs[:n_par], refs[n_par + len(prev):]
        lane = _lane_iota()
        s_main, s_code, clo, chi, s_gu = [prm_ref[i] for i in (0, 3, 4, 5, 7)]

        def total(c0):
            sl = slice(c0, c0 + LANES)
            return ((a_ref[:, sl].astype(F32) + r_ref[0, :, sl].astype(F32)) + r_ref[1, :, sl].astype(F32)) \
                + r_ref[2, :, sl].astype(F32)

        if kind == "in":
            grads = _tiles_down([total(t * LANES) for t in range(n_in)], s_main, lane)
            code = pltpu.roll(total(A_FZ), (LANES - s_code) % LANES, 1)
            for t in (0, n_in - 2, n_in - 1):
                col = lane + t * LANES
                grads[t] = jnp.where((col >= clo) & (col < chi), code, grads[t])
            per_weight = [grads]
        else:
            per_weight = [_tiles_down([total(base + t * LANES) for t in range(n_gu + 1)], s_gu, lane)[:n_gu]
                          for base in (0, WIN_GU)]
        for k, (tiles, width) in enumerate(zip(per_weight, widths)):
            w_ref, m_ref, v_ref = ins[3 * k:3 * k + 3]
            g_ref, d_ref, nm_ref, nv_ref = outs[4 * k:4 * k + 4]
            for t, g in enumerate(tiles):
                n = min(LANES, width - t * LANES)
                sl = slice(t * LANES, t * LANES + n)
                g = g[:, 0:n]
                d, nm, nv = _adamw_math(w_ref[:, sl], g, m_ref[:, sl], v_ref[:, sl])
                g_ref[:, sl] = g
                d_ref[:, sl] = d
                nm_ref[:, sl] = nm
                nv_ref[:, sl] = nv

    def native(width):
        return pl.BlockSpec((None, tm, width), lambda i, p, c: (l, i, 0))

    in_specs = [pl.BlockSpec((None, tm, cols), lambda i, p, c: (c[0], i, 0)),
                pl.BlockSpec((3, tm, cols), lambda i, p, c: (0, i, 0))]
    in_specs += [native(wd) for wd in widths for _ in range(3)] + [_ANY] * len(prev)
    grid_spec = pltpu.PrefetchScalarGridSpec(
        num_scalar_prefetch=2, grid=(D_MODEL // tm,), in_specs=in_specs,
        out_specs=[native(wd) for wd in widths for _ in range(4)])
    out = pl.pallas_call(
        body, grid_spec=grid_spec, name=name, compiler_params=_cparams(("parallel",)),
        out_shape=[jax.ShapeDtypeStruct((DEPTH, D_MODEL, wd), F32) for wd in widths for _ in range(4)],
        input_output_aliases={4 + n_par + k: k for k in range(len(prev))},
    )(prm, chip, a, recv, *params, *prev)
    return [out[4 * k:4 * k + 4] for k in range(len(widths))]


def _me():
    return lax.axis_index("x"), lax.axis_index("y"), lax.axis_index("c")


_CHIP_FLIPS = ((1, 0), (0, 1), (1, 1))
_ANY = pl.BlockSpec(memory_space=pl.ANY)


def _comm_call(body, peers, out_shape, sems, name, args, collective_id):
    if collective_id is None:
        n_in = len(args)
        return pl.pallas_call(body, out_shape=out_shape, name=name, in_specs=[_ANY] * n_in,
                              out_specs=[_ANY] * len(out_shape), scratch_shapes=sems)(*args)

    def sequencer_body(*refs):
        barrier = pltpu.get_barrier_semaphore()
        targets = peers()
        for peer in targets:
            pl.semaphore_signal(barrier, inc=1, device_id=peer, device_id_type=MESH)
        pl.semaphore_wait(barrier, len(targets))
        body(*refs)

    sequencer = plsc.ScalarSubcoreMesh(axis_name="seq", num_cores=1)
    return pl.kernel(sequencer_body, out_type=out_shape, mesh=sequencer, scratch_types=sems, name=name,
                     compiler_params=pltpu.CompilerParams(collective_id=collective_id))(*args)


def _sibling_peer():
    x, y, cc = _me()
    return [(x, y, 1 - cc)]


def _chip_peers():
    x, y, cc = _me()
    return [(x ^ fx, y ^ fy, cc) for fx, fy in _CHIP_FLIPS]


def _all_gather(shards, name, collective_id=None):
    n = len(shards)
    split = [s.shape[0] % (2 * SUBLANES_BF16) == 0 for s in shards]

    def body(*refs):
        x_refs, out_refs = refs[:n], refs[n:2 * n]
        send_sems, recv_sems, local_sems = refs[2 * n:]
        x, y, cc = _me()
        me, sibling = (x, y, cc), (x, y, 1 - cc)
        near, far = [(x ^ 1, y), (x, y ^ 1)], (x ^ 1, y ^ 1)

        def copy(a, k, block, to, half=None, from_shard=False):
            px, py, pc = block
            slot = out_refs[a].at[4 * px + 2 * py + pc]
            if half is not None:
                rows = shards[a].shape[0] // 2
                slot = slot.at[pl.ds(half * rows, rows)]
            return pltpu.make_async_remote_copy(
                src_ref=x_refs[a] if from_shard else slot, dst_ref=slot,
                send_sem=send_sems.at[a, k], recv_sem=recv_sems.at[a, k], device_id=to, device_id_type=MESH)

        mine = [pltpu.make_async_copy(x_refs[a], out_refs[a].at[4 * x + 2 * y + cc], local_sems.at[a])
                for a in range(n)]
        for cp in mine:
            cp.start()
        sent = [copy(a, 0, me, sibling, from_shard=True) for a in range(n)]
        sent += [copy(a, 1 + j, me, (*chip, cc), from_shard=True) for j, chip in enumerate(near) for a in range(n)]
        sent += [copy(a, 3, me, (*far, cc), from_shard=True) for a in range(n) if not split[a]]
        for cp in sent:
            cp.start()

        def pass_on(cp):
            cp.start()
            sent.append(cp)

        for j, chip in enumerate(near):
            for a in range(n):
                copy(a, 1 + j, (*chip, cc), me).wait_recv()
                pass_on(copy(a, 4 + j, (*chip, cc), sibling))
                if split[a]:
                    pass_on(copy(a, 7 + j, (*chip, cc), (*near[1 - j], cc), half=j))
        for a in range(n):
            if split[a]:
                copy(a, 7, (*far, cc), me, half=0).wait_recv()
                copy(a, 8, (*far, cc), me, half=1).wait_recv()
            else:
                copy(a, 3, (*far, cc), me).wait_recv()
            pass_on(copy(a, 6, (*far, cc), sibling))
        for a in range(n):
            copy(a, 0, sibling, me).wait_recv()
            for j, chip in enumerate(near + [far]):
                copy(a, 4 + j, (*chip, 1 - cc), me).wait_recv()
        for cp in sent:
            cp.wait_send()
        for cp in mine:
            cp.wait()

    return _comm_call(
        body, lambda: _sibling_peer() + _chip_peers(),
        [jax.ShapeDtypeStruct((N_DEV,) + s.shape, s.dtype) for s in shards],
        [pltpu.SemaphoreType.DMA((n, 9)), pltpu.SemaphoreType.DMA((n, 9)), pltpu.SemaphoreType.DMA((n,))],
        name, shards, collective_id)


def _send_to_sibling(parts, name, collective_id=None):
    n = len(parts)

    def body(*refs):
        g_refs, out_refs = refs[:n], refs[n:2 * n]
        send_sems, recv_sems = refs[2 * n:]
        x, y, cc = _me()
        copies = [pltpu.make_async_remote_copy(
            src_ref=g_refs[a].at[1 - cc], dst_ref=out_refs[a], send_sem=send_sems.at[a], recv_sem=recv_sems.at[a],
            device_id=(x, y, 1 - cc), device_id_type=MESH) for a in range(n)]
        for cp in copies:
            cp.start()
        for cp in copies:
            cp.wait()

    return _comm_call(
        body, _sibling_peer, [jax.ShapeDtypeStruct(p.shape[1:], p.dtype) for p in parts],
        [pltpu.SemaphoreType.DMA((n,)), pltpu.SemaphoreType.DMA((n,))], name, parts, collective_id)


def _send_to_chips(parts, name, collective_id=None):
    n = len(parts)

    def body(*refs):
        a_refs, out_refs = refs[:n], refs[n:2 * n]
        send_sems, recv_sems = refs[2 * n:]
        x, y, cc = _me()
        copies = []
        for k, (fx, fy) in enumerate(_CHIP_FLIPS):
            px, py = x ^ fx, y ^ fy
            for a in range(n):
                copies.append(pltpu.make_async_remote_copy(
                    src_ref=a_refs[a].at[2 * px + py], dst_ref=out_refs[a].at[k], send_sem=send_sems.at[a, k],
                    recv_sem=recv_sems.at[a, k], device_id=(px, py, cc), device_id_type=MESH))
                copies[-1].start()
        for cp in copies:
            cp.wait()

    return _comm_call(
        body, _chip_peers, [jax.ShapeDtypeStruct((3,) + p.shape[1:], p.dtype) for p in parts],
        [pltpu.SemaphoreType.DMA((n, 3)), pltpu.SemaphoreType.DMA((n, 3))], name, parts, collective_id)


def _pack_rows(w_oa, w_ob, w_o, w_down, w_fg2, conv_w, l):
    conv_bits = lax.bitcast_convert_type(conv_w[l].reshape(-1), BF16).reshape(1, -1)
    tail = jnp.concatenate([w_fg2[l].astype(BF16).reshape(1, D_MODEL),
                            jnp.pad(conv_bits, ((0, 0), (0, D_MODEL - conv_bits.shape[1])))], axis=0)
    tail = jnp.pad(tail, ((0, B_ROWS - B_FG2 - tail.shape[0]), (0, 0)))
    rows = jnp.concatenate([w_oa[l].astype(BF16), w_ob[l].astype(BF16), w_o[l].astype(BF16),
                            w_down[l].astype(BF16)], axis=0)
    return rows, tail


def _unpack_tail(tail):
    w_fg2 = tail[:, 0, :].reshape(N_DEV, RANK, KEY_W // N_DEV).transpose(1, 0, 2).reshape(RANK, KEY_W)
    conv_bits = tail[:, B_CONV - B_FG2, :2 * 3 * ROW_SHARD].reshape(N_DEV, 3 * ROW_SHARD, 2)
    conv_w = lax.bitcast_convert_type(conv_bits, F32).reshape(N_DEV, 3, ROW_SHARD)
    return jnp.pad(w_fg2, ((0, LANES - RANK), (0, 0))), conv_w.transpose(1, 0, 2).reshape(3, D_MODEL)


def _by_core_chip(t):
    return t.reshape((2, 2, 2) + t.shape[1:]).transpose((2, 0, 1) + tuple(range(3, t.ndim + 2))).reshape(
        (2, 4) + t.shape[1:])


def _grad_rows(g):
    fg2 = g["w_fg2"][:RANK].reshape(RANK, N_DEV, KEY_W // N_DEV).transpose(1, 0, 2).reshape(N_DEV, 1, D_MODEL)
    conv = g["conv_w"].astype(BF16).reshape(3, N_DEV, ROW_SHARD).transpose(1, 0, 2).reshape(N_DEV, 1, 3 * ROW_SHARD)
    tail = jnp.concatenate([fg2, jnp.pad(conv, ((0, 0), (0, 0), (0, D_MODEL - 3 * ROW_SHARD)))], axis=1)
    tail = jnp.pad(tail, ((0, 0), (0, B_ROWS - B_FG2 - 2), (0, 0)))
    parts = [g["w_oa"].reshape(N_DEV, ROW_SHARD, D_MODEL), g["w_ob"].reshape(N_DEV, ROW_SHARD, D_MODEL),
             g["w_o"].reshape(N_DEV, ROW_SHARD, D_MODEL), g["w_down"].reshape(N_DEV, FFN_SHARD, D_MODEL), tail]
    return _by_core_chip(jnp.concatenate(parts, axis=1))


def _ungrad_rows(gs):
    return dict(w_oa=gs[B_OA:B_OA + ROW_SHARD], w_ob=gs[B_OB:B_OB + ROW_SHARD], w_o=gs[B_O:B_O + ROW_SHARD],
                w_ffn_down=gs[B_DOWN:B_DOWN + FFN_SHARD], w_fg2=gs[B_FG2].reshape(RANK, KEY_W // N_DEV),
                conv_w=gs[B_CONV, :3 * ROW_SHARD].reshape(3, ROW_SHARD))


def _layer_fwd(x, h, p, l, next_norm_g=None):
    tag = f"l{l}_"
    if h is None:
        h = _rmsnorm_fwd(x, p["norm1_g"], tag + "norm1")
    proj = _matmul(h, p["w_in"], "nn", BF16, tag + "proj")
    o, states = _gla_fwd(proj, p["w_fg2"], p["b_fg"], tag + "gla_fwd")
    p.update(p.pop("rest")((o,)))
    oa = _gla_post_fwd(o, proj, p["gla_norm_g"], tag + "gla_post")
    ya = _matmul(oa, p["w_oa"], "nn", BF16, tag + "ya")
    cb = _conv_fwd(proj, p["conv_w"], p["conv_b"], tag + "conv")
    yb = _matmul(cb, p["w_ob"], "nn", BF16, tag + "yb")
    mix = _mix_fwd(proj, ya, yb, tag + "mix")
    x1, h2 = _matmul(mix, p["w_o"], "nn", F32, tag + "x1", residual=x, norm_g=p["norm2_g"])
    gu = _matmul(h2, p["w_gu"], "nn", BF16, tag + "gu")
    hid = _swiglu_fwd(gu, tag + "swiglu")
    x2 = _matmul(hid, p["w_down"], "nn", F32, tag + "x2", residual=x1, norm_g=next_norm_g)
    x2, h_next = x2 if next_norm_g is not None else (x2, None)
    saved = dict(x=x, h=h, proj=proj, o=o, states=states, oa=oa, ya=ya, cb=cb, yb=yb, mix=mix, x1=x1, h2=h2,
                 gu=gu, hid=hid)
    return x2, h_next, saved


def _layer_bwd(dx2, dx2h, p, sv, l, reduce=None):
    if reduce is None:
        reduce = lambda group, grads: ((), ())
    tag = f"l{l}_b_"
    dhid = _matmul(dx2h, p["w_down"], "nt", BF16, tag + "dhid")
    d_down = _matmul(sv["hid"], dx2h, "tn", BF16, tag + "dw_down")
    dgu = _swiglu_bwd(dhid, sv["gu"], tag + "swiglu")
    dx1, dx1h, dg2 = _matmul_norm_bwd(dgu, p["w_gu"], sv["x1"], p["norm2_g"], dx2, tag + "dh2")
    d_gu = _matmul(sv["h2"], dgu, "tn", BF16, tag + "dw_gu")
    gu_packed, gu_sums = reduce("gu", d_gu)
    dmix = _matmul(dx1h, p["w_o"], "nt", BF16, tag + "dmix", after=gu_packed)
    d_o = _matmul(sv["mix"], dx1h, "tn", BF16, tag + "dw_o")
    dproj, dya, dyb = _mix_bwd(dmix, sv["proj"], sv["ya"], sv["yb"], tag + "mix")
    dcb = _matmul(dyb, p["w_ob"], "nt", BF16, tag + "dcb", after=gu_sums)
    d_ob = _matmul(sv["cb"], dyb, "tn", BF16, tag + "dw_ob")
    dproj, dwb = _conv_bwd(dcb, sv["proj"], p["conv_w"], p["conv_b"], dproj, tag + "conv")
    doa = _matmul(dya, p["w_oa"], "nt", BF16, tag + "doa")
    d_oa = _matmul(sv["oa"], dya, "tn", BF16, tag + "dw_oa")
    dproj, do, dgg = _gla_post_bwd(doa, sv["o"], sv["proj"], p["gla_norm_g"], dproj, tag + "gla_post")
    dq, dk, dv, dfg, dbfg = _gla_bwd(sv["proj"], p["w_fg2"], p["b_fg"], sv["states"], do, tag + "gla")
    fz = sv["proj"][:, FZ_COL:]
    dproj = _place_qkv(dq, dk, dv, dproj, tag + "place_qkv")
    dproj = _matmul(dfg, p["w_fg2"], "nt", BF16, tag + "dfz", into=(dproj, FZ_COL))
    d_fg2 = _matmul(fz, dfg, "tn", BF16, tag + "dw_fg2")
    rows = dict(w_fg2=d_fg2, conv_w=dwb[0:3], w_oa=d_oa, w_ob=d_ob, w_o=d_o, w_down=d_down)
    rows_packed, rows_sums = reduce("rows", rows)
    d_in = _matmul(sv["h"], dproj, "tn", BF16, tag + "dw_in", after=rows_packed)
    _, in_sums = reduce("in", d_in)
    dx, dxh, dg1 = _matmul_norm_bwd(dproj, p["w_in"], sv["x"], p["norm1_g"], dx1, tag + "dh",
                                    after=(d_in,) + tuple(rows_sums) + tuple(in_sums))
    big = dict(w_in=d_in, w_gu=d_gu, **rows)
    pad = lambda t: jnp.pad(t, ((0, 0), (0, D_MODEL - t.shape[1])))
    small = [dg1[0:1], pad(dbfg[0:1]), pad(dgg[0:1]), dwb[3:4], dg2[0:1]]
    return dx, dxh, big, small


def _local_step(x, target, weights_of, final_g, reduce_of=None):
    saved, layers, h = [], [], None
    for l in range(DEPTH):
        layers.append(weights_of(l, x))
        x, h, sv = _layer_fwd(x, h, layers[l], l, layers[l].get("next_norm1_g"))
        saved.append(sv)
    loss, dx, dxh, dgf = _loss_head(x, final_g, target, "loss_head")
    bigs, smalls = [None] * DEPTH, [None] * DEPTH
    for l in reversed(range(DEPTH)):
        dx, dxh, bigs[l], smalls[l] = _layer_bwd(dx, dxh, layers[l], saved[l], l, reduce_of(l) if reduce_of else None)
    loss_row = jnp.pad(loss[0:1], ((0, 0), (0, D_MODEL - loss.shape[1])))
    small = jnp.concatenate(smalls[0] + smalls[1] + [dgf[0:1], loss_row], axis=0)
    small = jnp.pad(small, ((0, SMALL_ROWS - small.shape[0]), (0, 0)))
    return loss[0, 0], dx, bigs, small


def kernel(x, norm1_g, w_in, w_fg2, b_fg, gla_norm_g, w_oa, conv_w, conv_b, w_ob, w_o, norm2_g, w_ffn_gate, w_ffn_up, w_ffn_down, final_g, loss_target, m_norm1_g, m_w_in, m_w_fg2, m_b_fg, m_gla_norm_g, m_w_oa, m_conv_w, m_conv_b, m_w_ob, m_w_o, m_norm2_g, m_w_ffn_gate, m_w_ffn_up, m_w_ffn_down, m_final_g, v_norm1_g, v_w_in, v_w_fg2, v_b_fg, v_gla_norm_g, v_w_oa, v_conv_w, v_conv_b, v_w_ob, v_w_o, v_norm2_g, v_w_ffn_gate, v_w_ffn_up, v_w_ffn_down, v_final_g):
    names = ["norm1_g", "w_in", "w_fg2", "b_fg", "gla_norm_g", "w_oa", "conv_w", "conv_b", "w_ob", "w_o",
             "norm2_g", "w_ffn_gate", "w_ffn_up", "w_ffn_down", "final_g"]
    w = dict(zip(names, [norm1_g, w_in, w_fg2, b_fg, gla_norm_g, w_oa, conv_w, conv_b, w_ob, w_o, norm2_g,
                         w_ffn_gate, w_ffn_up, w_ffn_down, final_g]))
    m = dict(zip(names, [m_norm1_g, m_w_in, m_w_fg2, m_b_fg, m_gla_norm_g, m_w_oa, m_conv_w, m_conv_b, m_w_ob,
                         m_w_o, m_norm2_g, m_w_ffn_gate, m_w_ffn_up, m_w_ffn_down, m_final_g]))
    v = dict(zip(names, [v_norm1_g, v_w_in, v_w_fg2, v_b_fg, v_gla_norm_g, v_w_oa, v_conv_w, v_conv_b, v_w_ob,
                         v_w_o, v_norm2_g, v_w_ffn_gate, v_w_ffn_up, v_w_ffn_down, v_final_g]))
    col_names = ["w_in", "w_ffn_gate", "w_ffn_up"]
    cx, cy, cc = _me()
    prm = jnp.asarray(SHIFT_TABLE)[4 * cx + 2 * cy + cc]
    core = jnp.reshape(cc, (1,)).astype(jnp.int32)
    chip = jnp.reshape(2 * cx + cy, (1,)).astype(jnp.int32)

    ids = iter(range(32))

    gathered, previous = [], None
    for l in range(DEPTH):
        rows, tail = _pack_rows(w_oa, w_ob, w_o, w_ffn_down, w_fg2, conv_w, l)
        win_in, win_gu = _window_cols(prm, w_in, w_ffn_gate, w_ffn_up, l, f"l{l}_windows", after=previous)
        previous = rows
        first = _all_gather([win_in, tail], f"l{l}_gather_in", next(ids))
        gathered.append(list(first) + list(_all_gather([win_gu, rows], f"l{l}_gather_rest", next(ids))))

    def weights_of(l, x_in):
        all_in, all_tail, all_gu, all_rows = gathered[l]
        after = (x_in,) if l > 0 else ()
        w_in_full, tail = _assemble_in(all_in, all_tail, f"l{l}_assemble_in", after)
        w_fg2_full, conv_w_full = _unpack_tail(tail)

        def rest(after_rest):
            names_rest = ("w_gu", "w_oa", "w_ob", "w_o", "w_down")
            return dict(zip(names_rest, _assemble_rest(all_gu, all_rows, f"l{l}_assemble_rest", after + after_rest)))

        return dict(w_in=w_in_full, rest=rest, w_fg2=w_fg2_full,
                    conv_w=conv_w_full, norm1_g=norm1_g[l][None], b_fg=b_fg[l][None],
                    gla_norm_g=gla_norm_g[l][None], conv_b=conv_b[l][None], norm2_g=norm2_g[l][None],
                    next_norm1_g=norm1_g[l + 1][None] if l + 1 < DEPTH else None)

    pending = [dict() for _ in range(DEPTH)]
    landed = []

    def reduce_of(l):
        def reduce(group, grads):
            tag = f"l{l}_{group}"
            if group == "gu":
                packed = _grad_windows_gu(grads, tag + "_windows")
            elif group == "in":
                packed = _grad_windows_in(grads, tag + "_windows")
            else:
                packed = _grad_rows(grads)
            (from_sibling,) = _send_to_sibling([packed], tag + "_to_sibling", next(ids))
            waited = () if group == "in" else tuple(landed)
            if waited:
                landed.clear()
            sums = _pair_sum(packed, from_sibling, core, tag + "_pair_sum", after=waited)
            (from_chips,) = _send_to_chips([sums], tag + "_to_chips", next(ids))
            landed.append(from_chips)
            pending[l][group] = (sums, from_chips)
            return (packed,), (sums,)
        return reduce

    loss, dx, bigs, small = _local_step(x[0], loss_target[0], weights_of, final_g[None], reduce_of)

    grads, deltas, new_m, new_v = {}, {}, {}, {}
    for kind, group_names in (("gu", col_names[1:]), ("in", col_names[:1])):
        params = [t for n in group_names for t in (w[n], m[n], v[n])]
        out = None
        for l in reversed(range(DEPTH)):
            sums, from_chips = pending[l][kind]
            out = _final_windows(prm, chip, sums, from_chips, params, l, kind, f"l{l}_{kind}_final", out)
        for n, (g, d, nm, nv) in zip(group_names, out):
            grads[n], deltas[n], new_m[n], new_v[n] = g, d, nm, nv
    row_grads = [_ungrad_rows(_chip_sum(*pending[l]["rows"], chip, f"l{l}_rows_chip_sum")) for l in range(DEPTH)]
    for n in row_grads[0]:
        grads[n] = jnp.stack([row_grads[l][n] for l in range(DEPTH)])

    small_sum = _sum_devices(_all_gather([small], "gather_small")[0], "sum_small")
    r512, r256 = slice(0, KEY_W), slice(0, HEAD_V)
    grads.update(
        norm1_g=jnp.stack([small_sum[0], small_sum[5]]), b_fg=jnp.stack([small_sum[1, r512], small_sum[6, r512]]),
        gla_norm_g=jnp.stack([small_sum[2, r256], small_sum[7, r256]]),
        conv_b=jnp.stack([small_sum[3], small_sum[8]]), norm2_g=jnp.stack([small_sum[4], small_sum[9]]),
        final_g=small_sum[10])

    for n in names:
        if n not in col_names:
            deltas[n], new_m[n], new_v[n] = _adamw(w[n], grads[n], m[n], v[n], "adamw_" + n)

    total_loss = small_sum[2 * 5 + 1, 0]
    return (total_loss, dx[None], *[grads[n] for n in names], *[deltas[n] for n in names],
            *[new_m[n] for n in names], *[new_v[n] for n in names])
```

```python
import functools

import jax
import jax.numpy as jnp
import numpy as np
from jax import lax
from jax.experimental import pallas as pl
from jax.experimental.pallas import tpu as pltpu
from jax.experimental.pallas import tpu_sc as plsc

F32 = jnp.float32
BF16 = jnp.bfloat16
MESH = pl.DeviceIdType.MESH

D_MODEL = 1024
DEPTH = 2
CHUNK = 64
HEADS = 4
HEAD_K = 128
HEAD_V = 256
KEY_W = HEADS * HEAD_K
VAL_W = HEADS * HEAD_V
RANK = 16
TAU = 16.0
FFN = 2816
IN_WIDTH = 2 * KEY_W + 2 * VAL_W + RANK + 5 * D_MODEL
EPS = 1e-6
Q_SCALE = HEAD_K ** -0.5
N_DEV = 8
ADAM_LR, ADAM_B1, ADAM_B2, ADAM_EPS, ADAM_WD, ADAM_STEP = 0.001, 0.9, 0.999, 1e-08, 0.01, 10

LANES = 128
SUBLANES_BF16 = 16
VMEM_LIMIT = 48 * 1024 * 1024
VMEM_LIMIT_WIDE = 56 * 1024 * 1024

FZ_COL = 2 * KEY_W + 2 * VAL_W + 5 * D_MODEL
PROJ_W = FZ_COL + LANES
SEG_R, SEG_GBI, SEG_GCI, SEG_CX, SEG_GA, SEG_GB = 2, 3, 4, 5, 6, 7

W_IN_SHARD = IN_WIDTH // N_DEV
FFN_SHARD = FFN // N_DEV
ROW_SHARD = D_MODEL // N_DEV

WIN_IN = 9 * LANES
WIN_GU = 4 * LANES
A_FZ = WIN_IN
IN_COLS = A_FZ + LANES
GU_COLS = 2 * WIN_GU
ORIG_FZ = 2 * KEY_W + 2 * VAL_W


def _new_col(o):
    if o < ORIG_FZ:
        return o
    if o < ORIG_FZ + RANK:
        return FZ_COL + (o - ORIG_FZ)
    return o - RANK


def _shift_table():
    t_in, rows = [], []
    for j in range(N_DEV):
        new = [_new_col(W_IN_SHARD * j + i) for i in range(W_IN_SHARD)]
        main = [i for i in range(W_IN_SHARD) if new[i] < FZ_COL]
        code = [i for i in range(W_IN_SHARD) if new[i] >= FZ_COL]
        shift = new[main[0]] - main[0]
        t_in.append(shift // LANES)
        assert all(new[i] - i == shift for i in main) and shift % LANES + W_IN_SHARD <= WIN_IN
        if code:
            cshift = new[code[0]] - FZ_COL - code[0]
            crow = [cshift % LANES, code[0], code[-1] + 1, int(cshift < 0)]
        else:
            crow = [0, 0, 0, 0]
        rows.append([shift % LANES, main[0], main[-1] + 1] + crow + [FFN_SHARD * j % LANES])
    return tuple(t_in), np.asarray(rows, np.int32)


T_IN, SHIFT_TABLE = _shift_table()
T_GU = tuple(FFN_SHARD * j // LANES for j in range(N_DEV))

B_OA, B_OB, B_O, B_DOWN = 0, ROW_SHARD, 2 * ROW_SHARD, 3 * ROW_SHARD
B_FG2 = B_DOWN + FFN_SHARD
B_CONV = B_FG2 + 1
B_ROWS = B_FG2 + SUBLANES_BF16
SMALL_ROWS = 32


def _pick(n, candidates):
    for c in candidates:
        if n % c == 0:
            return c
    return n


def _cparams(sem):
    return pltpu.CompilerParams(dimension_semantics=sem, vmem_limit_bytes=VMEM_LIMIT)


def _sigmoid(x):
    return 1.0 / (1.0 + jnp.exp(-x))


def _matmul(a, b, dims, out_dtype, name, residual=None, after=(), into=None, norm_g=None):
    if dims == "nn":
        (m, k), (k2, n) = a.shape, b.shape
    elif dims == "nt":
        (m, k), (n, k2) = a.shape, b.shape
    else:
        (k, m), (k2, n) = a.shape, b.shape
    assert k == k2, (a.shape, b.shape, dims)
    tm = _pick(m, (1024, 1408, 512, 256, 128))
    tn = _pick(n, (1664, 1408, 1024, 512, 256, 128))
    tk = _pick(k, (1664, 1408, 1024, 512, 256, 128))
    nk = k // tk
    if dims == "nn":
        a_spec = pl.BlockSpec((tm, tk), lambda i, j, kk: (i, kk))
        b_spec = pl.BlockSpec((tk, tn), lambda i, j, kk: (kk, j))
        contract = (((1,), (0,)), ((), ()))
    elif dims == "nt":
        a_spec = pl.BlockSpec((tm, tk), lambda i, j, kk: (i, kk))
        b_spec = pl.BlockSpec((tn, tk), lambda i, j, kk: (j, kk))
        contract = (((1,), (1,)), ((), ()))
    else:
        a_spec = pl.BlockSpec((tk, tm), lambda i, j, kk: (kk, i))
        b_spec = pl.BlockSpec((tk, tn), lambda i, j, kk: (kk, j))
        contract = (((0,), (0,)), ((), ()))
    o_spec = pl.BlockSpec((tm, tn), lambda i, j, kk: (i, j))
    has_res = residual is not None
    out_spec, out_struct, placed, aliases = o_spec, jax.ShapeDtypeStruct((m, n), out_dtype), (), {}
    if into is not None:
        buffer, col = into
        assert col % tn == 0 and buffer.dtype == out_dtype and not has_res
        out_spec = pl.BlockSpec((tm, tn), lambda i, j, kk: (i, col // tn + j))
        out_struct, placed, aliases = jax.ShapeDtypeStruct(buffer.shape, out_dtype), (buffer,), {2 + len(after): 0}

    has_norm = norm_g is not None
    assert not has_norm or (tn == n and into is None)

    def body(*refs):
        a_ref, b_ref = refs[:2]
        r_ref = refs[2] if has_res else None
        n_in = 2 + has_res + has_norm + len(after) + len(placed)
        o_ref = refs[n_in]
        kk = pl.program_id(2)
        part = lax.dot_general(a_ref[...], b_ref[...], contract, preferred_element_type=F32)

        def finish(total):
            if has_res:
                total = total + r_ref[...]
            o_ref[...] = total.astype(o_ref.dtype)
            if has_norm:
                rstd = lax.rsqrt(jnp.mean(total * total, axis=-1, keepdims=True) + EPS)
                refs[n_in + 1][...] = (total * rstd * refs[2 + has_res][...]).astype(BF16)

        if nk == 1:
            finish(part)
            return
        acc_ref = refs[-1]

        @pl.when(kk == 0)
        def _():
            acc_ref[...] = part

        @pl.when((kk > 0) & (kk < nk - 1))
        def _():
            acc_ref[...] += part

        @pl.when(kk == nk - 1)
        def _():
            finish(acc_ref[...] + part)

    in_specs = [a_spec, b_spec] + ([o_spec] if has_res else [])
    in_specs += [pl.BlockSpec((1, n), lambda i, j, kk: (0, 0))] if has_norm else []
    in_specs += [_after_spec(t) for t in after] + [_ANY] * len(placed)
    args = (a, b) + ((residual,) if has_res else ()) + ((norm_g,) if has_norm else ()) + tuple(after) + placed
    if has_norm:
        out_spec, out_struct = [out_spec, o_spec], [out_struct, jax.ShapeDtypeStruct((m, n), BF16)]
    return pl.pallas_call(
        body, grid=(m // tm, n // tn, nk), in_specs=in_specs, out_specs=out_spec,
        out_shape=out_struct, input_output_aliases=aliases,
        scratch_shapes=[pltpu.VMEM((tm, tn), F32)] if nk > 1 else [], name=name,
        compiler_params=_cparams(("parallel", "parallel", "arbitrary")),
    )(*args)


def _rmsnorm_fwd(x, g, name):
    s, d = x.shape
    tm = _pick(s, (512, 256))

    def body(x_ref, g_ref, o_ref):
        xv = x_ref[...]
        r = lax.rsqrt(jnp.mean(xv * xv, axis=-1, keepdims=True) + EPS)
        o_ref[...] = (xv * r * g_ref[...]).astype(o_ref.dtype)

    row = pl.BlockSpec((tm, d), lambda i: (i, 0))
    return pl.pallas_call(
        body, grid=(s // tm,), in_specs=[row, pl.BlockSpec((1, d), lambda i: (0, 0))], out_specs=row,
        out_shape=jax.ShapeDtypeStruct((s, d), BF16), name=name, compiler_params=_cparams(("parallel",)),
    )(x, g)


def _matmul_norm_bwd(a, b, x, g, dres, name, after=()):
    (s, k), (d, k2) = a.shape, b.shape
    assert k == k2 and x.shape == (s, d)
    tm = _pick(s, (1024, 512, 256))
    tk = _pick(k, (1664, 1408, 1024, 512, 256, 128))
    nk = k // tk

    def body(a_ref, b_ref, x_ref, g_ref, dres_ref, *rest):
        dx_ref, dx16_ref, dg_ref, acc_ref = rest[len(after):]
        i, kk = pl.program_id(0), pl.program_id(1)
        part = lax.dot_general(a_ref[...], b_ref[...], _NT, preferred_element_type=F32)

        def finish(dh):
            xv = x_ref[...]
            r = lax.rsqrt(jnp.mean(xv * xv, axis=-1, keepdims=True) + EPS)
            xn = xv * r
            dxn = dh * g_ref[...]
            dx = dres_ref[...] + r * (dxn - xn * jnp.mean(dxn * xn, axis=-1, keepdims=True))
            dx_ref[...] = dx
            dx16_ref[...] = dx.astype(BF16)
            dg = jnp.broadcast_to(jnp.sum(dh * xn, axis=0, keepdims=True), dg_ref.shape)

            @pl.when(i == 0)
            def _():
                dg_ref[...] = dg

            @pl.when(i > 0)
            def _():
                dg_ref[...] += dg

        @pl.when(kk == 0)
        def _():
            acc_ref[...] = part

        @pl.when((kk > 0) & (kk < nk - 1))
        def _():
            acc_ref[...] += part

        @pl.when(kk == nk - 1)
        def _():
            finish(acc_ref[...] + part)

    assert nk > 1
    row = pl.BlockSpec((tm, d), lambda i, kk: (i, 0))
    return pl.pallas_call(
        body, grid=(s // tm, nk),
        in_specs=[pl.BlockSpec((tm, tk), lambda i, kk: (i, kk)), pl.BlockSpec((d, tk), lambda i, kk: (0, kk)), row,
                  pl.BlockSpec((1, d), lambda i, kk: (0, 0)), row] + [_after_spec(t) for t in after],
        out_specs=[row, row, pl.BlockSpec((8, d), lambda i, kk: (0, 0))],
        out_shape=[jax.ShapeDtypeStruct((s, d), F32), jax.ShapeDtypeStruct((s, d), BF16),
                   jax.ShapeDtypeStruct((8, d), F32)],
        scratch_shapes=[pltpu.VMEM((tm, d), F32)], name=name,
        compiler_params=pltpu.CompilerParams(dimension_semantics=("arbitrary", "arbitrary"),
                                             vmem_limit_bytes=VMEM_LIMIT_WIDE),
    )(a, b, x, g, dres, *after)


def _loss_head(x, g, target, name):
    s, d = x.shape
    tm = _pick(s, (512, 256))

    def body(x_ref, g_ref, t_ref, loss_ref, dx_ref, dx16_ref, dg_ref):
        xv = x_ref[...]
        gv = g_ref[...]
        r = lax.rsqrt(jnp.mean(xv * xv, axis=-1, keepdims=True) + EPS)
        xn = xv * r
        err = xn * gv - t_ref[...]
        dy = err * (1.0 / d)
        dxn = dy * gv
        dx = r * (dxn - xn * jnp.mean(dxn * xn, axis=-1, keepdims=True))
        dx_ref[...] = dx
        dx16_ref[...] = dx.astype(BF16)

        @pl.when(pl.program_id(0) == 0)
        def _():
            dg_ref[...] = jnp.zeros_like(dg_ref)
            loss_ref[...] = jnp.zeros_like(loss_ref)

        dg_ref[...] += jnp.broadcast_to(jnp.sum(dy * xn, axis=0, keepdims=True), dg_ref.shape)
        row_loss = jnp.sum(err * err, axis=-1, keepdims=True)
        loss_ref[...] += jnp.broadcast_to((0.5 / d) * jnp.sum(row_loss, axis=0, keepdims=True), loss_ref.shape)

    row = pl.BlockSpec((tm, d), lambda i: (i, 0))
    return pl.pallas_call(
        body, grid=(s // tm,), in_specs=[row, pl.BlockSpec((1, d), lambda i: (0, 0)), row],
        out_specs=[pl.BlockSpec((8, LANES), lambda i: (0, 0)), row, row, pl.BlockSpec((8, d), lambda i: (0, 0))],
        out_shape=[jax.ShapeDtypeStruct((8, LANES), F32), jax.ShapeDtypeStruct((s, d), F32),
                   jax.ShapeDtypeStruct((s, d), BF16), jax.ShapeDtypeStruct((8, d), F32)],
        name=name, compiler_params=_cparams(("arbitrary",)),
    )(x, g, target)


def _tri_dot(tri, x):
    hi = x.astype(BF16)
    lo = (x - hi.astype(F32)).astype(BF16)
    return jnp.dot(tri, hi, preferred_element_type=F32) + jnp.dot(tri, lo, preferred_element_type=F32)


def _chunk_masks(rows):
    r, c = np.arange(rows)[:, None], np.arange(rows)[None, :]
    same = (r // CHUNK) == (c // CHUNK)
    return [jnp.asarray(m, BF16) for m in (same & (r >= c), same & (r > c), same)]


def _block_decay(fz, w, b, incl, ones=None):
    fg = jnp.dot(fz, w, preferred_element_type=F32) + b
    la = (jnp.minimum(fg, 0.0) - jnp.log(1.0 + jnp.exp(-jnp.abs(fg)))) * (1.0 / TAU)
    cum = _tri_dot(incl, la)
    ends = [cum[i + CHUNK - 1:i + CHUNK, :] for i in range(0, fz.shape[0], CHUNK)]
    if ones is None:
        end = jnp.concatenate([jnp.broadcast_to(e, (CHUNK, e.shape[1])) for e in ends], axis=0)
    else:
        end = _tri_dot(ones, la)
    return fg, jnp.exp(end - cum), [jnp.exp(e) for e in ends]


_TN = (((0,), (0,)), ((), ()))
_NT = (((1,), (1,)), ((), ()))


def _gla_specs(rows):
    q_spec = pl.BlockSpec((rows, HEAD_K), lambda h, c: (c, h))
    k_spec = pl.BlockSpec((rows, HEAD_K), lambda h, c: (c, HEADS + h))
    v_spec = pl.BlockSpec((rows, HEAD_V), lambda h, c: (c, HEADS + h))
    fz_spec = pl.BlockSpec((rows, LANES), lambda h, c: (c, FZ_COL // LANES))
    w_spec = pl.BlockSpec((LANES, HEAD_K), lambda h, c: (0, h))
    b_spec = pl.BlockSpec((1, HEAD_K), lambda h, c: (0, h))
    mask_spec = pl.BlockSpec((rows, rows), lambda h, c: (0, 0))
    return q_spec, k_spec, v_spec, fz_spec, w_spec, b_spec, mask_spec


def _gla_fwd(proj, wfg, bfg, gla_g, name):
    s = proj.shape[0]
    nc = s // CHUNK
    per = _pick(nc, (8, 4, 2, 1))
    rows = per * CHUNK
    incl, _, ones = _chunk_masks(rows)

    def body(q_ref, k_ref, v_ref, fz_ref, w_ref, b_ref, incl_ref, ones_ref, r_ref, g_ref,
             o_ref, st_ref, oa_ref, state, update):
        @pl.when(pl.program_id(1) == 0)
        def _():
            state[...] = jnp.zeros_like(state)

        _, dec, gammas = _block_decay(fz_ref[...], w_ref[...], b_ref[...], incl_ref[...], ones_ref[...])
        kd = (k_ref[...].astype(F32) * dec).astype(BF16)
        qs = (q_ref[...].astype(F32) * Q_SCALE).astype(BF16)
        for i in range(per):
            sl = slice(i * CHUNK, (i + 1) * CHUNK)
            update[i] = lax.dot_general(v_ref[sl, :], kd[sl], _TN, preferred_element_type=F32)
        st = state[...]
        for i in range(per):
            st = st * gammas[i] + update[i]
            st_ref[0, i] = st.astype(BF16)
        state[...] = st
        for i in range(per):
            sl = slice(i * CHUNK, (i + 1) * CHUNK)
            o_ref[sl, :] = lax.dot_general(qs[sl], st_ref[0, i], _NT, preferred_element_type=F32).astype(o_ref.dtype)
        ov = o_ref[...].astype(F32)
        rstd = lax.rsqrt(jnp.mean(ov * ov, axis=-1, keepdims=True) + EPS)
        rv = r_ref[...].astype(F32)
        oa_ref[...] = (ov * rstd * g_ref[...] * (rv * _sigmoid(rv))).astype(oa_ref.dtype)

    q_spec, k_spec, v_spec, fz_spec, w_spec, b_spec, mask_spec = _gla_specs(rows)
    head_v = pl.BlockSpec((rows, HEAD_V), lambda h, c: (c, h))
    return pl.pallas_call(
        body, grid=(HEADS, nc // per),
        in_specs=[q_spec, k_spec, v_spec, fz_spec, w_spec, b_spec, mask_spec, mask_spec,
                  pl.BlockSpec((rows, HEAD_V), lambda h, c: (c, SEG_R * (D_MODEL // HEAD_V) + h)),
                  pl.BlockSpec((1, HEAD_V), lambda h, c: (0, 0))],
        out_specs=[head_v, pl.BlockSpec((1, per, HEAD_V, HEAD_K), lambda h, c: (h, c, 0, 0)), head_v],
        out_shape=[jax.ShapeDtypeStruct((s, VAL_W), BF16),
                   jax.ShapeDtypeStruct((HEADS, nc, HEAD_V, HEAD_K), BF16), jax.ShapeDtypeStruct((s, VAL_W), BF16)],
        scratch_shapes=[pltpu.VMEM((HEAD_V, HEAD_K), F32), pltpu.VMEM((per, HEAD_V, HEAD_K), F32)], name=name,
        compiler_params=_cparams(("parallel", "arbitrary")),
    )(proj, proj, proj, proj, wfg, bfg, incl, ones, proj, gla_g)


def _gla_bwd(proj, wfg, bfg, states, doa, o, gla_g, dproj, name):
    s = proj.shape[0]
    nc = s // CHUNK
    per = _pick(nc, (8, 4, 2, 1))
    rows = per * CHUNK
    nblk = nc // per
    incl, strict, _ = _chunk_masks(rows)

    def rev(spec_fn):
        return lambda h, j: spec_fn(h, nblk - 1 - j)

    def body(q_ref, k_ref, v_ref, fz_ref, w_ref, b_ref, incl_ref, strict_ref, doa_ref, o_ref, r_ref, g_ref,
             st_ref, prev_ref, _, dq_ref, dk_ref, dv_ref, dfg_ref, db_ref, dr_ref, dg_ref,
             carry, gt_all, dkd_all, dgg_all):
        j = pl.program_id(1)

        @pl.when(j == 0)
        def _():
            carry[...] = jnp.zeros_like(carry)
            db_ref[...] = jnp.zeros_like(db_ref)

        @pl.when((j == 0) & (pl.program_id(0) == 0))
        def _():
            dg_ref[...] = jnp.zeros_like(dg_ref)

        ov = o_ref[...].astype(F32)
        rstd = lax.rsqrt(jnp.mean(ov * ov, axis=-1, keepdims=True) + EPS)
        ohat = ov * rstd
        rv = r_ref[...].astype(F32)
        sg = _sigmoid(rv)
        doav = doa_ref[...].astype(F32)
        gv = g_ref[...]
        dr_ref[...] = (doav * ohat * gv * (sg * (1.0 + rv * (1.0 - sg)))).astype(dr_ref.dtype)
        don = doav * (rv * sg)
        dg_ref[...] += jnp.broadcast_to(jnp.sum(don * ohat, axis=0, keepdims=True), dg_ref.shape)
        dohat = don * gv
        do = (rstd * (dohat - ohat * jnp.mean(dohat * ohat, axis=-1, keepdims=True))).astype(BF16)

        fg, dec, gammas = _block_decay(fz_ref[...], w_ref[...], b_ref[...], incl_ref[...])
        kd = k_ref[...].astype(F32) * dec
        kd16 = kd.astype(BF16)
        qs = (q_ref[...].astype(F32) * Q_SCALE).astype(BF16)
        for i in range(per):
            sl = slice(i * CHUNK, (i + 1) * CHUNK)
            gt_all[i] = lax.dot_general(do[sl], qs[sl], _TN, preferred_element_type=F32)
        back = carry[...]
        for i in reversed(range(per)):
            gt = back + gt_all[i]
            gt_all[i] = gt
            back = gt * gammas[i]
        carry[...] = back
        has_prev = (j < nblk - 1).astype(F32)
        for i in range(per):
            sl = slice(i * CHUNK, (i + 1) * CHUNK)
            gt = gt_all[i]
            gt16 = gt.astype(BF16)
            dq_ref[sl, :] = (jnp.dot(do[sl], st_ref[0, i], preferred_element_type=F32) * Q_SCALE).astype(dq_ref.dtype)
            dkd_all[sl, :] = jnp.dot(v_ref[sl, :], gt16, preferred_element_type=F32)
            dv_ref[sl, :] = lax.dot_general(kd16[sl], gt16, _NT, preferred_element_type=F32).astype(dv_ref.dtype)
            if i > 0:
                st_prev = st_ref[0, i - 1].astype(F32)
            else:
                st_prev = prev_ref[0, 0].astype(F32) * has_prev
            dgamma = jnp.sum(gt * st_prev, axis=0, keepdims=True)
            dgg_all[sl, :] = jnp.broadcast_to(dgamma * gammas[i], (CHUNK, HEAD_K))
        dkd = dkd_all[...]
        dk_ref[...] = (dkd * dec).astype(dk_ref.dtype)
        dla = dgg_all[...] + _tri_dot(strict_ref[...], dkd * kd)
        dfg = dla * (1.0 / TAU) * _sigmoid(-fg)
        dfg_ref[...] = dfg.astype(dfg_ref.dtype)
        db_ref[...] += jnp.broadcast_to(jnp.sum(dfg, axis=0, keepdims=True), db_ref.shape)

    q_spec, k_spec, v_spec, fz_spec, w_spec, b_spec, mask_spec = _gla_specs(rows)
    q_spec, k_spec, v_spec, fz_spec = [
        pl.BlockSpec(sp.block_shape, rev(sp.index_map)) for sp in (q_spec, k_spec, v_spec, fz_spec)]
    do_spec = pl.BlockSpec((rows, HEAD_V), lambda h, j: (nblk - 1 - j, h))
    st_spec = pl.BlockSpec((1, per, HEAD_V, HEAD_K), lambda h, j: (h, nblk - 1 - j, 0, 0))
    prev_spec = pl.BlockSpec((1, 1, HEAD_V, HEAD_K),
                             lambda h, j: (h, jnp.maximum((nblk - 1 - j) * per - 1, 0), 0, 0))
    key_out = pl.BlockSpec((rows, HEAD_K), lambda h, j: (nblk - 1 - j, h))
    r_spec = pl.BlockSpec((rows, HEAD_V), lambda h, j: (nblk - 1 - j, SEG_R * (D_MODEL // HEAD_V) + h))
    return pl.pallas_call(
        body, grid=(HEADS, nblk),
        in_specs=[q_spec, k_spec, v_spec, fz_spec, w_spec, b_spec, mask_spec, mask_spec,
                  do_spec, do_spec, r_spec, pl.BlockSpec((1, HEAD_V), lambda h, j: (0, 0)),
                  st_spec, prev_spec, _ANY],
        out_specs=[key_out, key_out, do_spec, key_out, pl.BlockSpec((8, HEAD_K), lambda h, j: (0, h)),
                   r_spec, pl.BlockSpec((8, HEAD_V), lambda h, j: (0, 0))],
        out_shape=[jax.ShapeDtypeStruct((s, KEY_W), BF16), jax.ShapeDtypeStruct((s, KEY_W), BF16),
                   jax.ShapeDtypeStruct((s, VAL_W), BF16), jax.ShapeDtypeStruct((s, KEY_W), BF16),
                   jax.ShapeDtypeStruct((8, KEY_W), F32), jax.ShapeDtypeStruct(dproj.shape, BF16),
                   jax.ShapeDtypeStruct((8, HEAD_V), F32)],
        input_output_aliases={14: 5},
        scratch_shapes=[pltpu.VMEM((HEAD_V, HEAD_K), F32), pltpu.VMEM((per, HEAD_V, HEAD_K), F32),
                        pltpu.VMEM((rows, HEAD_K), F32), pltpu.VMEM((rows, HEAD_K), F32)], name=name,
        compiler_params=_cparams(("arbitrary", "arbitrary")),
    )(proj, proj, proj, proj, wfg, bfg, incl, strict, doa, o, proj, gla_g, states, states, dproj)


def _place_qkv(dq, dk, dv, dproj, name):
    s = dq.shape[0]
    tm = _pick(s, (512, 256))

    def body(dq_ref, dk_ref, dv_ref, _, o_ref):
        o_ref[:, 0:KEY_W] = dq_ref[...]
        o_ref[:, KEY_W:2 * KEY_W] = dk_ref[...]
        o_ref[:, 2 * KEY_W:2 * KEY_W + VAL_W] = dv_ref[...]

    def rows(width):
        return pl.BlockSpec((tm, width), lambda i: (i, 0))

    return pl.pallas_call(
        body, grid=(s // tm,), in_specs=[rows(KEY_W), rows(KEY_W), rows(VAL_W), _ANY],
        out_specs=rows(2 * KEY_W + VAL_W), out_shape=jax.ShapeDtypeStruct(dproj.shape, BF16),
        input_output_aliases={3: 0}, name=name, compiler_params=_cparams(("parallel",)),
    )(dq, dk, dv, dproj)


def _seg(tm, seg):
    return pl.BlockSpec((tm, D_MODEL), lambda i: (i, seg))


HALO = SUBLANES_BF16


def _shift_down(u, p1, p2, n, rows):
    rolled = pltpu.roll(u, n, 0)
    if n == 1:
        return jnp.where(rows == 0, p1, rolled)
    return jnp.where(rows == 0, p2, jnp.where(rows == 1, p1, rolled))


def _shift_up(u, n1, n2, n, rows, tm):
    rolled = pltpu.roll(u, tm - n, 0)
    if n == 1:
        return jnp.where(rows == tm - 1, n1, rolled)
    return jnp.where(rows == tm - 2, n1, jnp.where(rows == tm - 1, n2, rolled))


def _conv_terms(gc_ref, cx_ref, gcp_ref, cxp_ref, tm):
    i = pl.program_id(0)
    u = gc_ref[...].astype(F32) * cx_ref[...].astype(F32)
    up = gcp_ref[...].astype(F32) * cxp_ref[...].astype(F32) * (i > 0).astype(F32)
    rows = lax.broadcasted_iota(jnp.int32, (tm, 1), 0)
    u1 = _shift_down(u, up[HALO - 1:HALO, :], up[HALO - 2:HALO - 1, :], 1, rows)
    u2 = _shift_down(u, up[HALO - 1:HALO, :], up[HALO - 2:HALO - 1, :], 2, rows)
    return u, u1, u2, rows


def _prev_halo(tm, seg):
    return pl.BlockSpec((HALO, D_MODEL), lambda i: (jnp.maximum(i * (tm // HALO) - 1, 0), seg))


def _conv_fwd(proj, w, b, name):
    s = proj.shape[0]
    tm = _pick(s, (512, 256))

    def body(gbi_ref, gc_ref, cx_ref, gcp_ref, cxp_ref, w_ref, b_ref, cb_ref):
        u, u1, u2, _ = _conv_terms(gc_ref, cx_ref, gcp_ref, cxp_ref, tm)
        conv = w_ref[0:1, :] * u2 + w_ref[1:2, :] * u1 + w_ref[2:3, :] * u + b_ref[...]
        cb_ref[...] = (gbi_ref[...].astype(F32) * conv).astype(cb_ref.dtype)

    return pl.pallas_call(
        body, grid=(s // tm,),
        in_specs=[_seg(tm, SEG_GBI), _seg(tm, SEG_GCI), _seg(tm, SEG_CX),
                  _prev_halo(tm, SEG_GCI), _prev_halo(tm, SEG_CX),
                  pl.BlockSpec((3, D_MODEL), lambda i: (0, 0)), pl.BlockSpec((1, D_MODEL), lambda i: (0, 0))],
        out_specs=pl.BlockSpec((tm, D_MODEL), lambda i: (i, 0)),
        out_shape=jax.ShapeDtypeStruct((s, D_MODEL), BF16), name=name, compiler_params=_cparams(("parallel",)),
    )(proj, proj, proj, proj, proj, w, b)


def _conv_bwd(dcb, proj, w, b, dproj, name):
    s = proj.shape[0]
    tm = _pick(s, (512, 256))
    nt = s // tm

    def body(dcb_ref, gbi_ref, gc_ref, cx_ref, gcp_ref, cxp_ref, dcbn_ref, gbin_ref, w_ref, b_ref, _,
             d3_ref, dwb_ref):
        i = pl.program_id(0)

        @pl.when(i == 0)
        def _():
            dwb_ref[...] = jnp.zeros_like(dwb_ref)

        u, u1, u2, rows = _conv_terms(gc_ref, cx_ref, gcp_ref, cxp_ref, tm)
        w0, w1, w2 = w_ref[0:1, :], w_ref[1:2, :], w_ref[2:3, :]
        conv = w0 * u2 + w1 * u1 + w2 * u + b_ref[...]
        dcbv = dcb_ref[...].astype(F32)
        gbi = gbi_ref[...].astype(F32)
        dconv = dcbv * gbi
        dnext = dcbn_ref[...].astype(F32) * gbin_ref[...].astype(F32) * (i < nt - 1).astype(F32)
        dc1 = _shift_up(dconv, dnext[0:1, :], dnext[1:2, :], 1, rows, tm)
        dc2 = _shift_up(dconv, dnext[0:1, :], dnext[1:2, :], 2, rows, tm)
        du = w2 * dconv + w1 * dc1 + w0 * dc2
        d3_ref[:, 0:D_MODEL] = (dcbv * conv).astype(d3_ref.dtype)
        d3_ref[:, D_MODEL:2 * D_MODEL] = (du * cx_ref[...].astype(F32)).astype(d3_ref.dtype)
        d3_ref[:, 2 * D_MODEL:3 * D_MODEL] = (du * gc_ref[...].astype(F32)).astype(d3_ref.dtype)
        dwb_ref[0:1, :] += jnp.sum(dconv * u2, axis=0, keepdims=True)
        dwb_ref[1:2, :] += jnp.sum(dconv * u1, axis=0, keepdims=True)
        dwb_ref[2:3, :] += jnp.sum(dconv * u, axis=0, keepdims=True)
        dwb_ref[3:4, :] += jnp.sum(dconv, axis=0, keepdims=True)

    def next_halo(seg_fn):
        return pl.BlockSpec((HALO, D_MODEL), lambda i: (jnp.minimum((i + 1) * (tm // HALO), s // HALO - 1), seg_fn))

    return pl.pallas_call(
        body, grid=(nt,),
        in_specs=[pl.BlockSpec((tm, D_MODEL), lambda i: (i, 0)),
                  _seg(tm, SEG_GBI), _seg(tm, SEG_GCI), _seg(tm, SEG_CX),
                  _prev_halo(tm, SEG_GCI), _prev_halo(tm, SEG_CX),
                  next_halo(0), next_halo(SEG_GBI),
                  pl.BlockSpec((3, D_MODEL), lambda i: (0, 0)), pl.BlockSpec((1, D_MODEL), lambda i: (0, 0)), _ANY],
        out_specs=[pl.BlockSpec((tm, 3 * D_MODEL), lambda i: (i, SEG_GBI // 3)),
                   pl.BlockSpec((8, D_MODEL), lambda i: (0, 0))],
        out_shape=[jax.ShapeDtypeStruct(dproj.shape, BF16), jax.ShapeDtypeStruct((8, D_MODEL), F32)],
        input_output_aliases={10: 0}, name=name, compiler_params=_cparams(("arbitrary",)),
    )(dcb, proj, proj, proj, proj, proj, dcb, proj, w, b, dproj)


def _mix_fwd(proj, ya, yb, name):
    s = proj.shape[0]
    tm = _pick(s, (512, 256))

    def body(ga_ref, gb_ref, ya_ref, yb_ref, o_ref):
        o_ref[...] = (_sigmoid(ga_ref[...].astype(F32)) * ya_ref[...].astype(F32)
                      + _sigmoid(gb_ref[...].astype(F32)) * yb_ref[...].astype(F32)).astype(o_ref.dtype)

    row = pl.BlockSpec((tm, D_MODEL), lambda i: (i, 0))
    return pl.pallas_call(
        body, grid=(s // tm,), in_specs=[_seg(tm, SEG_GA), _seg(tm, SEG_GB), row, row], out_specs=row,
        out_shape=jax.ShapeDtypeStruct((s, D_MODEL), BF16), name=name, compiler_params=_cparams(("parallel",)),
    )(proj, proj, ya, yb)


def _mix_bwd(dmix, proj, ya, yb, name):
    s = proj.shape[0]
    tm = _pick(s, (512, 256))

    def body(dm_ref, ga_ref, gb_ref, ya_ref, yb_ref, dg_ref, dya_ref, dyb_ref):
        dm = dm_ref[...].astype(F32)
        sa = _sigmoid(ga_ref[...].astype(F32))
        sb = _sigmoid(gb_ref[...].astype(F32))
        dg_ref[:, 0:D_MODEL] = (dm * ya_ref[...].astype(F32) * sa * (1.0 - sa)).astype(dg_ref.dtype)
        dg_ref[:, D_MODEL:2 * D_MODEL] = (dm * yb_ref[...].astype(F32) * sb * (1.0 - sb)).astype(dg_ref.dtype)
        dya_ref[...] = (dm * sa).astype(dya_ref.dtype)
        dyb_ref[...] = (dm * sb).astype(dyb_ref.dtype)

    row = pl.BlockSpec((tm, D_MODEL), lambda i: (i, 0))
    return pl.pallas_call(
        body, grid=(s // tm,), in_specs=[row, _seg(tm, SEG_GA), _seg(tm, SEG_GB), row, row],
        out_specs=[pl.BlockSpec((tm, 2 * D_MODEL), lambda i: (i, SEG_GA // 2)), row, row],
        out_shape=[jax.ShapeDtypeStruct((s, PROJ_W), BF16), jax.ShapeDtypeStruct((s, D_MODEL), BF16),
                   jax.ShapeDtypeStruct((s, D_MODEL), BF16)],
        name=name, compiler_params=_cparams(("parallel",)),
    )(dmix, proj, proj, ya, yb)


def _swiglu_fwd(gu, name):
    s = gu.shape[0]
    tm = _pick(s, (256,))

    def body(gu_ref, o_ref):
        gate = gu_ref[:, 0:FFN].astype(F32)
        o_ref[...] = (gate * _sigmoid(gate) * gu_ref[:, FFN:2 * FFN].astype(F32)).astype(o_ref.dtype)

    return pl.pallas_call(
        body, grid=(s // tm,), in_specs=[pl.BlockSpec((tm, 2 * FFN), lambda i: (i, 0))],
        out_specs=pl.BlockSpec((tm, FFN), lambda i: (i, 0)),
        out_shape=jax.ShapeDtypeStruct((s, FFN), BF16), name=name, compiler_params=_cparams(("parallel",)),
    )(gu)


def _swiglu_bwd(dhid, gu, name):
    s = gu.shape[0]
    tm = _pick(s, (256,))

    def body(dh_ref, gu_ref, o_ref):
        gate = gu_ref[:, 0:FFN].astype(F32)
        up = gu_ref[:, FFN:2 * FFN].astype(F32)
        dh = dh_ref[...].astype(F32)
        sg = _sigmoid(gate)
        o_ref[:, 0:FFN] = (dh * up * (sg * (1.0 + gate * (1.0 - sg)))).astype(o_ref.dtype)
        o_ref[:, FFN:2 * FFN] = (dh * gate * sg).astype(o_ref.dtype)

    wide = pl.BlockSpec((tm, 2 * FFN), lambda i: (i, 0))
    return pl.pallas_call(
        body, grid=(s // tm,), in_specs=[pl.BlockSpec((tm, FFN), lambda i: (i, 0)), wide], out_specs=wide,
        out_shape=jax.ShapeDtypeStruct((s, 2 * FFN), BF16), name=name, compiler_params=_cparams(("parallel",)),
    )(dhid, gu)


def _adamw_math(w, g, m, v):
    m2 = ADAM_B1 * m + (1.0 - ADAM_B1) * g
    v2 = ADAM_B2 * v + (1.0 - ADAM_B2) * (g * g)
    m_hat = m2 / (1.0 - ADAM_B1 ** ADAM_STEP)
    v_hat = v2 / (1.0 - ADAM_B2 ** ADAM_STEP)
    delta = -ADAM_LR * (m_hat / (jnp.sqrt(v_hat) + ADAM_EPS) + ADAM_WD * w)
    return delta, m2, v2


def _adamw(w, g, m, v, name):
    shape = w.shape
    cols = shape[-1]
    rows = int(np.prod(shape[:-1])) if len(shape) > 1 else 1
    w2, g2, m2, v2 = [t.reshape(rows, cols) for t in (w, g, m, v)]
    tr = _pick(rows, (512, 352, 256)) if rows % 8 == 0 else rows

    def body(w_ref, g_ref, m_ref, v_ref, d_ref, nm_ref, nv_ref):
        d, nm, nv = _adamw_math(w_ref[...], g_ref[...], m_ref[...], v_ref[...])
        d_ref[...] = d
        nm_ref[...] = nm
        nv_ref[...] = nv

    blk = pl.BlockSpec((tr, cols), lambda i: (i, 0))
    out = pl.pallas_call(
        body, grid=(rows // tr,), in_specs=[blk] * 4, out_specs=[blk] * 3,
        out_shape=[jax.ShapeDtypeStruct((rows, cols), F32)] * 3, name=name,
        compiler_params=_cparams(("parallel",)),
    )(w2, g2, m2, v2)
    return [t.reshape(shape) for t in out]


def _pair_sum(g2, recv, core, name, after=()):
    _, nchip, r, c = g2.shape
    tr = _pick(r, (512,))

    def body(core_ref, a_ref, b_ref, *rest):
        o_ref = rest[-1]
        o_ref[...] = (a_ref[...].astype(F32) + b_ref[...].astype(F32)).astype(o_ref.dtype)

    grid_spec = pltpu.PrefetchScalarGridSpec(
        num_scalar_prefetch=1, grid=(nchip, r // tr),
        in_specs=[pl.BlockSpec((None, None, tr, c), lambda k, i, cr: (cr[0], k, i, 0)),
                  pl.BlockSpec((None, tr, c), lambda k, i, cr: (k, i, 0))] + [_after_spec(t) for t in after],
        out_specs=pl.BlockSpec((None, tr, c), lambda k, i, cr: (k, i, 0)))
    return pl.pallas_call(
        body, grid_spec=grid_spec, out_shape=jax.ShapeDtypeStruct((nchip, r, c), BF16), name=name,
        compiler_params=_cparams(("parallel", "parallel")),
    )(core, g2, recv, *after)


def _chip_sum(a, recv, chip, name, after=()):
    _, r, c = a.shape
    tr = _pick(r, (512,))

    def body(chip_ref, a_ref, b_ref, *rest):
        o_ref = rest[-1]
        o_ref[...] = ((a_ref[...].astype(F32) + b_ref[0].astype(F32)) + b_ref[1].astype(F32)) + b_ref[2].astype(F32)

    grid_spec = pltpu.PrefetchScalarGridSpec(
        num_scalar_prefetch=1, grid=(r // tr,),
        in_specs=[pl.BlockSpec((None, tr, c), lambda i, cr: (cr[0], i, 0)),
                  pl.BlockSpec((3, tr, c), lambda i, cr: (0, i, 0))] + [_after_spec(t) for t in after],
        out_specs=pl.BlockSpec((tr, c), lambda i, cr: (i, 0)))
    return pl.pallas_call(
        body, grid_spec=grid_spec, out_shape=jax.ShapeDtypeStruct((r, c), F32), name=name,
        compiler_params=_cparams(("parallel",)),
    )(chip, a, recv, *after)


def _sum_devices(parts, name):
    n, r, c = parts.shape

    def body(p_ref, o_ref):
        acc = p_ref[0]
        for d in range(1, n):
            acc = acc + p_ref[d]
        o_ref[...] = acc

    return pl.pallas_call(
        body, out_shape=jax.ShapeDtypeStruct((r, c), F32), name=name,
        in_specs=[pl.BlockSpec(memory_space=pltpu.VMEM)], out_specs=pl.BlockSpec(memory_space=pltpu.VMEM),
    )(parts)


def _lane_iota():
    return lax.broadcasted_iota(jnp.int32, (1, LANES), 1)


def _tiles_up(tiles, s, lane):
    rolled = [pltpu.roll(t, s, 1) for t in tiles]
    zero = jnp.zeros_like(tiles[0])
    return [jnp.where(lane < s, p, c) for p, c in zip([zero] + rolled, rolled + [zero])]


def _tiles_down(tiles, s, lane):
    back = (LANES - s) % LANES
    rolled = [pltpu.roll(t, back, 1) for t in tiles]
    zero = jnp.zeros_like(tiles[0])
    return [jnp.where(lane < LANES - s, c, n) for c, n in zip(rolled, rolled[1:] + [zero])]


def _window_cols(prm, w_in, w_gate, w_up, l, name, after=None):
    tm = 256
    n_in, n_gu = WIN_IN // LANES, FFN_SHARD // LANES + 1

    def body(prm_ref, win_ref, g_ref, u_ref, *rest):
        out_ref, gu_ref, scr_in, scr_gu = rest[-4:]
        lane = _lane_iota()
        s_main, lo, hi, s_code, clo, chi, code_hi, s_gu = [prm_ref[i] for i in range(8)]
        scr_in[:, D_MODEL:WIN_IN] = jnp.zeros((tm, WIN_IN - D_MODEL), F32)
        scr_in[:, 0:W_IN_SHARD] = win_ref[...]

        def keep(t, a, b):
            col = lane + t * LANES
            return jnp.where((col >= a) & (col < b), scr_in[:, t * LANES:(t + 1) * LANES], 0.0)

        main = _tiles_up([keep(t, lo, hi) for t in range(n_in)], s_main, lane)
        for t in range(n_in):
            out_ref[:, t * LANES:(t + 1) * LANES] = main[t].astype(BF16)
        low = _tiles_up([keep(0, clo, chi)], s_code, lane)[0]
        high = _tiles_up([keep(n_in - 2, clo, chi), keep(n_in - 1, clo, chi)], s_code, lane)[1]
        out_ref[:, A_FZ:A_FZ + LANES] = jnp.where(code_hi == 1, high, low).astype(BF16)
        for ref, base in ((g_ref, 0), (u_ref, WIN_GU)):
            scr_gu[:, (n_gu - 1) * LANES:n_gu * LANES] = jnp.zeros((tm, LANES), F32)
            scr_gu[:, 0:FFN_SHARD] = ref[...]
            moved = _tiles_up([scr_gu[:, t * LANES:(t + 1) * LANES] for t in range(n_gu)], s_gu, lane)
            for t in range(n_gu + 1):
                gu_ref[:, base + t * LANES:base + (t + 1) * LANES] = moved[t].astype(BF16)

    after_args = [] if after is None else [after]
    grid_spec = pltpu.PrefetchScalarGridSpec(
        num_scalar_prefetch=1, grid=(D_MODEL // tm,),
        in_specs=[pl.BlockSpec((None, tm, W_IN_SHARD), lambda i, p: (l, i, 0)),
                  pl.BlockSpec((None, tm, FFN_SHARD), lambda i, p: (l, i, 0)),
                  pl.BlockSpec((None, tm, FFN_SHARD), lambda i, p: (l, i, 0))] + [_after_spec(t) for t in after_args],
        out_specs=[pl.BlockSpec((tm, IN_COLS), lambda i, p: (i, 0)), pl.BlockSpec((tm, GU_COLS), lambda i, p: (i, 0))],
        scratch_shapes=[pltpu.VMEM((tm, WIN_IN), F32), pltpu.VMEM((tm, n_gu * LANES), F32)])
    return pl.pallas_call(
        body, grid_spec=grid_spec, name=name, compiler_params=_cparams(("parallel",)),
        out_shape=[jax.ShapeDtypeStruct((D_MODEL, IN_COLS), BF16), jax.ShapeDtypeStruct((D_MODEL, GU_COLS), BF16)],
    )(prm, w_in, w_gate, w_up, *after_args)


def _gu_width(j):
    return min(WIN_GU, FFN - T_GU[j] * LANES)


def _after_spec(t):
    tile = (SUBLANES_BF16 if t.dtype == BF16 else 8, LANES)
    return pl.BlockSpec((None,) * (t.ndim - 2) + tile, lambda *_: (0,) * t.ndim)


def _assemble_in(a_all, tail_all, name, after=()):
    tm = D_MODEL // N_DEV

    def body(a_ref, t_ref, *rest):
        win_ref, tail_ref = rest[-2:]
        tail_ref[...] = t_ref[...]
        win_ref[...] = jnp.zeros_like(win_ref)
        code = a_ref[0, :, A_FZ:A_FZ + LANES]
        for j in range(N_DEV):
            c0 = T_IN[j] * LANES
            win_ref[:, c0:c0 + WIN_IN] += a_ref[j, :, 0:WIN_IN]
            if j > 0:
                code = code + a_ref[j, :, A_FZ:A_FZ + LANES]
        win_ref[:, FZ_COL:PROJ_W] = code

    return pl.pallas_call(
        body, grid=(N_DEV,),
        in_specs=[pl.BlockSpec((N_DEV, tm, IN_COLS), lambda i: (0, i, 0)),
                  pl.BlockSpec((None,) + tail_all.shape[1:], lambda i: (i, 0, 0))] + [_after_spec(t) for t in after],
        out_specs=[pl.BlockSpec((tm, PROJ_W), lambda i: (i, 0)),
                   pl.BlockSpec((None,) + tail_all.shape[1:], lambda i: (i, 0, 0))],
        out_shape=[jax.ShapeDtypeStruct((D_MODEL, PROJ_W), BF16), jax.ShapeDtypeStruct(tail_all.shape, BF16)],
        name=name, compiler_params=_cparams(("parallel",)),
    )(a_all, tail_all, *after)


def _assemble_rest(a_all, rows_all, name, after=()):
    tm = D_MODEL // N_DEV
    n_in = 2 + len(after)

    def body(*refs):
        a_ref, r_ref = refs[:2]
        wgu_ref, oa_ref, ob_ref, o_ref, down_ref = refs[n_in:]
        wgu_ref[...] = jnp.zeros_like(wgu_ref)
        for j in range(N_DEV):
            g0, width = T_GU[j] * LANES, _gu_width(j)
            wgu_ref[:, g0:g0 + width] += a_ref[j, :, 0:width]
            wgu_ref[:, FFN + g0:FFN + g0 + width] += a_ref[j, :, WIN_GU:WIN_GU + width]
        oa_ref[...] = r_ref[B_OA:B_OA + ROW_SHARD, :]
        ob_ref[...] = r_ref[B_OB:B_OB + ROW_SHARD, :]
        o_ref[...] = r_ref[B_O:B_O + ROW_SHARD, :]
        down_ref[...] = r_ref[B_DOWN:B_DOWN + FFN_SHARD, :]

    def rows(n):
        return pl.BlockSpec((n, D_MODEL), lambda i: (i, 0))

    square = jax.ShapeDtypeStruct((D_MODEL, D_MODEL), BF16)
    return pl.pallas_call(
        body, grid=(N_DEV,),
        in_specs=[pl.BlockSpec((N_DEV, tm, GU_COLS), lambda i: (0, i, 0)),
                  pl.BlockSpec((None, B_FG2, D_MODEL), lambda i: (i, 0, 0))] + [_after_spec(t) for t in after],
        out_specs=[pl.BlockSpec((tm, 2 * FFN), lambda i: (i, 0)),
                   rows(ROW_SHARD), rows(ROW_SHARD), rows(ROW_SHARD), rows(FFN_SHARD)],
        out_shape=[jax.ShapeDtypeStruct((D_MODEL, 2 * FFN), BF16),
                   square, square, square, jax.ShapeDtypeStruct((FFN, D_MODEL), BF16)],
        name=name, compiler_params=_cparams(("parallel",)),
    )(a_all, rows_all, *after)


def _grad_windows_in(d_in, name):
    tm = 256

    def body(din_ref, out_ref):
        for j in range(N_DEV):
            c0 = T_IN[j] * LANES
            out_ref[j & 1, j >> 1, :, 0:WIN_IN] = din_ref[:, c0:c0 + WIN_IN]
            out_ref[j & 1, j >> 1, :, A_FZ:IN_COLS] = din_ref[:, FZ_COL:PROJ_W]

    return pl.pallas_call(
        body, grid=(D_MODEL // tm,), in_specs=[pl.BlockSpec((tm, PROJ_W), lambda i: (i, 0))],
        out_specs=pl.BlockSpec((2, 4, tm, IN_COLS), lambda i: (0, 0, i, 0)),
        out_shape=jax.ShapeDtypeStruct((2, 4, D_MODEL, IN_COLS), BF16), name=name,
        compiler_params=_cparams(("parallel",)),
    )(d_in)


def _grad_windows_gu(d_gu, name):
    tm = 256

    def body(dgu_ref, out_ref):
        for j in range(N_DEV):
            g0, width = T_GU[j] * LANES, _gu_width(j)
            for half, base in ((0, 0), (FFN, WIN_GU)):
                out_ref[j & 1, j >> 1, :, base:base + width] = dgu_ref[:, half + g0:half + g0 + width]
                if width < WIN_GU:
                    out_ref[j & 1, j >> 1, :, base + width:base + WIN_GU] = jnp.zeros((tm, WIN_GU - width), BF16)

    return pl.pallas_call(
        body, grid=(D_MODEL // tm,), in_specs=[pl.BlockSpec((tm, 2 * FFN), lambda i: (i, 0))],
        out_specs=pl.BlockSpec((2, 4, tm, GU_COLS), lambda i: (0, 0, i, 0)),
        out_shape=jax.ShapeDtypeStruct((2, 4, D_MODEL, GU_COLS), BF16), name=name,
        compiler_params=_cparams(("parallel",)),
    )(d_gu)


def _final_windows(prm, chip, a, recv, params, l, kind, name, prev=None):
    tm = 128
    n_in, n_gu = WIN_IN // LANES, FFN_SHARD // LANES + 1
    widths = (W_IN_SHARD,) if kind == "in" else (FFN_SHARD, FFN_SHARD)
    cols = IN_COLS if kind == "in" else GU_COLS
    n_par = 3 * len(widths)
    prev = [] if prev is None else [t for group in prev for t in group]

    def body(prm_ref, chip_ref, a_ref, r_ref, *refs):
        ins, outs = refs[:n_par], refs[n_par + len(prev):]
        lane = _lane_iota()
        s_main, s_code, clo, chi, s_gu = [prm_ref[i] for i in (0, 3, 4, 5, 7)]

        def total(c0):
            sl = slice(c0, c0 + LANES)
            return ((a_ref[:, sl].astype(F32) + r_ref[0, :, sl].astype(F32)) + r_ref[1, :, sl].astype(F32)) \
                + r_ref[2, :, sl].astype(F32)

        if kind == "in":
            grads = _tiles_down([total(t * LANES) for t in range(n_in)], s_main, lane)
            code = pltpu.roll(total(A_FZ), (LANES - s_code) % LANES, 1)
            for t in (0, n_in - 2, n_in - 1):
                col = lane + t * LANES
                grads[t] = jnp.where((col >= clo) & (col < chi), code, grads[t])
            per_weight = [grads]
        else:
            per_weight = [_tiles_down([total(base + t * LANES) for t in range(n_gu + 1)], s_gu, lane)[:n_gu]
                          for base in (0, WIN_GU)]
        for k, (tiles, width) in enumerate(zip(per_weight, widths)):
            w_ref, m_ref, v_ref = ins[3 * k:3 * k + 3]
            g_ref, d_ref, nm_ref, nv_ref = outs[4 * k:4 * k + 4]
            for t, g in enumerate(tiles):
                n = min(LANES, width - t * LANES)
                sl = slice(t * LANES, t * LANES + n)
                g = g[:, 0:n]
                d, nm, nv = _adamw_math(w_ref[:, sl], g, m_ref[:, sl], v_ref[:, sl])
                g_ref[:, sl] = g
                d_ref[:, sl] = d
                nm_ref[:, sl] = nm
                nv_ref[:, sl] = nv

    def native(width):
        return pl.BlockSpec((None, tm, width), lambda i, p, c: (l, i, 0))

    in_specs = [pl.BlockSpec((None, tm, cols), lambda i, p, c: (c[0], i, 0)),
                pl.BlockSpec((3, tm, cols), lambda i, p, c: (0, i, 0))]
    in_specs += [native(wd) for wd in widths for _ in range(3)] + [_ANY] * len(prev)
    grid_spec = pltpu.PrefetchScalarGridSpec(
        num_scalar_prefetch=2, grid=(D_MODEL // tm,), in_specs=in_specs,
        out_specs=[native(wd) for wd in widths for _ in range(4)])
    out = pl.pallas_call(
        body, grid_spec=grid_spec, name=name, compiler_params=_cparams(("parallel",)),
        out_shape=[jax.ShapeDtypeStruct((DEPTH, D_MODEL, wd), F32) for wd in widths for _ in range(4)],
        input_output_aliases={4 + n_par + k: k for k in range(len(prev))},
    )(prm, chip, a, recv, *params, *prev)
    return [out[4 * k:4 * k + 4] for k in range(len(widths))]


def _me():
    return lax.axis_index("x"), lax.axis_index("y"), lax.axis_index("c")


_CHIP_FLIPS = ((1, 0), (0, 1), (1, 1))
_ANY = pl.BlockSpec(memory_space=pl.ANY)


def _comm_call(body, peers, out_shape, sems, name, args, collective_id):
    if collective_id is None:
        n_in = len(args)
        return pl.pallas_call(body, out_shape=out_shape, name=name, in_specs=[_ANY] * n_in,
                              out_specs=[_ANY] * len(out_shape), scratch_shapes=sems)(*args)

    def sequencer_body(*refs):
        barrier = pltpu.get_barrier_semaphore()
        targets = peers()
        for peer in targets:
            pl.semaphore_signal(barrier, inc=1, device_id=peer, device_id_type=MESH)
        pl.semaphore_wait(barrier, len(targets))
        body(*refs)

    sequencer = plsc.ScalarSubcoreMesh(axis_name="seq", num_cores=1)
    return pl.kernel(sequencer_body, out_type=out_shape, mesh=sequencer, scratch_types=sems, name=name,
                     compiler_params=pltpu.CompilerParams(collective_id=collective_id))(*args)


def _sibling_peer():
    x, y, cc = _me()
    return [(x, y, 1 - cc)]


def _chip_peers():
    x, y, cc = _me()
    return [(x ^ fx, y ^ fy, cc) for fx, fy in _CHIP_FLIPS]


def _all_gather(shards, name, collective_id=None):
    n = len(shards)
    split = [s.shape[0] % (2 * SUBLANES_BF16) == 0 for s in shards]

    def body(*refs):
        x_refs, out_refs = refs[:n], refs[n:2 * n]
        send_sems, recv_sems, local_sems = refs[2 * n:]
        x, y, cc = _me()
        me, sibling = (x, y, cc), (x, y, 1 - cc)
        near, far = [(x ^ 1, y), (x, y ^ 1)], (x ^ 1, y ^ 1)

        def copy(a, k, block, to, half=None, from_shard=False):
            px, py, pc = block
            slot = out_refs[a].at[4 * px + 2 * py + pc]
            if half is not None:
                rows = shards[a].shape[0] // 2
                slot = slot.at[pl.ds(half * rows, rows)]
            return pltpu.make_async_remote_copy(
                src_ref=x_refs[a] if from_shard else slot, dst_ref=slot,
                send_sem=send_sems.at[a, k], recv_sem=recv_sems.at[a, k], device_id=to, device_id_type=MESH)

        mine = [pltpu.make_async_copy(x_refs[a], out_refs[a].at[4 * x + 2 * y + cc], local_sems.at[a])
                for a in range(n)]
        for cp in mine:
            cp.start()
        sent = [copy(a, 0, me, sibling, from_shard=True) for a in range(n)]
        sent += [copy(a, 1 + j, me, (*chip, cc), from_shard=True) for j, chip in enumerate(near) for a in range(n)]
        sent += [copy(a, 3, me, (*far, cc), from_shard=True) for a in range(n) if not split[a]]
        for cp in sent:
            cp.start()

        def pass_on(cp):
            cp.start()
            sent.append(cp)

        for j, chip in enumerate(near):
            for a in range(n):
                copy(a, 1 + j, (*chip, cc), me).wait_recv()
                pass_on(copy(a, 4 + j, (*chip, cc), sibling))
                if split[a]:
                    pass_on(copy(a, 7 + j, (*chip, cc), (*near[1 - j], cc), half=j))
        for a in range(n):
            if split[a]:
                copy(a, 7, (*far, cc), me, half=0).wait_recv()
                copy(a, 8, (*far, cc), me, half=1).wait_recv()
            else:
                copy(a, 3, (*far, cc), me).wait_recv()
            pass_on(copy(a, 6, (*far, cc), sibling))
        for a in range(n):
            copy(a, 0, sibling, me).wait_recv()
            for j, chip in enumerate(near + [far]):
                copy(a, 4 + j, (*chip, 1 - cc), me).wait_recv()
        for cp in sent:
            cp.wait_send()
        for cp in mine:
            cp.wait()

    return _comm_call(
        body, lambda: _sibling_peer() + _chip_peers(),
        [jax.ShapeDtypeStruct((N_DEV,) + s.shape, s.dtype) for s in shards],
        [pltpu.SemaphoreType.DMA((n, 9)), pltpu.SemaphoreType.DMA((n, 9)), pltpu.SemaphoreType.DMA((n,))],
        name, shards, collective_id)


def _send_to_sibling(parts, name, collective_id=None):
    n = len(parts)

    def body(*refs):
        g_refs, out_refs = refs[:n], refs[n:2 * n]
        send_sems, recv_sems = refs[2 * n:]
        x, y, cc = _me()
        copies = [pltpu.make_async_remote_copy(
            src_ref=g_refs[a].at[1 - cc], dst_ref=out_refs[a], send_sem=send_sems.at[a], recv_sem=recv_sems.at[a],
            device_id=(x, y, 1 - cc), device_id_type=MESH) for a in range(n)]
        for cp in copies:
            cp.start()
        for cp in copies:
            cp.wait()

    return _comm_call(
        body, _sibling_peer, [jax.ShapeDtypeStruct(p.shape[1:], p.dtype) for p in parts],
        [pltpu.SemaphoreType.DMA((n,)), pltpu.SemaphoreType.DMA((n,))], name, parts, collective_id)


def _send_to_chips(parts, name, collective_id=None):
    n = len(parts)

    def body(*refs):
        a_refs, out_refs = refs[:n], refs[n:2 * n]
        send_sems, recv_sems = refs[2 * n:]
        x, y, cc = _me()
        copies = []
        for k, (fx, fy) in enumerate(_CHIP_FLIPS):
            px, py = x ^ fx, y ^ fy
            for a in range(n):
                copies.append(pltpu.make_async_remote_copy(
                    src_ref=a_refs[a].at[2 * px + py], dst_ref=out_refs[a].at[k], send_sem=send_sems.at[a, k],
                    recv_sem=recv_sems.at[a, k], device_id=(px, py, cc), device_id_type=MESH))
                copies[-1].start()
        for cp in copies:
            cp.wait()

    return _comm_call(
        body, _chip_peers, [jax.ShapeDtypeStruct((3,) + p.shape[1:], p.dtype) for p in parts],
        [pltpu.SemaphoreType.DMA((n, 3)), pltpu.SemaphoreType.DMA((n, 3))], name, parts, collective_id)


def _pack_rows(w_oa, w_ob, w_o, w_down, w_fg2, conv_w, l):
    conv_bits = lax.bitcast_convert_type(conv_w[l].reshape(-1), BF16).reshape(1, -1)
    tail = jnp.concatenate([w_fg2[l].astype(BF16).reshape(1, D_MODEL),
                            jnp.pad(conv_bits, ((0, 0), (0, D_MODEL - conv_bits.shape[1])))], axis=0)
    tail = jnp.pad(tail, ((0, B_ROWS - B_FG2 - tail.shape[0]), (0, 0)))
    rows = jnp.concatenate([w_oa[l].astype(BF16), w_ob[l].astype(BF16), w_o[l].astype(BF16),
                            w_down[l].astype(BF16)], axis=0)
    return rows, tail


def _unpack_tail(tail):
    w_fg2 = tail[:, 0, :].reshape(N_DEV, RANK, KEY_W // N_DEV).transpose(1, 0, 2).reshape(RANK, KEY_W)
    conv_bits = tail[:, B_CONV - B_FG2, :2 * 3 * ROW_SHARD].reshape(N_DEV, 3 * ROW_SHARD, 2)
    conv_w = lax.bitcast_convert_type(conv_bits, F32).reshape(N_DEV, 3, ROW_SHARD)
    return jnp.pad(w_fg2, ((0, LANES - RANK), (0, 0))), conv_w.transpose(1, 0, 2).reshape(3, D_MODEL)


def _by_core_chip(t):
    return t.reshape((2, 2, 2) + t.shape[1:]).transpose((2, 0, 1) + tuple(range(3, t.ndim + 2))).reshape(
        (2, 4) + t.shape[1:])


def _grad_rows(g):
    fg2 = g["w_fg2"][:RANK].reshape(RANK, N_DEV, KEY_W // N_DEV).transpose(1, 0, 2).reshape(N_DEV, 1, D_MODEL)
    conv = g["conv_w"].astype(BF16).reshape(3, N_DEV, ROW_SHARD).transpose(1, 0, 2).reshape(N_DEV, 1, 3 * ROW_SHARD)
    tail = jnp.concatenate([fg2, jnp.pad(conv, ((0, 0), (0, 0), (0, D_MODEL - 3 * ROW_SHARD)))], axis=1)
    tail = jnp.pad(tail, ((0, 0), (0, B_ROWS - B_FG2 - 2), (0, 0)))
    parts = [g["w_oa"].reshape(N_DEV, ROW_SHARD, D_MODEL), g["w_ob"].reshape(N_DEV, ROW_SHARD, D_MODEL),
             g["w_o"].reshape(N_DEV, ROW_SHARD, D_MODEL), g["w_down"].reshape(N_DEV, FFN_SHARD, D_MODEL), tail]
    return _by_core_chip(jnp.concatenate(parts, axis=1))


def _ungrad_rows(gs):
    return dict(w_oa=gs[B_OA:B_OA + ROW_SHARD], w_ob=gs[B_OB:B_OB + ROW_SHARD], w_o=gs[B_O:B_O + ROW_SHARD],
                w_ffn_down=gs[B_DOWN:B_DOWN + FFN_SHARD], w_fg2=gs[B_FG2].reshape(RANK, KEY_W // N_DEV),
                conv_w=gs[B_CONV, :3 * ROW_SHARD].reshape(3, ROW_SHARD))


def _layer_fwd(x, h, p, l, next_norm_g=None):
    tag = f"l{l}_"
    if h is None:
        h = _rmsnorm_fwd(x, p["norm1_g"], tag + "norm1")
    proj = _matmul(h, p["w_in"], "nn", BF16, tag + "proj")
    o, states, oa = _gla_fwd(proj, p["w_fg2"], p["b_fg"], p["gla_norm_g"], tag + "gla_fwd")
    p.update(p.pop("rest")((o,)))
    ya = _matmul(oa, p["w_oa"], "nn", BF16, tag + "ya")
    cb = _conv_fwd(proj, p["conv_w"], p["conv_b"], tag + "conv")
    yb = _matmul(cb, p["w_ob"], "nn", BF16, tag + "yb")
    mix = _mix_fwd(proj, ya, yb, tag + "mix")
    x1, h2 = _matmul(mix, p["w_o"], "nn", F32, tag + "x1", residual=x, norm_g=p["norm2_g"])
    gu = _matmul(h2, p["w_gu"], "nn", BF16, tag + "gu")
    hid = _swiglu_fwd(gu, tag + "swiglu")
    x2 = _matmul(hid, p["w_down"], "nn", F32, tag + "x2", residual=x1, norm_g=next_norm_g)
    x2, h_next = x2 if next_norm_g is not None else (x2, None)
    saved = dict(x=x, h=h, proj=proj, o=o, states=states, oa=oa, ya=ya, cb=cb, yb=yb, mix=mix, x1=x1, h2=h2,
                 gu=gu, hid=hid)
    return x2, h_next, saved


def _layer_bwd(dx2, dx2h, p, sv, l, reduce=None):
    if reduce is None:
        reduce = lambda group, grads: ((), ())
    tag = f"l{l}_b_"
    dhid = _matmul(dx2h, p["w_down"], "nt", BF16, tag + "dhid")
    d_down = _matmul(sv["hid"], dx2h, "tn", BF16, tag + "dw_down")
    dgu = _swiglu_bwd(dhid, sv["gu"], tag + "swiglu")
    dx1, dx1h, dg2 = _matmul_norm_bwd(dgu, p["w_gu"], sv["x1"], p["norm2_g"], dx2, tag + "dh2")
    d_gu = _matmul(sv["h2"], dgu, "tn", BF16, tag + "dw_gu")
    gu_packed, gu_sums = reduce("gu", d_gu)
    dmix = _matmul(dx1h, p["w_o"], "nt", BF16, tag + "dmix", after=gu_packed)
    d_o = _matmul(sv["mix"], dx1h, "tn", BF16, tag + "dw_o")
    dproj, dya, dyb = _mix_bwd(dmix, sv["proj"], sv["ya"], sv["yb"], tag + "mix")
    dcb = _matmul(dyb, p["w_ob"], "nt", BF16, tag + "dcb", after=gu_sums)
    d_ob = _matmul(sv["cb"], dyb, "tn", BF16, tag + "dw_ob")
    dproj, dwb = _conv_bwd(dcb, sv["proj"], p["conv_w"], p["conv_b"], dproj, tag + "conv")
    doa = _matmul(dya, p["w_oa"], "nt", BF16, tag + "doa")
    d_oa = _matmul(sv["oa"], dya, "tn", BF16, tag + "dw_oa")
    dq, dk, dv, dfg, dbfg, dproj, dgg = _gla_bwd(sv["proj"], p["w_fg2"], p["b_fg"], sv["states"], doa, sv["o"],
                                                 p["gla_norm_g"], dproj, tag + "gla")
    fz = sv["proj"][:, FZ_COL:]
    dproj = _place_qkv(dq, dk, dv, dproj, tag + "place_qkv")
    dproj = _matmul(dfg, p["w_fg2"], "nt", BF16, tag + "dfz", into=(dproj, FZ_COL))
    d_fg2 = _matmul(fz, dfg, "tn", BF16, tag + "dw_fg2")
    rows = dict(w_fg2=d_fg2, conv_w=dwb[0:3], w_oa=d_oa, w_ob=d_ob, w_o=d_o, w_down=d_down)
    rows_packed, rows_sums = reduce("rows", rows)
    d_in = _matmul(sv["h"], dproj, "tn", BF16, tag + "dw_in", after=rows_packed)
    _, in_sums = reduce("in", d_in)
    dx, dxh, dg1 = _matmul_norm_bwd(dproj, p["w_in"], sv["x"], p["norm1_g"], dx1, tag + "dh",
                                    after=(d_in,) + tuple(rows_sums) + tuple(in_sums))
    big = dict(w_in=d_in, w_gu=d_gu, **rows)
    pad = lambda t: jnp.pad(t, ((0, 0), (0, D_MODEL - t.shape[1])))
    small = [dg1[0:1], pad(dbfg[0:1]), pad(dgg[0:1]), dwb[3:4], dg2[0:1]]
    return dx, dxh, big, small


def _local_step(x, target, weights_of, final_g, reduce_of=None):
    saved, layers, h = [], [], None
    for l in range(DEPTH):
        layers.append(weights_of(l, x))
        x, h, sv = _layer_fwd(x, h, layers[l], l, layers[l].get("next_norm1_g"))
        saved.append(sv)
    loss, dx, dxh, dgf = _loss_head(x, final_g, target, "loss_head")
    bigs, smalls = [None] * DEPTH, [None] * DEPTH
    for l in reversed(range(DEPTH)):
        dx, dxh, bigs[l], smalls[l] = _layer_bwd(dx, dxh, layers[l], saved[l], l, reduce_of(l) if reduce_of else None)
    loss_row = jnp.pad(loss[0:1], ((0, 0), (0, D_MODEL - loss.shape[1])))
    small = jnp.concatenate(smalls[0] + smalls[1] + [dgf[0:1], loss_row], axis=0)
    small = jnp.pad(small, ((0, SMALL_ROWS - small.shape[0]), (0, 0)))
    return loss[0, 0], dx, bigs, small


def kernel(x, norm1_g, w_in, w_fg2, b_fg, gla_norm_g, w_oa, conv_w, conv_b, w_ob, w_o, norm2_g, w_ffn_gate, w_ffn_up, w_ffn_down, final_g, loss_target, m_norm1_g, m_w_in, m_w_fg2, m_b_fg, m_gla_norm_g, m_w_oa, m_conv_w, m_conv_b, m_w_ob, m_w_o, m_norm2_g, m_w_ffn_gate, m_w_ffn_up, m_w_ffn_down, m_final_g, v_norm1_g, v_w_in, v_w_fg2, v_b_fg, v_gla_norm_g, v_w_oa, v_conv_w, v_conv_b, v_w_ob, v_w_o, v_norm2_g, v_w_ffn_gate, v_w_ffn_up, v_w_ffn_down, v_final_g):
    names = ["norm1_g", "w_in", "w_fg2", "b_fg", "gla_norm_g", "w_oa", "conv_w", "conv_b", "w_ob", "w_o",
             "norm2_g", "w_ffn_gate", "w_ffn_up", "w_ffn_down", "final_g"]
    w = dict(zip(names, [norm1_g, w_in, w_fg2, b_fg, gla_norm_g, w_oa, conv_w, conv_b, w_ob, w_o, norm2_g,
                         w_ffn_gate, w_ffn_up, w_ffn_down, final_g]))
    m = dict(zip(names, [m_norm1_g, m_w_in, m_w_fg2, m_b_fg, m_gla_norm_g, m_w_oa, m_conv_w, m_conv_b, m_w_ob,
                         m_w_o, m_norm2_g, m_w_ffn_gate, m_w_ffn_up, m_w_ffn_down, m_final_g]))
    v = dict(zip(names, [v_norm1_g, v_w_in, v_w_fg2, v_b_fg, v_gla_norm_g, v_w_oa, v_conv_w, v_conv_b, v_w_ob,
                         v_w_o, v_norm2_g, v_w_ffn_gate, v_w_ffn_up, v_w_ffn_down, v_final_g]))
    col_names = ["w_in", "w_ffn_gate", "w_ffn_up"]
    cx, cy, cc = _me()
    prm = jnp.asarray(SHIFT_TABLE)[4 * cx + 2 * cy + cc]
    core = jnp.reshape(cc, (1,)).astype(jnp.int32)
    chip = jnp.reshape(2 * cx + cy, (1,)).astype(jnp.int32)

    ids = iter(range(32))

    gathered, previous = [], None
    for l in range(DEPTH):
        rows, tail = _pack_rows(w_oa, w_ob, w_o, w_ffn_down, w_fg2, conv_w, l)
        win_in, win_gu = _window_cols(prm, w_in, w_ffn_gate, w_ffn_up, l, f"l{l}_windows", after=previous)
        previous = rows
        first = _all_gather([win_in, tail], f"l{l}_gather_in", next(ids))
        gathered.append(list(first) + list(_all_gather([win_gu, rows], f"l{l}_gather_rest", next(ids))))

    def weights_of(l, x_in):
        all_in, all_tail, all_gu, all_rows = gathered[l]
        after = (x_in,) if l > 0 else ()
        w_in_full, tail = _assemble_in(all_in, all_tail, f"l{l}_assemble_in", after)
        w_fg2_full, conv_w_full = _unpack_tail(tail)

        def rest(after_rest):
            names_rest = ("w_gu", "w_oa", "w_ob", "w_o", "w_down")
            return dict(zip(names_rest, _assemble_rest(all_gu, all_rows, f"l{l}_assemble_rest", after + after_rest)))

        return dict(w_in=w_in_full, rest=rest, w_fg2=w_fg2_full,
                    conv_w=conv_w_full, norm1_g=norm1_g[l][None], b_fg=b_fg[l][None],
                    gla_norm_g=gla_norm_g[l][None], conv_b=conv_b[l][None], norm2_g=norm2_g[l][None],
                    next_norm1_g=norm1_g[l + 1][None] if l + 1 < DEPTH else None)

    pending = [dict() for _ in range(DEPTH)]
    landed = []

    def reduce_of(l):
        def reduce(group, grads):
            tag = f"l{l}_{group}"
            if group == "gu":
                packed = _grad_windows_gu(grads, tag + "_windows")
            elif group == "in":
                packed = _grad_windows_in(grads, tag + "_windows")
            else:
                packed = _grad_rows(grads)
            (from_sibling,) = _send_to_sibling([packed], tag + "_to_sibling", next(ids))
            waited = () if group == "in" else tuple(landed)
            if waited:
                landed.clear()
            sums = _pair_sum(packed, from_sibling, core, tag + "_pair_sum", after=waited)
            (from_chips,) = _send_to_chips([sums], tag + "_to_chips", next(ids))
            landed.append(from_chips)
            pending[l][group] = (sums, from_chips)
            return (packed,), (sums,)
        return reduce

    loss, dx, bigs, small = _local_step(x[0], loss_target[0], weights_of, final_g[None], reduce_of)

    grads, deltas, new_m, new_v = {}, {}, {}, {}
    for kind, group_names in (("gu", col_names[1:]), ("in", col_names[:1])):
        params = [t for n in group_names for t in (w[n], m[n], v[n])]
        out = None
        for l in reversed(range(DEPTH)):
            sums, from_chips = pending[l][kind]
            out = _final_windows(prm, chip, sums, from_chips, params, l, kind, f"l{l}_{kind}_final", out)
        for n, (g, d, nm, nv) in zip(group_names, out):
            grads[n], deltas[n], new_m[n], new_v[n] = g, d, nm, nv
    row_grads = [_ungrad_rows(_chip_sum(*pending[l]["rows"], chip, f"l{l}_rows_chip_sum")) for l in range(DEPTH)]
    for n in row_grads[0]:
        grads[n] = jnp.stack([row_grads[l][n] for l in range(DEPTH)])

    small_sum = _sum_devices(_all_gather([small], "gather_small")[0], "sum_small")
    r512, r256 = slice(0, KEY_W), slice(0, HEAD_V)
    grads.update(
        norm1_g=jnp.stack([small_sum[0], small_sum[5]]), b_fg=jnp.stack([small_sum[1, r512], small_sum[6, r512]]),
        gla_norm_g=jnp.stack([small_sum[2, r256], small_sum[7, r256]]),
        conv_b=jnp.stack([small_sum[3], small_sum[8]]), norm2_g=jnp.stack([small_sum[4], small_sum[9]]),
        final_g=small_sum[10])

    for n in names:
        if n not in col_names:
            deltas[n], new_m[n], new_v[n] = _adamw(w[n], grads[n], m[n], v[n], "adamw_" + n)

    total_loss = small_sum[2 * 5 + 1, 0]
    return (total_loss, dx[None], *[grads[n] for n in names], *[deltas[n] for n in names],
            *[new_m[n] for n in names], *[new_v[n] for n in names])
```

```python
import functools

import jax
import jax.numpy as jnp
import numpy as np
from jax import lax
from jax.experimental import pallas as pl
from jax.experimental.pallas import tpu as pltpu
from jax.experimental.pallas import tpu_sc as plsc

F32 = jnp.float32
BF16 = jnp.bfloat16
MESH = pl.DeviceIdType.MESH

D_MODEL = 1024
DEPTH = 2
CHUNK = 64
HEADS = 4
HEAD_K = 128
HEAD_V = 256
KEY_W = HEADS * HEAD_K
VAL_W = HEADS * HEAD_V
RANK = 16
TAU = 16.0
FFN = 2816
IN_WIDTH = 2 * KEY_W + 2 * VAL_W + RANK + 5 * D_MODEL
EPS = 1e-6
Q_SCALE = HEAD_K ** -0.5
N_DEV = 8
ADAM_LR, ADAM_B1, ADAM_B2, ADAM_EPS, ADAM_WD, ADAM_STEP = 0.001, 0.9, 0.999, 1e-08, 0.01, 10

LANES = 128
SUBLANES_BF16 = 16
VMEM_LIMIT = 48 * 1024 * 1024
VMEM_LIMIT_WIDE = 56 * 1024 * 1024

FZ_COL = 2 * KEY_W + 2 * VAL_W + 5 * D_MODEL
PROJ_W = FZ_COL + LANES
SEG_R, SEG_GBI, SEG_GCI, SEG_CX, SEG_GA, SEG_GB = 2, 3, 4, 5, 6, 7

W_IN_SHARD = IN_WIDTH // N_DEV
FFN_SHARD = FFN // N_DEV
ROW_SHARD = D_MODEL // N_DEV

WIN_IN = 9 * LANES
WIN_GU = 4 * LANES
A_FZ = WIN_IN
IN_COLS = A_FZ + LANES
GU_COLS = 2 * WIN_GU
ORIG_FZ = 2 * KEY_W + 2 * VAL_W


def _new_col(o):
    if o < ORIG_FZ:
        return o
    if o < ORIG_FZ + RANK:
        return FZ_COL + (o - ORIG_FZ)
    return o - RANK


def _shift_table():
    t_in, rows = [], []
    for j in range(N_DEV):
        new = [_new_col(W_IN_SHARD * j + i) for i in range(W_IN_SHARD)]
        main = [i for i in range(W_IN_SHARD) if new[i] < FZ_COL]
        code = [i for i in range(W_IN_SHARD) if new[i] >= FZ_COL]
        shift = new[main[0]] - main[0]
        t_in.append(shift // LANES)
        assert all(new[i] - i == shift for i in main) and shift % LANES + W_IN_SHARD <= WIN_IN
        if code:
            cshift = new[code[0]] - FZ_COL - code[0]
            crow = [cshift % LANES, code[0], code[-1] + 1, int(cshift < 0)]
        else:
            crow = [0, 0, 0, 0]
        rows.append([shift % LANES, main[0], main[-1] + 1] + crow + [FFN_SHARD * j % LANES])
    return tuple(t_in), np.asarray(rows, np.int32)


T_IN, SHIFT_TABLE = _shift_table()
T_GU = tuple(FFN_SHARD * j // LANES for j in range(N_DEV))

B_OA, B_OB, B_O, B_DOWN = 0, ROW_SHARD, 2 * ROW_SHARD, 3 * ROW_SHARD
B_FG2 = B_DOWN + FFN_SHARD
B_CONV = B_FG2 + 1
B_ROWS = B_FG2 + SUBLANES_BF16
SMALL_ROWS = 32


def _pick(n, candidates):
    for c in candidates:
        if n % c == 0:
            return c
    return n


def _cparams(sem):
    return pltpu.CompilerParams(dimension_semantics=sem, vmem_limit_bytes=VMEM_LIMIT)


def _sigmoid(x):
    return 1.0 / (1.0 + jnp.exp(-x))


def _matmul(a, b, dims, out_dtype, name, residual=None, after=(), into=None, norm_g=None):
    if dims == "nn":
        (m, k), (k2, n) = a.shape, b.shape
    elif dims == "nt":
        (m, k), (n, k2) = a.shape, b.shape
    else:
        (k, m), (k2, n) = a.shape, b.shape
    assert k == k2, (a.shape, b.shape, dims)
    tm = _pick(m, (1024, 1408, 512, 256, 128))
    tn = _pick(n, (1664, 1408, 1024, 512, 256, 128))
    tk = _pick(k, (1664, 1408, 1024, 512, 256, 128))
    nk = k // tk
    if dims == "nn":
        a_spec = pl.BlockSpec((tm, tk), lambda i, j, kk: (i, kk))
        b_spec = pl.BlockSpec((tk, tn), lambda i, j, kk: (kk, j))
        contract = (((1,), (0,)), ((), ()))
    elif dims == "nt":
        a_spec = pl.BlockSpec((tm, tk), lambda i, j, kk: (i, kk))
        b_spec = pl.BlockSpec((tn, tk), lambda i, j, kk: (j, kk))
        contract = (((1,), (1,)), ((), ()))
    else:
        a_spec = pl.BlockSpec((tk, tm), lambda i, j, kk: (kk, i))
        b_spec = pl.BlockSpec((tk, tn), lambda i, j, kk: (kk, j))
        contract = (((0,), (0,)), ((), ()))
    o_spec = pl.BlockSpec((tm, tn), lambda i, j, kk: (i, j))
    has_res = residual is not None
    out_spec, out_struct, placed, aliases = o_spec, jax.ShapeDtypeStruct((m, n), out_dtype), (), {}
    if into is not None:
        buffer, col = into
        assert col % tn == 0 and buffer.dtype == out_dtype and not has_res
        out_spec = pl.BlockSpec((tm, tn), lambda i, j, kk: (i, col // tn + j))
        out_struct, placed, aliases = jax.ShapeDtypeStruct(buffer.shape, out_dtype), (buffer,), {2 + len(after): 0}

    has_norm = norm_g is not None
    assert not has_norm or (tn == n and into is None)

    def body(*refs):
        a_ref, b_ref = refs[:2]
        r_ref = refs[2] if has_res else None
        n_in = 2 + has_res + has_norm + len(after) + len(placed)
        o_ref = refs[n_in]
        kk = pl.program_id(2)
        part = lax.dot_general(a_ref[...], b_ref[...], contract, preferred_element_type=F32)

        def finish(total):
            if has_res:
                total = total + r_ref[...]
            o_ref[...] = total.astype(o_ref.dtype)
            if has_norm:
                rstd = lax.rsqrt(jnp.mean(total * total, axis=-1, keepdims=True) + EPS)
                refs[n_in + 1][...] = (total * rstd * refs[2 + has_res][...]).astype(BF16)

        if nk == 1:
            finish(part)
            return
        acc_ref = refs[-1]

        @pl.when(kk == 0)
        def _():
            acc_ref[...] = part

        @pl.when((kk > 0) & (kk < nk - 1))
        def _():
            acc_ref[...] += part

        @pl.when(kk == nk - 1)
        def _():
            finish(acc_ref[...] + part)

    in_specs = [a_spec, b_spec] + ([o_spec] if has_res else [])
    in_specs += [pl.BlockSpec((1, n), lambda i, j, kk: (0, 0))] if has_norm else []
    in_specs += [_after_spec(t) for t in after] + [_ANY] * len(placed)
    args = (a, b) + ((residual,) if has_res else ()) + ((norm_g,) if has_norm else ()) + tuple(after) + placed
    if has_norm:
        out_spec, out_struct = [out_spec, o_spec], [out_struct, jax.ShapeDtypeStruct((m, n), BF16)]
    return pl.pallas_call(
        body, grid=(m // tm, n // tn, nk), in_specs=in_specs, out_specs=out_spec,
        out_shape=out_struct, input_output_aliases=aliases,
        scratch_shapes=[pltpu.VMEM((tm, tn), F32)] if nk > 1 else [], name=name,
        compiler_params=_cparams(("parallel", "parallel", "arbitrary")),
    )(*args)


def _rmsnorm_fwd(x, g, name):
    s, d = x.shape
    tm = _pick(s, (512, 256))

    def body(x_ref, g_ref, o_ref):
        xv = x_ref[...]
        r = lax.rsqrt(jnp.mean(xv * xv, axis=-1, keepdims=True) + EPS)
        o_ref[...] = (xv * r * g_ref[...]).astype(o_ref.dtype)

    row = pl.BlockSpec((tm, d), lambda i: (i, 0))
    return pl.pallas_call(
        body, grid=(s // tm,), in_specs=[row, pl.BlockSpec((1, d), lambda i: (0, 0))], out_specs=row,
        out_shape=jax.ShapeDtypeStruct((s, d), BF16), name=name, compiler_params=_cparams(("parallel",)),
    )(x, g)


def _matmul_norm_bwd(a, b, x, g, dres, name, after=()):
    (s, k), (d, k2) = a.shape, b.shape
    assert k == k2 and x.shape == (s, d)
    tm = _pick(s, (1024, 512, 256))
    tk = _pick(k, (1664, 1408, 1024, 512, 256, 128))
    nk = k // tk

    def body(a_ref, b_ref, x_ref, g_ref, dres_ref, *rest):
        dx_ref, dx16_ref, dg_ref, acc_ref = rest[len(after):]
        i, kk = pl.program_id(0), pl.program_id(1)
        part = lax.dot_general(a_ref[...], b_ref[...], _NT, preferred_element_type=F32)

        def finish(dh):
            xv = x_ref[...]
            r = lax.rsqrt(jnp.mean(xv * xv, axis=-1, keepdims=True) + EPS)
            xn = xv * r
            dxn = dh * g_ref[...]
            dx = dres_ref[...] + r * (dxn - xn * jnp.mean(dxn * xn, axis=-1, keepdims=True))
            dx_ref[...] = dx
            dx16_ref[...] = dx.astype(BF16)
            dg = jnp.broadcast_to(jnp.sum(dh * xn, axis=0, keepdims=True), dg_ref.shape)

            @pl.when(i == 0)
            def _():
                dg_ref[...] = dg

            @pl.when(i > 0)
            def _():
                dg_ref[...] += dg

        @pl.when(kk == 0)
        def _():
            acc_ref[...] = part

        @pl.when((kk > 0) & (kk < nk - 1))
        def _():
            acc_ref[...] += part

        @pl.when(kk == nk - 1)
        def _():
            finish(acc_ref[...] + part)

    assert nk > 1
    row = pl.BlockSpec((tm, d), lambda i, kk: (i, 0))
    return pl.pallas_call(
        body, grid=(s // tm, nk),
        in_specs=[pl.BlockSpec((tm, tk), lambda i, kk: (i, kk)), pl.BlockSpec((d, tk), lambda i, kk: (0, kk)), row,
                  pl.BlockSpec((1, d), lambda i, kk: (0, 0)), row] + [_after_spec(t) for t in after],
        out_specs=[row, row, pl.BlockSpec((8, d), lambda i, kk: (0, 0))],
        out_shape=[jax.ShapeDtypeStruct((s, d), F32), jax.ShapeDtypeStruct((s, d), BF16),
                   jax.ShapeDtypeStruct((8, d), F32)],
        scratch_shapes=[pltpu.VMEM((tm, d), F32)], name=name,
        compiler_params=pltpu.CompilerParams(dimension_semantics=("arbitrary", "arbitrary"),
                                             vmem_limit_bytes=VMEM_LIMIT_WIDE),
    )(a, b, x, g, dres, *after)


def _loss_head(x, g, target, name):
    s, d = x.shape
    tm = _pick(s, (512, 256))

    def body(x_ref, g_ref, t_ref, loss_ref, dx_ref, dx16_ref, dg_ref):
        xv = x_ref[...]
        gv = g_ref[...]
        r = lax.rsqrt(jnp.mean(xv * xv, axis=-1, keepdims=True) + EPS)
        xn = xv * r
        err = xn * gv - t_ref[...]
        dy = err * (1.0 / d)
        dxn = dy * gv
        dx = r * (dxn - xn * jnp.mean(dxn * xn, axis=-1, keepdims=True))
        dx_ref[...] = dx
        dx16_ref[...] = dx.astype(BF16)

        @pl.when(pl.program_id(0) == 0)
        def _():
            dg_ref[...] = jnp.zeros_like(dg_ref)
            loss_ref[...] = jnp.zeros_like(loss_ref)

        dg_ref[...] += jnp.broadcast_to(jnp.sum(dy * xn, axis=0, keepdims=True), dg_ref.shape)
        row_loss = jnp.sum(err * err, axis=-1, keepdims=True)
        loss_ref[...] += jnp.broadcast_to((0.5 / d) * jnp.sum(row_loss, axis=0, keepdims=True), loss_ref.shape)

    row = pl.BlockSpec((tm, d), lambda i: (i, 0))
    return pl.pallas_call(
        body, grid=(s // tm,), in_specs=[row, pl.BlockSpec((1, d), lambda i: (0, 0)), row],
        out_specs=[pl.BlockSpec((8, LANES), lambda i: (0, 0)), row, row, pl.BlockSpec((8, d), lambda i: (0, 0))],
        out_shape=[jax.ShapeDtypeStruct((8, LANES), F32), jax.ShapeDtypeStruct((s, d), F32),
                   jax.ShapeDtypeStruct((s, d), BF16), jax.ShapeDtypeStruct((8, d), F32)],
        name=name, compiler_params=_cparams(("arbitrary",)),
    )(x, g, target)


def _chunk_cumsum(x):
    pos = lax.broadcasted_iota(jnp.int32, (x.shape[0], 1), 0) % CHUNK
    shift = 1
    while shift < CHUNK:
        x = x + jnp.where(pos >= shift, pltpu.roll(x, shift, 0), 0.0)
        shift *= 2
    return x


def _block_decay(fz, w, b):
    fg = jnp.dot(fz, w, preferred_element_type=F32) + b
    la = (jnp.minimum(fg, 0.0) - jnp.log(1.0 + jnp.exp(-jnp.abs(fg)))) * (1.0 / TAU)
    cum = _chunk_cumsum(la)
    ends = [cum[i + CHUNK - 1:i + CHUNK, :] for i in range(0, fz.shape[0], CHUNK)]
    end = jnp.concatenate([jnp.broadcast_to(e, (CHUNK, e.shape[1])) for e in ends], axis=0)
    return fg, jnp.exp(end - cum), [jnp.exp(e) for e in ends]


_TN = (((0,), (0,)), ((), ()))
_NT = (((1,), (1,)), ((), ()))


def _gla_specs(rows):
    q_spec = pl.BlockSpec((rows, HEAD_K), lambda h, c: (c, h))
    k_spec = pl.BlockSpec((rows, HEAD_K), lambda h, c: (c, HEADS + h))
    v_spec = pl.BlockSpec((rows, HEAD_V), lambda h, c: (c, HEADS + h))
    fz_spec = pl.BlockSpec((rows, LANES), lambda h, c: (c, FZ_COL // LANES))
    w_spec = pl.BlockSpec((LANES, HEAD_K), lambda h, c: (0, h))
    b_spec = pl.BlockSpec((1, HEAD_K), lambda h, c: (0, h))
    return q_spec, k_spec, v_spec, fz_spec, w_spec, b_spec


def _gla_fwd(proj, wfg, bfg, gla_g, name):
    s = proj.shape[0]
    nc = s // CHUNK
    per = _pick(nc, (8, 4, 2, 1))
    rows = per * CHUNK

    def body(q_ref, k_ref, v_ref, fz_ref, w_ref, b_ref, r_ref, g_ref, o_ref, st_ref, oa_ref, state, update):
        @pl.when(pl.program_id(1) == 0)
        def _():
            state[...] = jnp.zeros_like(state)

        _, dec, gammas = _block_decay(fz_ref[...], w_ref[...], b_ref[...])
        kd = (k_ref[...].astype(F32) * dec).astype(BF16)
        qs = (q_ref[...].astype(F32) * Q_SCALE).astype(BF16)
        for i in range(per):
            sl = slice(i * CHUNK, (i + 1) * CHUNK)
            update[i] = lax.dot_general(v_ref[sl, :], kd[sl], _TN, preferred_element_type=F32)
        st = state[...]
        for i in range(per):
            st = st * gammas[i] + update[i]
            st_ref[0, i] = st.astype(BF16)
        state[...] = st
        for i in range(per):
            sl = slice(i * CHUNK, (i + 1) * CHUNK)
            o_ref[sl, :] = lax.dot_general(qs[sl], st_ref[0, i], _NT, preferred_element_type=F32).astype(o_ref.dtype)
        ov = o_ref[...].astype(F32)
        rstd = lax.rsqrt(jnp.mean(ov * ov, axis=-1, keepdims=True) + EPS)
        rv = r_ref[...].astype(F32)
        oa_ref[...] = (ov * rstd * g_ref[...] * (rv * _sigmoid(rv))).astype(oa_ref.dtype)

    q_spec, k_spec, v_spec, fz_spec, w_spec, b_spec = _gla_specs(rows)
    head_v = pl.BlockSpec((rows, HEAD_V), lambda h, c: (c, h))
    return pl.pallas_call(
        body, grid=(HEADS, nc // per),
        in_specs=[q_spec, k_spec, v_spec, fz_spec, w_spec, b_spec,
                  pl.BlockSpec((rows, HEAD_V), lambda h, c: (c, SEG_R * (D_MODEL // HEAD_V) + h)),
                  pl.BlockSpec((1, HEAD_V), lambda h, c: (0, 0))],
        out_specs=[head_v, pl.BlockSpec((1, per, HEAD_V, HEAD_K), lambda h, c: (h, c, 0, 0)), head_v],
        out_shape=[jax.ShapeDtypeStruct((s, VAL_W), BF16),
                   jax.ShapeDtypeStruct((HEADS, nc, HEAD_V, HEAD_K), BF16), jax.ShapeDtypeStruct((s, VAL_W), BF16)],
        scratch_shapes=[pltpu.VMEM((HEAD_V, HEAD_K), F32), pltpu.VMEM((per, HEAD_V, HEAD_K), F32)], name=name,
        compiler_params=_cparams(("parallel", "arbitrary")),
    )(proj, proj, proj, proj, wfg, bfg, proj, gla_g)


def _gla_bwd(proj, wfg, bfg, states, doa, o, gla_g, dproj, name):
    s = proj.shape[0]
    nc = s // CHUNK
    per = _pick(nc, (8, 4, 2, 1))
    rows = per * CHUNK
    nblk = nc // per

    def rev(spec_fn):
        return lambda h, j: spec_fn(h, nblk - 1 - j)

    def body(q_ref, k_ref, v_ref, fz_ref, w_ref, b_ref, doa_ref, o_ref, r_ref, g_ref,
             st_ref, prev_ref, _, dq_ref, dk_ref, dv_ref, dfg_ref, db_ref, dr_ref, dg_ref,
             carry, gt_all, dkd_all, dgg_all):
        j = pl.program_id(1)

        @pl.when(j == 0)
        def _():
            carry[...] = jnp.zeros_like(carry)
            db_ref[...] = jnp.zeros_like(db_ref)

        @pl.when((j == 0) & (pl.program_id(0) == 0))
        def _():
            dg_ref[...] = jnp.zeros_like(dg_ref)

        ov = o_ref[...].astype(F32)
        rstd = lax.rsqrt(jnp.mean(ov * ov, axis=-1, keepdims=True) + EPS)
        ohat = ov * rstd
        rv = r_ref[...].astype(F32)
        sg = _sigmoid(rv)
        doav = doa_ref[...].astype(F32)
        gv = g_ref[...]
        dr_ref[...] = (doav * ohat * gv * (sg * (1.0 + rv * (1.0 - sg)))).astype(dr_ref.dtype)
        don = doav * (rv * sg)
        dg_ref[...] += jnp.broadcast_to(jnp.sum(don * ohat, axis=0, keepdims=True), dg_ref.shape)
        dohat = don * gv
        do = (rstd * (dohat - ohat * jnp.mean(dohat * ohat, axis=-1, keepdims=True))).astype(BF16)

        fg, dec, gammas = _block_decay(fz_ref[...], w_ref[...], b_ref[...])
        kd = k_ref[...].astype(F32) * dec
        kd16 = kd.astype(BF16)
        qs = (q_ref[...].astype(F32) * Q_SCALE).astype(BF16)
        for i in range(per):
            sl = slice(i * CHUNK, (i + 1) * CHUNK)
            gt_all[i] = lax.dot_general(do[sl], qs[sl], _TN, preferred_element_type=F32)
        back = carry[...]
        for i in reversed(range(per)):
            gt = back + gt_all[i]
            gt_all[i] = gt
            back = gt * gammas[i]
        carry[...] = back
        has_prev = (j < nblk - 1).astype(F32)
        for i in range(per):
            sl = slice(i * CHUNK, (i + 1) * CHUNK)
            gt = gt_all[i]
            gt16 = gt.astype(BF16)
            dq_ref[sl, :] = (jnp.dot(do[sl], st_ref[0, i], preferred_element_type=F32) * Q_SCALE).astype(dq_ref.dtype)
            dkd_all[sl, :] = jnp.dot(v_ref[sl, :], gt16, preferred_element_type=F32)
            dv_ref[sl, :] = lax.dot_general(kd16[sl], gt16, _NT, preferred_element_type=F32).astype(dv_ref.dtype)
            if i > 0:
                st_prev = st_ref[0, i - 1].astype(F32)
            else:
                st_prev = prev_ref[0, 0].astype(F32) * has_prev
            dgamma = jnp.sum(gt * st_prev, axis=0, keepdims=True)
            dgg_all[sl, :] = jnp.broadcast_to(dgamma * gammas[i], (CHUNK, HEAD_K))
        dkd = dkd_all[...]
        dk_ref[...] = (dkd * dec).astype(dk_ref.dtype)
        e = dkd * kd
        dla = dgg_all[...] + (_chunk_cumsum(e) - e)
        dfg = dla * (1.0 / TAU) * _sigmoid(-fg)
        dfg_ref[...] = dfg.astype(dfg_ref.dtype)
        db_ref[...] += jnp.broadcast_to(jnp.sum(dfg, axis=0, keepdims=True), db_ref.shape)

    q_spec, k_spec, v_spec, fz_spec, w_spec, b_spec = _gla_specs(rows)
    q_spec, k_spec, v_spec, fz_spec = [
        pl.BlockSpec(sp.block_shape, rev(sp.index_map)) for sp in (q_spec, k_spec, v_spec, fz_spec)]
    do_spec = pl.BlockSpec((rows, HEAD_V), lambda h, j: (nblk - 1 - j, h))
    st_spec = pl.BlockSpec((1, per, HEAD_V, HEAD_K), lambda h, j: (h, nblk - 1 - j, 0, 0))
    prev_spec = pl.BlockSpec((1, 1, HEAD_V, HEAD_K),
                             lambda h, j: (h, jnp.maximum((nblk - 1 - j) * per - 1, 0), 0, 0))
    key_out = pl.BlockSpec((rows, HEAD_K), lambda h, j: (nblk - 1 - j, h))
    r_spec = pl.BlockSpec((rows, HEAD_V), lambda h, j: (nblk - 1 - j, SEG_R * (D_MODEL // HEAD_V) + h))
    return pl.pallas_call(
        body, grid=(HEADS, nblk),
        in_specs=[q_spec, k_spec, v_spec, fz_spec, w_spec, b_spec,
                  do_spec, do_spec, r_spec, pl.BlockSpec((1, HEAD_V), lambda h, j: (0, 0)),
                  st_spec, prev_spec, _ANY],
        out_specs=[key_out, key_out, do_spec, key_out, pl.BlockSpec((8, HEAD_K), lambda h, j: (0, h)),
                   r_spec, pl.BlockSpec((8, HEAD_V), lambda h, j: (0, 0))],
        out_shape=[jax.ShapeDtypeStruct((s, KEY_W), BF16), jax.ShapeDtypeStruct((s, KEY_W), BF16),
                   jax.ShapeDtypeStruct((s, VAL_W), BF16), jax.ShapeDtypeStruct((s, KEY_W), BF16),
                   jax.ShapeDtypeStruct((8, KEY_W), F32), jax.ShapeDtypeStruct(dproj.shape, BF16),
                   jax.ShapeDtypeStruct((8, HEAD_V), F32)],
        input_output_aliases={12: 5},
        scratch_shapes=[pltpu.VMEM((HEAD_V, HEAD_K), F32), pltpu.VMEM((per, HEAD_V, HEAD_K), F32),
                        pltpu.VMEM((rows, HEAD_K), F32), pltpu.VMEM((rows, HEAD_K), F32)], name=name,
        compiler_params=_cparams(("arbitrary", "arbitrary")),
    )(proj, proj, proj, proj, wfg, bfg, doa, o, proj, gla_g, states, states, dproj)


def _place_qkv(dq, dk, dv, dproj, name):
    s = dq.shape[0]
    tm = _pick(s, (512, 256))

    def body(dq_ref, dk_ref, dv_ref, _, o_ref):
        o_ref[:, 0:KEY_W] = dq_ref[...]
        o_ref[:, KEY_W:2 * KEY_W] = dk_ref[...]
        o_ref[:, 2 * KEY_W:2 * KEY_W + VAL_W] = dv_ref[...]

    def rows(width):
        return pl.BlockSpec((tm, width), lambda i: (i, 0))

    return pl.pallas_call(
        body, grid=(s // tm,), in_specs=[rows(KEY_W), rows(KEY_W), rows(VAL_W), _ANY],
        out_specs=rows(2 * KEY_W + VAL_W), out_shape=jax.ShapeDtypeStruct(dproj.shape, BF16),
        input_output_aliases={3: 0}, name=name, compiler_params=_cparams(("parallel",)),
    )(dq, dk, dv, dproj)


def _seg(tm, seg):
    return pl.BlockSpec((tm, D_MODEL), lambda i: (i, seg))


HALO = SUBLANES_BF16


def _shift_down(u, p1, p2, n, rows):
    rolled = pltpu.roll(u, n, 0)
    if n == 1:
        return jnp.where(rows == 0, p1, rolled)
    return jnp.where(rows == 0, p2, jnp.where(rows == 1, p1, rolled))


def _shift_up(u, n1, n2, n, rows, tm):
    rolled = pltpu.roll(u, tm - n, 0)
    if n == 1:
        return jnp.where(rows == tm - 1, n1, rolled)
    return jnp.where(rows == tm - 2, n1, jnp.where(rows == tm - 1, n2, rolled))


def _conv_terms(gc_ref, cx_ref, gcp_ref, cxp_ref, tm):
    i = pl.program_id(0)
    u = gc_ref[...].astype(F32) * cx_ref[...].astype(F32)
    up = gcp_ref[...].astype(F32) * cxp_ref[...].astype(F32) * (i > 0).astype(F32)
    rows = lax.broadcasted_iota(jnp.int32, (tm, 1), 0)
    u1 = _shift_down(u, up[HALO - 1:HALO, :], up[HALO - 2:HALO - 1, :], 1, rows)
    u2 = _shift_down(u, up[HALO - 1:HALO, :], up[HALO - 2:HALO - 1, :], 2, rows)
    return u, u1, u2, rows


def _prev_halo(tm, seg):
    return pl.BlockSpec((HALO, D_MODEL), lambda i: (jnp.maximum(i * (tm // HALO) - 1, 0), seg))


def _conv_fwd(proj, w, b, name):
    s = proj.shape[0]
    tm = _pick(s, (512, 256))

    def body(gbi_ref, gc_ref, cx_ref, gcp_ref, cxp_ref, w_ref, b_ref, cb_ref):
        u, u1, u2, _ = _conv_terms(gc_ref, cx_ref, gcp_ref, cxp_ref, tm)
        conv = w_ref[0:1, :] * u2 + w_ref[1:2, :] * u1 + w_ref[2:3, :] * u + b_ref[...]
        cb_ref[...] = (gbi_ref[...].astype(F32) * conv).astype(cb_ref.dtype)

    return pl.pallas_call(
        body, grid=(s // tm,),
        in_specs=[_seg(tm, SEG_GBI), _seg(tm, SEG_GCI), _seg(tm, SEG_CX),
                  _prev_halo(tm, SEG_GCI), _prev_halo(tm, SEG_CX),
                  pl.BlockSpec((3, D_MODEL), lambda i: (0, 0)), pl.BlockSpec((1, D_MODEL), lambda i: (0, 0))],
        out_specs=pl.BlockSpec((tm, D_MODEL), lambda i: (i, 0)),
        out_shape=jax.ShapeDtypeStruct((s, D_MODEL), BF16), name=name, compiler_params=_cparams(("parallel",)),
    )(proj, proj, proj, proj, proj, w, b)


def _conv_bwd(dcb, proj, w, b, dproj, name):
    s = proj.shape[0]
    tm = _pick(s, (512, 256))
    nt = s // tm

    def body(dcb_ref, gbi_ref, gc_ref, cx_ref, gcp_ref, cxp_ref, dcbn_ref, gbin_ref, w_ref, b_ref, _,
             d3_ref, dwb_ref):
        i = pl.program_id(0)

        @pl.when(i == 0)
        def _():
            dwb_ref[...] = jnp.zeros_like(dwb_ref)

        u, u1, u2, rows = _conv_terms(gc_ref, cx_ref, gcp_ref, cxp_ref, tm)
        w0, w1, w2 = w_ref[0:1, :], w_ref[1:2, :], w_ref[2:3, :]
        conv = w0 * u2 + w1 * u1 + w2 * u + b_ref[...]
        dcbv = dcb_ref[...].astype(F32)
        gbi = gbi_ref[...].astype(F32)
        dconv = dcbv * gbi
        dnext = dcbn_ref[...].astype(F32) * gbin_ref[...].astype(F32) * (i < nt - 1).astype(F32)
        dc1 = _shift_up(dconv, dnext[0:1, :], dnext[1:2, :], 1, rows, tm)
        dc2 = _shift_up(dconv, dnext[0:1, :], dnext[1:2, :], 2, rows, tm)
        du = w2 * dconv + w1 * dc1 + w0 * dc2
        d3_ref[:, 0:D_MODEL] = (dcbv * conv).astype(d3_ref.dtype)
        d3_ref[:, D_MODEL:2 * D_MODEL] = (du * cx_ref[...].astype(F32)).astype(d3_ref.dtype)
        d3_ref[:, 2 * D_MODEL:3 * D_MODEL] = (du * gc_ref[...].astype(F32)).astype(d3_ref.dtype)
        dwb_ref[0:1, :] += jnp.sum(dconv * u2, axis=0, keepdims=True)
        dwb_ref[1:2, :] += jnp.sum(dconv * u1, axis=0, keepdims=True)
        dwb_ref[2:3, :] += jnp.sum(dconv * u, axis=0, keepdims=True)
        dwb_ref[3:4, :] += jnp.sum(dconv, axis=0, keepdims=True)

    def next_halo(seg_fn):
        return pl.BlockSpec((HALO, D_MODEL), lambda i: (jnp.minimum((i + 1) * (tm // HALO), s // HALO - 1), seg_fn))

    return pl.pallas_call(
        body, grid=(nt,),
        in_specs=[pl.BlockSpec((tm, D_MODEL), lambda i: (i, 0)),
                  _seg(tm, SEG_GBI), _seg(tm, SEG_GCI), _seg(tm, SEG_CX),
                  _prev_halo(tm, SEG_GCI), _prev_halo(tm, SEG_CX),
                  next_halo(0), next_halo(SEG_GBI),
                  pl.BlockSpec((3, D_MODEL), lambda i: (0, 0)), pl.BlockSpec((1, D_MODEL), lambda i: (0, 0)), _ANY],
        out_specs=[pl.BlockSpec((tm, 3 * D_MODEL), lambda i: (i, SEG_GBI // 3)),
                   pl.BlockSpec((8, D_MODEL), lambda i: (0, 0))],
        out_shape=[jax.ShapeDtypeStruct(dproj.shape, BF16), jax.ShapeDtypeStruct((8, D_MODEL), F32)],
        input_output_aliases={10: 0}, name=name, compiler_params=_cparams(("arbitrary",)),
    )(dcb, proj, proj, proj, proj, proj, dcb, proj, w, b, dproj)


def _mix_fwd(proj, ya, yb, name):
    s = proj.shape[0]
    tm = _pick(s, (512, 256))

    def body(ga_ref, gb_ref, ya_ref, yb_ref, o_ref):
        o_ref[...] = (_sigmoid(ga_ref[...].astype(F32)) * ya_ref[...].astype(F32)
                      + _sigmoid(gb_ref[...].astype(F32)) * yb_ref[...].astype(F32)).astype(o_ref.dtype)

    row = pl.BlockSpec((tm, D_MODEL), lambda i: (i, 0))
    return pl.pallas_call(
        body, grid=(s // tm,), in_specs=[_seg(tm, SEG_GA), _seg(tm, SEG_GB), row, row], out_specs=row,
        out_shape=jax.ShapeDtypeStruct((s, D_MODEL), BF16), name=name, compiler_params=_cparams(("parallel",)),
    )(proj, proj, ya, yb)


def _mix_bwd(dmix, proj, ya, yb, name):
    s = proj.shape[0]
    tm = _pick(s, (512, 256))

    def body(dm_ref, ga_ref, gb_ref, ya_ref, yb_ref, dg_ref, dya_ref, dyb_ref):
        dm = dm_ref[...].astype(F32)
        sa = _sigmoid(ga_ref[...].astype(F32))
        sb = _sigmoid(gb_ref[...].astype(F32))
        dg_ref[:, 0:D_MODEL] = (dm * ya_ref[...].astype(F32) * sa * (1.0 - sa)).astype(dg_ref.dtype)
        dg_ref[:, D_MODEL:2 * D_MODEL] = (dm * yb_ref[...].astype(F32) * sb * (1.0 - sb)).astype(dg_ref.dtype)
        dya_ref[...] = (dm * sa).astype(dya_ref.dtype)
        dyb_ref[...] = (dm * sb).astype(dyb_ref.dtype)

    row = pl.BlockSpec((tm, D_MODEL), lambda i: (i, 0))
    return pl.pallas_call(
        body, grid=(s // tm,), in_specs=[row, _seg(tm, SEG_GA), _seg(tm, SEG_GB), row, row],
        out_specs=[pl.BlockSpec((tm, 2 * D_MODEL), lambda i: (i, SEG_GA // 2)), row, row],
        out_shape=[jax.ShapeDtypeStruct((s, PROJ_W), BF16), jax.ShapeDtypeStruct((s, D_MODEL), BF16),
                   jax.ShapeDtypeStruct((s, D_MODEL), BF16)],
        name=name, compiler_params=_cparams(("parallel",)),
    )(dmix, proj, proj, ya, yb)


def _swiglu_fwd(gu, name):
    s = gu.shape[0]
    tm = _pick(s, (256,))

    def body(gu_ref, o_ref):
        gate = gu_ref[:, 0:FFN].astype(F32)
        o_ref[...] = (gate * _sigmoid(gate) * gu_ref[:, FFN:2 * FFN].astype(F32)).astype(o_ref.dtype)

    return pl.pallas_call(
        body, grid=(s // tm,), in_specs=[pl.BlockSpec((tm, 2 * FFN), lambda i: (i, 0))],
        out_specs=pl.BlockSpec((tm, FFN), lambda i: (i, 0)),
        out_shape=jax.ShapeDtypeStruct((s, FFN), BF16), name=name, compiler_params=_cparams(("parallel",)),
    )(gu)


def _swiglu_bwd(dhid, gu, name):
    s = gu.shape[0]
    tm = _pick(s, (256,))

    def body(dh_ref, gu_ref, o_ref):
        gate = gu_ref[:, 0:FFN].astype(F32)
        up = gu_ref[:, FFN:2 * FFN].astype(F32)
        dh = dh_ref[...].astype(F32)
        sg = _sigmoid(gate)
        o_ref[:, 0:FFN] = (dh * up * (sg * (1.0 + gate * (1.0 - sg)))).astype(o_ref.dtype)
        o_ref[:, FFN:2 * FFN] = (dh * gate * sg).astype(o_ref.dtype)

    wide = pl.BlockSpec((tm, 2 * FFN), lambda i: (i, 0))
    return pl.pallas_call(
        body, grid=(s // tm,), in_specs=[pl.BlockSpec((tm, FFN), lambda i: (i, 0)), wide], out_specs=wide,
        out_shape=jax.ShapeDtypeStruct((s, 2 * FFN), BF16), name=name, compiler_params=_cparams(("parallel",)),
    )(dhid, gu)


def _adamw_math(w, g, m, v):
    m2 = ADAM_B1 * m + (1.0 - ADAM_B1) * g
    v2 = ADAM_B2 * v + (1.0 - ADAM_B2) * (g * g)
    m_hat = m2 / (1.0 - ADAM_B1 ** ADAM_STEP)
    v_hat = v2 / (1.0 - ADAM_B2 ** ADAM_STEP)
    delta = -ADAM_LR * (m_hat / (jnp.sqrt(v_hat) + ADAM_EPS) + ADAM_WD * w)
    return delta, m2, v2


def _adamw(w, g, m, v, name):
    shape = w.shape
    cols = shape[-1]
    rows = int(np.prod(shape[:-1])) if len(shape) > 1 else 1
    w2, g2, m2, v2 = [t.reshape(rows, cols) for t in (w, g, m, v)]
    tr = _pick(rows, (512, 352, 256)) if rows % 8 == 0 else rows

    def body(w_ref, g_ref, m_ref, v_ref, d_ref, nm_ref, nv_ref):
        d, nm, nv = _adamw_math(w_ref[...], g_ref[...], m_ref[...], v_ref[...])
        d_ref[...] = d
        nm_ref[...] = nm
        nv_ref[...] = nv

    blk = pl.BlockSpec((tr, cols), lambda i: (i, 0))
    out = pl.pallas_call(
        body, grid=(rows // tr,), in_specs=[blk] * 4, out_specs=[blk] * 3,
        out_shape=[jax.ShapeDtypeStruct((rows, cols), F32)] * 3, name=name,
        compiler_params=_cparams(("parallel",)),
    )(w2, g2, m2, v2)
    return [t.reshape(shape) for t in out]


def _pair_sum(g2, recv, core, name, after=()):
    _, nchip, r, c = g2.shape
    tr = _pick(r, (512,))

    def body(core_ref, a_ref, b_ref, *rest):
        o_ref = rest[-1]
        o_ref[...] = (a_ref[...].astype(F32) + b_ref[...].astype(F32)).astype(o_ref.dtype)

    grid_spec = pltpu.PrefetchScalarGridSpec(
        num_scalar_prefetch=1, grid=(nchip, r // tr),
        in_specs=[pl.BlockSpec((None, None, tr, c), lambda k, i, cr: (cr[0], k, i, 0)),
                  pl.BlockSpec((None, tr, c), lambda k, i, cr: (k, i, 0))] + [_after_spec(t) for t in after],
        out_specs=pl.BlockSpec((None, tr, c), lambda k, i, cr: (k, i, 0)))
    return pl.pallas_call(
        body, grid_spec=grid_spec, out_shape=jax.ShapeDtypeStruct((nchip, r, c), BF16), name=name,
        compiler_params=_cparams(("parallel", "parallel")),
    )(core, g2, recv, *after)


def _chip_sum(a, recv, chip, name, after=()):
    _, r, c = a.shape
    tr = _pick(r, (512,))

    def body(chip_ref, a_ref, b_ref, *rest):
        o_ref = rest[-1]
        o_ref[...] = ((a_ref[...].astype(F32) + b_ref[0].astype(F32)) + b_ref[1].astype(F32)) + b_ref[2].astype(F32)

    grid_spec = pltpu.PrefetchScalarGridSpec(
        num_scalar_prefetch=1, grid=(r // tr,),
        in_specs=[pl.BlockSpec((None, tr, c), lambda i, cr: (cr[0], i, 0)),
                  pl.BlockSpec((3, tr, c), lambda i, cr: (0, i, 0))] + [_after_spec(t) for t in after],
        out_specs=pl.BlockSpec((tr, c), lambda i, cr: (i, 0)))
    return pl.pallas_call(
        body, grid_spec=grid_spec, out_shape=jax.ShapeDtypeStruct((r, c), F32), name=name,
        compiler_params=_cparams(("parallel",)),
    )(chip, a, recv, *after)


def _sum_devices(parts, name):
    n, r, c = parts.shape

    def body(p_ref, o_ref):
        acc = p_ref[0]
        for d in range(1, n):
            acc = acc + p_ref[d]
        o_ref[...] = acc

    return pl.pallas_call(
        body, out_shape=jax.ShapeDtypeStruct((r, c), F32), name=name,
        in_specs=[pl.BlockSpec(memory_space=pltpu.VMEM)], out_specs=pl.BlockSpec(memory_space=pltpu.VMEM),
    )(parts)


def _lane_iota():
    return lax.broadcasted_iota(jnp.int32, (1, LANES), 1)


def _tiles_up(tiles, s, lane):
    rolled = [pltpu.roll(t, s, 1) for t in tiles]
    zero = jnp.zeros_like(tiles[0])
    return [jnp.where(lane < s, p, c) for p, c in zip([zero] + rolled, rolled + [zero])]


def _tiles_down(tiles, s, lane):
    back = (LANES - s) % LANES
    rolled = [pltpu.roll(t, back, 1) for t in tiles]
    zero = jnp.zeros_like(tiles[0])
    return [jnp.where(lane < LANES - s, c, n) for c, n in zip(rolled, rolled[1:] + [zero])]


def _window_cols(prm, w_in, w_gate, w_up, l, name, after=None):
    tm = 256
    n_in, n_gu = WIN_IN // LANES, FFN_SHARD // LANES + 1

    def body(prm_ref, win_ref, g_ref, u_ref, *rest):
        out_ref, gu_ref, scr_in, scr_gu = rest[-4:]
        lane = _lane_iota()
        s_main, lo, hi, s_code, clo, chi, code_hi, s_gu = [prm_ref[i] for i in range(8)]
        scr_in[:, D_MODEL:WIN_IN] = jnp.zeros((tm, WIN_IN - D_MODEL), F32)
        scr_in[:, 0:W_IN_SHARD] = win_ref[...]

        def keep(t, a, b):
            col = lane + t * LANES
            return jnp.where((col >= a) & (col < b), scr_in[:, t * LANES:(t + 1) * LANES], 0.0)

        main = _tiles_up([keep(t, lo, hi) for t in range(n_in)], s_main, lane)
        for t in range(n_in):
            out_ref[:, t * LANES:(t + 1) * LANES] = main[t].astype(BF16)
        low = _tiles_up([keep(0, clo, chi)], s_code, lane)[0]
        high = _tiles_up([keep(n_in - 2, clo, chi), keep(n_in - 1, clo, chi)], s_code, lane)[1]
        out_ref[:, A_FZ:A_FZ + LANES] = jnp.where(code_hi == 1, high, low).astype(BF16)
        for ref, base in ((g_ref, 0), (u_ref, WIN_GU)):
            scr_gu[:, (n_gu - 1) * LANES:n_gu * LANES] = jnp.zeros((tm, LANES), F32)
            scr_gu[:, 0:FFN_SHARD] = ref[...]
            moved = _tiles_up([scr_gu[:, t * LANES:(t + 1) * LANES] for t in range(n_gu)], s_gu, lane)
            for t in range(n_gu + 1):
                gu_ref[:, base + t * LANES:base + (t + 1) * LANES] = moved[t].astype(BF16)

    after_args = [] if after is None else [after]
    grid_spec = pltpu.PrefetchScalarGridSpec(
        num_scalar_prefetch=1, grid=(D_MODEL // tm,),
        in_specs=[pl.BlockSpec((None, tm, W_IN_SHARD), lambda i, p: (l, i, 0)),
                  pl.BlockSpec((None, tm, FFN_SHARD), lambda i, p: (l, i, 0)),
                  pl.BlockSpec((None, tm, FFN_SHARD), lambda i, p: (l, i, 0))] + [_after_spec(t) for t in after_args],
        out_specs=[pl.BlockSpec((tm, IN_COLS), lambda i, p: (i, 0)), pl.BlockSpec((tm, GU_COLS), lambda i, p: (i, 0))],
        scratch_shapes=[pltpu.VMEM((tm, WIN_IN), F32), pltpu.VMEM((tm, n_gu * LANES), F32)])
    return pl.pallas_call(
        body, grid_spec=grid_spec, name=name, compiler_params=_cparams(("parallel",)),
        out_shape=[jax.ShapeDtypeStruct((D_MODEL, IN_COLS), BF16), jax.ShapeDtypeStruct((D_MODEL, GU_COLS), BF16)],
    )(prm, w_in, w_gate, w_up, *after_args)


def _gu_width(j):
    return min(WIN_GU, FFN - T_GU[j] * LANES)


def _after_spec(t):
    tile = (SUBLANES_BF16 if t.dtype == BF16 else 8, LANES)
    return pl.BlockSpec((None,) * (t.ndim - 2) + tile, lambda *_: (0,) * t.ndim)


def _assemble_in(a_all, tail_all, name, after=()):
    tm = D_MODEL // N_DEV

    def body(a_ref, t_ref, *rest):
        win_ref, tail_ref = rest[-2:]
        tail_ref[...] = t_ref[...]
        win_ref[...] = jnp.zeros_like(win_ref)
        code = a_ref[0, :, A_FZ:A_FZ + LANES]
        for j in range(N_DEV):
            c0 = T_IN[j] * LANES
            win_ref[:, c0:c0 + WIN_IN] += a_ref[j, :, 0:WIN_IN]
            if j > 0:
                code = code + a_ref[j, :, A_FZ:A_FZ + LANES]
        win_ref[:, FZ_COL:PROJ_W] = code

    return pl.pallas_call(
        body, grid=(N_DEV,),
        in_specs=[pl.BlockSpec((N_DEV, tm, IN_COLS), lambda i: (0, i, 0)),
                  pl.BlockSpec((None,) + tail_all.shape[1:], lambda i: (i, 0, 0))] + [_after_spec(t) for t in after],
        out_specs=[pl.BlockSpec((tm, PROJ_W), lambda i: (i, 0)),
                   pl.BlockSpec((None,) + tail_all.shape[1:], lambda i: (i, 0, 0))],
        out_shape=[jax.ShapeDtypeStruct((D_MODEL, PROJ_W), BF16), jax.ShapeDtypeStruct(tail_all.shape, BF16)],
        name=name, compiler_params=_cparams(("parallel",)),
    )(a_all, tail_all, *after)


def _assemble_rest(a_all, rows_all, name, after=()):
    tm = D_MODEL // N_DEV
    n_in = 2 + len(after)

    def body(*refs):
        a_ref, r_ref = refs[:2]
        wgu_ref, oa_ref, ob_ref, o_ref, down_ref = refs[n_in:]
        wgu_ref[...] = jnp.zeros_like(wgu_ref)
        for j in range(N_DEV):
            g0, width = T_GU[j] * LANES, _gu_width(j)
            wgu_ref[:, g0:g0 + width] += a_ref[j, :, 0:width]
            wgu_ref[:, FFN + g0:FFN + g0 + width] += a_ref[j, :, WIN_GU:WIN_GU + width]
        oa_ref[...] = r_ref[B_OA:B_OA + ROW_SHARD, :]
        ob_ref[...] = r_ref[B_OB:B_OB + ROW_SHARD, :]
        o_ref[...] = r_ref[B_O:B_O + ROW_SHARD, :]
        down_ref[...] = r_ref[B_DOWN:B_DOWN + FFN_SHARD, :]

    def rows(n):
        return pl.BlockSpec((n, D_MODEL), lambda i: (i, 0))

    square = jax.ShapeDtypeStruct((D_MODEL, D_MODEL), BF16)
    return pl.pallas_call(
        body, grid=(N_DEV,),
        in_specs=[pl.BlockSpec((N_DEV, tm, GU_COLS), lambda i: (0, i, 0)),
                  pl.BlockSpec((None, B_FG2, D_MODEL), lambda i: (i, 0, 0))] + [_after_spec(t) for t in after],
        out_specs=[pl.BlockSpec((tm, 2 * FFN), lambda i: (i, 0)),
                   rows(ROW_SHARD), rows(ROW_SHARD), rows(ROW_SHARD), rows(FFN_SHARD)],
        out_shape=[jax.ShapeDtypeStruct((D_MODEL, 2 * FFN), BF16),
                   square, square, square, jax.ShapeDtypeStruct((FFN, D_MODEL), BF16)],
        name=name, compiler_params=_cparams(("parallel",)),
    )(a_all, rows_all, *after)


def _grad_windows_in(d_in, name):
    tm = 256

    def body(din_ref, out_ref):
        for j in range(N_DEV):
            c0 = T_IN[j] * LANES
            out_ref[j & 1, j >> 1, :, 0:WIN_IN] = din_ref[:, c0:c0 + WIN_IN]
            out_ref[j & 1, j >> 1, :, A_FZ:IN_COLS] = din_ref[:, FZ_COL:PROJ_W]

    return pl.pallas_call(
        body, grid=(D_MODEL // tm,), in_specs=[pl.BlockSpec((tm, PROJ_W), lambda i: (i, 0))],
        out_specs=pl.BlockSpec((2, 4, tm, IN_COLS), lambda i: (0, 0, i, 0)),
        out_shape=jax.ShapeDtypeStruct((2, 4, D_MODEL, IN_COLS), BF16), name=name,
        compiler_params=_cparams(("parallel",)),
    )(d_in)


def _grad_windows_gu(d_gu, name):
    tm = 256

    def body(dgu_ref, out_ref):
        for j in range(N_DEV):
            g0, width = T_GU[j] * LANES, _gu_width(j)
            for half, base in ((0, 0), (FFN, WIN_GU)):
                out_ref[j & 1, j >> 1, :, base:base + width] = dgu_ref[:, half + g0:half + g0 + width]
                if width < WIN_GU:
                    out_ref[j & 1, j >> 1, :, base + width:base + WIN_GU] = jnp.zeros((tm, WIN_GU - width), BF16)

    return pl.pallas_call(
        body, grid=(D_MODEL // tm,), in_specs=[pl.BlockSpec((tm, 2 * FFN), lambda i: (i, 0))],
        out_specs=pl.BlockSpec((2, 4, tm, GU_COLS), lambda i: (0, 0, i, 0)),
        out_shape=jax.ShapeDtypeStruct((2, 4, D_MODEL, GU_COLS), BF16), name=name,
        compiler_params=_cparams(("parallel",)),
    )(d_gu)


def _final_windows(prm, chip, a, recv, params, l, kind, name, prev=None):
    tm = 128
    n_in, n_gu = WIN_IN // LANES, FFN_SHARD // LANES + 1
    widths = (W_IN_SHARD,) if kind == "in" else (FFN_SHARD, FFN_SHARD)
    cols = IN_COLS if kind == "in" else GU_COLS
    n_par = 3 * len(widths)
    prev = [] if prev is None else [t for group in prev for t in group]

    def body(prm_ref, chip_ref, a_ref, r_ref, *refs):
        ins, outs = refs[:n_par], refs[n_par + len(prev):]
        lane = _lane_iota()
        s_main, s_code, clo, chi, s_gu = [prm_ref[i] for i in (0, 3, 4, 5, 7)]

        def total(c0):
            sl = slice(c0, c0 + LANES)
            return ((a_ref[:, sl].astype(F32) + r_ref[0, :, sl].astype(F32)) + r_ref[1, :, sl].astype(F32)) \
                + r_ref[2, :, sl].astype(F32)

        if kind == "in":
            grads = _tiles_down([total(t * LANES) for t in range(n_in)], s_main, lane)
            code = pltpu.roll(total(A_FZ), (LANES - s_code) % LANES, 1)
            for t in (0, n_in - 2, n_in - 1):
                col = lane + t * LANES
                grads[t] = jnp.where((col >= clo) & (col < chi), code, grads[t])
            per_weight = [grads]
        else:
            per_weight = [_tiles_down([total(base + t * LANES) for t in range(n_gu + 1)], s_gu, lane)[:n_gu]
                          for base in (0, WIN_GU)]
        for k, (tiles, width) in enumerate(zip(per_weight, widths)):
            w_ref, m_ref, v_ref = ins[3 * k:3 * k + 3]
            g_ref, d_ref, nm_ref, nv_ref = outs[4 * k:4 * k + 4]
            for t, g in enumerate(tiles):
                n = min(LANES, width - t * LANES)
                sl = slice(t * LANES, t * LANES + n)
                g = g[:, 0:n]
                d, nm, nv = _adamw_math(w_ref[:, sl], g, m_ref[:, sl], v_ref[:, sl])
                g_ref[:, sl] = g
                d_ref[:, sl] = d
                nm_ref[:, sl] = nm
                nv_ref[:, sl] = nv

    def native(width):
        return pl.BlockSpec((None, tm, width), lambda i, p, c: (l, i, 0))

    in_specs = [pl.BlockSpec((None, tm, cols), lambda i, p, c: (c[0], i, 0)),
                pl.BlockSpec((3, tm, cols), lambda i, p, c: (0, i, 0))]
    in_specs += [native(wd) for wd in widths for _ in range(3)] + [_ANY] * len(prev)
    grid_spec = pltpu.PrefetchScalarGridSpec(
        num_scalar_prefetch=2, grid=(D_MODEL // tm,), in_specs=in_specs,
        out_specs=[native(wd) for wd in widths for _ in range(4)])
    out = pl.pallas_call(
        body, grid_spec=grid_spec, name=name, compiler_params=_cparams(("parallel",)),
        out_shape=[jax.ShapeDtypeStruct((DEPTH, D_MODEL, wd), F32) for wd in widths for _ in range(4)],
        input_output_aliases={4 + n_par + k: k for k in range(len(prev))},
    )(prm, chip, a, recv, *params, *prev)
    return [out[4 * k:4 * k + 4] for k in range(len(widths))]


def _me():
    return lax.axis_index("x"), lax.axis_index("y"), lax.axis_index("c")


_CHIP_FLIPS = ((1, 0), (0, 1), (1, 1))
_ANY = pl.BlockSpec(memory_space=pl.ANY)


def _comm_call(body, peers, out_shape, sems, name, args, collective_id):
    if collective_id is None:
        n_in = len(args)
        return pl.pallas_call(body, out_shape=out_shape, name=name, in_specs=[_ANY] * n_in,
                              out_specs=[_ANY] * len(out_shape), scratch_shapes=sems)(*args)

    def sequencer_body(*refs):
        barrier = pltpu.get_barrier_semaphore()
        targets = peers()
        for peer in targets:
            pl.semaphore_signal(barrier, inc=1, device_id=peer, device_id_type=MESH)
        pl.semaphore_wait(barrier, len(targets))
        body(*refs)

    sequencer = plsc.ScalarSubcoreMesh(axis_name="seq", num_cores=1)
    return pl.kernel(sequencer_body, out_type=out_shape, mesh=sequencer, scratch_types=sems, name=name,
                     compiler_params=pltpu.CompilerParams(collective_id=collective_id))(*args)


def _sibling_peer():
    x, y, cc = _me()
    return [(x, y, 1 - cc)]


def _chip_peers():
    x, y, cc = _me()
    return [(x ^ fx, y ^ fy, cc) for fx, fy in _CHIP_FLIPS]


def _all_gather(shards, name, collective_id=None):
    n = len(shards)
    split = [s.shape[0] % (2 * SUBLANES_BF16) == 0 for s in shards]

    def body(*refs):
        x_refs, out_refs = refs[:n], refs[n:2 * n]
        send_sems, recv_sems, local_sems = refs[2 * n:]
        x, y, cc = _me()
        me, sibling = (x, y, cc), (x, y, 1 - cc)
        near, far = [(x ^ 1, y), (x, y ^ 1)], (x ^ 1, y ^ 1)

        def copy(a, k, block, to, half=None, from_shard=False):
            px, py, pc = block
            slot = out_refs[a].at[4 * px + 2 * py + pc]
            if half is not None:
                rows = shards[a].shape[0] // 2
                slot = slot.at[pl.ds(half * rows, rows)]
            return pltpu.make_async_remote_copy(
                src_ref=x_refs[a] if from_shard else slot, dst_ref=slot,
                send_sem=send_sems.at[a, k], recv_sem=recv_sems.at[a, k], device_id=to, device_id_type=MESH)

        mine = [pltpu.make_async_copy(x_refs[a], out_refs[a].at[4 * x + 2 * y + cc], local_sems.at[a])
                for a in range(n)]
        for cp in mine:
            cp.start()
        sent = [copy(a, 0, me, sibling, from_shard=True) for a in range(n)]
        sent += [copy(a, 1 + j, me, (*chip, cc), from_shard=True) for j, chip in enumerate(near) for a in range(n)]
        sent += [copy(a, 3, me, (*far, cc), from_shard=True) for a in range(n) if not split[a]]
        for cp in sent:
            cp.start()

        def pass_on(cp):
            cp.start()
            sent.append(cp)

        for j, chip in enumerate(near):
            for a in range(n):
                copy(a, 1 + j, (*chip, cc), me).wait_recv()
                pass_on(copy(a, 4 + j, (*chip, cc), sibling))
                if split[a]:
                    pass_on(copy(a, 7 + j, (*chip, cc), (*near[1 - j], cc), half=j))
        for a in range(n):
            if split[a]:
                copy(a, 7, (*far, cc), me, half=0).wait_recv()
                copy(a, 8, (*far, cc), me, half=1).wait_recv()
            else:
                copy(a, 3, (*far, cc), me).wait_recv()
            pass_on(copy(a, 6, (*far, cc), sibling))
        for a in range(n):
            copy(a, 0, sibling, me).wait_recv()
            for j, chip in enumerate(near + [far]):
                copy(a, 4 + j, (*chip, 1 - cc), me).wait_recv()
        for cp in sent:
            cp.wait_send()
        for cp in mine:
            cp.wait()

    return _comm_call(
        body, lambda: _sibling_peer() + _chip_peers(),
        [jax.ShapeDtypeStruct((N_DEV,) + s.shape, s.dtype) for s in shards],
        [pltpu.SemaphoreType.DMA((n, 9)), pltpu.SemaphoreType.DMA((n, 9)), pltpu.SemaphoreType.DMA((n,))],
        name, shards, collective_id)


def _send_to_sibling(parts, name, collective_id=None):
    n = len(parts)

    def body(*refs):
        g_refs, out_refs = refs[:n], refs[n:2 * n]
        send_sems, recv_sems = refs[2 * n:]
        x, y, cc = _me()
        copies = [pltpu.make_async_remote_copy(
            src_ref=g_refs[a].at[1 - cc], dst_ref=out_refs[a], send_sem=send_sems.at[a], recv_sem=recv_sems.at[a],
            device_id=(x, y, 1 - cc), device_id_type=MESH) for a in range(n)]
        for cp in copies:
            cp.start()
        for cp in copies:
            cp.wait()

    return _comm_call(
        body, _sibling_peer, [jax.ShapeDtypeStruct(p.shape[1:], p.dtype) for p in parts],
        [pltpu.SemaphoreType.DMA((n,)), pltpu.SemaphoreType.DMA((n,))], name, parts, collective_id)


def _send_to_chips(parts, name, collective_id=None):
    n = len(parts)

    def body(*refs):
        a_refs, out_refs = refs[:n], refs[n:2 * n]
        send_sems, recv_sems = refs[2 * n:]
        x, y, cc = _me()
        copies = []
        for k, (fx, fy) in enumerate(_CHIP_FLIPS):
            px, py = x ^ fx, y ^ fy
            for a in range(n):
                copies.append(pltpu.make_async_remote_copy(
                    src_ref=a_refs[a].at[2 * px + py], dst_ref=out_refs[a].at[k], send_sem=send_sems.at[a, k],
                    recv_sem=recv_sems.at[a, k], device_id=(px, py, cc), device_id_type=MESH))
                copies[-1].start()
        for cp in copies:
            cp.wait()

    return _comm_call(
        body, _chip_peers, [jax.ShapeDtypeStruct((3,) + p.shape[1:], p.dtype) for p in parts],
        [pltpu.SemaphoreType.DMA((n, 3)), pltpu.SemaphoreType.DMA((n, 3))], name, parts, collective_id)


def _pack_rows(w_oa, w_ob, w_o, w_down, w_fg2, conv_w, l):
    conv_bits = lax.bitcast_convert_type(conv_w[l].reshape(-1), BF16).reshape(1, -1)
    tail = jnp.concatenate([w_fg2[l].astype(BF16).reshape(1, D_MODEL),
                            jnp.pad(conv_bits, ((0, 0), (0, D_MODEL - conv_bits.shape[1])))], axis=0)
    tail = jnp.pad(tail, ((0, B_ROWS - B_FG2 - tail.shape[0]), (0, 0)))
    rows = jnp.concatenate([w_oa[l].astype(BF16), w_ob[l].astype(BF16), w_o[l].astype(BF16),
                            w_down[l].astype(BF16)], axis=0)
    return rows, tail


def _unpack_tail(tail):
    w_fg2 = tail[:, 0, :].reshape(N_DEV, RANK, KEY_W // N_DEV).transpose(1, 0, 2).reshape(RANK, KEY_W)
    conv_bits = tail[:, B_CONV - B_FG2, :2 * 3 * ROW_SHARD].reshape(N_DEV, 3 * ROW_SHARD, 2)
    conv_w = lax.bitcast_convert_type(conv_bits, F32).reshape(N_DEV, 3, ROW_SHARD)
    return jnp.pad(w_fg2, ((0, LANES - RANK), (0, 0))), conv_w.transpose(1, 0, 2).reshape(3, D_MODEL)


def _by_core_chip(t):
    return t.reshape((2, 2, 2) + t.shape[1:]).transpose((2, 0, 1) + tuple(range(3, t.ndim + 2))).reshape(
        (2, 4) + t.shape[1:])


def _grad_rows(g):
    fg2 = g["w_fg2"][:RANK].reshape(RANK, N_DEV, KEY_W // N_DEV).transpose(1, 0, 2).reshape(N_DEV, 1, D_MODEL)
    conv = g["conv_w"].astype(BF16).reshape(3, N_DEV, ROW_SHARD).transpose(1, 0, 2).reshape(N_DEV, 1, 3 * ROW_SHARD)
    tail = jnp.concatenate([fg2, jnp.pad(conv, ((0, 0), (0, 0), (0, D_MODEL - 3 * ROW_SHARD)))], axis=1)
    tail = jnp.pad(tail, ((0, 0), (0, B_ROWS - B_FG2 - 2), (0, 0)))
    parts = [g["w_oa"].reshape(N_DEV, ROW_SHARD, D_MODEL), g["w_ob"].reshape(N_DEV, ROW_SHARD, D_MODEL),
             g["w_o"].reshape(N_DEV, ROW_SHARD, D_MODEL), g["w_down"].reshape(N_DEV, FFN_SHARD, D_MODEL), tail]
    return _by_core_chip(jnp.concatenate(parts, axis=1))


def _ungrad_rows(gs):
    return dict(w_oa=gs[B_OA:B_OA + ROW_SHARD], w_ob=gs[B_OB:B_OB + ROW_SHARD], w_o=gs[B_O:B_O + ROW_SHARD],
                w_ffn_down=gs[B_DOWN:B_DOWN + FFN_SHARD], w_fg2=gs[B_FG2].reshape(RANK, KEY_W // N_DEV),
                conv_w=gs[B_CONV, :3 * ROW_SHARD].reshape(3, ROW_SHARD))


def _layer_fwd(x, h, p, l, next_norm_g=None):
    tag = f"l{l}_"
    if h is None:
        h = _rmsnorm_fwd(x, p["norm1_g"], tag + "norm1")
    proj = _matmul(h, p["w_in"], "nn", BF16, tag + "proj")
    o, states, oa = _gla_fwd(proj, p["w_fg2"], p["b_fg"], p["gla_norm_g"], tag + "gla_fwd")
    p.update(p.pop("rest")((o,)))
    ya = _matmul(oa, p["w_oa"], "nn", BF16, tag + "ya")
    cb = _conv_fwd(proj, p["conv_w"], p["conv_b"], tag + "conv")
    yb = _matmul(cb, p["w_ob"], "nn", BF16, tag + "yb")
    mix = _mix_fwd(proj, ya, yb, tag + "mix")
    x1, h2 = _matmul(mix, p["w_o"], "nn", F32, tag + "x1", residual=x, norm_g=p["norm2_g"])
    gu = _matmul(h2, p["w_gu"], "nn", BF16, tag + "gu")
    hid = _swiglu_fwd(gu, tag + "swiglu")
    x2 = _matmul(hid, p["w_down"], "nn", F32, tag + "x2", residual=x1, norm_g=next_norm_g)
    x2, h_next = x2 if next_norm_g is not None else (x2, None)
    saved = dict(x=x, h=h, proj=proj, o=o, states=states, oa=oa, ya=ya, cb=cb, yb=yb, mix=mix, x1=x1, h2=h2,
                 gu=gu, hid=hid)
    return x2, h_next, saved


def _layer_bwd(dx2, dx2h, p, sv, l, reduce=None):
    if reduce is None:
        reduce = lambda group, grads: ((), ())
    tag = f"l{l}_b_"
    dhid = _matmul(dx2h, p["w_down"], "nt", BF16, tag + "dhid")
    d_down = _matmul(sv["hid"], dx2h, "tn", BF16, tag + "dw_down")
    dgu = _swiglu_bwd(dhid, sv["gu"], tag + "swiglu")
    dx1, dx1h, dg2 = _matmul_norm_bwd(dgu, p["w_gu"], sv["x1"], p["norm2_g"], dx2, tag + "dh2")
    d_gu = _matmul(sv["h2"], dgu, "tn", BF16, tag + "dw_gu")
    gu_packed, gu_sums = reduce("gu", d_gu)
    dmix = _matmul(dx1h, p["w_o"], "nt", BF16, tag + "dmix", after=gu_packed)
    d_o = _matmul(sv["mix"], dx1h, "tn", BF16, tag + "dw_o")
    dproj, dya, dyb = _mix_bwd(dmix, sv["proj"], sv["ya"], sv["yb"], tag + "mix")
    dcb = _matmul(dyb, p["w_ob"], "nt", BF16, tag + "dcb", after=gu_sums)
    d_ob = _matmul(sv["cb"], dyb, "tn", BF16, tag + "dw_ob")
    dproj, dwb = _conv_bwd(dcb, sv["proj"], p["conv_w"], p["conv_b"], dproj, tag + "conv")
    doa = _matmul(dya, p["w_oa"], "nt", BF16, tag + "doa")
    d_oa = _matmul(sv["oa"], dya, "tn", BF16, tag + "dw_oa")
    dq, dk, dv, dfg, dbfg, dproj, dgg = _gla_bwd(sv["proj"], p["w_fg2"], p["b_fg"], sv["states"], doa, sv["o"],
                                                 p["gla_norm_g"], dproj, tag + "gla")
    fz = sv["proj"][:, FZ_COL:]
    dproj = _place_qkv(dq, dk, dv, dproj, tag + "place_qkv")
    dproj = _matmul(dfg, p["w_fg2"], "nt", BF16, tag + "dfz", into=(dproj, FZ_COL))
    d_fg2 = _matmul(fz, dfg, "tn", BF16, tag + "dw_fg2")
    rows = dict(w_fg2=d_fg2, conv_w=dwb[0:3], w_oa=d_oa, w_ob=d_ob, w_o=d_o, w_down=d_down)
    rows_packed, rows_sums = reduce("rows", rows)
    d_in = _matmul(sv["h"], dproj, "tn", BF16, tag + "dw_in", after=rows_packed)
    _, in_sums = reduce("in", d_in)
    dx, dxh, dg1 = _matmul_norm_bwd(dproj, p["w_in"], sv["x"], p["norm1_g"], dx1, tag + "dh",
                                    after=(d_in,) + tuple(rows_sums) + tuple(in_sums))
    big = dict(w_in=d_in, w_gu=d_gu, **rows)
    pad = lambda t: jnp.pad(t, ((0, 0), (0, D_MODEL - t.shape[1])))
    small = [dg1[0:1], pad(dbfg[0:1]), pad(dgg[0:1]), dwb[3:4], dg2[0:1]]
    return dx, dxh, big, small


def _local_step(x, target, weights_of, final_g, reduce_of=None):
    saved, layers, h = [], [], None
    for l in range(DEPTH):
        layers.append(weights_of(l, x))
        x, h, sv = _layer_fwd(x, h, layers[l], l, layers[l].get("next_norm1_g"))
        saved.append(sv)
    loss, dx, dxh, dgf = _loss_head(x, final_g, target, "loss_head")
    bigs, smalls = [None] * DEPTH, [None] * DEPTH
    for l in reversed(range(DEPTH)):
        dx, dxh, bigs[l], smalls[l] = _layer_bwd(dx, dxh, layers[l], saved[l], l, reduce_of(l) if reduce_of else None)
    loss_row = jnp.pad(loss[0:1], ((0, 0), (0, D_MODEL - loss.shape[1])))
    small = jnp.concatenate(smalls[0] + smalls[1] + [dgf[0:1], loss_row], axis=0)
    small = jnp.pad(small, ((0, SMALL_ROWS - small.shape[0]), (0, 0)))
    return loss[0, 0], dx, bigs, small


def kernel(x, norm1_g, w_in, w_fg2, b_fg, gla_norm_g, w_oa, conv_w, conv_b, w_ob, w_o, norm2_g, w_ffn_gate, w_ffn_up, w_ffn_down, final_g, loss_target, m_norm1_g, m_w_in, m_w_fg2, m_b_fg, m_gla_norm_g, m_w_oa, m_conv_w, m_conv_b, m_w_ob, m_w_o, m_norm2_g, m_w_ffn_gate, m_w_ffn_up, m_w_ffn_down, m_final_g, v_norm1_g, v_w_in, v_w_fg2, v_b_fg, v_gla_norm_g, v_w_oa, v_conv_w, v_conv_b, v_w_ob, v_w_o, v_norm2_g, v_w_ffn_gate, v_w_ffn_up, v_w_ffn_down, v_final_g):
    names = ["norm1_g", "w_in", "w_fg2", "b_fg", "gla_norm_g", "w_oa", "conv_w", "conv_b", "w_ob", "w_o",
             "norm2_g", "w_ffn_gate", "w_ffn_up", "w_ffn_down", "final_g"]
    w = dict(zip(names, [norm1_g, w_in, w_fg2, b_fg, gla_norm_g, w_oa, conv_w, conv_b, w_ob, w_o, norm2_g,
                         w_ffn_gate, w_ffn_up, w_ffn_down, final_g]))
    m = dict(zip(names, [m_norm1_g, m_w_in, m_w_fg2, m_b_fg, m_gla_norm_g, m_w_oa, m_conv_w, m_conv_b, m_w_ob,
                         m_w_o, m_norm2_g, m_w_ffn_gate, m_w_ffn_up, m_w_ffn_down, m_final_g]))
    v = dict(zip(names, [v_norm1_g, v_w_in, v_w_fg2, v_b_fg, v_gla_norm_g, v_w_oa, v_conv_w, v_conv_b, v_w_ob,
                         v_w_o, v_norm2_g, v_w_ffn_gate, v_w_ffn_up, v_w_ffn_down, v_final_g]))
    col_names = ["w_in", "w_ffn_gate", "w_ffn_up"]
    cx, cy, cc = _me()
    prm = jnp.asarray(SHIFT_TABLE)[4 * cx + 2 * cy + cc]
    core = jnp.reshape(cc, (1,)).astype(jnp.int32)
    chip = jnp.reshape(2 * cx + cy, (1,)).astype(jnp.int32)

    ids = iter(range(32))

    gathered, previous = [], None
    for l in range(DEPTH):
        rows, tail = _pack_rows(w_oa, w_ob, w_o, w_ffn_down, w_fg2, conv_w, l)
        win_in, win_gu = _window_cols(prm, w_in, w_ffn_gate, w_ffn_up, l, f"l{l}_windows", after=previous)
        previous = rows
        first = _all_gather([win_in, tail], f"l{l}_gather_in", next(ids))
        gathered.append(list(first) + list(_all_gather([win_gu, rows], f"l{l}_gather_rest", next(ids))))

    def weights_of(l, x_in):
        all_in, all_tail, all_gu, all_rows = gathered[l]
        after = (x_in,) if l > 0 else ()
        w_in_full, tail = _assemble_in(all_in, all_tail, f"l{l}_assemble_in", after)
        w_fg2_full, conv_w_full = _unpack_tail(tail)

        def rest(after_rest):
            names_rest = ("w_gu", "w_oa", "w_ob", "w_o", "w_down")
            return dict(zip(names_rest, _assemble_rest(all_gu, all_rows, f"l{l}_assemble_rest", after + after_rest)))

        return dict(w_in=w_in_full, rest=rest, w_fg2=w_fg2_full,
                    conv_w=conv_w_full, norm1_g=norm1_g[l][None], b_fg=b_fg[l][None],
                    gla_norm_g=gla_norm_g[l][None], conv_b=conv_b[l][None], norm2_g=norm2_g[l][None],
                    next_norm1_g=norm1_g[l + 1][None] if l + 1 < DEPTH else None)

    pending = [dict() for _ in range(DEPTH)]
    landed = []

    def reduce_of(l):
        def reduce(group, grads):
            tag = f"l{l}_{group}"
            if group == "gu":
                packed = _grad_windows_gu(grads, tag + "_windows")
            elif group == "in":
                packed = _grad_windows_in(grads, tag + "_windows")
            else:
                packed = _grad_rows(grads)
            (from_sibling,) = _send_to_sibling([packed], tag + "_to_sibling", next(ids))
            waited = () if group == "in" else tuple(landed)
            if waited:
                landed.clear()
            sums = _pair_sum(packed, from_sibling, core, tag + "_pair_sum", after=waited)
            (from_chips,) = _send_to_chips([sums], tag + "_to_chips", next(ids))
            landed.append(from_chips)
            pending[l][group] = (sums, from_chips)
            return (packed,), (sums,)
        return reduce

    loss, dx, bigs, small = _local_step(x[0], loss_target[0], weights_of, final_g[None], reduce_of)

    grads, deltas, new_m, new_v = {}, {}, {}, {}
    for kind, group_names in (("gu", col_names[1:]), ("in", col_names[:1])):
        params = [t for n in group_names for t in (w[n], m[n], v[n])]
        out = None
        for l in reversed(range(DEPTH)):
            sums, from_chips = pending[l][kind]
            out = _final_windows(prm, chip, sums, from_chips, params, l, kind, f"l{l}_{kind}_final", out)
        for n, (g, d, nm, nv) in zip(group_names, out):
            grads[n], deltas[n], new_m[n], new_v[n] = g, d, nm, nv
    row_grads = [_ungrad_rows(_chip_sum(*pending[l]["rows"], chip, f"l{l}_rows_chip_sum")) for l in range(DEPTH)]
    for n in row_grads[0]:
        grads[n] = jnp.stack([row_grads[l][n] for l in range(DEPTH)])

    small_sum = _sum_devices(_all_gather([small], "gather_small")[0], "sum_small")
    r512, r256 = slice(0, KEY_W), slice(0, HEAD_V)
    grads.update(
        norm1_g=jnp.stack([small_sum[0], small_sum[5]]), b_fg=jnp.stack([small_sum[1, r512], small_sum[6, r512]]),
        gla_norm_g=jnp.stack([small_sum[2, r256], small_sum[7, r256]]),
        conv_b=jnp.stack([small_sum[3], small_sum[8]]), norm2_g=jnp.stack([small_sum[4], small_sum[9]]),
        final_g=small_sum[10])

    for n in names:
        if n not in col_names:
            deltas[n], new_m[n], new_v[n] = _adamw(w[n], grads[n], m[n], v[n], "adamw_" + n)

    total_loss = small_sum[2 * 5 + 1, 0]
    return (total_loss, dx[None], *[grads[n] for n in names], *[deltas[n] for n in names],
            *[new_m[n] for n in names], *[new_v[n] for n in names])
```

```python
import functools

import jax
import jax.numpy as jnp
import numpy as np
from jax import lax
from jax.experimental import pallas as pl
from jax.experimental.pallas import tpu as pltpu
from jax.experimental.pallas import tpu_sc as plsc

F32 = jnp.float32
BF16 = jnp.bfloat16
MESH = pl.DeviceIdType.MESH

D_MODEL = 1024
DEPTH = 2
CHUNK = 64
HEADS = 4
HEAD_K = 128
HEAD_V = 256
KEY_W = HEADS * HEAD_K
VAL_W = HEADS * HEAD_V
RANK = 16
TAU = 16.0
FFN = 2816
IN_WIDTH = 2 * KEY_W + 2 * VAL_W + RANK + 5 * D_MODEL
EPS = 1e-6
Q_SCALE = HEAD_K ** -0.5
N_DEV = 8
ADAM_LR, ADAM_B1, ADAM_B2, ADAM_EPS, ADAM_WD, ADAM_STEP = 0.001, 0.9, 0.999, 1e-08, 0.01, 10

LANES = 128
SUBLANES_BF16 = 16
VMEM_LIMIT = 48 * 1024 * 1024
VMEM_LIMIT_WIDE = 56 * 1024 * 1024

FZ_COL = 2 * KEY_W + 2 * VAL_W + 5 * D_MODEL
PROJ_W = FZ_COL + LANES
SEG_R, SEG_GBI, SEG_GCI, SEG_CX, SEG_GA, SEG_GB = 2, 3, 4, 5, 6, 7

W_IN_SHARD = IN_WIDTH // N_DEV
FFN_SHARD = FFN // N_DEV
ROW_SHARD = D_MODEL // N_DEV

WIN_IN = 9 * LANES
WIN_GU = 4 * LANES
A_FZ = WIN_IN
IN_COLS = A_FZ + LANES
GU_COLS = 2 * WIN_GU
ORIG_FZ = 2 * KEY_W + 2 * VAL_W


def _new_col(o):
    if o < ORIG_FZ:
        return o
    if o < ORIG_FZ + RANK:
        return FZ_COL + (o - ORIG_FZ)
    return o - RANK


def _shift_table():
    t_in, rows = [], []
    for j in range(N_DEV):
        new = [_new_col(W_IN_SHARD * j + i) for i in range(W_IN_SHARD)]
        main = [i for i in range(W_IN_SHARD) if new[i] < FZ_COL]
        code = [i for i in range(W_IN_SHARD) if new[i] >= FZ_COL]
        shift = new[main[0]] - main[0]
        t_in.append(shift // LANES)
        assert all(new[i] - i == shift for i in main) and shift % LANES + W_IN_SHARD <= WIN_IN
        if code:
            cshift = new[code[0]] - FZ_COL - code[0]
            crow = [cshift % LANES, code[0], code[-1] + 1, int(cshift < 0)]
        else:
            crow = [0, 0, 0, 0]
        rows.append([shift % LANES, main[0], main[-1] + 1] + crow + [FFN_SHARD * j % LANES])
    return tuple(t_in), np.asarray(rows, np.int32)


T_IN, SHIFT_TABLE = _shift_table()
T_GU = tuple(FFN_SHARD * j // LANES for j in range(N_DEV))

B_OA, B_OB, B_O, B_DOWN = 0, ROW_SHARD, 2 * ROW_SHARD, 3 * ROW_SHARD
B_FG2 = B_DOWN + FFN_SHARD
B_CONV = B_FG2 + 1
B_ROWS = B_FG2 + SUBLANES_BF16
SMALL_ROWS = 32


def _pick(n, candidates):
    for c in candidates:
        if n % c == 0:
            return c
    return n


def _cparams(sem):
    return pltpu.CompilerParams(dimension_semantics=sem, vmem_limit_bytes=VMEM_LIMIT)


def _sigmoid(x):
    return 1.0 / (1.0 + jnp.exp(-x))


def _matmul(a, b, dims, out_dtype, name, residual=None, after=(), into=None, norm_g=None, a_cols=None):
    if dims == "nn":
        (m, k), (k2, n) = a.shape, b.shape
    elif dims == "nt":
        (m, k), (n, k2) = a.shape, b.shape
    else:
        (k, m), (k2, n) = a.shape, b.shape
        m = m if a_cols is None else a_cols[1]
    assert k == k2 and (a_cols is None or dims == "tn"), (a.shape, b.shape, dims)
    tm = _pick(m, (1024, 1408, 512, 256, 128))
    tn = _pick(n, (1664, 1408, 1024, 512, 256, 128))
    tk = _pick(k, (1664, 1408, 1024, 512, 256, 128))
    nk = k // tk
    if dims == "nn":
        a_spec = pl.BlockSpec((tm, tk), lambda i, j, kk: (i, kk))
        b_spec = pl.BlockSpec((tk, tn), lambda i, j, kk: (kk, j))
        contract = (((1,), (0,)), ((), ()))
    elif dims == "nt":
        a_spec = pl.BlockSpec((tm, tk), lambda i, j, kk: (i, kk))
        b_spec = pl.BlockSpec((tn, tk), lambda i, j, kk: (j, kk))
        contract = (((1,), (1,)), ((), ()))
    else:
        first = 0 if a_cols is None else a_cols[0] // tm
        assert a_cols is None or (tm == m and a_cols[0] % tm == 0)
        a_spec = pl.BlockSpec((tk, tm), lambda i, j, kk: (kk, first + i))
        b_spec = pl.BlockSpec((tk, tn), lambda i, j, kk: (kk, j))
        contract = (((0,), (0,)), ((), ()))
    o_spec = pl.BlockSpec((tm, tn), lambda i, j, kk: (i, j))
    has_res = residual is not None
    out_spec, out_struct, placed, aliases = o_spec, jax.ShapeDtypeStruct((m, n), out_dtype), (), {}
    if into is not None:
        buffer, col = into
        assert col % tn == 0 and buffer.dtype == out_dtype and not has_res
        out_spec = pl.BlockSpec((tm, tn), lambda i, j, kk: (i, col // tn + j))
        out_struct, placed, aliases = jax.ShapeDtypeStruct(buffer.shape, out_dtype), (buffer,), {2 + len(after): 0}

    has_norm = norm_g is not None
    assert not has_norm or (tn == n and into is None)

    def body(*refs):
        a_ref, b_ref = refs[:2]
        r_ref = refs[2] if has_res else None
        n_in = 2 + has_res + has_norm + len(after) + len(placed)
        o_ref = refs[n_in]
        kk = pl.program_id(2)
        part = lax.dot_general(a_ref[...], b_ref[...], contract, preferred_element_type=F32)

        def finish(total):
            if has_res:
                total = total + r_ref[...]
            o_ref[...] = total.astype(o_ref.dtype)
            if has_norm:
                rstd = lax.rsqrt(jnp.mean(total * total, axis=-1, keepdims=True) + EPS)
                refs[n_in + 1][...] = (total * rstd * refs[2 + has_res][...]).astype(BF16)

        if nk == 1:
            finish(part)
            return
        acc_ref = refs[-1]

        @pl.when(kk == 0)
        def _():
            acc_ref[...] = part

        @pl.when((kk > 0) & (kk < nk - 1))
        def _():
            acc_ref[...] += part

        @pl.when(kk == nk - 1)
        def _():
            finish(acc_ref[...] + part)

    in_specs = [a_spec, b_spec] + ([o_spec] if has_res else [])
    in_specs += [pl.BlockSpec((1, n), lambda i, j, kk: (0, 0))] if has_norm else []
    in_specs += [_after_spec(t) for t in after] + [_ANY] * len(placed)
    args = (a, b) + ((residual,) if has_res else ()) + ((norm_g,) if has_norm else ()) + tuple(after) + placed
    if has_norm:
        out_spec, out_struct = [out_spec, o_spec], [out_struct, jax.ShapeDtypeStruct((m, n), BF16)]
    return pl.pallas_call(
        body, grid=(m // tm, n // tn, nk), in_specs=in_specs, out_specs=out_spec,
        out_shape=out_struct, input_output_aliases=aliases,
        scratch_shapes=[pltpu.VMEM((tm, tn), F32)] if nk > 1 else [], name=name,
        compiler_params=_cparams(("parallel", "parallel", "arbitrary")),
    )(*args)


def _rmsnorm_fwd(x, g, name):
    s, d = x.shape
    tm = _pick(s, (512, 256))

    def body(x_ref, g_ref, o_ref):
        xv = x_ref[...]
        r = lax.rsqrt(jnp.mean(xv * xv, axis=-1, keepdims=True) + EPS)
        o_ref[...] = (xv * r * g_ref[...]).astype(o_ref.dtype)

    row = pl.BlockSpec((tm, d), lambda i: (i, 0))
    return pl.pallas_call(
        body, grid=(s // tm,), in_specs=[row, pl.BlockSpec((1, d), lambda i: (0, 0))], out_specs=row,
        out_shape=jax.ShapeDtypeStruct((s, d), BF16), name=name, compiler_params=_cparams(("parallel",)),
    )(x, g)


def _matmul_norm_bwd(a, b, x, g, dres, name, after=()):
    (s, k), (d, k2) = a.shape, b.shape
    assert k == k2 and x.shape == (s, d)
    tm = _pick(s, (1024, 512, 256))
    tk = _pick(k, (1664, 1408, 1024, 512, 256, 128))
    nk = k // tk

    def body(a_ref, b_ref, x_ref, g_ref, dres_ref, *rest):
        dx_ref, dx16_ref, dg_ref, acc_ref = rest[len(after):]
        i, kk = pl.program_id(0), pl.program_id(1)
        part = lax.dot_general(a_ref[...], b_ref[...], _NT, preferred_element_type=F32)

        def finish(dh):
            xv = x_ref[...]
            r = lax.rsqrt(jnp.mean(xv * xv, axis=-1, keepdims=True) + EPS)
            xn = xv * r
            dxn = dh * g_ref[...]
            dx = dres_ref[...] + r * (dxn - xn * jnp.mean(dxn * xn, axis=-1, keepdims=True))
            dx_ref[...] = dx
            dx16_ref[...] = dx.astype(BF16)
            dg = jnp.broadcast_to(jnp.sum(dh * xn, axis=0, keepdims=True), dg_ref.shape)

            @pl.when(i == 0)
            def _():
                dg_ref[...] = dg

            @pl.when(i > 0)
            def _():
                dg_ref[...] += dg

        @pl.when(kk == 0)
        def _():
            acc_ref[...] = part

        @pl.when((kk > 0) & (kk < nk - 1))
        def _():
            acc_ref[...] += part

        @pl.when(kk == nk - 1)
        def _():
            finish(acc_ref[...] + part)

    assert nk > 1
    row = pl.BlockSpec((tm, d), lambda i, kk: (i, 0))
    return pl.pallas_call(
        body, grid=(s // tm, nk),
        in_specs=[pl.BlockSpec((tm, tk), lambda i, kk: (i, kk)), pl.BlockSpec((d, tk), lambda i, kk: (0, kk)), row,
                  pl.BlockSpec((1, d), lambda i, kk: (0, 0)), row] + [_after_spec(t) for t in after],
        out_specs=[row, row, pl.BlockSpec((8, d), lambda i, kk: (0, 0))],
        out_shape=[jax.ShapeDtypeStruct((s, d), F32), jax.ShapeDtypeStruct((s, d), BF16),
                   jax.ShapeDtypeStruct((8, d), F32)],
        scratch_shapes=[pltpu.VMEM((tm, d), F32)], name=name,
        compiler_params=pltpu.CompilerParams(dimension_semantics=("arbitrary", "arbitrary"),
                                             vmem_limit_bytes=VMEM_LIMIT_WIDE),
    )(a, b, x, g, dres, *after)


def _loss_head(x, g, target, name):
    s, d = x.shape
    tm = _pick(s, (512, 256))

    def body(x_ref, g_ref, t_ref, loss_ref, dx_ref, dx16_ref, dg_ref):
        xv = x_ref[...]
        gv = g_ref[...]
        r = lax.rsqrt(jnp.mean(xv * xv, axis=-1, keepdims=True) + EPS)
        xn = xv * r
        err = xn * gv - t_ref[...]
        dy = err * (1.0 / d)
        dxn = dy * gv
        dx = r * (dxn - xn * jnp.mean(dxn * xn, axis=-1, keepdims=True))
        dx_ref[...] = dx
        dx16_ref[...] = dx.astype(BF16)

        @pl.when(pl.program_id(0) == 0)
        def _():
            dg_ref[...] = jnp.zeros_like(dg_ref)
            loss_ref[...] = jnp.zeros_like(loss_ref)

        dg_ref[...] += jnp.broadcast_to(jnp.sum(dy * xn, axis=0, keepdims=True), dg_ref.shape)
        row_loss = jnp.sum(err * err, axis=-1, keepdims=True)
        loss_ref[...] += jnp.broadcast_to((0.5 / d) * jnp.sum(row_loss, axis=0, keepdims=True), loss_ref.shape)

    row = pl.BlockSpec((tm, d), lambda i: (i, 0))
    return pl.pallas_call(
        body, grid=(s // tm,), in_specs=[row, pl.BlockSpec((1, d), lambda i: (0, 0)), row],
        out_specs=[pl.BlockSpec((8, LANES), lambda i: (0, 0)), row, row, pl.BlockSpec((8, d), lambda i: (0, 0))],
        out_shape=[jax.ShapeDtypeStruct((8, LANES), F32), jax.ShapeDtypeStruct((s, d), F32),
                   jax.ShapeDtypeStruct((s, d), BF16), jax.ShapeDtypeStruct((8, d), F32)],
        name=name, compiler_params=_cparams(("arbitrary",)),
    )(x, g, target)


def _chunk_cumsum(x):
    pos = lax.broadcasted_iota(jnp.int32, (x.shape[0], 1), 0) % CHUNK
    shift = 1
    while shift < CHUNK:
        x = x + jnp.where(pos >= shift, pltpu.roll(x, shift, 0), 0.0)
        shift *= 2
    return x


def _block_decay(fz, w, b):
    fg = jnp.dot(fz, w, preferred_element_type=F32) + b
    la = (jnp.minimum(fg, 0.0) - jnp.log(1.0 + jnp.exp(-jnp.abs(fg)))) * (1.0 / TAU)
    cum = _chunk_cumsum(la)
    ends = [cum[i + CHUNK - 1:i + CHUNK, :] for i in range(0, fz.shape[0], CHUNK)]
    end = jnp.concatenate([jnp.broadcast_to(e, (CHUNK, e.shape[1])) for e in ends], axis=0)
    return fg, jnp.exp(end - cum), [jnp.exp(e) for e in ends]


_TN = (((0,), (0,)), ((), ()))
_NT = (((1,), (1,)), ((), ()))


def _gla_specs(rows):
    q_spec = pl.BlockSpec((rows, HEAD_K), lambda h, c: (c, h))
    k_spec = pl.BlockSpec((rows, HEAD_K), lambda h, c: (c, HEADS + h))
    v_spec = pl.BlockSpec((rows, HEAD_V), lambda h, c: (c, HEADS + h))
    fz_spec = pl.BlockSpec((rows, LANES), lambda h, c: (c, FZ_COL // LANES))
    w_spec = pl.BlockSpec((LANES, HEAD_K), lambda h, c: (0, h))
    b_spec = pl.BlockSpec((1, HEAD_K), lambda h, c: (0, h))
    return q_spec, k_spec, v_spec, fz_spec, w_spec, b_spec


def _gla_fwd(proj, wfg, bfg, gla_g, name):
    s = proj.shape[0]
    nc = s // CHUNK
    per = _pick(nc, (8, 4, 2, 1))
    rows = per * CHUNK

    def body(q_ref, k_ref, v_ref, fz_ref, w_ref, b_ref, r_ref, g_ref, o_ref, st_ref, oa_ref, state, update):
        @pl.when(pl.program_id(1) == 0)
        def _():
            state[...] = jnp.zeros_like(state)

        _, dec, gammas = _block_decay(fz_ref[...], w_ref[...], b_ref[...])
        kd = (k_ref[...].astype(F32) * dec).astype(BF16)
        qs = (q_ref[...].astype(F32) * Q_SCALE).astype(BF16)
        for i in range(per):
            sl = slice(i * CHUNK, (i + 1) * CHUNK)
            update[i] = lax.dot_general(v_ref[sl, :], kd[sl], _TN, preferred_element_type=F32)
        st = state[...]
        for i in range(per):
            st = st * gammas[i] + update[i]
            st_ref[0, i] = st.astype(BF16)
        state[...] = st
        for i in range(per):
            sl = slice(i * CHUNK, (i + 1) * CHUNK)
            o_ref[sl, :] = lax.dot_general(qs[sl], st_ref[0, i], _NT, preferred_element_type=F32).astype(o_ref.dtype)
        ov = o_ref[...].astype(F32)
        rstd = lax.rsqrt(jnp.mean(ov * ov, axis=-1, keepdims=True) + EPS)
        rv = r_ref[...].astype(F32)
        oa_ref[...] = (ov * rstd * g_ref[...] * (rv * _sigmoid(rv))).astype(oa_ref.dtype)

    q_spec, k_spec, v_spec, fz_spec, w_spec, b_spec = _gla_specs(rows)
    head_v = pl.BlockSpec((rows, HEAD_V), lambda h, c: (c, h))
    return pl.pallas_call(
        body, grid=(HEADS, nc // per),
        in_specs=[q_spec, k_spec, v_spec, fz_spec, w_spec, b_spec,
                  pl.BlockSpec((rows, HEAD_V), lambda h, c: (c, SEG_R * (D_MODEL // HEAD_V) + h)),
                  pl.BlockSpec((1, HEAD_V), lambda h, c: (0, 0))],
        out_specs=[head_v, pl.BlockSpec((1, per, HEAD_V, HEAD_K), lambda h, c: (h, c, 0, 0)), head_v],
        out_shape=[jax.ShapeDtypeStruct((s, VAL_W), BF16),
                   jax.ShapeDtypeStruct((HEADS, nc, HEAD_V, HEAD_K), BF16), jax.ShapeDtypeStruct((s, VAL_W), BF16)],
        scratch_shapes=[pltpu.VMEM((HEAD_V, HEAD_K), F32), pltpu.VMEM((per, HEAD_V, HEAD_K), F32)], name=name,
        compiler_params=_cparams(("parallel", "arbitrary")),
    )(proj, proj, proj, proj, wfg, bfg, proj, gla_g)


def _gla_bwd(proj, wfg, bfg, states, doa, o, gla_g, dproj, name):
    s = proj.shape[0]
    nc = s // CHUNK
    per = _pick(nc, (8, 4, 2, 1))
    rows = per * CHUNK
    nblk = nc // per

    def rev(spec_fn):
        return lambda h, j: spec_fn(h, nblk - 1 - j)

    def body(q_ref, k_ref, v_ref, fz_ref, w_ref, b_ref, doa_ref, o_ref, r_ref, g_ref,
             st_ref, prev_ref, _, dq_ref, dk_ref, dv_ref, dfg_ref, db_ref, dr_ref, dg_ref,
             carry, gt_all, dkd_all, dgg_all):
        j = pl.program_id(1)

        @pl.when(j == 0)
        def _():
            carry[...] = jnp.zeros_like(carry)
            db_ref[...] = jnp.zeros_like(db_ref)

        @pl.when((j == 0) & (pl.program_id(0) == 0))
        def _():
            dg_ref[...] = jnp.zeros_like(dg_ref)

        ov = o_ref[...].astype(F32)
        rstd = lax.rsqrt(jnp.mean(ov * ov, axis=-1, keepdims=True) + EPS)
        ohat = ov * rstd
        rv = r_ref[...].astype(F32)
        sg = _sigmoid(rv)
        doav = doa_ref[...].astype(F32)
        gv = g_ref[...]
        dr_ref[...] = (doav * ohat * gv * (sg * (1.0 + rv * (1.0 - sg)))).astype(dr_ref.dtype)
        don = doav * (rv * sg)
        dg_ref[...] += jnp.broadcast_to(jnp.sum(don * ohat, axis=0, keepdims=True), dg_ref.shape)
        dohat = don * gv
        do = (rstd * (dohat - ohat * jnp.mean(dohat * ohat, axis=-1, keepdims=True))).astype(BF16)

        fg, dec, gammas = _block_decay(fz_ref[...], w_ref[...], b_ref[...])
        kd = k_ref[...].astype(F32) * dec
        kd16 = kd.astype(BF16)
        qs = (q_ref[...].astype(F32) * Q_SCALE).astype(BF16)
        for i in range(per):
            sl = slice(i * CHUNK, (i + 1) * CHUNK)
            gt_all[i] = lax.dot_general(do[sl], qs[sl], _TN, preferred_element_type=F32)
        back = carry[...]
        for i in reversed(range(per)):
            gt = back + gt_all[i]
            gt_all[i] = gt
            back = gt * gammas[i]
        carry[...] = back
        has_prev = (j < nblk - 1).astype(F32)
        for i in range(per):
            sl = slice(i * CHUNK, (i + 1) * CHUNK)
            gt = gt_all[i]
            gt16 = gt.astype(BF16)
            dq_ref[sl, :] = (jnp.dot(do[sl], st_ref[0, i], preferred_element_type=F32) * Q_SCALE).astype(dq_ref.dtype)
            dkd_all[sl, :] = jnp.dot(v_ref[sl, :], gt16, preferred_element_type=F32)
            dv_ref[sl, :] = lax.dot_general(kd16[sl], gt16, _NT, preferred_element_type=F32).astype(dv_ref.dtype)
            if i > 0:
                st_prev = st_ref[0, i - 1].astype(F32)
            else:
                st_prev = prev_ref[0, 0].astype(F32) * has_prev
            dgamma = jnp.sum(gt * st_prev, axis=0, keepdims=True)
            dgg_all[sl, :] = jnp.broadcast_to(dgamma * gammas[i], (CHUNK, HEAD_K))
        dkd = dkd_all[...]
        dk_ref[...] = (dkd * dec).astype(dk_ref.dtype)
        e = dkd * kd
        dla = dgg_all[...] + (_chunk_cumsum(e) - e)
        dfg = dla * (1.0 / TAU) * _sigmoid(-fg)
        dfg_ref[...] = dfg.astype(dfg_ref.dtype)
        db_ref[...] += jnp.broadcast_to(jnp.sum(dfg, axis=0, keepdims=True), db_ref.shape)

    q_spec, k_spec, v_spec, fz_spec, w_spec, b_spec = _gla_specs(rows)
    q_spec, k_spec, v_spec, fz_spec = [
        pl.BlockSpec(sp.block_shape, rev(sp.index_map)) for sp in (q_spec, k_spec, v_spec, fz_spec)]
    do_spec = pl.BlockSpec((rows, HEAD_V), lambda h, j: (nblk - 1 - j, h))
    st_spec = pl.BlockSpec((1, per, HEAD_V, HEAD_K), lambda h, j: (h, nblk - 1 - j, 0, 0))
    prev_spec = pl.BlockSpec((1, 1, HEAD_V, HEAD_K),
                             lambda h, j: (h, jnp.maximum((nblk - 1 - j) * per - 1, 0), 0, 0))
    key_out = pl.BlockSpec((rows, HEAD_K), lambda h, j: (nblk - 1 - j, h))
    r_spec = pl.BlockSpec((rows, HEAD_V), lambda h, j: (nblk - 1 - j, SEG_R * (D_MODEL // HEAD_V) + h))
    return pl.pallas_call(
        body, grid=(HEADS, nblk),
        in_specs=[q_spec, k_spec, v_spec, fz_spec, w_spec, b_spec,
                  do_spec, do_spec, r_spec, pl.BlockSpec((1, HEAD_V), lambda h, j: (0, 0)),
                  st_spec, prev_spec, _ANY],
        out_specs=[key_out, key_out, do_spec, key_out, pl.BlockSpec((8, HEAD_K), lambda h, j: (0, h)),
                   r_spec, pl.BlockSpec((8, HEAD_V), lambda h, j: (0, 0))],
        out_shape=[jax.ShapeDtypeStruct((s, KEY_W), BF16), jax.ShapeDtypeStruct((s, KEY_W), BF16),
                   jax.ShapeDtypeStruct((s, VAL_W), BF16), jax.ShapeDtypeStruct((s, KEY_W), BF16),
                   jax.ShapeDtypeStruct((8, KEY_W), F32), jax.ShapeDtypeStruct(dproj.shape, BF16),
                   jax.ShapeDtypeStruct((8, HEAD_V), F32)],
        input_output_aliases={12: 5},
        scratch_shapes=[pltpu.VMEM((HEAD_V, HEAD_K), F32), pltpu.VMEM((per, HEAD_V, HEAD_K), F32),
                        pltpu.VMEM((rows, HEAD_K), F32), pltpu.VMEM((rows, HEAD_K), F32)], name=name,
        compiler_params=_cparams(("arbitrary", "arbitrary")),
    )(proj, proj, proj, proj, wfg, bfg, doa, o, proj, gla_g, states, states, dproj)


def _place_qkv(dq, dk, dv, dproj, name):
    s = dq.shape[0]
    tm = _pick(s, (512, 256))

    def body(dq_ref, dk_ref, dv_ref, _, o_ref):
        o_ref[:, 0:KEY_W] = dq_ref[...]
        o_ref[:, KEY_W:2 * KEY_W] = dk_ref[...]
        o_ref[:, 2 * KEY_W:2 * KEY_W + VAL_W] = dv_ref[...]

    def rows(width):
        return pl.BlockSpec((tm, width), lambda i: (i, 0))

    return pl.pallas_call(
        body, grid=(s // tm,), in_specs=[rows(KEY_W), rows(KEY_W), rows(VAL_W), _ANY],
        out_specs=rows(2 * KEY_W + VAL_W), out_shape=jax.ShapeDtypeStruct(dproj.shape, BF16),
        input_output_aliases={3: 0}, name=name, compiler_params=_cparams(("parallel",)),
    )(dq, dk, dv, dproj)


def _seg(tm, seg):
    return pl.BlockSpec((tm, D_MODEL), lambda i: (i, seg))


HALO = SUBLANES_BF16


def _shift_down(u, p1, p2, n, rows):
    rolled = pltpu.roll(u, n, 0)
    if n == 1:
        return jnp.where(rows == 0, p1, rolled)
    return jnp.where(rows == 0, p2, jnp.where(rows == 1, p1, rolled))


def _shift_up(u, n1, n2, n, rows, tm):
    rolled = pltpu.roll(u, tm - n, 0)
    if n == 1:
        return jnp.where(rows == tm - 1, n1, rolled)
    return jnp.where(rows == tm - 2, n1, jnp.where(rows == tm - 1, n2, rolled))


def _conv_terms(gc_ref, cx_ref, gcp_ref, cxp_ref, tm):
    i = pl.program_id(0)
    u = gc_ref[...].astype(F32) * cx_ref[...].astype(F32)
    up = gcp_ref[...].astype(F32) * cxp_ref[...].astype(F32) * (i > 0).astype(F32)
    rows = lax.broadcasted_iota(jnp.int32, (tm, 1), 0)
    u1 = _shift_down(u, up[HALO - 1:HALO, :], up[HALO - 2:HALO - 1, :], 1, rows)
    u2 = _shift_down(u, up[HALO - 1:HALO, :], up[HALO - 2:HALO - 1, :], 2, rows)
    return u, u1, u2, rows


def _prev_halo(tm, seg):
    return pl.BlockSpec((HALO, D_MODEL), lambda i: (jnp.maximum(i * (tm // HALO) - 1, 0), seg))


def _conv_fwd(proj, w, b, name):
    s = proj.shape[0]
    tm = _pick(s, (512, 256))

    def body(gbi_ref, gc_ref, cx_ref, gcp_ref, cxp_ref, w_ref, b_ref, cb_ref):
        u, u1, u2, _ = _conv_terms(gc_ref, cx_ref, gcp_ref, cxp_ref, tm)
        conv = w_ref[0:1, :] * u2 + w_ref[1:2, :] * u1 + w_ref[2:3, :] * u + b_ref[...]
        cb_ref[...] = (gbi_ref[...].astype(F32) * conv).astype(cb_ref.dtype)

    return pl.pallas_call(
        body, grid=(s // tm,),
        in_specs=[_seg(tm, SEG_GBI), _seg(tm, SEG_GCI), _seg(tm, SEG_CX),
                  _prev_halo(tm, SEG_GCI), _prev_halo(tm, SEG_CX),
                  pl.BlockSpec((3, D_MODEL), lambda i: (0, 0)), pl.BlockSpec((1, D_MODEL), lambda i: (0, 0))],
        out_specs=pl.BlockSpec((tm, D_MODEL), lambda i: (i, 0)),
        out_shape=jax.ShapeDtypeStruct((s, D_MODEL), BF16), name=name, compiler_params=_cparams(("parallel",)),
    )(proj, proj, proj, proj, proj, w, b)


def _conv_bwd(dcb, proj, w, b, dproj, name):
    s = proj.shape[0]
    tm = _pick(s, (512, 256))
    nt = s // tm

    def body(dcb_ref, gbi_ref, gc_ref, cx_ref, gcp_ref, cxp_ref, dcbn_ref, gbin_ref, w_ref, b_ref, _,
             d3_ref, dwb_ref):
        i = pl.program_id(0)

        @pl.when(i == 0)
        def _():
            dwb_ref[...] = jnp.zeros_like(dwb_ref)

        u, u1, u2, rows = _conv_terms(gc_ref, cx_ref, gcp_ref, cxp_ref, tm)
        w0, w1, w2 = w_ref[0:1, :], w_ref[1:2, :], w_ref[2:3, :]
        conv = w0 * u2 + w1 * u1 + w2 * u + b_ref[...]
        dcbv = dcb_ref[...].astype(F32)
        gbi = gbi_ref[...].astype(F32)
        dconv = dcbv * gbi
        dnext = dcbn_ref[...].astype(F32) * gbin_ref[...].astype(F32) * (i < nt - 1).astype(F32)
        dc1 = _shift_up(dconv, dnext[0:1, :], dnext[1:2, :], 1, rows, tm)
        dc2 = _shift_up(dconv, dnext[0:1, :], dnext[1:2, :], 2, rows, tm)
        du = w2 * dconv + w1 * dc1 + w0 * dc2
        d3_ref[:, 0:D_MODEL] = (dcbv * conv).astype(d3_ref.dtype)
        d3_ref[:, D_MODEL:2 * D_MODEL] = (du * cx_ref[...].astype(F32)).astype(d3_ref.dtype)
        d3_ref[:, 2 * D_MODEL:3 * D_MODEL] = (du * gc_ref[...].astype(F32)).astype(d3_ref.dtype)
        dwb_ref[0:1, :] += jnp.sum(dconv * u2, axis=0, keepdims=True)
        dwb_ref[1:2, :] += jnp.sum(dconv * u1, axis=0, keepdims=True)
        dwb_ref[2:3, :] += jnp.sum(dconv * u, axis=0, keepdims=True)
        dwb_ref[3:4, :] += jnp.sum(dconv, axis=0, keepdims=True)

    def next_halo(seg_fn):
        return pl.BlockSpec((HALO, D_MODEL), lambda i: (jnp.minimum((i + 1) * (tm // HALO), s // HALO - 1), seg_fn))

    return pl.pallas_call(
        body, grid=(nt,),
        in_specs=[pl.BlockSpec((tm, D_MODEL), lambda i: (i, 0)),
                  _seg(tm, SEG_GBI), _seg(tm, SEG_GCI), _seg(tm, SEG_CX),
                  _prev_halo(tm, SEG_GCI), _prev_halo(tm, SEG_CX),
                  next_halo(0), next_halo(SEG_GBI),
                  pl.BlockSpec((3, D_MODEL), lambda i: (0, 0)), pl.BlockSpec((1, D_MODEL), lambda i: (0, 0)), _ANY],
        out_specs=[pl.BlockSpec((tm, 3 * D_MODEL), lambda i: (i, SEG_GBI // 3)),
                   pl.BlockSpec((8, D_MODEL), lambda i: (0, 0))],
        out_shape=[jax.ShapeDtypeStruct(dproj.shape, BF16), jax.ShapeDtypeStruct((8, D_MODEL), F32)],
        input_output_aliases={10: 0}, name=name, compiler_params=_cparams(("arbitrary",)),
    )(dcb, proj, proj, proj, proj, proj, dcb, proj, w, b, dproj)


def _mix_project(proj, ya, yb, w_o, x, norm_g, name):
    s = proj.shape[0]
    tm = _pick(s, (512, 256))

    def body(ga_ref, gb_ref, ya_ref, yb_ref, w_ref, x_ref, g_ref, mix_ref, x1_ref, h2_ref):
        mix = (_sigmoid(ga_ref[...].astype(F32)) * ya_ref[...].astype(F32)
               + _sigmoid(gb_ref[...].astype(F32)) * yb_ref[...].astype(F32)).astype(BF16)
        mix_ref[...] = mix
        x1 = x_ref[...] + jnp.dot(mix, w_ref[...], preferred_element_type=F32)
        x1_ref[...] = x1
        rstd = lax.rsqrt(jnp.mean(x1 * x1, axis=-1, keepdims=True) + EPS)
        h2_ref[...] = (x1 * rstd * g_ref[...]).astype(BF16)

    row = pl.BlockSpec((tm, D_MODEL), lambda i: (i, 0))
    whole = pl.BlockSpec((D_MODEL, D_MODEL), lambda i: (0, 0))
    return pl.pallas_call(
        body, grid=(s // tm,),
        in_specs=[_seg(tm, SEG_GA), _seg(tm, SEG_GB), row, row, whole, row, pl.BlockSpec((1, D_MODEL), lambda i: (0, 0))],
        out_specs=[row, row, row],
        out_shape=[jax.ShapeDtypeStruct((s, D_MODEL), BF16), jax.ShapeDtypeStruct((s, D_MODEL), F32),
                   jax.ShapeDtypeStruct((s, D_MODEL), BF16)],
        name=name, compiler_params=_cparams(("parallel",)),
    )(proj, proj, ya, yb, w_o, x, norm_g)


def _mix_bwd(dmix, proj, ya, yb, name):
    s = proj.shape[0]
    tm = _pick(s, (512, 256))

    def body(dm_ref, ga_ref, gb_ref, ya_ref, yb_ref, dg_ref, dya_ref, dyb_ref):
        dm = dm_ref[...].astype(F32)
        sa = _sigmoid(ga_ref[...].astype(F32))
        sb = _sigmoid(gb_ref[...].astype(F32))
        dg_ref[:, 0:D_MODEL] = (dm * ya_ref[...].astype(F32) * sa * (1.0 - sa)).astype(dg_ref.dtype)
        dg_ref[:, D_MODEL:2 * D_MODEL] = (dm * yb_ref[...].astype(F32) * sb * (1.0 - sb)).astype(dg_ref.dtype)
        dya_ref[...] = (dm * sa).astype(dya_ref.dtype)
        dyb_ref[...] = (dm * sb).astype(dyb_ref.dtype)

    row = pl.BlockSpec((tm, D_MODEL), lambda i: (i, 0))
    return pl.pallas_call(
        body, grid=(s // tm,), in_specs=[row, _seg(tm, SEG_GA), _seg(tm, SEG_GB), row, row],
        out_specs=[pl.BlockSpec((tm, 2 * D_MODEL), lambda i: (i, SEG_GA // 2)), row, row],
        out_shape=[jax.ShapeDtypeStruct((s, PROJ_W), BF16), jax.ShapeDtypeStruct((s, D_MODEL), BF16),
                   jax.ShapeDtypeStruct((s, D_MODEL), BF16)],
        name=name, compiler_params=_cparams(("parallel",)),
    )(dmix, proj, proj, ya, yb)


def _swiglu_fwd(gu, name):
    s = gu.shape[0]
    tm = _pick(s, (256,))

    def body(gu_ref, o_ref):
        gate = gu_ref[:, 0:FFN].astype(F32)
        o_ref[...] = (gate * _sigmoid(gate) * gu_ref[:, FFN:2 * FFN].astype(F32)).astype(o_ref.dtype)

    return pl.pallas_call(
        body, grid=(s // tm,), in_specs=[pl.BlockSpec((tm, 2 * FFN), lambda i: (i, 0))],
        out_specs=pl.BlockSpec((tm, FFN), lambda i: (i, 0)),
        out_shape=jax.ShapeDtypeStruct((s, FFN), BF16), name=name, compiler_params=_cparams(("parallel",)),
    )(gu)


def _swiglu_bwd(dhid, gu, name):
    s = gu.shape[0]
    tm = _pick(s, (256,))

    def body(dh_ref, gu_ref, o_ref):
        gate = gu_ref[:, 0:FFN].astype(F32)
        up = gu_ref[:, FFN:2 * FFN].astype(F32)
        dh = dh_ref[...].astype(F32)
        sg = _sigmoid(gate)
        o_ref[:, 0:FFN] = (dh * up * (sg * (1.0 + gate * (1.0 - sg)))).astype(o_ref.dtype)
        o_ref[:, FFN:2 * FFN] = (dh * gate * sg).astype(o_ref.dtype)

    wide = pl.BlockSpec((tm, 2 * FFN), lambda i: (i, 0))
    return pl.pallas_call(
        body, grid=(s // tm,), in_specs=[pl.BlockSpec((tm, FFN), lambda i: (i, 0)), wide], out_specs=wide,
        out_shape=jax.ShapeDtypeStruct((s, 2 * FFN), BF16), name=name, compiler_params=_cparams(("parallel",)),
    )(dhid, gu)


def _adamw_math(w, g, m, v):
    m2 = ADAM_B1 * m + (1.0 - ADAM_B1) * g
    v2 = ADAM_B2 * v + (1.0 - ADAM_B2) * (g * g)
    m_hat = m2 / (1.0 - ADAM_B1 ** ADAM_STEP)
    v_hat = v2 / (1.0 - ADAM_B2 ** ADAM_STEP)
    delta = -ADAM_LR * (m_hat / (jnp.sqrt(v_hat) + ADAM_EPS) + ADAM_WD * w)
    return delta, m2, v2


def _adamw(w, g, m, v, name):
    shape = w.shape
    cols = shape[-1]
    rows = int(np.prod(shape[:-1])) if len(shape) > 1 else 1
    w2, g2, m2, v2 = [t.reshape(rows, cols) for t in (w, g, m, v)]
    tr = _pick(rows, (512, 352, 256)) if rows % 8 == 0 else rows

    def body(w_ref, g_ref, m_ref, v_ref, d_ref, nm_ref, nv_ref):
        d, nm, nv = _adamw_math(w_ref[...], g_ref[...], m_ref[...], v_ref[...])
        d_ref[...] = d
        nm_ref[...] = nm
        nv_ref[...] = nv

    blk = pl.BlockSpec((tr, cols), lambda i: (i, 0))
    out = pl.pallas_call(
        body, grid=(rows // tr,), in_specs=[blk] * 4, out_specs=[blk] * 3,
        out_shape=[jax.ShapeDtypeStruct((rows, cols), F32)] * 3, name=name,
        compiler_params=_cparams(("parallel",)),
    )(w2, g2, m2, v2)
    return [t.reshape(shape) for t in out]


def _pair_sum(g2, recv, core, name, after=()):
    _, nchip, r, c = g2.shape
    tr = _pick(r, (512,))

    def body(core_ref, a_ref, b_ref, *rest):
        o_ref = rest[-1]
        o_ref[...] = (a_ref[...].astype(F32) + b_ref[...].astype(F32)).astype(o_ref.dtype)

    grid_spec = pltpu.PrefetchScalarGridSpec(
        num_scalar_prefetch=1, grid=(nchip, r // tr),
        in_specs=[pl.BlockSpec((None, None, tr, c), lambda k, i, cr: (cr[0], k, i, 0)),
                  pl.BlockSpec((None, tr, c), lambda k, i, cr: (k, i, 0))] + [_after_spec(t) for t in after],
        out_specs=pl.BlockSpec((None, tr, c), lambda k, i, cr: (k, i, 0)))
    return pl.pallas_call(
        body, grid_spec=grid_spec, out_shape=jax.ShapeDtypeStruct((nchip, r, c), BF16), name=name,
        compiler_params=_cparams(("parallel", "parallel")),
    )(core, g2, recv, *after)


def _chip_sum(a, recv, chip, name, after=()):
    _, r, c = a.shape
    tr = _pick(r, (512,))

    def body(chip_ref, a_ref, b_ref, *rest):
        o_ref = rest[-1]
        o_ref[...] = ((a_ref[...].astype(F32) + b_ref[0].astype(F32)) + b_ref[1].astype(F32)) + b_ref[2].astype(F32)

    grid_spec = pltpu.PrefetchScalarGridSpec(
        num_scalar_prefetch=1, grid=(r // tr,),
        in_specs=[pl.BlockSpec((None, tr, c), lambda i, cr: (cr[0], i, 0)),
                  pl.BlockSpec((3, tr, c), lambda i, cr: (0, i, 0))] + [_after_spec(t) for t in after],
        out_specs=pl.BlockSpec((tr, c), lambda i, cr: (i, 0)))
    return pl.pallas_call(
        body, grid_spec=grid_spec, out_shape=jax.ShapeDtypeStruct((r, c), F32), name=name,
        compiler_params=_cparams(("parallel",)),
    )(chip, a, recv, *after)


def _sum_devices(parts, name):
    n, r, c = parts.shape

    def body(p_ref, o_ref):
        acc = p_ref[0]
        for d in range(1, n):
            acc = acc + p_ref[d]
        o_ref[...] = acc

    return pl.pallas_call(
        body, out_shape=jax.ShapeDtypeStruct((r, c), F32), name=name,
        in_specs=[pl.BlockSpec(memory_space=pltpu.VMEM)], out_specs=pl.BlockSpec(memory_space=pltpu.VMEM),
    )(parts)


def _lane_iota():
    return lax.broadcasted_iota(jnp.int32, (1, LANES), 1)


def _tiles_up(tiles, s, lane):
    rolled = [pltpu.roll(t, s, 1) for t in tiles]
    zero = jnp.zeros_like(tiles[0])
    return [jnp.where(lane < s, p, c) for p, c in zip([zero] + rolled, rolled + [zero])]


def _tiles_down(tiles, s, lane):
    back = (LANES - s) % LANES
    rolled = [pltpu.roll(t, back, 1) for t in tiles]
    zero = jnp.zeros_like(tiles[0])
    return [jnp.where(lane < LANES - s, c, n) for c, n in zip(rolled, rolled[1:] + [zero])]


def _window_cols(prm, w_in, w_gate, w_up, l, name, after=None):
    tm = 256
    n_in, n_gu = WIN_IN // LANES, FFN_SHARD // LANES + 1

    def body(prm_ref, win_ref, g_ref, u_ref, *rest):
        out_ref, gu_ref, scr_in, scr_gu = rest[-4:]
        lane = _lane_iota()
        s_main, lo, hi, s_code, clo, chi, code_hi, s_gu = [prm_ref[i] for i in range(8)]
        scr_in[:, D_MODEL:WIN_IN] = jnp.zeros((tm, WIN_IN - D_MODEL), F32)
        scr_in[:, 0:W_IN_SHARD] = win_ref[...]

        def keep(t, a, b):
            col = lane + t * LANES
            return jnp.where((col >= a) & (col < b), scr_in[:, t * LANES:(t + 1) * LANES], 0.0)

        main = _tiles_up([keep(t, lo, hi) for t in range(n_in)], s_main, lane)
        for t in range(n_in):
            out_ref[:, t * LANES:(t + 1) * LANES] = main[t].astype(BF16)
        low = _tiles_up([keep(0, clo, chi)], s_code, lane)[0]
        high = _tiles_up([keep(n_in - 2, clo, chi), keep(n_in - 1, clo, chi)], s_code, lane)[1]
        out_ref[:, A_FZ:A_FZ + LANES] = jnp.where(code_hi == 1, high, low).astype(BF16)
        for ref, base in ((g_ref, 0), (u_ref, WIN_GU)):
            scr_gu[:, (n_gu - 1) * LANES:n_gu * LANES] = jnp.zeros((tm, LANES), F32)
            scr_gu[:, 0:FFN_SHARD] = ref[...]
            moved = _tiles_up([scr_gu[:, t * LANES:(t + 1) * LANES] for t in range(n_gu)], s_gu, lane)
            for t in range(n_gu + 1):
                gu_ref[:, base + t * LANES:base + (t + 1) * LANES] = moved[t].astype(BF16)

    after_args = [] if after is None else [after]
    grid_spec = pltpu.PrefetchScalarGridSpec(
        num_scalar_prefetch=1, grid=(D_MODEL // tm,),
        in_specs=[pl.BlockSpec((None, tm, W_IN_SHARD), lambda i, p: (l, i, 0)),
                  pl.BlockSpec((None, tm, FFN_SHARD), lambda i, p: (l, i, 0)),
                  pl.BlockSpec((None, tm, FFN_SHARD), lambda i, p: (l, i, 0))] + [_after_spec(t) for t in after_args],
        out_specs=[pl.BlockSpec((tm, IN_COLS), lambda i, p: (i, 0)), pl.BlockSpec((tm, GU_COLS), lambda i, p: (i, 0))],
        scratch_shapes=[pltpu.VMEM((tm, WIN_IN), F32), pltpu.VMEM((tm, n_gu * LANES), F32)])
    return pl.pallas_call(
        body, grid_spec=grid_spec, name=name, compiler_params=_cparams(("parallel",)),
        out_shape=[jax.ShapeDtypeStruct((D_MODEL, IN_COLS), BF16), jax.ShapeDtypeStruct((D_MODEL, GU_COLS), BF16)],
    )(prm, w_in, w_gate, w_up, *after_args)


def _gu_width(j):
    return min(WIN_GU, FFN - T_GU[j] * LANES)


def _after_spec(t):
    tile = (SUBLANES_BF16 if t.dtype == BF16 else 8, LANES)
    return pl.BlockSpec((None,) * (t.ndim - 2) + tile, lambda *_: (0,) * t.ndim)


def _assemble_in(a_all, tail_all, name, after=()):
    tm = D_MODEL // N_DEV

    def body(a_ref, t_ref, *rest):
        win_ref, tail_ref = rest[-2:]
        tail_ref[...] = t_ref[...]
        win_ref[...] = jnp.zeros_like(win_ref)
        code = a_ref[0, :, A_FZ:A_FZ + LANES]
        for j in range(N_DEV):
            c0 = T_IN[j] * LANES
            win_ref[:, c0:c0 + WIN_IN] += a_ref[j, :, 0:WIN_IN]
            if j > 0:
                code = code + a_ref[j, :, A_FZ:A_FZ + LANES]
        win_ref[:, FZ_COL:PROJ_W] = code

    return pl.pallas_call(
        body, grid=(N_DEV,),
        in_specs=[pl.BlockSpec((N_DEV, tm, IN_COLS), lambda i: (0, i, 0)),
                  pl.BlockSpec((None,) + tail_all.shape[1:], lambda i: (i, 0, 0))] + [_after_spec(t) for t in after],
        out_specs=[pl.BlockSpec((tm, PROJ_W), lambda i: (i, 0)),
                   pl.BlockSpec((None,) + tail_all.shape[1:], lambda i: (i, 0, 0))],
        out_shape=[jax.ShapeDtypeStruct((D_MODEL, PROJ_W), BF16), jax.ShapeDtypeStruct(tail_all.shape, BF16)],
        name=name, compiler_params=_cparams(("parallel",)),
    )(a_all, tail_all, *after)


def _assemble_rest(a_all, rows_all, name, after=()):
    tm = D_MODEL // N_DEV
    n_in = 2 + len(after)

    def body(*refs):
        a_ref, r_ref = refs[:2]
        wgu_ref, oa_ref, ob_ref, o_ref, down_ref = refs[n_in:]
        wgu_ref[...] = jnp.zeros_like(wgu_ref)
        for j in range(N_DEV):
            g0, width = T_GU[j] * LANES, _gu_width(j)
            wgu_ref[:, g0:g0 + width] += a_ref[j, :, 0:width]
            wgu_ref[:, FFN + g0:FFN + g0 + width] += a_ref[j, :, WIN_GU:WIN_GU + width]
        oa_ref[...] = r_ref[B_OA:B_OA + ROW_SHARD, :]
        ob_ref[...] = r_ref[B_OB:B_OB + ROW_SHARD, :]
        o_ref[...] = r_ref[B_O:B_O + ROW_SHARD, :]
        down_ref[...] = r_ref[B_DOWN:B_DOWN + FFN_SHARD, :]

    def rows(n):
        return pl.BlockSpec((n, D_MODEL), lambda i: (i, 0))

    square = jax.ShapeDtypeStruct((D_MODEL, D_MODEL), BF16)
    return pl.pallas_call(
        body, grid=(N_DEV,),
        in_specs=[pl.BlockSpec((N_DEV, tm, GU_COLS), lambda i: (0, i, 0)),
                  pl.BlockSpec((None, B_FG2, D_MODEL), lambda i: (i, 0, 0))] + [_after_spec(t) for t in after],
        out_specs=[pl.BlockSpec((tm, 2 * FFN), lambda i: (i, 0)),
                   rows(ROW_SHARD), rows(ROW_SHARD), rows(ROW_SHARD), rows(FFN_SHARD)],
        out_shape=[jax.ShapeDtypeStruct((D_MODEL, 2 * FFN), BF16),
                   square, square, square, jax.ShapeDtypeStruct((FFN, D_MODEL), BF16)],
        name=name, compiler_params=_cparams(("parallel",)),
    )(a_all, rows_all, *after)


def _grad_windows_in(d_in, name):
    tm = 256

    def body(din_ref, out_ref):
        for j in range(N_DEV):
            c0 = T_IN[j] * LANES
            out_ref[j & 1, j >> 1, :, 0:WIN_IN] = din_ref[:, c0:c0 + WIN_IN]
            out_ref[j & 1, j >> 1, :, A_FZ:IN_COLS] = din_ref[:, FZ_COL:PROJ_W]

    return pl.pallas_call(
        body, grid=(D_MODEL // tm,), in_specs=[pl.BlockSpec((tm, PROJ_W), lambda i: (i, 0))],
        out_specs=pl.BlockSpec((2, 4, tm, IN_COLS), lambda i: (0, 0, i, 0)),
        out_shape=jax.ShapeDtypeStruct((2, 4, D_MODEL, IN_COLS), BF16), name=name,
        compiler_params=_cparams(("parallel",)),
    )(d_in)


def _grad_windows_gu(d_gu, name):
    tm = 256

    def body(dgu_ref, out_ref):
        for j in range(N_DEV):
            g0, width = T_GU[j] * LANES, _gu_width(j)
            for half, base in ((0, 0), (FFN, WIN_GU)):
                out_ref[j & 1, j >> 1, :, base:base + width] = dgu_ref[:, half + g0:half + g0 + width]
                if width < WIN_GU:
                    out_ref[j & 1, j >> 1, :, base + width:base + WIN_GU] = jnp.zeros((tm, WIN_GU - width), BF16)

    return pl.pallas_call(
        body, grid=(D_MODEL // tm,), in_specs=[pl.BlockSpec((tm, 2 * FFN), lambda i: (i, 0))],
        out_specs=pl.BlockSpec((2, 4, tm, GU_COLS), lambda i: (0, 0, i, 0)),
        out_shape=jax.ShapeDtypeStruct((2, 4, D_MODEL, GU_COLS), BF16), name=name,
        compiler_params=_cparams(("parallel",)),
    )(d_gu)


def _final_windows(prm, chip, a, recv, params, l, kind, name, prev=None):
    tm = 128
    n_in, n_gu = WIN_IN // LANES, FFN_SHARD // LANES + 1
    widths = (W_IN_SHARD,) if kind == "in" else (FFN_SHARD, FFN_SHARD)
    cols = IN_COLS if kind == "in" else GU_COLS
    n_par = 3 * len(widths)
    prev = [] if prev is None else [t for group in prev for t in group]

    def body(prm_ref, chip_ref, a_ref, r_ref, *refs):
        ins, outs = refs[:n_par], refs[n_par + len(prev):]
        lane = _lane_iota()
        s_main, s_code, clo, chi, s_gu = [prm_ref[i] for i in (0, 3, 4, 5, 7)]

        def total(c0):
            sl = slice(c0, c0 + LANES)
            return ((a_ref[:, sl].astype(F32) + r_ref[0, :, sl].astype(F32)) + r_ref[1, :, sl].astype(F32)) \
                + r_ref[2, :, sl].astype(F32)

        if kind == "in":
            grads = _tiles_down([total(t * LANES) for t in range(n_in)], s_main, lane)
            code = pltpu.roll(total(A_FZ), (LANES - s_code) % LANES, 1)
            for t in (0, n_in - 2, n_in - 1):
                col = lane + t * LANES
                grads[t] = jnp.where((col >= clo) & (col < chi), code, grads[t])
            per_weight = [grads]
        else:
            per_weight = [_tiles_down([total(base + t * LANES) for t in range(n_gu + 1)], s_gu, lane)[:n_gu]
                          for base in (0, WIN_GU)]
        for k, (tiles, width) in enumerate(zip(per_weight, widths)):
            w_ref, m_ref, v_ref = ins[3 * k:3 * k + 3]
            g_ref, d_ref, nm_ref, nv_ref = outs[4 * k:4 * k + 4]
            for t, g in enumerate(tiles):
                n = min(LANES, width - t * LANES)
                sl = slice(t * LANES, t * LANES + n)
                g = g[:, 0:n]
                d, nm, nv = _adamw_math(w_ref[:, sl], g, m_ref[:, sl], v_ref[:, sl])
                g_ref[:, sl] = g
                d_ref[:, sl] = d
                nm_ref[:, sl] = nm
                nv_ref[:, sl] = nv

    def native(width):
        return pl.BlockSpec((None, tm, width), lambda i, p, c: (l, i, 0))

    in_specs = [pl.BlockSpec((None, tm, cols), lambda i, p, c: (c[0], i, 0)),
                pl.BlockSpec((3, tm, cols), lambda i, p, c: (0, i, 0))]
    in_specs += [native(wd) for wd in widths for _ in range(3)] + [_ANY] * len(prev)
    grid_spec = pltpu.PrefetchScalarGridSpec(
        num_scalar_prefetch=2, grid=(D_MODEL // tm,), in_specs=in_specs,
        out_specs=[native(wd) for wd in widths for _ in range(4)])
    out = pl.pallas_call(
        body, grid_spec=grid_spec, name=name, compiler_params=_cparams(("parallel",)),
        out_shape=[jax.ShapeDtypeStruct((DEPTH, D_MODEL, wd), F32) for wd in widths for _ in range(4)],
        input_output_aliases={4 + n_par + k: k for k in range(len(prev))},
    )(prm, chip, a, recv, *params, *prev)
    return [out[4 * k:4 * k + 4] for k in range(len(widths))]


def _me():
    return lax.axis_index("x"), lax.axis_index("y"), lax.axis_index("c")


_CHIP_FLIPS = ((1, 0), (0, 1), (1, 1))
_ANY = pl.BlockSpec(memory_space=pl.ANY)


def _comm_call(body, peers, out_shape, sems, name, args, collective_id):
    if collective_id is None:
        n_in = len(args)
        return pl.pallas_call(body, out_shape=out_shape, name=name, in_specs=[_ANY] * n_in,
                              out_specs=[_ANY] * len(out_shape), scratch_shapes=sems)(*args)

    def sequencer_body(*refs):
        barrier = pltpu.get_barrier_semaphore()
        targets = peers()
        for peer in targets:
            pl.semaphore_signal(barrier, inc=1, device_id=peer, device_id_type=MESH)
        pl.semaphore_wait(barrier, len(targets))
        body(*refs)

    sequencer = plsc.ScalarSubcoreMesh(axis_name="seq", num_cores=1)
    return pl.kernel(sequencer_body, out_type=out_shape, mesh=sequencer, scratch_types=sems, name=name,
                     compiler_params=pltpu.CompilerParams(collective_id=collective_id))(*args)


def _sibling_peer():
    x, y, cc = _me()
    return [(x, y, 1 - cc)]


def _chip_peers():
    x, y, cc = _me()
    return [(x ^ fx, y ^ fy, cc) for fx, fy in _CHIP_FLIPS]


def _all_gather(shards, name, collective_id=None):
    n = len(shards)
    split = [s.shape[0] % (2 * SUBLANES_BF16) == 0 for s in shards]

    def body(*refs):
        x_refs, out_refs = refs[:n], refs[n:2 * n]
        send_sems, recv_sems, local_sems = refs[2 * n:]
        x, y, cc = _me()
        me, sibling = (x, y, cc), (x, y, 1 - cc)
        near, far = [(x ^ 1, y), (x, y ^ 1)], (x ^ 1, y ^ 1)

        def copy(a, k, block, to, half=None, from_shard=False):
            px, py, pc = block
            slot = out_refs[a].at[4 * px + 2 * py + pc]
            if half is not None:
                rows = shards[a].shape[0] // 2
                slot = slot.at[pl.ds(half * rows, rows)]
            return pltpu.make_async_remote_copy(
                src_ref=x_refs[a] if from_shard else slot, dst_ref=slot,
                send_sem=send_sems.at[a, k], recv_sem=recv_sems.at[a, k], device_id=to, device_id_type=MESH)

        mine = [pltpu.make_async_copy(x_refs[a], out_refs[a].at[4 * x + 2 * y + cc], local_sems.at[a])
                for a in range(n)]
        for cp in mine:
            cp.start()
        sent = [copy(a, 0, me, sibling, from_shard=True) for a in range(n)]
        sent += [copy(a, 1 + j, me, (*chip, cc), from_shard=True) for j, chip in enumerate(near) for a in range(n)]
        sent += [copy(a, 3, me, (*far, cc), from_shard=True) for a in range(n) if not split[a]]
        for cp in sent:
            cp.start()

        def pass_on(cp):
            cp.start()
            sent.append(cp)

        for j, chip in enumerate(near):
            for a in range(n):
                copy(a, 1 + j, (*chip, cc), me).wait_recv()
                pass_on(copy(a, 4 + j, (*chip, cc), sibling))
                if split[a]:
                    pass_on(copy(a, 7 + j, (*chip, cc), (*near[1 - j], cc), half=j))
        for a in range(n):
            if split[a]:
                copy(a, 7, (*far, cc), me, half=0).wait_recv()
                copy(a, 8, (*far, cc), me, half=1).wait_recv()
            else:
                copy(a, 3, (*far, cc), me).wait_recv()
            pass_on(copy(a, 6, (*far, cc), sibling))
        for a in range(n):
            copy(a, 0, sibling, me).wait_recv()
            for j, chip in enumerate(near + [far]):
                copy(a, 4 + j, (*chip, 1 - cc), me).wait_recv()
        for cp in sent:
            cp.wait_send()
        for cp in mine:
            cp.wait()

    return _comm_call(
        body, lambda: _sibling_peer() + _chip_peers(),
        [jax.ShapeDtypeStruct((N_DEV,) + s.shape, s.dtype) for s in shards],
        [pltpu.SemaphoreType.DMA((n, 9)), pltpu.SemaphoreType.DMA((n, 9)), pltpu.SemaphoreType.DMA((n,))],
        name, shards, collective_id)


def _send_to_sibling(parts, name, collective_id=None):
    n = len(parts)

    def body(*refs):
        g_refs, out_refs = refs[:n], refs[n:2 * n]
        send_sems, recv_sems = refs[2 * n:]
        x, y, cc = _me()
        copies = [pltpu.make_async_remote_copy(
            src_ref=g_refs[a].at[1 - cc], dst_ref=out_refs[a], send_sem=send_sems.at[a], recv_sem=recv_sems.at[a],
            device_id=(x, y, 1 - cc), device_id_type=MESH) for a in range(n)]
        for cp in copies:
            cp.start()
        for cp in copies:
            cp.wait()

    return _comm_call(
        body, _sibling_peer, [jax.ShapeDtypeStruct(p.shape[1:], p.dtype) for p in parts],
        [pltpu.SemaphoreType.DMA((n,)), pltpu.SemaphoreType.DMA((n,))], name, parts, collective_id)


def _send_to_chips(parts, name, collective_id=None):
    n = len(parts)

    def body(*refs):
        a_refs, out_refs = refs[:n], refs[n:2 * n]
        send_sems, recv_sems = refs[2 * n:]
        x, y, cc = _me()
        copies = []
        for k, (fx, fy) in enumerate(_CHIP_FLIPS):
            px, py = x ^ fx, y ^ fy
            for a in range(n):
                copies.append(pltpu.make_async_remote_copy(
                    src_ref=a_refs[a].at[2 * px + py], dst_ref=out_refs[a].at[k], send_sem=send_sems.at[a, k],
                    recv_sem=recv_sems.at[a, k], device_id=(px, py, cc), device_id_type=MESH))
                copies[-1].start()
        for cp in copies:
            cp.wait()

    return _comm_call(
        body, _chip_peers, [jax.ShapeDtypeStruct((3,) + p.shape[1:], p.dtype) for p in parts],
        [pltpu.SemaphoreType.DMA((n, 3)), pltpu.SemaphoreType.DMA((n, 3))], name, parts, collective_id)


def _pack_rows(w_oa, w_ob, w_o, w_down, w_fg2, conv_w, l):
    conv_bits = lax.bitcast_convert_type(conv_w[l].reshape(-1), BF16).reshape(1, -1)
    tail = jnp.concatenate([w_fg2[l].astype(BF16).reshape(1, D_MODEL),
                            jnp.pad(conv_bits, ((0, 0), (0, D_MODEL - conv_bits.shape[1])))], axis=0)
    tail = jnp.pad(tail, ((0, B_ROWS - B_FG2 - tail.shape[0]), (0, 0)))
    rows = jnp.concatenate([w_oa[l].astype(BF16), w_ob[l].astype(BF16), w_o[l].astype(BF16),
                            w_down[l].astype(BF16)], axis=0)
    return rows, tail


def _unpack_tail(tail):
    w_fg2 = tail[:, 0, :].reshape(N_DEV, RANK, KEY_W // N_DEV).transpose(1, 0, 2).reshape(RANK, KEY_W)
    conv_bits = tail[:, B_CONV - B_FG2, :2 * 3 * ROW_SHARD].reshape(N_DEV, 3 * ROW_SHARD, 2)
    conv_w = lax.bitcast_convert_type(conv_bits, F32).reshape(N_DEV, 3, ROW_SHARD)
    return jnp.pad(w_fg2, ((0, LANES - RANK), (0, 0))), conv_w.transpose(1, 0, 2).reshape(3, D_MODEL)


def _by_core_chip(t):
    return t.reshape((2, 2, 2) + t.shape[1:]).transpose((2, 0, 1) + tuple(range(3, t.ndim + 2))).reshape(
        (2, 4) + t.shape[1:])


def _grad_rows(g):
    fg2 = g["w_fg2"][:RANK].reshape(RANK, N_DEV, KEY_W // N_DEV).transpose(1, 0, 2).reshape(N_DEV, 1, D_MODEL)
    conv = g["conv_w"].astype(BF16).reshape(3, N_DEV, ROW_SHARD).transpose(1, 0, 2).reshape(N_DEV, 1, 3 * ROW_SHARD)
    tail = jnp.concatenate([fg2, jnp.pad(conv, ((0, 0), (0, 0), (0, D_MODEL - 3 * ROW_SHARD)))], axis=1)
    tail = jnp.pad(tail, ((0, 0), (0, B_ROWS - B_FG2 - 2), (0, 0)))
    parts = [g["w_oa"].reshape(N_DEV, ROW_SHARD, D_MODEL), g["w_ob"].reshape(N_DEV, ROW_SHARD, D_MODEL),
             g["w_o"].reshape(N_DEV, ROW_SHARD, D_MODEL), g["w_down"].reshape(N_DEV, FFN_SHARD, D_MODEL), tail]
    return _by_core_chip(jnp.concatenate(parts, axis=1))


def _ungrad_rows(gs):
    return dict(w_oa=gs[B_OA:B_OA + ROW_SHARD], w_ob=gs[B_OB:B_OB + ROW_SHARD], w_o=gs[B_O:B_O + ROW_SHARD],
                w_ffn_down=gs[B_DOWN:B_DOWN + FFN_SHARD], w_fg2=gs[B_FG2].reshape(RANK, KEY_W // N_DEV),
                conv_w=gs[B_CONV, :3 * ROW_SHARD].reshape(3, ROW_SHARD))


def _layer_fwd(x, h, p, l, next_norm_g=None):
    tag = f"l{l}_"
    if h is None:
        h = _rmsnorm_fwd(x, p["norm1_g"], tag + "norm1")
    proj = _matmul(h, p["w_in"], "nn", BF16, tag + "proj")
    o, states, oa = _gla_fwd(proj, p["w_fg2"], p["b_fg"], p["gla_norm_g"], tag + "gla_fwd")
    p.update(p.pop("rest")((o,)))
    ya = _matmul(oa, p["w_oa"], "nn", BF16, tag + "ya")
    cb = _conv_fwd(proj, p["conv_w"], p["conv_b"], tag + "conv")
    yb = _matmul(cb, p["w_ob"], "nn", BF16, tag + "yb")
    mix, x1, h2 = _mix_project(proj, ya, yb, p["w_o"], x, p["norm2_g"], tag + "x1")
    gu = _matmul(h2, p["w_gu"], "nn", BF16, tag + "gu")
    hid = _swiglu_fwd(gu, tag + "swiglu")
    x2 = _matmul(hid, p["w_down"], "nn", F32, tag + "x2", residual=x1, norm_g=next_norm_g)
    x2, h_next = x2 if next_norm_g is not None else (x2, None)
    saved = dict(x=x, h=h, proj=proj, o=o, states=states, oa=oa, ya=ya, cb=cb, yb=yb, mix=mix, x1=x1, h2=h2,
                 gu=gu, hid=hid)
    return x2, h_next, saved


def _layer_bwd(dx2, dx2h, p, sv, l, reduce=None):
    if reduce is None:
        reduce = lambda group, grads: ((), ())
    tag = f"l{l}_b_"
    dhid = _matmul(dx2h, p["w_down"], "nt", BF16, tag + "dhid")
    d_down = _matmul(sv["hid"], dx2h, "tn", BF16, tag + "dw_down")
    dgu = _swiglu_bwd(dhid, sv["gu"], tag + "swiglu")
    dx1, dx1h, dg2 = _matmul_norm_bwd(dgu, p["w_gu"], sv["x1"], p["norm2_g"], dx2, tag + "dh2")
    d_gu = _matmul(sv["h2"], dgu, "tn", BF16, tag + "dw_gu")
    gu_packed, gu_sums = reduce("gu", d_gu)
    dmix = _matmul(dx1h, p["w_o"], "nt", BF16, tag + "dmix", after=gu_packed)
    d_o = _matmul(sv["mix"], dx1h, "tn", BF16, tag + "dw_o")
    dproj, dya, dyb = _mix_bwd(dmix, sv["proj"], sv["ya"], sv["yb"], tag + "mix")
    dcb = _matmul(dyb, p["w_ob"], "nt", BF16, tag + "dcb", after=gu_sums)
    d_ob = _matmul(sv["cb"], dyb, "tn", BF16, tag + "dw_ob")
    dproj, dwb = _conv_bwd(dcb, sv["proj"], p["conv_w"], p["conv_b"], dproj, tag + "conv")
    doa = _matmul(dya, p["w_oa"], "nt", BF16, tag + "doa")
    d_oa = _matmul(sv["oa"], dya, "tn", BF16, tag + "dw_oa")
    dq, dk, dv, dfg, dbfg, dproj, dgg = _gla_bwd(sv["proj"], p["w_fg2"], p["b_fg"], sv["states"], doa, sv["o"],
                                                 p["gla_norm_g"], dproj, tag + "gla")
    dproj = _place_qkv(dq, dk, dv, dproj, tag + "place_qkv")
    dproj = _matmul(dfg, p["w_fg2"], "nt", BF16, tag + "dfz", into=(dproj, FZ_COL))
    d_fg2 = _matmul(sv["proj"], dfg, "tn", BF16, tag + "dw_fg2", a_cols=(FZ_COL, LANES))
    rows = dict(w_fg2=d_fg2, conv_w=dwb[0:3], w_oa=d_oa, w_ob=d_ob, w_o=d_o, w_down=d_down)
    rows_packed, rows_sums = reduce("rows", rows)
    d_in = _matmul(sv["h"], dproj, "tn", BF16, tag + "dw_in", after=rows_packed)
    _, in_sums = reduce("in", d_in)
    dx, dxh, dg1 = _matmul_norm_bwd(dproj, p["w_in"], sv["x"], p["norm1_g"], dx1, tag + "dh",
                                    after=(d_in,) + tuple(rows_sums) + tuple(in_sums))
    big = dict(w_in=d_in, w_gu=d_gu, **rows)
    pad = lambda t: jnp.pad(t, ((0, 0), (0, D_MODEL - t.shape[1])))
    small = [dg1[0:1], pad(dbfg[0:1]), pad(dgg[0:1]), dwb[3:4], dg2[0:1]]
    return dx, dxh, big, small


def _local_step(x, target, weights_of, final_g, reduce_of=None):
    saved, layers, h = [], [], None
    for l in range(DEPTH):
        layers.append(weights_of(l, x))
        x, h, sv = _layer_fwd(x, h, layers[l], l, layers[l].get("next_norm1_g"))
        saved.append(sv)
    loss, dx, dxh, dgf = _loss_head(x, final_g, target, "loss_head")
    bigs, smalls = [None] * DEPTH, [None] * DEPTH
    for l in reversed(range(DEPTH)):
        dx, dxh, bigs[l], smalls[l] = _layer_bwd(dx, dxh, layers[l], saved[l], l, reduce_of(l) if reduce_of else None)
    loss_row = jnp.pad(loss[0:1], ((0, 0), (0, D_MODEL - loss.shape[1])))
    small = jnp.concatenate(smalls[0] + smalls[1] + [dgf[0:1], loss_row], axis=0)
    small = jnp.pad(small, ((0, SMALL_ROWS - small.shape[0]), (0, 0)))
    return loss[0, 0], dx, bigs, small


def kernel(x, norm1_g, w_in, w_fg2, b_fg, gla_norm_g, w_oa, conv_w, conv_b, w_ob, w_o, norm2_g, w_ffn_gate, w_ffn_up, w_ffn_down, final_g, loss_target, m_norm1_g, m_w_in, m_w_fg2, m_b_fg, m_gla_norm_g, m_w_oa, m_conv_w, m_conv_b, m_w_ob, m_w_o, m_norm2_g, m_w_ffn_gate, m_w_ffn_up, m_w_ffn_down, m_final_g, v_norm1_g, v_w_in, v_w_fg2, v_b_fg, v_gla_norm_g, v_w_oa, v_conv_w, v_conv_b, v_w_ob, v_w_o, v_norm2_g, v_w_ffn_gate, v_w_ffn_up, v_w_ffn_down, v_final_g):
    names = ["norm1_g", "w_in", "w_fg2", "b_fg", "gla_norm_g", "w_oa", "conv_w", "conv_b", "w_ob", "w_o",
             "norm2_g", "w_ffn_gate", "w_ffn_up", "w_ffn_down", "final_g"]
    w = dict(zip(names, [norm1_g, w_in, w_fg2, b_fg, gla_norm_g, w_oa, conv_w, conv_b, w_ob, w_o, norm2_g,
                         w_ffn_gate, w_ffn_up, w_ffn_down, final_g]))
    m = dict(zip(names, [m_norm1_g, m_w_in, m_w_fg2, m_b_fg, m_gla_norm_g, m_w_oa, m_conv_w, m_conv_b, m_w_ob,
                         m_w_o, m_norm2_g, m_w_ffn_gate, m_w_ffn_up, m_w_ffn_down, m_final_g]))
    v = dict(zip(names, [v_norm1_g, v_w_in, v_w_fg2, v_b_fg, v_gla_norm_g, v_w_oa, v_conv_w, v_conv_b, v_w_ob,
                         v_w_o, v_norm2_g, v_w_ffn_gate, v_w_ffn_up, v_w_ffn_down, v_final_g]))
    col_names = ["w_in", "w_ffn_gate", "w_ffn_up"]
    cx, cy, cc = _me()
    prm = jnp.asarray(SHIFT_TABLE)[4 * cx + 2 * cy + cc]
    core = jnp.reshape(cc, (1,)).astype(jnp.int32)
    chip = jnp.reshape(2 * cx + cy, (1,)).astype(jnp.int32)

    ids = iter(range(32))

    gathered, previous = [], None
    for l in range(DEPTH):
        rows, tail = _pack_rows(w_oa, w_ob, w_o, w_ffn_down, w_fg2, conv_w, l)
        win_in, win_gu = _window_cols(prm, w_in, w_ffn_gate, w_ffn_up, l, f"l{l}_windows", after=previous)
        previous = rows
        first = _all_gather([win_in, tail], f"l{l}_gather_in", next(ids))
        gathered.append(list(first) + list(_all_gather([win_gu, rows], f"l{l}_gather_rest", next(ids))))

    def weights_of(l, x_in):
        all_in, all_tail, all_gu, all_rows = gathered[l]
        after = (x_in,) if l > 0 else ()
        w_in_full, tail = _assemble_in(all_in, all_tail, f"l{l}_assemble_in", after)
        w_fg2_full, conv_w_full = _unpack_tail(tail)

        def rest(after_rest):
            names_rest = ("w_gu", "w_oa", "w_ob", "w_o", "w_down")
            return dict(zip(names_rest, _assemble_rest(all_gu, all_rows, f"l{l}_assemble_rest", after + after_rest)))

        return dict(w_in=w_in_full, rest=rest, w_fg2=w_fg2_full,
                    conv_w=conv_w_full, norm1_g=norm1_g[l][None], b_fg=b_fg[l][None],
                    gla_norm_g=gla_norm_g[l][None], conv_b=conv_b[l][None], norm2_g=norm2_g[l][None],
                    next_norm1_g=norm1_g[l + 1][None] if l + 1 < DEPTH else None)

    pending = [dict() for _ in range(DEPTH)]
    landed = []

    def reduce_of(l):
        def reduce(group, grads):
            tag = f"l{l}_{group}"
            if group == "gu":
                packed = _grad_windows_gu(grads, tag + "_windows")
            elif group == "in":
                packed = _grad_windows_in(grads, tag + "_windows")
            else:
                packed = _grad_rows(grads)
            (from_sibling,) = _send_to_sibling([packed], tag + "_to_sibling", next(ids))
            waited = () if group == "in" else tuple(landed)
            if waited:
                landed.clear()
            sums = _pair_sum(packed, from_sibling, core, tag + "_pair_sum", after=waited)
            (from_chips,) = _send_to_chips([sums], tag + "_to_chips", next(ids))
            landed.append(from_chips)
            pending[l][group] = (sums, from_chips)
            return (packed,), (sums,)
        return reduce

    loss, dx, bigs, small = _local_step(x[0], loss_target[0], weights_of, final_g[None], reduce_of)

    grads, deltas, new_m, new_v = {}, {}, {}, {}
    for kind, group_names in (("gu", col_names[1:]), ("in", col_names[:1])):
        params = [t for n in group_names for t in (w[n], m[n], v[n])]
        out = None
        for l in reversed(range(DEPTH)):
            sums, from_chips = pending[l][kind]
            out = _final_windows(prm, chip, sums, from_chips, params, l, kind, f"l{l}_{kind}_final", out)
        for n, (g, d, nm, nv) in zip(group_names, out):
            grads[n], deltas[n], new_m[n], new_v[n] = g, d, nm, nv
    row_grads = [_ungrad_rows(_chip_sum(*pending[l]["rows"], chip, f"l{l}_rows_chip_sum")) for l in range(DEPTH)]
    for n in row_grads[0]:
        grads[n] = jnp.stack([row_grads[l][n] for l in range(DEPTH)])

    small_sum = _sum_devices(_all_gather([small], "gather_small")[0], "sum_small")
    r512, r256 = slice(0, KEY_W), slice(0, HEAD_V)
    grads.update(
        norm1_g=jnp.stack([small_sum[0], small_sum[5]]), b_fg=jnp.stack([small_sum[1, r512], small_sum[6, r512]]),
        gla_norm_g=jnp.stack([small_sum[2, r256], small_sum[7, r256]]),
        conv_b=jnp.stack([small_sum[3], small_sum[8]]), norm2_g=jnp.stack([small_sum[4], small_sum[9]]),
        final_g=small_sum[10])

    for n in names:
        if n not in col_names:
            deltas[n], new_m[n], new_v[n] = _adamw(w[n], grads[n], m[n], v[n], "adamw_" + n)

    total_loss = small_sum[2 * 5 + 1, 0]
    return (total_loss, dx[None], *[grads[n] for n in names], *[deltas[n] for n in names],
            *[new_m[n] for n in names], *[new_v[n] for n in names])
```

```python
import functools

import jax
import jax.numpy as jnp
import numpy as np
from jax import lax
from jax.experimental import pallas as pl
from jax.experimental.pallas import tpu as pltpu
from jax.experimental.pallas import tpu_sc as plsc

F32 = jnp.float32
BF16 = jnp.bfloat16
MESH = pl.DeviceIdType.MESH

D_MODEL = 1024
DEPTH = 2
CHUNK = 64
HEADS = 4
HEAD_K = 128
HEAD_V = 256
KEY_W = HEADS * HEAD_K
VAL_W = HEADS * HEAD_V
RANK = 16
TAU = 16.0
FFN = 2816
IN_WIDTH = 2 * KEY_W + 2 * VAL_W + RANK + 5 * D_MODEL
EPS = 1e-6
Q_SCALE = HEAD_K ** -0.5
N_DEV = 8
ADAM_LR, ADAM_B1, ADAM_B2, ADAM_EPS, ADAM_WD, ADAM_STEP = 0.001, 0.9, 0.999, 1e-08, 0.01, 10

LANES = 128
SUBLANES_BF16 = 16
VMEM_LIMIT = 48 * 1024 * 1024
VMEM_LIMIT_WIDE = 56 * 1024 * 1024

FZ_COL = 2 * KEY_W + 2 * VAL_W + 5 * D_MODEL
PROJ_W = FZ_COL + LANES
SEG_R, SEG_GBI, SEG_GCI, SEG_CX, SEG_GA, SEG_GB = 2, 3, 4, 5, 6, 7

W_IN_SHARD = IN_WIDTH // N_DEV
FFN_SHARD = FFN // N_DEV
ROW_SHARD = D_MODEL // N_DEV

WIN_IN = 9 * LANES
WIN_GU = 4 * LANES
A_FZ = WIN_IN
IN_COLS = A_FZ + LANES
GU_COLS = 2 * WIN_GU
ORIG_FZ = 2 * KEY_W + 2 * VAL_W


def _new_col(o):
    if o < ORIG_FZ:
        return o
    if o < ORIG_FZ + RANK:
        return FZ_COL + (o - ORIG_FZ)
    return o - RANK


def _shift_table():
    t_in, rows = [], []
    for j in range(N_DEV):
        new = [_new_col(W_IN_SHARD * j + i) for i in range(W_IN_SHARD)]
        main = [i for i in range(W_IN_SHARD) if new[i] < FZ_COL]
        code = [i for i in range(W_IN_SHARD) if new[i] >= FZ_COL]
        shift = new[main[0]] - main[0]
        t_in.append(shift // LANES)
        assert all(new[i] - i == shift for i in main) and shift % LANES + W_IN_SHARD <= WIN_IN
        if code:
            cshift = new[code[0]] - FZ_COL - code[0]
            crow = [cshift % LANES, code[0], code[-1] + 1, int(cshift < 0)]
        else:
            crow = [0, 0, 0, 0]
        rows.append([shift % LANES, main[0], main[-1] + 1] + crow + [FFN_SHARD * j % LANES])
    return tuple(t_in), np.asarray(rows, np.int32)


T_IN, SHIFT_TABLE = _shift_table()
T_GU = tuple(FFN_SHARD * j // LANES for j in range(N_DEV))

B_OA, B_OB, B_O, B_DOWN = 0, ROW_SHARD, 2 * ROW_SHARD, 3 * ROW_SHARD
B_FG2 = B_DOWN + FFN_SHARD
B_CONV = B_FG2 + 1
B_ROWS = B_FG2 + SUBLANES_BF16
SMALL_ROWS = 32


def _pick(n, candidates):
    for c in candidates:
        if n % c == 0:
            return c
    return n


def _cparams(sem):
    return pltpu.CompilerParams(dimension_semantics=sem, vmem_limit_bytes=VMEM_LIMIT)


def _sigmoid(x):
    return 1.0 / (1.0 + jnp.exp(-x))


def _matmul(a, b, dims, out_dtype, name, after=(), into=None, a_cols=None):
    if dims == "nn":
        (m, k), (k2, n) = a.shape, b.shape
    elif dims == "nt":
        (m, k), (n, k2) = a.shape, b.shape
    else:
        (k, m), (k2, n) = a.shape, b.shape
        m = m if a_cols is None else a_cols[1]
    assert k == k2 and (a_cols is None or dims == "tn"), (a.shape, b.shape, dims)
    tm = _pick(m, (1024, 1408, 512, 256, 128))
    tn = _pick(n, (1664, 1408, 1024, 512, 256, 128))
    tk = _pick(k, (1664, 1408, 1024, 512, 256, 128))
    nk = k // tk
    if dims == "nn":
        a_spec = pl.BlockSpec((tm, tk), lambda i, j, kk: (i, kk))
        b_spec = pl.BlockSpec((tk, tn), lambda i, j, kk: (kk, j))
        contract = (((1,), (0,)), ((), ()))
    elif dims == "nt":
        a_spec = pl.BlockSpec((tm, tk), lambda i, j, kk: (i, kk))
        b_spec = pl.BlockSpec((tn, tk), lambda i, j, kk: (j, kk))
        contract = (((1,), (1,)), ((), ()))
    else:
        first = 0 if a_cols is None else a_cols[0] // tm
        assert a_cols is None or (tm == m and a_cols[0] % tm == 0)
        a_spec = pl.BlockSpec((tk, tm), lambda i, j, kk: (kk, first + i))
        b_spec = pl.BlockSpec((tk, tn), lambda i, j, kk: (kk, j))
        contract = (((0,), (0,)), ((), ()))
    o_spec = pl.BlockSpec((tm, tn), lambda i, j, kk: (i, j))
    out_spec, out_struct, placed, aliases = o_spec, jax.ShapeDtypeStruct((m, n), out_dtype), (), {}
    if into is not None:
        buffer, col = into
        assert col % tn == 0 and buffer.dtype == out_dtype
        out_spec = pl.BlockSpec((tm, tn), lambda i, j, kk: (i, col // tn + j))
        out_struct, placed, aliases = jax.ShapeDtypeStruct(buffer.shape, out_dtype), (buffer,), {2 + len(after): 0}

    def body(*refs):
        a_ref, b_ref = refs[:2]
        o_ref = refs[2 + len(after) + len(placed)]
        kk = pl.program_id(2)
        part = lax.dot_general(a_ref[...], b_ref[...], contract, preferred_element_type=F32)

        def finish(total):
            o_ref[...] = total.astype(o_ref.dtype)

        if nk == 1:
            finish(part)
            return
        acc_ref = refs[-1]

        @pl.when(kk == 0)
        def _():
            acc_ref[...] = part

        @pl.when((kk > 0) & (kk < nk - 1))
        def _():
            acc_ref[...] += part

        @pl.when(kk == nk - 1)
        def _():
            finish(acc_ref[...] + part)

    in_specs = [a_spec, b_spec] + [_after_spec(t) for t in after] + [_ANY] * len(placed)
    args = (a, b) + tuple(after) + placed
    return pl.pallas_call(
        body, grid=(m // tm, n // tn, nk), in_specs=in_specs, out_specs=out_spec,
        out_shape=out_struct, input_output_aliases=aliases,
        scratch_shapes=[pltpu.VMEM((tm, tn), F32)] if nk > 1 else [], name=name,
        compiler_params=_cparams(("parallel", "parallel", "arbitrary")),
    )(*args)


def _rmsnorm_fwd(x, g, name):
    s, d = x.shape
    tm = _pick(s, (512, 256))

    def body(x_ref, g_ref, o_ref):
        xv = x_ref[...]
        r = lax.rsqrt(jnp.mean(xv * xv, axis=-1, keepdims=True) + EPS)
        o_ref[...] = (xv * r * g_ref[...]).astype(o_ref.dtype)

    row = pl.BlockSpec((tm, d), lambda i: (i, 0))
    return pl.pallas_call(
        body, grid=(s // tm,), in_specs=[row, pl.BlockSpec((1, d), lambda i: (0, 0))], out_specs=row,
        out_shape=jax.ShapeDtypeStruct((s, d), BF16), name=name, compiler_params=_cparams(("parallel",)),
    )(x, g)


def _matmul_norm_bwd(a, b, x, g, dres, name, after=()):
    (s, k), (d, k2) = a.shape, b.shape
    assert k == k2 and x.shape == (s, d)
    tm = _pick(s, (1024, 512, 256))
    tk = _pick(k, (1664, 1408, 1024, 512, 256, 128))
    nk = k // tk

    def body(a_ref, b_ref, x_ref, g_ref, dres_ref, *rest):
        dx_ref, dx16_ref, dg_ref, acc_ref = rest[len(after):]
        i, kk = pl.program_id(0), pl.program_id(1)
        part = lax.dot_general(a_ref[...], b_ref[...], _NT, preferred_element_type=F32)

        def finish(dh):
            xv = x_ref[...]
            r = lax.rsqrt(jnp.mean(xv * xv, axis=-1, keepdims=True) + EPS)
            xn = xv * r
            dxn = dh * g_ref[...]
            dx = dres_ref[...] + r * (dxn - xn * jnp.mean(dxn * xn, axis=-1, keepdims=True))
            dx_ref[...] = dx
            dx16_ref[...] = dx.astype(BF16)
            dg = jnp.broadcast_to(jnp.sum(dh * xn, axis=0, keepdims=True), dg_ref.shape)

            @pl.when(i == 0)
            def _():
                dg_ref[...] = dg

            @pl.when(i > 0)
            def _():
                dg_ref[...] += dg

        @pl.when(kk == 0)
        def _():
            acc_ref[...] = part

        @pl.when((kk > 0) & (kk < nk - 1))
        def _():
            acc_ref[...] += part

        @pl.when(kk == nk - 1)
        def _():
            finish(acc_ref[...] + part)

    assert nk > 1
    row = pl.BlockSpec((tm, d), lambda i, kk: (i, 0))
    return pl.pallas_call(
        body, grid=(s // tm, nk),
        in_specs=[pl.BlockSpec((tm, tk), lambda i, kk: (i, kk)), pl.BlockSpec((d, tk), lambda i, kk: (0, kk)), row,
                  pl.BlockSpec((1, d), lambda i, kk: (0, 0)), row] + [_after_spec(t) for t in after],
        out_specs=[row, row, pl.BlockSpec((8, d), lambda i, kk: (0, 0))],
        out_shape=[jax.ShapeDtypeStruct((s, d), F32), jax.ShapeDtypeStruct((s, d), BF16),
                   jax.ShapeDtypeStruct((8, d), F32)],
        scratch_shapes=[pltpu.VMEM((tm, d), F32)], name=name,
        compiler_params=pltpu.CompilerParams(dimension_semantics=("arbitrary", "arbitrary"),
                                             vmem_limit_bytes=VMEM_LIMIT_WIDE),
    )(a, b, x, g, dres, *after)


def _loss_head(x, g, target, name):
    s, d = x.shape
    tm = _pick(s, (512, 256))

    def body(x_ref, g_ref, t_ref, loss_ref, dx_ref, dx16_ref, dg_ref):
        xv = x_ref[...]
        gv = g_ref[...]
        r = lax.rsqrt(jnp.mean(xv * xv, axis=-1, keepdims=True) + EPS)
        xn = xv * r
        err = xn * gv - t_ref[...]
        dy = err * (1.0 / d)
        dxn = dy * gv
        dx = r * (dxn - xn * jnp.mean(dxn * xn, axis=-1, keepdims=True))
        dx_ref[...] = dx
        dx16_ref[...] = dx.astype(BF16)

        @pl.when(pl.program_id(0) == 0)
        def _():
            dg_ref[...] = jnp.zeros_like(dg_ref)
            loss_ref[...] = jnp.zeros_like(loss_ref)

        dg_ref[...] += jnp.broadcast_to(jnp.sum(dy * xn, axis=0, keepdims=True), dg_ref.shape)
        row_loss = jnp.sum(err * err, axis=-1, keepdims=True)
        loss_ref[...] += jnp.broadcast_to((0.5 / d) * jnp.sum(row_loss, axis=0, keepdims=True), loss_ref.shape)

    row = pl.BlockSpec((tm, d), lambda i: (i, 0))
    return pl.pallas_call(
        body, grid=(s // tm,), in_specs=[row, pl.BlockSpec((1, d), lambda i: (0, 0)), row],
        out_specs=[pl.BlockSpec((8, LANES), lambda i: (0, 0)), row, row, pl.BlockSpec((8, d), lambda i: (0, 0))],
        out_shape=[jax.ShapeDtypeStruct((8, LANES), F32), jax.ShapeDtypeStruct((s, d), F32),
                   jax.ShapeDtypeStruct((s, d), BF16), jax.ShapeDtypeStruct((8, d), F32)],
        name=name, compiler_params=_cparams(("arbitrary",)),
    )(x, g, target)


def _chunk_cumsum(x):
    pos = lax.broadcasted_iota(jnp.int32, (x.shape[0], 1), 0) % CHUNK
    shift = 1
    while shift < CHUNK:
        x = x + jnp.where(pos >= shift, pltpu.roll(x, shift, 0), 0.0)
        shift *= 2
    return x


def _block_decay(fz, w, b):
    fg = jnp.dot(fz, w, preferred_element_type=F32) + b
    la = (jnp.minimum(fg, 0.0) - jnp.log(1.0 + jnp.exp(-jnp.abs(fg)))) * (1.0 / TAU)
    cum = _chunk_cumsum(la)
    ends = [cum[i + CHUNK - 1:i + CHUNK, :] for i in range(0, fz.shape[0], CHUNK)]
    end = jnp.concatenate([jnp.broadcast_to(e, (CHUNK, e.shape[1])) for e in ends], axis=0)
    return fg, jnp.exp(end - cum), [jnp.exp(e) for e in ends]


_TN = (((0,), (0,)), ((), ()))
_NT = (((1,), (1,)), ((), ()))


def _gla_specs(rows):
    q_spec = pl.BlockSpec((rows, HEAD_K), lambda h, c: (c, h))
    k_spec = pl.BlockSpec((rows, HEAD_K), lambda h, c: (c, HEADS + h))
    v_spec = pl.BlockSpec((rows, HEAD_V), lambda h, c: (c, HEADS + h))
    fz_spec = pl.BlockSpec((rows, LANES), lambda h, c: (c, FZ_COL // LANES))
    w_spec = pl.BlockSpec((LANES, HEAD_K), lambda h, c: (0, h))
    b_spec = pl.BlockSpec((1, HEAD_K), lambda h, c: (0, h))
    return q_spec, k_spec, v_spec, fz_spec, w_spec, b_spec


def _gla_fwd(proj, wfg, bfg, gla_g, name):
    s = proj.shape[0]
    nc = s // CHUNK
    per = _pick(nc, (8, 4, 2, 1))
    rows = per * CHUNK

    def body(q_ref, k_ref, v_ref, fz_ref, w_ref, b_ref, r_ref, g_ref, o_ref, st_ref, oa_ref, state, update):
        @pl.when(pl.program_id(1) == 0)
        def _():
            state[...] = jnp.zeros_like(state)

        _, dec, gammas = _block_decay(fz_ref[...], w_ref[...], b_ref[...])
        kd = (k_ref[...].astype(F32) * dec).astype(BF16)
        qs = (q_ref[...].astype(F32) * Q_SCALE).astype(BF16)
        for i in range(per):
            sl = slice(i * CHUNK, (i + 1) * CHUNK)
            update[i] = lax.dot_general(v_ref[sl, :], kd[sl], _TN, preferred_element_type=F32)
        st = state[...]
        for i in range(per):
            st = st * gammas[i] + update[i]
            st_ref[0, i] = st.astype(BF16)
        state[...] = st
        for i in range(per):
            sl = slice(i * CHUNK, (i + 1) * CHUNK)
            o_ref[sl, :] = lax.dot_general(qs[sl], st_ref[0, i], _NT, preferred_element_type=F32).astype(o_ref.dtype)
        ov = o_ref[...].astype(F32)
        rstd = lax.rsqrt(jnp.mean(ov * ov, axis=-1, keepdims=True) + EPS)
        rv = r_ref[...].astype(F32)
        oa_ref[...] = (ov * rstd * g_ref[...] * (rv * _sigmoid(rv))).astype(oa_ref.dtype)

    q_spec, k_spec, v_spec, fz_spec, w_spec, b_spec = _gla_specs(rows)
    head_v = pl.BlockSpec((rows, HEAD_V), lambda h, c: (c, h))
    return pl.pallas_call(
        body, grid=(HEADS, nc // per),
        in_specs=[q_spec, k_spec, v_spec, fz_spec, w_spec, b_spec,
                  pl.BlockSpec((rows, HEAD_V), lambda h, c: (c, SEG_R * (D_MODEL // HEAD_V) + h)),
                  pl.BlockSpec((1, HEAD_V), lambda h, c: (0, 0))],
        out_specs=[head_v, pl.BlockSpec((1, per, HEAD_V, HEAD_K), lambda h, c: (h, c, 0, 0)), head_v],
        out_shape=[jax.ShapeDtypeStruct((s, VAL_W), BF16),
                   jax.ShapeDtypeStruct((HEADS, nc, HEAD_V, HEAD_K), BF16), jax.ShapeDtypeStruct((s, VAL_W), BF16)],
        scratch_shapes=[pltpu.VMEM((HEAD_V, HEAD_K), F32), pltpu.VMEM((per, HEAD_V, HEAD_K), F32)], name=name,
        compiler_params=_cparams(("parallel", "arbitrary")),
    )(proj, proj, proj, proj, wfg, bfg, proj, gla_g)


def _gla_bwd(proj, wfg, bfg, states, doa, o, gla_g, dproj, name):
    s = proj.shape[0]
    nc = s // CHUNK
    per = _pick(nc, (8, 4, 2, 1))
    rows = per * CHUNK
    nblk = nc // per

    def rev(spec_fn):
        return lambda h, j: spec_fn(h, nblk - 1 - j)

    def body(q_ref, k_ref, v_ref, fz_ref, w_ref, b_ref, doa_ref, o_ref, r_ref, g_ref,
             st_ref, prev_ref, _, dq_ref, dk_ref, dv_ref, dfg_ref, db_ref, dr_ref, dg_ref,
             carry, gt_all, dkd_all, dgg_all):
        j = pl.program_id(1)

        @pl.when(j == 0)
        def _():
            carry[...] = jnp.zeros_like(carry)
            db_ref[...] = jnp.zeros_like(db_ref)

        @pl.when((j == 0) & (pl.program_id(0) == 0))
        def _():
            dg_ref[...] = jnp.zeros_like(dg_ref)

        ov = o_ref[...].astype(F32)
        rstd = lax.rsqrt(jnp.mean(ov * ov, axis=-1, keepdims=True) + EPS)
        ohat = ov * rstd
        rv = r_ref[...].astype(F32)
        sg = _sigmoid(rv)
        doav = doa_ref[...].astype(F32)
        gv = g_ref[...]
        dr_ref[...] = (doav * ohat * gv * (sg * (1.0 + rv * (1.0 - sg)))).astype(dr_ref.dtype)
        don = doav * (rv * sg)
        dg_ref[...] += jnp.broadcast_to(jnp.sum(don * ohat, axis=0, keepdims=True), dg_ref.shape)
        dohat = don * gv
        do = (rstd * (dohat - ohat * jnp.mean(dohat * ohat, axis=-1, keepdims=True))).astype(BF16)

        fg, dec, gammas = _block_decay(fz_ref[...], w_ref[...], b_ref[...])
        kd = k_ref[...].astype(F32) * dec
        kd16 = kd.astype(BF16)
        qs = (q_ref[...].astype(F32) * Q_SCALE).astype(BF16)
        for i in range(per):
            sl = slice(i * CHUNK, (i + 1) * CHUNK)
            gt_all[i] = lax.dot_general(do[sl], qs[sl], _TN, preferred_element_type=F32)
        back = carry[...]
        for i in reversed(range(per)):
            gt = back + gt_all[i]
            gt_all[i] = gt
            back = gt * gammas[i]
        carry[...] = back
        has_prev = (j < nblk - 1).astype(F32)
        for i in range(per):
            sl = slice(i * CHUNK, (i + 1) * CHUNK)
            gt = gt_all[i]
            gt16 = gt.astype(BF16)
            dq_ref[sl, :] = (jnp.dot(do[sl], st_ref[0, i], preferred_element_type=F32) * Q_SCALE).astype(dq_ref.dtype)
            dkd_all[sl, :] = jnp.dot(v_ref[sl, :], gt16, preferred_element_type=F32)
            dv_ref[sl, :] = lax.dot_general(kd16[sl], gt16, _NT, preferred_element_type=F32).astype(dv_ref.dtype)
            if i > 0:
                st_prev = st_ref[0, i - 1].astype(F32)
            else:
                st_prev = prev_ref[0, 0].astype(F32) * has_prev
            dgamma = jnp.sum(gt * st_prev, axis=0, keepdims=True)
            dgg_all[sl, :] = jnp.broadcast_to(dgamma * gammas[i], (CHUNK, HEAD_K))
        dkd = dkd_all[...]
        dk_ref[...] = (dkd * dec).astype(dk_ref.dtype)
        e = dkd * kd
        dla = dgg_all[...] + (_chunk_cumsum(e) - e)
        dfg = dla * (1.0 / TAU) * _sigmoid(-fg)
        dfg_ref[...] = dfg.astype(dfg_ref.dtype)
        db_ref[...] += jnp.broadcast_to(jnp.sum(dfg, axis=0, keepdims=True), db_ref.shape)

    q_spec, k_spec, v_spec, fz_spec, w_spec, b_spec = _gla_specs(rows)
    q_spec, k_spec, v_spec, fz_spec = [
        pl.BlockSpec(sp.block_shape, rev(sp.index_map)) for sp in (q_spec, k_spec, v_spec, fz_spec)]
    do_spec = pl.BlockSpec((rows, HEAD_V), lambda h, j: (nblk - 1 - j, h))
    st_spec = pl.BlockSpec((1, per, HEAD_V, HEAD_K), lambda h, j: (h, nblk - 1 - j, 0, 0))
    prev_spec = pl.BlockSpec((1, 1, HEAD_V, HEAD_K),
                             lambda h, j: (h, jnp.maximum((nblk - 1 - j) * per - 1, 0), 0, 0))
    key_out = pl.BlockSpec((rows, HEAD_K), lambda h, j: (nblk - 1 - j, h))
    r_spec = pl.BlockSpec((rows, HEAD_V), lambda h, j: (nblk - 1 - j, SEG_R * (D_MODEL // HEAD_V) + h))
    return pl.pallas_call(
        body, grid=(HEADS, nblk),
        in_specs=[q_spec, k_spec, v_spec, fz_spec, w_spec, b_spec,
                  do_spec, do_spec, r_spec, pl.BlockSpec((1, HEAD_V), lambda h, j: (0, 0)),
                  st_spec, prev_spec, _ANY],
        out_specs=[key_out, key_out, do_spec, key_out, pl.BlockSpec((8, HEAD_K), lambda h, j: (0, h)),
                   r_spec, pl.BlockSpec((8, HEAD_V), lambda h, j: (0, 0))],
        out_shape=[jax.ShapeDtypeStruct((s, KEY_W), BF16), jax.ShapeDtypeStruct((s, KEY_W), BF16),
                   jax.ShapeDtypeStruct((s, VAL_W), BF16), jax.ShapeDtypeStruct((s, KEY_W), BF16),
                   jax.ShapeDtypeStruct((8, KEY_W), F32), jax.ShapeDtypeStruct(dproj.shape, BF16),
                   jax.ShapeDtypeStruct((8, HEAD_V), F32)],
        input_output_aliases={12: 5},
        scratch_shapes=[pltpu.VMEM((HEAD_V, HEAD_K), F32), pltpu.VMEM((per, HEAD_V, HEAD_K), F32),
                        pltpu.VMEM((rows, HEAD_K), F32), pltpu.VMEM((rows, HEAD_K), F32)], name=name,
        compiler_params=_cparams(("arbitrary", "arbitrary")),
    )(proj, proj, proj, proj, wfg, bfg, doa, o, proj, gla_g, states, states, dproj)


def _place_qkv(dq, dk, dv, dproj, name):
    s = dq.shape[0]
    tm = _pick(s, (512, 256))

    def body(dq_ref, dk_ref, dv_ref, _, o_ref):
        o_ref[:, 0:KEY_W] = dq_ref[...]
        o_ref[:, KEY_W:2 * KEY_W] = dk_ref[...]
        o_ref[:, 2 * KEY_W:2 * KEY_W + VAL_W] = dv_ref[...]

    def rows(width):
        return pl.BlockSpec((tm, width), lambda i: (i, 0))

    return pl.pallas_call(
        body, grid=(s // tm,), in_specs=[rows(KEY_W), rows(KEY_W), rows(VAL_W), _ANY],
        out_specs=rows(2 * KEY_W + VAL_W), out_shape=jax.ShapeDtypeStruct(dproj.shape, BF16),
        input_output_aliases={3: 0}, name=name, compiler_params=_cparams(("parallel",)),
    )(dq, dk, dv, dproj)


def _seg(tm, seg):
    return pl.BlockSpec((tm, D_MODEL), lambda i: (i, seg))


HALO = SUBLANES_BF16


def _shift_down(u, p1, p2, n, rows):
    rolled = pltpu.roll(u, n, 0)
    if n == 1:
        return jnp.where(rows == 0, p1, rolled)
    return jnp.where(rows == 0, p2, jnp.where(rows == 1, p1, rolled))


def _shift_up(u, n1, n2, n, rows, tm):
    rolled = pltpu.roll(u, tm - n, 0)
    if n == 1:
        return jnp.where(rows == tm - 1, n1, rolled)
    return jnp.where(rows == tm - 2, n1, jnp.where(rows == tm - 1, n2, rolled))


def _conv_terms(gc_ref, cx_ref, gcp_ref, cxp_ref, tm):
    i = pl.program_id(0)
    u = gc_ref[...].astype(F32) * cx_ref[...].astype(F32)
    up = gcp_ref[...].astype(F32) * cxp_ref[...].astype(F32) * (i > 0).astype(F32)
    rows = lax.broadcasted_iota(jnp.int32, (tm, 1), 0)
    u1 = _shift_down(u, up[HALO - 1:HALO, :], up[HALO - 2:HALO - 1, :], 1, rows)
    u2 = _shift_down(u, up[HALO - 1:HALO, :], up[HALO - 2:HALO - 1, :], 2, rows)
    return u, u1, u2, rows


def _prev_halo(tm, seg):
    return pl.BlockSpec((HALO, D_MODEL), lambda i: (jnp.maximum(i * (tm // HALO) - 1, 0), seg))


def _conv_fwd(proj, w, b, name):
    s = proj.shape[0]
    tm = _pick(s, (512, 256))

    def body(gbi_ref, gc_ref, cx_ref, gcp_ref, cxp_ref, w_ref, b_ref, cb_ref):
        u, u1, u2, _ = _conv_terms(gc_ref, cx_ref, gcp_ref, cxp_ref, tm)
        conv = w_ref[0:1, :] * u2 + w_ref[1:2, :] * u1 + w_ref[2:3, :] * u + b_ref[...]
        cb_ref[...] = (gbi_ref[...].astype(F32) * conv).astype(cb_ref.dtype)

    return pl.pallas_call(
        body, grid=(s // tm,),
        in_specs=[_seg(tm, SEG_GBI), _seg(tm, SEG_GCI), _seg(tm, SEG_CX),
                  _prev_halo(tm, SEG_GCI), _prev_halo(tm, SEG_CX),
                  pl.BlockSpec((3, D_MODEL), lambda i: (0, 0)), pl.BlockSpec((1, D_MODEL), lambda i: (0, 0))],
        out_specs=pl.BlockSpec((tm, D_MODEL), lambda i: (i, 0)),
        out_shape=jax.ShapeDtypeStruct((s, D_MODEL), BF16), name=name, compiler_params=_cparams(("parallel",)),
    )(proj, proj, proj, proj, proj, w, b)


def _conv_bwd(dcb, proj, w, b, dproj, name):
    s = proj.shape[0]
    tm = _pick(s, (512, 256))
    nt = s // tm

    def body(dcb_ref, gbi_ref, gc_ref, cx_ref, gcp_ref, cxp_ref, dcbn_ref, gbin_ref, w_ref, b_ref, _,
             d3_ref, dwb_ref):
        i = pl.program_id(0)

        @pl.when(i == 0)
        def _():
            dwb_ref[...] = jnp.zeros_like(dwb_ref)

        u, u1, u2, rows = _conv_terms(gc_ref, cx_ref, gcp_ref, cxp_ref, tm)
        w0, w1, w2 = w_ref[0:1, :], w_ref[1:2, :], w_ref[2:3, :]
        conv = w0 * u2 + w1 * u1 + w2 * u + b_ref[...]
        dcbv = dcb_ref[...].astype(F32)
        gbi = gbi_ref[...].astype(F32)
        dconv = dcbv * gbi
        dnext = dcbn_ref[...].astype(F32) * gbin_ref[...].astype(F32) * (i < nt - 1).astype(F32)
        dc1 = _shift_up(dconv, dnext[0:1, :], dnext[1:2, :], 1, rows, tm)
        dc2 = _shift_up(dconv, dnext[0:1, :], dnext[1:2, :], 2, rows, tm)
        du = w2 * dconv + w1 * dc1 + w0 * dc2
        d3_ref[:, 0:D_MODEL] = (dcbv * conv).astype(d3_ref.dtype)
        d3_ref[:, D_MODEL:2 * D_MODEL] = (du * cx_ref[...].astype(F32)).astype(d3_ref.dtype)
        d3_ref[:, 2 * D_MODEL:3 * D_MODEL] = (du * gc_ref[...].astype(F32)).astype(d3_ref.dtype)
        dwb_ref[0:1, :] += jnp.sum(dconv * u2, axis=0, keepdims=True)
        dwb_ref[1:2, :] += jnp.sum(dconv * u1, axis=0, keepdims=True)
        dwb_ref[2:3, :] += jnp.sum(dconv * u, axis=0, keepdims=True)
        dwb_ref[3:4, :] += jnp.sum(dconv, axis=0, keepdims=True)

    def next_halo(seg_fn):
        return pl.BlockSpec((HALO, D_MODEL), lambda i: (jnp.minimum((i + 1) * (tm // HALO), s // HALO - 1), seg_fn))

    return pl.pallas_call(
        body, grid=(nt,),
        in_specs=[pl.BlockSpec((tm, D_MODEL), lambda i: (i, 0)),
                  _seg(tm, SEG_GBI), _seg(tm, SEG_GCI), _seg(tm, SEG_CX),
                  _prev_halo(tm, SEG_GCI), _prev_halo(tm, SEG_CX),
                  next_halo(0), next_halo(SEG_GBI),
                  pl.BlockSpec((3, D_MODEL), lambda i: (0, 0)), pl.BlockSpec((1, D_MODEL), lambda i: (0, 0)), _ANY],
        out_specs=[pl.BlockSpec((tm, 3 * D_MODEL), lambda i: (i, SEG_GBI // 3)),
                   pl.BlockSpec((8, D_MODEL), lambda i: (0, 0))],
        out_shape=[jax.ShapeDtypeStruct(dproj.shape, BF16), jax.ShapeDtypeStruct((8, D_MODEL), F32)],
        input_output_aliases={10: 0}, name=name, compiler_params=_cparams(("arbitrary",)),
    )(dcb, proj, proj, proj, proj, proj, dcb, proj, w, b, dproj)


def _mix_project(proj, ya, yb, w_o, x, norm_g, name):
    s = proj.shape[0]
    tm = _pick(s, (512, 256))

    def body(ga_ref, gb_ref, ya_ref, yb_ref, w_ref, x_ref, g_ref, mix_ref, x1_ref, h2_ref):
        mix = (_sigmoid(ga_ref[...].astype(F32)) * ya_ref[...].astype(F32)
               + _sigmoid(gb_ref[...].astype(F32)) * yb_ref[...].astype(F32)).astype(BF16)
        mix_ref[...] = mix
        x1 = x_ref[...] + jnp.dot(mix, w_ref[...], preferred_element_type=F32)
        x1_ref[...] = x1
        rstd = lax.rsqrt(jnp.mean(x1 * x1, axis=-1, keepdims=True) + EPS)
        h2_ref[...] = (x1 * rstd * g_ref[...]).astype(BF16)

    row = pl.BlockSpec((tm, D_MODEL), lambda i: (i, 0))
    whole = pl.BlockSpec((D_MODEL, D_MODEL), lambda i: (0, 0))
    return pl.pallas_call(
        body, grid=(s // tm,),
        in_specs=[_seg(tm, SEG_GA), _seg(tm, SEG_GB), row, row, whole, row, pl.BlockSpec((1, D_MODEL), lambda i: (0, 0))],
        out_specs=[row, row, row],
        out_shape=[jax.ShapeDtypeStruct((s, D_MODEL), BF16), jax.ShapeDtypeStruct((s, D_MODEL), F32),
                   jax.ShapeDtypeStruct((s, D_MODEL), BF16)],
        name=name, compiler_params=_cparams(("parallel",)),
    )(proj, proj, ya, yb, w_o, x, norm_g)


def _mix_bwd(dmix, proj, ya, yb, name):
    s = proj.shape[0]
    tm = _pick(s, (512, 256))

    def body(dm_ref, ga_ref, gb_ref, ya_ref, yb_ref, dg_ref, dya_ref, dyb_ref):
        dm = dm_ref[...].astype(F32)
        sa = _sigmoid(ga_ref[...].astype(F32))
        sb = _sigmoid(gb_ref[...].astype(F32))
        dg_ref[:, 0:D_MODEL] = (dm * ya_ref[...].astype(F32) * sa * (1.0 - sa)).astype(dg_ref.dtype)
        dg_ref[:, D_MODEL:2 * D_MODEL] = (dm * yb_ref[...].astype(F32) * sb * (1.0 - sb)).astype(dg_ref.dtype)
        dya_ref[...] = (dm * sa).astype(dya_ref.dtype)
        dyb_ref[...] = (dm * sb).astype(dyb_ref.dtype)

    row = pl.BlockSpec((tm, D_MODEL), lambda i: (i, 0))
    return pl.pallas_call(
        body, grid=(s // tm,), in_specs=[row, _seg(tm, SEG_GA), _seg(tm, SEG_GB), row, row],
        out_specs=[pl.BlockSpec((tm, 2 * D_MODEL), lambda i: (i, SEG_GA // 2)), row, row],
        out_shape=[jax.ShapeDtypeStruct((s, PROJ_W), BF16), jax.ShapeDtypeStruct((s, D_MODEL), BF16),
                   jax.ShapeDtypeStruct((s, D_MODEL), BF16)],
        name=name, compiler_params=_cparams(("parallel",)),
    )(dmix, proj, proj, ya, yb)


def _swiglu_project(gu, w_down, x1, norm_g, name):
    s = gu.shape[0]
    tm = _pick(s, (512, 256))
    tk = _pick(FFN, (1408, 256))
    nk = FFN // tk
    has_norm = norm_g is not None

    def body(gate_ref, up_ref, w_ref, x_ref, *rest):
        outs = rest[has_norm:]
        hid_ref, x2_ref, acc_ref = outs[0], outs[1], rest[-1]
        kk = pl.program_id(1)
        gate = gate_ref[...].astype(F32)
        hid = (gate * _sigmoid(gate) * up_ref[...].astype(F32)).astype(BF16)
        hid_ref[...] = hid
        part = jnp.dot(hid, w_ref[...], preferred_element_type=F32)

        @pl.when(kk == 0)
        def _():
            acc_ref[...] = part

        @pl.when((kk > 0) & (kk < nk - 1))
        def _():
            acc_ref[...] += part

        @pl.when(kk == nk - 1)
        def _():
            x2 = acc_ref[...] + part + x_ref[...]
            x2_ref[...] = x2
            if has_norm:
                rstd = lax.rsqrt(jnp.mean(x2 * x2, axis=-1, keepdims=True) + EPS)
                outs[2][...] = (x2 * rstd * rest[0][...]).astype(BF16)

    assert nk > 1
    row = pl.BlockSpec((tm, D_MODEL), lambda i, kk: (i, 0))
    out = pl.pallas_call(
        body, grid=(s // tm, nk),
        in_specs=[pl.BlockSpec((tm, tk), lambda i, kk: (i, kk)), pl.BlockSpec((tm, tk), lambda i, kk: (i, nk + kk)),
                  pl.BlockSpec((tk, D_MODEL), lambda i, kk: (kk, 0)), row]
        + ([pl.BlockSpec((1, D_MODEL), lambda i, kk: (0, 0))] if has_norm else []),
        out_specs=[pl.BlockSpec((tm, tk), lambda i, kk: (i, kk)), row] + ([row] if has_norm else []),
        out_shape=[jax.ShapeDtypeStruct((s, FFN), BF16), jax.ShapeDtypeStruct((s, D_MODEL), F32)]
        + ([jax.ShapeDtypeStruct((s, D_MODEL), BF16)] if has_norm else []),
        scratch_shapes=[pltpu.VMEM((tm, D_MODEL), F32)], name=name,
        compiler_params=_cparams(("parallel", "arbitrary")),
    )(gu, gu, w_down, x1, *((norm_g,) if has_norm else ()))
    return out[0], out[1], (out[2] if has_norm else None)


def _swiglu_bwd(dhid, gu, name):
    s = gu.shape[0]
    tm = _pick(s, (256,))

    def body(dh_ref, gu_ref, o_ref):
        gate = gu_ref[:, 0:FFN].astype(F32)
        up = gu_ref[:, FFN:2 * FFN].astype(F32)
        dh = dh_ref[...].astype(F32)
        sg = _sigmoid(gate)
        o_ref[:, 0:FFN] = (dh * up * (sg * (1.0 + gate * (1.0 - sg)))).astype(o_ref.dtype)
        o_ref[:, FFN:2 * FFN] = (dh * gate * sg).astype(o_ref.dtype)

    wide = pl.BlockSpec((tm, 2 * FFN), lambda i: (i, 0))
    return pl.pallas_call(
        body, grid=(s // tm,), in_specs=[pl.BlockSpec((tm, FFN), lambda i: (i, 0)), wide], out_specs=wide,
        out_shape=jax.ShapeDtypeStruct((s, 2 * FFN), BF16), name=name, compiler_params=_cparams(("parallel",)),
    )(dhid, gu)


def _adamw_math(w, g, m, v):
    m2 = ADAM_B1 * m + (1.0 - ADAM_B1) * g
    v2 = ADAM_B2 * v + (1.0 - ADAM_B2) * (g * g)
    m_hat = m2 / (1.0 - ADAM_B1 ** ADAM_STEP)
    v_hat = v2 / (1.0 - ADAM_B2 ** ADAM_STEP)
    delta = -ADAM_LR * (m_hat / (jnp.sqrt(v_hat) + ADAM_EPS) + ADAM_WD * w)
    return delta, m2, v2


def _adamw(w, g, m, v, name):
    shape = w.shape
    cols = shape[-1]
    rows = int(np.prod(shape[:-1])) if len(shape) > 1 else 1
    w2, g2, m2, v2 = [t.reshape(rows, cols) for t in (w, g, m, v)]
    tr = _pick(rows, (512, 352, 256)) if rows % 8 == 0 else rows

    def body(w_ref, g_ref, m_ref, v_ref, d_ref, nm_ref, nv_ref):
        d, nm, nv = _adamw_math(w_ref[...], g_ref[...], m_ref[...], v_ref[...])
        d_ref[...] = d
        nm_ref[...] = nm
        nv_ref[...] = nv

    blk = pl.BlockSpec((tr, cols), lambda i: (i, 0))
    out = pl.pallas_call(
        body, grid=(rows // tr,), in_specs=[blk] * 4, out_specs=[blk] * 3,
        out_shape=[jax.ShapeDtypeStruct((rows, cols), F32)] * 3, name=name,
        compiler_params=_cparams(("parallel",)),
    )(w2, g2, m2, v2)
    return [t.reshape(shape) for t in out]


def _pair_sum(g2, recv, core, name, after=()):
    _, nchip, r, c = g2.shape
    tr = _pick(r, (512,))

    def body(core_ref, a_ref, b_ref, *rest):
        o_ref = rest[-1]
        o_ref[...] = (a_ref[...].astype(F32) + b_ref[...].astype(F32)).astype(o_ref.dtype)

    grid_spec = pltpu.PrefetchScalarGridSpec(
        num_scalar_prefetch=1, grid=(nchip, r // tr),
        in_specs=[pl.BlockSpec((None, None, tr, c), lambda k, i, cr: (cr[0], k, i, 0)),
                  pl.BlockSpec((None, tr, c), lambda k, i, cr: (k, i, 0))] + [_after_spec(t) for t in after],
        out_specs=pl.BlockSpec((None, tr, c), lambda k, i, cr: (k, i, 0)))
    return pl.pallas_call(
        body, grid_spec=grid_spec, out_shape=jax.ShapeDtypeStruct((nchip, r, c), BF16), name=name,
        compiler_params=_cparams(("parallel", "parallel")),
    )(core, g2, recv, *after)


def _chip_sum(a, recv, chip, name, after=()):
    _, r, c = a.shape
    tr = _pick(r, (512,))

    def body(chip_ref, a_ref, b_ref, *rest):
        o_ref = rest[-1]
        o_ref[...] = ((a_ref[...].astype(F32) + b_ref[0].astype(F32)) + b_ref[1].astype(F32)) + b_ref[2].astype(F32)

    grid_spec = pltpu.PrefetchScalarGridSpec(
        num_scalar_prefetch=1, grid=(r // tr,),
        in_specs=[pl.BlockSpec((None, tr, c), lambda i, cr: (cr[0], i, 0)),
                  pl.BlockSpec((3, tr, c), lambda i, cr: (0, i, 0))] + [_after_spec(t) for t in after],
        out_specs=pl.BlockSpec((tr, c), lambda i, cr: (i, 0)))
    return pl.pallas_call(
        body, grid_spec=grid_spec, out_shape=jax.ShapeDtypeStruct((r, c), F32), name=name,
        compiler_params=_cparams(("parallel",)),
    )(chip, a, recv, *after)


def _sum_devices(parts, name):
    n, r, c = parts.shape

    def body(p_ref, o_ref):
        acc = p_ref[0]
        for d in range(1, n):
            acc = acc + p_ref[d]
        o_ref[...] = acc

    return pl.pallas_call(
        body, out_shape=jax.ShapeDtypeStruct((r, c), F32), name=name,
        in_specs=[pl.BlockSpec(memory_space=pltpu.VMEM)], out_specs=pl.BlockSpec(memory_space=pltpu.VMEM),
    )(parts)


def _lane_iota():
    return lax.broadcasted_iota(jnp.int32, (1, LANES), 1)


def _tiles_up(tiles, s, lane):
    rolled = [pltpu.roll(t, s, 1) for t in tiles]
    zero = jnp.zeros_like(tiles[0])
    return [jnp.where(lane < s, p, c) for p, c in zip([zero] + rolled, rolled + [zero])]


def _tiles_down(tiles, s, lane):
    back = (LANES - s) % LANES
    rolled = [pltpu.roll(t, back, 1) for t in tiles]
    zero = jnp.zeros_like(tiles[0])
    return [jnp.where(lane < LANES - s, c, n) for c, n in zip(rolled, rolled[1:] + [zero])]


def _window_cols(prm, w_in, w_gate, w_up, l, name, after=None):
    tm = 256
    n_in, n_gu = WIN_IN // LANES, FFN_SHARD // LANES + 1

    def body(prm_ref, win_ref, g_ref, u_ref, *rest):
        out_ref, gu_ref, scr_in, scr_gu = rest[-4:]
        lane = _lane_iota()
        s_main, lo, hi, s_code, clo, chi, code_hi, s_gu = [prm_ref[i] for i in range(8)]
        scr_in[:, D_MODEL:WIN_IN] = jnp.zeros((tm, WIN_IN - D_MODEL), F32)
        scr_in[:, 0:W_IN_SHARD] = win_ref[...]

        def keep(t, a, b):
            col = lane + t * LANES
            return jnp.where((col >= a) & (col < b), scr_in[:, t * LANES:(t + 1) * LANES], 0.0)

        main = _tiles_up([keep(t, lo, hi) for t in range(n_in)], s_main, lane)
        for t in range(n_in):
            out_ref[:, t * LANES:(t + 1) * LANES] = main[t].astype(BF16)
        low = _tiles_up([keep(0, clo, chi)], s_code, lane)[0]
        high = _tiles_up([keep(n_in - 2, clo, chi), keep(n_in - 1, clo, chi)], s_code, lane)[1]
        out_ref[:, A_FZ:A_FZ + LANES] = jnp.where(code_hi == 1, high, low).astype(BF16)
        for ref, base in ((g_ref, 0), (u_ref, WIN_GU)):
            scr_gu[:, (n_gu - 1) * LANES:n_gu * LANES] = jnp.zeros((tm, LANES), F32)
            scr_gu[:, 0:FFN_SHARD] = ref[...]
            moved = _tiles_up([scr_gu[:, t * LANES:(t + 1) * LANES] for t in range(n_gu)], s_gu, lane)
            for t in range(n_gu + 1):
                gu_ref[:, base + t * LANES:base + (t + 1) * LANES] = moved[t].astype(BF16)

    after_args = [] if after is None else [after]
    grid_spec = pltpu.PrefetchScalarGridSpec(
        num_scalar_prefetch=1, grid=(D_MODEL // tm,),
        in_specs=[pl.BlockSpec((None, tm, W_IN_SHARD), lambda i, p: (l, i, 0)),
                  pl.BlockSpec((None, tm, FFN_SHARD), lambda i, p: (l, i, 0)),
                  pl.BlockSpec((None, tm, FFN_SHARD), lambda i, p: (l, i, 0))] + [_after_spec(t) for t in after_args],
        out_specs=[pl.BlockSpec((tm, IN_COLS), lambda i, p: (i, 0)), pl.BlockSpec((tm, GU_COLS), lambda i, p: (i, 0))],
        scratch_shapes=[pltpu.VMEM((tm, WIN_IN), F32), pltpu.VMEM((tm, n_gu * LANES), F32)])
    return pl.pallas_call(
        body, grid_spec=grid_spec, name=name, compiler_params=_cparams(("parallel",)),
        out_shape=[jax.ShapeDtypeStruct((D_MODEL, IN_COLS), BF16), jax.ShapeDtypeStruct((D_MODEL, GU_COLS), BF16)],
    )(prm, w_in, w_gate, w_up, *after_args)


def _gu_width(j):
    return min(WIN_GU, FFN - T_GU[j] * LANES)


def _after_spec(t):
    tile = (SUBLANES_BF16 if t.dtype == BF16 else 8, LANES)
    return pl.BlockSpec((None,) * (t.ndim - 2) + tile, lambda *_: (0,) * t.ndim)


def _assemble_in(a_all, tail_all, name, after=()):
    tm = D_MODEL // N_DEV

    def body(a_ref, t_ref, *rest):
        win_ref, tail_ref = rest[-2:]
        tail_ref[...] = t_ref[...]
        win_ref[...] = jnp.zeros_like(win_ref)
        code = a_ref[0, :, A_FZ:A_FZ + LANES]
        for j in range(N_DEV):
            c0 = T_IN[j] * LANES
            win_ref[:, c0:c0 + WIN_IN] += a_ref[j, :, 0:WIN_IN]
            if j > 0:
                code = code + a_ref[j, :, A_FZ:A_FZ + LANES]
        win_ref[:, FZ_COL:PROJ_W] = code

    return pl.pallas_call(
        body, grid=(N_DEV,),
        in_specs=[pl.BlockSpec((N_DEV, tm, IN_COLS), lambda i: (0, i, 0)),
                  pl.BlockSpec((None,) + tail_all.shape[1:], lambda i: (i, 0, 0))] + [_after_spec(t) for t in after],
        out_specs=[pl.BlockSpec((tm, PROJ_W), lambda i: (i, 0)),
                   pl.BlockSpec((None,) + tail_all.shape[1:], lambda i: (i, 0, 0))],
        out_shape=[jax.ShapeDtypeStruct((D_MODEL, PROJ_W), BF16), jax.ShapeDtypeStruct(tail_all.shape, BF16)],
        name=name, compiler_params=_cparams(("parallel",)),
    )(a_all, tail_all, *after)


def _assemble_rest(a_all, rows_all, name, after=()):
    tm = D_MODEL // N_DEV
    n_in = 2 + len(after)

    def body(*refs):
        a_ref, r_ref = refs[:2]
        wgu_ref, oa_ref, ob_ref, o_ref, down_ref = refs[n_in:]
        wgu_ref[...] = jnp.zeros_like(wgu_ref)
        for j in range(N_DEV):
            g0, width = T_GU[j] * LANES, _gu_width(j)
            wgu_ref[:, g0:g0 + width] += a_ref[j, :, 0:width]
            wgu_ref[:, FFN + g0:FFN + g0 + width] += a_ref[j, :, WIN_GU:WIN_GU + width]
        oa_ref[...] = r_ref[B_OA:B_OA + ROW_SHARD, :]
        ob_ref[...] = r_ref[B_OB:B_OB + ROW_SHARD, :]
        o_ref[...] = r_ref[B_O:B_O + ROW_SHARD, :]
        down_ref[...] = r_ref[B_DOWN:B_DOWN + FFN_SHARD, :]

    def rows(n):
        return pl.BlockSpec((n, D_MODEL), lambda i: (i, 0))

    square = jax.ShapeDtypeStruct((D_MODEL, D_MODEL), BF16)
    return pl.pallas_call(
        body, grid=(N_DEV,),
        in_specs=[pl.BlockSpec((N_DEV, tm, GU_COLS), lambda i: (0, i, 0)),
                  pl.BlockSpec((None, B_FG2, D_MODEL), lambda i: (i, 0, 0))] + [_after_spec(t) for t in after],
        out_specs=[pl.BlockSpec((tm, 2 * FFN), lambda i: (i, 0)),
                   rows(ROW_SHARD), rows(ROW_SHARD), rows(ROW_SHARD), rows(FFN_SHARD)],
        out_shape=[jax.ShapeDtypeStruct((D_MODEL, 2 * FFN), BF16),
                   square, square, square, jax.ShapeDtypeStruct((FFN, D_MODEL), BF16)],
        name=name, compiler_params=_cparams(("parallel",)),
    )(a_all, rows_all, *after)


def _grad_windows_in(d_in, name):
    tm = 256

    def body(din_ref, out_ref):
        for j in range(N_DEV):
            c0 = T_IN[j] * LANES
            out_ref[j & 1, j >> 1, :, 0:WIN_IN] = din_ref[:, c0:c0 + WIN_IN]
            out_ref[j & 1, j >> 1, :, A_FZ:IN_COLS] = din_ref[:, FZ_COL:PROJ_W]

    return pl.pallas_call(
        body, grid=(D_MODEL // tm,), in_specs=[pl.BlockSpec((tm, PROJ_W), lambda i: (i, 0))],
        out_specs=pl.BlockSpec((2, 4, tm, IN_COLS), lambda i: (0, 0, i, 0)),
        out_shape=jax.ShapeDtypeStruct((2, 4, D_MODEL, IN_COLS), BF16), name=name,
        compiler_params=_cparams(("parallel",)),
    )(d_in)


def _grad_windows_gu(d_gu, name):
    tm = 256

    def body(dgu_ref, out_ref):
        for j in range(N_DEV):
            g0, width = T_GU[j] * LANES, _gu_width(j)
            for half, base in ((0, 0), (FFN, WIN_GU)):
                out_ref[j & 1, j >> 1, :, base:base + width] = dgu_ref[:, half + g0:half + g0 + width]
                if width < WIN_GU:
                    out_ref[j & 1, j >> 1, :, base + width:base + WIN_GU] = jnp.zeros((tm, WIN_GU - width), BF16)

    return pl.pallas_call(
        body, grid=(D_MODEL // tm,), in_specs=[pl.BlockSpec((tm, 2 * FFN), lambda i: (i, 0))],
        out_specs=pl.BlockSpec((2, 4, tm, GU_COLS), lambda i: (0, 0, i, 0)),
        out_shape=jax.ShapeDtypeStruct((2, 4, D_MODEL, GU_COLS), BF16), name=name,
        compiler_params=_cparams(("parallel",)),
    )(d_gu)


def _final_windows(prm, chip, a, recv, params, l, kind, name, prev=None):
    tm = 128
    n_in, n_gu = WIN_IN // LANES, FFN_SHARD // LANES + 1
    widths = (W_IN_SHARD,) if kind == "in" else (FFN_SHARD, FFN_SHARD)
    cols = IN_COLS if kind == "in" else GU_COLS
    n_par = 3 * len(widths)
    prev = [] if prev is None else [t for group in prev for t in group]

    def body(prm_ref, chip_ref, a_ref, r_ref, *refs):
        ins, outs = refs[:n_par], refs[n_par + len(prev):]
        lane = _lane_iota()
        s_main, s_code, clo, chi, s_gu = [prm_ref[i] for i in (0, 3, 4, 5, 7)]

        def total(c0):
            sl = slice(c0, c0 + LANES)
            return ((a_ref[:, sl].astype(F32) + r_ref[0, :, sl].astype(F32)) + r_ref[1, :, sl].astype(F32)) \
                + r_ref[2, :, sl].astype(F32)

        if kind == "in":
            grads = _tiles_down([total(t * LANES) for t in range(n_in)], s_main, lane)
            code = pltpu.roll(total(A_FZ), (LANES - s_code) % LANES, 1)
            for t in (0, n_in - 2, n_in - 1):
                col = lane + t * LANES
                grads[t] = jnp.where((col >= clo) & (col < chi), code, grads[t])
            per_weight = [grads]
        else:
            per_weight = [_tiles_down([total(base + t * LANES) for t in range(n_gu + 1)], s_gu, lane)[:n_gu]
                          for base in (0, WIN_GU)]
        for k, (tiles, width) in enumerate(zip(per_weight, widths)):
            w_ref, m_ref, v_ref = ins[3 * k:3 * k + 3]
            g_ref, d_ref, nm_ref, nv_ref = outs[4 * k:4 * k + 4]
            for t, g in enumerate(tiles):
                n = min(LANES, width - t * LANES)
                sl = slice(t * LANES, t * LANES + n)
                g = g[:, 0:n]
                d, nm, nv = _adamw_math(w_ref[:, sl], g, m_ref[:, sl], v_ref[:, sl])
                g_ref[:, sl] = g
                d_ref[:, sl] = d
                nm_ref[:, sl] = nm
                nv_ref[:, sl] = nv

    def native(width):
        return pl.BlockSpec((None, tm, width), lambda i, p, c: (l, i, 0))

    in_specs = [pl.BlockSpec((None, tm, cols), lambda i, p, c: (c[0], i, 0)),
                pl.BlockSpec((3, tm, cols), lambda i, p, c: (0, i, 0))]
    in_specs += [native(wd) for wd in widths for _ in range(3)] + [_ANY] * len(prev)
    grid_spec = pltpu.PrefetchScalarGridSpec(
        num_scalar_prefetch=2, grid=(D_MODEL // tm,), in_specs=in_specs,
        out_specs=[native(wd) for wd in widths for _ in range(4)])
    out = pl.pallas_call(
        body, grid_spec=grid_spec, name=name, compiler_params=_cparams(("parallel",)),
        out_shape=[jax.ShapeDtypeStruct((DEPTH, D_MODEL, wd), F32) for wd in widths for _ in range(4)],
        input_output_aliases={4 + n_par + k: k for k in range(len(prev))},
    )(prm, chip, a, recv, *params, *prev)
    return [out[4 * k:4 * k + 4] for k in range(len(widths))]


def _me():
    return lax.axis_index("x"), lax.axis_index("y"), lax.axis_index("c")


_CHIP_FLIPS = ((1, 0), (0, 1), (1, 1))
_ANY = pl.BlockSpec(memory_space=pl.ANY)


def _comm_call(body, peers, out_shape, sems, name, args, collective_id):
    if collective_id is None:
        n_in = len(args)
        return pl.pallas_call(body, out_shape=out_shape, name=name, in_specs=[_ANY] * n_in,
                              out_specs=[_ANY] * len(out_shape), scratch_shapes=sems)(*args)

    def sequencer_body(*refs):
        barrier = pltpu.get_barrier_semaphore()
        targets = peers()
        for peer in targets:
            pl.semaphore_signal(barrier, inc=1, device_id=peer, device_id_type=MESH)
        pl.semaphore_wait(barrier, len(targets))
        body(*refs)

    sequencer = plsc.ScalarSubcoreMesh(axis_name="seq", num_cores=1)
    return pl.kernel(sequencer_body, out_type=out_shape, mesh=sequencer, scratch_types=sems, name=name,
                     compiler_params=pltpu.CompilerParams(collective_id=collective_id))(*args)


def _sibling_peer():
    x, y, cc = _me()
    return [(x, y, 1 - cc)]


def _chip_peers():
    x, y, cc = _me()
    return [(x ^ fx, y ^ fy, cc) for fx, fy in _CHIP_FLIPS]


def _all_gather(shards, name, collective_id=None):
    n = len(shards)
    split = [s.shape[0] % (2 * SUBLANES_BF16) == 0 for s in shards]

    def body(*refs):
        x_refs, out_refs = refs[:n], refs[n:2 * n]
        send_sems, recv_sems, local_sems = refs[2 * n:]
        x, y, cc = _me()
        me, sibling = (x, y, cc), (x, y, 1 - cc)
        near, far = [(x ^ 1, y), (x, y ^ 1)], (x ^ 1, y ^ 1)

        def copy(a, k, block, to, half=None, from_shard=False):
            px, py, pc = block
            slot = out_refs[a].at[4 * px + 2 * py + pc]
            if half is not None:
                rows = shards[a].shape[0] // 2
                slot = slot.at[pl.ds(half * rows, rows)]
            return pltpu.make_async_remote_copy(
                src_ref=x_refs[a] if from_shard else slot, dst_ref=slot,
                send_sem=send_sems.at[a, k], recv_sem=recv_sems.at[a, k], device_id=to, device_id_type=MESH)

        mine = [pltpu.make_async_copy(x_refs[a], out_refs[a].at[4 * x + 2 * y + cc], local_sems.at[a])
                for a in range(n)]
        for cp in mine:
            cp.start()
        sent = [copy(a, 0, me, sibling, from_shard=True) for a in range(n)]
        sent += [copy(a, 1 + j, me, (*chip, cc), from_shard=True) for j, chip in enumerate(near) for a in range(n)]
        sent += [copy(a, 3, me, (*far, cc), from_shard=True) for a in range(n) if not split[a]]
        for cp in sent:
            cp.start()

        def pass_on(cp):
            cp.start()
            sent.append(cp)

        for j, chip in enumerate(near):
            for a in range(n):
                copy(a, 1 + j, (*chip, cc), me).wait_recv()
                pass_on(copy(a, 4 + j, (*chip, cc), sibling))
                if split[a]:
                    pass_on(copy(a, 7 + j, (*chip, cc), (*near[1 - j], cc), half=j))
        for a in range(n):
            if split[a]:
                copy(a, 7, (*far, cc), me, half=0).wait_recv()
                copy(a, 8, (*far, cc), me, half=1).wait_recv()
            else:
                copy(a, 3, (*far, cc), me).wait_recv()
            pass_on(copy(a, 6, (*far, cc), sibling))
        for a in range(n):
            copy(a, 0, sibling, me).wait_recv()
            for j, chip in enumerate(near + [far]):
                copy(a, 4 + j, (*chip, 1 - cc), me).wait_recv()
        for cp in sent:
            cp.wait_send()
        for cp in mine:
            cp.wait()

    return _comm_call(
        body, lambda: _sibling_peer() + _chip_peers(),
        [jax.ShapeDtypeStruct((N_DEV,) + s.shape, s.dtype) for s in shards],
        [pltpu.SemaphoreType.DMA((n, 9)), pltpu.SemaphoreType.DMA((n, 9)), pltpu.SemaphoreType.DMA((n,))],
        name, shards, collective_id)


def _send_to_sibling(parts, name, collective_id=None):
    n = len(parts)

    def body(*refs):
        g_refs, out_refs = refs[:n], refs[n:2 * n]
        send_sems, recv_sems = refs[2 * n:]
        x, y, cc = _me()
        copies = [pltpu.make_async_remote_copy(
            src_ref=g_refs[a].at[1 - cc], dst_ref=out_refs[a], send_sem=send_sems.at[a], recv_sem=recv_sems.at[a],
            device_id=(x, y, 1 - cc), device_id_type=MESH) for a in range(n)]
        for cp in copies:
            cp.start()
        for cp in copies:
            cp.wait()

    return _comm_call(
        body, _sibling_peer, [jax.ShapeDtypeStruct(p.shape[1:], p.dtype) for p in parts],
        [pltpu.SemaphoreType.DMA((n,)), pltpu.SemaphoreType.DMA((n,))], name, parts, collective_id)


def _send_to_chips(parts, name, collective_id=None):
    n = len(parts)

    def body(*refs):
        a_refs, out_refs = refs[:n], refs[n:2 * n]
        send_sems, recv_sems = refs[2 * n:]
        x, y, cc = _me()
        copies = []
        for k, (fx, fy) in enumerate(_CHIP_FLIPS):
            px, py = x ^ fx, y ^ fy
            for a in range(n):
                copies.append(pltpu.make_async_remote_copy(
                    src_ref=a_refs[a].at[2 * px + py], dst_ref=out_refs[a].at[k], send_sem=send_sems.at[a, k],
                    recv_sem=recv_sems.at[a, k], device_id=(px, py, cc), device_id_type=MESH))
                copies[-1].start()
        for cp in copies:
            cp.wait()

    return _comm_call(
        body, _chip_peers, [jax.ShapeDtypeStruct((3,) + p.shape[1:], p.dtype) for p in parts],
        [pltpu.SemaphoreType.DMA((n, 3)), pltpu.SemaphoreType.DMA((n, 3))], name, parts, collective_id)


def _pack_rows(w_oa, w_ob, w_o, w_down, w_fg2, conv_w, l):
    conv_bits = lax.bitcast_convert_type(conv_w[l].reshape(-1), BF16).reshape(1, -1)
    tail = jnp.concatenate([w_fg2[l].astype(BF16).reshape(1, D_MODEL),
                            jnp.pad(conv_bits, ((0, 0), (0, D_MODEL - conv_bits.shape[1])))], axis=0)
    tail = jnp.pad(tail, ((0, B_ROWS - B_FG2 - tail.shape[0]), (0, 0)))
    rows = jnp.concatenate([w_oa[l].astype(BF16), w_ob[l].astype(BF16), w_o[l].astype(BF16),
                            w_down[l].astype(BF16)], axis=0)
    return rows, tail


def _unpack_tail(tail):
    w_fg2 = tail[:, 0, :].reshape(N_DEV, RANK, KEY_W // N_DEV).transpose(1, 0, 2).reshape(RANK, KEY_W)
    conv_bits = tail[:, B_CONV - B_FG2, :2 * 3 * ROW_SHARD].reshape(N_DEV, 3 * ROW_SHARD, 2)
    conv_w = lax.bitcast_convert_type(conv_bits, F32).reshape(N_DEV, 3, ROW_SHARD)
    return jnp.pad(w_fg2, ((0, LANES - RANK), (0, 0))), conv_w.transpose(1, 0, 2).reshape(3, D_MODEL)


def _by_core_chip(t):
    return t.reshape((2, 2, 2) + t.shape[1:]).transpose((2, 0, 1) + tuple(range(3, t.ndim + 2))).reshape(
        (2, 4) + t.shape[1:])


def _grad_rows(g):
    fg2 = g["w_fg2"][:RANK].reshape(RANK, N_DEV, KEY_W // N_DEV).transpose(1, 0, 2).reshape(N_DEV, 1, D_MODEL)
    conv = g["conv_w"].astype(BF16).reshape(3, N_DEV, ROW_SHARD).transpose(1, 0, 2).reshape(N_DEV, 1, 3 * ROW_SHARD)
    tail = jnp.concatenate([fg2, jnp.pad(conv, ((0, 0), (0, 0), (0, D_MODEL - 3 * ROW_SHARD)))], axis=1)
    tail = jnp.pad(tail, ((0, 0), (0, B_ROWS - B_FG2 - 2), (0, 0)))
    parts = [g["w_oa"].reshape(N_DEV, ROW_SHARD, D_MODEL), g["w_ob"].reshape(N_DEV, ROW_SHARD, D_MODEL),
             g["w_o"].reshape(N_DEV, ROW_SHARD, D_MODEL), g["w_down"].reshape(N_DEV, FFN_SHARD, D_MODEL), tail]
    return _by_core_chip(jnp.concatenate(parts, axis=1))


def _ungrad_rows(gs):
    return dict(w_oa=gs[B_OA:B_OA + ROW_SHARD], w_ob=gs[B_OB:B_OB + ROW_SHARD], w_o=gs[B_O:B_O + ROW_SHARD],
                w_ffn_down=gs[B_DOWN:B_DOWN + FFN_SHARD], w_fg2=gs[B_FG2].reshape(RANK, KEY_W // N_DEV),
                conv_w=gs[B_CONV, :3 * ROW_SHARD].reshape(3, ROW_SHARD))


def _layer_fwd(x, h, p, l, next_norm_g=None):
    tag = f"l{l}_"
    if h is None:
        h = _rmsnorm_fwd(x, p["norm1_g"], tag + "norm1")
    proj = _matmul(h, p["w_in"], "nn", BF16, tag + "proj")
    o, states, oa = _gla_fwd(proj, p["w_fg2"], p["b_fg"], p["gla_norm_g"], tag + "gla_fwd")
    p.update(p.pop("rest")((o,)))
    ya = _matmul(oa, p["w_oa"], "nn", BF16, tag + "ya")
    cb = _conv_fwd(proj, p["conv_w"], p["conv_b"], tag + "conv")
    yb = _matmul(cb, p["w_ob"], "nn", BF16, tag + "yb")
    mix, x1, h2 = _mix_project(proj, ya, yb, p["w_o"], x, p["norm2_g"], tag + "x1")
    gu = _matmul(h2, p["w_gu"], "nn", BF16, tag + "gu")
    hid, x2, h_next = _swiglu_project(gu, p["w_down"], x1, next_norm_g, tag + "x2")
    saved = dict(x=x, h=h, proj=proj, o=o, states=states, oa=oa, ya=ya, cb=cb, yb=yb, mix=mix, x1=x1, h2=h2,
                 gu=gu, hid=hid)
    return x2, h_next, saved


def _layer_bwd(dx2, dx2h, p, sv, l, reduce=None):
    if reduce is None:
        reduce = lambda group, grads: ((), ())
    tag = f"l{l}_b_"
    dhid = _matmul(dx2h, p["w_down"], "nt", BF16, tag + "dhid")
    d_down = _matmul(sv["hid"], dx2h, "tn", BF16, tag + "dw_down")
    dgu = _swiglu_bwd(dhid, sv["gu"], tag + "swiglu")
    dx1, dx1h, dg2 = _matmul_norm_bwd(dgu, p["w_gu"], sv["x1"], p["norm2_g"], dx2, tag + "dh2")
    d_gu = _matmul(sv["h2"], dgu, "tn", BF16, tag + "dw_gu")
    gu_packed, gu_sums = reduce("gu", d_gu)
    dmix = _matmul(dx1h, p["w_o"], "nt", BF16, tag + "dmix", after=gu_packed)
    d_o = _matmul(sv["mix"], dx1h, "tn", BF16, tag + "dw_o")
    dproj, dya, dyb = _mix_bwd(dmix, sv["proj"], sv["ya"], sv["yb"], tag + "mix")
    dcb = _matmul(dyb, p["w_ob"], "nt", BF16, tag + "dcb", after=gu_sums)
    d_ob = _matmul(sv["cb"], dyb, "tn", BF16, tag + "dw_ob")
    dproj, dwb = _conv_bwd(dcb, sv["proj"], p["conv_w"], p["conv_b"], dproj, tag + "conv")
    doa = _matmul(dya, p["w_oa"], "nt", BF16, tag + "doa")
    d_oa = _matmul(sv["oa"], dya, "tn", BF16, tag + "dw_oa")
    dq, dk, dv, dfg, dbfg, dproj, dgg = _gla_bwd(sv["proj"], p["w_fg2"], p["b_fg"], sv["states"], doa, sv["o"],
                                                 p["gla_norm_g"], dproj, tag + "gla")
    dproj = _place_qkv(dq, dk, dv, dproj, tag + "place_qkv")
    dproj = _matmul(dfg, p["w_fg2"], "nt", BF16, tag + "dfz", into=(dproj, FZ_COL))
    d_fg2 = _matmul(sv["proj"], dfg, "tn", BF16, tag + "dw_fg2", a_cols=(FZ_COL, LANES))
    rows = dict(w_fg2=d_fg2, conv_w=dwb[0:3], w_oa=d_oa, w_ob=d_ob, w_o=d_o, w_down=d_down)
    rows_packed, rows_sums = reduce("rows", rows)
    d_in = _matmul(sv["h"], dproj, "tn", BF16, tag + "dw_in", after=rows_packed)
    _, in_sums = reduce("in", d_in)
    dx, dxh, dg1 = _matmul_norm_bwd(dproj, p["w_in"], sv["x"], p["norm1_g"], dx1, tag + "dh",
                                    after=(d_in,) + tuple(rows_sums) + tuple(in_sums))
    big = dict(w_in=d_in, w_gu=d_gu, **rows)
    pad = lambda t: jnp.pad(t, ((0, 0), (0, D_MODEL - t.shape[1])))
    small = [dg1[0:1], pad(dbfg[0:1]), pad(dgg[0:1]), dwb[3:4], dg2[0:1]]
    return dx, dxh, big, small


def _local_step(x, target, weights_of, final_g, reduce_of=None):
    saved, layers, h = [], [], None
    for l in range(DEPTH):
        layers.append(weights_of(l, x))
        x, h, sv = _layer_fwd(x, h, layers[l], l, layers[l].get("next_norm1_g"))
        saved.append(sv)
    loss, dx, dxh, dgf = _loss_head(x, final_g, target, "loss_head")
    bigs, smalls = [None] * DEPTH, [None] * DEPTH
    for l in reversed(range(DEPTH)):
        dx, dxh, bigs[l], smalls[l] = _layer_bwd(dx, dxh, layers[l], saved[l], l, reduce_of(l) if reduce_of else None)
    loss_row = jnp.pad(loss[0:1], ((0, 0), (0, D_MODEL - loss.shape[1])))
    small = jnp.concatenate(smalls[0] + smalls[1] + [dgf[0:1], loss_row], axis=0)
    small = jnp.pad(small, ((0, SMALL_ROWS - small.shape[0]), (0, 0)))
    return loss[0, 0], dx, bigs, small


def kernel(x, norm1_g, w_in, w_fg2, b_fg, gla_norm_g, w_oa, conv_w, conv_b, w_ob, w_o, norm2_g, w_ffn_gate, w_ffn_up, w_ffn_down, final_g, loss_target, m_norm1_g, m_w_in, m_w_fg2, m_b_fg, m_gla_norm_g, m_w_oa, m_conv_w, m_conv_b, m_w_ob, m_w_o, m_norm2_g, m_w_ffn_gate, m_w_ffn_up, m_w_ffn_down, m_final_g, v_norm1_g, v_w_in, v_w_fg2, v_b_fg, v_gla_norm_g, v_w_oa, v_conv_w, v_conv_b, v_w_ob, v_w_o, v_norm2_g, v_w_ffn_gate, v_w_ffn_up, v_w_ffn_down, v_final_g):
    names = ["norm1_g", "w_in", "w_fg2", "b_fg", "gla_norm_g", "w_oa", "conv_w", "conv_b", "w_ob", "w_o",
             "norm2_g", "w_ffn_gate", "w_ffn_up", "w_ffn_down", "final_g"]
    w = dict(zip(names, [norm1_g, w_in, w_fg2, b_fg, gla_norm_g, w_oa, conv_w, conv_b, w_ob, w_o, norm2_g,
                         w_ffn_gate, w_ffn_up, w_ffn_down, final_g]))
    m = dict(zip(names, [m_norm1_g, m_w_in, m_w_fg2, m_b_fg, m_gla_norm_g, m_w_oa, m_conv_w, m_conv_b, m_w_ob,
                         m_w_o, m_norm2_g, m_w_ffn_gate, m_w_ffn_up, m_w_ffn_down, m_final_g]))
    v = dict(zip(names, [v_norm1_g, v_w_in, v_w_fg2, v_b_fg, v_gla_norm_g, v_w_oa, v_conv_w, v_conv_b, v_w_ob,
                         v_w_o, v_norm2_g, v_w_ffn_gate, v_w_ffn_up, v_w_ffn_down, v_final_g]))
    col_names = ["w_in", "w_ffn_gate", "w_ffn_up"]
    cx, cy, cc = _me()
    prm = jnp.asarray(SHIFT_TABLE)[4 * cx + 2 * cy + cc]
    core = jnp.reshape(cc, (1,)).astype(jnp.int32)
    chip = jnp.reshape(2 * cx + cy, (1,)).astype(jnp.int32)

    ids = iter(range(32))

    gathered, previous = [], None
    for l in range(DEPTH):
        rows, tail = _pack_rows(w_oa, w_ob, w_o, w_ffn_down, w_fg2, conv_w, l)
        win_in, win_gu = _window_cols(prm, w_in, w_ffn_gate, w_ffn_up, l, f"l{l}_windows", after=previous)
        previous = rows
        first = _all_gather([win_in, tail], f"l{l}_gather_in", next(ids))
        gathered.append(list(first) + list(_all_gather([win_gu, rows], f"l{l}_gather_rest", next(ids))))

    def weights_of(l, x_in):
        all_in, all_tail, all_gu, all_rows = gathered[l]
        after = (x_in,) if l > 0 else ()
        w_in_full, tail = _assemble_in(all_in, all_tail, f"l{l}_assemble_in", after)
        w_fg2_full, conv_w_full = _unpack_tail(tail)

        def rest(after_rest):
            names_rest = ("w_gu", "w_oa", "w_ob", "w_o", "w_down")
            return dict(zip(names_rest, _assemble_rest(all_gu, all_rows, f"l{l}_assemble_rest", after + after_rest)))

        return dict(w_in=w_in_full, rest=rest, w_fg2=w_fg2_full,
                    conv_w=conv_w_full, norm1_g=norm1_g[l][None], b_fg=b_fg[l][None],
                    gla_norm_g=gla_norm_g[l][None], conv_b=conv_b[l][None], norm2_g=norm2_g[l][None],
                    next_norm1_g=norm1_g[l + 1][None] if l + 1 < DEPTH else None)

    pending = [dict() for _ in range(DEPTH)]
    landed = []

    def reduce_of(l):
        def reduce(group, grads):
            tag = f"l{l}_{group}"
            if group == "gu":
                packed = _grad_windows_gu(grads, tag + "_windows")
            elif group == "in":
                packed = _grad_windows_in(grads, tag + "_windows")
            else:
                packed = _grad_rows(grads)
            (from_sibling,) = _send_to_sibling([packed], tag + "_to_sibling", next(ids))
            waited = () if group == "in" else tuple(landed)
            if waited:
                landed.clear()
            sums = _pair_sum(packed, from_sibling, core, tag + "_pair_sum", after=waited)
            (from_chips,) = _send_to_chips([sums], tag + "_to_chips", next(ids))
            landed.append(from_chips)
            pending[l][group] = (sums, from_chips)
            return (packed,), (sums,)
        return reduce

    loss, dx, bigs, small = _local_step(x[0], loss_target[0], weights_of, final_g[None], reduce_of)

    grads, deltas, new_m, new_v = {}, {}, {}, {}
    for kind, group_names in (("gu", col_names[1:]), ("in", col_names[:1])):
        params = [t for n in group_names for t in (w[n], m[n], v[n])]
        out = None
        for l in reversed(range(DEPTH)):
            sums, from_chips = pending[l][kind]
            out = _final_windows(prm, chip, sums, from_chips, params, l, kind, f"l{l}_{kind}_final", out)
        for n, (g, d, nm, nv) in zip(group_names, out):
            grads[n], deltas[n], new_m[n], new_v[n] = g, d, nm, nv
    row_grads = [_ungrad_rows(_chip_sum(*pending[l]["rows"], chip, f"l{l}_rows_chip_sum")) for l in range(DEPTH)]
    for n in row_grads[0]:
        grads[n] = jnp.stack([row_grads[l][n] for l in range(DEPTH)])

    small_sum = _sum_devices(_all_gather([small], "gather_small")[0], "sum_small")
    r512, r256 = slice(0, KEY_W), slice(0, HEAD_V)
    grads.update(
        norm1_g=jnp.stack([small_sum[0], small_sum[5]]), b_fg=jnp.stack([small_sum[1, r512], small_sum[6, r512]]),
        gla_norm_g=jnp.stack([small_sum[2, r256], small_sum[7, r256]]),
        conv_b=jnp.stack([small_sum[3], small_sum[8]]), norm2_g=jnp.stack([small_sum[4], small_sum[9]]),
        final_g=small_sum[10])

    for n in names:
        if n not in col_names:
            deltas[n], new_m[n], new_v[n] = _adamw(w[n], grads[n], m[n], v[n], "adamw_" + n)

    total_loss = small_sum[2 * 5 + 1, 0]
    return (total_loss, dx[None], *[grads[n] for n in names], *[deltas[n] for n in names],
            *[new_m[n] for n in names], *[new_v[n] for n in names])
```

```python
import functools

import jax
import jax.numpy as jnp
import numpy as np
from jax import lax
from jax.experimental import pallas as pl
from jax.experimental.pallas import tpu as pltpu
from jax.experimental.pallas import tpu_sc as plsc

F32 = jnp.float32
BF16 = jnp.bfloat16
MESH = pl.DeviceIdType.MESH

D_MODEL = 1024
DEPTH = 2
CHUNK = 64
HEADS = 4
HEAD_K = 128
HEAD_V = 256
KEY_W = HEADS * HEAD_K
VAL_W = HEADS * HEAD_V
RANK = 16
TAU = 16.0
FFN = 2816
IN_WIDTH = 2 * KEY_W + 2 * VAL_W + RANK + 5 * D_MODEL
EPS = 1e-6
Q_SCALE = HEAD_K ** -0.5
N_DEV = 8
ADAM_LR, ADAM_B1, ADAM_B2, ADAM_EPS, ADAM_WD, ADAM_STEP = 0.001, 0.9, 0.999, 1e-08, 0.01, 10

LANES = 128
SUBLANES_BF16 = 16
VMEM_LIMIT = 48 * 1024 * 1024
VMEM_LIMIT_WIDE = 56 * 1024 * 1024

FZ_COL = 2 * KEY_W + 2 * VAL_W + 5 * D_MODEL
PROJ_W = FZ_COL + LANES
SEG_R, SEG_GBI, SEG_GCI, SEG_CX, SEG_GA, SEG_GB = 2, 3, 4, 5, 6, 7

W_IN_SHARD = IN_WIDTH // N_DEV
FFN_SHARD = FFN // N_DEV
ROW_SHARD = D_MODEL // N_DEV

WIN_IN = 9 * LANES
WIN_GU = 4 * LANES
A_FZ = WIN_IN
IN_COLS = A_FZ + LANES
GU_COLS = 2 * WIN_GU
ORIG_FZ = 2 * KEY_W + 2 * VAL_W


def _new_col(o):
    if o < ORIG_FZ:
        return o
    if o < ORIG_FZ + RANK:
        return FZ_COL + (o - ORIG_FZ)
    return o - RANK


def _shift_table():
    t_in, rows = [], []
    for j in range(N_DEV):
        new = [_new_col(W_IN_SHARD * j + i) for i in range(W_IN_SHARD)]
        main = [i for i in range(W_IN_SHARD) if new[i] < FZ_COL]
        code = [i for i in range(W_IN_SHARD) if new[i] >= FZ_COL]
        shift = new[main[0]] - main[0]
        t_in.append(shift // LANES)
        assert all(new[i] - i == shift for i in main) and shift % LANES + W_IN_SHARD <= WIN_IN
        if code:
            cshift = new[code[0]] - FZ_COL - code[0]
            crow = [cshift % LANES, code[0], code[-1] + 1, int(cshift < 0)]
        else:
            crow = [0, 0, 0, 0]
        rows.append([shift % LANES, main[0], main[-1] + 1] + crow + [FFN_SHARD * j % LANES])
    return tuple(t_in), np.asarray(rows, np.int32)


T_IN, SHIFT_TABLE = _shift_table()
T_GU = tuple(FFN_SHARD * j // LANES for j in range(N_DEV))

B_OA, B_OB, B_O, B_DOWN = 0, ROW_SHARD, 2 * ROW_SHARD, 3 * ROW_SHARD
B_FG2 = B_DOWN + FFN_SHARD
B_CONV = B_FG2 + 1
B_ROWS = B_FG2 + SUBLANES_BF16
SMALL_ROWS = 32


def _pick(n, candidates):
    for c in candidates:
        if n % c == 0:
            return c
    return n


def _cparams(sem):
    return pltpu.CompilerParams(dimension_semantics=sem, vmem_limit_bytes=VMEM_LIMIT)


def _sigmoid(x):
    return 1.0 / (1.0 + jnp.exp(-x))


def _matmul(a, b, dims, out_dtype, name, after=(), into=None, a_cols=None):
    if dims == "nn":
        (m, k), (k2, n) = a.shape, b.shape
    elif dims == "nt":
        (m, k), (n, k2) = a.shape, b.shape
    else:
        (k, m), (k2, n) = a.shape, b.shape
        m = m if a_cols is None else a_cols[1]
    assert k == k2 and (a_cols is None or dims == "tn"), (a.shape, b.shape, dims)
    tm = _pick(m, (1024, 1408, 512, 256, 128))
    tn = _pick(n, (1664, 1408, 1024, 512, 256, 128))
    tk = _pick(k, (1664, 1408, 1024, 512, 256, 128))
    nk = k // tk
    if dims == "nn":
        a_spec = pl.BlockSpec((tm, tk), lambda i, j, kk: (i, kk))
        b_spec = pl.BlockSpec((tk, tn), lambda i, j, kk: (kk, j))
        contract = (((1,), (0,)), ((), ()))
    elif dims == "nt":
        a_spec = pl.BlockSpec((tm, tk), lambda i, j, kk: (i, kk))
        b_spec = pl.BlockSpec((tn, tk), lambda i, j, kk: (j, kk))
        contract = (((1,), (1,)), ((), ()))
    else:
        first = 0 if a_cols is None else a_cols[0] // tm
        assert a_cols is None or (tm == m and a_cols[0] % tm == 0)
        a_spec = pl.BlockSpec((tk, tm), lambda i, j, kk: (kk, first + i))
        b_spec = pl.BlockSpec((tk, tn), lambda i, j, kk: (kk, j))
        contract = (((0,), (0,)), ((), ()))
    o_spec = pl.BlockSpec((tm, tn), lambda i, j, kk: (i, j))
    out_spec, out_struct, placed, aliases = o_spec, jax.ShapeDtypeStruct((m, n), out_dtype), (), {}
    if into is not None:
        buffer, col = into
        assert col % tn == 0 and buffer.dtype == out_dtype
        out_spec = pl.BlockSpec((tm, tn), lambda i, j, kk: (i, col // tn + j))
        out_struct, placed, aliases = jax.ShapeDtypeStruct(buffer.shape, out_dtype), (buffer,), {2 + len(after): 0}

    def body(*refs):
        a_ref, b_ref = refs[:2]
        o_ref = refs[2 + len(after) + len(placed)]
        kk = pl.program_id(2)
        part = lax.dot_general(a_ref[...], b_ref[...], contract, preferred_element_type=F32)

        def finish(total):
            o_ref[...] = total.astype(o_ref.dtype)

        if nk == 1:
            finish(part)
            return
        acc_ref = refs[-1]

        @pl.when(kk == 0)
        def _():
            acc_ref[...] = part

        @pl.when((kk > 0) & (kk < nk - 1))
        def _():
            acc_ref[...] += part

        @pl.when(kk == nk - 1)
        def _():
            finish(acc_ref[...] + part)

    in_specs = [a_spec, b_spec] + [_after_spec(t) for t in after] + [_ANY] * len(placed)
    args = (a, b) + tuple(after) + placed
    return pl.pallas_call(
        body, grid=(m // tm, n // tn, nk), in_specs=in_specs, out_specs=out_spec,
        out_shape=out_struct, input_output_aliases=aliases,
        scratch_shapes=[pltpu.VMEM((tm, tn), F32)] if nk > 1 else [], name=name,
        compiler_params=_cparams(("parallel", "parallel", "arbitrary")),
    )(*args)


def _rmsnorm_fwd(x, g, name):
    s, d = x.shape
    tm = _pick(s, (512, 256))

    def body(x_ref, g_ref, o_ref):
        xv = x_ref[...]
        r = lax.rsqrt(jnp.mean(xv * xv, axis=-1, keepdims=True) + EPS)
        o_ref[...] = (xv * r * g_ref[...]).astype(o_ref.dtype)

    row = pl.BlockSpec((tm, d), lambda i: (i, 0))
    return pl.pallas_call(
        body, grid=(s // tm,), in_specs=[row, pl.BlockSpec((1, d), lambda i: (0, 0))], out_specs=row,
        out_shape=jax.ShapeDtypeStruct((s, d), BF16), name=name, compiler_params=_cparams(("parallel",)),
    )(x, g)


def _matmul_norm_bwd(a, b, x, g, dres, name, after=()):
    (s, k), (d, k2) = a.shape, b.shape
    assert k == k2 and x.shape == (s, d)
    tm = _pick(s, (1024, 512, 256))
    tk = _pick(k, (1664, 1408, 1024, 512, 256, 128))
    nk = k // tk

    def body(a_ref, b_ref, x_ref, g_ref, dres_ref, *rest):
        dx_ref, dx16_ref, dg_ref, acc_ref = rest[len(after):]
        i, kk = pl.program_id(0), pl.program_id(1)
        part = lax.dot_general(a_ref[...], b_ref[...], _NT, preferred_element_type=F32)

        def finish(dh):
            xv = x_ref[...]
            r = lax.rsqrt(jnp.mean(xv * xv, axis=-1, keepdims=True) + EPS)
            xn = xv * r
            dxn = dh * g_ref[...]
            dx = dres_ref[...] + r * (dxn - xn * jnp.mean(dxn * xn, axis=-1, keepdims=True))
            dx_ref[...] = dx
            dx16_ref[...] = dx.astype(BF16)
            dg = jnp.broadcast_to(jnp.sum(dh * xn, axis=0, keepdims=True), dg_ref.shape)

            @pl.when(i == 0)
            def _():
                dg_ref[...] = dg

            @pl.when(i > 0)
            def _():
                dg_ref[...] += dg

        @pl.when(kk == 0)
        def _():
            acc_ref[...] = part

        @pl.when((kk > 0) & (kk < nk - 1))
        def _():
            acc_ref[...] += part

        @pl.when(kk == nk - 1)
        def _():
            finish(acc_ref[...] + part)

    assert nk > 1
    row = pl.BlockSpec((tm, d), lambda i, kk: (i, 0))
    return pl.pallas_call(
        body, grid=(s // tm, nk),
        in_specs=[pl.BlockSpec((tm, tk), lambda i, kk: (i, kk)), pl.BlockSpec((d, tk), lambda i, kk: (0, kk)), row,
                  pl.BlockSpec((1, d), lambda i, kk: (0, 0)), row] + [_after_spec(t) for t in after],
        out_specs=[row, row, pl.BlockSpec((8, d), lambda i, kk: (0, 0))],
        out_shape=[jax.ShapeDtypeStruct((s, d), F32), jax.ShapeDtypeStruct((s, d), BF16),
                   jax.ShapeDtypeStruct((8, d), F32)],
        scratch_shapes=[pltpu.VMEM((tm, d), F32)], name=name,
        compiler_params=pltpu.CompilerParams(dimension_semantics=("arbitrary", "arbitrary"),
                                             vmem_limit_bytes=VMEM_LIMIT_WIDE),
    )(a, b, x, g, dres, *after)


def _loss_head(x, g, target, name):
    s, d = x.shape
    tm = _pick(s, (512, 256))

    def body(x_ref, g_ref, t_ref, loss_ref, dx_ref, dx16_ref, dg_ref):
        xv = x_ref[...]
        gv = g_ref[...]
        r = lax.rsqrt(jnp.mean(xv * xv, axis=-1, keepdims=True) + EPS)
        xn = xv * r
        err = xn * gv - t_ref[...]
        dy = err * (1.0 / d)
        dxn = dy * gv
        dx = r * (dxn - xn * jnp.mean(dxn * xn, axis=-1, keepdims=True))
        dx_ref[...] = dx
        dx16_ref[...] = dx.astype(BF16)

        @pl.when(pl.program_id(0) == 0)
        def _():
            dg_ref[...] = jnp.zeros_like(dg_ref)
            loss_ref[...] = jnp.zeros_like(loss_ref)

        dg_ref[...] += jnp.broadcast_to(jnp.sum(dy * xn, axis=0, keepdims=True), dg_ref.shape)
        row_loss = jnp.sum(err * err, axis=-1, keepdims=True)
        loss_ref[...] += jnp.broadcast_to((0.5 / d) * jnp.sum(row_loss, axis=0, keepdims=True), loss_ref.shape)

    row = pl.BlockSpec((tm, d), lambda i: (i, 0))
    return pl.pallas_call(
        body, grid=(s // tm,), in_specs=[row, pl.BlockSpec((1, d), lambda i: (0, 0)), row],
        out_specs=[pl.BlockSpec((8, LANES), lambda i: (0, 0)), row, row, pl.BlockSpec((8, d), lambda i: (0, 0))],
        out_shape=[jax.ShapeDtypeStruct((8, LANES), F32), jax.ShapeDtypeStruct((s, d), F32),
                   jax.ShapeDtypeStruct((s, d), BF16), jax.ShapeDtypeStruct((8, d), F32)],
        name=name, compiler_params=_cparams(("arbitrary",)),
    )(x, g, target)


def _chunk_cumsum(x):
    pos = lax.broadcasted_iota(jnp.int32, (x.shape[0], 1), 0) % CHUNK
    shift = 1
    while shift < CHUNK:
        x = x + jnp.where(pos >= shift, pltpu.roll(x, shift, 0), 0.0)
        shift *= 2
    return x


def _block_decay(fz, w, b):
    fg = jnp.dot(fz, w, preferred_element_type=F32) + b
    la = (jnp.minimum(fg, 0.0) - jnp.log(1.0 + jnp.exp(-jnp.abs(fg)))) * (1.0 / TAU)
    cum = _chunk_cumsum(la)
    ends = [cum[i + CHUNK - 1:i + CHUNK, :] for i in range(0, fz.shape[0], CHUNK)]
    end = jnp.concatenate([jnp.broadcast_to(e, (CHUNK, e.shape[1])) for e in ends], axis=0)
    return fg, jnp.exp(end - cum), [jnp.exp(e) for e in ends]


_TN = (((0,), (0,)), ((), ()))
_NT = (((1,), (1,)), ((), ()))


def _gla_specs(rows):
    q_spec = pl.BlockSpec((rows, HEAD_K), lambda h, c: (c, h))
    k_spec = pl.BlockSpec((rows, HEAD_K), lambda h, c: (c, HEADS + h))
    v_spec = pl.BlockSpec((rows, HEAD_V), lambda h, c: (c, HEADS + h))
    fz_spec = pl.BlockSpec((rows, LANES), lambda h, c: (c, FZ_COL // LANES))
    w_spec = pl.BlockSpec((LANES, HEAD_K), lambda h, c: (0, h))
    b_spec = pl.BlockSpec((1, HEAD_K), lambda h, c: (0, h))
    return q_spec, k_spec, v_spec, fz_spec, w_spec, b_spec


def _gla_fwd(proj, wfg, bfg, gla_g, name):
    s = proj.shape[0]
    nc = s // CHUNK
    per = _pick(nc, (8, 4, 2, 1))
    rows = per * CHUNK

    def body(q_ref, k_ref, v_ref, fz_ref, w_ref, b_ref, r_ref, g_ref, o_ref, st_ref, oa_ref, state, update):
        @pl.when(pl.program_id(1) == 0)
        def _():
            state[...] = jnp.zeros_like(state)

        _, dec, gammas = _block_decay(fz_ref[...], w_ref[...], b_ref[...])
        kd = (k_ref[...].astype(F32) * dec).astype(BF16)
        qs = (q_ref[...].astype(F32) * Q_SCALE).astype(BF16)
        for i in range(per):
            sl = slice(i * CHUNK, (i + 1) * CHUNK)
            update[i] = lax.dot_general(v_ref[sl, :], kd[sl], _TN, preferred_element_type=F32)
        st = state[...]
        for i in range(per):
            st = st * gammas[i] + update[i]
            st_ref[0, i] = st.astype(BF16)
        state[...] = st
        for i in range(per):
            sl = slice(i * CHUNK, (i + 1) * CHUNK)
            o_ref[sl, :] = lax.dot_general(qs[sl], st_ref[0, i], _NT, preferred_element_type=F32).astype(o_ref.dtype)
        ov = o_ref[...].astype(F32)
        rstd = lax.rsqrt(jnp.mean(ov * ov, axis=-1, keepdims=True) + EPS)
        rv = r_ref[...].astype(F32)
        oa_ref[...] = (ov * rstd * g_ref[...] * (rv * _sigmoid(rv))).astype(oa_ref.dtype)

    q_spec, k_spec, v_spec, fz_spec, w_spec, b_spec = _gla_specs(rows)
    head_v = pl.BlockSpec((rows, HEAD_V), lambda h, c: (c, h))
    return pl.pallas_call(
        body, grid=(HEADS, nc // per),
        in_specs=[q_spec, k_spec, v_spec, fz_spec, w_spec, b_spec,
                  pl.BlockSpec((rows, HEAD_V), lambda h, c: (c, SEG_R * (D_MODEL // HEAD_V) + h)),
                  pl.BlockSpec((1, HEAD_V), lambda h, c: (0, 0))],
        out_specs=[head_v, pl.BlockSpec((1, per, HEAD_V, HEAD_K), lambda h, c: (h, c, 0, 0)), head_v],
        out_shape=[jax.ShapeDtypeStruct((s, VAL_W), BF16),
                   jax.ShapeDtypeStruct((HEADS, nc, HEAD_V, HEAD_K), BF16), jax.ShapeDtypeStruct((s, VAL_W), BF16)],
        scratch_shapes=[pltpu.VMEM((HEAD_V, HEAD_K), F32), pltpu.VMEM((per, HEAD_V, HEAD_K), F32)], name=name,
        compiler_params=_cparams(("parallel", "arbitrary")),
    )(proj, proj, proj, proj, wfg, bfg, proj, gla_g)


def _gla_bwd(proj, wfg, bfg, states, doa, o, gla_g, dproj, name):
    s = proj.shape[0]
    nc = s // CHUNK
    per = _pick(nc, (8, 4, 2, 1))
    rows = per * CHUNK
    nblk = nc // per

    def rev(spec_fn):
        return lambda h, j: spec_fn(h, nblk - 1 - j)

    def body(q_ref, k_ref, v_ref, fz_ref, w_ref, b_ref, doa_ref, o_ref, r_ref, g_ref,
             st_ref, prev_ref, _, dq_ref, dk_ref, dv_ref, dfg_ref, db_ref, dr_ref, dg_ref,
             carry, gt_all, dkd_all, dgg_all):
        j = pl.program_id(1)

        @pl.when(j == 0)
        def _():
            carry[...] = jnp.zeros_like(carry)
            db_ref[...] = jnp.zeros_like(db_ref)

        @pl.when((j == 0) & (pl.program_id(0) == 0))
        def _():
            dg_ref[...] = jnp.zeros_like(dg_ref)

        ov = o_ref[...].astype(F32)
        rstd = lax.rsqrt(jnp.mean(ov * ov, axis=-1, keepdims=True) + EPS)
        ohat = ov * rstd
        rv = r_ref[...].astype(F32)
        sg = _sigmoid(rv)
        doav = doa_ref[...].astype(F32)
        gv = g_ref[...]
        dr_ref[...] = (doav * ohat * gv * (sg * (1.0 + rv * (1.0 - sg)))).astype(dr_ref.dtype)
        don = doav * (rv * sg)
        dg_ref[...] += jnp.broadcast_to(jnp.sum(don * ohat, axis=0, keepdims=True), dg_ref.shape)
        dohat = don * gv
        do = (rstd * (dohat - ohat * jnp.mean(dohat * ohat, axis=-1, keepdims=True))).astype(BF16)

        fg, dec, gammas = _block_decay(fz_ref[...], w_ref[...], b_ref[...])
        kd = k_ref[...].astype(F32) * dec
        kd16 = kd.astype(BF16)
        qs = (q_ref[...].astype(F32) * Q_SCALE).astype(BF16)
        for i in range(per):
            sl = slice(i * CHUNK, (i + 1) * CHUNK)
            gt_all[i] = lax.dot_general(do[sl], qs[sl], _TN, preferred_element_type=F32)
        back = carry[...]
        for i in reversed(range(per)):
            gt = back + gt_all[i]
            gt_all[i] = gt
            back = gt * gammas[i]
        carry[...] = back
        has_prev = (j < nblk - 1).astype(F32)
        for i in range(per):
            sl = slice(i * CHUNK, (i + 1) * CHUNK)
            gt = gt_all[i]
            gt16 = gt.astype(BF16)
            dq_ref[sl, :] = (jnp.dot(do[sl], st_ref[0, i], preferred_element_type=F32) * Q_SCALE).astype(dq_ref.dtype)
            dkd_all[sl, :] = jnp.dot(v_ref[sl, :], gt16, preferred_element_type=F32)
            dv_ref[sl, :] = lax.dot_general(kd16[sl], gt16, _NT, preferred_element_type=F32).astype(dv_ref.dtype)
            if i > 0:
                st_prev = st_ref[0, i - 1].astype(F32)
            else:
                st_prev = prev_ref[0, 0].astype(F32) * has_prev
            dgamma = jnp.sum(gt * st_prev, axis=0, keepdims=True)
            dgg_all[sl, :] = jnp.broadcast_to(dgamma * gammas[i], (CHUNK, HEAD_K))
        dkd = dkd_all[...]
        dk_ref[...] = (dkd * dec).astype(dk_ref.dtype)
        e = dkd * kd
        dla = dgg_all[...] + (_chunk_cumsum(e) - e)
        dfg = dla * (1.0 / TAU) * _sigmoid(-fg)
        dfg_ref[...] = dfg.astype(dfg_ref.dtype)
        db_ref[...] += jnp.broadcast_to(jnp.sum(dfg, axis=0, keepdims=True), db_ref.shape)

    q_spec, k_spec, v_spec, fz_spec, w_spec, b_spec = _gla_specs(rows)
    q_spec, k_spec, v_spec, fz_spec = [
        pl.BlockSpec(sp.block_shape, rev(sp.index_map)) for sp in (q_spec, k_spec, v_spec, fz_spec)]
    do_spec = pl.BlockSpec((rows, HEAD_V), lambda h, j: (nblk - 1 - j, h))
    st_spec = pl.BlockSpec((1, per, HEAD_V, HEAD_K), lambda h, j: (h, nblk - 1 - j, 0, 0))
    prev_spec = pl.BlockSpec((1, 1, HEAD_V, HEAD_K),
                             lambda h, j: (h, jnp.maximum((nblk - 1 - j) * per - 1, 0), 0, 0))
    key_out = pl.BlockSpec((rows, HEAD_K), lambda h, j: (nblk - 1 - j, h))
    r_spec = pl.BlockSpec((rows, HEAD_V), lambda h, j: (nblk - 1 - j, SEG_R * (D_MODEL // HEAD_V) + h))
    return pl.pallas_call(
        body, grid=(HEADS, nblk),
        in_specs=[q_spec, k_spec, v_spec, fz_spec, w_spec, b_spec,
                  do_spec, do_spec, r_spec, pl.BlockSpec((1, HEAD_V), lambda h, j: (0, 0)),
                  st_spec, prev_spec, _ANY],
        out_specs=[key_out, key_out, do_spec, key_out, pl.BlockSpec((8, HEAD_K), lambda h, j: (0, h)),
                   r_spec, pl.BlockSpec((8, HEAD_V), lambda h, j: (0, 0))],
        out_shape=[jax.ShapeDtypeStruct((s, KEY_W), BF16), jax.ShapeDtypeStruct((s, KEY_W), BF16),
                   jax.ShapeDtypeStruct((s, VAL_W), BF16), jax.ShapeDtypeStruct((s, KEY_W), BF16),
                   jax.ShapeDtypeStruct((8, KEY_W), F32), jax.ShapeDtypeStruct(dproj.shape, BF16),
                   jax.ShapeDtypeStruct((8, HEAD_V), F32)],
        input_output_aliases={12: 5},
        scratch_shapes=[pltpu.VMEM((HEAD_V, HEAD_K), F32), pltpu.VMEM((per, HEAD_V, HEAD_K), F32),
                        pltpu.VMEM((rows, HEAD_K), F32), pltpu.VMEM((rows, HEAD_K), F32)], name=name,
        compiler_params=_cparams(("arbitrary", "arbitrary")),
    )(proj, proj, proj, proj, wfg, bfg, doa, o, proj, gla_g, states, states, dproj)


def _place_qkv(dq, dk, dv, dproj, name):
    s = dq.shape[0]
    tm = _pick(s, (512, 256))

    def body(dq_ref, dk_ref, dv_ref, _, o_ref):
        o_ref[:, 0:KEY_W] = dq_ref[...]
        o_ref[:, KEY_W:2 * KEY_W] = dk_ref[...]
        o_ref[:, 2 * KEY_W:2 * KEY_W + VAL_W] = dv_ref[...]

    def rows(width):
        return pl.BlockSpec((tm, width), lambda i: (i, 0))

    return pl.pallas_call(
        body, grid=(s // tm,), in_specs=[rows(KEY_W), rows(KEY_W), rows(VAL_W), _ANY],
        out_specs=rows(2 * KEY_W + VAL_W), out_shape=jax.ShapeDtypeStruct(dproj.shape, BF16),
        input_output_aliases={3: 0}, name=name, compiler_params=_cparams(("parallel",)),
    )(dq, dk, dv, dproj)


def _seg(tm, seg):
    return pl.BlockSpec((tm, D_MODEL), lambda i: (i, seg))


HALO = SUBLANES_BF16


def _shift_down(u, p1, p2, n, rows):
    rolled = pltpu.roll(u, n, 0)
    if n == 1:
        return jnp.where(rows == 0, p1, rolled)
    return jnp.where(rows == 0, p2, jnp.where(rows == 1, p1, rolled))


def _shift_up(u, n1, n2, n, rows, tm):
    rolled = pltpu.roll(u, tm - n, 0)
    if n == 1:
        return jnp.where(rows == tm - 1, n1, rolled)
    return jnp.where(rows == tm - 2, n1, jnp.where(rows == tm - 1, n2, rolled))


def _conv_terms(gc_ref, cx_ref, gcp_ref, cxp_ref, tm):
    i = pl.program_id(0)
    u = gc_ref[...].astype(F32) * cx_ref[...].astype(F32)
    up = gcp_ref[...].astype(F32) * cxp_ref[...].astype(F32) * (i > 0).astype(F32)
    rows = lax.broadcasted_iota(jnp.int32, (tm, 1), 0)
    u1 = _shift_down(u, up[HALO - 1:HALO, :], up[HALO - 2:HALO - 1, :], 1, rows)
    u2 = _shift_down(u, up[HALO - 1:HALO, :], up[HALO - 2:HALO - 1, :], 2, rows)
    return u, u1, u2, rows


def _prev_halo(tm, seg):
    return pl.BlockSpec((HALO, D_MODEL), lambda i: (jnp.maximum(i * (tm // HALO) - 1, 0), seg))


def _conv_project(proj, w, b, w_ob, name):
    s = proj.shape[0]
    tm = _pick(s, (512, 256))

    def body(gbi_ref, gc_ref, cx_ref, gcp_ref, cxp_ref, w_ref, b_ref, wob_ref, cb_ref, yb_ref):
        u, u1, u2, _ = _conv_terms(gc_ref, cx_ref, gcp_ref, cxp_ref, tm)
        conv = w_ref[0:1, :] * u2 + w_ref[1:2, :] * u1 + w_ref[2:3, :] * u + b_ref[...]
        cb = (gbi_ref[...].astype(F32) * conv).astype(BF16)
        cb_ref[...] = cb
        yb_ref[...] = jnp.dot(cb, wob_ref[...], preferred_element_type=F32).astype(BF16)

    row = pl.BlockSpec((tm, D_MODEL), lambda i: (i, 0))
    return pl.pallas_call(
        body, grid=(s // tm,),
        in_specs=[_seg(tm, SEG_GBI), _seg(tm, SEG_GCI), _seg(tm, SEG_CX),
                  _prev_halo(tm, SEG_GCI), _prev_halo(tm, SEG_CX),
                  pl.BlockSpec((3, D_MODEL), lambda i: (0, 0)), pl.BlockSpec((1, D_MODEL), lambda i: (0, 0)),
                  pl.BlockSpec((D_MODEL, D_MODEL), lambda i: (0, 0))],
        out_specs=[row, row],
        out_shape=[jax.ShapeDtypeStruct((s, D_MODEL), BF16), jax.ShapeDtypeStruct((s, D_MODEL), BF16)],
        name=name, compiler_params=_cparams(("parallel",)),
    )(proj, proj, proj, proj, proj, w, b, w_ob)


def _conv_bwd(dcb, proj, w, b, dproj, name):
    s = proj.shape[0]
    tm = _pick(s, (512, 256))
    nt = s // tm

    def body(dcb_ref, gbi_ref, gc_ref, cx_ref, gcp_ref, cxp_ref, dcbn_ref, gbin_ref, w_ref, b_ref, _,
             d3_ref, dwb_ref):
        i = pl.program_id(0)

        @pl.when(i == 0)
        def _():
            dwb_ref[...] = jnp.zeros_like(dwb_ref)

        u, u1, u2, rows = _conv_terms(gc_ref, cx_ref, gcp_ref, cxp_ref, tm)
        w0, w1, w2 = w_ref[0:1, :], w_ref[1:2, :], w_ref[2:3, :]
        conv = w0 * u2 + w1 * u1 + w2 * u + b_ref[...]
        dcbv = dcb_ref[...].astype(F32)
        gbi = gbi_ref[...].astype(F32)
        dconv = dcbv * gbi
        dnext = dcbn_ref[...].astype(F32) * gbin_ref[...].astype(F32) * (i < nt - 1).astype(F32)
        dc1 = _shift_up(dconv, dnext[0:1, :], dnext[1:2, :], 1, rows, tm)
        dc2 = _shift_up(dconv, dnext[0:1, :], dnext[1:2, :], 2, rows, tm)
        du = w2 * dconv + w1 * dc1 + w0 * dc2
        d3_ref[:, 0:D_MODEL] = (dcbv * conv).astype(d3_ref.dtype)
        d3_ref[:, D_MODEL:2 * D_MODEL] = (du * cx_ref[...].astype(F32)).astype(d3_ref.dtype)
        d3_ref[:, 2 * D_MODEL:3 * D_MODEL] = (du * gc_ref[...].astype(F32)).astype(d3_ref.dtype)
        dwb_ref[0:1, :] += jnp.sum(dconv * u2, axis=0, keepdims=True)
        dwb_ref[1:2, :] += jnp.sum(dconv * u1, axis=0, keepdims=True)
        dwb_ref[2:3, :] += jnp.sum(dconv * u, axis=0, keepdims=True)
        dwb_ref[3:4, :] += jnp.sum(dconv, axis=0, keepdims=True)

    def next_halo(seg_fn):
        return pl.BlockSpec((HALO, D_MODEL), lambda i: (jnp.minimum((i + 1) * (tm // HALO), s // HALO - 1), seg_fn))

    return pl.pallas_call(
        body, grid=(nt,),
        in_specs=[pl.BlockSpec((tm, D_MODEL), lambda i: (i, 0)),
                  _seg(tm, SEG_GBI), _seg(tm, SEG_GCI), _seg(tm, SEG_CX),
                  _prev_halo(tm, SEG_GCI), _prev_halo(tm, SEG_CX),
                  next_halo(0), next_halo(SEG_GBI),
                  pl.BlockSpec((3, D_MODEL), lambda i: (0, 0)), pl.BlockSpec((1, D_MODEL), lambda i: (0, 0)), _ANY],
        out_specs=[pl.BlockSpec((tm, 3 * D_MODEL), lambda i: (i, SEG_GBI // 3)),
                   pl.BlockSpec((8, D_MODEL), lambda i: (0, 0))],
        out_shape=[jax.ShapeDtypeStruct(dproj.shape, BF16), jax.ShapeDtypeStruct((8, D_MODEL), F32)],
        input_output_aliases={10: 0}, name=name, compiler_params=_cparams(("arbitrary",)),
    )(dcb, proj, proj, proj, proj, proj, dcb, proj, w, b, dproj)


def _mix_project(proj, ya, yb, w_o, x, norm_g, name):
    s = proj.shape[0]
    tm = _pick(s, (512, 256))

    def body(ga_ref, gb_ref, ya_ref, yb_ref, w_ref, x_ref, g_ref, mix_ref, x1_ref, h2_ref):
        mix = (_sigmoid(ga_ref[...].astype(F32)) * ya_ref[...].astype(F32)
               + _sigmoid(gb_ref[...].astype(F32)) * yb_ref[...].astype(F32)).astype(BF16)
        mix_ref[...] = mix
        x1 = x_ref[...] + jnp.dot(mix, w_ref[...], preferred_element_type=F32)
        x1_ref[...] = x1
        rstd = lax.rsqrt(jnp.mean(x1 * x1, axis=-1, keepdims=True) + EPS)
        h2_ref[...] = (x1 * rstd * g_ref[...]).astype(BF16)

    row = pl.BlockSpec((tm, D_MODEL), lambda i: (i, 0))
    whole = pl.BlockSpec((D_MODEL, D_MODEL), lambda i: (0, 0))
    return pl.pallas_call(
        body, grid=(s // tm,),
        in_specs=[_seg(tm, SEG_GA), _seg(tm, SEG_GB), row, row, whole, row, pl.BlockSpec((1, D_MODEL), lambda i: (0, 0))],
        out_specs=[row, row, row],
        out_shape=[jax.ShapeDtypeStruct((s, D_MODEL), BF16), jax.ShapeDtypeStruct((s, D_MODEL), F32),
                   jax.ShapeDtypeStruct((s, D_MODEL), BF16)],
        name=name, compiler_params=_cparams(("parallel",)),
    )(proj, proj, ya, yb, w_o, x, norm_g)


def _mix_bwd(dmix, proj, ya, yb, name):
    s = proj.shape[0]
    tm = _pick(s, (512, 256))

    def body(dm_ref, ga_ref, gb_ref, ya_ref, yb_ref, dg_ref, dya_ref, dyb_ref):
        dm = dm_ref[...].astype(F32)
        sa = _sigmoid(ga_ref[...].astype(F32))
        sb = _sigmoid(gb_ref[...].astype(F32))
        dg_ref[:, 0:D_MODEL] = (dm * ya_ref[...].astype(F32) * sa * (1.0 - sa)).astype(dg_ref.dtype)
        dg_ref[:, D_MODEL:2 * D_MODEL] = (dm * yb_ref[...].astype(F32) * sb * (1.0 - sb)).astype(dg_ref.dtype)
        dya_ref[...] = (dm * sa).astype(dya_ref.dtype)
        dyb_ref[...] = (dm * sb).astype(dyb_ref.dtype)

    row = pl.BlockSpec((tm, D_MODEL), lambda i: (i, 0))
    return pl.pallas_call(
        body, grid=(s // tm,), in_specs=[row, _seg(tm, SEG_GA), _seg(tm, SEG_GB), row, row],
        out_specs=[pl.BlockSpec((tm, 2 * D_MODEL), lambda i: (i, SEG_GA // 2)), row, row],
        out_shape=[jax.ShapeDtypeStruct((s, PROJ_W), BF16), jax.ShapeDtypeStruct((s, D_MODEL), BF16),
                   jax.ShapeDtypeStruct((s, D_MODEL), BF16)],
        name=name, compiler_params=_cparams(("parallel",)),
    )(dmix, proj, proj, ya, yb)


def _swiglu_project(gu, w_down, x1, norm_g, name):
    s = gu.shape[0]
    tm = _pick(s, (1024, 512, 256))
    tk = _pick(FFN, (1408, 256))
    nk = FFN // tk
    has_norm = norm_g is not None

    def body(gate_ref, up_ref, w_ref, x_ref, *rest):
        outs = rest[has_norm:]
        hid_ref, x2_ref, acc_ref = outs[0], outs[1], rest[-1]
        kk = pl.program_id(1)
        gate = gate_ref[...].astype(F32)
        hid = (gate * _sigmoid(gate) * up_ref[...].astype(F32)).astype(BF16)
        hid_ref[...] = hid
        part = jnp.dot(hid, w_ref[...], preferred_element_type=F32)

        @pl.when(kk == 0)
        def _():
            acc_ref[...] = part

        @pl.when((kk > 0) & (kk < nk - 1))
        def _():
            acc_ref[...] += part

        @pl.when(kk == nk - 1)
        def _():
            x2 = acc_ref[...] + part + x_ref[...]
            x2_ref[...] = x2
            if has_norm:
                rstd = lax.rsqrt(jnp.mean(x2 * x2, axis=-1, keepdims=True) + EPS)
                outs[2][...] = (x2 * rstd * rest[0][...]).astype(BF16)

    assert nk > 1
    row = pl.BlockSpec((tm, D_MODEL), lambda i, kk: (i, 0))
    out = pl.pallas_call(
        body, grid=(s // tm, nk),
        in_specs=[pl.BlockSpec((tm, tk), lambda i, kk: (i, kk)), pl.BlockSpec((tm, tk), lambda i, kk: (i, nk + kk)),
                  pl.BlockSpec((tk, D_MODEL), lambda i, kk: (kk, 0)), row]
        + ([pl.BlockSpec((1, D_MODEL), lambda i, kk: (0, 0))] if has_norm else []),
        out_specs=[pl.BlockSpec((tm, tk), lambda i, kk: (i, kk)), row] + ([row] if has_norm else []),
        out_shape=[jax.ShapeDtypeStruct((s, FFN), BF16), jax.ShapeDtypeStruct((s, D_MODEL), F32)]
        + ([jax.ShapeDtypeStruct((s, D_MODEL), BF16)] if has_norm else []),
        scratch_shapes=[pltpu.VMEM((tm, D_MODEL), F32)], name=name,
        compiler_params=pltpu.CompilerParams(dimension_semantics=("parallel", "arbitrary"),
                                             vmem_limit_bytes=VMEM_LIMIT_WIDE),
    )(gu, gu, w_down, x1, *((norm_g,) if has_norm else ()))
    return out[0], out[1], (out[2] if has_norm else None)


def _swiglu_bwd(dhid, gu, name):
    s = gu.shape[0]
    tm = _pick(s, (256,))

    def body(dh_ref, gu_ref, o_ref):
        gate = gu_ref[:, 0:FFN].astype(F32)
        up = gu_ref[:, FFN:2 * FFN].astype(F32)
        dh = dh_ref[...].astype(F32)
        sg = _sigmoid(gate)
        o_ref[:, 0:FFN] = (dh * up * (sg * (1.0 + gate * (1.0 - sg)))).astype(o_ref.dtype)
        o_ref[:, FFN:2 * FFN] = (dh * gate * sg).astype(o_ref.dtype)

    wide = pl.BlockSpec((tm, 2 * FFN), lambda i: (i, 0))
    return pl.pallas_call(
        body, grid=(s // tm,), in_specs=[pl.BlockSpec((tm, FFN), lambda i: (i, 0)), wide], out_specs=wide,
        out_shape=jax.ShapeDtypeStruct((s, 2 * FFN), BF16), name=name, compiler_params=_cparams(("parallel",)),
    )(dhid, gu)


def _adamw_math(w, g, m, v):
    m2 = ADAM_B1 * m + (1.0 - ADAM_B1) * g
    v2 = ADAM_B2 * v + (1.0 - ADAM_B2) * (g * g)
    m_hat = m2 / (1.0 - ADAM_B1 ** ADAM_STEP)
    v_hat = v2 / (1.0 - ADAM_B2 ** ADAM_STEP)
    delta = -ADAM_LR * (m_hat / (jnp.sqrt(v_hat) + ADAM_EPS) + ADAM_WD * w)
    return delta, m2, v2


def _adamw(w, g, m, v, name):
    shape = w.shape
    cols = shape[-1]
    rows = int(np.prod(shape[:-1])) if len(shape) > 1 else 1
    w2, g2, m2, v2 = [t.reshape(rows, cols) for t in (w, g, m, v)]
    tr = _pick(rows, (512, 352, 256)) if rows % 8 == 0 else rows

    def body(w_ref, g_ref, m_ref, v_ref, d_ref, nm_ref, nv_ref):
        d, nm, nv = _adamw_math(w_ref[...], g_ref[...], m_ref[...], v_ref[...])
        d_ref[...] = d
        nm_ref[...] = nm
        nv_ref[...] = nv

    blk = pl.BlockSpec((tr, cols), lambda i: (i, 0))
    out = pl.pallas_call(
        body, grid=(rows // tr,), in_specs=[blk] * 4, out_specs=[blk] * 3,
        out_shape=[jax.ShapeDtypeStruct((rows, cols), F32)] * 3, name=name,
        compiler_params=_cparams(("parallel",)),
    )(w2, g2, m2, v2)
    return [t.reshape(shape) for t in out]


def _pair_sum(g2, recv, core, name, after=()):
    _, nchip, r, c = g2.shape
    tr = _pick(r, (512,))

    def body(core_ref, a_ref, b_ref, *rest):
        o_ref = rest[-1]
        o_ref[...] = (a_ref[...].astype(F32) + b_ref[...].astype(F32)).astype(o_ref.dtype)

    grid_spec = pltpu.PrefetchScalarGridSpec(
        num_scalar_prefetch=1, grid=(nchip, r // tr),
        in_specs=[pl.BlockSpec((None, None, tr, c), lambda k, i, cr: (cr[0], k, i, 0)),
                  pl.BlockSpec((None, tr, c), lambda k, i, cr: (k, i, 0))] + [_after_spec(t) for t in after],
        out_specs=pl.BlockSpec((None, tr, c), lambda k, i, cr: (k, i, 0)))
    return pl.pallas_call(
        body, grid_spec=grid_spec, out_shape=jax.ShapeDtypeStruct((nchip, r, c), BF16), name=name,
        compiler_params=_cparams(("parallel", "parallel")),
    )(core, g2, recv, *after)


def _chip_sum(a, recv, chip, name, after=()):
    _, r, c = a.shape
    tr = _pick(r, (512,))

    def body(chip_ref, a_ref, b_ref, *rest):
        o_ref = rest[-1]
        o_ref[...] = ((a_ref[...].astype(F32) + b_ref[0].astype(F32)) + b_ref[1].astype(F32)) + b_ref[2].astype(F32)

    grid_spec = pltpu.PrefetchScalarGridSpec(
        num_scalar_prefetch=1, grid=(r // tr,),
        in_specs=[pl.BlockSpec((None, tr, c), lambda i, cr: (cr[0], i, 0)),
                  pl.BlockSpec((3, tr, c), lambda i, cr: (0, i, 0))] + [_after_spec(t) for t in after],
        out_specs=pl.BlockSpec((tr, c), lambda i, cr: (i, 0)))
    return pl.pallas_call(
        body, grid_spec=grid_spec, out_shape=jax.ShapeDtypeStruct((r, c), F32), name=name,
        compiler_params=_cparams(("parallel",)),
    )(chip, a, recv, *after)


def _sum_devices(parts, name):
    n, r, c = parts.shape

    def body(p_ref, o_ref):
        acc = p_ref[0]
        for d in range(1, n):
            acc = acc + p_ref[d]
        o_ref[...] = acc

    return pl.pallas_call(
        body, out_shape=jax.ShapeDtypeStruct((r, c), F32), name=name,
        in_specs=[pl.BlockSpec(memory_space=pltpu.VMEM)], out_specs=pl.BlockSpec(memory_space=pltpu.VMEM),
    )(parts)


def _lane_iota():
    return lax.broadcasted_iota(jnp.int32, (1, LANES), 1)


def _tiles_up(tiles, s, lane):
    rolled = [pltpu.roll(t, s, 1) for t in tiles]
    zero = jnp.zeros_like(tiles[0])
    return [jnp.where(lane < s, p, c) for p, c in zip([zero] + rolled, rolled + [zero])]


def _tiles_down(tiles, s, lane):
    back = (LANES - s) % LANES
    rolled = [pltpu.roll(t, back, 1) for t in tiles]
    zero = jnp.zeros_like(tiles[0])
    return [jnp.where(lane < LANES - s, c, n) for c, n in zip(rolled, rolled[1:] + [zero])]


def _window_cols(prm, w_in, w_gate, w_up, l, name, after=None):
    tm = 256
    n_in, n_gu = WIN_IN // LANES, FFN_SHARD // LANES + 1

    def body(prm_ref, win_ref, g_ref, u_ref, *rest):
        out_ref, gu_ref, scr_in, scr_gu = rest[-4:]
        lane = _lane_iota()
        s_main, lo, hi, s_code, clo, chi, code_hi, s_gu = [prm_ref[i] for i in range(8)]
        scr_in[:, D_MODEL:WIN_IN] = jnp.zeros((tm, WIN_IN - D_MODEL), F32)
        scr_in[:, 0:W_IN_SHARD] = win_ref[...]

        def keep(t, a, b):
            col = lane + t * LANES
            return jnp.where((col >= a) & (col < b), scr_in[:, t * LANES:(t + 1) * LANES], 0.0)

        main = _tiles_up([keep(t, lo, hi) for t in range(n_in)], s_main, lane)
        for t in range(n_in):
            out_ref[:, t * LANES:(t + 1) * LANES] = main[t].astype(BF16)
        low = _tiles_up([keep(0, clo, chi)], s_code, lane)[0]
        high = _tiles_up([keep(n_in - 2, clo, chi), keep(n_in - 1, clo, chi)], s_code, lane)[1]
        out_ref[:, A_FZ:A_FZ + LANES] = jnp.where(code_hi == 1, high, low).astype(BF16)
        for ref, base in ((g_ref, 0), (u_ref, WIN_GU)):
            scr_gu[:, (n_gu - 1) * LANES:n_gu * LANES] = jnp.zeros((tm, LANES), F32)
            scr_gu[:, 0:FFN_SHARD] = ref[...]
            moved = _tiles_up([scr_gu[:, t * LANES:(t + 1) * LANES] for t in range(n_gu)], s_gu, lane)
            for t in range(n_gu + 1):
                gu_ref[:, base + t * LANES:base + (t + 1) * LANES] = moved[t].astype(BF16)

    after_args = [] if after is None else [after]
    grid_spec = pltpu.PrefetchScalarGridSpec(
        num_scalar_prefetch=1, grid=(D_MODEL // tm,),
        in_specs=[pl.BlockSpec((None, tm, W_IN_SHARD), lambda i, p: (l, i, 0)),
                  pl.BlockSpec((None, tm, FFN_SHARD), lambda i, p: (l, i, 0)),
                  pl.BlockSpec((None, tm, FFN_SHARD), lambda i, p: (l, i, 0))] + [_after_spec(t) for t in after_args],
        out_specs=[pl.BlockSpec((tm, IN_COLS), lambda i, p: (i, 0)), pl.BlockSpec((tm, GU_COLS), lambda i, p: (i, 0))],
        scratch_shapes=[pltpu.VMEM((tm, WIN_IN), F32), pltpu.VMEM((tm, n_gu * LANES), F32)])
    return pl.pallas_call(
        body, grid_spec=grid_spec, name=name, compiler_params=_cparams(("parallel",)),
        out_shape=[jax.ShapeDtypeStruct((D_MODEL, IN_COLS), BF16), jax.ShapeDtypeStruct((D_MODEL, GU_COLS), BF16)],
    )(prm, w_in, w_gate, w_up, *after_args)


def _gu_width(j):
    return min(WIN_GU, FFN - T_GU[j] * LANES)


def _after_spec(t):
    tile = (SUBLANES_BF16 if t.dtype == BF16 else 8, LANES)
    return pl.BlockSpec((None,) * (t.ndim - 2) + tile, lambda *_: (0,) * t.ndim)


def _assemble_in(a_all, tail_all, name, after=()):
    tm = D_MODEL // N_DEV

    def body(a_ref, t_ref, *rest):
        win_ref, tail_ref = rest[-2:]
        tail_ref[...] = t_ref[...]
        win_ref[...] = jnp.zeros_like(win_ref)
        code = a_ref[0, :, A_FZ:A_FZ + LANES]
        for j in range(N_DEV):
            c0 = T_IN[j] * LANES
            win_ref[:, c0:c0 + WIN_IN] += a_ref[j, :, 0:WIN_IN]
            if j > 0:
                code = code + a_ref[j, :, A_FZ:A_FZ + LANES]
        win_ref[:, FZ_COL:PROJ_W] = code

    return pl.pallas_call(
        body, grid=(N_DEV,),
        in_specs=[pl.BlockSpec((N_DEV, tm, IN_COLS), lambda i: (0, i, 0)),
                  pl.BlockSpec((None,) + tail_all.shape[1:], lambda i: (i, 0, 0))] + [_after_spec(t) for t in after],
        out_specs=[pl.BlockSpec((tm, PROJ_W), lambda i: (i, 0)),
                   pl.BlockSpec((None,) + tail_all.shape[1:], lambda i: (i, 0, 0))],
        out_shape=[jax.ShapeDtypeStruct((D_MODEL, PROJ_W), BF16), jax.ShapeDtypeStruct(tail_all.shape, BF16)],
        name=name, compiler_params=_cparams(("parallel",)),
    )(a_all, tail_all, *after)


def _assemble_rest(a_all, rows_all, name, after=()):
    tm = D_MODEL // N_DEV
    n_in = 2 + len(after)

    def body(*refs):
        a_ref, r_ref = refs[:2]
        wgu_ref, oa_ref, ob_ref, o_ref, down_ref = refs[n_in:]
        wgu_ref[...] = jnp.zeros_like(wgu_ref)
        for j in range(N_DEV):
            g0, width = T_GU[j] * LANES, _gu_width(j)
            wgu_ref[:, g0:g0 + width] += a_ref[j, :, 0:width]
            wgu_ref[:, FFN + g0:FFN + g0 + width] += a_ref[j, :, WIN_GU:WIN_GU + width]
        oa_ref[...] = r_ref[B_OA:B_OA + ROW_SHARD, :]
        ob_ref[...] = r_ref[B_OB:B_OB + ROW_SHARD, :]
        o_ref[...] = r_ref[B_O:B_O + ROW_SHARD, :]
        down_ref[...] = r_ref[B_DOWN:B_DOWN + FFN_SHARD, :]

    def rows(n):
        return pl.BlockSpec((n, D_MODEL), lambda i: (i, 0))

    square = jax.ShapeDtypeStruct((D_MODEL, D_MODEL), BF16)
    return pl.pallas_call(
        body, grid=(N_DEV,),
        in_specs=[pl.BlockSpec((N_DEV, tm, GU_COLS), lambda i: (0, i, 0)),
                  pl.BlockSpec((None, B_FG2, D_MODEL), lambda i: (i, 0, 0))] + [_after_spec(t) for t in after],
        out_specs=[pl.BlockSpec((tm, 2 * FFN), lambda i: (i, 0)),
                   rows(ROW_SHARD), rows(ROW_SHARD), rows(ROW_SHARD), rows(FFN_SHARD)],
        out_shape=[jax.ShapeDtypeStruct((D_MODEL, 2 * FFN), BF16),
                   square, square, square, jax.ShapeDtypeStruct((FFN, D_MODEL), BF16)],
        name=name, compiler_params=_cparams(("parallel",)),
    )(a_all, rows_all, *after)


def _grad_windows_in(d_in, name):
    tm = 256

    def body(din_ref, out_ref):
        for j in range(N_DEV):
            c0 = T_IN[j] * LANES
            out_ref[j & 1, j >> 1, :, 0:WIN_IN] = din_ref[:, c0:c0 + WIN_IN]
            out_ref[j & 1, j >> 1, :, A_FZ:IN_COLS] = din_ref[:, FZ_COL:PROJ_W]

    return pl.pallas_call(
        body, grid=(D_MODEL // tm,), in_specs=[pl.BlockSpec((tm, PROJ_W), lambda i: (i, 0))],
        out_specs=pl.BlockSpec((2, 4, tm, IN_COLS), lambda i: (0, 0, i, 0)),
        out_shape=jax.ShapeDtypeStruct((2, 4, D_MODEL, IN_COLS), BF16), name=name,
        compiler_params=_cparams(("parallel",)),
    )(d_in)


def _grad_windows_gu(d_gu, name):
    tm = 256

    def body(dgu_ref, out_ref):
        for j in range(N_DEV):
            g0, width = T_GU[j] * LANES, _gu_width(j)
            for half, base in ((0, 0), (FFN, WIN_GU)):
                out_ref[j & 1, j >> 1, :, base:base + width] = dgu_ref[:, half + g0:half + g0 + width]
                if width < WIN_GU:
                    out_ref[j & 1, j >> 1, :, base + width:base + WIN_GU] = jnp.zeros((tm, WIN_GU - width), BF16)

    return pl.pallas_call(
        body, grid=(D_MODEL // tm,), in_specs=[pl.BlockSpec((tm, 2 * FFN), lambda i: (i, 0))],
        out_specs=pl.BlockSpec((2, 4, tm, GU_COLS), lambda i: (0, 0, i, 0)),
        out_shape=jax.ShapeDtypeStruct((2, 4, D_MODEL, GU_COLS), BF16), name=name,
        compiler_params=_cparams(("parallel",)),
    )(d_gu)


def _final_windows(prm, chip, a, recv, params, l, kind, name, prev=None):
    tm = 128
    n_in, n_gu = WIN_IN // LANES, FFN_SHARD // LANES + 1
    widths = (W_IN_SHARD,) if kind == "in" else (FFN_SHARD, FFN_SHARD)
    cols = IN_COLS if kind == "in" else GU_COLS
    n_par = 3 * len(widths)
    prev = [] if prev is None else [t for group in prev for t in group]

    def body(prm_ref, chip_ref, a_ref, r_ref, *refs):
        ins, outs = refs[:n_par], refs[n_par + len(prev):]
        lane = _lane_iota()
        s_main, s_code, clo, chi, s_gu = [prm_ref[i] for i in (0, 3, 4, 5, 7)]

        def total(c0):
            sl = slice(c0, c0 + LANES)
            return ((a_ref[:, sl].astype(F32) + r_ref[0, :, sl].astype(F32)) + r_ref[1, :, sl].astype(F32)) \
                + r_ref[2, :, sl].astype(F32)

        if kind == "in":
            grads = _tiles_down([total(t * LANES) for t in range(n_in)], s_main, lane)
            code = pltpu.roll(total(A_FZ), (LANES - s_code) % LANES, 1)
            for t in (0, n_in - 2, n_in - 1):
                col = lane + t * LANES
                grads[t] = jnp.where((col >= clo) & (col < chi), code, grads[t])
            per_weight = [grads]
        else:
            per_weight = [_tiles_down([total(base + t * LANES) for t in range(n_gu + 1)], s_gu, lane)[:n_gu]
                          for base in (0, WIN_GU)]
        for k, (tiles, width) in enumerate(zip(per_weight, widths)):
            w_ref, m_ref, v_ref = ins[3 * k:3 * k + 3]
            g_ref, d_ref, nm_ref, nv_ref = outs[4 * k:4 * k + 4]
            for t, g in enumerate(tiles):
                n = min(LANES, width - t * LANES)
                sl = slice(t * LANES, t * LANES + n)
                g = g[:, 0:n]
                d, nm, nv = _adamw_math(w_ref[:, sl], g, m_ref[:, sl], v_ref[:, sl])
                g_ref[:, sl] = g
                d_ref[:, sl] = d
                nm_ref[:, sl] = nm
                nv_ref[:, sl] = nv

    def native(width):
        return pl.BlockSpec((None, tm, width), lambda i, p, c: (l, i, 0))

    in_specs = [pl.BlockSpec((None, tm, cols), lambda i, p, c: (c[0], i, 0)),
                pl.BlockSpec((3, tm, cols), lambda i, p, c: (0, i, 0))]
    in_specs += [native(wd) for wd in widths for _ in range(3)] + [_ANY] * len(prev)
    grid_spec = pltpu.PrefetchScalarGridSpec(
        num_scalar_prefetch=2, grid=(D_MODEL // tm,), in_specs=in_specs,
        out_specs=[native(wd) for wd in widths for _ in range(4)])
    out = pl.pallas_call(
        body, grid_spec=grid_spec, name=name, compiler_params=_cparams(("parallel",)),
        out_shape=[jax.ShapeDtypeStruct((DEPTH, D_MODEL, wd), F32) for wd in widths for _ in range(4)],
        input_output_aliases={4 + n_par + k: k for k in range(len(prev))},
    )(prm, chip, a, recv, *params, *prev)
    return [out[4 * k:4 * k + 4] for k in range(len(widths))]


def _me():
    return lax.axis_index("x"), lax.axis_index("y"), lax.axis_index("c")


_CHIP_FLIPS = ((1, 0), (0, 1), (1, 1))
_ANY = pl.BlockSpec(memory_space=pl.ANY)


def _comm_call(body, peers, out_shape, sems, name, args, collective_id):
    if collective_id is None:
        n_in = len(args)
        return pl.pallas_call(body, out_shape=out_shape, name=name, in_specs=[_ANY] * n_in,
                              out_specs=[_ANY] * len(out_shape), scratch_shapes=sems)(*args)

    def sequencer_body(*refs):
        barrier = pltpu.get_barrier_semaphore()
        targets = peers()
        for peer in targets:
            pl.semaphore_signal(barrier, inc=1, device_id=peer, device_id_type=MESH)
        pl.semaphore_wait(barrier, len(targets))
        body(*refs)

    sequencer = plsc.ScalarSubcoreMesh(axis_name="seq", num_cores=1)
    return pl.kernel(sequencer_body, out_type=out_shape, mesh=sequencer, scratch_types=sems, name=name,
                     compiler_params=pltpu.CompilerParams(collective_id=collective_id))(*args)


def _sibling_peer():
    x, y, cc = _me()
    return [(x, y, 1 - cc)]


def _chip_peers():
    x, y, cc = _me()
    return [(x ^ fx, y ^ fy, cc) for fx, fy in _CHIP_FLIPS]


def _all_gather(shards, name, collective_id=None):
    n = len(shards)
    split = [s.shape[0] % (2 * SUBLANES_BF16) == 0 for s in shards]

    def body(*refs):
        x_refs, out_refs = refs[:n], refs[n:2 * n]
        send_sems, recv_sems, local_sems = refs[2 * n:]
        x, y, cc = _me()
        me, sibling = (x, y, cc), (x, y, 1 - cc)
        near, far = [(x ^ 1, y), (x, y ^ 1)], (x ^ 1, y ^ 1)

        def copy(a, k, block, to, half=None, from_shard=False):
            px, py, pc = block
            slot = out_refs[a].at[4 * px + 2 * py + pc]
            if half is not None:
                rows = shards[a].shape[0] // 2
                slot = slot.at[pl.ds(half * rows, rows)]
            return pltpu.make_async_remote_copy(
                src_ref=x_refs[a] if from_shard else slot, dst_ref=slot,
                send_sem=send_sems.at[a, k], recv_sem=recv_sems.at[a, k], device_id=to, device_id_type=MESH)

        mine = [pltpu.make_async_copy(x_refs[a], out_refs[a].at[4 * x + 2 * y + cc], local_sems.at[a])
                for a in range(n)]
        for cp in mine:
            cp.start()
        sent = [copy(a, 0, me, sibling, from_shard=True) for a in range(n)]
        sent += [copy(a, 1 + j, me, (*chip, cc), from_shard=True) for j, chip in enumerate(near) for a in range(n)]
        sent += [copy(a, 3, me, (*far, cc), from_shard=True) for a in range(n) if not split[a]]
        for cp in sent:
            cp.start()

        def pass_on(cp):
            cp.start()
            sent.append(cp)

        for j, chip in enumerate(near):
            for a in range(n):
                copy(a, 1 + j, (*chip, cc), me).wait_recv()
                pass_on(copy(a, 4 + j, (*chip, cc), sibling))
                if split[a]:
                    pass_on(copy(a, 7 + j, (*chip, cc), (*near[1 - j], cc), half=j))
        for a in range(n):
            if split[a]:
                copy(a, 7, (*far, cc), me, half=0).wait_recv()
                copy(a, 8, (*far, cc), me, half=1).wait_recv()
            else:
                copy(a, 3, (*far, cc), me).wait_recv()
            pass_on(copy(a, 6, (*far, cc), sibling))
        for a in range(n):
            copy(a, 0, sibling, me).wait_recv()
            for j, chip in enumerate(near + [far]):
                copy(a, 4 + j, (*chip, 1 - cc), me).wait_recv()
        for cp in sent:
            cp.wait_send()
        for cp in mine:
            cp.wait()

    return _comm_call(
        body, lambda: _sibling_peer() + _chip_peers(),
        [jax.ShapeDtypeStruct((N_DEV,) + s.shape, s.dtype) for s in shards],
        [pltpu.SemaphoreType.DMA((n, 9)), pltpu.SemaphoreType.DMA((n, 9)), pltpu.SemaphoreType.DMA((n,))],
        name, shards, collective_id)


def _send_to_sibling(parts, name, collective_id=None):
    n = len(parts)

    def body(*refs):
        g_refs, out_refs = refs[:n], refs[n:2 * n]
        send_sems, recv_sems = refs[2 * n:]
        x, y, cc = _me()
        copies = [pltpu.make_async_remote_copy(
            src_ref=g_refs[a].at[1 - cc], dst_ref=out_refs[a], send_sem=send_sems.at[a], recv_sem=recv_sems.at[a],
            device_id=(x, y, 1 - cc), device_id_type=MESH) for a in range(n)]
        for cp in copies:
            cp.start()
        for cp in copies:
            cp.wait()

    return _comm_call(
        body, _sibling_peer, [jax.ShapeDtypeStruct(p.shape[1:], p.dtype) for p in parts],
        [pltpu.SemaphoreType.DMA((n,)), pltpu.SemaphoreType.DMA((n,))], name, parts, collective_id)


def _send_to_chips(parts, name, collective_id=None):
    n = len(parts)

    def body(*refs):
        a_refs, out_refs = refs[:n], refs[n:2 * n]
        send_sems, recv_sems = refs[2 * n:]
        x, y, cc = _me()
        copies = []
        for k, (fx, fy) in enumerate(_CHIP_FLIPS):
            px, py = x ^ fx, y ^ fy
            for a in range(n):
                copies.append(pltpu.make_async_remote_copy(
                    src_ref=a_refs[a].at[2 * px + py], dst_ref=out_refs[a].at[k], send_sem=send_sems.at[a, k],
                    recv_sem=recv_sems.at[a, k], device_id=(px, py, cc), device_id_type=MESH))
                copies[-1].start()
        for cp in copies:
            cp.wait()

    return _comm_call(
        body, _chip_peers, [jax.ShapeDtypeStruct((3,) + p.shape[1:], p.dtype) for p in parts],
        [pltpu.SemaphoreType.DMA((n, 3)), pltpu.SemaphoreType.DMA((n, 3))], name, parts, collective_id)


def _pack_rows(w_oa, w_ob, w_o, w_down, w_fg2, conv_w, l):
    conv_bits = lax.bitcast_convert_type(conv_w[l].reshape(-1), BF16).reshape(1, -1)
    tail = jnp.concatenate([w_fg2[l].astype(BF16).reshape(1, D_MODEL),
                            jnp.pad(conv_bits, ((0, 0), (0, D_MODEL - conv_bits.shape[1])))], axis=0)
    tail = jnp.pad(tail, ((0, B_ROWS - B_FG2 - tail.shape[0]), (0, 0)))
    rows = jnp.concatenate([w_oa[l].astype(BF16), w_ob[l].astype(BF16), w_o[l].astype(BF16),
                            w_down[l].astype(BF16)], axis=0)
    return rows, tail


def _unpack_tail(tail):
    w_fg2 = tail[:, 0, :].reshape(N_DEV, RANK, KEY_W // N_DEV).transpose(1, 0, 2).reshape(RANK, KEY_W)
    conv_bits = tail[:, B_CONV - B_FG2, :2 * 3 * ROW_SHARD].reshape(N_DEV, 3 * ROW_SHARD, 2)
    conv_w = lax.bitcast_convert_type(conv_bits, F32).reshape(N_DEV, 3, ROW_SHARD)
    return jnp.pad(w_fg2, ((0, LANES - RANK), (0, 0))), conv_w.transpose(1, 0, 2).reshape(3, D_MODEL)


def _by_core_chip(t):
    return t.reshape((2, 2, 2) + t.shape[1:]).transpose((2, 0, 1) + tuple(range(3, t.ndim + 2))).reshape(
        (2, 4) + t.shape[1:])


def _grad_rows(g):
    fg2 = g["w_fg2"][:RANK].reshape(RANK, N_DEV, KEY_W // N_DEV).transpose(1, 0, 2).reshape(N_DEV, 1, D_MODEL)
    conv = g["conv_w"].astype(BF16).reshape(3, N_DEV, ROW_SHARD).transpose(1, 0, 2).reshape(N_DEV, 1, 3 * ROW_SHARD)
    tail = jnp.concatenate([fg2, jnp.pad(conv, ((0, 0), (0, 0), (0, D_MODEL - 3 * ROW_SHARD)))], axis=1)
    tail = jnp.pad(tail, ((0, 0), (0, B_ROWS - B_FG2 - 2), (0, 0)))
    parts = [g["w_oa"].reshape(N_DEV, ROW_SHARD, D_MODEL), g["w_ob"].reshape(N_DEV, ROW_SHARD, D_MODEL),
             g["w_o"].reshape(N_DEV, ROW_SHARD, D_MODEL), g["w_down"].reshape(N_DEV, FFN_SHARD, D_MODEL), tail]
    return _by_core_chip(jnp.concatenate(parts, axis=1))


def _ungrad_rows(gs):
    return dict(w_oa=gs[B_OA:B_OA + ROW_SHARD], w_ob=gs[B_OB:B_OB + ROW_SHARD], w_o=gs[B_O:B_O + ROW_SHARD],
                w_ffn_down=gs[B_DOWN:B_DOWN + FFN_SHARD], w_fg2=gs[B_FG2].reshape(RANK, KEY_W // N_DEV),
                conv_w=gs[B_CONV, :3 * ROW_SHARD].reshape(3, ROW_SHARD))


def _layer_fwd(x, h, p, l, next_norm_g=None):
    tag = f"l{l}_"
    if h is None:
        h = _rmsnorm_fwd(x, p["norm1_g"], tag + "norm1")
    proj = _matmul(h, p["w_in"], "nn", BF16, tag + "proj")
    o, states, oa = _gla_fwd(proj, p["w_fg2"], p["b_fg"], p["gla_norm_g"], tag + "gla_fwd")
    p.update(p.pop("rest")((o,)))
    ya = _matmul(oa, p["w_oa"], "nn", BF16, tag + "ya")
    cb, yb = _conv_project(proj, p["conv_w"], p["conv_b"], p["w_ob"], tag + "yb")
    mix, x1, h2 = _mix_project(proj, ya, yb, p["w_o"], x, p["norm2_g"], tag + "x1")
    gu = _matmul(h2, p["w_gu"], "nn", BF16, tag + "gu")
    hid, x2, h_next = _swiglu_project(gu, p["w_down"], x1, next_norm_g, tag + "x2")
    saved = dict(x=x, h=h, proj=proj, o=o, states=states, oa=oa, ya=ya, cb=cb, yb=yb, mix=mix, x1=x1, h2=h2,
                 gu=gu, hid=hid)
    return x2, h_next, saved


def _layer_bwd(dx2, dx2h, p, sv, l, reduce=None):
    if reduce is None:
        reduce = lambda group, grads: ((), ())
    tag = f"l{l}_b_"
    dhid = _matmul(dx2h, p["w_down"], "nt", BF16, tag + "dhid")
    d_down = _matmul(sv["hid"], dx2h, "tn", BF16, tag + "dw_down")
    dgu = _swiglu_bwd(dhid, sv["gu"], tag + "swiglu")
    dx1, dx1h, dg2 = _matmul_norm_bwd(dgu, p["w_gu"], sv["x1"], p["norm2_g"], dx2, tag + "dh2")
    d_gu = _matmul(sv["h2"], dgu, "tn", BF16, tag + "dw_gu")
    gu_packed, gu_sums = reduce("gu", d_gu)
    dmix = _matmul(dx1h, p["w_o"], "nt", BF16, tag + "dmix", after=gu_packed)
    d_o = _matmul(sv["mix"], dx1h, "tn", BF16, tag + "dw_o")
    dproj, dya, dyb = _mix_bwd(dmix, sv["proj"], sv["ya"], sv["yb"], tag + "mix")
    dcb = _matmul(dyb, p["w_ob"], "nt", BF16, tag + "dcb", after=gu_sums)
    d_ob = _matmul(sv["cb"], dyb, "tn", BF16, tag + "dw_ob")
    dproj, dwb = _conv_bwd(dcb, sv["proj"], p["conv_w"], p["conv_b"], dproj, tag + "conv")
    doa = _matmul(dya, p["w_oa"], "nt", BF16, tag + "doa")
    d_oa = _matmul(sv["oa"], dya, "tn", BF16, tag + "dw_oa")
    dq, dk, dv, dfg, dbfg, dproj, dgg = _gla_bwd(sv["proj"], p["w_fg2"], p["b_fg"], sv["states"], doa, sv["o"],
                                                 p["gla_norm_g"], dproj, tag + "gla")
    dproj = _place_qkv(dq, dk, dv, dproj, tag + "place_qkv")
    dproj = _matmul(dfg, p["w_fg2"], "nt", BF16, tag + "dfz", into=(dproj, FZ_COL))
    d_fg2 = _matmul(sv["proj"], dfg, "tn", BF16, tag + "dw_fg2", a_cols=(FZ_COL, LANES))
    rows = dict(w_fg2=d_fg2, conv_w=dwb[0:3], w_oa=d_oa, w_ob=d_ob, w_o=d_o, w_down=d_down)
    rows_packed, rows_sums = reduce("rows", rows)
    d_in = _matmul(sv["h"], dproj, "tn", BF16, tag + "dw_in", after=rows_packed)
    _, in_sums = reduce("in", d_in)
    dx, dxh, dg1 = _matmul_norm_bwd(dproj, p["w_in"], sv["x"], p["norm1_g"], dx1, tag + "dh",
                                    after=(d_in,) + tuple(rows_sums) + tuple(in_sums))
    big = dict(w_in=d_in, w_gu=d_gu, **rows)
    pad = lambda t: jnp.pad(t, ((0, 0), (0, D_MODEL - t.shape[1])))
    small = [dg1[0:1], pad(dbfg[0:1]), pad(dgg[0:1]), dwb[3:4], dg2[0:1]]
    return dx, dxh, big, small


def _local_step(x, target, weights_of, final_g, reduce_of=None):
    saved, layers, h = [], [], None
    for l in range(DEPTH):
        layers.append(weights_of(l, x))
        x, h, sv = _layer_fwd(x, h, layers[l], l, layers[l].get("next_norm1_g"))
        saved.append(sv)
    loss, dx, dxh, dgf = _loss_head(x, final_g, target, "loss_head")
    bigs, smalls = [None] * DEPTH, [None] * DEPTH
    for l in reversed(range(DEPTH)):
        dx, dxh, bigs[l], smalls[l] = _layer_bwd(dx, dxh, layers[l], saved[l], l, reduce_of(l) if reduce_of else None)
    loss_row = jnp.pad(loss[0:1], ((0, 0), (0, D_MODEL - loss.shape[1])))
    small = jnp.concatenate(smalls[0] + smalls[1] + [dgf[0:1], loss_row], axis=0)
    small = jnp.pad(small, ((0, SMALL_ROWS - small.shape[0]), (0, 0)))
    return loss[0, 0], dx, bigs, small


def kernel(x, norm1_g, w_in, w_fg2, b_fg, gla_norm_g, w_oa, conv_w, conv_b, w_ob, w_o, norm2_g, w_ffn_gate, w_ffn_up, w_ffn_down, final_g, loss_target, m_norm1_g, m_w_in, m_w_fg2, m_b_fg, m_gla_norm_g, m_w_oa, m_conv_w, m_conv_b, m_w_ob, m_w_o, m_norm2_g, m_w_ffn_gate, m_w_ffn_up, m_w_ffn_down, m_final_g, v_norm1_g, v_w_in, v_w_fg2, v_b_fg, v_gla_norm_g, v_w_oa, v_conv_w, v_conv_b, v_w_ob, v_w_o, v_norm2_g, v_w_ffn_gate, v_w_ffn_up, v_w_ffn_down, v_final_g):
    names = ["norm1_g", "w_in", "w_fg2", "b_fg", "gla_norm_g", "w_oa", "conv_w", "conv_b", "w_ob", "w_o",
             "norm2_g", "w_ffn_gate", "w_ffn_up", "w_ffn_down", "final_g"]
    w = dict(zip(names, [norm1_g, w_in, w_fg2, b_fg, gla_norm_g, w_oa, conv_w, conv_b, w_ob, w_o, norm2_g,
                         w_ffn_gate, w_ffn_up, w_ffn_down, final_g]))
    m = dict(zip(names, [m_norm1_g, m_w_in, m_w_fg2, m_b_fg, m_gla_norm_g, m_w_oa, m_conv_w, m_conv_b, m_w_ob,
                         m_w_o, m_norm2_g, m_w_ffn_gate, m_w_ffn_up, m_w_ffn_down, m_final_g]))
    v = dict(zip(names, [v_norm1_g, v_w_in, v_w_fg2, v_b_fg, v_gla_norm_g, v_w_oa, v_conv_w, v_conv_b, v_w_ob,
                         v_w_o, v_norm2_g, v_w_ffn_gate, v_w_ffn_up, v_w_ffn_down, v_final_g]))
    col_names = ["w_in", "w_ffn_gate", "w_ffn_up"]
    cx, cy, cc = _me()
    prm = jnp.asarray(SHIFT_TABLE)[4 * cx + 2 * cy + cc]
    core = jnp.reshape(cc, (1,)).astype(jnp.int32)
    chip = jnp.reshape(2 * cx + cy, (1,)).astype(jnp.int32)

    ids = iter(range(32))

    gathered, previous = [], None
    for l in range(DEPTH):
        rows, tail = _pack_rows(w_oa, w_ob, w_o, w_ffn_down, w_fg2, conv_w, l)
        win_in, win_gu = _window_cols(prm, w_in, w_ffn_gate, w_ffn_up, l, f"l{l}_windows", after=previous)
        previous = rows
        first = _all_gather([win_in, tail], f"l{l}_gather_in", next(ids))
        gathered.append(list(first) + list(_all_gather([win_gu, rows], f"l{l}_gather_rest", next(ids))))

    def weights_of(l, x_in):
        all_in, all_tail, all_gu, all_rows = gathered[l]
        after = (x_in,) if l > 0 else ()
        w_in_full, tail = _assemble_in(all_in, all_tail, f"l{l}_assemble_in", after)
        w_fg2_full, conv_w_full = _unpack_tail(tail)

        def rest(after_rest):
            names_rest = ("w_gu", "w_oa", "w_ob", "w_o", "w_down")
            return dict(zip(names_rest, _assemble_rest(all_gu, all_rows, f"l{l}_assemble_rest", after + after_rest)))

        return dict(w_in=w_in_full, rest=rest, w_fg2=w_fg2_full,
                    conv_w=conv_w_full, norm1_g=norm1_g[l][None], b_fg=b_fg[l][None],
                    gla_norm_g=gla_norm_g[l][None], conv_b=conv_b[l][None], norm2_g=norm2_g[l][None],
                    next_norm1_g=norm1_g[l + 1][None] if l + 1 < DEPTH else None)

    pending = [dict() for _ in range(DEPTH)]
    landed = []

    def reduce_of(l):
        def reduce(group, grads):
            tag = f"l{l}_{group}"
            if group == "gu":
                packed = _grad_windows_gu(grads, tag + "_windows")
            elif group == "in":
                packed = _grad_windows_in(grads, tag + "_windows")
            else:
                packed = _grad_rows(grads)
            (from_sibling,) = _send_to_sibling([packed], tag + "_to_sibling", next(ids))
            waited = () if group == "in" else tuple(landed)
            if waited:
                landed.clear()
            sums = _pair_sum(packed, from_sibling, core, tag + "_pair_sum", after=waited)
            (from_chips,) = _send_to_chips([sums], tag + "_to_chips", next(ids))
            landed.append(from_chips)
            pending[l][group] = (sums, from_chips)
            return (packed,), (sums,)
        return reduce

    loss, dx, bigs, small = _local_step(x[0], loss_target[0], weights_of, final_g[None], reduce_of)

    grads, deltas, new_m, new_v = {}, {}, {}, {}
    for kind, group_names in (("gu", col_names[1:]), ("in", col_names[:1])):
        params = [t for n in group_names for t in (w[n], m[n], v[n])]
        out = None
        for l in reversed(range(DEPTH)):
            sums, from_chips = pending[l][kind]
            out = _final_windows(prm, chip, sums, from_chips, params, l, kind, f"l{l}_{kind}_final", out)
        for n, (g, d, nm, nv) in zip(group_names, out):
            grads[n], deltas[n], new_m[n], new_v[n] = g, d, nm, nv
    row_grads = [_ungrad_rows(_chip_sum(*pending[l]["rows"], chip, f"l{l}_rows_chip_sum")) for l in range(DEPTH)]
    for n in row_grads[0]:
        grads[n] = jnp.stack([row_grads[l][n] for l in range(DEPTH)])

    small_sum = _sum_devices(_all_gather([small], "gather_small")[0], "sum_small")
    r512, r256 = slice(0, KEY_W), slice(0, HEAD_V)
    grads.update(
        norm1_g=jnp.stack([small_sum[0], small_sum[5]]), b_fg=jnp.stack([small_sum[1, r512], small_sum[6, r512]]),
        gla_norm_g=jnp.stack([small_sum[2, r256], small_sum[7, r256]]),
        conv_b=jnp.stack([small_sum[3], small_sum[8]]), norm2_g=jnp.stack([small_sum[4], small_sum[9]]),
        final_g=small_sum[10])

    for n in names:
        if n not in col_names:
            deltas[n], new_m[n], new_v[n] = _adamw(w[n], grads[n], m[n], v[n], "adamw_" + n)

    total_loss = small_sum[2 * 5 + 1, 0]
    return (total_loss, dx[None], *[grads[n] for n in names], *[deltas[n] for n in names],
            *[new_m[n] for n in names], *[new_v[n] for n in names])
```

```python
import jax
import jax.numpy as jnp
import numpy as np
from jax import lax
from jax.experimental import pallas as pl
from jax.experimental.pallas import tpu as pltpu
from jax.experimental.pallas import tpu_sc as plsc

F32 = jnp.float32
BF16 = jnp.bfloat16
MESH = pl.DeviceIdType.MESH

D_MODEL = 1024
DEPTH = 2
CHUNK = 64
HEADS = 4
HEAD_K = 128
HEAD_V = 256
KEY_W = HEADS * HEAD_K
VAL_W = HEADS * HEAD_V
RANK = 16
TAU = 16.0
FFN = 2816
IN_WIDTH = 2 * KEY_W + 2 * VAL_W + RANK + 5 * D_MODEL
EPS = 1e-6
Q_SCALE = HEAD_K ** -0.5
N_DEV = 8
ADAM_LR, ADAM_B1, ADAM_B2, ADAM_EPS, ADAM_WD, ADAM_STEP = 0.001, 0.9, 0.999, 1e-08, 0.01, 10

LANES = 128
SUBLANES_BF16 = 16
VMEM_LIMIT = 48 * 1024 * 1024
VMEM_LIMIT_WIDE = 56 * 1024 * 1024

FZ_COL = 2 * KEY_W + 2 * VAL_W + 5 * D_MODEL
PROJ_W = FZ_COL + LANES
SEG_R, SEG_GBI, SEG_GCI, SEG_CX, SEG_GA, SEG_GB = 2, 3, 4, 5, 6, 7

W_IN_SHARD = IN_WIDTH // N_DEV
FFN_SHARD = FFN // N_DEV
ROW_SHARD = D_MODEL // N_DEV

WIN_IN = 9 * LANES
WIN_GU = 4 * LANES
A_FZ = WIN_IN
IN_COLS = A_FZ + LANES
GU_COLS = 2 * WIN_GU
ORIG_FZ = 2 * KEY_W + 2 * VAL_W


def _new_col(o):
    if o < ORIG_FZ:
        return o
    if o < ORIG_FZ + RANK:
        return FZ_COL + (o - ORIG_FZ)
    return o - RANK


def _shift_table():
    t_in, rows = [], []
    for j in range(N_DEV):
        new = [_new_col(W_IN_SHARD * j + i) for i in range(W_IN_SHARD)]
        main = [i for i in range(W_IN_SHARD) if new[i] < FZ_COL]
        code = [i for i in range(W_IN_SHARD) if new[i] >= FZ_COL]
        shift = new[main[0]] - main[0]
        t_in.append(shift // LANES)
        assert all(new[i] - i == shift for i in main) and shift % LANES + W_IN_SHARD <= WIN_IN
        if code:
            cshift = new[code[0]] - FZ_COL - code[0]
            crow = [cshift % LANES, code[0], code[-1] + 1, int(cshift < 0)]
        else:
            crow = [0, 0, 0, 0]
        rows.append([shift % LANES, main[0], main[-1] + 1] + crow + [FFN_SHARD * j % LANES])
    return tuple(t_in), np.asarray(rows, np.int32)


T_IN, SHIFT_TABLE = _shift_table()
T_GU = tuple(FFN_SHARD * j // LANES for j in range(N_DEV))

B_OA, B_OB, B_O, B_DOWN = 0, ROW_SHARD, 2 * ROW_SHARD, 3 * ROW_SHARD
B_FG2 = B_DOWN + FFN_SHARD
B_CONV = B_FG2 + 1
B_ROWS = B_FG2 + SUBLANES_BF16
SMALL_ROWS = 32


def _pick(n, candidates):
    for c in candidates:
        if n % c == 0:
            return c
    return n


def _cparams(sem):
    return pltpu.CompilerParams(dimension_semantics=sem, vmem_limit_bytes=VMEM_LIMIT)


def _sigmoid(x):
    return 1.0 / (1.0 + jnp.exp(-x))


def _matmul(a, b, dims, out_dtype, name, after=(), into=None, a_cols=None):
    if dims == "nn":
        (m, k), (k2, n) = a.shape, b.shape
    elif dims == "nt":
        (m, k), (n, k2) = a.shape, b.shape
    else:
        (k, m), (k2, n) = a.shape, b.shape
        m = m if a_cols is None else a_cols[1]
    assert k == k2 and (a_cols is None or dims == "tn"), (a.shape, b.shape, dims)
    tm = _pick(m, (1024, 1408, 512, 256, 128))
    tn = _pick(n, (1664, 1408, 1024, 512, 256, 128))
    tk = _pick(k, (1664, 1408, 1024, 512, 256, 128))
    nk = k // tk
    if dims == "nn":
        a_spec = pl.BlockSpec((tm, tk), lambda i, j, kk: (i, kk))
        b_spec = pl.BlockSpec((tk, tn), lambda i, j, kk: (kk, j))
        contract = (((1,), (0,)), ((), ()))
    elif dims == "nt":
        a_spec = pl.BlockSpec((tm, tk), lambda i, j, kk: (i, kk))
        b_spec = pl.BlockSpec((tn, tk), lambda i, j, kk: (j, kk))
        contract = (((1,), (1,)), ((), ()))
    else:
        first = 0 if a_cols is None else a_cols[0] // tm
        assert a_cols is None or (tm == m and a_cols[0] % tm == 0)
        a_spec = pl.BlockSpec((tk, tm), lambda i, j, kk: (kk, first + i))
        b_spec = pl.BlockSpec((tk, tn), lambda i, j, kk: (kk, j))
        contract = (((0,), (0,)), ((), ()))
    o_spec = pl.BlockSpec((tm, tn), lambda i, j, kk: (i, j))
    out_spec, out_struct, placed, aliases = o_spec, jax.ShapeDtypeStruct((m, n), out_dtype), (), {}
    if into is not None:
        buffer, col = into
        assert col % tn == 0 and buffer.dtype == out_dtype
        out_spec = pl.BlockSpec((tm, tn), lambda i, j, kk: (i, col // tn + j))
        out_struct, placed, aliases = jax.ShapeDtypeStruct(buffer.shape, out_dtype), (buffer,), {2 + len(after): 0}

    def body(*refs):
        a_ref, b_ref = refs[:2]
        o_ref = refs[2 + len(after) + len(placed)]
        kk = pl.program_id(2)
        part = lax.dot_general(a_ref[...], b_ref[...], contract, preferred_element_type=F32)

        def finish(total):
            o_ref[...] = total.astype(o_ref.dtype)

        if nk == 1:
            finish(part)
            return
        acc_ref = refs[-1]

        @pl.when(kk == 0)
        def _():
            acc_ref[...] = part

        @pl.when((kk > 0) & (kk < nk - 1))
        def _():
            acc_ref[...] += part

        @pl.when(kk == nk - 1)
        def _():
            finish(acc_ref[...] + part)

    in_specs = [a_spec, b_spec] + [_after_spec(t) for t in after] + [_ANY] * len(placed)
    args = (a, b) + tuple(after) + placed
    return pl.pallas_call(
        body, grid=(m // tm, n // tn, nk), in_specs=in_specs, out_specs=out_spec,
        out_shape=out_struct, input_output_aliases=aliases,
        scratch_shapes=[pltpu.VMEM((tm, tn), F32)] if nk > 1 else [], name=name,
        compiler_params=_cparams(("parallel", "parallel", "arbitrary")),
    )(*args)


def _rmsnorm_fwd(x, g, name):
    s, d = x.shape
    tm = _pick(s, (512, 256))

    def body(x_ref, g_ref, o_ref):
        xv = x_ref[...]
        r = lax.rsqrt(jnp.mean(xv * xv, axis=-1, keepdims=True) + EPS)
        o_ref[...] = (xv * r * g_ref[...]).astype(o_ref.dtype)

    row = pl.BlockSpec((tm, d), lambda i: (i, 0))
    return pl.pallas_call(
        body, grid=(s // tm,), in_specs=[row, pl.BlockSpec((1, d), lambda i: (0, 0))], out_specs=row,
        out_shape=jax.ShapeDtypeStruct((s, d), BF16), name=name, compiler_params=_cparams(("parallel",)),
    )(x, g)


def _matmul_norm_bwd(a, b, x, g, dres, name, after=()):
    (s, k), (d, k2) = a.shape, b.shape
    assert k == k2 and x.shape == (s, d)
    tm = _pick(s, (1024, 512, 256))
    tk = _pick(k, (1664, 1408, 1024, 512, 256, 128))
    nk = k // tk

    def body(a_ref, b_ref, x_ref, g_ref, dres_ref, *rest):
        dx_ref, dx16_ref, dg_ref, acc_ref = rest[len(after):]
        i, kk = pl.program_id(0), pl.program_id(1)
        part = lax.dot_general(a_ref[...], b_ref[...], _NT, preferred_element_type=F32)

        def finish(dh):
            xv = x_ref[...]
            r = lax.rsqrt(jnp.mean(xv * xv, axis=-1, keepdims=True) + EPS)
            xn = xv * r
            dxn = dh * g_ref[...]
            dx = dres_ref[...] + r * (dxn - xn * jnp.mean(dxn * xn, axis=-1, keepdims=True))
            dx_ref[...] = dx
            dx16_ref[...] = dx.astype(BF16)
            dg = jnp.broadcast_to(jnp.sum(dh * xn, axis=0, keepdims=True), dg_ref.shape)

            @pl.when(i == 0)
            def _():
                dg_ref[...] = dg

            @pl.when(i > 0)
            def _():
                dg_ref[...] += dg

        @pl.when(kk == 0)
        def _():
            acc_ref[...] = part

        @pl.when((kk > 0) & (kk < nk - 1))
        def _():
            acc_ref[...] += part

        @pl.when(kk == nk - 1)
        def _():
            finish(acc_ref[...] + part)

    assert nk > 1
    row = pl.BlockSpec((tm, d), lambda i, kk: (i, 0))
    return pl.pallas_call(
        body, grid=(s // tm, nk),
        in_specs=[pl.BlockSpec((tm, tk), lambda i, kk: (i, kk)), pl.BlockSpec((d, tk), lambda i, kk: (0, kk)), row,
                  pl.BlockSpec((1, d), lambda i, kk: (0, 0)), row] + [_after_spec(t) for t in after],
        out_specs=[row, row, pl.BlockSpec((8, d), lambda i, kk: (0, 0))],
        out_shape=[jax.ShapeDtypeStruct((s, d), F32), jax.ShapeDtypeStruct((s, d), BF16),
                   jax.ShapeDtypeStruct((8, d), F32)],
        scratch_shapes=[pltpu.VMEM((tm, d), F32)], name=name,
        compiler_params=pltpu.CompilerParams(dimension_semantics=("arbitrary", "arbitrary"),
                                             vmem_limit_bytes=VMEM_LIMIT_WIDE),
    )(a, b, x, g, dres, *after)


def _loss_head(x, g, target, name):
    s, d = x.shape
    tm = _pick(s, (512, 256))

    def body(x_ref, g_ref, t_ref, loss_ref, dx_ref, dx16_ref, dg_ref):
        xv = x_ref[...]
        gv = g_ref[...]
        r = lax.rsqrt(jnp.mean(xv * xv, axis=-1, keepdims=True) + EPS)
        xn = xv * r
        err = xn * gv - t_ref[...]
        dy = err * (1.0 / d)
        dxn = dy * gv
        dx = r * (dxn - xn * jnp.mean(dxn * xn, axis=-1, keepdims=True))
        dx_ref[...] = dx
        dx16_ref[...] = dx.astype(BF16)

        @pl.when(pl.program_id(0) == 0)
        def _():
            dg_ref[...] = jnp.zeros_like(dg_ref)
            loss_ref[...] = jnp.zeros_like(loss_ref)

        dg_ref[...] += jnp.broadcast_to(jnp.sum(dy * xn, axis=0, keepdims=True), dg_ref.shape)
        row_loss = jnp.sum(err * err, axis=-1, keepdims=True)
        loss_ref[...] += jnp.broadcast_to((0.5 / d) * jnp.sum(row_loss, axis=0, keepdims=True), loss_ref.shape)

    row = pl.BlockSpec((tm, d), lambda i: (i, 0))
    return pl.pallas_call(
        body, grid=(s // tm,), in_specs=[row, pl.BlockSpec((1, d), lambda i: (0, 0)), row],
        out_specs=[pl.BlockSpec((8, LANES), lambda i: (0, 0)), row, row, pl.BlockSpec((8, d), lambda i: (0, 0))],
        out_shape=[jax.ShapeDtypeStruct((8, LANES), F32), jax.ShapeDtypeStruct((s, d), F32),
                   jax.ShapeDtypeStruct((s, d), BF16), jax.ShapeDtypeStruct((8, d), F32)],
        name=name, compiler_params=_cparams(("arbitrary",)),
    )(x, g, target)


def _chunk_cumsum(x):
    pos = lax.broadcasted_iota(jnp.int32, (x.shape[0], 1), 0) % CHUNK
    shift = 1
    while shift < CHUNK:
        x = x + jnp.where(pos >= shift, pltpu.roll(x, shift, 0), 0.0)
        shift *= 2
    return x


def _block_decay(fz, w, b):
    fg = jnp.dot(fz, w, preferred_element_type=F32) + b
    la = (jnp.minimum(fg, 0.0) - jnp.log(1.0 + jnp.exp(-jnp.abs(fg)))) * (1.0 / TAU)
    cum = _chunk_cumsum(la)
    ends = [cum[i + CHUNK - 1:i + CHUNK, :] for i in range(0, fz.shape[0], CHUNK)]
    end = jnp.concatenate([jnp.broadcast_to(e, (CHUNK, e.shape[1])) for e in ends], axis=0)
    return fg, jnp.exp(end - cum), [jnp.exp(e) for e in ends]


_TN = (((0,), (0,)), ((), ()))
_NT = (((1,), (1,)), ((), ()))


def _gla_specs(rows):
    q_spec = pl.BlockSpec((rows, HEAD_K), lambda h, c: (c, h))
    k_spec = pl.BlockSpec((rows, HEAD_K), lambda h, c: (c, HEADS + h))
    v_spec = pl.BlockSpec((rows, HEAD_V), lambda h, c: (c, HEADS + h))
    fz_spec = pl.BlockSpec((rows, LANES), lambda h, c: (c, FZ_COL // LANES))
    w_spec = pl.BlockSpec((LANES, HEAD_K), lambda h, c: (0, h))
    b_spec = pl.BlockSpec((1, HEAD_K), lambda h, c: (0, h))
    return q_spec, k_spec, v_spec, fz_spec, w_spec, b_spec


def _gla_fwd(proj, wfg, bfg, gla_g, name):
    s = proj.shape[0]
    nc = s // CHUNK
    per = _pick(nc, (8, 4, 2, 1))
    rows = per * CHUNK

    def body(q_ref, k_ref, v_ref, fz_ref, w_ref, b_ref, r_ref, g_ref, o_ref, st_ref, oa_ref, state, update):
        @pl.when(pl.program_id(1) == 0)
        def _():
            state[...] = jnp.zeros_like(state)

        _, dec, gammas = _block_decay(fz_ref[...], w_ref[...], b_ref[...])
        kd = (k_ref[...].astype(F32) * dec).astype(BF16)
        qs = (q_ref[...].astype(F32) * Q_SCALE).astype(BF16)
        for i in range(per):
            sl = slice(i * CHUNK, (i + 1) * CHUNK)
            update[i] = lax.dot_general(v_ref[sl, :], kd[sl], _TN, preferred_element_type=F32)
        st = state[...]
        for i in range(per):
            st = st * gammas[i] + update[i]
            st_ref[0, i] = st.astype(BF16)
        state[...] = st
        for i in range(per):
            sl = slice(i * CHUNK, (i + 1) * CHUNK)
            o_ref[sl, :] = lax.dot_general(qs[sl], st_ref[0, i], _NT, preferred_element_type=F32).astype(o_ref.dtype)
        ov = o_ref[...].astype(F32)
        rstd = lax.rsqrt(jnp.mean(ov * ov, axis=-1, keepdims=True) + EPS)
        rv = r_ref[...].astype(F32)
        oa_ref[...] = (ov * rstd * g_ref[...] * (rv * _sigmoid(rv))).astype(oa_ref.dtype)

    q_spec, k_spec, v_spec, fz_spec, w_spec, b_spec = _gla_specs(rows)
    head_v = pl.BlockSpec((rows, HEAD_V), lambda h, c: (c, h))
    return pl.pallas_call(
        body, grid=(HEADS, nc // per),
        in_specs=[q_spec, k_spec, v_spec, fz_spec, w_spec, b_spec,
                  pl.BlockSpec((rows, HEAD_V), lambda h, c: (c, SEG_R * (D_MODEL // HEAD_V) + h)),
                  pl.BlockSpec((1, HEAD_V), lambda h, c: (0, 0))],
        out_specs=[head_v, pl.BlockSpec((1, per, HEAD_V, HEAD_K), lambda h, c: (h, c, 0, 0)), head_v],
        out_shape=[jax.ShapeDtypeStruct((s, VAL_W), BF16),
                   jax.ShapeDtypeStruct((HEADS, nc, HEAD_V, HEAD_K), BF16), jax.ShapeDtypeStruct((s, VAL_W), BF16)],
        scratch_shapes=[pltpu.VMEM((HEAD_V, HEAD_K), F32), pltpu.VMEM((per, HEAD_V, HEAD_K), F32)], name=name,
        compiler_params=_cparams(("parallel", "arbitrary")),
    )(proj, proj, proj, proj, wfg, bfg, proj, gla_g)


def _gla_bwd(proj, wfg, bfg, states, doa, o, gla_g, dproj, name):
    s = proj.shape[0]
    nc = s // CHUNK
    per = _pick(nc, (8, 4, 2, 1))
    rows = per * CHUNK
    nblk = nc // per

    def rev(spec_fn):
        return lambda h, j: spec_fn(h, nblk - 1 - j)

    def body(q_ref, k_ref, v_ref, fz_ref, w_ref, b_ref, doa_ref, o_ref, r_ref, g_ref,
             st_ref, prev_ref, _, dq_ref, dk_ref, dv_ref, dfg_ref, db_ref, dr_ref, dg_ref,
             carry, gt_all, dkd_all, dgg_all):
        j = pl.program_id(1)

        @pl.when(j == 0)
        def _():
            carry[...] = jnp.zeros_like(carry)
            db_ref[...] = jnp.zeros_like(db_ref)

        @pl.when((j == 0) & (pl.program_id(0) == 0))
        def _():
            dg_ref[...] = jnp.zeros_like(dg_ref)

        ov = o_ref[...].astype(F32)
        rstd = lax.rsqrt(jnp.mean(ov * ov, axis=-1, keepdims=True) + EPS)
        ohat = ov * rstd
        rv = r_ref[...].astype(F32)
        sg = _sigmoid(rv)
        doav = doa_ref[...].astype(F32)
        gv = g_ref[...]
        dr_ref[...] = (doav * ohat * gv * (sg * (1.0 + rv * (1.0 - sg)))).astype(dr_ref.dtype)
        don = doav * (rv * sg)
        dg_ref[...] += jnp.broadcast_to(jnp.sum(don * ohat, axis=0, keepdims=True), dg_ref.shape)
        dohat = don * gv
        do = (rstd * (dohat - ohat * jnp.mean(dohat * ohat, axis=-1, keepdims=True))).astype(BF16)

        fg, dec, gammas = _block_decay(fz_ref[...], w_ref[...], b_ref[...])
        kd = k_ref[...].astype(F32) * dec
        kd16 = kd.astype(BF16)
        qs = (q_ref[...].astype(F32) * Q_SCALE).astype(BF16)
        for i in range(per):
            sl = slice(i * CHUNK, (i + 1) * CHUNK)
            gt_all[i] = lax.dot_general(do[sl], qs[sl], _TN, preferred_element_type=F32)
        back = carry[...]
        for i in reversed(range(per)):
            gt = back + gt_all[i]
            gt_all[i] = gt
            back = gt * gammas[i]
        carry[...] = back
        has_prev = (j < nblk - 1).astype(F32)
        for i in range(per):
            sl = slice(i * CHUNK, (i + 1) * CHUNK)
            gt = gt_all[i]
            gt16 = gt.astype(BF16)
            dq_ref[sl, :] = (jnp.dot(do[sl], st_ref[0, i], preferred_element_type=F32) * Q_SCALE).astype(dq_ref.dtype)
            dkd_all[sl, :] = jnp.dot(v_ref[sl, :], gt16, preferred_element_type=F32)
            dv_ref[sl, :] = lax.dot_general(kd16[sl], gt16, _NT, preferred_element_type=F32).astype(dv_ref.dtype)
            if i > 0:
                st_prev = st_ref[0, i - 1].astype(F32)
            else:
                st_prev = prev_ref[0, 0].astype(F32) * has_prev
            dgamma = jnp.sum(gt * st_prev, axis=0, keepdims=True)
            dgg_all[sl, :] = jnp.broadcast_to(dgamma * gammas[i], (CHUNK, HEAD_K))
        dkd = dkd_all[...]
        dk_ref[...] = (dkd * dec).astype(dk_ref.dtype)
        e = dkd * kd
        dla = dgg_all[...] + (_chunk_cumsum(e) - e)
        dfg = dla * (1.0 / TAU) * _sigmoid(-fg)
        dfg_ref[...] = dfg.astype(dfg_ref.dtype)
        db_ref[...] += jnp.broadcast_to(jnp.sum(dfg, axis=0, keepdims=True), db_ref.shape)

    q_spec, k_spec, v_spec, fz_spec, w_spec, b_spec = _gla_specs(rows)
    q_spec, k_spec, v_spec, fz_spec = [
        pl.BlockSpec(sp.block_shape, rev(sp.index_map)) for sp in (q_spec, k_spec, v_spec, fz_spec)]
    do_spec = pl.BlockSpec((rows, HEAD_V), lambda h, j: (nblk - 1 - j, h))
    st_spec = pl.BlockSpec((1, per, HEAD_V, HEAD_K), lambda h, j: (h, nblk - 1 - j, 0, 0))
    prev_spec = pl.BlockSpec((1, 1, HEAD_V, HEAD_K),
                             lambda h, j: (h, jnp.maximum((nblk - 1 - j) * per - 1, 0), 0, 0))
    key_out = pl.BlockSpec((rows, HEAD_K), lambda h, j: (nblk - 1 - j, h))
    r_spec = pl.BlockSpec((rows, HEAD_V), lambda h, j: (nblk - 1 - j, SEG_R * (D_MODEL // HEAD_V) + h))
    return pl.pallas_call(
        body, grid=(HEADS, nblk),
        in_specs=[q_spec, k_spec, v_spec, fz_spec, w_spec, b_spec,
                  do_spec, do_spec, r_spec, pl.BlockSpec((1, HEAD_V), lambda h, j: (0, 0)),
                  st_spec, prev_spec, _ANY],
        out_specs=[key_out, key_out, do_spec, key_out, pl.BlockSpec((8, HEAD_K), lambda h, j: (0, h)),
                   r_spec, pl.BlockSpec((8, HEAD_V), lambda h, j: (0, 0))],
        out_shape=[jax.ShapeDtypeStruct((s, KEY_W), BF16), jax.ShapeDtypeStruct((s, KEY_W), BF16),
                   jax.ShapeDtypeStruct((s, VAL_W), BF16), jax.ShapeDtypeStruct((s, KEY_W), BF16),
                   jax.ShapeDtypeStruct((8, KEY_W), F32), jax.ShapeDtypeStruct(dproj.shape, BF16),
                   jax.ShapeDtypeStruct((8, HEAD_V), F32)],
        input_output_aliases={12: 5},
        scratch_shapes=[pltpu.VMEM((HEAD_V, HEAD_K), F32), pltpu.VMEM((per, HEAD_V, HEAD_K), F32),
                        pltpu.VMEM((rows, HEAD_K), F32), pltpu.VMEM((rows, HEAD_K), F32)], name=name,
        compiler_params=_cparams(("arbitrary", "arbitrary")),
    )(proj, proj, proj, proj, wfg, bfg, doa, o, proj, gla_g, states, states, dproj)


def _place_qkv(dq, dk, dv, dproj, name):
    s = dq.shape[0]
    tm = _pick(s, (512, 256))

    def body(dq_ref, dk_ref, dv_ref, _, o_ref):
        o_ref[:, 0:KEY_W] = dq_ref[...]
        o_ref[:, KEY_W:2 * KEY_W] = dk_ref[...]
        o_ref[:, 2 * KEY_W:2 * KEY_W + VAL_W] = dv_ref[...]

    def rows(width):
        return pl.BlockSpec((tm, width), lambda i: (i, 0))

    return pl.pallas_call(
        body, grid=(s // tm,), in_specs=[rows(KEY_W), rows(KEY_W), rows(VAL_W), _ANY],
        out_specs=rows(2 * KEY_W + VAL_W), out_shape=jax.ShapeDtypeStruct(dproj.shape, BF16),
        input_output_aliases={3: 0}, name=name, compiler_params=_cparams(("parallel",)),
    )(dq, dk, dv, dproj)


def _seg(tm, seg):
    return pl.BlockSpec((tm, D_MODEL), lambda i: (i, seg))


HALO = SUBLANES_BF16


def _shift_down(u, p1, p2, n, rows):
    rolled = pltpu.roll(u, n, 0)
    if n == 1:
        return jnp.where(rows == 0, p1, rolled)
    return jnp.where(rows == 0, p2, jnp.where(rows == 1, p1, rolled))


def _shift_up(u, n1, n2, n, rows, tm):
    rolled = pltpu.roll(u, tm - n, 0)
    if n == 1:
        return jnp.where(rows == tm - 1, n1, rolled)
    return jnp.where(rows == tm - 2, n1, jnp.where(rows == tm - 1, n2, rolled))


def _conv_terms(gc_ref, cx_ref, gcp_ref, cxp_ref, tm):
    i = pl.program_id(0)
    u = gc_ref[...].astype(F32) * cx_ref[...].astype(F32)
    up = gcp_ref[...].astype(F32) * cxp_ref[...].astype(F32) * (i > 0).astype(F32)
    rows = lax.broadcasted_iota(jnp.int32, (tm, 1), 0)
    u1 = _shift_down(u, up[HALO - 1:HALO, :], up[HALO - 2:HALO - 1, :], 1, rows)
    u2 = _shift_down(u, up[HALO - 1:HALO, :], up[HALO - 2:HALO - 1, :], 2, rows)
    return u, u1, u2, rows


def _prev_halo(tm, seg):
    return pl.BlockSpec((HALO, D_MODEL), lambda i: (jnp.maximum(i * (tm // HALO) - 1, 0), seg))


def _conv_project(proj, w, b, w_ob, name):
    s = proj.shape[0]
    tm = _pick(s, (512, 256))

    def body(gbi_ref, gc_ref, cx_ref, gcp_ref, cxp_ref, w_ref, b_ref, wob_ref, cb_ref, yb_ref):
        u, u1, u2, _ = _conv_terms(gc_ref, cx_ref, gcp_ref, cxp_ref, tm)
        conv = w_ref[0:1, :] * u2 + w_ref[1:2, :] * u1 + w_ref[2:3, :] * u + b_ref[...]
        cb = (gbi_ref[...].astype(F32) * conv).astype(BF16)
        cb_ref[...] = cb
        yb_ref[...] = jnp.dot(cb, wob_ref[...], preferred_element_type=F32).astype(BF16)

    row = pl.BlockSpec((tm, D_MODEL), lambda i: (i, 0))
    return pl.pallas_call(
        body, grid=(s // tm,),
        in_specs=[_seg(tm, SEG_GBI), _seg(tm, SEG_GCI), _seg(tm, SEG_CX),
                  _prev_halo(tm, SEG_GCI), _prev_halo(tm, SEG_CX),
                  pl.BlockSpec((3, D_MODEL), lambda i: (0, 0)), pl.BlockSpec((1, D_MODEL), lambda i: (0, 0)),
                  pl.BlockSpec((D_MODEL, D_MODEL), lambda i: (0, 0))],
        out_specs=[row, row],
        out_shape=[jax.ShapeDtypeStruct((s, D_MODEL), BF16), jax.ShapeDtypeStruct((s, D_MODEL), BF16)],
        name=name, compiler_params=_cparams(("parallel",)),
    )(proj, proj, proj, proj, proj, w, b, w_ob)


def _conv_bwd(dcb, proj, w, b, dproj, name):
    s = proj.shape[0]
    tm = _pick(s, (512, 256))
    nt = s // tm

    def body(dcb_ref, gbi_ref, gc_ref, cx_ref, gcp_ref, cxp_ref, dcbn_ref, gbin_ref, w_ref, b_ref, _,
             d3_ref, dwb_ref):
        i = pl.program_id(0)

        @pl.when(i == 0)
        def _():
            dwb_ref[...] = jnp.zeros_like(dwb_ref)

        u, u1, u2, rows = _conv_terms(gc_ref, cx_ref, gcp_ref, cxp_ref, tm)
        w0, w1, w2 = w_ref[0:1, :], w_ref[1:2, :], w_ref[2:3, :]
        conv = w0 * u2 + w1 * u1 + w2 * u + b_ref[...]
        dcbv = dcb_ref[...].astype(F32)
        gbi = gbi_ref[...].astype(F32)
        dconv = dcbv * gbi
        dnext = dcbn_ref[...].astype(F32) * gbin_ref[...].astype(F32) * (i < nt - 1).astype(F32)
        dc1 = _shift_up(dconv, dnext[0:1, :], dnext[1:2, :], 1, rows, tm)
        dc2 = _shift_up(dconv, dnext[0:1, :], dnext[1:2, :], 2, rows, tm)
        du = w2 * dconv + w1 * dc1 + w0 * dc2
        d3_ref[:, 0:D_MODEL] = (dcbv * conv).astype(d3_ref.dtype)
        d3_ref[:, D_MODEL:2 * D_MODEL] = (du * cx_ref[...].astype(F32)).astype(d3_ref.dtype)
        d3_ref[:, 2 * D_MODEL:3 * D_MODEL] = (du * gc_ref[...].astype(F32)).astype(d3_ref.dtype)
        dwb_ref[0:1, :] += jnp.sum(dconv * u2, axis=0, keepdims=True)
        dwb_ref[1:2, :] += jnp.sum(dconv * u1, axis=0, keepdims=True)
        dwb_ref[2:3, :] += jnp.sum(dconv * u, axis=0, keepdims=True)
        dwb_ref[3:4, :] += jnp.sum(dconv, axis=0, keepdims=True)

    def next_halo(seg_fn):
        return pl.BlockSpec((HALO, D_MODEL), lambda i: (jnp.minimum((i + 1) * (tm // HALO), s // HALO - 1), seg_fn))

    return pl.pallas_call(
        body, grid=(nt,),
        in_specs=[pl.BlockSpec((tm, D_MODEL), lambda i: (i, 0)),
                  _seg(tm, SEG_GBI), _seg(tm, SEG_GCI), _seg(tm, SEG_CX),
                  _prev_halo(tm, SEG_GCI), _prev_halo(tm, SEG_CX),
                  next_halo(0), next_halo(SEG_GBI),
                  pl.BlockSpec((3, D_MODEL), lambda i: (0, 0)), pl.BlockSpec((1, D_MODEL), lambda i: (0, 0)), _ANY],
        out_specs=[pl.BlockSpec((tm, 3 * D_MODEL), lambda i: (i, SEG_GBI // 3)),
                   pl.BlockSpec((8, D_MODEL), lambda i: (0, 0))],
        out_shape=[jax.ShapeDtypeStruct(dproj.shape, BF16), jax.ShapeDtypeStruct((8, D_MODEL), F32)],
        input_output_aliases={10: 0}, name=name, compiler_params=_cparams(("arbitrary",)),
    )(dcb, proj, proj, proj, proj, proj, dcb, proj, w, b, dproj)


def _mix_project(proj, ya, yb, w_o, x, norm_g, name):
    s = proj.shape[0]
    tm = _pick(s, (512, 256))

    def body(ga_ref, gb_ref, ya_ref, yb_ref, w_ref, x_ref, g_ref, mix_ref, x1_ref, h2_ref):
        mix = (_sigmoid(ga_ref[...].astype(F32)) * ya_ref[...].astype(F32)
               + _sigmoid(gb_ref[...].astype(F32)) * yb_ref[...].astype(F32)).astype(BF16)
        mix_ref[...] = mix
        x1 = x_ref[...] + jnp.dot(mix, w_ref[...], preferred_element_type=F32)
        x1_ref[...] = x1
        rstd = lax.rsqrt(jnp.mean(x1 * x1, axis=-1, keepdims=True) + EPS)
        h2_ref[...] = (x1 * rstd * g_ref[...]).astype(BF16)

    row = pl.BlockSpec((tm, D_MODEL), lambda i: (i, 0))
    whole = pl.BlockSpec((D_MODEL, D_MODEL), lambda i: (0, 0))
    return pl.pallas_call(
        body, grid=(s // tm,),
        in_specs=[_seg(tm, SEG_GA), _seg(tm, SEG_GB), row, row, whole, row, pl.BlockSpec((1, D_MODEL), lambda i: (0, 0))],
        out_specs=[row, row, row],
        out_shape=[jax.ShapeDtypeStruct((s, D_MODEL), BF16), jax.ShapeDtypeStruct((s, D_MODEL), F32),
                   jax.ShapeDtypeStruct((s, D_MODEL), BF16)],
        name=name, compiler_params=_cparams(("parallel",)),
    )(proj, proj, ya, yb, w_o, x, norm_g)


def _mix_bwd(dmix, proj, ya, yb, name):
    s = proj.shape[0]
    tm = _pick(s, (512, 256))

    def body(dm_ref, ga_ref, gb_ref, ya_ref, yb_ref, dg_ref, dya_ref, dyb_ref):
        dm = dm_ref[...].astype(F32)
        sa = _sigmoid(ga_ref[...].astype(F32))
        sb = _sigmoid(gb_ref[...].astype(F32))
        dg_ref[:, 0:D_MODEL] = (dm * ya_ref[...].astype(F32) * sa * (1.0 - sa)).astype(dg_ref.dtype)
        dg_ref[:, D_MODEL:2 * D_MODEL] = (dm * yb_ref[...].astype(F32) * sb * (1.0 - sb)).astype(dg_ref.dtype)
        dya_ref[...] = (dm * sa).astype(dya_ref.dtype)
        dyb_ref[...] = (dm * sb).astype(dyb_ref.dtype)

    row = pl.BlockSpec((tm, D_MODEL), lambda i: (i, 0))
    return pl.pallas_call(
        body, grid=(s // tm,), in_specs=[row, _seg(tm, SEG_GA), _seg(tm, SEG_GB), row, row],
        out_specs=[pl.BlockSpec((tm, 2 * D_MODEL), lambda i: (i, SEG_GA // 2)), row, row],
        out_shape=[jax.ShapeDtypeStruct((s, PROJ_W), BF16), jax.ShapeDtypeStruct((s, D_MODEL), BF16),
                   jax.ShapeDtypeStruct((s, D_MODEL), BF16)],
        name=name, compiler_params=_cparams(("parallel",)),
    )(dmix, proj, proj, ya, yb)


def _swiglu_project(gu, w_down, x1, norm_g, name):
    s = gu.shape[0]
    tm = _pick(s, (512, 256))
    tk = _pick(FFN, (1408, 256))
    nk = FFN // tk
    has_norm = norm_g is not None

    def body(gate_ref, up_ref, w_ref, x_ref, *rest):
        outs = rest[has_norm:]
        hid_ref, x2_ref, acc_ref = outs[0], outs[1], rest[-1]
        kk = pl.program_id(1)
        gate = gate_ref[...].astype(F32)
        hid = (gate * _sigmoid(gate) * up_ref[...].astype(F32)).astype(BF16)
        hid_ref[...] = hid
        part = jnp.dot(hid, w_ref[...], preferred_element_type=F32)

        @pl.when(kk == 0)
        def _():
            acc_ref[...] = part

        @pl.when((kk > 0) & (kk < nk - 1))
        def _():
            acc_ref[...] += part

        @pl.when(kk == nk - 1)
        def _():
            x2 = acc_ref[...] + part + x_ref[...]
            x2_ref[...] = x2
            if has_norm:
                rstd = lax.rsqrt(jnp.mean(x2 * x2, axis=-1, keepdims=True) + EPS)
                outs[2][...] = (x2 * rstd * rest[0][...]).astype(BF16)

    assert nk > 1
    row = pl.BlockSpec((tm, D_MODEL), lambda i, kk: (i, 0))
    out = pl.pallas_call(
        body, grid=(s // tm, nk),
        in_specs=[pl.BlockSpec((tm, tk), lambda i, kk: (i, kk)), pl.BlockSpec((tm, tk), lambda i, kk: (i, nk + kk)),
                  pl.BlockSpec((tk, D_MODEL), lambda i, kk: (kk, 0)), row]
        + ([pl.BlockSpec((1, D_MODEL), lambda i, kk: (0, 0))] if has_norm else []),
        out_specs=[pl.BlockSpec((tm, tk), lambda i, kk: (i, kk)), row] + ([row] if has_norm else []),
        out_shape=[jax.ShapeDtypeStruct((s, FFN), BF16), jax.ShapeDtypeStruct((s, D_MODEL), F32)]
        + ([jax.ShapeDtypeStruct((s, D_MODEL), BF16)] if has_norm else []),
        scratch_shapes=[pltpu.VMEM((tm, D_MODEL), F32)], name=name,
        compiler_params=pltpu.CompilerParams(dimension_semantics=("parallel", "arbitrary"),
                                             vmem_limit_bytes=VMEM_LIMIT_WIDE),
    )(gu, gu, w_down, x1, *((norm_g,) if has_norm else ()))
    return out[0], out[1], (out[2] if has_norm else None)


def _swiglu_bwd(dhid, gu, name):
    s = gu.shape[0]
    tm = _pick(s, (256,))

    def body(dh_ref, gu_ref, o_ref):
        gate = gu_ref[:, 0:FFN].astype(F32)
        up = gu_ref[:, FFN:2 * FFN].astype(F32)
        dh = dh_ref[...].astype(F32)
        sg = _sigmoid(gate)
        o_ref[:, 0:FFN] = (dh * up * (sg * (1.0 + gate * (1.0 - sg)))).astype(o_ref.dtype)
        o_ref[:, FFN:2 * FFN] = (dh * gate * sg).astype(o_ref.dtype)

    wide = pl.BlockSpec((tm, 2 * FFN), lambda i: (i, 0))
    return pl.pallas_call(
        body, grid=(s // tm,), in_specs=[pl.BlockSpec((tm, FFN), lambda i: (i, 0)), wide], out_specs=wide,
        out_shape=jax.ShapeDtypeStruct((s, 2 * FFN), BF16), name=name, compiler_params=_cparams(("parallel",)),
    )(dhid, gu)


def _adamw_math(w, g, m, v):
    m2 = ADAM_B1 * m + (1.0 - ADAM_B1) * g
    v2 = ADAM_B2 * v + (1.0 - ADAM_B2) * (g * g)
    m_hat = m2 / (1.0 - ADAM_B1 ** ADAM_STEP)
    v_hat = v2 / (1.0 - ADAM_B2 ** ADAM_STEP)
    delta = -ADAM_LR * (m_hat / (jnp.sqrt(v_hat) + ADAM_EPS) + ADAM_WD * w)
    return delta, m2, v2


def _adamw(w, g, m, v, name):
    shape = w.shape
    cols = shape[-1]
    rows = int(np.prod(shape[:-1])) if len(shape) > 1 else 1
    w2, g2, m2, v2 = [t.reshape(rows, cols) for t in (w, g, m, v)]
    tr = _pick(rows, (512, 352, 256)) if rows % 8 == 0 else rows

    def body(w_ref, g_ref, m_ref, v_ref, d_ref, nm_ref, nv_ref):
        d, nm, nv = _adamw_math(w_ref[...], g_ref[...], m_ref[...], v_ref[...])
        d_ref[...] = d
        nm_ref[...] = nm
        nv_ref[...] = nv

    blk = pl.BlockSpec((tr, cols), lambda i: (i, 0))
    out = pl.pallas_call(
        body, grid=(rows // tr,), in_specs=[blk] * 4, out_specs=[blk] * 3,
        out_shape=[jax.ShapeDtypeStruct((rows, cols), F32)] * 3, name=name,
        compiler_params=_cparams(("parallel",)),
    )(w2, g2, m2, v2)
    return [t.reshape(shape) for t in out]


def _pair_sum(g2, recv, core, name, after=()):
    _, nchip, r, c = g2.shape
    tr = _pick(r, (512,))

    def body(core_ref, a_ref, b_ref, *rest):
        o_ref = rest[-1]
        o_ref[...] = (a_ref[...].astype(F32) + b_ref[...].astype(F32)).astype(o_ref.dtype)

    grid_spec = pltpu.PrefetchScalarGridSpec(
        num_scalar_prefetch=1, grid=(nchip, r // tr),
        in_specs=[pl.BlockSpec((None, None, tr, c), lambda k, i, cr: (cr[0], k, i, 0)),
                  pl.BlockSpec((None, tr, c), lambda k, i, cr: (k, i, 0))] + [_after_spec(t) for t in after],
        out_specs=pl.BlockSpec((None, tr, c), lambda k, i, cr: (k, i, 0)))
    return pl.pallas_call(
        body, grid_spec=grid_spec, out_shape=jax.ShapeDtypeStruct((nchip, r, c), BF16), name=name,
        compiler_params=_cparams(("parallel", "parallel")),
    )(core, g2, recv, *after)


def _chip_sum(a, recv, chip, name, after=()):
    _, r, c = a.shape
    tr = _pick(r, (512,))

    def body(chip_ref, a_ref, b_ref, *rest):
        o_ref = rest[-1]
        o_ref[...] = ((a_ref[...].astype(F32) + b_ref[0].astype(F32)) + b_ref[1].astype(F32)) + b_ref[2].astype(F32)

    grid_spec = pltpu.PrefetchScalarGridSpec(
        num_scalar_prefetch=1, grid=(r // tr,),
        in_specs=[pl.BlockSpec((None, tr, c), lambda i, cr: (cr[0], i, 0)),
                  pl.BlockSpec((3, tr, c), lambda i, cr: (0, i, 0))] + [_after_spec(t) for t in after],
        out_specs=pl.BlockSpec((tr, c), lambda i, cr: (i, 0)))
    return pl.pallas_call(
        body, grid_spec=grid_spec, out_shape=jax.ShapeDtypeStruct((r, c), F32), name=name,
        compiler_params=_cparams(("parallel",)),
    )(chip, a, recv, *after)


def _sum_devices(parts, name):
    n, r, c = parts.shape

    def body(p_ref, o_ref):
        acc = p_ref[0]
        for d in range(1, n):
            acc = acc + p_ref[d]
        o_ref[...] = acc

    return pl.pallas_call(
        body, out_shape=jax.ShapeDtypeStruct((r, c), F32), name=name,
        in_specs=[pl.BlockSpec(memory_space=pltpu.VMEM)], out_specs=pl.BlockSpec(memory_space=pltpu.VMEM),
    )(parts)


def _lane_iota():
    return lax.broadcasted_iota(jnp.int32, (1, LANES), 1)


def _tiles_up(tiles, s, lane):
    rolled = [pltpu.roll(t, s, 1) for t in tiles]
    zero = jnp.zeros_like(tiles[0])
    return [jnp.where(lane < s, p, c) for p, c in zip([zero] + rolled, rolled + [zero])]


def _tiles_down(tiles, s, lane):
    back = (LANES - s) % LANES
    rolled = [pltpu.roll(t, back, 1) for t in tiles]
    zero = jnp.zeros_like(tiles[0])
    return [jnp.where(lane < LANES - s, c, n) for c, n in zip(rolled, rolled[1:] + [zero])]


def _window_cols(prm, w_in, w_gate, w_up, l, name, after=None):
    tm = 256
    n_in, n_gu = WIN_IN // LANES, FFN_SHARD // LANES + 1

    def body(prm_ref, win_ref, g_ref, u_ref, *rest):
        out_ref, gu_ref, scr_in, scr_gu = rest[-4:]
        lane = _lane_iota()
        s_main, lo, hi, s_code, clo, chi, code_hi, s_gu = [prm_ref[i] for i in range(8)]
        scr_in[:, D_MODEL:WIN_IN] = jnp.zeros((tm, WIN_IN - D_MODEL), F32)
        scr_in[:, 0:W_IN_SHARD] = win_ref[...]

        def keep(t, a, b):
            col = lane + t * LANES
            return jnp.where((col >= a) & (col < b), scr_in[:, t * LANES:(t + 1) * LANES], 0.0)

        main = _tiles_up([keep(t, lo, hi) for t in range(n_in)], s_main, lane)
        for t in range(n_in):
            out_ref[:, t * LANES:(t + 1) * LANES] = main[t].astype(BF16)
        low = _tiles_up([keep(0, clo, chi)], s_code, lane)[0]
        high = _tiles_up([keep(n_in - 2, clo, chi), keep(n_in - 1, clo, chi)], s_code, lane)[1]
        out_ref[:, A_FZ:A_FZ + LANES] = jnp.where(code_hi == 1, high, low).astype(BF16)
        for ref, base in ((g_ref, 0), (u_ref, WIN_GU)):
            scr_gu[:, (n_gu - 1) * LANES:n_gu * LANES] = jnp.zeros((tm, LANES), F32)
            scr_gu[:, 0:FFN_SHARD] = ref[...]
            moved = _tiles_up([scr_gu[:, t * LANES:(t + 1) * LANES] for t in range(n_gu)], s_gu, lane)
            for t in range(n_gu + 1):
                gu_ref[:, base + t * LANES:base + (t + 1) * LANES] = moved[t].astype(BF16)

    after_args = [] if after is None else [after]
    grid_spec = pltpu.PrefetchScalarGridSpec(
        num_scalar_prefetch=1, grid=(D_MODEL // tm,),
        in_specs=[pl.BlockSpec((None, tm, W_IN_SHARD), lambda i, p: (l, i, 0)),
                  pl.BlockSpec((None, tm, FFN_SHARD), lambda i, p: (l, i, 0)),
                  pl.BlockSpec((None, tm, FFN_SHARD), lambda i, p: (l, i, 0))] + [_after_spec(t) for t in after_args],
        out_specs=[pl.BlockSpec((tm, IN_COLS), lambda i, p: (i, 0)), pl.BlockSpec((tm, GU_COLS), lambda i, p: (i, 0))],
        scratch_shapes=[pltpu.VMEM((tm, WIN_IN), F32), pltpu.VMEM((tm, n_gu * LANES), F32)])
    return pl.pallas_call(
        body, grid_spec=grid_spec, name=name, compiler_params=_cparams(("parallel",)),
        out_shape=[jax.ShapeDtypeStruct((D_MODEL, IN_COLS), BF16), jax.ShapeDtypeStruct((D_MODEL, GU_COLS), BF16)],
    )(prm, w_in, w_gate, w_up, *after_args)


def _gu_width(j):
    return min(WIN_GU, FFN - T_GU[j] * LANES)


def _after_spec(t):
    tile = (SUBLANES_BF16 if t.dtype == BF16 else 8, LANES)
    return pl.BlockSpec((None,) * (t.ndim - 2) + tile, lambda *_: (0,) * t.ndim)


def _assemble_in(a_all, tail_all, name, after=()):
    tm = D_MODEL // N_DEV

    def body(a_ref, t_ref, *rest):
        win_ref, tail_ref = rest[-2:]
        tail_ref[...] = t_ref[...]
        win_ref[...] = jnp.zeros_like(win_ref)
        code = a_ref[0, :, A_FZ:A_FZ + LANES]
        for j in range(N_DEV):
            c0 = T_IN[j] * LANES
            win_ref[:, c0:c0 + WIN_IN] += a_ref[j, :, 0:WIN_IN]
            if j > 0:
                code = code + a_ref[j, :, A_FZ:A_FZ + LANES]
        win_ref[:, FZ_COL:PROJ_W] = code

    return pl.pallas_call(
        body, grid=(N_DEV,),
        in_specs=[pl.BlockSpec((N_DEV, tm, IN_COLS), lambda i: (0, i, 0)),
                  pl.BlockSpec((None,) + tail_all.shape[1:], lambda i: (i, 0, 0))] + [_after_spec(t) for t in after],
        out_specs=[pl.BlockSpec((tm, PROJ_W), lambda i: (i, 0)),
                   pl.BlockSpec((None,) + tail_all.shape[1:], lambda i: (i, 0, 0))],
        out_shape=[jax.ShapeDtypeStruct((D_MODEL, PROJ_W), BF16), jax.ShapeDtypeStruct(tail_all.shape, BF16)],
        name=name, compiler_params=_cparams(("parallel",)),
    )(a_all, tail_all, *after)


def _assemble_rest(a_all, rows_all, name, after=()):
    tm = D_MODEL // N_DEV
    n_in = 2 + len(after)

    def body(*refs):
        a_ref, r_ref = refs[:2]
        wgu_ref, oa_ref, ob_ref, o_ref, down_ref = refs[n_in:]
        wgu_ref[...] = jnp.zeros_like(wgu_ref)
        for j in range(N_DEV):
            g0, width = T_GU[j] * LANES, _gu_width(j)
            wgu_ref[:, g0:g0 + width] += a_ref[j, :, 0:width]
            wgu_ref[:, FFN + g0:FFN + g0 + width] += a_ref[j, :, WIN_GU:WIN_GU + width]
        oa_ref[...] = r_ref[B_OA:B_OA + ROW_SHARD, :]
        ob_ref[...] = r_ref[B_OB:B_OB + ROW_SHARD, :]
        o_ref[...] = r_ref[B_O:B_O + ROW_SHARD, :]
        down_ref[...] = r_ref[B_DOWN:B_DOWN + FFN_SHARD, :]

    def rows(n):
        return pl.BlockSpec((n, D_MODEL), lambda i: (i, 0))

    square = jax.ShapeDtypeStruct((D_MODEL, D_MODEL), BF16)
    return pl.pallas_call(
        body, grid=(N_DEV,),
        in_specs=[pl.BlockSpec((N_DEV, tm, GU_COLS), lambda i: (0, i, 0)),
                  pl.BlockSpec((None, B_FG2, D_MODEL), lambda i: (i, 0, 0))] + [_after_spec(t) for t in after],
        out_specs=[pl.BlockSpec((tm, 2 * FFN), lambda i: (i, 0)),
                   rows(ROW_SHARD), rows(ROW_SHARD), rows(ROW_SHARD), rows(FFN_SHARD)],
        out_shape=[jax.ShapeDtypeStruct((D_MODEL, 2 * FFN), BF16),
                   square, square, square, jax.ShapeDtypeStruct((FFN, D_MODEL), BF16)],
        name=name, compiler_params=_cparams(("parallel",)),
    )(a_all, rows_all, *after)


def _grad_windows_in(d_in, name):
    tm = 256

    def body(din_ref, out_ref):
        for j in range(N_DEV):
            c0 = T_IN[j] * LANES
            out_ref[j & 1, j >> 1, :, 0:WIN_IN] = din_ref[:, c0:c0 + WIN_IN]
            out_ref[j & 1, j >> 1, :, A_FZ:IN_COLS] = din_ref[:, FZ_COL:PROJ_W]

    return pl.pallas_call(
        body, grid=(D_MODEL // tm,), in_specs=[pl.BlockSpec((tm, PROJ_W), lambda i: (i, 0))],
        out_specs=pl.BlockSpec((2, 4, tm, IN_COLS), lambda i: (0, 0, i, 0)),
        out_shape=jax.ShapeDtypeStruct((2, 4, D_MODEL, IN_COLS), BF16), name=name,
        compiler_params=_cparams(("parallel",)),
    )(d_in)


def _grad_windows_gu(d_gu, name):
    tm = 256

    def body(dgu_ref, out_ref):
        for j in range(N_DEV):
            g0, width = T_GU[j] * LANES, _gu_width(j)
            for half, base in ((0, 0), (FFN, WIN_GU)):
                out_ref[j & 1, j >> 1, :, base:base + width] = dgu_ref[:, half + g0:half + g0 + width]
                if width < WIN_GU:
                    out_ref[j & 1, j >> 1, :, base + width:base + WIN_GU] = jnp.zeros((tm, WIN_GU - width), BF16)

    return pl.pallas_call(
        body, grid=(D_MODEL // tm,), in_specs=[pl.BlockSpec((tm, 2 * FFN), lambda i: (i, 0))],
        out_specs=pl.BlockSpec((2, 4, tm, GU_COLS), lambda i: (0, 0, i, 0)),
        out_shape=jax.ShapeDtypeStruct((2, 4, D_MODEL, GU_COLS), BF16), name=name,
        compiler_params=_cparams(("parallel",)),
    )(d_gu)


def _final_windows(prm, chip, a, recv, params, l, kind, name, prev=None):
    tm = 128
    n_in, n_gu = WIN_IN // LANES, FFN_SHARD // LANES + 1
    widths = (W_IN_SHARD,) if kind == "in" else (FFN_SHARD, FFN_SHARD)
    cols = IN_COLS if kind == "in" else GU_COLS
    n_par = 3 * len(widths)
    prev = [] if prev is None else [t for group in prev for t in group]

    def body(prm_ref, chip_ref, a_ref, r_ref, *refs):
        ins, outs = refs[:n_par], refs[n_par + len(prev):]
        lane = _lane_iota()
        s_main, s_code, clo, chi, s_gu = [prm_ref[i] for i in (0, 3, 4, 5, 7)]

        def total(c0):
            sl = slice(c0, c0 + LANES)
            return ((a_ref[:, sl].astype(F32) + r_ref[0, :, sl].astype(F32)) + r_ref[1, :, sl].astype(F32)) \
                + r_ref[2, :, sl].astype(F32)

        if kind == "in":
            grads = _tiles_down([total(t * LANES) for t in range(n_in)], s_main, lane)
            code = pltpu.roll(total(A_FZ), (LANES - s_code) % LANES, 1)
            for t in (0, n_in - 2, n_in - 1):
                col = lane + t * LANES
                grads[t] = jnp.where((col >= clo) & (col < chi), code, grads[t])
            per_weight = [grads]
        else:
            per_weight = [_tiles_down([total(base + t * LANES) for t in range(n_gu + 1)], s_gu, lane)[:n_gu]
                          for base in (0, WIN_GU)]
        for k, (tiles, width) in enumerate(zip(per_weight, widths)):
            w_ref, m_ref, v_ref = ins[3 * k:3 * k + 3]
            g_ref, d_ref, nm_ref, nv_ref = outs[4 * k:4 * k + 4]
            for t, g in enumerate(tiles):
                n = min(LANES, width - t * LANES)
                sl = slice(t * LANES, t * LANES + n)
                g = g[:, 0:n]
                d, nm, nv = _adamw_math(w_ref[:, sl], g, m_ref[:, sl], v_ref[:, sl])
                g_ref[:, sl] = g
                d_ref[:, sl] = d
                nm_ref[:, sl] = nm
                nv_ref[:, sl] = nv

    def native(width):
        return pl.BlockSpec((None, tm, width), lambda i, p, c: (l, i, 0))

    in_specs = [pl.BlockSpec((None, tm, cols), lambda i, p, c: (c[0], i, 0)),
                pl.BlockSpec((3, tm, cols), lambda i, p, c: (0, i, 0))]
    in_specs += [native(wd) for wd in widths for _ in range(3)] + [_ANY] * len(prev)
    grid_spec = pltpu.PrefetchScalarGridSpec(
        num_scalar_prefetch=2, grid=(D_MODEL // tm,), in_specs=in_specs,
        out_specs=[native(wd) for wd in widths for _ in range(4)])
    out = pl.pallas_call(
        body, grid_spec=grid_spec, name=name, compiler_params=_cparams(("parallel",)),
        out_shape=[jax.ShapeDtypeStruct((DEPTH, D_MODEL, wd), F32) for wd in widths for _ in range(4)],
        input_output_aliases={4 + n_par + k: k for k in range(len(prev))},
    )(prm, chip, a, recv, *params, *prev)
    return [out[4 * k:4 * k + 4] for k in range(len(widths))]


def _me():
    return lax.axis_index("x"), lax.axis_index("y"), lax.axis_index("c")


_CHIP_FLIPS = ((1, 0), (0, 1), (1, 1))
_ANY = pl.BlockSpec(memory_space=pl.ANY)


def _comm_call(body, peers, out_shape, sems, name, args, collective_id):
    if collective_id is None:
        n_in = len(args)
        return pl.pallas_call(body, out_shape=out_shape, name=name, in_specs=[_ANY] * n_in,
                              out_specs=[_ANY] * len(out_shape), scratch_shapes=sems)(*args)

    def sequencer_body(*refs):
        barrier = pltpu.get_barrier_semaphore()
        targets = peers()
        for peer in targets:
            pl.semaphore_signal(barrier, inc=1, device_id=peer, device_id_type=MESH)
        pl.semaphore_wait(barrier, len(targets))
        body(*refs)

    sequencer = plsc.ScalarSubcoreMesh(axis_name="seq", num_cores=1)
    return pl.kernel(sequencer_body, out_type=out_shape, mesh=sequencer, scratch_types=sems, name=name,
                     compiler_params=pltpu.CompilerParams(collective_id=collective_id))(*args)


def _sibling_peer():
    x, y, cc = _me()
    return [(x, y, 1 - cc)]


def _chip_peers():
    x, y, cc = _me()
    return [(x ^ fx, y ^ fy, cc) for fx, fy in _CHIP_FLIPS]


def _all_gather(shards, name, collective_id=None):
    n = len(shards)
    split = [s.shape[0] % (2 * SUBLANES_BF16) == 0 for s in shards]

    def body(*refs):
        x_refs, out_refs = refs[:n], refs[n:2 * n]
        send_sems, recv_sems, local_sems = refs[2 * n:]
        x, y, cc = _me()
        me, sibling = (x, y, cc), (x, y, 1 - cc)
        near, far = [(x ^ 1, y), (x, y ^ 1)], (x ^ 1, y ^ 1)

        def copy(a, k, block, to, half=None, from_shard=False):
            px, py, pc = block
            slot = out_refs[a].at[4 * px + 2 * py + pc]
            if half is not None:
                rows = shards[a].shape[0] // 2
                slot = slot.at[pl.ds(half * rows, rows)]
            return pltpu.make_async_remote_copy(
                src_ref=x_refs[a] if from_shard else slot, dst_ref=slot,
                send_sem=send_sems.at[a, k], recv_sem=recv_sems.at[a, k], device_id=to, device_id_type=MESH)

        mine = [pltpu.make_async_copy(x_refs[a], out_refs[a].at[4 * x + 2 * y + cc], local_sems.at[a])
                for a in range(n)]
        for cp in mine:
            cp.start()
        sent = [copy(a, 0, me, sibling, from_shard=True) for a in range(n)]
        sent += [copy(a, 1 + j, me, (*chip, cc), from_shard=True) for j, chip in enumerate(near) for a in range(n)]
        sent += [copy(a, 3, me, (*far, cc), from_shard=True) for a in range(n) if not split[a]]
        for cp in sent:
            cp.start()

        def pass_on(cp):
            cp.start()
            sent.append(cp)

        for j, chip in enumerate(near):
            for a in range(n):
                copy(a, 1 + j, (*chip, cc), me).wait_recv()
                pass_on(copy(a, 4 + j, (*chip, cc), sibling))
                if split[a]:
                    pass_on(copy(a, 7 + j, (*chip, cc), (*near[1 - j], cc), half=j))
        for a in range(n):
            if split[a]:
                copy(a, 7, (*far, cc), me, half=0).wait_recv()
                copy(a, 8, (*far, cc), me, half=1).wait_recv()
            else:
                copy(a, 3, (*far, cc), me).wait_recv()
            pass_on(copy(a, 6, (*far, cc), sibling))
        for a in range(n):
            copy(a, 0, sibling, me).wait_recv()
            for j, chip in enumerate(near + [far]):
                copy(a, 4 + j, (*chip, 1 - cc), me).wait_recv()
        for cp in sent:
            cp.wait_send()
        for cp in mine:
            cp.wait()

    return _comm_call(
        body, lambda: _sibling_peer() + _chip_peers(),
        [jax.ShapeDtypeStruct((N_DEV,) + s.shape, s.dtype) for s in shards],
        [pltpu.SemaphoreType.DMA((n, 9)), pltpu.SemaphoreType.DMA((n, 9)), pltpu.SemaphoreType.DMA((n,))],
        name, shards, collective_id)


def _send_to_sibling(parts, name, collective_id=None):
    n = len(parts)

    def body(*refs):
        g_refs, out_refs = refs[:n], refs[n:2 * n]
        send_sems, recv_sems = refs[2 * n:]
        x, y, cc = _me()
        copies = [pltpu.make_async_remote_copy(
            src_ref=g_refs[a].at[1 - cc], dst_ref=out_refs[a], send_sem=send_sems.at[a], recv_sem=recv_sems.at[a],
            device_id=(x, y, 1 - cc), device_id_type=MESH) for a in range(n)]
        for cp in copies:
            cp.start()
        for cp in copies:
            cp.wait()

    return _comm_call(
        body, _sibling_peer, [jax.ShapeDtypeStruct(p.shape[1:], p.dtype) for p in parts],
        [pltpu.SemaphoreType.DMA((n,)), pltpu.SemaphoreType.DMA((n,))], name, parts, collective_id)


def _send_to_chips(parts, name, collective_id=None):
    n = len(parts)

    def body(*refs):
        a_refs, out_refs = refs[:n], refs[n:2 * n]
        send_sems, recv_sems = refs[2 * n:]
        x, y, cc = _me()
        copies = []
        for k, (fx, fy) in enumerate(_CHIP_FLIPS):
            px, py = x ^ fx, y ^ fy
            for a in range(n):
                copies.append(pltpu.make_async_remote_copy(
                    src_ref=a_refs[a].at[2 * px + py], dst_ref=out_refs[a].at[k], send_sem=send_sems.at[a, k],
                    recv_sem=recv_sems.at[a, k], device_id=(px, py, cc), device_id_type=MESH))
                copies[-1].start()
        for cp in copies:
            cp.wait()

    return _comm_call(
        body, _chip_peers, [jax.ShapeDtypeStruct((3,) + p.shape[1:], p.dtype) for p in parts],
        [pltpu.SemaphoreType.DMA((n, 3)), pltpu.SemaphoreType.DMA((n, 3))], name, parts, collective_id)


def _pack_rows(w_oa, w_ob, w_o, w_down, w_fg2, conv_w, l):
    conv_bits = lax.bitcast_convert_type(conv_w[l].reshape(-1), BF16).reshape(1, -1)
    tail = jnp.concatenate([w_fg2[l].astype(BF16).reshape(1, D_MODEL),
                            jnp.pad(conv_bits, ((0, 0), (0, D_MODEL - conv_bits.shape[1])))], axis=0)
    tail = jnp.pad(tail, ((0, B_ROWS - B_FG2 - tail.shape[0]), (0, 0)))
    rows = jnp.concatenate([w_oa[l].astype(BF16), w_ob[l].astype(BF16), w_o[l].astype(BF16),
                            w_down[l].astype(BF16)], axis=0)
    return rows, tail


def _unpack_tail(tail):
    w_fg2 = tail[:, 0, :].reshape(N_DEV, RANK, KEY_W // N_DEV).transpose(1, 0, 2).reshape(RANK, KEY_W)
    conv_bits = tail[:, B_CONV - B_FG2, :2 * 3 * ROW_SHARD].reshape(N_DEV, 3 * ROW_SHARD, 2)
    conv_w = lax.bitcast_convert_type(conv_bits, F32).reshape(N_DEV, 3, ROW_SHARD)
    return jnp.pad(w_fg2, ((0, LANES - RANK), (0, 0))), conv_w.transpose(1, 0, 2).reshape(3, D_MODEL)


def _by_core_chip(t):
    return t.reshape((2, 2, 2) + t.shape[1:]).transpose((2, 0, 1) + tuple(range(3, t.ndim + 2))).reshape(
        (2, 4) + t.shape[1:])


def _grad_rows(g):
    fg2 = g["w_fg2"][:RANK].reshape(RANK, N_DEV, KEY_W // N_DEV).transpose(1, 0, 2).reshape(N_DEV, 1, D_MODEL)
    conv = g["conv_w"].astype(BF16).reshape(3, N_DEV, ROW_SHARD).transpose(1, 0, 2).reshape(N_DEV, 1, 3 * ROW_SHARD)
    tail = jnp.concatenate([fg2, jnp.pad(conv, ((0, 0), (0, 0), (0, D_MODEL - 3 * ROW_SHARD)))], axis=1)
    tail = jnp.pad(tail, ((0, 0), (0, B_ROWS - B_FG2 - 2), (0, 0)))
    parts = [g["w_oa"].reshape(N_DEV, ROW_SHARD, D_MODEL), g["w_ob"].reshape(N_DEV, ROW_SHARD, D_MODEL),
             g["w_o"].reshape(N_DEV, ROW_SHARD, D_MODEL), g["w_down"].reshape(N_DEV, FFN_SHARD, D_MODEL), tail]
    return _by_core_chip(jnp.concatenate(parts, axis=1))


def _ungrad_rows(gs):
    return dict(w_oa=gs[B_OA:B_OA + ROW_SHARD], w_ob=gs[B_OB:B_OB + ROW_SHARD], w_o=gs[B_O:B_O + ROW_SHARD],
                w_ffn_down=gs[B_DOWN:B_DOWN + FFN_SHARD], w_fg2=gs[B_FG2].reshape(RANK, KEY_W // N_DEV),
                conv_w=gs[B_CONV, :3 * ROW_SHARD].reshape(3, ROW_SHARD))


def _layer_fwd(x, h, p, l, next_norm_g=None):
    tag = f"l{l}_"
    if h is None:
        h = _rmsnorm_fwd(x, p["norm1_g"], tag + "norm1")
    proj = _matmul(h, p["w_in"], "nn", BF16, tag + "proj")
    o, states, oa = _gla_fwd(proj, p["w_fg2"], p["b_fg"], p["gla_norm_g"], tag + "gla_fwd")
    p.update(p.pop("rest")((o,)))
    ya = _matmul(oa, p["w_oa"], "nn", BF16, tag + "ya")
    cb, yb = _conv_project(proj, p["conv_w"], p["conv_b"], p["w_ob"], tag + "yb")
    mix, x1, h2 = _mix_project(proj, ya, yb, p["w_o"], x, p["norm2_g"], tag + "x1")
    gu = _matmul(h2, p["w_gu"], "nn", BF16, tag + "gu")
    hid, x2, h_next = _swiglu_project(gu, p["w_down"], x1, next_norm_g, tag + "x2")
    saved = dict(x=x, h=h, proj=proj, o=o, states=states, oa=oa, ya=ya, cb=cb, yb=yb, mix=mix, x1=x1, h2=h2,
                 gu=gu, hid=hid)
    return x2, h_next, saved


def _layer_bwd(dx2, dx2h, p, sv, l, reduce=None):
    if reduce is None:
        reduce = lambda group, grads: ((), ())
    tag = f"l{l}_b_"
    dhid = _matmul(dx2h, p["w_down"], "nt", BF16, tag + "dhid")
    d_down = _matmul(sv["hid"], dx2h, "tn", BF16, tag + "dw_down")
    dgu = _swiglu_bwd(dhid, sv["gu"], tag + "swiglu")
    dx1, dx1h, dg2 = _matmul_norm_bwd(dgu, p["w_gu"], sv["x1"], p["norm2_g"], dx2, tag + "dh2")
    d_gu = _matmul(sv["h2"], dgu, "tn", BF16, tag + "dw_gu")
    gu_packed, gu_sums = reduce("gu", d_gu)
    dmix = _matmul(dx1h, p["w_o"], "nt", BF16, tag + "dmix", after=gu_packed)
    d_o = _matmul(sv["mix"], dx1h, "tn", BF16, tag + "dw_o")
    dproj, dya, dyb = _mix_bwd(dmix, sv["proj"], sv["ya"], sv["yb"], tag + "mix")
    dcb = _matmul(dyb, p["w_ob"], "nt", BF16, tag + "dcb", after=gu_sums)
    d_ob = _matmul(sv["cb"], dyb, "tn", BF16, tag + "dw_ob")
    dproj, dwb = _conv_bwd(dcb, sv["proj"], p["conv_w"], p["conv_b"], dproj, tag + "conv")
    doa = _matmul(dya, p["w_oa"], "nt", BF16, tag + "doa")
    d_oa = _matmul(sv["oa"], dya, "tn", BF16, tag + "dw_oa")
    dq, dk, dv, dfg, dbfg, dproj, dgg = _gla_bwd(sv["proj"], p["w_fg2"], p["b_fg"], sv["states"], doa, sv["o"],
                                                 p["gla_norm_g"], dproj, tag + "gla")
    dproj = _place_qkv(dq, dk, dv, dproj, tag + "place_qkv")
    dproj = _matmul(dfg, p["w_fg2"], "nt", BF16, tag + "dfz", into=(dproj, FZ_COL))
    d_fg2 = _matmul(sv["proj"], dfg, "tn", BF16, tag + "dw_fg2", a_cols=(FZ_COL, LANES))
    rows = dict(w_fg2=d_fg2, conv_w=dwb[0:3], w_oa=d_oa, w_ob=d_ob, w_o=d_o, w_down=d_down)
    rows_packed, rows_sums = reduce("rows", rows)
    d_in = _matmul(sv["h"], dproj, "tn", BF16, tag + "dw_in", after=rows_packed)
    _, in_sums = reduce("in", d_in)
    dx, dxh, dg1 = _matmul_norm_bwd(dproj, p["w_in"], sv["x"], p["norm1_g"], dx1, tag + "dh",
                                    after=(d_in,) + tuple(rows_sums) + tuple(in_sums))
    big = dict(w_in=d_in, w_gu=d_gu, **rows)
    pad = lambda t: jnp.pad(t, ((0, 0), (0, D_MODEL - t.shape[1])))
    small = [dg1[0:1], pad(dbfg[0:1]), pad(dgg[0:1]), dwb[3:4], dg2[0:1]]
    return dx, dxh, big, small


def _local_step(x, target, weights_of, final_g, reduce_of=None):
    saved, layers, h = [], [], None
    for l in range(DEPTH):
        layers.append(weights_of(l, x))
        x, h, sv = _layer_fwd(x, h, layers[l], l, layers[l].get("next_norm1_g"))
        saved.append(sv)
    loss, dx, dxh, dgf = _loss_head(x, final_g, target, "loss_head")
    bigs, smalls = [None] * DEPTH, [None] * DEPTH
    for l in reversed(range(DEPTH)):
        dx, dxh, bigs[l], smalls[l] = _layer_bwd(dx, dxh, layers[l], saved[l], l, reduce_of(l) if reduce_of else None)
    loss_row = jnp.pad(loss[0:1], ((0, 0), (0, D_MODEL - loss.shape[1])))
    small = jnp.concatenate(smalls[0] + smalls[1] + [dgf[0:1], loss_row], axis=0)
    small = jnp.pad(small, ((0, SMALL_ROWS - small.shape[0]), (0, 0)))
    return loss[0, 0], dx, bigs, small


def kernel(x, norm1_g, w_in, w_fg2, b_fg, gla_norm_g, w_oa, conv_w, conv_b, w_ob, w_o, norm2_g, w_ffn_gate, w_ffn_up, w_ffn_down, final_g, loss_target, m_norm1_g, m_w_in, m_w_fg2, m_b_fg, m_gla_norm_g, m_w_oa, m_conv_w, m_conv_b, m_w_ob, m_w_o, m_norm2_g, m_w_ffn_gate, m_w_ffn_up, m_w_ffn_down, m_final_g, v_norm1_g, v_w_in, v_w_fg2, v_b_fg, v_gla_norm_g, v_w_oa, v_conv_w, v_conv_b, v_w_ob, v_w_o, v_norm2_g, v_w_ffn_gate, v_w_ffn_up, v_w_ffn_down, v_final_g):
    names = ["norm1_g", "w_in", "w_fg2", "b_fg", "gla_norm_g", "w_oa", "conv_w", "conv_b", "w_ob", "w_o",
             "norm2_g", "w_ffn_gate", "w_ffn_up", "w_ffn_down", "final_g"]
    w = dict(zip(names, [norm1_g, w_in, w_fg2, b_fg, gla_norm_g, w_oa, conv_w, conv_b, w_ob, w_o, norm2_g,
                         w_ffn_gate, w_ffn_up, w_ffn_down, final_g]))
    m = dict(zip(names, [m_norm1_g, m_w_in, m_w_fg2, m_b_fg, m_gla_norm_g, m_w_oa, m_conv_w, m_conv_b, m_w_ob,
                         m_w_o, m_norm2_g, m_w_ffn_gate, m_w_ffn_up, m_w_ffn_down, m_final_g]))
    v = dict(zip(names, [v_norm1_g, v_w_in, v_w_fg2, v_b_fg, v_gla_norm_g, v_w_oa, v_conv_w, v_conv_b, v_w_ob,
                         v_w_o, v_norm2_g, v_w_ffn_gate, v_w_ffn_up, v_w_ffn_down, v_final_g]))
    col_names = ["w_in", "w_ffn_gate", "w_ffn_up"]
    cx, cy, cc = _me()
    prm = jnp.asarray(SHIFT_TABLE)[4 * cx + 2 * cy + cc]
    core = jnp.reshape(cc, (1,)).astype(jnp.int32)
    chip = jnp.reshape(2 * cx + cy, (1,)).astype(jnp.int32)

    ids = iter(range(32))

    gathered, previous = [], None
    for l in range(DEPTH):
        rows, tail = _pack_rows(w_oa, w_ob, w_o, w_ffn_down, w_fg2, conv_w, l)
        win_in, win_gu = _window_cols(prm, w_in, w_ffn_gate, w_ffn_up, l, f"l{l}_windows", after=previous)
        previous = rows
        first = _all_gather([win_in, tail], f"l{l}_gather_in", next(ids))
        gathered.append(list(first) + list(_all_gather([win_gu, rows], f"l{l}_gather_rest", next(ids))))

    def weights_of(l, x_in):
        all_in, all_tail, all_gu, all_rows = gathered[l]
        after = (x_in,) if l > 0 else ()
        w_in_full, tail = _assemble_in(all_in, all_tail, f"l{l}_assemble_in", after)
        w_fg2_full, conv_w_full = _unpack_tail(tail)

        def rest(after_rest):
            names_rest = ("w_gu", "w_oa", "w_ob", "w_o", "w_down")
            return dict(zip(names_rest, _assemble_rest(all_gu, all_rows, f"l{l}_assemble_rest", after + after_rest)))

        return dict(w_in=w_in_full, rest=rest, w_fg2=w_fg2_full,
                    conv_w=conv_w_full, norm1_g=norm1_g[l][None], b_fg=b_fg[l][None],
                    gla_norm_g=gla_norm_g[l][None], conv_b=conv_b[l][None], norm2_g=norm2_g[l][None],
                    next_norm1_g=norm1_g[l + 1][None] if l + 1 < DEPTH else None)

    pending = [dict() for _ in range(DEPTH)]
    landed = []

    def reduce_of(l):
        def reduce(group, grads):
            tag = f"l{l}_{group}"
            if group == "gu":
                packed = _grad_windows_gu(grads, tag + "_windows")
            elif group == "in":
                packed = _grad_windows_in(grads, tag + "_windows")
            else:
                packed = _grad_rows(grads)
            (from_sibling,) = _send_to_sibling([packed], tag + "_to_sibling", next(ids))
            waited = () if group == "in" else tuple(landed)
            if waited:
                landed.clear()
            sums = _pair_sum(packed, from_sibling, core, tag + "_pair_sum", after=waited)
            (from_chips,) = _send_to_chips([sums], tag + "_to_chips", next(ids))
            landed.append(from_chips)
            pending[l][group] = (sums, from_chips)
            return (packed,), (sums,)
        return reduce

    loss, dx, bigs, small = _local_step(x[0], loss_target[0], weights_of, final_g[None], reduce_of)

    grads, deltas, new_m, new_v = {}, {}, {}, {}
    for kind, group_names in (("gu", col_names[1:]), ("in", col_names[:1])):
        params = [t for n in group_names for t in (w[n], m[n], v[n])]
        out = None
        for l in reversed(range(DEPTH)):
            sums, from_chips = pending[l][kind]
            out = _final_windows(prm, chip, sums, from_chips, params, l, kind, f"l{l}_{kind}_final", out)
        for n, (g, d, nm, nv) in zip(group_names, out):
            grads[n], deltas[n], new_m[n], new_v[n] = g, d, nm, nv
    row_grads = [_ungrad_rows(_chip_sum(*pending[l]["rows"], chip, f"l{l}_rows_chip_sum")) for l in range(DEPTH)]
    for n in row_grads[0]:
        grads[n] = jnp.stack([row_grads[l][n] for l in range(DEPTH)])

    small_sum = _sum_devices(_all_gather([small], "gather_small")[0], "sum_small")
    r512, r256 = slice(0, KEY_W), slice(0, HEAD_V)
    grads.update(
        norm1_g=jnp.stack([small_sum[0], small_sum[5]]), b_fg=jnp.stack([small_sum[1, r512], small_sum[6, r512]]),
        gla_norm_g=jnp.stack([small_sum[2, r256], small_sum[7, r256]]),
        conv_b=jnp.stack([small_sum[3], small_sum[8]]), norm2_g=jnp.stack([small_sum[4], small_sum[9]]),
        final_g=small_sum[10])

    for n in names:
        if n not in col_names:
            deltas[n], new_m[n], new_v[n] = _adamw(w[n], grads[n], m[n], v[n], "adamw_" + n)

    total_loss = small_sum[2 * 5 + 1, 0]
    return (total_loss, dx[None], *[grads[n] for n in names], *[deltas[n] for n in names],
            *[new_m[n] for n in names], *[new_v[n] for n in names])
```

```python
import jax
import jax.numpy as jnp
import numpy as np
from jax import lax
from jax.experimental import pallas as pl
from jax.experimental.pallas import tpu as pltpu
from jax.experimental.pallas import tpu_sc as plsc

F32 = jnp.float32
BF16 = jnp.bfloat16
MESH = pl.DeviceIdType.MESH

D_MODEL = 1024
DEPTH = 2
CHUNK = 64
HEADS = 4
HEAD_K = 128
HEAD_V = 256
KEY_W = HEADS * HEAD_K
VAL_W = HEADS * HEAD_V
RANK = 16
TAU = 16.0
FFN = 2816
IN_WIDTH = 2 * KEY_W + 2 * VAL_W + RANK + 5 * D_MODEL
EPS = 1e-6
Q_SCALE = HEAD_K ** -0.5
N_DEV = 8
ADAM_LR, ADAM_B1, ADAM_B2, ADAM_EPS, ADAM_WD, ADAM_STEP = 0.001, 0.9, 0.999, 1e-08, 0.01, 10

LANES = 128
SUBLANES_BF16 = 16
VMEM_LIMIT = 48 * 1024 * 1024
VMEM_LIMIT_WIDE = 56 * 1024 * 1024

FZ_COL = 2 * KEY_W + 2 * VAL_W + 5 * D_MODEL
PROJ_W = FZ_COL + LANES
SEG_R, SEG_GBI, SEG_GCI, SEG_CX, SEG_GA, SEG_GB = 2, 3, 4, 5, 6, 7

W_IN_SHARD = IN_WIDTH // N_DEV
FFN_SHARD = FFN // N_DEV
ROW_SHARD = D_MODEL // N_DEV

WIN_IN = 9 * LANES
WIN_GU = 4 * LANES
A_FZ = WIN_IN
IN_COLS = A_FZ + LANES
GU_COLS = 2 * WIN_GU
ORIG_FZ = 2 * KEY_W + 2 * VAL_W


def _new_col(o):
    if o < ORIG_FZ:
        return o
    if o < ORIG_FZ + RANK:
        return FZ_COL + (o - ORIG_FZ)
    return o - RANK


def _shift_table():
    t_in, rows = [], []
    for j in range(N_DEV):
        new = [_new_col(W_IN_SHARD * j + i) for i in range(W_IN_SHARD)]
        main = [i for i in range(W_IN_SHARD) if new[i] < FZ_COL]
        code = [i for i in range(W_IN_SHARD) if new[i] >= FZ_COL]
        shift = new[main[0]] - main[0]
        t_in.append(shift // LANES)
        assert all(new[i] - i == shift for i in main) and shift % LANES + W_IN_SHARD <= WIN_IN
        if code:
            cshift = new[code[0]] - FZ_COL - code[0]
            crow = [cshift % LANES, code[0], code[-1] + 1, int(cshift < 0)]
        else:
            crow = [0, 0, 0, 0]
        rows.append([shift % LANES, main[0], main[-1] + 1] + crow + [FFN_SHARD * j % LANES])
    return tuple(t_in), np.asarray(rows, np.int32)


T_IN, SHIFT_TABLE = _shift_table()
T_GU = tuple(FFN_SHARD * j // LANES for j in range(N_DEV))

B_OA, B_OB, B_O, B_DOWN = 0, ROW_SHARD, 2 * ROW_SHARD, 3 * ROW_SHARD
B_FG2 = B_DOWN + FFN_SHARD
B_CONV = B_FG2 + 1
B_ROWS = B_FG2 + SUBLANES_BF16
SMALL_ROWS = 32


def _pick(n, candidates):
    for c in candidates:
        if n % c == 0:
            return c
    return n


def _cparams(sem):
    return pltpu.CompilerParams(dimension_semantics=sem, vmem_limit_bytes=VMEM_LIMIT)


def _sigmoid(x):
    return 1.0 / (1.0 + jnp.exp(-x))


def _matmul(a, b, dims, out_dtype, name, after=(), into=None, a_cols=None):
    if dims == "nn":
        (m, k), (k2, n) = a.shape, b.shape
    elif dims == "nt":
        (m, k), (n, k2) = a.shape, b.shape
    else:
        (k, m), (k2, n) = a.shape, b.shape
        m = m if a_cols is None else a_cols[1]
    assert k == k2 and (a_cols is None or dims == "tn"), (a.shape, b.shape, dims)
    tm = _pick(m, (1024, 1408, 512, 256, 128))
    tn = _pick(n, (1664, 1408, 1024, 512, 256, 128))
    tk = _pick(k, (1664, 1408, 1024, 512, 256, 128))
    nk = k // tk
    if dims == "nn":
        a_spec = pl.BlockSpec((tm, tk), lambda i, j, kk: (i, kk))
        b_spec = pl.BlockSpec((tk, tn), lambda i, j, kk: (kk, j))
        contract = (((1,), (0,)), ((), ()))
    elif dims == "nt":
        a_spec = pl.BlockSpec((tm, tk), lambda i, j, kk: (i, kk))
        b_spec = pl.BlockSpec((tn, tk), lambda i, j, kk: (j, kk))
        contract = (((1,), (1,)), ((), ()))
    else:
        first = 0 if a_cols is None else a_cols[0] // tm
        assert a_cols is None or (tm == m and a_cols[0] % tm == 0)
        a_spec = pl.BlockSpec((tk, tm), lambda i, j, kk: (kk, first + i))
        b_spec = pl.BlockSpec((tk, tn), lambda i, j, kk: (kk, j))
        contract = (((0,), (0,)), ((), ()))
    o_spec = pl.BlockSpec((tm, tn), lambda i, j, kk: (i, j))
    out_spec, out_struct, placed, aliases = o_spec, jax.ShapeDtypeStruct((m, n), out_dtype), (), {}
    if into is not None:
        buffer, col = into
        assert col % tn == 0 and buffer.dtype == out_dtype
        out_spec = pl.BlockSpec((tm, tn), lambda i, j, kk: (i, col // tn + j))
        out_struct, placed, aliases = jax.ShapeDtypeStruct(buffer.shape, out_dtype), (buffer,), {2 + len(after): 0}

    def body(*refs):
        a_ref, b_ref = refs[:2]
        o_ref = refs[2 + len(after) + len(placed)]
        kk = pl.program_id(2)
        part = lax.dot_general(a_ref[...], b_ref[...], contract, preferred_element_type=F32)

        def finish(total):
            o_ref[...] = total.astype(o_ref.dtype)

        if nk == 1:
            finish(part)
            return
        acc_ref = refs[-1]

        @pl.when(kk == 0)
        def _():
            acc_ref[...] = part

        @pl.when((kk > 0) & (kk < nk - 1))
        def _():
            acc_ref[...] += part

        @pl.when(kk == nk - 1)
        def _():
            finish(acc_ref[...] + part)

    in_specs = [a_spec, b_spec] + [_after_spec(t) for t in after] + [_ANY] * len(placed)
    args = (a, b) + tuple(after) + placed
    return pl.pallas_call(
        body, grid=(m // tm, n // tn, nk), in_specs=in_specs, out_specs=out_spec,
        out_shape=out_struct, input_output_aliases=aliases,
        scratch_shapes=[pltpu.VMEM((tm, tn), F32)] if nk > 1 else [], name=name,
        compiler_params=_cparams(("parallel", "parallel", "arbitrary")),
    )(*args)


def _rmsnorm_fwd(x, g, name):
    s, d = x.shape
    tm = _pick(s, (512, 256))

    def body(x_ref, g_ref, o_ref):
        xv = x_ref[...]
        r = lax.rsqrt(jnp.mean(xv * xv, axis=-1, keepdims=True) + EPS)
        o_ref[...] = (xv * r * g_ref[...]).astype(o_ref.dtype)

    row = pl.BlockSpec((tm, d), lambda i: (i, 0))
    return pl.pallas_call(
        body, grid=(s // tm,), in_specs=[row, pl.BlockSpec((1, d), lambda i: (0, 0))], out_specs=row,
        out_shape=jax.ShapeDtypeStruct((s, d), BF16), name=name, compiler_params=_cparams(("parallel",)),
    )(x, g)


def _matmul_norm_bwd(a, b, x, g, dres, name, after=()):
    (s, k), (d, k2) = a.shape, b.shape
    assert k == k2 and x.shape == (s, d)
    tm = _pick(s, (1024, 512, 256))
    tk = _pick(k, (1664, 1408, 1024, 512, 256, 128))
    nk = k // tk

    def body(a_ref, b_ref, x_ref, g_ref, dres_ref, *rest):
        dx_ref, dx16_ref, dg_ref, acc_ref = rest[len(after):]
        i, kk = pl.program_id(0), pl.program_id(1)
        part = lax.dot_general(a_ref[...], b_ref[...], _NT, preferred_element_type=F32)

        def finish(dh):
            xv = x_ref[...]
            r = lax.rsqrt(jnp.mean(xv * xv, axis=-1, keepdims=True) + EPS)
            xn = xv * r
            dxn = dh * g_ref[...]
            dx = dres_ref[...] + r * (dxn - xn * jnp.mean(dxn * xn, axis=-1, keepdims=True))
            dx_ref[...] = dx
            dx16_ref[...] = dx.astype(BF16)
            dg = jnp.broadcast_to(jnp.sum(dh * xn, axis=0, keepdims=True), dg_ref.shape)

            @pl.when(i == 0)
            def _():
                dg_ref[...] = dg

            @pl.when(i > 0)
            def _():
                dg_ref[...] += dg

        @pl.when(kk == 0)
        def _():
            acc_ref[...] = part

        @pl.when((kk > 0) & (kk < nk - 1))
        def _():
            acc_ref[...] += part

        @pl.when(kk == nk - 1)
        def _():
            finish(acc_ref[...] + part)

    assert nk > 1
    row = pl.BlockSpec((tm, d), lambda i, kk: (i, 0))
    return pl.pallas_call(
        body, grid=(s // tm, nk),
        in_specs=[pl.BlockSpec((tm, tk), lambda i, kk: (i, kk)), pl.BlockSpec((d, tk), lambda i, kk: (0, kk)), row,
                  pl.BlockSpec((1, d), lambda i, kk: (0, 0)), row] + [_after_spec(t) for t in after],
        out_specs=[row, row, pl.BlockSpec((8, d), lambda i, kk: (0, 0))],
        out_shape=[jax.ShapeDtypeStruct((s, d), F32), jax.ShapeDtypeStruct((s, d), BF16),
                   jax.ShapeDtypeStruct((8, d), F32)],
        scratch_shapes=[pltpu.VMEM((tm, d), F32)], name=name,
        compiler_params=pltpu.CompilerParams(dimension_semantics=("arbitrary", "arbitrary"),
                                             vmem_limit_bytes=VMEM_LIMIT_WIDE),
    )(a, b, x, g, dres, *after)


def _loss_head(x, g, target, name):
    s, d = x.shape
    tm = _pick(s, (512, 256))

    def body(x_ref, g_ref, t_ref, loss_ref, dx_ref, dx16_ref, dg_ref):
        xv = x_ref[...]
        gv = g_ref[...]
        r = lax.rsqrt(jnp.mean(xv * xv, axis=-1, keepdims=True) + EPS)
        xn = xv * r
        err = xn * gv - t_ref[...]
        dy = err * (1.0 / d)
        dxn = dy * gv
        dx = r * (dxn - xn * jnp.mean(dxn * xn, axis=-1, keepdims=True))
        dx_ref[...] = dx
        dx16_ref[...] = dx.astype(BF16)

        @pl.when(pl.program_id(0) == 0)
        def _():
            dg_ref[...] = jnp.zeros_like(dg_ref)
            loss_ref[...] = jnp.zeros_like(loss_ref)

        dg_ref[...] += jnp.broadcast_to(jnp.sum(dy * xn, axis=0, keepdims=True), dg_ref.shape)
        row_loss = jnp.sum(err * err, axis=-1, keepdims=True)
        loss_ref[...] += jnp.broadcast_to((0.5 / d) * jnp.sum(row_loss, axis=0, keepdims=True), loss_ref.shape)

    row = pl.BlockSpec((tm, d), lambda i: (i, 0))
    return pl.pallas_call(
        body, grid=(s // tm,), in_specs=[row, pl.BlockSpec((1, d), lambda i: (0, 0)), row],
        out_specs=[pl.BlockSpec((8, LANES), lambda i: (0, 0)), row, row, pl.BlockSpec((8, d), lambda i: (0, 0))],
        out_shape=[jax.ShapeDtypeStruct((8, LANES), F32), jax.ShapeDtypeStruct((s, d), F32),
                   jax.ShapeDtypeStruct((s, d), BF16), jax.ShapeDtypeStruct((8, d), F32)],
        name=name, compiler_params=_cparams(("arbitrary",)),
    )(x, g, target)


def _chunk_cumsum(x):
    pos = lax.broadcasted_iota(jnp.int32, (x.shape[0], 1), 0) % CHUNK
    shift = 1
    while shift < CHUNK:
        x = x + jnp.where(pos >= shift, pltpu.roll(x, shift, 0), 0.0)
        shift *= 2
    return x


def _block_decay(fz, w, b):
    fg = jnp.dot(fz, w, preferred_element_type=F32) + b
    la = (jnp.minimum(fg, 0.0) - jnp.log(1.0 + jnp.exp(-jnp.abs(fg)))) * (1.0 / TAU)
    cum = _chunk_cumsum(la)
    ends = [cum[i + CHUNK - 1:i + CHUNK, :] for i in range(0, fz.shape[0], CHUNK)]
    end = jnp.concatenate([jnp.broadcast_to(e, (CHUNK, e.shape[1])) for e in ends], axis=0)
    return fg, jnp.exp(end - cum), [jnp.exp(e) for e in ends]


_TN = (((0,), (0,)), ((), ()))
_NT = (((1,), (1,)), ((), ()))


def _gla_specs(rows):
    q_spec = pl.BlockSpec((rows, HEAD_K), lambda h, c: (c, h))
    k_spec = pl.BlockSpec((rows, HEAD_K), lambda h, c: (c, HEADS + h))
    v_spec = pl.BlockSpec((rows, HEAD_V), lambda h, c: (c, HEADS + h))
    fz_spec = pl.BlockSpec((rows, LANES), lambda h, c: (c, FZ_COL // LANES))
    w_spec = pl.BlockSpec((LANES, HEAD_K), lambda h, c: (0, h))
    b_spec = pl.BlockSpec((1, HEAD_K), lambda h, c: (0, h))
    return q_spec, k_spec, v_spec, fz_spec, w_spec, b_spec


def _gla_fwd(proj, wfg, bfg, gla_g, name):
    s = proj.shape[0]
    nc = s // CHUNK
    per = _pick(nc, (8, 4, 2, 1))
    rows = per * CHUNK

    def body(q_ref, k_ref, v_ref, fz_ref, w_ref, b_ref, r_ref, g_ref, o_ref, st_ref, oa_ref, state, update):
        @pl.when(pl.program_id(1) == 0)
        def _():
            state[...] = jnp.zeros_like(state)

        _, dec, gammas = _block_decay(fz_ref[...], w_ref[...], b_ref[...])
        kd = (k_ref[...].astype(F32) * dec).astype(BF16)
        qs = (q_ref[...].astype(F32) * Q_SCALE).astype(BF16)
        for i in range(per):
            sl = slice(i * CHUNK, (i + 1) * CHUNK)
            update[i] = lax.dot_general(v_ref[sl, :], kd[sl], _TN, preferred_element_type=F32)
        st = state[...]
        for i in range(per):
            st = st * gammas[i] + update[i]
            st_ref[0, i] = st.astype(BF16)
        state[...] = st
        for i in range(per):
            sl = slice(i * CHUNK, (i + 1) * CHUNK)
            o_ref[sl, :] = lax.dot_general(qs[sl], st_ref[0, i], _NT, preferred_element_type=F32).astype(o_ref.dtype)
        ov = o_ref[...].astype(F32)
        rstd = lax.rsqrt(jnp.mean(ov * ov, axis=-1, keepdims=True) + EPS)
        rv = r_ref[...].astype(F32)
        oa_ref[...] = (ov * rstd * g_ref[...] * (rv * _sigmoid(rv))).astype(oa_ref.dtype)

    q_spec, k_spec, v_spec, fz_spec, w_spec, b_spec = _gla_specs(rows)
    head_v = pl.BlockSpec((rows, HEAD_V), lambda h, c: (c, h))
    return pl.pallas_call(
        body, grid=(HEADS, nc // per),
        in_specs=[q_spec, k_spec, v_spec, fz_spec, w_spec, b_spec,
                  pl.BlockSpec((rows, HEAD_V), lambda h, c: (c, SEG_R * (D_MODEL // HEAD_V) + h)),
                  pl.BlockSpec((1, HEAD_V), lambda h, c: (0, 0))],
        out_specs=[head_v, pl.BlockSpec((1, per, HEAD_V, HEAD_K), lambda h, c: (h, c, 0, 0)), head_v],
        out_shape=[jax.ShapeDtypeStruct((s, VAL_W), BF16),
                   jax.ShapeDtypeStruct((HEADS, nc, HEAD_V, HEAD_K), BF16), jax.ShapeDtypeStruct((s, VAL_W), BF16)],
        scratch_shapes=[pltpu.VMEM((HEAD_V, HEAD_K), F32), pltpu.VMEM((per, HEAD_V, HEAD_K), F32)], name=name,
        compiler_params=_cparams(("parallel", "arbitrary")),
    )(proj, proj, proj, proj, wfg, bfg, proj, gla_g)


def _gla_bwd(proj, wfg, bfg, states, doa, o, gla_g, dproj, name):
    s = proj.shape[0]
    nc = s // CHUNK
    per = _pick(nc, (8, 4, 2, 1))
    rows = per * CHUNK
    nblk = nc // per

    def rev(spec_fn):
        return lambda h, j: spec_fn(h, nblk - 1 - j)

    def body(q_ref, k_ref, v_ref, fz_ref, w_ref, b_ref, doa_ref, o_ref, r_ref, g_ref,
             st_ref, prev_ref, _, dq_ref, dk_ref, dv_ref, dfg_ref, db_ref, dr_ref, dg_ref,
             carry, gt_all, dkd_all, dgg_all):
        j = pl.program_id(1)

        @pl.when(j == 0)
        def _():
            carry[...] = jnp.zeros_like(carry)
            db_ref[...] = jnp.zeros_like(db_ref)

        @pl.when((j == 0) & (pl.program_id(0) == 0))
        def _():
            dg_ref[...] = jnp.zeros_like(dg_ref)

        ov = o_ref[...].astype(F32)
        rstd = lax.rsqrt(jnp.mean(ov * ov, axis=-1, keepdims=True) + EPS)
        ohat = ov * rstd
        rv = r_ref[...].astype(F32)
        sg = _sigmoid(rv)
        doav = doa_ref[...].astype(F32)
        gv = g_ref[...]
        dr_ref[...] = (doav * ohat * gv * (sg * (1.0 + rv * (1.0 - sg)))).astype(dr_ref.dtype)
        don = doav * (rv * sg)
        dg_ref[...] += jnp.broadcast_to(jnp.sum(don * ohat, axis=0, keepdims=True), dg_ref.shape)
        dohat = don * gv
        do = (rstd * (dohat - ohat * jnp.mean(dohat * ohat, axis=-1, keepdims=True))).astype(BF16)

        fg, dec, gammas = _block_decay(fz_ref[...], w_ref[...], b_ref[...])
        kd = k_ref[...].astype(F32) * dec
        kd16 = kd.astype(BF16)
        qs = (q_ref[...].astype(F32) * Q_SCALE).astype(BF16)
        for i in range(per):
            sl = slice(i * CHUNK, (i + 1) * CHUNK)
            gt_all[i] = lax.dot_general(do[sl], qs[sl], _TN, preferred_element_type=F32)
        back = carry[...]
        for i in reversed(range(per)):
            gt = back + gt_all[i]
            gt_all[i] = gt
            back = gt * gammas[i]
        carry[...] = back
        has_prev = (j < nblk - 1).astype(F32)
        for i in range(per):
            sl = slice(i * CHUNK, (i + 1) * CHUNK)
            gt = gt_all[i]
            gt16 = gt.astype(BF16)
            dq_ref[sl, :] = (jnp.dot(do[sl], st_ref[0, i], preferred_element_type=F32) * Q_SCALE).astype(dq_ref.dtype)
            dkd_all[sl, :] = jnp.dot(v_ref[sl, :], gt16, preferred_element_type=F32)
            dv_ref[sl, :] = lax.dot_general(kd16[sl], gt16, _NT, preferred_element_type=F32).astype(dv_ref.dtype)
            if i > 0:
                st_prev = st_ref[0, i - 1].astype(F32)
            else:
                st_prev = prev_ref[0, 0].astype(F32) * has_prev
            dgamma = jnp.sum(gt * st_prev, axis=0, keepdims=True)
            dgg_all[sl, :] = jnp.broadcast_to(dgamma * gammas[i], (CHUNK, HEAD_K))
        dkd = dkd_all[...]
        dk_ref[...] = (dkd * dec).astype(dk_ref.dtype)
        e = dkd * kd
        dla = dgg_all[...] + (_chunk_cumsum(e) - e)
        dfg = dla * (1.0 / TAU) * _sigmoid(-fg)
        dfg_ref[...] = dfg.astype(dfg_ref.dtype)
        db_ref[...] += jnp.broadcast_to(jnp.sum(dfg, axis=0, keepdims=True), db_ref.shape)

    q_spec, k_spec, v_spec, fz_spec, w_spec, b_spec = _gla_specs(rows)
    q_spec, k_spec, v_spec, fz_spec = [
        pl.BlockSpec(sp.block_shape, rev(sp.index_map)) for sp in (q_spec, k_spec, v_spec, fz_spec)]
    do_spec = pl.BlockSpec((rows, HEAD_V), lambda h, j: (nblk - 1 - j, h))
    st_spec = pl.BlockSpec((1, per, HEAD_V, HEAD_K), lambda h, j: (h, nblk - 1 - j, 0, 0))
    prev_spec = pl.BlockSpec((1, 1, HEAD_V, HEAD_K),
                             lambda h, j: (h, jnp.maximum((nblk - 1 - j) * per - 1, 0), 0, 0))
    key_out = pl.BlockSpec((rows, HEAD_K), lambda h, j: (nblk - 1 - j, h))
    r_spec = pl.BlockSpec((rows, HEAD_V), lambda h, j: (nblk - 1 - j, SEG_R * (D_MODEL // HEAD_V) + h))
    return pl.pallas_call(
        body, grid=(HEADS, nblk),
        in_specs=[q_spec, k_spec, v_spec, fz_spec, w_spec, b_spec,
                  do_spec, do_spec, r_spec, pl.BlockSpec((1, HEAD_V), lambda h, j: (0, 0)),
                  st_spec, prev_spec, _ANY],
        out_specs=[key_out, key_out, do_spec, key_out, pl.BlockSpec((8, HEAD_K), lambda h, j: (0, h)),
                   r_spec, pl.BlockSpec((8, HEAD_V), lambda h, j: (0, 0))],
        out_shape=[jax.ShapeDtypeStruct((s, KEY_W), BF16), jax.ShapeDtypeStruct((s, KEY_W), BF16),
                   jax.ShapeDtypeStruct((s, VAL_W), BF16), jax.ShapeDtypeStruct((s, KEY_W), BF16),
                   jax.ShapeDtypeStruct((8, KEY_W), F32), jax.ShapeDtypeStruct(dproj.shape, BF16),
                   jax.ShapeDtypeStruct((8, HEAD_V), F32)],
        input_output_aliases={12: 5},
        scratch_shapes=[pltpu.VMEM((HEAD_V, HEAD_K), F32), pltpu.VMEM((per, HEAD_V, HEAD_K), F32),
                        pltpu.VMEM((rows, HEAD_K), F32), pltpu.VMEM((rows, HEAD_K), F32)], name=name,
        compiler_params=_cparams(("arbitrary", "arbitrary")),
    )(proj, proj, proj, proj, wfg, bfg, doa, o, proj, gla_g, states, states, dproj)


def _place_qkv(dq, dk, dv, dproj, name):
    s = dq.shape[0]
    tm = _pick(s, (512, 256))

    def body(dq_ref, dk_ref, dv_ref, _, o_ref):
        o_ref[:, 0:KEY_W] = dq_ref[...]
        o_ref[:, KEY_W:2 * KEY_W] = dk_ref[...]
        o_ref[:, 2 * KEY_W:2 * KEY_W + VAL_W] = dv_ref[...]

    def rows(width):
        return pl.BlockSpec((tm, width), lambda i: (i, 0))

    return pl.pallas_call(
        body, grid=(s // tm,), in_specs=[rows(KEY_W), rows(KEY_W), rows(VAL_W), _ANY],
        out_specs=rows(2 * KEY_W + VAL_W), out_shape=jax.ShapeDtypeStruct(dproj.shape, BF16),
        input_output_aliases={3: 0}, name=name, compiler_params=_cparams(("parallel",)),
    )(dq, dk, dv, dproj)


def _seg(tm, seg):
    return pl.BlockSpec((tm, D_MODEL), lambda i: (i, seg))


HALO = SUBLANES_BF16


def _shift_down(u, p1, p2, n, rows):
    rolled = pltpu.roll(u, n, 0)
    if n == 1:
        return jnp.where(rows == 0, p1, rolled)
    return jnp.where(rows == 0, p2, jnp.where(rows == 1, p1, rolled))


def _shift_up(u, n1, n2, n, rows, tm):
    rolled = pltpu.roll(u, tm - n, 0)
    if n == 1:
        return jnp.where(rows == tm - 1, n1, rolled)
    return jnp.where(rows == tm - 2, n1, jnp.where(rows == tm - 1, n2, rolled))


def _conv_terms(gc_ref, cx_ref, gcp_ref, cxp_ref, tm):
    i = pl.program_id(0)
    u = gc_ref[...].astype(F32) * cx_ref[...].astype(F32)
    up = gcp_ref[...].astype(F32) * cxp_ref[...].astype(F32) * (i > 0).astype(F32)
    rows = lax.broadcasted_iota(jnp.int32, (tm, 1), 0)
    u1 = _shift_down(u, up[HALO - 1:HALO, :], up[HALO - 2:HALO - 1, :], 1, rows)
    u2 = _shift_down(u, up[HALO - 1:HALO, :], up[HALO - 2:HALO - 1, :], 2, rows)
    return u, u1, u2, rows


def _prev_halo(tm, seg):
    return pl.BlockSpec((HALO, D_MODEL), lambda i: (jnp.maximum(i * (tm // HALO) - 1, 0), seg))


def _conv_project(proj, w, b, w_ob, name):
    s = proj.shape[0]
    tm = _pick(s, (512, 256))

    def body(gbi_ref, gc_ref, cx_ref, gcp_ref, cxp_ref, w_ref, b_ref, wob_ref, cb_ref, yb_ref):
        u, u1, u2, _ = _conv_terms(gc_ref, cx_ref, gcp_ref, cxp_ref, tm)
        conv = w_ref[0:1, :] * u2 + w_ref[1:2, :] * u1 + w_ref[2:3, :] * u + b_ref[...]
        cb = (gbi_ref[...].astype(F32) * conv).astype(BF16)
        cb_ref[...] = cb
        yb_ref[...] = jnp.dot(cb, wob_ref[...], preferred_element_type=F32).astype(BF16)

    row = pl.BlockSpec((tm, D_MODEL), lambda i: (i, 0))
    return pl.pallas_call(
        body, grid=(s // tm,),
        in_specs=[_seg(tm, SEG_GBI), _seg(tm, SEG_GCI), _seg(tm, SEG_CX),
                  _prev_halo(tm, SEG_GCI), _prev_halo(tm, SEG_CX),
                  pl.BlockSpec((3, D_MODEL), lambda i: (0, 0)), pl.BlockSpec((1, D_MODEL), lambda i: (0, 0)),
                  pl.BlockSpec((D_MODEL, D_MODEL), lambda i: (0, 0))],
        out_specs=[row, row],
        out_shape=[jax.ShapeDtypeStruct((s, D_MODEL), BF16), jax.ShapeDtypeStruct((s, D_MODEL), BF16)],
        name=name, compiler_params=_cparams(("parallel",)),
    )(proj, proj, proj, proj, proj, w, b, w_ob)


def _conv_bwd(dcb, proj, w, b, dproj, name):
    s = proj.shape[0]
    tm = _pick(s, (512, 256))
    nt = s // tm

    def body(dcb_ref, gbi_ref, gc_ref, cx_ref, gcp_ref, cxp_ref, dcbn_ref, gbin_ref, w_ref, b_ref, _,
             d3_ref, dwb_ref):
        i = pl.program_id(0)

        @pl.when(i == 0)
        def _():
            dwb_ref[...] = jnp.zeros_like(dwb_ref)

        u, u1, u2, rows = _conv_terms(gc_ref, cx_ref, gcp_ref, cxp_ref, tm)
        w0, w1, w2 = w_ref[0:1, :], w_ref[1:2, :], w_ref[2:3, :]
        conv = w0 * u2 + w1 * u1 + w2 * u + b_ref[...]
        dcbv = dcb_ref[...].astype(F32)
        gbi = gbi_ref[...].astype(F32)
        dconv = dcbv * gbi
        dnext = dcbn_ref[...].astype(F32) * gbin_ref[...].astype(F32) * (i < nt - 1).astype(F32)
        dc1 = _shift_up(dconv, dnext[0:1, :], dnext[1:2, :], 1, rows, tm)
        dc2 = _shift_up(dconv, dnext[0:1, :], dnext[1:2, :], 2, rows, tm)
        du = w2 * dconv + w1 * dc1 + w0 * dc2
        d3_ref[:, 0:D_MODEL] = (dcbv * conv).astype(d3_ref.dtype)
        d3_ref[:, D_MODEL:2 * D_MODEL] = (du * cx_ref[...].astype(F32)).astype(d3_ref.dtype)
        d3_ref[:, 2 * D_MODEL:3 * D_MODEL] = (du * gc_ref[...].astype(F32)).astype(d3_ref.dtype)
        dwb_ref[0:1, :] += jnp.sum(dconv * u2, axis=0, keepdims=True)
        dwb_ref[1:2, :] += jnp.sum(dconv * u1, axis=0, keepdims=True)
        dwb_ref[2:3, :] += jnp.sum(dconv * u, axis=0, keepdims=True)
        dwb_ref[3:4, :] += jnp.sum(dconv, axis=0, keepdims=True)

    def next_halo(seg_fn):
        return pl.BlockSpec((HALO, D_MODEL), lambda i: (jnp.minimum((i + 1) * (tm // HALO), s // HALO - 1), seg_fn))

    return pl.pallas_call(
        body, grid=(nt,),
        in_specs=[pl.BlockSpec((tm, D_MODEL), lambda i: (i, 0)),
                  _seg(tm, SEG_GBI), _seg(tm, SEG_GCI), _seg(tm, SEG_CX),
                  _prev_halo(tm, SEG_GCI), _prev_halo(tm, SEG_CX),
                  next_halo(0), next_halo(SEG_GBI),
                  pl.BlockSpec((3, D_MODEL), lambda i: (0, 0)), pl.BlockSpec((1, D_MODEL), lambda i: (0, 0)), _ANY],
        out_specs=[pl.BlockSpec((tm, 3 * D_MODEL), lambda i: (i, SEG_GBI // 3)),
                   pl.BlockSpec((8, D_MODEL), lambda i: (0, 0))],
        out_shape=[jax.ShapeDtypeStruct(dproj.shape, BF16), jax.ShapeDtypeStruct((8, D_MODEL), F32)],
        input_output_aliases={10: 0}, name=name, compiler_params=_cparams(("arbitrary",)),
    )(dcb, proj, proj, proj, proj, proj, dcb, proj, w, b, dproj)


def _mix_project(proj, ya, yb, w_o, x, norm_g, name):
    s = proj.shape[0]
    tm = _pick(s, (512, 256))

    def body(ga_ref, gb_ref, ya_ref, yb_ref, w_ref, x_ref, g_ref, mix_ref, x1_ref, h2_ref):
        mix = (_sigmoid(ga_ref[...].astype(F32)) * ya_ref[...].astype(F32)
               + _sigmoid(gb_ref[...].astype(F32)) * yb_ref[...].astype(F32)).astype(BF16)
        mix_ref[...] = mix
        x1 = x_ref[...] + jnp.dot(mix, w_ref[...], preferred_element_type=F32)
        x1_ref[...] = x1
        rstd = lax.rsqrt(jnp.mean(x1 * x1, axis=-1, keepdims=True) + EPS)
        h2_ref[...] = (x1 * rstd * g_ref[...]).astype(BF16)

    row = pl.BlockSpec((tm, D_MODEL), lambda i: (i, 0))
    whole = pl.BlockSpec((D_MODEL, D_MODEL), lambda i: (0, 0))
    return pl.pallas_call(
        body, grid=(s // tm,),
        in_specs=[_seg(tm, SEG_GA), _seg(tm, SEG_GB), row, row, whole, row, pl.BlockSpec((1, D_MODEL), lambda i: (0, 0))],
        out_specs=[row, row, row],
        out_shape=[jax.ShapeDtypeStruct((s, D_MODEL), BF16), jax.ShapeDtypeStruct((s, D_MODEL), F32),
                   jax.ShapeDtypeStruct((s, D_MODEL), BF16)],
        name=name, compiler_params=_cparams(("parallel",)),
    )(proj, proj, ya, yb, w_o, x, norm_g)


def _mix_bwd(dmix, proj, ya, yb, name):
    s = proj.shape[0]
    tm = _pick(s, (512, 256))

    def body(dm_ref, ga_ref, gb_ref, ya_ref, yb_ref, dg_ref, dya_ref, dyb_ref):
        dm = dm_ref[...].astype(F32)
        sa = _sigmoid(ga_ref[...].astype(F32))
        sb = _sigmoid(gb_ref[...].astype(F32))
        dg_ref[:, 0:D_MODEL] = (dm * ya_ref[...].astype(F32) * sa * (1.0 - sa)).astype(dg_ref.dtype)
        dg_ref[:, D_MODEL:2 * D_MODEL] = (dm * yb_ref[...].astype(F32) * sb * (1.0 - sb)).astype(dg_ref.dtype)
        dya_ref[...] = (dm * sa).astype(dya_ref.dtype)
        dyb_ref[...] = (dm * sb).astype(dyb_ref.dtype)

    row = pl.BlockSpec((tm, D_MODEL), lambda i: (i, 0))
    return pl.pallas_call(
        body, grid=(s // tm,), in_specs=[row, _seg(tm, SEG_GA), _seg(tm, SEG_GB), row, row],
        out_specs=[pl.BlockSpec((tm, 2 * D_MODEL), lambda i: (i, SEG_GA // 2)), row, row],
        out_shape=[jax.ShapeDtypeStruct((s, PROJ_W), BF16), jax.ShapeDtypeStruct((s, D_MODEL), BF16),
                   jax.ShapeDtypeStruct((s, D_MODEL), BF16)],
        name=name, compiler_params=_cparams(("parallel",)),
    )(dmix, proj, proj, ya, yb)


def _swiglu_project(gu, w_down, x1, norm_g, name):
    s = gu.shape[0]
    tm = _pick(s, (1024, 512, 256))
    tk = _pick(FFN, (1408, 256))
    nk = FFN // tk
    has_norm = norm_g is not None

    def body(gate_ref, up_ref, w_ref, x_ref, *rest):
        outs = rest[has_norm:]
        hid_ref, x2_ref, acc_ref = outs[0], outs[1], rest[-1]
        kk = pl.program_id(1)
        gate = gate_ref[...].astype(F32)
        hid = (gate * _sigmoid(gate) * up_ref[...].astype(F32)).astype(BF16)
        hid_ref[...] = hid
        part = jnp.dot(hid, w_ref[...], preferred_element_type=F32)

        @pl.when(kk == 0)
        def _():
            acc_ref[...] = part

        @pl.when((kk > 0) & (kk < nk - 1))
        def _():
            acc_ref[...] += part

        @pl.when(kk == nk - 1)
        def _():
            x2 = acc_ref[...] + part + x_ref[...]
            x2_ref[...] = x2
            if has_norm:
                rstd = lax.rsqrt(jnp.mean(x2 * x2, axis=-1, keepdims=True) + EPS)
                outs[2][...] = (x2 * rstd * rest[0][...]).astype(BF16)

    assert nk > 1
    row = pl.BlockSpec((tm, D_MODEL), lambda i, kk: (i, 0))
    out = pl.pallas_call(
        body, grid=(s // tm, nk),
        in_specs=[pl.BlockSpec((tm, tk), lambda i, kk: (i, kk)), pl.BlockSpec((tm, tk), lambda i, kk: (i, nk + kk)),
                  pl.BlockSpec((tk, D_MODEL), lambda i, kk: (kk, 0)), row]
        + ([pl.BlockSpec((1, D_MODEL), lambda i, kk: (0, 0))] if has_norm else []),
        out_specs=[pl.BlockSpec((tm, tk), lambda i, kk: (i, kk)), row] + ([row] if has_norm else []),
        out_shape=[jax.ShapeDtypeStruct((s, FFN), BF16), jax.ShapeDtypeStruct((s, D_MODEL), F32)]
        + ([jax.ShapeDtypeStruct((s, D_MODEL), BF16)] if has_norm else []),
        scratch_shapes=[pltpu.VMEM((tm, D_MODEL), F32)], name=name,
        compiler_params=pltpu.CompilerParams(dimension_semantics=("parallel", "arbitrary"),
                                             vmem_limit_bytes=VMEM_LIMIT_WIDE),
    )(gu, gu, w_down, x1, *((norm_g,) if has_norm else ()))
    return out[0], out[1], (out[2] if has_norm else None)


def _swiglu_bwd(dhid, gu, name):
    s = gu.shape[0]
    tm = _pick(s, (256,))

    steps, slots = s // tm, 3

    def body(dh_hbm, gu_hbm, o_ref, dh_buf, gu_buf, sems):
        i = pl.program_id(0)

        def fetch(step):
            slot = step % slots
            rows = pl.ds(step * tm, tm)
            return (pltpu.make_async_copy(dh_hbm.at[rows], dh_buf.at[slot], sems.at[0, slot]),
                    pltpu.make_async_copy(gu_hbm.at[rows], gu_buf.at[slot], sems.at[1, slot]))

        @pl.when(i == 0)
        def _():
            for step in range(min(2, steps)):
                for cp in fetch(step):
                    cp.start()

        @pl.when(i + 2 < steps)
        def _():
            for cp in fetch(i + 2):
                cp.start()

        for cp in fetch(i):
            cp.wait()
        slot = i % slots
        gate = gu_buf[slot, :, 0:FFN].astype(F32)
        up = gu_buf[slot, :, FFN:2 * FFN].astype(F32)
        dh = dh_buf[slot].astype(F32)
        sg = _sigmoid(gate)
        o_ref[:, 0:FFN] = (dh * up * (sg * (1.0 + gate * (1.0 - sg)))).astype(o_ref.dtype)
        o_ref[:, FFN:2 * FFN] = (dh * gate * sg).astype(o_ref.dtype)

    return pl.pallas_call(
        body, grid=(steps,), in_specs=[_ANY, _ANY], out_specs=pl.BlockSpec((tm, 2 * FFN), lambda i: (i, 0)),
        out_shape=jax.ShapeDtypeStruct((s, 2 * FFN), BF16), name=name,
        scratch_shapes=[pltpu.VMEM((slots, tm, FFN), BF16), pltpu.VMEM((slots, tm, 2 * FFN), BF16),
                        pltpu.SemaphoreType.DMA((2, slots))],
        compiler_params=_cparams(("arbitrary",)),
    )(dhid, gu)


def _adamw_math(w, g, m, v):
    m2 = ADAM_B1 * m + (1.0 - ADAM_B1) * g
    v2 = ADAM_B2 * v + (1.0 - ADAM_B2) * (g * g)
    m_hat = m2 / (1.0 - ADAM_B1 ** ADAM_STEP)
    v_hat = v2 / (1.0 - ADAM_B2 ** ADAM_STEP)
    delta = -ADAM_LR * (m_hat / (jnp.sqrt(v_hat) + ADAM_EPS) + ADAM_WD * w)
    return delta, m2, v2


def _adamw(w, g, m, v, name):
    shape = w.shape
    cols = shape[-1]
    rows = int(np.prod(shape[:-1])) if len(shape) > 1 else 1
    w2, g2, m2, v2 = [t.reshape(rows, cols) for t in (w, g, m, v)]
    tr = _pick(rows, (512, 352, 256)) if rows % 8 == 0 else rows

    def body(w_ref, g_ref, m_ref, v_ref, d_ref, nm_ref, nv_ref):
        d, nm, nv = _adamw_math(w_ref[...], g_ref[...], m_ref[...], v_ref[...])
        d_ref[...] = d
        nm_ref[...] = nm
        nv_ref[...] = nv

    blk = pl.BlockSpec((tr, cols), lambda i: (i, 0))
    out = pl.pallas_call(
        body, grid=(rows // tr,), in_specs=[blk] * 4, out_specs=[blk] * 3,
        out_shape=[jax.ShapeDtypeStruct((rows, cols), F32)] * 3, name=name,
        compiler_params=_cparams(("parallel",)),
    )(w2, g2, m2, v2)
    return [t.reshape(shape) for t in out]


def _pair_sum(g2, recv, core, name, after=()):
    _, nchip, r, c = g2.shape
    tr = _pick(r, (512,))

    def body(core_ref, a_ref, b_ref, *rest):
        o_ref = rest[-1]
        o_ref[...] = (a_ref[...].astype(F32) + b_ref[...].astype(F32)).astype(o_ref.dtype)

    grid_spec = pltpu.PrefetchScalarGridSpec(
        num_scalar_prefetch=1, grid=(nchip, r // tr),
        in_specs=[pl.BlockSpec((None, None, tr, c), lambda k, i, cr: (cr[0], k, i, 0)),
                  pl.BlockSpec((None, tr, c), lambda k, i, cr: (k, i, 0))] + [_after_spec(t) for t in after],
        out_specs=pl.BlockSpec((None, tr, c), lambda k, i, cr: (k, i, 0)))
    return pl.pallas_call(
        body, grid_spec=grid_spec, out_shape=jax.ShapeDtypeStruct((nchip, r, c), BF16), name=name,
        compiler_params=_cparams(("parallel", "parallel")),
    )(core, g2, recv, *after)


def _chip_sum(a, recv, chip, name, after=()):
    _, r, c = a.shape
    tr = _pick(r, (512,))

    def body(chip_ref, a_ref, b_ref, *rest):
        o_ref = rest[-1]
        o_ref[...] = ((a_ref[...].astype(F32) + b_ref[0].astype(F32)) + b_ref[1].astype(F32)) + b_ref[2].astype(F32)

    grid_spec = pltpu.PrefetchScalarGridSpec(
        num_scalar_prefetch=1, grid=(r // tr,),
        in_specs=[pl.BlockSpec((None, tr, c), lambda i, cr: (cr[0], i, 0)),
                  pl.BlockSpec((3, tr, c), lambda i, cr: (0, i, 0))] + [_after_spec(t) for t in after],
        out_specs=pl.BlockSpec((tr, c), lambda i, cr: (i, 0)))
    return pl.pallas_call(
        body, grid_spec=grid_spec, out_shape=jax.ShapeDtypeStruct((r, c), F32), name=name,
        compiler_params=_cparams(("parallel",)),
    )(chip, a, recv, *after)


def _sum_devices(parts, name):
    n, r, c = parts.shape

    def body(p_ref, o_ref):
        acc = p_ref[0]
        for d in range(1, n):
            acc = acc + p_ref[d]
        o_ref[...] = acc

    return pl.pallas_call(
        body, out_shape=jax.ShapeDtypeStruct((r, c), F32), name=name,
        in_specs=[pl.BlockSpec(memory_space=pltpu.VMEM)], out_specs=pl.BlockSpec(memory_space=pltpu.VMEM),
    )(parts)


def _lane_iota():
    return lax.broadcasted_iota(jnp.int32, (1, LANES), 1)


def _tiles_up(tiles, s, lane):
    rolled = [pltpu.roll(t, s, 1) for t in tiles]
    zero = jnp.zeros_like(tiles[0])
    return [jnp.where(lane < s, p, c) for p, c in zip([zero] + rolled, rolled + [zero])]


def _tiles_down(tiles, s, lane):
    back = (LANES - s) % LANES
    rolled = [pltpu.roll(t, back, 1) for t in tiles]
    zero = jnp.zeros_like(tiles[0])
    return [jnp.where(lane < LANES - s, c, n) for c, n in zip(rolled, rolled[1:] + [zero])]


def _window_cols(prm, w_in, w_gate, w_up, l, name, after=None):
    tm = 256
    n_in, n_gu = WIN_IN // LANES, FFN_SHARD // LANES + 1

    def body(prm_ref, win_ref, g_ref, u_ref, *rest):
        out_ref, gu_ref, scr_in, scr_gu = rest[-4:]
        lane = _lane_iota()
        s_main, lo, hi, s_code, clo, chi, code_hi, s_gu = [prm_ref[i] for i in range(8)]
        scr_in[:, D_MODEL:WIN_IN] = jnp.zeros((tm, WIN_IN - D_MODEL), F32)
        scr_in[:, 0:W_IN_SHARD] = win_ref[...]

        def keep(t, a, b):
            col = lane + t * LANES
            return jnp.where((col >= a) & (col < b), scr_in[:, t * LANES:(t + 1) * LANES], 0.0)

        main = _tiles_up([keep(t, lo, hi) for t in range(n_in)], s_main, lane)
        for t in range(n_in):
            out_ref[:, t * LANES:(t + 1) * LANES] = main[t].astype(BF16)
        low = _tiles_up([keep(0, clo, chi)], s_code, lane)[0]
        high = _tiles_up([keep(n_in - 2, clo, chi), keep(n_in - 1, clo, chi)], s_code, lane)[1]
        out_ref[:, A_FZ:A_FZ + LANES] = jnp.where(code_hi == 1, high, low).astype(BF16)
        for ref, base in ((g_ref, 0), (u_ref, WIN_GU)):
            scr_gu[:, (n_gu - 1) * LANES:n_gu * LANES] = jnp.zeros((tm, LANES), F32)
            scr_gu[:, 0:FFN_SHARD] = ref[...]
            moved = _tiles_up([scr_gu[:, t * LANES:(t + 1) * LANES] for t in range(n_gu)], s_gu, lane)
            for t in range(n_gu + 1):
                gu_ref[:, base + t * LANES:base + (t + 1) * LANES] = moved[t].astype(BF16)

    after_args = [] if after is None else [after]
    grid_spec = pltpu.PrefetchScalarGridSpec(
        num_scalar_prefetch=1, grid=(D_MODEL // tm,),
        in_specs=[pl.BlockSpec((None, tm, W_IN_SHARD), lambda i, p: (l, i, 0)),
                  pl.BlockSpec((None, tm, FFN_SHARD), lambda i, p: (l, i, 0)),
                  pl.BlockSpec((None, tm, FFN_SHARD), lambda i, p: (l, i, 0))] + [_after_spec(t) for t in after_args],
        out_specs=[pl.BlockSpec((tm, IN_COLS), lambda i, p: (i, 0)), pl.BlockSpec((tm, GU_COLS), lambda i, p: (i, 0))],
        scratch_shapes=[pltpu.VMEM((tm, WIN_IN), F32), pltpu.VMEM((tm, n_gu * LANES), F32)])
    return pl.pallas_call(
        body, grid_spec=grid_spec, name=name, compiler_params=_cparams(("parallel",)),
        out_shape=[jax.ShapeDtypeStruct((D_MODEL, IN_COLS), BF16), jax.ShapeDtypeStruct((D_MODEL, GU_COLS), BF16)],
    )(prm, w_in, w_gate, w_up, *after_args)


def _gu_width(j):
    return min(WIN_GU, FFN - T_GU[j] * LANES)


def _after_spec(t):
    tile = (SUBLANES_BF16 if t.dtype == BF16 else 8, LANES)
    return pl.BlockSpec((None,) * (t.ndim - 2) + tile, lambda *_: (0,) * t.ndim)


def _assemble_in(a_all, tail_all, name, after=()):
    tm = D_MODEL // N_DEV

    def body(a_ref, t_ref, *rest):
        win_ref, tail_ref = rest[-2:]
        tail_ref[...] = t_ref[...]
        win_ref[...] = jnp.zeros_like(win_ref)
        code = a_ref[0, :, A_FZ:A_FZ + LANES]
        for j in range(N_DEV):
            c0 = T_IN[j] * LANES
            win_ref[:, c0:c0 + WIN_IN] += a_ref[j, :, 0:WIN_IN]
            if j > 0:
                code = code + a_ref[j, :, A_FZ:A_FZ + LANES]
        win_ref[:, FZ_COL:PROJ_W] = code

    return pl.pallas_call(
        body, grid=(N_DEV,),
        in_specs=[pl.BlockSpec((N_DEV, tm, IN_COLS), lambda i: (0, i, 0)),
                  pl.BlockSpec((None,) + tail_all.shape[1:], lambda i: (i, 0, 0))] + [_after_spec(t) for t in after],
        out_specs=[pl.BlockSpec((tm, PROJ_W), lambda i: (i, 0)),
                   pl.BlockSpec((None,) + tail_all.shape[1:], lambda i: (i, 0, 0))],
        out_shape=[jax.ShapeDtypeStruct((D_MODEL, PROJ_W), BF16), jax.ShapeDtypeStruct(tail_all.shape, BF16)],
        name=name, compiler_params=_cparams(("parallel",)),
    )(a_all, tail_all, *after)


def _assemble_rest(a_all, rows_all, name, after=()):
    tm = D_MODEL // N_DEV
    n_in = 2 + len(after)

    def body(*refs):
        a_ref, r_ref = refs[:2]
        wgu_ref, oa_ref, ob_ref, o_ref, down_ref = refs[n_in:]
        wgu_ref[...] = jnp.zeros_like(wgu_ref)
        for j in range(N_DEV):
            g0, width = T_GU[j] * LANES, _gu_width(j)
            wgu_ref[:, g0:g0 + width] += a_ref[j, :, 0:width]
            wgu_ref[:, FFN + g0:FFN + g0 + width] += a_ref[j, :, WIN_GU:WIN_GU + width]
        oa_ref[...] = r_ref[B_OA:B_OA + ROW_SHARD, :]
        ob_ref[...] = r_ref[B_OB:B_OB + ROW_SHARD, :]
        o_ref[...] = r_ref[B_O:B_O + ROW_SHARD, :]
        down_ref[...] = r_ref[B_DOWN:B_DOWN + FFN_SHARD, :]

    def rows(n):
        return pl.BlockSpec((n, D_MODEL), lambda i: (i, 0))

    square = jax.ShapeDtypeStruct((D_MODEL, D_MODEL), BF16)
    return pl.pallas_call(
        body, grid=(N_DEV,),
        in_specs=[pl.BlockSpec((N_DEV, tm, GU_COLS), lambda i: (0, i, 0)),
                  pl.BlockSpec((None, B_FG2, D_MODEL), lambda i: (i, 0, 0))] + [_after_spec(t) for t in after],
        out_specs=[pl.BlockSpec((tm, 2 * FFN), lambda i: (i, 0)),
                   rows(ROW_SHARD), rows(ROW_SHARD), rows(ROW_SHARD), rows(FFN_SHARD)],
        out_shape=[jax.ShapeDtypeStruct((D_MODEL, 2 * FFN), BF16),
                   square, square, square, jax.ShapeDtypeStruct((FFN, D_MODEL), BF16)],
        name=name, compiler_params=_cparams(("parallel",)),
    )(a_all, rows_all, *after)


def _grad_windows_in(d_in, name):
    tm = 256

    def body(din_ref, out_ref):
        for j in range(N_DEV):
            c0 = T_IN[j] * LANES
            out_ref[j & 1, j >> 1, :, 0:WIN_IN] = din_ref[:, c0:c0 + WIN_IN]
            out_ref[j & 1, j >> 1, :, A_FZ:IN_COLS] = din_ref[:, FZ_COL:PROJ_W]

    return pl.pallas_call(
        body, grid=(D_MODEL // tm,), in_specs=[pl.BlockSpec((tm, PROJ_W), lambda i: (i, 0))],
        out_specs=pl.BlockSpec((2, 4, tm, IN_COLS), lambda i: (0, 0, i, 0)),
        out_shape=jax.ShapeDtypeStruct((2, 4, D_MODEL, IN_COLS), BF16), name=name,
        compiler_params=_cparams(("parallel",)),
    )(d_in)


def _grad_windows_gu(d_gu, name):
    tm = 256

    def body(dgu_ref, out_ref):
        for j in range(N_DEV):
            g0, width = T_GU[j] * LANES, _gu_width(j)
            for half, base in ((0, 0), (FFN, WIN_GU)):
                out_ref[j & 1, j >> 1, :, base:base + width] = dgu_ref[:, half + g0:half + g0 + width]
                if width < WIN_GU:
                    out_ref[j & 1, j >> 1, :, base + width:base + WIN_GU] = jnp.zeros((tm, WIN_GU - width), BF16)

    return pl.pallas_call(
        body, grid=(D_MODEL // tm,), in_specs=[pl.BlockSpec((tm, 2 * FFN), lambda i: (i, 0))],
        out_specs=pl.BlockSpec((2, 4, tm, GU_COLS), lambda i: (0, 0, i, 0)),
        out_shape=jax.ShapeDtypeStruct((2, 4, D_MODEL, GU_COLS), BF16), name=name,
        compiler_params=_cparams(("parallel",)),
    )(d_gu)


def _final_windows(prm, chip, a, recv, params, l, kind, name, prev=None):
    tm = 128
    n_in, n_gu = WIN_IN // LANES, FFN_SHARD // LANES + 1
    widths = (W_IN_SHARD,) if kind == "in" else (FFN_SHARD, FFN_SHARD)
    cols = IN_COLS if kind == "in" else GU_COLS
    n_par = 3 * len(widths)
    prev = [] if prev is None else [t for group in prev for t in group]

    def body(prm_ref, chip_ref, a_ref, r_ref, *refs):
        ins, outs = refs[:n_par], refs[n_par + len(prev):]
        lane = _lane_iota()
        s_main, s_code, clo, chi, s_gu = [prm_ref[i] for i in (0, 3, 4, 5, 7)]

        def total(c0):
            sl = slice(c0, c0 + LANES)
            return ((a_ref[:, sl].astype(F32) + r_ref[0, :, sl].astype(F32)) + r_ref[1, :, sl].astype(F32)) \
                + r_ref[2, :, sl].astype(F32)

        if kind == "in":
            grads = _tiles_down([total(t * LANES) for t in range(n_in)], s_main, lane)
            code = pltpu.roll(total(A_FZ), (LANES - s_code) % LANES, 1)
            for t in (0, n_in - 2, n_in - 1):
                col = lane + t * LANES
                grads[t] = jnp.where((col >= clo) & (col < chi), code, grads[t])
            per_weight = [grads]
        else:
            per_weight = [_tiles_down([total(base + t * LANES) for t in range(n_gu + 1)], s_gu, lane)[:n_gu]
                          for base in (0, WIN_GU)]
        for k, (tiles, width) in enumerate(zip(per_weight, widths)):
            w_ref, m_ref, v_ref = ins[3 * k:3 * k + 3]
            g_ref, d_ref, nm_ref, nv_ref = outs[4 * k:4 * k + 4]
            for t, g in enumerate(tiles):
                n = min(LANES, width - t * LANES)
                sl = slice(t * LANES, t * LANES + n)
                g = g[:, 0:n]
                d, nm, nv = _adamw_math(w_ref[:, sl], g, m_ref[:, sl], v_ref[:, sl])
                g_ref[:, sl] = g
                d_ref[:, sl] = d
                nm_ref[:, sl] = nm
                nv_ref[:, sl] = nv

    def native(width):
        return pl.BlockSpec((None, tm, width), lambda i, p, c: (l, i, 0))

    in_specs = [pl.BlockSpec((None, tm, cols), lambda i, p, c: (c[0], i, 0)),
                pl.BlockSpec((3, tm, cols), lambda i, p, c: (0, i, 0))]
    in_specs += [native(wd) for wd in widths for _ in range(3)] + [_ANY] * len(prev)
    grid_spec = pltpu.PrefetchScalarGridSpec(
        num_scalar_prefetch=2, grid=(D_MODEL // tm,), in_specs=in_specs,
        out_specs=[native(wd) for wd in widths for _ in range(4)])
    out = pl.pallas_call(
        body, grid_spec=grid_spec, name=name, compiler_params=_cparams(("parallel",)),
        out_shape=[jax.ShapeDtypeStruct((DEPTH, D_MODEL, wd), F32) for wd in widths for _ in range(4)],
        input_output_aliases={4 + n_par + k: k for k in range(len(prev))},
    )(prm, chip, a, recv, *params, *prev)
    return [out[4 * k:4 * k + 4] for k in range(len(widths))]


def _me():
    return lax.axis_index("x"), lax.axis_index("y"), lax.axis_index("c")


_CHIP_FLIPS = ((1, 0), (0, 1), (1, 1))
_ANY = pl.BlockSpec(memory_space=pl.ANY)


def _comm_call(body, peers, out_shape, sems, name, args, collective_id):
    if collective_id is None:
        n_in = len(args)
        return pl.pallas_call(body, out_shape=out_shape, name=name, in_specs=[_ANY] * n_in,
                              out_specs=[_ANY] * len(out_shape), scratch_shapes=sems)(*args)

    def sequencer_body(*refs):
        barrier = pltpu.get_barrier_semaphore()
        targets = peers()
        for peer in targets:
            pl.semaphore_signal(barrier, inc=1, device_id=peer, device_id_type=MESH)
        pl.semaphore_wait(barrier, len(targets))
        body(*refs)

    sequencer = plsc.ScalarSubcoreMesh(axis_name="seq", num_cores=1)
    return pl.kernel(sequencer_body, out_type=out_shape, mesh=sequencer, scratch_types=sems, name=name,
                     compiler_params=pltpu.CompilerParams(collective_id=collective_id))(*args)


def _sibling_peer():
    x, y, cc = _me()
    return [(x, y, 1 - cc)]


def _chip_peers():
    x, y, cc = _me()
    return [(x ^ fx, y ^ fy, cc) for fx, fy in _CHIP_FLIPS]


def _all_gather(shards, name, collective_id=None):
    n = len(shards)
    split = [s.shape[0] % (2 * SUBLANES_BF16) == 0 for s in shards]

    def body(*refs):
        x_refs, out_refs = refs[:n], refs[n:2 * n]
        send_sems, recv_sems, local_sems = refs[2 * n:]
        x, y, cc = _me()
        me, sibling = (x, y, cc), (x, y, 1 - cc)
        near, far = [(x ^ 1, y), (x, y ^ 1)], (x ^ 1, y ^ 1)

        def copy(a, k, block, to, half=None, from_shard=False):
            px, py, pc = block
            slot = out_refs[a].at[4 * px + 2 * py + pc]
            if half is not None:
                rows = shards[a].shape[0] // 2
                slot = slot.at[pl.ds(half * rows, rows)]
            return pltpu.make_async_remote_copy(
                src_ref=x_refs[a] if from_shard else slot, dst_ref=slot,
                send_sem=send_sems.at[a, k], recv_sem=recv_sems.at[a, k], device_id=to, device_id_type=MESH)

        mine = [pltpu.make_async_copy(x_refs[a], out_refs[a].at[4 * x + 2 * y + cc], local_sems.at[a])
                for a in range(n)]
        for cp in mine:
            cp.start()
        sent = [copy(a, 0, me, sibling, from_shard=True) for a in range(n)]
        sent += [copy(a, 1 + j, me, (*chip, cc), from_shard=True) for j, chip in enumerate(near) for a in range(n)]
        sent += [copy(a, 3, me, (*far, cc), from_shard=True) for a in range(n) if not split[a]]
        for cp in sent:
            cp.start()

        def pass_on(cp):
            cp.start()
            sent.append(cp)

        for j, chip in enumerate(near):
            for a in range(n):
                copy(a, 1 + j, (*chip, cc), me).wait_recv()
                pass_on(copy(a, 4 + j, (*chip, cc), sibling))
                if split[a]:
                    pass_on(copy(a, 7 + j, (*chip, cc), (*near[1 - j], cc), half=j))
        for a in range(n):
            if split[a]:
                copy(a, 7, (*far, cc), me, half=0).wait_recv()
                copy(a, 8, (*far, cc), me, half=1).wait_recv()
            else:
                copy(a, 3, (*far, cc), me).wait_recv()
            pass_on(copy(a, 6, (*far, cc), sibling))
        for a in range(n):
            copy(a, 0, sibling, me).wait_recv()
            for j, chip in enumerate(near + [far]):
                copy(a, 4 + j, (*chip, 1 - cc), me).wait_recv()
        for cp in sent:
            cp.wait_send()
        for cp in mine:
            cp.wait()

    return _comm_call(
        body, lambda: _sibling_peer() + _chip_peers(),
        [jax.ShapeDtypeStruct((N_DEV,) + s.shape, s.dtype) for s in shards],
        [pltpu.SemaphoreType.DMA((n, 9)), pltpu.SemaphoreType.DMA((n, 9)), pltpu.SemaphoreType.DMA((n,))],
        name, shards, collective_id)


def _send_to_sibling(parts, name, collective_id=None):
    n = len(parts)

    def body(*refs):
        g_refs, out_refs = refs[:n], refs[n:2 * n]
        send_sems, recv_sems = refs[2 * n:]
        x, y, cc = _me()
        copies = [pltpu.make_async_remote_copy(
            src_ref=g_refs[a].at[1 - cc], dst_ref=out_refs[a], send_sem=send_sems.at[a], recv_sem=recv_sems.at[a],
            device_id=(x, y, 1 - cc), device_id_type=MESH) for a in range(n)]
        for cp in copies:
            cp.start()
        for cp in copies:
            cp.wait()

    return _comm_call(
        body, _sibling_peer, [jax.ShapeDtypeStruct(p.shape[1:], p.dtype) for p in parts],
        [pltpu.SemaphoreType.DMA((n,)), pltpu.SemaphoreType.DMA((n,))], name, parts, collective_id)


def _send_to_chips(parts, name, collective_id=None):
    n = len(parts)

    def body(*refs):
        a_refs, out_refs = refs[:n], refs[n:2 * n]
        send_sems, recv_sems = refs[2 * n:]
        x, y, cc = _me()
        copies = []
        for k, (fx, fy) in enumerate(_CHIP_FLIPS):
            px, py = x ^ fx, y ^ fy
            for a in range(n):
                copies.append(pltpu.make_async_remote_copy(
                    src_ref=a_refs[a].at[2 * px + py], dst_ref=out_refs[a].at[k], send_sem=send_sems.at[a, k],
                    recv_sem=recv_sems.at[a, k], device_id=(px, py, cc), device_id_type=MESH))
                copies[-1].start()
        for cp in copies:
            cp.wait()

    return _comm_call(
        body, _chip_peers, [jax.ShapeDtypeStruct((3,) + p.shape[1:], p.dtype) for p in parts],
        [pltpu.SemaphoreType.DMA((n, 3)), pltpu.SemaphoreType.DMA((n, 3))], name, parts, collective_id)


def _pack_rows(w_oa, w_ob, w_o, w_down, w_fg2, conv_w, l):
    conv_bits = lax.bitcast_convert_type(conv_w[l].reshape(-1), BF16).reshape(1, -1)
    tail = jnp.concatenate([w_fg2[l].astype(BF16).reshape(1, D_MODEL),
                            jnp.pad(conv_bits, ((0, 0), (0, D_MODEL - conv_bits.shape[1])))], axis=0)
    tail = jnp.pad(tail, ((0, B_ROWS - B_FG2 - tail.shape[0]), (0, 0)))
    rows = jnp.concatenate([w_oa[l].astype(BF16), w_ob[l].astype(BF16), w_o[l].astype(BF16),
                            w_down[l].astype(BF16)], axis=0)
    return rows, tail


def _unpack_tail(tail):
    w_fg2 = tail[:, 0, :].reshape(N_DEV, RANK, KEY_W // N_DEV).transpose(1, 0, 2).reshape(RANK, KEY_W)
    conv_bits = tail[:, B_CONV - B_FG2, :2 * 3 * ROW_SHARD].reshape(N_DEV, 3 * ROW_SHARD, 2)
    conv_w = lax.bitcast_convert_type(conv_bits, F32).reshape(N_DEV, 3, ROW_SHARD)
    return jnp.pad(w_fg2, ((0, LANES - RANK), (0, 0))), conv_w.transpose(1, 0, 2).reshape(3, D_MODEL)


def _by_core_chip(t):
    return t.reshape((2, 2, 2) + t.shape[1:]).transpose((2, 0, 1) + tuple(range(3, t.ndim + 2))).reshape(
        (2, 4) + t.shape[1:])


def _grad_rows(g):
    fg2 = g["w_fg2"][:RANK].reshape(RANK, N_DEV, KEY_W // N_DEV).transpose(1, 0, 2).reshape(N_DEV, 1, D_MODEL)
    conv = g["conv_w"].astype(BF16).reshape(3, N_DEV, ROW_SHARD).transpose(1, 0, 2).reshape(N_DEV, 1, 3 * ROW_SHARD)
    tail = jnp.concatenate([fg2, jnp.pad(conv, ((0, 0), (0, 0), (0, D_MODEL - 3 * ROW_SHARD)))], axis=1)
    tail = jnp.pad(tail, ((0, 0), (0, B_ROWS - B_FG2 - 2), (0, 0)))
    parts = [g["w_oa"].reshape(N_DEV, ROW_SHARD, D_MODEL), g["w_ob"].reshape(N_DEV, ROW_SHARD, D_MODEL),
             g["w_o"].reshape(N_DEV, ROW_SHARD, D_MODEL), g["w_down"].reshape(N_DEV, FFN_SHARD, D_MODEL), tail]
    return _by_core_chip(jnp.concatenate(parts, axis=1))


def _ungrad_rows(gs):
    return dict(w_oa=gs[B_OA:B_OA + ROW_SHARD], w_ob=gs[B_OB:B_OB + ROW_SHARD], w_o=gs[B_O:B_O + ROW_SHARD],
                w_ffn_down=gs[B_DOWN:B_DOWN + FFN_SHARD], w_fg2=gs[B_FG2].reshape(RANK, KEY_W // N_DEV),
                conv_w=gs[B_CONV, :3 * ROW_SHARD].reshape(3, ROW_SHARD))


def _layer_fwd(x, h, p, l, next_norm_g=None):
    tag = f"l{l}_"
    if h is None:
        h = _rmsnorm_fwd(x, p["norm1_g"], tag + "norm1")
    proj = _matmul(h, p["w_in"], "nn", BF16, tag + "proj")
    o, states, oa = _gla_fwd(proj, p["w_fg2"], p["b_fg"], p["gla_norm_g"], tag + "gla_fwd")
    p.update(p.pop("rest")((o,)))
    ya = _matmul(oa, p["w_oa"], "nn", BF16, tag + "ya")
    cb, yb = _conv_project(proj, p["conv_w"], p["conv_b"], p["w_ob"], tag + "yb")
    mix, x1, h2 = _mix_project(proj, ya, yb, p["w_o"], x, p["norm2_g"], tag + "x1")
    gu = _matmul(h2, p["w_gu"], "nn", BF16, tag + "gu")
    hid, x2, h_next = _swiglu_project(gu, p["w_down"], x1, next_norm_g, tag + "x2")
    saved = dict(x=x, h=h, proj=proj, o=o, states=states, oa=oa, ya=ya, cb=cb, yb=yb, mix=mix, x1=x1, h2=h2,
                 gu=gu, hid=hid)
    return x2, h_next, saved


def _layer_bwd(dx2, dx2h, p, sv, l, reduce=None):
    if reduce is None:
        reduce = lambda group, grads: ((), ())
    tag = f"l{l}_b_"
    dhid = _matmul(dx2h, p["w_down"], "nt", BF16, tag + "dhid")
    d_down = _matmul(sv["hid"], dx2h, "tn", BF16, tag + "dw_down")
    dgu = _swiglu_bwd(dhid, sv["gu"], tag + "swiglu")
    dx1, dx1h, dg2 = _matmul_norm_bwd(dgu, p["w_gu"], sv["x1"], p["norm2_g"], dx2, tag + "dh2")
    d_gu = _matmul(sv["h2"], dgu, "tn", BF16, tag + "dw_gu")
    gu_packed, gu_sums = reduce("gu", d_gu)
    dmix = _matmul(dx1h, p["w_o"], "nt", BF16, tag + "dmix", after=gu_packed)
    d_o = _matmul(sv["mix"], dx1h, "tn", BF16, tag + "dw_o")
    dproj, dya, dyb = _mix_bwd(dmix, sv["proj"], sv["ya"], sv["yb"], tag + "mix")
    dcb = _matmul(dyb, p["w_ob"], "nt", BF16, tag + "dcb", after=gu_sums)
    d_ob = _matmul(sv["cb"], dyb, "tn", BF16, tag + "dw_ob")
    dproj, dwb = _conv_bwd(dcb, sv["proj"], p["conv_w"], p["conv_b"], dproj, tag + "conv")
    doa = _matmul(dya, p["w_oa"], "nt", BF16, tag + "doa")
    d_oa = _matmul(sv["oa"], dya, "tn", BF16, tag + "dw_oa")
    dq, dk, dv, dfg, dbfg, dproj, dgg = _gla_bwd(sv["proj"], p["w_fg2"], p["b_fg"], sv["states"], doa, sv["o"],
                                                 p["gla_norm_g"], dproj, tag + "gla")
    dproj = _place_qkv(dq, dk, dv, dproj, tag + "place_qkv")
    dproj = _matmul(dfg, p["w_fg2"], "nt", BF16, tag + "dfz", into=(dproj, FZ_COL))
    d_fg2 = _matmul(sv["proj"], dfg, "tn", BF16, tag + "dw_fg2", a_cols=(FZ_COL, LANES))
    rows = dict(w_fg2=d_fg2, conv_w=dwb[0:3], w_oa=d_oa, w_ob=d_ob, w_o=d_o, w_down=d_down)
    rows_packed, rows_sums = reduce("rows", rows)
    d_in = _matmul(sv["h"], dproj, "tn", BF16, tag + "dw_in", after=rows_packed)
    _, in_sums = reduce("in", d_in)
    dx, dxh, dg1 = _matmul_norm_bwd(dproj, p["w_in"], sv["x"], p["norm1_g"], dx1, tag + "dh",
                                    after=(d_in,) + tuple(rows_sums) + tuple(in_sums))
    big = dict(w_in=d_in, w_gu=d_gu, **rows)
    pad = lambda t: jnp.pad(t, ((0, 0), (0, D_MODEL - t.shape[1])))
    small = [dg1[0:1], pad(dbfg[0:1]), pad(dgg[0:1]), dwb[3:4], dg2[0:1]]
    return dx, dxh, big, small


def _local_step(x, target, weights_of, final_g, reduce_of=None):
    saved, layers, h = [], [], None
    for l in range(DEPTH):
        layers.append(weights_of(l, x))
        x, h, sv = _layer_fwd(x, h, layers[l], l, layers[l].get("next_norm1_g"))
        saved.append(sv)
    loss, dx, dxh, dgf = _loss_head(x, final_g, target, "loss_head")
    bigs, smalls = [None] * DEPTH, [None] * DEPTH
    for l in reversed(range(DEPTH)):
        dx, dxh, bigs[l], smalls[l] = _layer_bwd(dx, dxh, layers[l], saved[l], l, reduce_of(l) if reduce_of else None)
    loss_row = jnp.pad(loss[0:1], ((0, 0), (0, D_MODEL - loss.shape[1])))
    small = jnp.concatenate(smalls[0] + smalls[1] + [dgf[0:1], loss_row], axis=0)
    small = jnp.pad(small, ((0, SMALL_ROWS - small.shape[0]), (0, 0)))
    return loss[0, 0], dx, bigs, small


def kernel(x, norm1_g, w_in, w_fg2, b_fg, gla_norm_g, w_oa, conv_w, conv_b, w_ob, w_o, norm2_g, w_ffn_gate, w_ffn_up, w_ffn_down, final_g, loss_target, m_norm1_g, m_w_in, m_w_fg2, m_b_fg, m_gla_norm_g, m_w_oa, m_conv_w, m_conv_b, m_w_ob, m_w_o, m_norm2_g, m_w_ffn_gate, m_w_ffn_up, m_w_ffn_down, m_final_g, v_norm1_g, v_w_in, v_w_fg2, v_b_fg, v_gla_norm_g, v_w_oa, v_conv_w, v_conv_b, v_w_ob, v_w_o, v_norm2_g, v_w_ffn_gate, v_w_ffn_up, v_w_ffn_down, v_final_g):
    names = ["norm1_g", "w_in", "w_fg2", "b_fg", "gla_norm_g", "w_oa", "conv_w", "conv_b", "w_ob", "w_o",
             "norm2_g", "w_ffn_gate", "w_ffn_up", "w_ffn_down", "final_g"]
    w = dict(zip(names, [norm1_g, w_in, w_fg2, b_fg, gla_norm_g, w_oa, conv_w, conv_b, w_ob, w_o, norm2_g,
                         w_ffn_gate, w_ffn_up, w_ffn_down, final_g]))
    m = dict(zip(names, [m_norm1_g, m_w_in, m_w_fg2, m_b_fg, m_gla_norm_g, m_w_oa, m_conv_w, m_conv_b, m_w_ob,
                         m_w_o, m_norm2_g, m_w_ffn_gate, m_w_ffn_up, m_w_ffn_down, m_final_g]))
    v = dict(zip(names, [v_norm1_g, v_w_in, v_w_fg2, v_b_fg, v_gla_norm_g, v_w_oa, v_conv_w, v_conv_b, v_w_ob,
                         v_w_o, v_norm2_g, v_w_ffn_gate, v_w_ffn_up, v_w_ffn_down, v_final_g]))
    col_names = ["w_in", "w_ffn_gate", "w_ffn_up"]
    cx, cy, cc = _me()
    prm = jnp.asarray(SHIFT_TABLE)[4 * cx + 2 * cy + cc]
    core = jnp.reshape(cc, (1,)).astype(jnp.int32)
    chip = jnp.reshape(2 * cx + cy, (1,)).astype(jnp.int32)

    ids = iter(range(32))

    gathered, previous = [], None
    for l in range(DEPTH):
        rows, tail = _pack_rows(w_oa, w_ob, w_o, w_ffn_down, w_fg2, conv_w, l)
        win_in, win_gu = _window_cols(prm, w_in, w_ffn_gate, w_ffn_up, l, f"l{l}_windows", after=previous)
        previous = rows
        first = _all_gather([win_in, tail], f"l{l}_gather_in", next(ids))
        gathered.append(list(first) + list(_all_gather([win_gu, rows], f"l{l}_gather_rest", next(ids))))

    def weights_of(l, x_in):
        all_in, all_tail, all_gu, all_rows = gathered[l]
        after = (x_in,) if l > 0 else ()
        w_in_full, tail = _assemble_in(all_in, all_tail, f"l{l}_assemble_in", after)
        w_fg2_full, conv_w_full = _unpack_tail(tail)

        def rest(after_rest):
            names_rest = ("w_gu", "w_oa", "w_ob", "w_o", "w_down")
            return dict(zip(names_rest, _assemble_rest(all_gu, all_rows, f"l{l}_assemble_rest", after + after_rest)))

        return dict(w_in=w_in_full, rest=rest, w_fg2=w_fg2_full,
                    conv_w=conv_w_full, norm1_g=norm1_g[l][None], b_fg=b_fg[l][None],
                    gla_norm_g=gla_norm_g[l][None], conv_b=conv_b[l][None], norm2_g=norm2_g[l][None],
                    next_norm1_g=norm1_g[l + 1][None] if l + 1 < DEPTH else None)

    pending = [dict() for _ in range(DEPTH)]
    landed = []

    def reduce_of(l):
        def reduce(group, grads):
            tag = f"l{l}_{group}"
            if group == "gu":
                packed = _grad_windows_gu(grads, tag + "_windows")
            elif group == "in":
                packed = _grad_windows_in(grads, tag + "_windows")
            else:
                packed = _grad_rows(grads)
            (from_sibling,) = _send_to_sibling([packed], tag + "_to_sibling", next(ids))
            waited = () if group == "in" else tuple(landed)
            if waited:
                landed.clear()
            sums = _pair_sum(packed, from_sibling, core, tag + "_pair_sum", after=waited)
            (from_chips,) = _send_to_chips([sums], tag + "_to_chips", next(ids))
            landed.append(from_chips)
            pending[l][group] = (sums, from_chips)
            return (packed,), (sums,)
        return reduce

    loss, dx, bigs, small = _local_step(x[0], loss_target[0], weights_of, final_g[None], reduce_of)

    grads, deltas, new_m, new_v = {}, {}, {}, {}
    for kind, group_names in (("gu", col_names[1:]), ("in", col_names[:1])):
        params = [t for n in group_names for t in (w[n], m[n], v[n])]
        out = None
        for l in reversed(range(DEPTH)):
            sums, from_chips = pending[l][kind]
            out = _final_windows(prm, chip, sums, from_chips, params, l, kind, f"l{l}_{kind}_final", out)
        for n, (g, d, nm, nv) in zip(group_names, out):
            grads[n], deltas[n], new_m[n], new_v[n] = g, d, nm, nv
    row_grads = [_ungrad_rows(_chip_sum(*pending[l]["rows"], chip, f"l{l}_rows_chip_sum")) for l in range(DEPTH)]
    for n in row_grads[0]:
        grads[n] = jnp.stack([row_grads[l][n] for l in range(DEPTH)])

    small_sum = _sum_devices(_all_gather([small], "gather_small")[0], "sum_small")
    r512, r256 = slice(0, KEY_W), slice(0, HEAD_V)
    grads.update(
        norm1_g=jnp.stack([small_sum[0], small_sum[5]]), b_fg=jnp.stack([small_sum[1, r512], small_sum[6, r512]]),
        gla_norm_g=jnp.stack([small_sum[2, r256], small_sum[7, r256]]),
        conv_b=jnp.stack([small_sum[3], small_sum[8]]), norm2_g=jnp.stack([small_sum[4], small_sum[9]]),
        final_g=small_sum[10])

    for n in names:
        if n not in col_names:
            deltas[n], new_m[n], new_v[n] = _adamw(w[n], grads[n], m[n], v[n], "adamw_" + n)

    total_loss = small_sum[2 * 5 + 1, 0]
    return (total_loss, dx[None], *[grads[n] for n in names], *[deltas[n] for n in names],
            *[new_m[n] for n in names], *[new_v[n] for n in names])
```
